```python
import math
import jax
import jax.numpy as jnp
from jax import lax
import numpy as np

D_MODEL = 1024
BATCH = 16
SEQ = 256
DEPTH = 4
DEC_BATCH = 8
DEC_SEQ = 1024
PAST_LEN = 256

GRID_W = 64
N_MIXERS = 4
N_DA = (DEPTH + 3) // 4
N_NA = (DEPTH + 2) // 4
N_GQ = (DEPTH + 1) // 4
N_HY = DEPTH // 4
Q_BLOCK = 128
D_FF = 4 * D_MODEL
DA_DH = 64
DA_HEADS = D_MODEL // (2 * DA_DH)
NA_DH = 64
NA_HEADS = D_MODEL // NA_DH
NA_WIN_ROWS = 8
NA_WIN_COLS = 16
GQ_DH = 64
GQ_HEADS = D_MODEL // GQ_DH
GQ_KV_HEADS = 4
GQ_GROUP = GQ_HEADS // GQ_KV_HEADS
HY_ORDER = 2
HY_SHORT = 3
HY_BANDS = 16
HY_EMB = 1 + 2 * HY_BANDS
HY_FFN = 64
HY_DECAY_MIN = 3.07
HY_DECAY_MAX = 15.35
ROPE_BASE = 10000.0
LN_EPS = 1e-5
RMS_EPS = 1e-6
DN_ALPHA = (2 * DEPTH) ** 0.25
DN_BETA = (8 * DEPTH) ** -0.25
NEG_INF = -1e30

kernel_name = "hybrid_diffusion_trunk_step"


def layer_norm(x, g, b):
    xf = x.astype(jnp.float32)
    mu = jnp.mean(xf, axis=-1, keepdims=True)
    var = jnp.mean(jnp.square(xf - mu), axis=-1, keepdims=True)
    return ((xf - mu) * lax.rsqrt(var + LN_EPS) * g + b).astype(x.dtype)


def rms_norm(x, g):
    xf = x.astype(jnp.float32)
    return (xf * lax.rsqrt(jnp.mean(xf * xf, axis=-1, keepdims=True) + RMS_EPS) * g).astype(x.dtype)


def softmax_f32(s):
    return jax.nn.softmax(s.astype(jnp.float32), axis=-1)


def modulate(x, shift, scale):
    return x * (1.0 + scale) + shift


def axial_rope(x):
    L, dh = x.shape[1], x.shape[-1]
    n = dh // 4
    pos = jnp.arange(L)
    inv = ROPE_BASE ** (-jnp.arange(n, dtype=jnp.float32) / n)
    shp = (L,) + (1,) * (x.ndim - 3) + (n,)
    ang_r = ((pos // GRID_W).astype(jnp.float32)[:, None] * inv).reshape(shp)
    ang_c = ((pos % GRID_W).astype(jnp.float32)[:, None] * inv).reshape(shp)
    cr, sr, cc, sc = jnp.cos(ang_r), jnp.sin(ang_r), jnp.cos(ang_c), jnp.sin(ang_c)
    xr1, xr2, xc1, xc2 = jnp.split(x, 4, axis=-1)
    out = jnp.concatenate([xr1 * cr - xr2 * sr, xr1 * sr + xr2 * cr,
                           xc1 * cc - xc2 * sc, xc1 * sc + xc2 * cc], axis=-1)
    return out.astype(x.dtype)


def sweep_query_blocks(fn, q):
    B, L = q.shape[:2]
    nb = L // Q_BLOCK
    qb = jnp.moveaxis(q.reshape((B, nb, Q_BLOCK) + q.shape[2:]), 1, 0)
    ob = lax.map(fn, qb)
    return jnp.moveaxis(ob, 0, 1).reshape((B, L) + ob.shape[3:])


def mha_block(qb, k, v, scale):
    p = softmax_f32(jnp.einsum("bqhd,bkhd->bhqk", qb, k) * scale).astype(v.dtype)
    return jnp.einsum("bhqk,bkhd->bqhd", p, v)


def diff_attention(h_ctx, h_lat, cache_k, cache_v, w_qkv, w_o, lam_p, subln_g, layer_idx):
    lam_init = 0.8 - 0.6 * math.exp(-0.3 * layer_idx)
    lam = (jnp.exp(jnp.sum(lam_p[0] * lam_p[1]).astype(jnp.float32))
           - jnp.exp(jnp.sum(lam_p[2] * lam_p[3]).astype(jnp.float32)) + lam_init)
    scale = DA_DH ** -0.5

    def project(h):
        B, L, _ = h.shape
        q, k, v = jnp.split(h @ w_qkv, 3, axis=-1)
        shp = (B, L, DA_HEADS, 2 * DA_DH)
        return q.reshape(shp), k.reshape(shp), v.reshape(shp)

    def attend(k, v):
        k1, k2 = jnp.split(k, 2, axis=-1)

        def block(qb):
            q1, q2 = jnp.split(qb, 2, axis=-1)
            p1 = softmax_f32(jnp.einsum("bqhd,bkhd->bhqk", q1, k1) * scale)
            p2 = softmax_f32(jnp.einsum("bqhd,bkhd->bhqk", q2, k2) * scale)
            return jnp.einsum("bhqk,bkhd->bqhd", (p1 - lam * p2).astype(v.dtype), v)
        return block

    def finish(o):
        B, L = o.shape[:2]
        o = rms_norm(o, subln_g) * (1.0 - lam_init)
        return o.reshape(B, L, DA_HEADS * 2 * DA_DH) @ w_o

    def rope_pair(x):
        B, L = x.shape[:2]
        return axial_rope(x.reshape(B, L, DA_HEADS, 2, DA_DH)).reshape(x.shape)

    qc, kc, vc = project(h_ctx)
    out_ctx = finish(sweep_query_blocks(attend(kc, vc), qc))
    ql, kl, vl = project(h_lat)
    ql, kl = rope_pair(ql), rope_pair(kl)
    k_all = jnp.concatenate([kl, cache_k], axis=1)
    v_all = jnp.concatenate([vl, cache_v], axis=1)
    out_lat = finish(sweep_query_blocks(attend(k_all, v_all), ql))
    return out_ctx, out_lat, kc, vc


def neighbourhood_attention(h_ctx, h_lat, cache_k, cache_v, w_qkv, w_o, rel_bias):
    scale = NA_DH ** -0.5

    def project(h):
        B, L, _ = h.shape
        q, k, v = jnp.split(h @ w_qkv, 3, axis=-1)
        shp = (B, L, NA_HEADS, NA_DH)
        return q.reshape(shp), k.reshape(shp), v.reshape(shp)

    qc, kc, vc = project(h_ctx)
    Bc, Lc = h_ctx.shape[:2]
    oc = sweep_query_blocks(lambda qb: mha_block(qb, kc, vc, scale), qc)
    out_ctx = oc.reshape(Bc, Lc, NA_HEADS * NA_DH) @ w_o

    ql, kl, vl = project(h_lat)
    B, L = h_lat.shape[:2]
    rows = L // GRID_W
    kr = min(NA_WIN_ROWS, rows)
    grid = (B, rows, GRID_W, NA_HEADS, NA_DH)
    qg, kg, vg = ql.reshape(grid), kl.reshape(grid), vl.reshape(grid)
    row_start = jnp.clip(jnp.arange(rows) - kr // 2, 0, rows - kr)
    cols = jnp.arange(GRID_W)
    col_start = jnp.clip(cols - NA_WIN_COLS // 2, 0, GRID_W - NA_WIN_COLS)
    col_in = ((cols[None, :] >= col_start[:, None])
              & (cols[None, :] < col_start[:, None] + NA_WIN_COLS))
    rel_c = jnp.clip(cols[None, :] - cols[:, None], -(NA_WIN_COLS - 1), NA_WIN_COLS - 1) + NA_WIN_COLS - 1

    def row_block(r):
        rs = row_start[r]
        q_r = lax.dynamic_index_in_dim(qg, r, axis=1, keepdims=False)
        k_r = lax.dynamic_slice_in_dim(kg, rs, kr, axis=1)
        v_r = lax.dynamic_slice_in_dim(vg, rs, kr, axis=1)
        rel_r = rs + jnp.arange(kr) - r + NA_WIN_ROWS - 1
        bias = rel_bias[:, rel_r[None, :, None], rel_c[:, None, :]]
        s_loc = jnp.einsum("bqhd,bikhd->bhqik", q_r, k_r).astype(jnp.float32) * scale + bias
        s_loc = jnp.where(col_in[:, None, :], s_loc, NEG_INF).reshape(B, NA_HEADS, GRID_W, kr * GRID_W)
        s_ctx = jnp.einsum("bqhd,bshd->bhqs", q_r, cache_k).astype(jnp.float32) * scale
        p = jax.nn.softmax(jnp.concatenate([s_loc, s_ctx], axis=-1), axis=-1).astype(v_r.dtype)
        p_loc = p[..., :kr * GRID_W].reshape(B, NA_HEADS, GRID_W, kr, GRID_W)
        p_ctx = p[..., kr * GRID_W:]
        return (jnp.einsum("bhqik,bikhd->bqhd", p_loc, v_r)
                + jnp.einsum("bhqs,bshd->bqhd", p_ctx, cache_v))

    ol = lax.map(row_block, jnp.arange(rows))
    out_lat = jnp.moveaxis(ol, 0, 1).reshape(B, L, NA_HEADS * NA_DH) @ w_o
    return out_ctx, out_lat, kc, vc


def gq_attention(h_ctx, h_lat, cache_k, cache_v, w_qkv, w_o, q_norm, k_norm):
    scale = GQ_DH ** -0.5
    nq, nk = GQ_HEADS * GQ_DH, GQ_KV_HEADS * GQ_DH

    def project(h):
        B, L, _ = h.shape
        u = h @ w_qkv
        q = rms_norm(u[..., :nq].reshape(B, L, GQ_HEADS, GQ_DH), q_norm)
        k = rms_norm(u[..., nq:nq + nk].reshape(B, L, GQ_KV_HEADS, GQ_DH), k_norm)
        v = u[..., nq + nk:].reshape(B, L, GQ_KV_HEADS, GQ_DH)
        return q, k, v

    def attend(k, v):
        def block(qb):
            B, Q = qb.shape[:2]
            qg = qb.reshape(B, Q, GQ_KV_HEADS, GQ_GROUP, GQ_DH)
            p = softmax_f32(jnp.einsum("bqkgd,bskd->bkgqs", qg, k) * scale).astype(v.dtype)
            return jnp.einsum("bkgqs,bskd->bqkgd", p, v).reshape(B, Q, nq)
        return block

    qc, kc, vc = project(h_ctx)
    out_ctx = sweep_query_blocks(attend(kc, vc), qc) @ w_o
    ql, kl, vl = project(h_lat)
    ql, kl = axial_rope(ql), axial_rope(kl)
    k_all = jnp.concatenate([kl, cache_k], axis=1)
    v_all = jnp.concatenate([vl, cache_v], axis=1)
    out_lat = sweep_query_blocks(attend(k_all, v_all), ql) @ w_o
    return out_ctx, out_lat, kc, vc


def hyena_filters_fft(L, w1, b1, w2, b2, freq, w3, log_decay):
    t = jnp.arange(L, dtype=jnp.float32) / L
    ang = 2.0 * math.pi * t[:, None] * jnp.arange(1, HY_BANDS + 1, dtype=jnp.float32)
    emb = jnp.concatenate([t[:, None], jnp.cos(ang), jnp.sin(ang)], axis=-1)
    hid = jnp.sin(freq * (emb @ w1 + b1))
    hid = jnp.sin(freq * (hid @ w2 + b2))
    window = jnp.exp(-jnp.exp(log_decay.astype(jnp.float32)) * t[:, None])
    filt = ((hid @ w3).astype(jnp.float32) * window).reshape(L, HY_ORDER, 2, D_MODEL)
    fwd, bwd = filt[:, :, 0], filt[:, :, 1]
    two_sided = jnp.concatenate(
        [fwd, jnp.zeros((1, HY_ORDER, D_MODEL), jnp.float32), bwd[:0:-1]], axis=0)
    return jnp.fft.rfft(two_sided, axis=0)


def long_conv(z, freq_resp):
    L = z.shape[1]
    zf = jnp.fft.rfft(z.astype(jnp.float32), n=2 * L, axis=1)
    return jnp.fft.irfft(zf * freq_resp[None], n=2 * L, axis=1)[:, :L].astype(z.dtype)


def centred_short_conv(u, w, b):
    L = u.shape[1]
    up = jnp.pad(u, ((0, 0), (HY_SHORT // 2, HY_SHORT // 2), (0, 0)))
    return up[:, :L] * w[0] + up[:, 1:L + 1] * w[1] + up[:, 2:L + 2] * w[2] + b


def hyena(h, w_in, short_w, short_b, w1, b1, w2, b2, freq, w3, log_decay, filter_bias, w_o):
    L = h.shape[1]
    freq_resp = hyena_filters_fft(L, w1, b1, w2, b2, freq, w3, log_decay)
    parts = jnp.split(centred_short_conv(h @ w_in, short_w, short_b), HY_ORDER + 1, axis=-1)
    z = parts[0]
    for o in range(HY_ORDER):
        z = parts[o + 1] * (long_conv(z, freq_resp[:, o]) + z * filter_bias[o])
    return z @ w_o


def squared_relu_mlp(h, w1, w2):
    return jnp.square(jax.nn.relu(h @ w1)) @ w2


def setup_inputs(seed: int = 0) -> dict:
    key = jax.random.key(seed)
    ks = iter(jax.random.split(key, 64))
    d = D_MODEL

    def nrm(shape, scale):
        return jax.random.normal(next(ks), shape, jnp.float32) * scale

    da_w = DA_HEADS * 2 * DA_DH
    na_w = NA_HEADS * NA_DH
    gq_qkv = (GQ_HEADS + 2 * GQ_KV_HEADS) * GQ_DH
    n_filt = 2 * HY_ORDER * d
    return {
        "x_prompt": nrm((BATCH, SEQ, d), 1.0),
        "x_sample": nrm((DEC_BATCH, DEC_SEQ, d), 1.0),
        "c": nrm((DEC_BATCH, d), 1.0),
        "cache_da_k": nrm((DEC_BATCH, N_DA, PAST_LEN, DA_HEADS, 2 * DA_DH), 1.0),
        "cache_da_v": nrm((DEC_BATCH, N_DA, PAST_LEN, DA_HEADS, 2 * DA_DH), 1.0),
        "cache_na_k": nrm((DEC_BATCH, N_NA, PAST_LEN, NA_HEADS, NA_DH), 1.0),
        "cache_na_v": nrm((DEC_BATCH, N_NA, PAST_LEN, NA_HEADS, NA_DH), 1.0),
        "cache_gq_k": nrm((DEC_BATCH, N_GQ, PAST_LEN, GQ_KV_HEADS, GQ_DH), 1.0),
        "cache_gq_v": nrm((DEC_BATCH, N_GQ, PAST_LEN, GQ_KV_HEADS, GQ_DH), 1.0),
        "c_ctx": nrm((d,), 1.0),
        "ada_w": nrm((DEPTH, d, 6 * d), 0.02),
        "ada_b": nrm((DEPTH, 6 * d), 0.02),
        "ln_g": 1.0 + nrm((DEPTH, 2, d), 0.02),
        "ln_b": nrm((DEPTH, 2, d), 0.02),
        "mlp_w1": nrm((DEPTH, d, D_FF), d ** -0.5),
        "mlp_w2": nrm((DEPTH, D_FF, d), DN_BETA * D_FF ** -0.5),
        "da_w_qkv": nrm((N_DA, d, 3 * da_w), d ** -0.5),
        "da_w_o": nrm((N_DA, da_w, d), DN_BETA * da_w ** -0.5),
        "da_lambda": nrm((N_DA, 4, DA_DH), 0.1),
        "da_subln_g": 1.0 + nrm((N_DA, 2 * DA_DH), 0.02),
        "na_w_qkv": nrm((N_NA, d, 3 * na_w), d ** -0.5),
        "na_w_o": nrm((N_NA, na_w, d), DN_BETA * na_w ** -0.5),
        "na_rel_bias": nrm((N_NA, NA_HEADS, 2 * NA_WIN_ROWS - 1, 2 * NA_WIN_COLS - 1), 0.1),
        "gq_w_qkv": nrm((N_GQ, d, gq_qkv), d ** -0.5),
        "gq_w_o": nrm((N_GQ, GQ_HEADS * GQ_DH, d), DN_BETA * (GQ_HEADS * GQ_DH) ** -0.5),
        "gq_q_norm": 1.0 + nrm((N_GQ, GQ_DH), 0.02),
        "gq_k_norm": 1.0 + nrm((N_GQ, GQ_DH), 0.02),
        "hy_w_in": nrm((N_HY, d, (HY_ORDER + 1) * d), d ** -0.5),
        "hy_short_w": nrm((N_HY, HY_SHORT, (HY_ORDER + 1) * d), 0.5),
        "hy_short_b": nrm((N_HY, (HY_ORDER + 1) * d), 0.02),
        "hy_ffn_w1": nrm((N_HY, HY_EMB, HY_FFN), HY_EMB ** -0.5),
        "hy_ffn_b1": nrm((N_HY, HY_FFN), 0.02),
        "hy_ffn_w2": nrm((N_HY, HY_FFN, HY_FFN), HY_FFN ** -0.5),
        "hy_ffn_b2": nrm((N_HY, HY_FFN), 0.02),
        "hy_ffn_freq": 1.0 + nrm((N_HY, HY_FFN), 0.02),
        "hy_ffn_w3": nrm((N_HY, HY_FFN, n_filt), 0.1 * HY_FFN ** -0.5),
        "hy_log_decay": jnp.log(jnp.linspace(HY_DECAY_MIN, HY_DECAY_MAX, n_filt))[None, :]
                        + nrm((N_HY, n_filt), 0.01),
        "hy_filter_bias": nrm((N_HY, HY_ORDER, d), 0.1),
        "hy_w_o": nrm((N_HY, d, d), DN_BETA * d ** -0.5),
    }


def reference(x_prompt, x_sample, c, cache_da_k, cache_da_v, cache_na_k, cache_na_v,
              cache_gq_k, cache_gq_v, c_ctx, ada_w, ada_b, ln_g, ln_b, mlp_w1, mlp_w2,
              da_w_qkv, da_w_o, da_lambda, da_subln_g, na_w_qkv, na_w_o, na_rel_bias,
              gq_w_qkv, gq_w_o, gq_q_norm, gq_k_norm, hy_w_in, hy_short_w, hy_short_b,
              hy_ffn_w1, hy_ffn_b1, hy_ffn_w2, hy_ffn_b2, hy_ffn_freq, hy_ffn_w3,
              hy_log_decay, hy_filter_bias, hy_w_o):
    xp, xs = x_prompt, x_sample
    silu_ctx = jax.nn.silu(c_ctx)[None, :]
    silu_c = jax.nn.silu(c)
    da_k, da_v, na_k, na_v, gq_k, gq_v = [], [], [], [], [], []
    for i in range(DEPTH):
        m, j = i % N_MIXERS, i // N_MIXERS
        mod_p = jnp.split((silu_ctx @ ada_w[i] + ada_b[i])[:, None, :], 6, axis=-1)
        mod_s = jnp.split((silu_c @ ada_w[i] + ada_b[i])[:, None, :], 6, axis=-1)
        hp = modulate(xp, mod_p[0], mod_p[1])
        hs = modulate(xs, mod_s[0], mod_s[1])
        if m == 0:
            op, os_, kc, vc = diff_attention(hp, hs, cache_da_k[:, j], cache_da_v[:, j],
                                             da_w_qkv[j], da_w_o[j], da_lambda[j], da_subln_g[j], i)
            da_k.append(kc)
            da_v.append(vc)
        elif m == 1:
            op, os_, kc, vc = neighbourhood_attention(hp, hs, cache_na_k[:, j], cache_na_v[:, j],
                                                      na_w_qkv[j], na_w_o[j], na_rel_bias[j])
            na_k.append(kc)
            na_v.append(vc)
        elif m == 2:
            op, os_, kc, vc = gq_attention(hp, hs, cache_gq_k[:, j], cache_gq_v[:, j],
                                           gq_w_qkv[j], gq_w_o[j], gq_q_norm[j], gq_k_norm[j])
            gq_k.append(kc)
            gq_v.append(vc)
        else:
            hy_args = (hy_w_in[j], hy_short_w[j], hy_short_b[j], hy_ffn_w1[j], hy_ffn_b1[j],
                       hy_ffn_w2[j], hy_ffn_b2[j], hy_ffn_freq[j], hy_ffn_w3[j],
                       hy_log_decay[j], hy_filter_bias[j], hy_w_o[j])
            op = hyena(hp, *hy_args)
            os_ = hyena(hs, *hy_args)
        xp = layer_norm(DN_ALPHA * xp + mod_p[2] * op, ln_g[i, 0], ln_b[i, 0])
        xs = layer_norm(DN_ALPHA * xs + mod_s[2] * os_, ln_g[i, 0], ln_b[i, 0])
        fp = squared_relu_mlp(modulate(xp, mod_p[3], mod_p[4]), mlp_w1[i], mlp_w2[i])
        fs = squared_relu_mlp(modulate(xs, mod_s[3], mod_s[4]), mlp_w1[i], mlp_w2[i])
        xp = layer_norm(DN_ALPHA * xp + mod_p[5] * fp, ln_g[i, 1], ln_b[i, 1])
        xs = layer_norm(DN_ALPHA * xs + mod_s[5] * fs, ln_g[i, 1], ln_b[i, 1])
    state_da_k = jnp.stack(da_k, axis=1)
    state_da_v = jnp.stack(da_v, axis=1)
    state_na_k = jnp.stack(na_k, axis=1)
    state_na_v = jnp.stack(na_v, axis=1)
    state_gq_k = jnp.stack(gq_k, axis=1)
    state_gq_v = jnp.stack(gq_v, axis=1)
    return (xp, xs, state_da_k, state_da_v, state_na_k, state_na_v, state_gq_k, state_gq_v)
```

```python
import functools
import math

import numpy as np
import jax
import jax.numpy as jnp
from jax import lax
from jax.experimental import pallas as pl
from jax.experimental.pallas import tpu as pltpu

F32 = jnp.float32
BF = jnp.bfloat16

D = 1024
BATCH = 16
SEQ = 256
DEC_BATCH = 8
DEC_SEQ = 1024
PAST = 256
DEPTH = 4
GRID_W = 64
GRID_ROWS = DEC_SEQ // GRID_W
D_FF = 4 * D
T_CTX = BATCH * SEQ
T_LAT = DEC_BATCH * DEC_SEQ
T = T_CTX + T_LAT
HEAD_DIM = 64
ATT_SCALE = HEAD_DIM ** -0.5
DA_HEADS = 8
NA_HEADS = 16
NA_WIN_ROWS = 8
NA_WIN_COLS = 16
GQ_HEADS = 16
GQ_KV_HEADS = 4
HY_ORDER = 2
HY_BANDS = 16
HY_EMB = 1 + 2 * HY_BANDS
HY_EMB_PAD = 40
HY_FFN = 64
ROPE_BASE = 10000.0
LN_EPS = 1e-5
RMS_EPS = 1e-6
DN_ALPHA = (2 * DEPTH) ** 0.25
NEG_INF = -1e30

LANES = 128
TM = 512
N_CTX_TILES = T_CTX // TM
TQ = 256
MOD_ROWS = 16
VMEM_LIMIT = 56 * 1024 * 1024


def _cparams(n_axes):
    return pltpu.CompilerParams(dimension_semantics=("arbitrary",) * n_axes,
                                vmem_limit_bytes=VMEM_LIMIT)


def _dot(a, b):
    return jnp.dot(a, b, preferred_element_type=F32)


def _dot_nt(a, b):
    return lax.dot_general(a, b, (((1,), (1,)), ((), ())), preferred_element_type=F32)


def _const_spec(shape):
    nd = len(shape)
    return pl.BlockSpec(shape, lambda *_: (0,) * nd, pipeline_mode=pl.Buffered(1))


def _mod_row(i):
    return jnp.where(i < N_CTX_TILES, 0, 1 + (i - N_CTX_TILES) // (DEC_SEQ // TM))


def _mod_spec(layer):
    return pl.BlockSpec((None, None, 6, D), lambda i: (layer, _mod_row(i), 0, 0))


def _tok_spec(width):
    return pl.BlockSpec((TM, width), lambda i: (i, 0))


def _layer_norm(r, g, b):
    mu = jnp.mean(r, axis=-1, keepdims=True)
    c = r - mu
    var = jnp.mean(c * c, axis=-1, keepdims=True)
    return c * lax.rsqrt(var + LN_EPS) * g + b


def _mods_kernel(c_ref, w_ref, b_ref, o_ref):
    c = c_ref[...]
    s = (c / (1.0 + jnp.exp(-c))).astype(BF)
    o_ref[...] = _dot(s, w_ref[...].astype(BF)) + b_ref[...]


def _mods(cvec, ada_w, ada_b):
    tn = 1536
    out = pl.pallas_call(
        _mods_kernel,
        grid=(DEPTH, 6 * D // tn),
        in_specs=[pl.BlockSpec((MOD_ROWS, D), lambda l, n: (0, 0)),
                  pl.BlockSpec((None, D, tn), lambda l, n: (l, 0, n)),
                  pl.BlockSpec((None, 1, tn), lambda l, n: (l, 0, n))],
        out_specs=pl.BlockSpec((None, MOD_ROWS, tn), lambda l, n: (l, 0, n)),
        out_shape=jax.ShapeDtypeStruct((DEPTH, MOD_ROWS, 6 * D), F32),
        compiler_params=_cparams(2),
        name="adaln_mods",
    )(cvec, ada_w, ada_b.reshape(DEPTH, 1, 6 * D))
    return out.reshape(DEPTH, MOD_ROWS, 6, D)


def _modulated(x_ref, mod_ref):
    return (x_ref[...] * (1.0 + mod_ref[1:2, :]) + mod_ref[0:1, :]).astype(BF)


def _rope(x, a, b):
    n = x.shape[1]
    lane = lax.broadcasted_iota(jnp.int32, x.shape, 1)
    partner = jnp.where((lane & 16) == 0, pltpu.roll(x, n - 16, 1), pltpu.roll(x, 16, 1))
    return x * a + partner * b


def _plain_proj_kernel(x_ref, mod_ref, w_ref, *out_refs):
    h = _modulated(x_ref, mod_ref)
    off = 0
    for o_ref in out_refs:
        n = o_ref.shape[1]
        o_ref[...] = _dot(h, w_ref[:, off:off + n])
        off += n


def _plain_proj(x, mods, layer, w, widths):
    n = sum(widths)
    return pl.pallas_call(
        _plain_proj_kernel,
        grid=(T // TM,),
        in_specs=[_tok_spec(D), _mod_spec(layer), _const_spec((D, n))],
        out_specs=[_tok_spec(wd) for wd in widths],
        out_shape=[jax.ShapeDtypeStruct((T, wd), F32) for wd in widths],
        compiler_params=_cparams(1),
        name=f"proj_l{layer}",
    )(x, mods, w)


def _rope_spec(width):
    per = DEC_SEQ // TM
    return pl.BlockSpec((TM, width), lambda i: (jnp.maximum(i - N_CTX_TILES, 0) % per, 0))


def _da_proj_kernel(x_ref, mod_ref, w_ref, ra_ref, rb_ref, q_ref, k_ref, v_ref):
    h = _modulated(x_ref, mod_ref)
    q = _dot(h, w_ref[:, 0:D])
    k = _dot(h, w_ref[:, D:2 * D])
    v_ref[...] = _dot(h, w_ref[:, 2 * D:3 * D])
    is_lat = pl.program_id(0) >= N_CTX_TILES

    @pl.when(jnp.logical_not(is_lat))
    def _():
        q_ref[...] = q
        k_ref[...] = k

    @pl.when(is_lat)
    def _():
        a, b = ra_ref[...], rb_ref[...]
        q_ref[...] = _rope(q, a, b)
        k_ref[...] = _rope(k, a, b)


def _da_proj(x, mods, layer, w, rope_a, rope_b):
    return pl.pallas_call(
        _da_proj_kernel,
        grid=(T // TM,),
        in_specs=[_tok_spec(D), _mod_spec(layer), _const_spec((D, 3 * D)),
                  _rope_spec(D), _rope_spec(D)],
        out_specs=[_tok_spec(D)] * 3,
        out_shape=[jax.ShapeDtypeStruct((T, D), F32)] * 3,
        compiler_params=_cparams(1),
        name=f"da_proj_l{layer}",
    )(x, mods, w, rope_a, rope_b)


def _head_rms(x, g_mat, gain):
    x2 = x * x
    hi = x2.astype(BF)
    lo = (x2 - hi.astype(F32)).astype(BF)
    ms = _dot(hi, g_mat) + _dot(lo, g_mat)
    return x * lax.rsqrt(ms + RMS_EPS) * gain


def _gq_proj_kernel(x_ref, mod_ref, w_ref, g_ref, qn_ref, kn_ref, ra_ref, rb_ref, q_ref, k_ref, v_ref):
    nq, nk = GQ_HEADS * HEAD_DIM, GQ_KV_HEADS * HEAD_DIM
    h = _modulated(x_ref, mod_ref)
    q = _head_rms(_dot(h, w_ref[:, 0:nq]), g_ref[...], qn_ref[...])
    k = _head_rms(_dot(h, w_ref[:, nq:nq + nk]), g_ref[0:nk, 0:nk], kn_ref[...])
    v_ref[...] = _dot(h, w_ref[:, nq + nk:nq + 2 * nk])
    is_lat = pl.program_id(0) >= N_CTX_TILES

    @pl.when(jnp.logical_not(is_lat))
    def _():
        q_ref[...] = q
        k_ref[...] = k

    @pl.when(is_lat)
    def _():
        a, b = ra_ref[...], rb_ref[...]
        q_ref[...] = _rope(q, a, b)
        k_ref[...] = _rope(k, a[:, 0:nk], b[:, 0:nk])


def _gq_proj(x, mods, layer, w, g_mat, qn, kn, rope_a, rope_b):
    nq, nk = GQ_HEADS * HEAD_DIM, GQ_KV_HEADS * HEAD_DIM
    return pl.pallas_call(
        _gq_proj_kernel,
        grid=(T // TM,),
        in_specs=[_tok_spec(D), _mod_spec(layer), _const_spec((D, nq + 2 * nk)),
                  _const_spec((nq, nq)), _const_spec((1, nq)), _const_spec((1, nk)),
                  _rope_spec(D), _rope_spec(D)],
        out_specs=[_tok_spec(nq), _tok_spec(nk), _tok_spec(nk)],
        out_shape=[jax.ShapeDtypeStruct((T, nq), F32), jax.ShapeDtypeStruct((T, nk), F32),
                   jax.ShapeDtypeStruct((T, nk), F32)],
        compiler_params=_cparams(1),
        name=f"gq_proj_l{layer}",
    )(x, mods, w, g_mat, qn, kn, rope_a, rope_b)


def _softmax_pv(qm, segs):
    scores = [_dot_nt(qm, k) for k, _ in segs]
    m = scores[0].max(axis=-1, keepdims=True)
    for s in scores[1:]:
        m = jnp.maximum(m, s.max(axis=-1, keepdims=True))
    den = None
    out = None
    for s, (_, v) in zip(scores, segs):
        e = jnp.exp(s - m)
        d = e.sum(axis=-1, keepdims=True)
        o = _dot(e.astype(BF), v)
        den = d if den is None else den + d
        out = o if out is None else out + o
    return out / den


def _lane_half(shape):
    return lax.broadcasted_iota(jnp.int32, shape, 1) // HEAD_DIM


def _da_attn_kernel(*refs, has_cache, lam_init):
    if has_cache:
        q_ref, k_ref, v_ref, ck_ref, cv_ref, lam_ref, g_ref, _, o_ref = refs
    else:
        q_ref, k_ref, v_ref, lam_ref, g_ref, o_ref = refs
    q = q_ref[...] * ATT_SCALE
    half = _lane_half(q.shape)
    segs = [(k_ref[...].astype(BF), v_ref[...].astype(BF))]
    if has_cache:
        segs.append((ck_ref[...].astype(BF), cv_ref[...].astype(BF)))
    o1 = _softmax_pv(jnp.where(half == 0, q, 0.0).astype(BF), segs)
    o2 = _softmax_pv(jnp.where(half == 1, q, 0.0).astype(BF), segs)
    lp = lam_ref[...]
    lam = (jnp.exp(jnp.sum(lp[0:1] * lp[1:2], axis=-1, keepdims=True))
           - jnp.exp(jnp.sum(lp[2:3] * lp[3:4], axis=-1, keepdims=True)) + lam_init)
    o = o1 - lam * o2
    ms = jnp.mean(o * o, axis=-1, keepdims=True)
    o_ref[...] = o * lax.rsqrt(ms + RMS_EPS) * g_ref[...] * (1.0 - lam_init)


def _da_attention(q, k, v, cache_k, cache_v, lam_p, subln_g, layer_idx):
    lam_init = 0.8 - 0.6 * math.exp(-0.3 * layer_idx)
    w = 2 * HEAD_DIM
    small = [pl.BlockSpec((4, HEAD_DIM), lambda *_: (0, 0)), pl.BlockSpec((1, w), lambda *_: (0, 0))]
    o_ctx = pl.pallas_call(
        functools.partial(_da_attn_kernel, has_cache=False, lam_init=lam_init),
        grid=(BATCH, DA_HEADS),
        in_specs=[pl.BlockSpec((SEQ, w), lambda b, h: (b, h))] * 3 + small,
        out_specs=pl.BlockSpec((SEQ, w), lambda b, h: (b, h)),
        out_shape=jax.ShapeDtypeStruct((T, D), F32),
        compiler_params=_cparams(2),
        name="da_attn_ctx",
    )(q, k, v, lam_p, subln_g)
    qt = DEC_SEQ // TQ
    q0, k0 = T_CTX // TQ, T_CTX // DEC_SEQ
    kv_spec = pl.BlockSpec((DEC_SEQ, w), lambda b, h, t: (k0 + b, h))
    c_spec = pl.BlockSpec((None, PAST, w), lambda b, h, t: (b, 0, h))
    q_spec = pl.BlockSpec((TQ, w), lambda b, h, t: (q0 + b * qt + t, h))
    return pl.pallas_call(
        functools.partial(_da_attn_kernel, has_cache=True, lam_init=lam_init),
        grid=(DEC_BATCH, DA_HEADS, qt),
        in_specs=[q_spec, kv_spec, kv_spec, c_spec, c_spec] + small + [pl.BlockSpec(memory_space=pl.ANY)],
        out_specs=q_spec,
        out_shape=jax.ShapeDtypeStruct((T, D), F32),
        input_output_aliases={7: 0},
        compiler_params=_cparams(3),
        name="da_attn_lat",
    )(q, k, v, cache_k, cache_v, lam_p, subln_g, o_ctx)


def _pair_ctx_kernel(q_ref, k_ref, v_ref, o_ref):
    q = q_ref[...] * ATT_SCALE
    half = _lane_half(q.shape)
    segs = [(k_ref[...].astype(BF), v_ref[...].astype(BF))]
    o0 = _softmax_pv(jnp.where(half == 0, q, 0.0).astype(BF), segs)
    o1 = _softmax_pv(jnp.where(half == 1, q, 0.0).astype(BF), segs)
    o_ref[...] = jnp.where(half == 0, o0, o1)


def _na_ctx_attention(q, k, v):
    spec = pl.BlockSpec((SEQ, LANES), lambda b, p: (b, p))
    return pl.pallas_call(
        _pair_ctx_kernel,
        grid=(BATCH, NA_HEADS // 2),
        in_specs=[spec] * 3,
        out_specs=spec,
        out_shape=jax.ShapeDtypeStruct((T, D), F32),
        compiler_params=_cparams(2),
        name="na_attn_ctx",
    )(q, k, v)


NA_TILES = ((0, (0, 2, 4, 6)), (4, (0, 2, 4, 6, 8, 10)), (8, (4, 6, 8, 10, 12, 14)), (12, (8, 10, 12, 14)))
NA_MAX_CHUNKS = 6
NA_BIAS_BLOCKS = 2 * NA_WIN_ROWS - 2


def _na_lat_kernel(q_ref, k_ref, v_ref, ck_ref, cv_ref, w_ref, m_ref, _, o_ref):
    kb, vb = k_ref[...].astype(BF), v_ref[...].astype(BF)
    ckb, cvb = ck_ref[...].astype(BF), cv_ref[...].astype(BF)
    rows = 4 * GRID_W
    half = _lane_half((rows, LANES))
    for i, (r0, chunks) in enumerate(NA_TILES):
        q = q_ref[i * rows:(i + 1) * rows, :] * ATT_SCALE
        key0, nkey = chunks[0] * GRID_W, len(chunks) * LANES
        mask = m_ref[i, :, 0:nkey]
        outs = []
        for a in (0, 1):
            qm = jnp.where(half == a, q, 0.0).astype(BF)
            bias = jnp.concatenate(
                [w_ref[a, (6 - kr + r0) * GRID_W:(6 - kr + r0) * GRID_W + rows, :] for kr in chunks], axis=1)
            s_loc = _dot_nt(qm, kb[key0:key0 + nkey, :]) + bias + mask
            s_ctx = _dot_nt(qm, ckb)
            m = jnp.maximum(s_loc.max(axis=-1, keepdims=True), s_ctx.max(axis=-1, keepdims=True))
            e_loc, e_ctx = jnp.exp(s_loc - m), jnp.exp(s_ctx - m)
            den = e_loc.sum(axis=-1, keepdims=True) + e_ctx.sum(axis=-1, keepdims=True)
            o = _dot(e_loc.astype(BF), vb[key0:key0 + nkey, :]) + _dot(e_ctx.astype(BF), cvb)
            outs.append(o / den)
        o_ref[i * rows:(i + 1) * rows, :] = jnp.where(half == 0, outs[0], outs[1])


def _na_lat_attention(q, k, v, cache_k, cache_v, bias_tab, mask_tab, o_ctx):
    k0 = T_CTX // DEC_SEQ
    tok = pl.BlockSpec((DEC_SEQ, LANES), lambda p, b: (k0 + b, p))
    c_spec = pl.BlockSpec((None, PAST, LANES), lambda p, b: (b, 0, p))
    return pl.pallas_call(
        _na_lat_kernel,
        grid=(NA_HEADS // 2, DEC_BATCH),
        in_specs=[tok, tok, tok, c_spec, c_spec,
                  pl.BlockSpec((None, 2, NA_BIAS_BLOCKS * GRID_W, LANES), lambda p, b: (p, 0, 0, 0)),
                  _const_spec(mask_tab.shape), pl.BlockSpec(memory_space=pl.ANY)],
        out_specs=tok,
        out_shape=jax.ShapeDtypeStruct((T, D), F32),
        input_output_aliases={7: 0},
        compiler_params=_cparams(2),
        name="na_attn_lat",
    )(q, k, v, cache_k, cache_v, bias_tab, mask_tab, o_ctx)


def _na_bias_kernel(t_ref, r_ref, n_ref, o_ref):
    t = t_ref[...]
    t1 = t.astype(BF)
    r1 = t - t1.astype(F32)
    t2 = r1.astype(BF)
    t3 = (r1 - t2.astype(F32)).astype(BF)
    r = r_ref[...]
    o_ref[...] = _dot(t1, r) + _dot(t2, r) + _dot(t3, r) + n_ref[...]


def _na_bias_table(rel_bias, onehot, neg):
    nrel = 2 * NA_WIN_COLS
    idx = 13 - np.arange(NA_BIAS_BLOCKS)[:, None] + np.arange(2)[None, :]
    t = jnp.pad(rel_bias[:, idx, :], ((0, 0), (0, 0), (0, 0), (0, 1)))
    t = t.reshape(NA_HEADS * NA_BIAS_BLOCKS, 2 * nrel)
    n = GRID_W * LANES
    tn = 2048
    out = pl.pallas_call(
        _na_bias_kernel,
        grid=(n // tn,),
        in_specs=[pl.BlockSpec(t.shape, lambda j: (0, 0)),
                  pl.BlockSpec((2 * nrel, tn), lambda j: (0, j)),
                  pl.BlockSpec((1, tn), lambda j: (0, j))],
        out_specs=pl.BlockSpec((t.shape[0], tn), lambda j: (0, j)),
        out_shape=jax.ShapeDtypeStruct((t.shape[0], n), F32),
        compiler_params=_cparams(1),
        name="na_bias_table",
    )(t, onehot, neg)
    return out.reshape(NA_HEADS // 2, 2, NA_BIAS_BLOCKS * GRID_W, LANES)


def _na_constants():
    nrel = 2 * NA_WIN_COLS
    qc = np.arange(GRID_W)[:, None]
    kc = np.arange(GRID_W)[None, :]
    rel = np.clip(kc - qc, -(NA_WIN_COLS - 1), NA_WIN_COLS - 1) + NA_WIN_COLS - 1
    cs = np.clip(qc - NA_WIN_COLS // 2, 0, GRID_W - NA_WIN_COLS)
    col_in = (kc >= cs) & (kc < cs + NA_WIN_COLS)
    onehot = np.zeros((2, nrel, GRID_W, 2, GRID_W), np.float32)
    for hf in range(2):
        onehot[hf, rel, qc, hf, kc] = 1.0
    neg = np.where(col_in, 0.0, NEG_INF).astype(np.float32)
    neg = np.broadcast_to(neg[:, None, :], (GRID_W, 2, GRID_W)).reshape(1, -1)
    rows = 4 * GRID_W
    mask = np.full((len(NA_TILES), rows, NA_MAX_CHUNKS * LANES), NEG_INF, np.float32)
    kr = min(NA_WIN_ROWS, GRID_ROWS)
    for i, (r0, chunks) in enumerate(NA_TILES):
        qr = r0 + np.arange(rows)[:, None] // GRID_W
        rs = np.clip(qr - kr // 2, 0, GRID_ROWS - kr)
        for c, krow0 in enumerate(chunks):
            krow = krow0 + np.arange(LANES)[None, :] // GRID_W
            mask[i, :, c * LANES:(c + 1) * LANES] = np.where((krow >= rs) & (krow < rs + kr), 0.0, NEG_INF)
    return (jnp.asarray(onehot.reshape(2 * nrel, GRID_W * LANES), BF), jnp.asarray(neg), jnp.asarray(mask))


def _gq_attn_kernel(*refs, has_cache):
    if has_cache:
        q_ref, k_ref, v_ref, ck_ref, cv_ref, _, o_ref = refs
    else:
        q_ref, k_ref, v_ref, o_ref = refs
    segs = [(k_ref[...].astype(BF), v_ref[...].astype(BF))]
    if has_cache:
        segs.append((ck_ref[...].astype(BF), cv_ref[...].astype(BF)))
    half = _lane_half((q_ref.shape[0], LANES))
    group = GQ_HEADS // GQ_KV_HEADS
    for pair in range(group):
        q = q_ref[:, pair * LANES:(pair + 1) * LANES] * ATT_SCALE
        kv_half = (2 * pair) // group
        outs = []
        for a in (0, 1):
            qm = jnp.where(half == a, q, 0.0)
            if a != kv_half:
                qm = pltpu.roll(qm, HEAD_DIM, 1)
            o = _softmax_pv(qm.astype(BF), segs)
            if a != kv_half:
                o = pltpu.roll(o, HEAD_DIM, 1)
            outs.append(o)
        o_ref[:, pair * LANES:(pair + 1) * LANES] = jnp.where(half == 0, outs[0], outs[1])


def _gq_attention(q, k, v, cache_k, cache_v):
    qw = LANES * (GQ_HEADS // GQ_KV_HEADS)
    npair = GQ_KV_HEADS // 2
    o_ctx = pl.pallas_call(
        functools.partial(_gq_attn_kernel, has_cache=False),
        grid=(BATCH, npair),
        in_specs=[pl.BlockSpec((SEQ, qw), lambda b, p: (b, p))] + [pl.BlockSpec((SEQ, LANES), lambda b, p: (b, p))] * 2,
        out_specs=pl.BlockSpec((SEQ, qw), lambda b, p: (b, p)),
        out_shape=jax.ShapeDtypeStruct((T, D), F32),
        compiler_params=_cparams(2),
        name="gq_attn_ctx",
    )(q, k, v)
    qt = DEC_SEQ // TQ
    q0, k0 = T_CTX // TQ, T_CTX // DEC_SEQ
    q_spec = pl.BlockSpec((TQ, qw), lambda b, p, t: (q0 + b * qt + t, p))
    kv_spec = pl.BlockSpec((DEC_SEQ, LANES), lambda b, p, t: (k0 + b, p))
    c_spec = pl.BlockSpec((None, PAST, LANES), lambda b, p, t: (b, 0, p))
    return pl.pallas_call(
        functools.partial(_gq_attn_kernel, has_cache=True),
        grid=(DEC_BATCH, npair, qt),
        in_specs=[q_spec, kv_spec, kv_spec, c_spec, c_spec, pl.BlockSpec(memory_space=pl.ANY)],
        out_specs=q_spec,
        out_shape=jax.ShapeDtypeStruct((T, D), F32),
        input_output_aliases={5: 0},
        compiler_params=_cparams(3),
        name="gq_attn_lat",
    )(q, k, v, cache_k, cache_v, o_ctx)


def _hy_filter_kernel(emb_ref, w1_ref, b1_ref, w2_ref, b2_ref, fr_ref, w3f_ref, w3b_ref, ldf_ref, ldb_ref,
                      c_ref, s_ref, hre_ref, him_ref, hny_ref):
    seq = emb_ref.shape[0]
    hp = lax.Precision.HIGHEST
    emb = emb_ref[...]
    fr = fr_ref[...]
    hid = jnp.sin(fr * (jnp.dot(emb, w1_ref[...], precision=hp, preferred_element_type=F32) + b1_ref[...]))
    hid = jnp.sin(fr * (jnp.dot(hid, w2_ref[...], precision=hp, preferred_element_type=F32) + b2_ref[...]))
    t = emb[:, 0:1]
    fwd = jnp.dot(hid, w3f_ref[...], precision=hp, preferred_element_type=F32) * jnp.exp(-jnp.exp(ldf_ref[...]) * t)
    bwd = jnp.dot(hid, w3b_ref[...], precision=hp, preferred_element_type=F32) * jnp.exp(-jnp.exp(ldb_ref[...]) * t)
    row = lax.broadcasted_iota(jnp.int32, fwd.shape, 0)
    bwd = jnp.where(row == 0, 0.0, bwd)
    even = fwd + bwd
    odd = bwd - fwd
    wk = jnp.where(row == 0, 0.5 / seq, 1.0 / seq)
    hre_ref[...] = _dot(c_ref[...].astype(BF), even.astype(BF)) * wk
    him_ref[...] = _dot(s_ref[...].astype(BF), odd.astype(BF)) * wk
    alt = jnp.where((row & 1) == 0, 1.0, -1.0)
    hny_ref[...] = jnp.sum(alt * even, axis=0, keepdims=True) * (0.5 / seq)


def _hy_filter(seq, emb, w1, b1, w2, b2, freq, w3, log_decay, cmat, smat):
    dc = 512
    nj = D // dc
    small = [_const_spec(a.shape) for a in (emb, w1, b1, w2, b2, freq)]
    return pl.pallas_call(
        _hy_filter_kernel,
        grid=(HY_ORDER, nj),
        in_specs=small + [pl.BlockSpec((HY_FFN, dc), lambda o, j: (0, (2 * o) * nj + j)),
                          pl.BlockSpec((HY_FFN, dc), lambda o, j: (0, (2 * o + 1) * nj + j)),
                          pl.BlockSpec((1, dc), lambda o, j: (0, (2 * o) * nj + j)),
                          pl.BlockSpec((1, dc), lambda o, j: (0, (2 * o + 1) * nj + j)),
                          _const_spec((seq, seq)), _const_spec((seq, seq))],
        out_specs=[pl.BlockSpec((None, seq, dc), lambda o, j: (o, 0, j)),
                   pl.BlockSpec((None, seq, dc), lambda o, j: (o, 0, j)),
                   pl.BlockSpec((None, 1, dc), lambda o, j: (o, 0, j))],
        out_shape=[jax.ShapeDtypeStruct((HY_ORDER, seq, D), F32), jax.ShapeDtypeStruct((HY_ORDER, seq, D), F32),
                   jax.ShapeDtypeStruct((HY_ORDER, 1, D), F32)],
        compiler_params=_cparams(2),
        name=f"hy_filter_{seq}",
    )(emb, w1, b1, w2, b2, freq, w3, w3, log_decay, log_decay, cmat, smat)


def _hy_conv_kernel(u0_ref, u1_ref, u2_ref, sw0_ref, sw1_ref, sw2_ref, sb0_ref, sb1_ref, sb2_ref,
                    fb_ref, hre_ref, him_ref, hny_ref, c_ref, s_ref, *rest):
    o_ref, cb_ref, sb_ref = rest[-3:]
    seq = u0_ref.shape[0]

    @pl.when((pl.program_id(0) == 0) & (pl.program_id(1) == 0))
    def _():
        cb_ref[...] = c_ref[...].astype(BF)
        sb_ref[...] = s_ref[...].astype(BF)

    row = lax.broadcasted_iota(jnp.int32, u0_ref.shape, 0)
    alt = jnp.where((row & 1) == 0, 1.0, -1.0)

    def short_conv(u_ref, w_ref, b_ref):
        u = u_ref[...]
        prev = jnp.where(row == 0, 0.0, pltpu.roll(u, 1, 0))
        nxt = jnp.where(row == seq - 1, 0.0, pltpu.roll(u, seq - 1, 0))
        return prev * w_ref[0:1, :] + u * w_ref[1:2, :] + nxt * w_ref[2:3, :] + b_ref[...]

    cm, sm = cb_ref[...], sb_ref[...]
    z = short_conv(u0_ref, sw0_ref, sb0_ref)
    gates = (short_conv(u1_ref, sw1_ref, sb1_ref), short_conv(u2_ref, sw2_ref, sb2_ref))
    for o in range(HY_ORDER):
        zb = z.astype(BF)
        zc, zs = _dot(cm, zb), _dot(sm, zb)
        hre, him = hre_ref[o], him_ref[o]
        p_re = zc * hre + zs * him
        p_im = zc * him - zs * hre
        nyq = jnp.sum(alt * z, axis=0, keepdims=True) * hny_ref[o]
        y = _dot(cm, p_re.astype(BF)) - _dot(sm, p_im.astype(BF)) + alt * nyq
        z = gates[o] * (y + z * fb_ref[o:o + 1, :])
    o_ref[...] = z


def _hy_conv(u, short_w, short_b, filter_bias, hre, him, hny, cmat, smat, seq, nbatch, row0, prev):
    dc = 256
    nj = D // dc
    r0 = row0 // seq

    def part(p):
        return pl.BlockSpec((seq, dc), lambda j, b: (r0 + b, p * nj + j))

    def vec(rows, p):
        return pl.BlockSpec((rows, dc), lambda j, b: (0, p * nj + j))

    in_specs = ([part(p) for p in range(3)] + [vec(3, p) for p in range(3)] + [vec(1, p) for p in range(3)]
                + [pl.BlockSpec((HY_ORDER, dc), lambda j, b: (0, j)),
                   pl.BlockSpec((HY_ORDER, seq, dc), lambda j, b: (0, 0, j)),
                   pl.BlockSpec((HY_ORDER, seq, dc), lambda j, b: (0, 0, j)),
                   pl.BlockSpec((HY_ORDER, 1, dc), lambda j, b: (0, 0, j)),
                   _const_spec((seq, seq)), _const_spec((seq, seq))])
    args = [u, u, u, short_w, short_w, short_w, short_b, short_b, short_b, filter_bias, hre, him, hny, cmat, smat]
    aliases = {}
    if prev is not None:
        in_specs.append(pl.BlockSpec(memory_space=pl.ANY))
        args.append(prev)
        aliases = {len(args) - 1: 0}
    return pl.pallas_call(
        _hy_conv_kernel,
        grid=(nj, nbatch),
        in_specs=in_specs,
        out_specs=pl.BlockSpec((seq, dc), lambda j, b: (r0 + b, j)),
        out_shape=jax.ShapeDtypeStruct((T, D), F32),
        input_output_aliases=aliases,
        scratch_shapes=[pltpu.VMEM((seq, seq), BF), pltpu.VMEM((seq, seq), BF)],
        compiler_params=_cparams(2),
        name=f"hy_conv_{seq}",
    )(*args)


def _dft_tables(seq):
    k = np.arange(seq, dtype=np.int64)
    ang = np.pi * ((k[:, None] * k[None, :]) % (2 * seq)) / seq
    return jnp.asarray(np.cos(ang), F32), jnp.asarray(np.sin(ang), F32)


def _hy_embedding(seq):
    t = np.arange(seq, dtype=np.float32) / np.float32(seq)
    ang = (2.0 * math.pi) * t[:, None] * np.arange(1, HY_BANDS + 1, dtype=np.float32)
    emb = np.concatenate([t[:, None], np.cos(ang), np.sin(ang)], axis=-1).astype(np.float32)
    return jnp.asarray(np.pad(emb, ((0, 0), (0, HY_EMB_PAD - HY_EMB))))


def _outproj_kernel(o_ref, x_ref, mod_ref, w_ref, g_ref, b_ref, y_ref):
    a = _dot(o_ref[...].astype(BF), w_ref[...])
    r = DN_ALPHA * x_ref[...] + mod_ref[2:3, :] * a
    y_ref[...] = _layer_norm(r, g_ref[...], b_ref[...])


def _outproj(o, x, mods, layer, w, g, b):
    return pl.pallas_call(
        _outproj_kernel,
        grid=(T // TM,),
        in_specs=[_tok_spec(D), _tok_spec(D), _mod_spec(layer), _const_spec((D, D)),
                  _const_spec((1, D)), _const_spec((1, D))],
        out_specs=_tok_spec(D),
        out_shape=jax.ShapeDtypeStruct((T, D), F32),
        compiler_params=_cparams(1),
        name=f"outproj_l{layer}",
    )(o, x, mods, w, g, b)


def _mlp_kernel(x_ref, mod_ref, w1_ref, w2_ref, g_ref, b_ref, y_ref):
    x = x_ref[...]
    h = (x * (1.0 + mod_ref[4:5, :]) + mod_ref[3:4, :]).astype(BF)
    acc = None
    fc = 1024
    for c in range(D_FF // fc):
        a = jnp.maximum(_dot(h, w1_ref[:, c * fc:(c + 1) * fc]), 0.0)
        part = _dot((a * a).astype(BF), w2_ref[c * fc:(c + 1) * fc, :])
        acc = part if acc is None else acc + part
    r = DN_ALPHA * x + mod_ref[5:6, :] * acc
    y_ref[...] = _layer_norm(r, g_ref[...], b_ref[...])


def _mlp(x, mods, layer, w1, w2, g, b):
    return pl.pallas_call(
        _mlp_kernel,
        grid=(T // TM,),
        in_specs=[_tok_spec(D), _mod_spec(layer), _const_spec((D, D_FF)), _const_spec((D_FF, D)),
                  _const_spec((1, D)), _const_spec((1, D))],
        out_specs=_tok_spec(D),
        out_shape=jax.ShapeDtypeStruct((T, D), F32),
        compiler_params=_cparams(1),
        name=f"mlp_l{layer}",
    )(x, mods, w1, w2, g, b)


def _rope_tables():
    n = HEAD_DIM // 4
    pos = np.arange(DEC_SEQ)
    inv = (np.float32(ROPE_BASE) ** (-np.arange(n, dtype=np.float32) / np.float32(n))).astype(np.float32)
    ang_r = ((pos // GRID_W).astype(np.float32)[:, None] * inv).astype(np.float32)
    ang_c = ((pos % GRID_W).astype(np.float32)[:, None] * inv).astype(np.float32)
    ang_r, ang_c = jnp.asarray(ang_r), jnp.asarray(ang_c)
    cr, sr, cc, sc = jnp.cos(ang_r), jnp.sin(ang_r), jnp.cos(ang_c), jnp.sin(ang_c)
    a = jnp.concatenate([cr, cr, cc, cc], axis=-1)
    b = jnp.concatenate([-sr, sr, -sc, sc], axis=-1)
    return jnp.tile(a, (1, D // HEAD_DIM)), jnp.tile(b, (1, D // HEAD_DIM))


def kernel(x_prompt, x_sample, c, cache_da_k, cache_da_v, cache_na_k, cache_na_v, cache_gq_k, cache_gq_v, c_ctx, ada_w, ada_b, ln_g, ln_b, mlp_w1, mlp_w2, da_w_qkv, da_w_o, da_lambda, da_subln_g, na_w_qkv, na_w_o, na_rel_bias, gq_w_qkv, gq_w_o, gq_q_norm, gq_k_norm, hy_w_in, hy_short_w, hy_short_b, hy_ffn_w1, hy_ffn_b1, hy_ffn_w2, hy_ffn_b2, hy_ffn_freq, hy_ffn_w3, hy_log_decay, hy_filter_bias, hy_w_o):
    x = jnp.concatenate([x_prompt.reshape(T_CTX, D), x_sample.reshape(T_LAT, D)], axis=0)
    cvec = jnp.concatenate([c_ctx[None, :], c, jnp.zeros((MOD_ROWS - 1 - DEC_BATCH, D), F32)], axis=0)
    mods = _mods(cvec, ada_w, ada_b)
    rope_a, rope_b = _rope_tables()

    def finish(o, x, layer, w_o):
        x = _outproj(o, x, mods, layer, w_o.astype(BF), ln_g[layer, 0][None], ln_b[layer, 0][None])
        return _mlp(x, mods, layer, mlp_w1[layer].astype(BF), mlp_w2[layer].astype(BF),
                    ln_g[layer, 1][None], ln_b[layer, 1][None])

    q, k, v = _da_proj(x, mods, 0, da_w_qkv[0].astype(BF), rope_a, rope_b)
    state_da_k = k[:T_CTX].reshape(BATCH, 1, SEQ, DA_HEADS, 2 * HEAD_DIM)
    state_da_v = v[:T_CTX].reshape(BATCH, 1, SEQ, DA_HEADS, 2 * HEAD_DIM)
    o = _da_attention(q, k, v, cache_da_k.reshape(DEC_BATCH, PAST, D), cache_da_v.reshape(DEC_BATCH, PAST, D),
                      da_lambda[0], da_subln_g[0][None], 0)
    x = finish(o, x, 0, da_w_o[0])

    q, k, v = _plain_proj(x, mods, 1, na_w_qkv[0].astype(BF), (D, D, D))
    state_na_k = k[:T_CTX].reshape(BATCH, 1, SEQ, NA_HEADS, HEAD_DIM)
    state_na_v = v[:T_CTX].reshape(BATCH, 1, SEQ, NA_HEADS, HEAD_DIM)
    onehot, neg, mask = _na_constants()
    bias_tab = _na_bias_table(na_rel_bias[0], onehot, neg)
    o = _na_ctx_attention(q, k, v)
    o = _na_lat_attention(q, k, v, cache_na_k.reshape(DEC_BATCH, PAST, D), cache_na_v.reshape(DEC_BATCH, PAST, D),
                          bias_tab, mask, o)
    x = finish(o, x, 1, na_w_o[0])

    nk = GQ_KV_HEADS * HEAD_DIM
    g_mat = jnp.asarray(np.kron(np.eye(GQ_HEADS), np.full((HEAD_DIM, HEAD_DIM), 1.0 / HEAD_DIM)), BF)
    q, k, v = _gq_proj(x, mods, 2, gq_w_qkv[0].astype(BF), g_mat,
                       jnp.tile(gq_q_norm[0], GQ_HEADS)[None], jnp.tile(gq_k_norm[0], GQ_KV_HEADS)[None],
                       rope_a, rope_b)
    state_gq_k = k[:T_CTX].reshape(BATCH, 1, SEQ, GQ_KV_HEADS, HEAD_DIM)
    state_gq_v = v[:T_CTX].reshape(BATCH, 1, SEQ, GQ_KV_HEADS, HEAD_DIM)
    o = _gq_attention(q, k, v, cache_gq_k.reshape(DEC_BATCH, PAST, nk), cache_gq_v.reshape(DEC_BATCH, PAST, nk))
    x = finish(o, x, 2, gq_w_o[0])

    (u,) = _plain_proj(x, mods, 3, hy_w_in[0].astype(BF), ((HY_ORDER + 1) * D,))
    w1 = jnp.pad(hy_ffn_w1[0], ((0, HY_EMB_PAD - HY_EMB), (0, 0)))
    z = None
    for seq, nbatch, row0 in ((SEQ, BATCH, 0), (DEC_SEQ, DEC_BATCH, T_CTX)):
        cmat, smat = _dft_tables(seq)
        hre, him, hny = _hy_filter(seq, _hy_embedding(seq), w1, hy_ffn_b1[0][None], hy_ffn_w2[0], hy_ffn_b2[0][None],
                                   hy_ffn_freq[0][None], hy_ffn_w3[0], hy_log_decay[0][None], cmat, smat)
        z = _hy_conv(u, hy_short_w[0], hy_short_b[0][None], hy_filter_bias[0], hre, him, hny, cmat, smat,
                     seq, nbatch, row0, z)
    x = finish(z, x, 3, hy_w_o[0])

    return (x[:T_CTX].reshape(BATCH, SEQ, D), x[T_CTX:].reshape(DEC_BATCH, DEC_SEQ, D),
            state_da_k, state_da_v, state_na_k, state_na_v, state_gq_k, state_gq_v)
```

```python
import functools
import math

import numpy as np
import jax
import jax.numpy as jnp
from jax import lax
from jax.experimental import pallas as pl
from jax.experimental.pallas import tpu as pltpu

F32 = jnp.float32
BF = jnp.bfloat16

D = 1024
BATCH = 16
SEQ = 256
DEC_BATCH = 8
DEC_SEQ = 1024
PAST = 256
DEPTH = 4
GRID_W = 64
GRID_ROWS = DEC_SEQ // GRID_W
D_FF = 4 * D
T_CTX = BATCH * SEQ
T_LAT = DEC_BATCH * DEC_SEQ
T = T_CTX + T_LAT
HEAD_DIM = 64
ATT_SCALE = HEAD_DIM ** -0.5
LOG2E = math.log2(math.e)
Q_SCALE = ATT_SCALE * LOG2E
DA_HEADS = 8
NA_HEADS = 16
NA_WIN_ROWS = 8
NA_WIN_COLS = 16
GQ_HEADS = 16
GQ_KV_HEADS = 4
HY_ORDER = 2
HY_BANDS = 16
HY_EMB = 1 + 2 * HY_BANDS
HY_EMB_PAD = 40
HY_FFN = 64
ROPE_BASE = 10000.0
LN_EPS = 1e-5
RMS_EPS = 1e-6
DN_ALPHA = (2 * DEPTH) ** 0.25
NEG_INF = -1e30

LANES = 128
TM = 512
N_CTX_TILES = T_CTX // TM
N_TILES = T // TM
TQ = 256
MOD_ROWS = 16
VMEM_LIMIT = 56 * 1024 * 1024


def _cparams(n_axes):
    return pltpu.CompilerParams(dimension_semantics=("arbitrary",) * n_axes,
                                vmem_limit_bytes=VMEM_LIMIT)


def _dot(a, b):
    return jnp.dot(a, b, preferred_element_type=F32)


def _dot_nt(a, b):
    return lax.dot_general(a, b, (((1,), (1,)), ((), ())), preferred_element_type=F32)


def _const_spec(shape):
    nd = len(shape)
    return pl.BlockSpec(shape, lambda *_: (0,) * nd, pipeline_mode=pl.Buffered(1))


def _mod_row(i):
    return jnp.where(i < N_CTX_TILES, 0, 1 + (i - N_CTX_TILES) // (DEC_SEQ // TM))


def _mod_spec(layer):
    return pl.BlockSpec((None, None, 6, D), lambda i: (layer, _mod_row(i), 0, 0))


def _tok_spec(width):
    return pl.BlockSpec((TM, width), lambda i: (i, 0))


def _ctx_spec(width):
    return pl.BlockSpec((TM, width), lambda i: (jnp.minimum(i, N_CTX_TILES - 1), 0))


def _lat_spec(width):
    return pl.BlockSpec((TM, width), lambda i: (jnp.maximum(i - N_CTX_TILES, 0), 0))


def _is_lat():
    return pl.program_id(0) >= N_CTX_TILES


def _pick(ctx_ref, lat_ref):
    return jnp.where(_is_lat(), lat_ref[...], ctx_ref[...])


def _layer_norm(r, g, b):
    mu = jnp.mean(r, axis=-1, keepdims=True)
    c = r - mu
    var = jnp.mean(c * c, axis=-1, keepdims=True)
    return c * lax.rsqrt(var + LN_EPS) * g + b


def _mods_kernel(c_ref, w_ref, b_ref, o_ref):
    c = c_ref[...]
    s = (c / (1.0 + jnp.exp(-c))).astype(BF)
    o_ref[...] = _dot(s, w_ref[...].astype(BF)) + b_ref[...]


def _mods(cvec, ada_w, ada_b):
    tn = 1536
    out = pl.pallas_call(
        _mods_kernel,
        grid=(DEPTH, 6 * D // tn),
        in_specs=[pl.BlockSpec((MOD_ROWS, D), lambda l, n: (0, 0)),
                  pl.BlockSpec((None, D, tn), lambda l, n: (l, 0, n)),
                  pl.BlockSpec((None, 1, tn), lambda l, n: (l, 0, n))],
        out_specs=pl.BlockSpec((None, MOD_ROWS, tn), lambda l, n: (l, 0, n)),
        out_shape=jax.ShapeDtypeStruct((DEPTH, MOD_ROWS, 6 * D), F32),
        compiler_params=_cparams(2),
        name="adaln_mods",
    )(cvec, ada_w, ada_b.reshape(DEPTH, 1, 6 * D))
    return out.reshape(DEPTH, MOD_ROWS, 6, D)


def _modulate(x, mod_ref, shift, scale):
    return (x * (1.0 + mod_ref[scale:scale + 1, :]) + mod_ref[shift:shift + 1, :]).astype(BF)


def _rope(x, a, b):
    n = x.shape[1]
    lane = lax.broadcasted_iota(jnp.int32, x.shape, 1)
    partner = jnp.where((lane & 16) == 0, pltpu.roll(x, n - 16, 1), pltpu.roll(x, 16, 1))
    return x * a + partner * b


def _rope_spec(width):
    per = DEC_SEQ // TM
    return pl.BlockSpec((TM, width), lambda i: (jnp.maximum(i - N_CTX_TILES, 0) % per, 0))


def _store_qkv(q, k, v, ra_ref, rb_ref, qb_ref, kb_ref, vb_ref, ks_ref, vs_ref):
    vb_ref[...] = v.astype(BF)

    @pl.when(jnp.logical_not(_is_lat()))
    def _():
        qb_ref[...] = (q * Q_SCALE).astype(BF)
        kb_ref[...] = k.astype(BF)
        ks_ref[...] = k
        vs_ref[...] = v

    if ra_ref is None:
        @pl.when(_is_lat())
        def _():
            qb_ref[...] = (q * Q_SCALE).astype(BF)
            kb_ref[...] = k.astype(BF)
    else:
        @pl.when(_is_lat())
        def _():
            a, b = ra_ref[...], rb_ref[...]
            nk = k.shape[1]
            qb_ref[...] = (_rope(q, a, b) * Q_SCALE).astype(BF)
            kb_ref[...] = _rope(k, a[:, 0:nk], b[:, 0:nk]).astype(BF)


def _qkv_out(nq, nk):
    specs = [_tok_spec(nq), _tok_spec(nk), _tok_spec(nk), _ctx_spec(nk), _ctx_spec(nk)]
    shapes = [jax.ShapeDtypeStruct((T, nq), BF), jax.ShapeDtypeStruct((T, nk), BF), jax.ShapeDtypeStruct((T, nk), BF),
              jax.ShapeDtypeStruct((T_CTX, nk), F32), jax.ShapeDtypeStruct((T_CTX, nk), F32)]
    return specs, shapes


def _da_proj_kernel(xc_ref, xl_ref, mod_ref, w_ref, ra_ref, rb_ref, *outs):
    h = _modulate(_pick(xc_ref, xl_ref), mod_ref, 0, 1)
    q = _dot(h, w_ref[:, 0:D])
    k = _dot(h, w_ref[:, D:2 * D])
    v = _dot(h, w_ref[:, 2 * D:3 * D])
    _store_qkv(q, k, v, ra_ref, rb_ref, *outs)


def _da_proj(x_ctx, x_lat, mods, layer, w, rope_a, rope_b):
    specs, shapes = _qkv_out(D, D)
    return pl.pallas_call(
        _da_proj_kernel,
        grid=(N_TILES,),
        in_specs=[_ctx_spec(D), _lat_spec(D), _mod_spec(layer), _const_spec((D, 3 * D)),
                  _rope_spec(D), _rope_spec(D)],
        out_specs=specs, out_shape=shapes,
        compiler_params=_cparams(1),
        name=f"da_proj_l{layer}",
    )(x_ctx, x_lat, mods, w, rope_a, rope_b)


def _na_proj_kernel(x_ref, mod_ref, w_ref, *outs):
    h = _modulate(x_ref[...], mod_ref, 0, 1)
    q = _dot(h, w_ref[:, 0:D])
    k = _dot(h, w_ref[:, D:2 * D])
    v = _dot(h, w_ref[:, 2 * D:3 * D])
    _store_qkv(q, k, v, None, None, *outs)


def _na_proj(x, mods, layer, w):
    specs, shapes = _qkv_out(D, D)
    return pl.pallas_call(
        _na_proj_kernel,
        grid=(N_TILES,),
        in_specs=[_tok_spec(D), _mod_spec(layer), _const_spec((D, 3 * D))],
        out_specs=specs, out_shape=shapes,
        compiler_params=_cparams(1),
        name=f"na_proj_l{layer}",
    )(x, mods, w)


def _head_rms(x, g_mat, gain):
    x2 = x * x
    hi = x2.astype(BF)
    lo = (x2 - hi.astype(F32)).astype(BF)
    ms = _dot(hi, g_mat) + _dot(lo, g_mat)
    return x * lax.rsqrt(ms + RMS_EPS) * gain


def _gq_proj_kernel(x_ref, mod_ref, w_ref, g_ref, qn_ref, kn_ref, ra_ref, rb_ref, *outs):
    nq, nk = GQ_HEADS * HEAD_DIM, GQ_KV_HEADS * HEAD_DIM
    h = _modulate(x_ref[...], mod_ref, 0, 1)
    q = _head_rms(_dot(h, w_ref[:, 0:nq]), g_ref[...], qn_ref[...])
    k = _head_rms(_dot(h, w_ref[:, nq:nq + nk]), g_ref[0:nk, 0:nk], kn_ref[...])
    v = _dot(h, w_ref[:, nq + nk:nq + 2 * nk])
    _store_qkv(q, k, v, ra_ref, rb_ref, *outs)


def _gq_proj(x, mods, layer, w, g_mat, qn, kn, rope_a, rope_b):
    nq, nk = GQ_HEADS * HEAD_DIM, GQ_KV_HEADS * HEAD_DIM
    specs, shapes = _qkv_out(nq, nk)
    return pl.pallas_call(
        _gq_proj_kernel,
        grid=(N_TILES,),
        in_specs=[_tok_spec(D), _mod_spec(layer), _const_spec((D, nq + 2 * nk)),
                  _const_spec((nq, nq)), _const_spec((1, nq)), _const_spec((1, nk)),
                  _rope_spec(D), _rope_spec(D)],
        out_specs=specs, out_shape=shapes,
        compiler_params=_cparams(1),
        name=f"gq_proj_l{layer}",
    )(x, mods, w, g_mat, qn, kn, rope_a, rope_b)


def _hy_proj_kernel(x_ref, mod_ref, w_ref, u_ref):
    h = _modulate(x_ref[...], mod_ref, 0, 1)
    for c in range(HY_ORDER + 1):
        u_ref[:, c * D:(c + 1) * D] = _dot(h, w_ref[:, c * D:(c + 1) * D])


def _hy_proj(x, mods, layer, w):
    n = (HY_ORDER + 1) * D
    return pl.pallas_call(
        _hy_proj_kernel,
        grid=(N_TILES,),
        in_specs=[_tok_spec(D), _mod_spec(layer), _const_spec((D, n))],
        out_specs=_tok_spec(n),
        out_shape=jax.ShapeDtypeStruct((T, n), F32),
        compiler_params=_cparams(1),
        name=f"hy_proj_l{layer}",
    )(x, mods, w)


def _softmax_pv(qm, segs):
    scores = [_dot_nt(qm, k) for k, _ in segs]
    m = scores[0].max(axis=-1, keepdims=True)
    for s in scores[1:]:
        m = jnp.maximum(m, s.max(axis=-1, keepdims=True))
    den = None
    out = None
    for s, (_, v) in zip(scores, segs):
        e = jnp.exp2(s - m)
        d = e.sum(axis=-1, keepdims=True)
        o = _dot(e.astype(BF), v)
        den = d if den is None else den + d
        out = o if out is None else out + o
    return out / den


def _lane_half(shape):
    return lax.broadcasted_iota(jnp.int32, shape, 1) // HEAD_DIM


def _half_keep(half):
    return tuple(jnp.where(half == a, 1.0, 0.0).astype(BF) for a in (0, 1))


def _da_attn_kernel(*refs, has_cache, lam_init):
    if has_cache:
        q_ref, k_ref, v_ref, ck_ref, cv_ref, lam_ref, g_ref, o_ref = refs
    else:
        q_ref, k_ref, v_ref, lam_ref, g_ref, o_ref = refs
    lp = lam_ref[...]
    lam = (jnp.exp(jnp.sum(lp[0:1] * lp[1:2], axis=-1, keepdims=True))
           - jnp.exp(jnp.sum(lp[2:3] * lp[3:4], axis=-1, keepdims=True)) + lam_init)
    gain = g_ref[...] * (1.0 - lam_init)
    w = 2 * HEAD_DIM
    keep = _half_keep(_lane_half((TQ, w)))
    for hd in range(k_ref.shape[1] // w):
        cols = slice(hd * w, (hd + 1) * w)
        segs = [(k_ref[:, cols], v_ref[:, cols])]
        if has_cache:
            segs.append((ck_ref[:, cols].astype(BF), cv_ref[:, cols].astype(BF)))
        for t in range(q_ref.shape[0] // TQ):
            rows = slice(t * TQ, (t + 1) * TQ)
            q = q_ref[rows, cols]
            o1 = _softmax_pv(q * keep[0], segs)
            o2 = _softmax_pv(q * keep[1], segs)
            o = o1 - lam * o2
            ms = jnp.mean(o * o, axis=-1, keepdims=True)
            o_ref[rows, cols] = (o * lax.rsqrt(ms + RMS_EPS) * gain).astype(BF)


def _da_attention(qb, kb, vb, cache_k, cache_v, lam_p, subln_g, layer_idx):
    lam_init = 0.8 - 0.6 * math.exp(-0.3 * layer_idx)
    w = 2 * HEAD_DIM
    small = [pl.BlockSpec((4, HEAD_DIM), lambda *_: (0, 0)), pl.BlockSpec((1, w), lambda *_: (0, 0))]
    o_ctx = pl.pallas_call(
        functools.partial(_da_attn_kernel, has_cache=False, lam_init=lam_init),
        grid=(BATCH,),
        in_specs=[pl.BlockSpec((SEQ, D), lambda b: (b, 0))] * 3 + small,
        out_specs=pl.BlockSpec((SEQ, D), lambda b: (b, 0)),
        out_shape=jax.ShapeDtypeStruct((T_CTX, D), BF),
        compiler_params=_cparams(1),
        name="da_attn_ctx",
    )(qb, kb, vb, lam_p, subln_g)
    k0 = T_CTX // DEC_SEQ
    tok = pl.BlockSpec((DEC_SEQ, w), lambda b, h: (k0 + b, h))
    c_spec = pl.BlockSpec((None, PAST, w), lambda b, h: (b, 0, h))
    o_lat = pl.pallas_call(
        functools.partial(_da_attn_kernel, has_cache=True, lam_init=lam_init),
        grid=(DEC_BATCH, DA_HEADS),
        in_specs=[tok, tok, tok, c_spec, c_spec] + small,
        out_specs=pl.BlockSpec((DEC_SEQ, w), lambda b, h: (b, h)),
        out_shape=jax.ShapeDtypeStruct((T_LAT, D), BF),
        compiler_params=_cparams(2),
        name="da_attn_lat",
    )(qb, kb, vb, cache_k, cache_v, lam_p, subln_g)
    return o_ctx, o_lat


def _na_ctx_kernel(q_ref, k_ref, v_ref, o_ref):
    half = _lane_half((SEQ, LANES))
    keep = _half_keep(half)
    for p in range(NA_HEADS // 2):
        cols = slice(p * LANES, (p + 1) * LANES)
        segs = [(k_ref[:, cols], v_ref[:, cols])]
        q = q_ref[:, cols]
        o0 = _softmax_pv(q * keep[0], segs)
        o1 = _softmax_pv(q * keep[1], segs)
        o_ref[:, cols] = jnp.where(half == 0, o0, o1).astype(BF)


def _na_ctx_attention(qb, kb, vb):
    spec = pl.BlockSpec((SEQ, D), lambda b: (b, 0))
    return pl.pallas_call(
        _na_ctx_kernel,
        grid=(BATCH,),
        in_specs=[spec] * 3,
        out_specs=spec,
        out_shape=jax.ShapeDtypeStruct((T_CTX, D), BF),
        compiler_params=_cparams(1),
        name="na_attn_ctx",
    )(qb, kb, vb)


NA_TILES = ((0, (0, 2, 4, 6)), (4, (0, 2, 4, 6, 8, 10)), (8, (4, 6, 8, 10, 12, 14)), (12, (8, 10, 12, 14)))
NA_MAX_CHUNKS = 6
NA_BIAS_BLOCKS = 2 * NA_WIN_ROWS - 2


def _na_lat_kernel(q_ref, k_ref, v_ref, ck_ref, cv_ref, w_ref, m_ref, o_ref):
    ckb, cvb = ck_ref[...].astype(BF), cv_ref[...].astype(BF)
    rows = 4 * GRID_W
    half = _lane_half((rows, LANES))
    keep = _half_keep(half)
    for i, (r0, chunks) in enumerate(NA_TILES):
        q = q_ref[i * rows:(i + 1) * rows, :]
        key0, nkey = chunks[0] * GRID_W, len(chunks) * LANES
        mask = m_ref[i, :, 0:nkey]
        kb, vb = k_ref[key0:key0 + nkey, :], v_ref[key0:key0 + nkey, :]
        outs = []
        for a in (0, 1):
            qm = q * keep[a]
            bias = jnp.concatenate(
                [w_ref[a, (6 - kr + r0) * GRID_W:(6 - kr + r0) * GRID_W + rows, :] for kr in chunks], axis=1)
            s_loc = _dot_nt(qm, kb) + bias + mask
            s_ctx = _dot_nt(qm, ckb)
            m = jnp.maximum(s_loc.max(axis=-1, keepdims=True), s_ctx.max(axis=-1, keepdims=True))
            e_loc, e_ctx = jnp.exp2(s_loc - m), jnp.exp2(s_ctx - m)
            den = e_loc.sum(axis=-1, keepdims=True) + e_ctx.sum(axis=-1, keepdims=True)
            o = _dot(e_loc.astype(BF), vb) + _dot(e_ctx.astype(BF), cvb)
            outs.append(o / den)
        o_ref[i * rows:(i + 1) * rows, :] = jnp.where(half == 0, outs[0], outs[1]).astype(BF)


def _na_lat_attention(qb, kb, vb, cache_k, cache_v, bias_tab, mask_tab):
    k0 = T_CTX // DEC_SEQ
    tok = pl.BlockSpec((DEC_SEQ, LANES), lambda p, b: (k0 + b, p))
    c_spec = pl.BlockSpec((None, PAST, LANES), lambda p, b: (b, 0, p))
    return pl.pallas_call(
        _na_lat_kernel,
        grid=(NA_HEADS // 2, DEC_BATCH),
        in_specs=[tok, tok, tok, c_spec, c_spec,
                  pl.BlockSpec((None, 2, NA_BIAS_BLOCKS * GRID_W, LANES), lambda p, b: (p, 0, 0, 0)),
                  _const_spec(mask_tab.shape)],
        out_specs=pl.BlockSpec((DEC_SEQ, LANES), lambda p, b: (b, p)),
        out_shape=jax.ShapeDtypeStruct((T_LAT, D), BF),
        compiler_params=_cparams(2),
        name="na_attn_lat",
    )(qb, kb, vb, cache_k, cache_v, bias_tab, mask_tab)


def _na_bias_kernel(t_ref, r_ref, n_ref, o_ref):
    t = t_ref[...]
    t1 = t.astype(BF)
    r1 = t - t1.astype(F32)
    t2 = r1.astype(BF)
    t3 = (r1 - t2.astype(F32)).astype(BF)
    r = r_ref[...]
    o_ref[...] = (_dot(t1, r) + _dot(t2, r) + _dot(t3, r) + n_ref[...]) * LOG2E


def _na_bias_table(rel_bias, onehot, neg):
    nrel = 2 * NA_WIN_COLS
    idx = 13 - np.arange(NA_BIAS_BLOCKS)[:, None] + np.arange(2)[None, :]
    t = jnp.pad(rel_bias[:, idx, :], ((0, 0), (0, 0), (0, 0), (0, 1)))
    t = t.reshape(NA_HEADS * NA_BIAS_BLOCKS, 2 * nrel)
    n = GRID_W * LANES
    tn = 2048
    out = pl.pallas_call(
        _na_bias_kernel,
        grid=(n // tn,),
        in_specs=[pl.BlockSpec(t.shape, lambda j: (0, 0)),
                  pl.BlockSpec((2 * nrel, tn), lambda j: (0, j)),
                  pl.BlockSpec((1, tn), lambda j: (0, j))],
        out_specs=pl.BlockSpec((t.shape[0], tn), lambda j: (0, j)),
        out_shape=jax.ShapeDtypeStruct((t.shape[0], n), F32),
        compiler_params=_cparams(1),
        name="na_bias_table",
    )(t, onehot, neg)
    return out.reshape(NA_HEADS // 2, 2, NA_BIAS_BLOCKS * GRID_W, LANES)


def _na_constants():
    nrel = 2 * NA_WIN_COLS
    qc = np.arange(GRID_W)[:, None]
    kc = np.arange(GRID_W)[None, :]
    rel = np.clip(kc - qc, -(NA_WIN_COLS - 1), NA_WIN_COLS - 1) + NA_WIN_COLS - 1
    cs = np.clip(qc - NA_WIN_COLS // 2, 0, GRID_W - NA_WIN_COLS)
    col_in = (kc >= cs) & (kc < cs + NA_WIN_COLS)
    onehot = np.zeros((2, nrel, GRID_W, 2, GRID_W), np.float32)
    for hf in range(2):
        onehot[hf, rel, qc, hf, kc] = 1.0
    neg = np.where(col_in, 0.0, NEG_INF).astype(np.float32)
    neg = np.broadcast_to(neg[:, None, :], (GRID_W, 2, GRID_W)).reshape(1, -1)
    rows = 4 * GRID_W
    mask = np.full((len(NA_TILES), rows, NA_MAX_CHUNKS * LANES), NEG_INF, np.float32)
    kr = min(NA_WIN_ROWS, GRID_ROWS)
    for i, (r0, chunks) in enumerate(NA_TILES):
        qr = r0 + np.arange(rows)[:, None] // GRID_W
        rs = np.clip(qr - kr // 2, 0, GRID_ROWS - kr)
        for c, krow0 in enumerate(chunks):
            krow = krow0 + np.arange(LANES)[None, :] // GRID_W
            mask[i, :, c * LANES:(c + 1) * LANES] = np.where((krow >= rs) & (krow < rs + kr), 0.0, NEG_INF)
    return (jnp.asarray(onehot.reshape(2 * nrel, GRID_W * LANES), BF), jnp.asarray(neg), jnp.asarray(mask))


def _gq_attn_kernel(*refs, has_cache):
    if has_cache:
        q_ref, k_ref, v_ref, ck_ref, cv_ref, o_ref = refs
    else:
        q_ref, k_ref, v_ref, o_ref = refs
    group = GQ_HEADS // GQ_KV_HEADS
    qw = LANES * group
    half = _lane_half((q_ref.shape[0], LANES))
    keep = _half_keep(half)
    for kvp in range(k_ref.shape[1] // LANES):
        kcols = slice(kvp * LANES, (kvp + 1) * LANES)
        segs = [(k_ref[:, kcols], v_ref[:, kcols])]
        if has_cache:
            segs.append((ck_ref[:, kcols].astype(BF), cv_ref[:, kcols].astype(BF)))
        for pair in range(group):
            cols = slice(kvp * qw + pair * LANES, kvp * qw + (pair + 1) * LANES)
            q = q_ref[:, cols]
            kv_half = (2 * pair) // group
            outs = []
            for a in (0, 1):
                qm = q * keep[a]
                if a != kv_half:
                    qm = pltpu.roll(qm.astype(F32), HEAD_DIM, 1).astype(BF)
                o = _softmax_pv(qm, segs)
                if a != kv_half:
                    o = pltpu.roll(o, HEAD_DIM, 1)
                outs.append(o)
            o_ref[:, cols] = jnp.where(half == 0, outs[0], outs[1]).astype(BF)


def _gq_attention(qb, kb, vb, cache_k, cache_v):
    nk = GQ_KV_HEADS * HEAD_DIM
    qw = LANES * (GQ_HEADS // GQ_KV_HEADS)
    npair = GQ_KV_HEADS // 2
    o_ctx = pl.pallas_call(
        functools.partial(_gq_attn_kernel, has_cache=False),
        grid=(BATCH,),
        in_specs=[pl.BlockSpec((SEQ, D), lambda b: (b, 0))] + [pl.BlockSpec((SEQ, nk), lambda b: (b, 0))] * 2,
        out_specs=pl.BlockSpec((SEQ, D), lambda b: (b, 0)),
        out_shape=jax.ShapeDtypeStruct((T_CTX, D), BF),
        compiler_params=_cparams(1),
        name="gq_attn_ctx",
    )(qb, kb, vb)
    qt = DEC_SEQ // TQ
    q0, k0 = T_CTX // TQ, T_CTX // DEC_SEQ
    kv_spec = pl.BlockSpec((DEC_SEQ, LANES), lambda b, p, t: (k0 + b, p))
    c_spec = pl.BlockSpec((None, PAST, LANES), lambda b, p, t: (b, 0, p))
    o_lat = pl.pallas_call(
        functools.partial(_gq_attn_kernel, has_cache=True),
        grid=(DEC_BATCH, npair, qt),
        in_specs=[pl.BlockSpec((TQ, qw), lambda b, p, t: (q0 + b * qt + t, p)), kv_spec, kv_spec, c_spec, c_spec],
        out_specs=pl.BlockSpec((TQ, qw), lambda b, p, t: (b * qt + t, p)),
        out_shape=jax.ShapeDtypeStruct((T_LAT, D), BF),
        compiler_params=_cparams(3),
        name="gq_attn_lat",
    )(qb, kb, vb, cache_k, cache_v)
    return o_ctx, o_lat


def _hy_filter_kernel(emb_ref, w1_ref, b1_ref, w2_ref, b2_ref, fr_ref, w3f_ref, w3b_ref, ldf_ref, ldb_ref,
                      c_ref, s_ref, hre_ref, him_ref, hny_ref):
    seq = emb_ref.shape[0]
    hp = lax.Precision.HIGHEST
    emb = emb_ref[...]
    fr = fr_ref[...]
    hid = jnp.sin(fr * (jnp.dot(emb, w1_ref[...], precision=hp, preferred_element_type=F32) + b1_ref[...]))
    hid = jnp.sin(fr * (jnp.dot(hid, w2_ref[...], precision=hp, preferred_element_type=F32) + b2_ref[...]))
    t = emb[:, 0:1]
    fwd = jnp.dot(hid, w3f_ref[...], precision=hp, preferred_element_type=F32) * jnp.exp(-jnp.exp(ldf_ref[...]) * t)
    bwd = jnp.dot(hid, w3b_ref[...], precision=hp, preferred_element_type=F32) * jnp.exp(-jnp.exp(ldb_ref[...]) * t)
    row = lax.broadcasted_iota(jnp.int32, fwd.shape, 0)
    bwd = jnp.where(row == 0, 0.0, bwd)
    even = fwd + bwd
    odd = bwd - fwd
    wk = jnp.where(row == 0, 0.5 / seq, 1.0 / seq)
    hre_ref[...] = _dot(c_ref[...].astype(BF), even.astype(BF)) * wk
    him_ref[...] = _dot(s_ref[...].astype(BF), odd.astype(BF)) * wk
    alt = jnp.where((row & 1) == 0, 1.0, -1.0)
    hny_ref[...] = jnp.sum(alt * even, axis=0, keepdims=True) * (0.5 / seq)


def _hy_filter(seq, emb, w1, b1, w2, b2, freq, w3, log_decay, cmat, smat):
    dc = 512
    nj = D // dc
    small = [_const_spec(a.shape) for a in (emb, w1, b1, w2, b2, freq)]
    return pl.pallas_call(
        _hy_filter_kernel,
        grid=(HY_ORDER, nj),
        in_specs=small + [pl.BlockSpec((HY_FFN, dc), lambda o, j: (0, (2 * o) * nj + j)),
                          pl.BlockSpec((HY_FFN, dc), lambda o, j: (0, (2 * o + 1) * nj + j)),
                          pl.BlockSpec((1, dc), lambda o, j: (0, (2 * o) * nj + j)),
                          pl.BlockSpec((1, dc), lambda o, j: (0, (2 * o + 1) * nj + j)),
                          _const_spec((seq, seq)), _const_spec((seq, seq))],
        out_specs=[pl.BlockSpec((None, seq, dc), lambda o, j: (o, 0, j)),
                   pl.BlockSpec((None, seq, dc), lambda o, j: (o, 0, j)),
                   pl.BlockSpec((None, 1, dc), lambda o, j: (o, 0, j))],
        out_shape=[jax.ShapeDtypeStruct((HY_ORDER, seq, D), F32), jax.ShapeDtypeStruct((HY_ORDER, seq, D), F32),
                   jax.ShapeDtypeStruct((HY_ORDER, 1, D), F32)],
        compiler_params=_cparams(2),
        name=f"hy_filter_{seq}",
    )(emb, w1, b1, w2, b2, freq, w3, w3, log_decay, log_decay, cmat, smat)


def _hy_conv_kernel(u0_ref, u1_ref, u2_ref, sw0_ref, sw1_ref, sw2_ref, sb0_ref, sb1_ref, sb2_ref,
                    fb_ref, hre_ref, him_ref, hny_ref, c_ref, s_ref, o_ref, cb_ref, sb_ref):
    seq = u0_ref.shape[0]

    @pl.when((pl.program_id(0) == 0) & (pl.program_id(1) == 0))
    def _():
        cb_ref[...] = c_ref[...].astype(BF)
        sb_ref[...] = s_ref[...].astype(BF)

    row = lax.broadcasted_iota(jnp.int32, u0_ref.shape, 0)
    alt = jnp.where((row & 1) == 0, 1.0, -1.0)

    def short_conv(u_ref, w_ref, b_ref):
        u = u_ref[...]
        prev = jnp.where(row == 0, 0.0, pltpu.roll(u, 1, 0))
        nxt = jnp.where(row == seq - 1, 0.0, pltpu.roll(u, seq - 1, 0))
        return prev * w_ref[0:1, :] + u * w_ref[1:2, :] + nxt * w_ref[2:3, :] + b_ref[...]

    cm, sm = cb_ref[...], sb_ref[...]
    z = short_conv(u0_ref, sw0_ref, sb0_ref)
    gates = (short_conv(u1_ref, sw1_ref, sb1_ref), short_conv(u2_ref, sw2_ref, sb2_ref))
    for o in range(HY_ORDER):
        zb = z.astype(BF)
        zc, zs = _dot(cm, zb), _dot(sm, zb)
        hre, him = hre_ref[o], him_ref[o]
        p_re = zc * hre + zs * him
        p_im = zc * him - zs * hre
        nyq = jnp.sum(alt * z, axis=0, keepdims=True) * hny_ref[o]
        y = _dot(cm, p_re.astype(BF)) - _dot(sm, p_im.astype(BF)) + alt * nyq
        z = gates[o] * (y + z * fb_ref[o:o + 1, :])
    o_ref[...] = z.astype(BF)


def _hy_conv(u, short_w, short_b, filter_bias, hre, him, hny, cmat, smat, seq, nbatch, row0, dc):
    nj = D // dc
    r0 = row0 // seq

    def part(p):
        return pl.BlockSpec((seq, dc), lambda j, b: (r0 + b, p * nj + j))

    def vec(rows, p):
        return pl.BlockSpec((rows, dc), lambda j, b: (0, p * nj + j))

    in_specs = ([part(p) for p in range(3)] + [vec(3, p) for p in range(3)] + [vec(1, p) for p in range(3)]
                + [pl.BlockSpec((HY_ORDER, dc), lambda j, b: (0, j)),
                   pl.BlockSpec((HY_ORDER, seq, dc), lambda j, b: (0, 0, j)),
                   pl.BlockSpec((HY_ORDER, seq, dc), lambda j, b: (0, 0, j)),
                   pl.BlockSpec((HY_ORDER, 1, dc), lambda j, b: (0, 0, j)),
                   _const_spec((seq, seq)), _const_spec((seq, seq))])
    return pl.pallas_call(
        _hy_conv_kernel,
        grid=(nj, nbatch),
        in_specs=in_specs,
        out_specs=pl.BlockSpec((seq, dc), lambda j, b: (b, j)),
        out_shape=jax.ShapeDtypeStruct((nbatch * seq, D), BF),
        scratch_shapes=[pltpu.VMEM((seq, seq), BF), pltpu.VMEM((seq, seq), BF)],
        compiler_params=_cparams(2),
        name=f"hy_conv_{seq}",
    )(u, u, u, short_w, short_w, short_w, short_b, short_b, short_b, filter_bias, hre, him, hny, cmat, smat)


def _dft_tables(seq):
    k = np.arange(seq, dtype=np.int64)
    ang = np.pi * ((k[:, None] * k[None, :]) % (2 * seq)) / seq
    return jnp.asarray(np.cos(ang), F32), jnp.asarray(np.sin(ang), F32)


def _hy_embedding(seq):
    t = np.arange(seq, dtype=np.float32) / np.float32(seq)
    ang = (2.0 * math.pi) * t[:, None] * np.arange(1, HY_BANDS + 1, dtype=np.float32)
    emb = np.concatenate([t[:, None], np.cos(ang), np.sin(ang)], axis=-1).astype(np.float32)
    return jnp.asarray(np.pad(emb, ((0, 0), (0, HY_EMB_PAD - HY_EMB))))


def _post_kernel(*refs, split_x, split_out):
    oc_ref, ol_ref = refs[0:2]
    if split_x:
        x = _pick(refs[2], refs[3])
        refs = refs[4:]
    else:
        x = refs[2][...]
        refs = refs[3:]
    mod_ref, wo_ref, g1_ref, b1_ref, w1_ref, w2_ref, g2_ref, b2_ref = refs[0:8]
    outs = refs[8:]
    a = _dot(_pick(oc_ref, ol_ref), wo_ref[...])
    x = _layer_norm(DN_ALPHA * x + mod_ref[2:3, :] * a, g1_ref[...], b1_ref[...])
    h = _modulate(x, mod_ref, 3, 4)
    acc = None
    fc = 1024
    for c in range(D_FF // fc):
        a = jnp.maximum(_dot(h, w1_ref[:, c * fc:(c + 1) * fc]), 0.0)
        part = _dot((a * a).astype(BF), w2_ref[c * fc:(c + 1) * fc, :])
        acc = part if acc is None else acc + part
    y = _layer_norm(DN_ALPHA * x + mod_ref[5:6, :] * acc, g2_ref[...], b2_ref[...])
    if split_out:
        yc_ref, yl_ref = outs

        @pl.when(jnp.logical_not(_is_lat()))
        def _():
            yc_ref[...] = y

        @pl.when(_is_lat())
        def _():
            yl_ref[...] = y
    else:
        outs[0][...] = y


def _post(o_ctx, o_lat, xs, mods, layer, w_o, g1, b1, w1, w2, g2, b2, split_out):
    split_x = len(xs) == 2
    x_specs = [_ctx_spec(D), _lat_spec(D)] if split_x else [_tok_spec(D)]
    vec = _const_spec((1, D))
    if split_out:
        out_specs = [_ctx_spec(D), _lat_spec(D)]
        out_shape = [jax.ShapeDtypeStruct((T_CTX, D), F32), jax.ShapeDtypeStruct((T_LAT, D), F32)]
    else:
        out_specs = _tok_spec(D)
        out_shape = jax.ShapeDtypeStruct((T, D), F32)
    return pl.pallas_call(
        functools.partial(_post_kernel, split_x=split_x, split_out=split_out),
        grid=(N_TILES,),
        in_specs=[_ctx_spec(D), _lat_spec(D)] + x_specs + [
            _mod_spec(layer), _const_spec((D, D)), vec, vec, _const_spec((D, D_FF)), _const_spec((D_FF, D)), vec, vec],
        out_specs=out_specs,
        out_shape=out_shape,
        compiler_params=_cparams(1),
        name=f"post_l{layer}",
    )(o_ctx, o_lat, *xs, mods, w_o, g1, b1, w1, w2, g2, b2)


def _rope_tables():
    n = HEAD_DIM // 4
    pos = np.arange(DEC_SEQ)
    inv = (np.float32(ROPE_BASE) ** (-np.arange(n, dtype=np.float32) / np.float32(n))).astype(np.float32)
    ang_r = jnp.asarray(((pos // GRID_W).astype(np.float32)[:, None] * inv).astype(np.float32))
    ang_c = jnp.asarray(((pos % GRID_W).astype(np.float32)[:, None] * inv).astype(np.float32))
    cr, sr, cc, sc = jnp.cos(ang_r), jnp.sin(ang_r), jnp.cos(ang_c), jnp.sin(ang_c)
    a = jnp.concatenate([cr, cr, cc, cc], axis=-1)
    b = jnp.concatenate([-sr, sr, -sc, sc], axis=-1)
    return jnp.tile(a, (1, D // HEAD_DIM)), jnp.tile(b, (1, D // HEAD_DIM))


def kernel(x_prompt, x_sample, c, cache_da_k, cache_da_v, cache_na_k, cache_na_v, cache_gq_k, cache_gq_v, c_ctx, ada_w, ada_b, ln_g, ln_b, mlp_w1, mlp_w2, da_w_qkv, da_w_o, da_lambda, da_subln_g, na_w_qkv, na_w_o, na_rel_bias, gq_w_qkv, gq_w_o, gq_q_norm, gq_k_norm, hy_w_in, hy_short_w, hy_short_b, hy_ffn_w1, hy_ffn_b1, hy_ffn_w2, hy_ffn_b2, hy_ffn_freq, hy_ffn_w3, hy_log_decay, hy_filter_bias, hy_w_o):
    cvec = jnp.concatenate([c_ctx[None, :], c, jnp.zeros((MOD_ROWS - 1 - DEC_BATCH, D), F32)], axis=0)
    mods = _mods(cvec, ada_w, ada_b)
    rope_a, rope_b = _rope_tables()

    def finish(o_ctx, o_lat, xs, layer, w_o, split_out=False):
        return _post(o_ctx, o_lat, xs, mods, layer, w_o.astype(BF), ln_g[layer, 0][None], ln_b[layer, 0][None],
                     mlp_w1[layer].astype(BF), mlp_w2[layer].astype(BF), ln_g[layer, 1][None], ln_b[layer, 1][None],
                     split_out)

    xs = (x_prompt.reshape(T_CTX, D), x_sample.reshape(T_LAT, D))
    qb, kb, vb, ks, vs = _da_proj(*xs, mods, 0, da_w_qkv[0].astype(BF), rope_a, rope_b)
    state_da_k = ks.reshape(BATCH, 1, SEQ, DA_HEADS, 2 * HEAD_DIM)
    state_da_v = vs.reshape(BATCH, 1, SEQ, DA_HEADS, 2 * HEAD_DIM)
    o_ctx, o_lat = _da_attention(qb, kb, vb, cache_da_k.reshape(DEC_BATCH, PAST, D),
                                 cache_da_v.reshape(DEC_BATCH, PAST, D), da_lambda[0], da_subln_g[0][None], 0)
    x = finish(o_ctx, o_lat, xs, 0, da_w_o[0])

    qb, kb, vb, ks, vs = _na_proj(x, mods, 1, na_w_qkv[0].astype(BF))
    state_na_k = ks.reshape(BATCH, 1, SEQ, NA_HEADS, HEAD_DIM)
    state_na_v = vs.reshape(BATCH, 1, SEQ, NA_HEADS, HEAD_DIM)
    onehot, neg, mask = _na_constants()
    bias_tab = _na_bias_table(na_rel_bias[0], onehot, neg)
    o_ctx = _na_ctx_attention(qb, kb, vb)
    o_lat = _na_lat_attention(qb, kb, vb, cache_na_k.reshape(DEC_BATCH, PAST, D),
                              cache_na_v.reshape(DEC_BATCH, PAST, D), bias_tab, mask)
    x = finish(o_ctx, o_lat, (x,), 1, na_w_o[0])

    nk = GQ_KV_HEADS * HEAD_DIM
    g_mat = jnp.asarray(np.kron(np.eye(GQ_HEADS), np.full((HEAD_DIM, HEAD_DIM), 1.0 / HEAD_DIM)), BF)
    qb, kb, vb, ks, vs = _gq_proj(x, mods, 2, gq_w_qkv[0].astype(BF), g_mat,
                                  jnp.tile(gq_q_norm[0], GQ_HEADS)[None], jnp.tile(gq_k_norm[0], GQ_KV_HEADS)[None],
                                  rope_a, rope_b)
    state_gq_k = ks.reshape(BATCH, 1, SEQ, GQ_KV_HEADS, HEAD_DIM)
    state_gq_v = vs.reshape(BATCH, 1, SEQ, GQ_KV_HEADS, HEAD_DIM)
    o_ctx, o_lat = _gq_attention(qb, kb, vb, cache_gq_k.reshape(DEC_BATCH, PAST, nk),
                                 cache_gq_v.reshape(DEC_BATCH, PAST, nk))
    x = finish(o_ctx, o_lat, (x,), 2, gq_w_o[0])

    u = _hy_proj(x, mods, 3, hy_w_in[0].astype(BF))
    w1 = jnp.pad(hy_ffn_w1[0], ((0, HY_EMB_PAD - HY_EMB), (0, 0)))
    zs = []
    for seq, nbatch, row0, dc in ((SEQ, BATCH, 0, D), (DEC_SEQ, DEC_BATCH, T_CTX, 256)):
        cmat, smat = _dft_tables(seq)
        hre, him, hny = _hy_filter(seq, _hy_embedding(seq), w1, hy_ffn_b1[0][None], hy_ffn_w2[0], hy_ffn_b2[0][None],
                                   hy_ffn_freq[0][None], hy_ffn_w3[0], hy_log_decay[0][None], cmat, smat)
        zs.append(_hy_conv(u, hy_short_w[0], hy_short_b[0][None], hy_filter_bias[0], hre, him, hny, cmat, smat,
                           seq, nbatch, row0, dc))
    y_ctx, y_lat = finish(zs[0], zs[1], (x,), 3, hy_w_o[0], split_out=True)

    return (y_ctx.reshape(BATCH, SEQ, D), y_lat.reshape(DEC_BATCH, DEC_SEQ, D),
            state_da_k, state_da_v, state_na_k, state_na_v, state_gq_k, state_gq_v)
```

```python
import functools
import math

import numpy as np
import jax
import jax.numpy as jnp
from jax import lax
from jax.experimental import pallas as pl
from jax.experimental.pallas import tpu as pltpu

F32 = jnp.float32
BF = jnp.bfloat16

D = 1024
BATCH = 16
SEQ = 256
DEC_BATCH = 8
DEC_SEQ = 1024
PAST = 256
DEPTH = 4
GRID_W = 64
GRID_ROWS = DEC_SEQ // GRID_W
D_FF = 4 * D
T_CTX = BATCH * SEQ
T_LAT = DEC_BATCH * DEC_SEQ
T = T_CTX + T_LAT
HEAD_DIM = 64
ATT_SCALE = HEAD_DIM ** -0.5
LOG2E = math.log2(math.e)
Q_SCALE = ATT_SCALE * LOG2E
DA_HEADS = 8
NA_HEADS = 16
NA_WIN_ROWS = 8
NA_WIN_COLS = 16
GQ_HEADS = 16
GQ_KV_HEADS = 4
HY_ORDER = 2
HY_BANDS = 16
HY_EMB = 1 + 2 * HY_BANDS
HY_EMB_PAD = 40
HY_FFN = 64
ROPE_BASE = 10000.0
LN_EPS = 1e-5
RMS_EPS = 1e-6
DN_ALPHA = (2 * DEPTH) ** 0.25
NEG_INF = -1e30

LANES = 128
TM = 512
N_CTX_TILES = T_CTX // TM
N_TILES = T // TM
TQ = 256
MOD_ROWS = 16
VMEM_LIMIT = 56 * 1024 * 1024


def _cparams(n_axes):
    return pltpu.CompilerParams(dimension_semantics=("arbitrary",) * n_axes,
                                vmem_limit_bytes=VMEM_LIMIT)


def _dot(a, b):
    return jnp.dot(a, b, preferred_element_type=F32)


def _dot_nt(a, b):
    return lax.dot_general(a, b, (((1,), (1,)), ((), ())), preferred_element_type=F32)


def _const_spec(shape):
    nd = len(shape)
    return pl.BlockSpec(shape, lambda *_: (0,) * nd, pipeline_mode=pl.Buffered(1))


def _mod_row(i):
    return jnp.where(i < N_CTX_TILES, 0, 1 + (i - N_CTX_TILES) // (DEC_SEQ // TM))


def _mod_spec(layer):
    return pl.BlockSpec((None, None, 6, D), lambda i: (layer, _mod_row(i), 0, 0))


def _tok_spec(width):
    return pl.BlockSpec((TM, width), lambda i: (i, 0))


def _ctx_spec(width):
    return pl.BlockSpec((TM, width), lambda i: (jnp.minimum(i, N_CTX_TILES - 1), 0))


def _lat_spec(width):
    return pl.BlockSpec((TM, width), lambda i: (jnp.maximum(i - N_CTX_TILES, 0), 0))


def _is_lat():
    return pl.program_id(0) >= N_CTX_TILES


def _pick(ctx_ref, lat_ref):
    return jnp.where(_is_lat(), lat_ref[...], ctx_ref[...])


def _layer_norm(r, g, b):
    mu = jnp.mean(r, axis=-1, keepdims=True)
    c = r - mu
    var = jnp.mean(c * c, axis=-1, keepdims=True)
    return c * lax.rsqrt(var + LN_EPS) * g + b


def _mods_kernel(c_ref, w_ref, b_ref, o_ref):
    c = c_ref[...]
    s = (c / (1.0 + jnp.exp(-c))).astype(BF)
    o_ref[...] = _dot(s, w_ref[...].astype(BF)) + b_ref[...]


def _mods(cvec, ada_w, ada_b):
    tn = 1536
    out = pl.pallas_call(
        _mods_kernel,
        grid=(DEPTH, 6 * D // tn),
        in_specs=[pl.BlockSpec((MOD_ROWS, D), lambda l, n: (0, 0)),
                  pl.BlockSpec((None, D, tn), lambda l, n: (l, 0, n)),
                  pl.BlockSpec((None, 1, tn), lambda l, n: (l, 0, n))],
        out_specs=pl.BlockSpec((None, MOD_ROWS, tn), lambda l, n: (l, 0, n)),
        out_shape=jax.ShapeDtypeStruct((DEPTH, MOD_ROWS, 6 * D), F32),
        compiler_params=_cparams(2),
        name="adaln_mods",
    )(cvec, ada_w, ada_b.reshape(DEPTH, 1, 6 * D))
    return out.reshape(DEPTH, MOD_ROWS, 6, D)


def _modulate(x, mod_ref, shift, scale):
    return (x * (1.0 + mod_ref[scale:scale + 1, :]) + mod_ref[shift:shift + 1, :]).astype(BF)


def _rope(x, a, b):
    n = x.shape[1]
    lane = lax.broadcasted_iota(jnp.int32, x.shape, 1)
    partner = jnp.where((lane & 16) == 0, pltpu.roll(x, n - 16, 1), pltpu.roll(x, 16, 1))
    return x * a + partner * b


def _rope_spec(width):
    per = DEC_SEQ // TM
    return pl.BlockSpec((TM, width), lambda i: (jnp.where(i < N_CTX_TILES, per, (i - N_CTX_TILES) % per), 0))


def _store_state(k, v, ks_ref, vs_ref):
    @pl.when(jnp.logical_not(_is_lat()))
    def _():
        ks_ref[...] = k
        vs_ref[...] = v


def _qkv_out(nq, nk):
    specs = [_tok_spec(nq), _tok_spec(nk), _tok_spec(nk), _ctx_spec(nk), _ctx_spec(nk)]
    shapes = [jax.ShapeDtypeStruct((T, nq), BF), jax.ShapeDtypeStruct((T, nk), BF), jax.ShapeDtypeStruct((T, nk), BF),
              jax.ShapeDtypeStruct((T_CTX, nk), F32), jax.ShapeDtypeStruct((T_CTX, nk), F32)]
    return specs, shapes


def _da_proj_kernel(xc_ref, xl_ref, mod_ref, w_ref, ra_ref, rb_ref, qb_ref, kb_ref, vb_ref, ks_ref, vs_ref):
    h = _modulate(_pick(xc_ref, xl_ref), mod_ref, 0, 1)
    a, b = ra_ref[...], rb_ref[...]
    q = _dot(h, w_ref[:, 0:D])
    k = _dot(h, w_ref[:, D:2 * D])
    qb_ref[...] = (_rope(q, a, b) * Q_SCALE).astype(BF)
    v = _dot(h, w_ref[:, 2 * D:3 * D])
    kb_ref[...] = _rope(k, a, b).astype(BF)
    vb_ref[...] = v.astype(BF)
    _store_state(k, v, ks_ref, vs_ref)


def _da_proj(x_ctx, x_lat, mods, layer, w, rope_a, rope_b):
    specs, shapes = _qkv_out(D, D)
    return pl.pallas_call(
        _da_proj_kernel,
        grid=(N_TILES,),
        in_specs=[_ctx_spec(D), _lat_spec(D), _mod_spec(layer), _const_spec((D, 3 * D)),
                  _rope_spec(D), _rope_spec(D)],
        out_specs=specs, out_shape=shapes,
        compiler_params=_cparams(1),
        name=f"da_proj_l{layer}",
    )(x_ctx, x_lat, mods, w, rope_a, rope_b)


def _na_proj_kernel(x_ref, mod_ref, w_ref, qb_ref, kb_ref, vb_ref, ks_ref, vs_ref):
    h = _modulate(x_ref[...], mod_ref, 0, 1)
    q = _dot(h, w_ref[:, 0:D])
    k = _dot(h, w_ref[:, D:2 * D])
    qb_ref[...] = (q * Q_SCALE).astype(BF)
    v = _dot(h, w_ref[:, 2 * D:3 * D])
    kb_ref[...] = k.astype(BF)
    vb_ref[...] = v.astype(BF)
    _store_state(k, v, ks_ref, vs_ref)


def _na_proj(x, mods, layer, w):
    specs, shapes = _qkv_out(D, D)
    return pl.pallas_call(
        _na_proj_kernel,
        grid=(N_TILES,),
        in_specs=[_tok_spec(D), _mod_spec(layer), _const_spec((D, 3 * D))],
        out_specs=specs, out_shape=shapes,
        compiler_params=_cparams(1),
        name=f"na_proj_l{layer}",
    )(x, mods, w)


GN_BLOCK = 256


def _head_rms(x, g_ref, gain):
    x2 = x * x
    hi = x2.astype(BF)
    lo = (x2 - hi.astype(F32)).astype(BF)
    g = g_ref[...]
    ms = jnp.concatenate(
        [_dot(hi[:, j:j + GN_BLOCK], g) + _dot(lo[:, j:j + GN_BLOCK], g) for j in range(0, x.shape[1], GN_BLOCK)],
        axis=1)
    return x * lax.rsqrt(ms + RMS_EPS) * gain


def _gq_proj_kernel(x_ref, mod_ref, w_ref, g_ref, qn_ref, kn_ref, ra_ref, rb_ref,
                    qb_ref, kb_ref, vb_ref, ks_ref, vs_ref):
    nq, nk = GQ_HEADS * HEAD_DIM, GQ_KV_HEADS * HEAD_DIM
    h = _modulate(x_ref[...], mod_ref, 0, 1)
    a, b = ra_ref[...], rb_ref[...]
    q = _dot(h, w_ref[:, 0:nq])
    k = _dot(h, w_ref[:, nq:nq + nk])
    v = _dot(h, w_ref[:, nq + nk:nq + 2 * nk])
    k = _head_rms(k, g_ref, kn_ref[...])
    q = _head_rms(q, g_ref, qn_ref[...])
    kb_ref[...] = _rope(k, a[:, 0:nk], b[:, 0:nk]).astype(BF)
    qb_ref[...] = (_rope(q, a, b) * Q_SCALE).astype(BF)
    vb_ref[...] = v.astype(BF)
    _store_state(k, v, ks_ref, vs_ref)


def _gq_proj(x, mods, layer, w, g_mat, qn, kn, rope_a, rope_b):
    nq, nk = GQ_HEADS * HEAD_DIM, GQ_KV_HEADS * HEAD_DIM
    specs, shapes = _qkv_out(nq, nk)
    return pl.pallas_call(
        _gq_proj_kernel,
        grid=(N_TILES,),
        in_specs=[_tok_spec(D), _mod_spec(layer), _const_spec((D, nq + 2 * nk)),
                  _const_spec((GN_BLOCK, GN_BLOCK)), _const_spec((1, nq)), _const_spec((1, nk)),
                  _rope_spec(D), _rope_spec(D)],
        out_specs=specs, out_shape=shapes,
        compiler_params=_cparams(1),
        name=f"gq_proj_l{layer}",
    )(x, mods, w, g_mat, qn, kn, rope_a, rope_b)


def _hy_proj_kernel(x_ref, mod_ref, w_ref, u_ref):
    h = _modulate(x_ref[...], mod_ref, 0, 1)
    for c in range(HY_ORDER + 1):
        u_ref[:, c * D:(c + 1) * D] = _dot(h, w_ref[:, c * D:(c + 1) * D])


def _hy_proj(x, mods, layer, w):
    n = (HY_ORDER + 1) * D
    return pl.pallas_call(
        _hy_proj_kernel,
        grid=(N_TILES,),
        in_specs=[_tok_spec(D), _mod_spec(layer), _const_spec((D, n))],
        out_specs=_tok_spec(n),
        out_shape=jax.ShapeDtypeStruct((T, n), F32),
        compiler_params=_cparams(1),
        name=f"hy_proj_l{layer}",
    )(x, mods, w)


def _softmax_pv(qm, segs):
    return _softmax_finish(_scores(qm, segs), segs)


def _scores(qm, segs):
    return [_dot_nt(qm, k) for k, _ in segs]


def _softmax_finish(scores, segs):
    m = scores[0].max(axis=-1, keepdims=True)
    for s in scores[1:]:
        m = jnp.maximum(m, s.max(axis=-1, keepdims=True))
    den = None
    out = None
    for s, (_, v) in zip(scores, segs):
        e = jnp.exp2(s - m)
        d = e.sum(axis=-1, keepdims=True)
        o = _dot(e.astype(BF), v)
        den = d if den is None else den + d
        out = o if out is None else out + o
    return out / den


def _pipelined(jobs, score_fn, finish_fn):
    nxt = score_fn(jobs[0])
    for n, job in enumerate(jobs):
        cur, nxt = nxt, (score_fn(jobs[n + 1]) if n + 1 < len(jobs) else None)
        finish_fn(job, cur)


def _lane_half(shape):
    return lax.broadcasted_iota(jnp.int32, shape, 1) // HEAD_DIM


def _half_keep(half):
    return tuple(jnp.where(half == a, 1.0, 0.0).astype(BF) for a in (0, 1))


def _da_attn_kernel(*refs, has_cache, lam_init):
    if has_cache:
        q_ref, k_ref, v_ref, ck_ref, cv_ref, lam_ref, g_ref, o_ref = refs
    else:
        q_ref, k_ref, v_ref, lam_ref, g_ref, o_ref = refs
    lp = lam_ref[...]
    lam = (jnp.exp(jnp.sum(lp[0:1] * lp[1:2], axis=-1, keepdims=True))
           - jnp.exp(jnp.sum(lp[2:3] * lp[3:4], axis=-1, keepdims=True)) + lam_init)
    gain = g_ref[...] * (1.0 - lam_init)
    w = 2 * HEAD_DIM
    keep = _half_keep(_lane_half((TQ, w)))
    segs = []
    for hd in range(k_ref.shape[1] // w):
        cols = slice(hd * w, (hd + 1) * w)
        seg = [(k_ref[:, cols], v_ref[:, cols])]
        if has_cache:
            seg.append((ck_ref[:, cols].astype(BF), cv_ref[:, cols].astype(BF)))
        segs.append(seg)
    jobs = [(hd, t, a) for hd in range(len(segs)) for t in range(q_ref.shape[0] // TQ) for a in (0, 1)]
    first = {}

    def score_fn(job):
        hd, t, a = job
        return _scores(q_ref[t * TQ:(t + 1) * TQ, hd * w:(hd + 1) * w] * keep[a], segs[hd])

    def finish_fn(job, scores):
        hd, t, a = job
        o = _softmax_finish(scores, segs[hd])
        if a == 0:
            first[0] = o
            return
        o = first[0] - lam * o
        ms = jnp.mean(o * o, axis=-1, keepdims=True)
        o_ref[t * TQ:(t + 1) * TQ, hd * w:(hd + 1) * w] = (o * lax.rsqrt(ms + RMS_EPS) * gain).astype(BF)

    _pipelined(jobs, score_fn, finish_fn)


def _da_attention(qb, kb, vb, cache_k, cache_v, lam_p, subln_g, layer_idx):
    lam_init = 0.8 - 0.6 * math.exp(-0.3 * layer_idx)
    w = 2 * HEAD_DIM
    small = [pl.BlockSpec((4, HEAD_DIM), lambda *_: (0, 0)), pl.BlockSpec((1, w), lambda *_: (0, 0))]
    o_ctx = pl.pallas_call(
        functools.partial(_da_attn_kernel, has_cache=False, lam_init=lam_init),
        grid=(BATCH,),
        in_specs=[pl.BlockSpec((SEQ, D), lambda b: (b, 0))] * 3 + small,
        out_specs=pl.BlockSpec((SEQ, D), lambda b: (b, 0)),
        out_shape=jax.ShapeDtypeStruct((T_CTX, D), BF),
        compiler_params=_cparams(1),
        name="da_attn_ctx",
    )(qb, kb, vb, lam_p, subln_g)
    k0 = T_CTX // DEC_SEQ
    tok = pl.BlockSpec((DEC_SEQ, w), lambda b, h: (k0 + b, h))
    c_spec = pl.BlockSpec((None, PAST, w), lambda b, h: (b, 0, h))
    o_lat = pl.pallas_call(
        functools.partial(_da_attn_kernel, has_cache=True, lam_init=lam_init),
        grid=(DEC_BATCH, DA_HEADS),
        in_specs=[tok, tok, tok, c_spec, c_spec] + small,
        out_specs=pl.BlockSpec((DEC_SEQ, w), lambda b, h: (b, h)),
        out_shape=jax.ShapeDtypeStruct((T_LAT, D), BF),
        compiler_params=_cparams(2),
        name="da_attn_lat",
    )(qb, kb, vb, cache_k, cache_v, lam_p, subln_g)
    return o_ctx, o_lat


def _na_ctx_kernel(q_ref, k_ref, v_ref, o_ref):
    half = _lane_half((SEQ, LANES))
    keep = _half_keep(half)
    jobs = [(p, a) for p in range(NA_HEADS // 2) for a in (0, 1)]
    first = {}

    def seg(p):
        return [(k_ref[:, p * LANES:(p + 1) * LANES], v_ref[:, p * LANES:(p + 1) * LANES])]

    def score_fn(job):
        p, a = job
        return _scores(q_ref[:, p * LANES:(p + 1) * LANES] * keep[a], seg(p))

    def finish_fn(job, scores):
        p, a = job
        o = _softmax_finish(scores, seg(p))
        if a == 0:
            first[0] = o
        else:
            o_ref[:, p * LANES:(p + 1) * LANES] = jnp.where(half == 0, first[0], o).astype(BF)

    _pipelined(jobs, score_fn, finish_fn)


def _na_ctx_attention(qb, kb, vb):
    spec = pl.BlockSpec((SEQ, D), lambda b: (b, 0))
    return pl.pallas_call(
        _na_ctx_kernel,
        grid=(BATCH,),
        in_specs=[spec] * 3,
        out_specs=spec,
        out_shape=jax.ShapeDtypeStruct((T_CTX, D), BF),
        compiler_params=_cparams(1),
        name="na_attn_ctx",
    )(qb, kb, vb)


NA_TILES = ((0, (0, 2, 4, 6)), (4, (0, 2, 4, 6, 8, 10)), (8, (4, 6, 8, 10, 12, 14)), (12, (8, 10, 12, 14)))
NA_MAX_CHUNKS = 6
NA_BIAS_BLOCKS = 2 * NA_WIN_ROWS - 2


def _na_lat_kernel(q_ref, k_ref, v_ref, ck_ref, cv_ref, w_ref, m_ref, o_ref):
    ckb, cvb = ck_ref[...].astype(BF), cv_ref[...].astype(BF)
    rows = 4 * GRID_W
    half = _lane_half((rows, LANES))
    keep = _half_keep(half)
    jobs = [(i, a) for i in range(len(NA_TILES)) for a in (0, 1)]
    first = {}

    def key_rows(i):
        chunks = NA_TILES[i][1]
        return slice(chunks[0] * GRID_W, chunks[0] * GRID_W + len(chunks) * LANES)

    def score_fn(job):
        i, a = job
        r0, chunks = NA_TILES[i]
        qm = q_ref[i * rows:(i + 1) * rows, :] * keep[a]
        bias = jnp.concatenate(
            [w_ref[a, (6 - kr + r0) * GRID_W:(6 - kr + r0) * GRID_W + rows, :] for kr in chunks], axis=1)
        s_loc = _dot_nt(qm, k_ref[key_rows(i), :]) + bias + m_ref[i, :, 0:len(chunks) * LANES]
        return [s_loc, _dot_nt(qm, ckb)]

    def finish_fn(job, scores):
        i, a = job
        o = _softmax_finish(scores, [(None, v_ref[key_rows(i), :]), (None, cvb)])
        if a == 0:
            first[0] = o
        else:
            o_ref[i * rows:(i + 1) * rows, :] = jnp.where(half == 0, first[0], o).astype(BF)

    _pipelined(jobs, score_fn, finish_fn)


def _na_lat_attention(qb, kb, vb, cache_k, cache_v, bias_tab, mask_tab):
    k0 = T_CTX // DEC_SEQ
    tok = pl.BlockSpec((DEC_SEQ, LANES), lambda p, b: (k0 + b, p))
    c_spec = pl.BlockSpec((None, PAST, LANES), lambda p, b: (b, 0, p))
    return pl.pallas_call(
        _na_lat_kernel,
        grid=(NA_HEADS // 2, DEC_BATCH),
        in_specs=[tok, tok, tok, c_spec, c_spec,
                  pl.BlockSpec((None, 2, NA_BIAS_BLOCKS * GRID_W, LANES), lambda p, b: (p, 0, 0, 0)),
                  _const_spec(mask_tab.shape)],
        out_specs=pl.BlockSpec((DEC_SEQ, LANES), lambda p, b: (b, p)),
        out_shape=jax.ShapeDtypeStruct((T_LAT, D), BF),
        compiler_params=_cparams(2),
        name="na_attn_lat",
    )(qb, kb, vb, cache_k, cache_v, bias_tab, mask_tab)


def _na_bias_kernel(t_ref, r_ref, n_ref, o_ref):
    t = t_ref[...]
    t1 = t.astype(BF)
    r1 = t - t1.astype(F32)
    t2 = r1.astype(BF)
    t3 = (r1 - t2.astype(F32)).astype(BF)
    r = r_ref[...]
    o_ref[...] = (_dot(t1, r) + _dot(t2, r) + _dot(t3, r) + n_ref[...]) * LOG2E


def _na_bias_table(rel_bias, onehot, neg):
    nrel = 2 * NA_WIN_COLS
    idx = 13 - np.arange(NA_BIAS_BLOCKS)[:, None] + np.arange(2)[None, :]
    t = jnp.pad(rel_bias[:, idx, :], ((0, 0), (0, 0), (0, 0), (0, 1)))
    t = t.reshape(NA_HEADS * NA_BIAS_BLOCKS, 2 * nrel)
    n = GRID_W * LANES
    tn = 2048
    out = pl.pallas_call(
        _na_bias_kernel,
        grid=(n // tn,),
        in_specs=[pl.BlockSpec(t.shape, lambda j: (0, 0)),
                  pl.BlockSpec((2 * nrel, tn), lambda j: (0, j)),
                  pl.BlockSpec((1, tn), lambda j: (0, j))],
        out_specs=pl.BlockSpec((t.shape[0], tn), lambda j: (0, j)),
        out_shape=jax.ShapeDtypeStruct((t.shape[0], n), F32),
        compiler_params=_cparams(1),
        name="na_bias_table",
    )(t, onehot, neg)
    return out.reshape(NA_HEADS // 2, 2, NA_BIAS_BLOCKS * GRID_W, LANES)


def _na_constants():
    nrel = 2 * NA_WIN_COLS
    qc = np.arange(GRID_W)[:, None]
    kc = np.arange(GRID_W)[None, :]
    rel = np.clip(kc - qc, -(NA_WIN_COLS - 1), NA_WIN_COLS - 1) + NA_WIN_COLS - 1
    cs = np.clip(qc - NA_WIN_COLS // 2, 0, GRID_W - NA_WIN_COLS)
    col_in = (kc >= cs) & (kc < cs + NA_WIN_COLS)
    onehot = np.zeros((2, nrel, GRID_W, 2, GRID_W), np.float32)
    for hf in range(2):
        onehot[hf, rel, qc, hf, kc] = 1.0
    neg = np.where(col_in, 0.0, NEG_INF).astype(np.float32)
    neg = np.broadcast_to(neg[:, None, :], (GRID_W, 2, GRID_W)).reshape(1, -1)
    rows = 4 * GRID_W
    mask = np.full((len(NA_TILES), rows, NA_MAX_CHUNKS * LANES), NEG_INF, np.float32)
    kr = min(NA_WIN_ROWS, GRID_ROWS)
    for i, (r0, chunks) in enumerate(NA_TILES):
        qr = r0 + np.arange(rows)[:, None] // GRID_W
        rs = np.clip(qr - kr // 2, 0, GRID_ROWS - kr)
        for c, krow0 in enumerate(chunks):
            krow = krow0 + np.arange(LANES)[None, :] // GRID_W
            mask[i, :, c * LANES:(c + 1) * LANES] = np.where((krow >= rs) & (krow < rs + kr), 0.0, NEG_INF)
    return (jnp.asarray(onehot.reshape(2 * nrel, GRID_W * LANES), BF), jnp.asarray(neg), jnp.asarray(mask))


def _gq_attn_kernel(*refs, has_cache):
    if has_cache:
        q_ref, k_ref, v_ref, ck_ref, cv_ref, o_ref = refs
    else:
        q_ref, k_ref, v_ref, o_ref = refs
    group = GQ_HEADS // GQ_KV_HEADS
    qw = LANES * group
    half = _lane_half((q_ref.shape[0], LANES))
    keep = _half_keep(half)
    for kvp in range(k_ref.shape[1] // LANES):
        kcols = slice(kvp * LANES, (kvp + 1) * LANES)
        segs = [(k_ref[:, kcols], v_ref[:, kcols])]
        if has_cache:
            segs.append((ck_ref[:, kcols].astype(BF), cv_ref[:, kcols].astype(BF)))
        def head_scores(j):
            pair, a = divmod(j, 2)
            q = q_ref[:, kvp * qw + pair * LANES:kvp * qw + (pair + 1) * LANES]
            qm = q * keep[a]
            if a != (2 * pair) // group:
                qm = pltpu.roll(qm.astype(F32), HEAD_DIM, 1).astype(BF)
            return _scores(qm, segs)

        nxt = head_scores(0)
        outs = []
        for j in range(2 * group):
            cur, nxt = nxt, (head_scores(j + 1) if j + 1 < 2 * group else None)
            pair, a = divmod(j, 2)
            o = _softmax_finish(cur, segs)
            if a != (2 * pair) // group:
                o = pltpu.roll(o, HEAD_DIM, 1)
            outs.append(o)
            if a == 1:
                cols = slice(kvp * qw + pair * LANES, kvp * qw + (pair + 1) * LANES)
                o_ref[:, cols] = jnp.where(half == 0, outs[0], outs[1]).astype(BF)
                outs = []


def _gq_attention(qb, kb, vb, cache_k, cache_v):
    nk = GQ_KV_HEADS * HEAD_DIM
    qw = LANES * (GQ_HEADS // GQ_KV_HEADS)
    npair = GQ_KV_HEADS // 2
    o_ctx = pl.pallas_call(
        functools.partial(_gq_attn_kernel, has_cache=False),
        grid=(BATCH,),
        in_specs=[pl.BlockSpec((SEQ, D), lambda b: (b, 0))] + [pl.BlockSpec((SEQ, nk), lambda b: (b, 0))] * 2,
        out_specs=pl.BlockSpec((SEQ, D), lambda b: (b, 0)),
        out_shape=jax.ShapeDtypeStruct((T_CTX, D), BF),
        compiler_params=_cparams(1),
        name="gq_attn_ctx",
    )(qb, kb, vb)
    qt = DEC_SEQ // TQ
    q0, k0 = T_CTX // TQ, T_CTX // DEC_SEQ
    kv_spec = pl.BlockSpec((DEC_SEQ, LANES), lambda b, p, t: (k0 + b, p))
    c_spec = pl.BlockSpec((None, PAST, LANES), lambda b, p, t: (b, 0, p))
    o_lat = pl.pallas_call(
        functools.partial(_gq_attn_kernel, has_cache=True),
        grid=(DEC_BATCH, npair, qt),
        in_specs=[pl.BlockSpec((TQ, qw), lambda b, p, t: (q0 + b * qt + t, p)), kv_spec, kv_spec, c_spec, c_spec],
        out_specs=pl.BlockSpec((TQ, qw), lambda b, p, t: (b * qt + t, p)),
        out_shape=jax.ShapeDtypeStruct((T_LAT, D), BF),
        compiler_params=_cparams(3),
        name="gq_attn_lat",
    )(qb, kb, vb, cache_k, cache_v)
    return o_ctx, o_lat


def _hy_filter_kernel(emb_ref, w1_ref, b1_ref, w2_ref, b2_ref, fr_ref, w3f_ref, w3b_ref, ldf_ref, ldb_ref,
                      c_ref, s_ref, hre_ref, him_ref, hny_ref):
    seq = emb_ref.shape[0]
    hp = lax.Precision.HIGHEST
    emb = emb_ref[...]
    fr = fr_ref[...]
    hid = jnp.sin(fr * (jnp.dot(emb, w1_ref[...], precision=hp, preferred_element_type=F32) + b1_ref[...]))
    hid = jnp.sin(fr * (jnp.dot(hid, w2_ref[...], precision=hp, preferred_element_type=F32) + b2_ref[...]))
    t = emb[:, 0:1]
    fwd = jnp.dot(hid, w3f_ref[...], precision=hp, preferred_element_type=F32) * jnp.exp(-jnp.exp(ldf_ref[...]) * t)
    bwd = jnp.dot(hid, w3b_ref[...], precision=hp, preferred_element_type=F32) * jnp.exp(-jnp.exp(ldb_ref[...]) * t)
    row = lax.broadcasted_iota(jnp.int32, fwd.shape, 0)
    bwd = jnp.where(row == 0, 0.0, bwd)
    even = fwd + bwd
    odd = bwd - fwd
    wk = jnp.where(row == 0, 0.5 / seq, 1.0 / seq)
    hre_ref[...] = _dot(c_ref[...].astype(BF), even.astype(BF)) * wk
    him_ref[...] = _dot(s_ref[...].astype(BF), odd.astype(BF)) * wk
    alt = jnp.where((row & 1) == 0, 1.0, -1.0)
    hny_ref[...] = jnp.sum(alt * even, axis=0, keepdims=True) * (0.5 / seq)


def _hy_filter(seq, emb, w1, b1, w2, b2, freq, w3, log_decay, cmat, smat):
    dc = 512
    nj = D // dc
    small = [_const_spec(a.shape) for a in (emb, w1, b1, w2, b2, freq)]
    return pl.pallas_call(
        _hy_filter_kernel,
        grid=(HY_ORDER, nj),
        in_specs=small + [pl.BlockSpec((HY_FFN, dc), lambda o, j: (0, (2 * o) * nj + j)),
                          pl.BlockSpec((HY_FFN, dc), lambda o, j: (0, (2 * o + 1) * nj + j)),
                          pl.BlockSpec((1, dc), lambda o, j: (0, (2 * o) * nj + j)),
                          pl.BlockSpec((1, dc), lambda o, j: (0, (2 * o + 1) * nj + j)),
                          _const_spec((seq, seq)), _const_spec((seq, seq))],
        out_specs=[pl.BlockSpec((None, seq, dc), lambda o, j: (o, 0, j)),
                   pl.BlockSpec((None, seq, dc), lambda o, j: (o, 0, j)),
                   pl.BlockSpec((None, 1, dc), lambda o, j: (o, 0, j))],
        out_shape=[jax.ShapeDtypeStruct((HY_ORDER, seq, D), F32), jax.ShapeDtypeStruct((HY_ORDER, seq, D), F32),
                   jax.ShapeDtypeStruct((HY_ORDER, 1, D), F32)],
        compiler_params=_cparams(2),
        name=f"hy_filter_{seq}",
    )(emb, w1, b1, w2, b2, freq, w3, w3, log_decay, log_decay, cmat, smat)


def _hy_conv_kernel(u0_ref, u1_ref, u2_ref, sw0_ref, sw1_ref, sw2_ref, sb0_ref, sb1_ref, sb2_ref,
                    fb_ref, hre_ref, him_ref, hny_ref, c_ref, s_ref, o_ref, cb_ref, sb_ref):
    seq = u0_ref.shape[0]

    @pl.when((pl.program_id(0) == 0) & (pl.program_id(1) == 0))
    def _():
        cb_ref[...] = c_ref[...].astype(BF)
        sb_ref[...] = s_ref[...].astype(BF)

    row = lax.broadcasted_iota(jnp.int32, u0_ref.shape, 0)
    alt = jnp.where((row & 1) == 0, 1.0, -1.0)

    def short_conv(u_ref, w_ref, b_ref):
        u = u_ref[...]
        prev = jnp.where(row == 0, 0.0, pltpu.roll(u, 1, 0))
        nxt = jnp.where(row == seq - 1, 0.0, pltpu.roll(u, seq - 1, 0))
        return prev * w_ref[0:1, :] + u * w_ref[1:2, :] + nxt * w_ref[2:3, :] + b_ref[...]

    cm, sm = cb_ref[...], sb_ref[...]
    z = short_conv(u0_ref, sw0_ref, sb0_ref)
    gates = (short_conv(u1_ref, sw1_ref, sb1_ref), short_conv(u2_ref, sw2_ref, sb2_ref))
    for o in range(HY_ORDER):
        zb = z.astype(BF)
        zc, zs = _dot(cm, zb), _dot(sm, zb)
        hre, him = hre_ref[o], him_ref[o]
        p_re = zc * hre + zs * him
        p_im = zc * him - zs * hre
        nyq = jnp.sum(alt * z, axis=0, keepdims=True) * hny_ref[o]
        y = _dot(cm, p_re.astype(BF)) - _dot(sm, p_im.astype(BF)) + alt * nyq
        z = gates[o] * (y + z * fb_ref[o:o + 1, :])
    o_ref[...] = z.astype(BF)


def _hy_conv(u, short_w, short_b, filter_bias, hre, him, hny, cmat, smat, seq, nbatch, row0, dc):
    nj = D // dc
    r0 = row0 // seq

    def part(p):
        return pl.BlockSpec((seq, dc), lambda j, b: (r0 + b, p * nj + j))

    def vec(rows, p):
        return pl.BlockSpec((rows, dc), lambda j, b: (0, p * nj + j))

    in_specs = ([part(p) for p in range(3)] + [vec(3, p) for p in range(3)] + [vec(1, p) for p in range(3)]
                + [pl.BlockSpec((HY_ORDER, dc), lambda j, b: (0, j)),
                   pl.BlockSpec((HY_ORDER, seq, dc), lambda j, b: (0, 0, j)),
                   pl.BlockSpec((HY_ORDER, seq, dc), lambda j, b: (0, 0, j)),
                   pl.BlockSpec((HY_ORDER, 1, dc), lambda j, b: (0, 0, j)),
                   _const_spec((seq, seq)), _const_spec((seq, seq))])
    return pl.pallas_call(
        _hy_conv_kernel,
        grid=(nj, nbatch),
        in_specs=in_specs,
        out_specs=pl.BlockSpec((seq, dc), lambda j, b: (b, j)),
        out_shape=jax.ShapeDtypeStruct((nbatch * seq, D), BF),
        scratch_shapes=[pltpu.VMEM((seq, seq), BF), pltpu.VMEM((seq, seq), BF)],
        compiler_params=_cparams(2),
        name=f"hy_conv_{seq}",
    )(u, u, u, short_w, short_w, short_w, short_b, short_b, short_b, filter_bias, hre, him, hny, cmat, smat)


def _dft_tables(seq):
    k = np.arange(seq, dtype=np.int64)
    ang = np.pi * ((k[:, None] * k[None, :]) % (2 * seq)) / seq
    return jnp.asarray(np.cos(ang), F32), jnp.asarray(np.sin(ang), F32)


def _hy_embedding(seq):
    t = np.arange(seq, dtype=np.float32) / np.float32(seq)
    ang = (2.0 * math.pi) * t[:, None] * np.arange(1, HY_BANDS + 1, dtype=np.float32)
    emb = np.concatenate([t[:, None], np.cos(ang), np.sin(ang)], axis=-1).astype(np.float32)
    return jnp.asarray(np.pad(emb, ((0, 0), (0, HY_EMB_PAD - HY_EMB))))


def _post_kernel(*refs, split_x, split_out):
    oc_ref, ol_ref = refs[0:2]
    if split_x:
        x = _pick(refs[2], refs[3])
        refs = refs[4:]
    else:
        x = refs[2][...]
        refs = refs[3:]
    mod_ref, wo_ref, g1_ref, b1_ref, w1_ref, w2_ref, g2_ref, b2_ref = refs[0:8]
    outs = refs[8:]
    a = _dot(_pick(oc_ref, ol_ref), wo_ref[...])
    x = _layer_norm(DN_ALPHA * x + mod_ref[2:3, :] * a, g1_ref[...], b1_ref[...])
    h = _modulate(x, mod_ref, 3, 4)
    acc = None
    fc = 1024
    for c in range(D_FF // fc):
        a = jnp.maximum(_dot(h, w1_ref[:, c * fc:(c + 1) * fc]), 0.0)
        part = _dot((a * a).astype(BF), w2_ref[c * fc:(c + 1) * fc, :])
        acc = part if acc is None else acc + part
    y = _layer_norm(DN_ALPHA * x + mod_ref[5:6, :] * acc, g2_ref[...], b2_ref[...])
    if split_out:
        yc_ref, yl_ref = outs

        @pl.when(jnp.logical_not(_is_lat()))
        def _():
            yc_ref[...] = y

        @pl.when(_is_lat())
        def _():
            yl_ref[...] = y
    else:
        outs[0][...] = y


def _post(o_ctx, o_lat, xs, mods, layer, w_o, g1, b1, w1, w2, g2, b2, split_out):
    split_x = len(xs) == 2
    x_specs = [_ctx_spec(D), _lat_spec(D)] if split_x else [_tok_spec(D)]
    vec = _const_spec((1, D))
    if split_out:
        out_specs = [_ctx_spec(D), _lat_spec(D)]
        out_shape = [jax.ShapeDtypeStruct((T_CTX, D), F32), jax.ShapeDtypeStruct((T_LAT, D), F32)]
    else:
        out_specs = _tok_spec(D)
        out_shape = jax.ShapeDtypeStruct((T, D), F32)
    return pl.pallas_call(
        functools.partial(_post_kernel, split_x=split_x, split_out=split_out),
        grid=(N_TILES,),
        in_specs=[_ctx_spec(D), _lat_spec(D)] + x_specs + [
            _mod_spec(layer), _const_spec((D, D)), vec, vec, _const_spec((D, D_FF)), _const_spec((D_FF, D)), vec, vec],
        out_specs=out_specs,
        out_shape=out_shape,
        compiler_params=_cparams(1),
        name=f"post_l{layer}",
    )(o_ctx, o_lat, *xs, mods, w_o, g1, b1, w1, w2, g2, b2)


def _rope_tables():
    n = HEAD_DIM // 4
    pos = np.arange(DEC_SEQ)
    inv = (np.float32(ROPE_BASE) ** (-np.arange(n, dtype=np.float32) / np.float32(n))).astype(np.float32)
    ang_r = jnp.asarray(((pos // GRID_W).astype(np.float32)[:, None] * inv).astype(np.float32))
    ang_c = jnp.asarray(((pos % GRID_W).astype(np.float32)[:, None] * inv).astype(np.float32))
    cr, sr, cc, sc = jnp.cos(ang_r), jnp.sin(ang_r), jnp.cos(ang_c), jnp.sin(ang_c)
    a = jnp.concatenate([cr, cr, cc, cc], axis=-1)
    b = jnp.concatenate([-sr, sr, -sc, sc], axis=-1)
    a, b = jnp.tile(a, (1, D // HEAD_DIM)), jnp.tile(b, (1, D // HEAD_DIM))
    return (jnp.concatenate([a, jnp.ones((TM, D), F32)], axis=0), jnp.concatenate([b, jnp.zeros((TM, D), F32)], axis=0))


def kernel(x_prompt, x_sample, c, cache_da_k, cache_da_v, cache_na_k, cache_na_v, cache_gq_k, cache_gq_v, c_ctx, ada_w, ada_b, ln_g, ln_b, mlp_w1, mlp_w2, da_w_qkv, da_w_o, da_lambda, da_subln_g, na_w_qkv, na_w_o, na_rel_bias, gq_w_qkv, gq_w_o, gq_q_norm, gq_k_norm, hy_w_in, hy_short_w, hy_short_b, hy_ffn_w1, hy_ffn_b1, hy_ffn_w2, hy_ffn_b2, hy_ffn_freq, hy_ffn_w3, hy_log_decay, hy_filter_bias, hy_w_o):
    cvec = jnp.concatenate([c_ctx[None, :], c, jnp.zeros((MOD_ROWS - 1 - DEC_BATCH, D), F32)], axis=0)
    mods = _mods(cvec, ada_w, ada_b)
    rope_a, rope_b = _rope_tables()

    def finish(o_ctx, o_lat, xs, layer, w_o, split_out=False):
        return _post(o_ctx, o_lat, xs, mods, layer, w_o.astype(BF), ln_g[layer, 0][None], ln_b[layer, 0][None],
                     mlp_w1[layer].astype(BF), mlp_w2[layer].astype(BF), ln_g[layer, 1][None], ln_b[layer, 1][None],
                     split_out)

    xs = (x_prompt.reshape(T_CTX, D), x_sample.reshape(T_LAT, D))
    qb, kb, vb, ks, vs = _da_proj(*xs, mods, 0, da_w_qkv[0].astype(BF), rope_a, rope_b)
    state_da_k = ks.reshape(BATCH, 1, SEQ, DA_HEADS, 2 * HEAD_DIM)
    state_da_v = vs.reshape(BATCH, 1, SEQ, DA_HEADS, 2 * HEAD_DIM)
    o_ctx, o_lat = _da_attention(qb, kb, vb, cache_da_k.reshape(DEC_BATCH, PAST, D),
                                 cache_da_v.reshape(DEC_BATCH, PAST, D), da_lambda[0], da_subln_g[0][None], 0)
    x = finish(o_ctx, o_lat, xs, 0, da_w_o[0])

    qb, kb, vb, ks, vs = _na_proj(x, mods, 1, na_w_qkv[0].astype(BF))
    state_na_k = ks.reshape(BATCH, 1, SEQ, NA_HEADS, HEAD_DIM)
    state_na_v = vs.reshape(BATCH, 1, SEQ, NA_HEADS, HEAD_DIM)
    onehot, neg, mask = _na_constants()
    bias_tab = _na_bias_table(na_rel_bias[0], onehot, neg)
    o_ctx = _na_ctx_attention(qb, kb, vb)
    o_lat = _na_lat_attention(qb, kb, vb, cache_na_k.reshape(DEC_BATCH, PAST, D),
                              cache_na_v.reshape(DEC_BATCH, PAST, D), bias_tab, mask)
    x = finish(o_ctx, o_lat, (x,), 1, na_w_o[0])

    nk = GQ_KV_HEADS * HEAD_DIM
    g_mat = jnp.asarray(np.kron(np.eye(GN_BLOCK // HEAD_DIM), np.full((HEAD_DIM, HEAD_DIM), 1.0 / HEAD_DIM)), BF)
    qb, kb, vb, ks, vs = _gq_proj(x, mods, 2, gq_w_qkv[0].astype(BF), g_mat,
                                  jnp.tile(gq_q_norm[0], GQ_HEADS)[None], jnp.tile(gq_k_norm[0], GQ_KV_HEADS)[None],
                                  rope_a, rope_b)
    state_gq_k = ks.reshape(BATCH, 1, SEQ, GQ_KV_HEADS, HEAD_DIM)
    state_gq_v = vs.reshape(BATCH, 1, SEQ, GQ_KV_HEADS, HEAD_DIM)
    o_ctx, o_lat = _gq_attention(qb, kb, vb, cache_gq_k.reshape(DEC_BATCH, PAST, nk),
                                 cache_gq_v.reshape(DEC_BATCH, PAST, nk))
    x = finish(o_ctx, o_lat, (x,), 2, gq_w_o[0])

    u = _hy_proj(x, mods, 3, hy_w_in[0].astype(BF))
    w1 = jnp.pad(hy_ffn_w1[0], ((0, HY_EMB_PAD - HY_EMB), (0, 0)))
    zs = []
    for seq, nbatch, row0, dc in ((SEQ, BATCH, 0, D), (DEC_SEQ, DEC_BATCH, T_CTX, 256)):
        cmat, smat = _dft_tables(seq)
        hre, him, hny = _hy_filter(seq, _hy_embedding(seq), w1, hy_ffn_b1[0][None], hy_ffn_w2[0], hy_ffn_b2[0][None],
                                   hy_ffn_freq[0][None], hy_ffn_w3[0], hy_log_decay[0][None], cmat, smat)
        zs.append(_hy_conv(u, hy_short_w[0], hy_short_b[0][None], hy_filter_bias[0], hre, him, hny, cmat, smat,
                           seq, nbatch, row0, dc))
    y_ctx, y_lat = finish(zs[0], zs[1], (x,), 3, hy_w_o[0], split_out=True)

    return (y_ctx.reshape(BATCH, SEQ, D), y_lat.reshape(DEC_BATCH, DEC_SEQ, D),
            state_da_k, state_da_v, state_na_k, state_na_v, state_gq_k, state_gq_v)
```

```python
import functools
import math

import numpy as np
import jax
import jax.numpy as jnp
from jax import lax
from jax.experimental import pallas as pl
from jax.experimental.pallas import tpu as pltpu

F32 = jnp.float32
BF = jnp.bfloat16

D = 1024
BATCH = 16
SEQ = 256
DEC_BATCH = 8
DEC_SEQ = 1024
PAST = 256
DEPTH = 4
GRID_W = 64
GRID_ROWS = DEC_SEQ // GRID_W
D_FF = 4 * D
T_CTX = BATCH * SEQ
T_LAT = DEC_BATCH * DEC_SEQ
T = T_CTX + T_LAT
HEAD_DIM = 64
ATT_SCALE = HEAD_DIM ** -0.5
LOG2E = math.log2(math.e)
Q_SCALE = ATT_SCALE * LOG2E
DA_HEADS = 8
NA_HEADS = 16
NA_WIN_ROWS = 8
NA_WIN_COLS = 16
GQ_HEADS = 16
GQ_KV_HEADS = 4
HY_ORDER = 2
HY_BANDS = 16
HY_EMB = 1 + 2 * HY_BANDS
HY_EMB_PAD = 40
HY_FFN = 64
ROPE_BASE = 10000.0
LN_EPS = 1e-5
RMS_EPS = 1e-6
DN_ALPHA = (2 * DEPTH) ** 0.25
NEG_INF = -1e30

LANES = 128
TM = 512
N_CTX_TILES = T_CTX // TM
N_TILES = T // TM
TQ = 256
MOD_ROWS = 16
VMEM_LIMIT = 56 * 1024 * 1024


def _cparams(n_axes, flags=None):
    return pltpu.CompilerParams(dimension_semantics=("arbitrary",) * n_axes,
                                vmem_limit_bytes=VMEM_LIMIT, flags=flags)


def _dot(a, b):
    return jnp.dot(a, b, preferred_element_type=F32)


def _dot_nt(a, b):
    return lax.dot_general(a, b, (((1,), (1,)), ((), ())), preferred_element_type=F32)


def _const_spec(shape):
    nd = len(shape)
    return pl.BlockSpec(shape, lambda *_: (0,) * nd, pipeline_mode=pl.Buffered(1))


def _mod_row(i):
    return jnp.where(i < N_CTX_TILES, 0, 1 + (i - N_CTX_TILES) // (DEC_SEQ // TM))


def _mod_spec(layer):
    return pl.BlockSpec((None, None, 6, D), lambda i: (layer, _mod_row(i), 0, 0))


def _tok_spec(width):
    return pl.BlockSpec((TM, width), lambda i: (i, 0))


def _ctx_spec(width):
    return pl.BlockSpec((TM, width), lambda i: (jnp.minimum(i, N_CTX_TILES - 1), 0))


def _lat_spec(width):
    return pl.BlockSpec((TM, width), lambda i: (jnp.maximum(i - N_CTX_TILES, 0), 0))


def _is_lat():
    return pl.program_id(0) >= N_CTX_TILES


def _pick(ctx_ref, lat_ref):
    return jnp.where(_is_lat(), lat_ref[...], ctx_ref[...])


def _layer_norm(r, g, b):
    mu = jnp.mean(r, axis=-1, keepdims=True)
    c = r - mu
    var = jnp.mean(c * c, axis=-1, keepdims=True)
    return c * lax.rsqrt(var + LN_EPS) * g + b


def _mods_kernel(c_ref, w_ref, b_ref, o_ref):
    c = c_ref[...]
    s = (c / (1.0 + jnp.exp(-c))).astype(BF)
    o_ref[...] = _dot(s, w_ref[...].astype(BF)) + b_ref[...]


def _mods(cvec, ada_w, ada_b):
    tn = 1536
    out = pl.pallas_call(
        _mods_kernel,
        grid=(DEPTH, 6 * D // tn),
        in_specs=[pl.BlockSpec((MOD_ROWS, D), lambda l, n: (0, 0)),
                  pl.BlockSpec((None, D, tn), lambda l, n: (l, 0, n)),
                  pl.BlockSpec((None, 1, tn), lambda l, n: (l, 0, n))],
        out_specs=pl.BlockSpec((None, MOD_ROWS, tn), lambda l, n: (l, 0, n)),
        out_shape=jax.ShapeDtypeStruct((DEPTH, MOD_ROWS, 6 * D), F32),
        compiler_params=_cparams(2),
        name="adaln_mods",
    )(cvec, ada_w, ada_b.reshape(DEPTH, 1, 6 * D))
    return out.reshape(DEPTH, MOD_ROWS, 6, D)


def _modulate(x, mod_ref, shift, scale):
    return (x * (1.0 + mod_ref[scale:scale + 1, :]) + mod_ref[shift:shift + 1, :]).astype(BF)


def _rope(x, a, b):
    n = x.shape[1]
    lane = lax.broadcasted_iota(jnp.int32, x.shape, 1)
    partner = jnp.where((lane & 16) == 0, pltpu.roll(x, n - 16, 1), pltpu.roll(x, 16, 1))
    return x * a + partner * b


def _rope_spec(width):
    per = DEC_SEQ // TM
    return pl.BlockSpec((TM, width), lambda i: (jnp.where(i < N_CTX_TILES, per, (i - N_CTX_TILES) % per), 0))


def _store_state(k, v, ks_ref, vs_ref, transposed):
    @pl.when(jnp.logical_not(_is_lat()))
    def _():
        if not transposed:
            ks_ref[...] = k
            vs_ref[...] = v
        else:
            n = k.shape[1]
            for x, ref in ((k, ks_ref), (v, vs_ref)):
                xt = x.T
                for j in range(TM // SEQ):
                    ref[j * n:(j + 1) * n, :] = xt[:, j * SEQ:(j + 1) * SEQ]


def _qkv_out(nq, nk, transposed_state):
    specs = [_tok_spec(nq), _tok_spec(nk), _tok_spec(nk)]
    shapes = [jax.ShapeDtypeStruct((T, nq), BF), jax.ShapeDtypeStruct((T, nk), BF), jax.ShapeDtypeStruct((T, nk), BF)]
    if transposed_state:
        rows = (TM // SEQ) * nk
        specs += [pl.BlockSpec((rows, SEQ), lambda i: (jnp.minimum(i, N_CTX_TILES - 1), 0))] * 2
        shapes += [jax.ShapeDtypeStruct((BATCH * nk, SEQ), F32)] * 2
    else:
        specs += [_ctx_spec(nk)] * 2
        shapes += [jax.ShapeDtypeStruct((T_CTX, nk), F32)] * 2
    return specs, shapes


def _untranspose_state(st, heads):
    return st.reshape(BATCH, heads, HEAD_DIM, SEQ).transpose(0, 3, 1, 2)[:, None]


def _da_proj_kernel(xc_ref, xl_ref, mod_ref, w_ref, ra_ref, rb_ref, qb_ref, kb_ref, vb_ref, ks_ref, vs_ref):
    h = _modulate(_pick(xc_ref, xl_ref), mod_ref, 0, 1)
    a, b = ra_ref[...], rb_ref[...]
    q = _dot(h, w_ref[:, 0:D])
    k = _dot(h, w_ref[:, D:2 * D])
    qb_ref[...] = (_rope(q, a, b) * Q_SCALE).astype(BF)
    v = _dot(h, w_ref[:, 2 * D:3 * D])
    kb_ref[...] = _rope(k, a, b).astype(BF)
    vb_ref[...] = v.astype(BF)
    _store_state(k, v, ks_ref, vs_ref, False)


def _da_proj(x_ctx, x_lat, mods, layer, w, rope_a, rope_b):
    specs, shapes = _qkv_out(D, D, False)
    return pl.pallas_call(
        _da_proj_kernel,
        grid=(N_TILES,),
        in_specs=[_ctx_spec(D), _lat_spec(D), _mod_spec(layer), _const_spec((D, 3 * D)),
                  _rope_spec(D), _rope_spec(D)],
        out_specs=specs, out_shape=shapes,
        compiler_params=_cparams(1),
        name=f"da_proj_l{layer}",
    )(x_ctx, x_lat, mods, w, rope_a, rope_b)


def _na_proj_kernel(x_ref, mod_ref, w_ref, qb_ref, kb_ref, vb_ref, ks_ref, vs_ref):
    h = _modulate(x_ref[...], mod_ref, 0, 1)
    q = _dot(h, w_ref[:, 0:D])
    k = _dot(h, w_ref[:, D:2 * D])
    qb_ref[...] = (q * Q_SCALE).astype(BF)
    v = _dot(h, w_ref[:, 2 * D:3 * D])
    kb_ref[...] = k.astype(BF)
    vb_ref[...] = v.astype(BF)
    _store_state(k, v, ks_ref, vs_ref, True)


def _na_proj(x, mods, layer, w):
    specs, shapes = _qkv_out(D, D, True)
    return pl.pallas_call(
        _na_proj_kernel,
        grid=(N_TILES,),
        in_specs=[_tok_spec(D), _mod_spec(layer), _const_spec((D, 3 * D))],
        out_specs=specs, out_shape=shapes,
        compiler_params=_cparams(1),
        name=f"na_proj_l{layer}",
    )(x, mods, w)


GN_BLOCK = 256


def _head_rms(x, g_ref, gain):
    x2 = x * x
    hi = x2.astype(BF)
    lo = (x2 - hi.astype(F32)).astype(BF)
    g = g_ref[...]
    ms = jnp.concatenate(
        [_dot(hi[:, j:j + GN_BLOCK], g) + _dot(lo[:, j:j + GN_BLOCK], g) for j in range(0, x.shape[1], GN_BLOCK)],
        axis=1)
    return x * lax.rsqrt(ms + RMS_EPS) * gain


def _gq_proj_kernel(x_ref, mod_ref, w_ref, g_ref, qn_ref, kn_ref, ra_ref, rb_ref,
                    qb_ref, kb_ref, vb_ref, ks_ref, vs_ref):
    nq, nk = GQ_HEADS * HEAD_DIM, GQ_KV_HEADS * HEAD_DIM
    h = _modulate(x_ref[...], mod_ref, 0, 1)
    a, b = ra_ref[...], rb_ref[...]
    q = _dot(h, w_ref[:, 0:nq])
    k = _dot(h, w_ref[:, nq:nq + nk])
    v = _dot(h, w_ref[:, nq + nk:nq + 2 * nk])
    k = _head_rms(k, g_ref, kn_ref[...])
    q = _head_rms(q, g_ref, qn_ref[...])
    kb_ref[...] = _rope(k, a[:, 0:nk], b[:, 0:nk]).astype(BF)
    qb_ref[...] = (_rope(q, a, b) * Q_SCALE).astype(BF)
    vb_ref[...] = v.astype(BF)
    _store_state(k, v, ks_ref, vs_ref, True)


def _gq_proj(x, mods, layer, w, g_mat, qn, kn, rope_a, rope_b):
    nq, nk = GQ_HEADS * HEAD_DIM, GQ_KV_HEADS * HEAD_DIM
    specs, shapes = _qkv_out(nq, nk, True)
    return pl.pallas_call(
        _gq_proj_kernel,
        grid=(N_TILES,),
        in_specs=[_tok_spec(D), _mod_spec(layer), _const_spec((D, nq + 2 * nk)),
                  _const_spec((GN_BLOCK, GN_BLOCK)), _const_spec((1, nq)), _const_spec((1, nk)),
                  _rope_spec(D), _rope_spec(D)],
        out_specs=specs, out_shape=shapes,
        compiler_params=_cparams(1),
        name=f"gq_proj_l{layer}",
    )(x, mods, w, g_mat, qn, kn, rope_a, rope_b)


def _hy_proj_kernel(x_ref, mod_ref, w_ref, u_ref):
    h = _modulate(x_ref[...], mod_ref, 0, 1)
    for c in range(HY_ORDER + 1):
        u_ref[:, c * D:(c + 1) * D] = _dot(h, w_ref[:, c * D:(c + 1) * D])


def _hy_proj(x, mods, layer, w):
    n = (HY_ORDER + 1) * D
    return pl.pallas_call(
        _hy_proj_kernel,
        grid=(N_TILES,),
        in_specs=[_tok_spec(D), _mod_spec(layer), _const_spec((D, n))],
        out_specs=_tok_spec(n),
        out_shape=jax.ShapeDtypeStruct((T, n), F32),
        compiler_params=_cparams(1),
        name=f"hy_proj_l{layer}",
    )(x, mods, w)


def _softmax_pv(qm, segs):
    return _softmax_finish(_scores(qm, segs), segs)


def _scores(qm, segs):
    return [_dot_nt(qm, k) for k, _ in segs]


def _softmax_finish(scores, segs):
    m = scores[0].max(axis=-1, keepdims=True)
    for s in scores[1:]:
        m = jnp.maximum(m, s.max(axis=-1, keepdims=True))
    den = None
    out = None
    for s, (_, v) in zip(scores, segs):
        e = jnp.exp2(s - m)
        d = e.sum(axis=-1, keepdims=True)
        o = _dot(e.astype(BF), v)
        den = d if den is None else den + d
        out = o if out is None else out + o
    return out / den


def _pipelined(jobs, score_fn, finish_fn):
    nxt = score_fn(jobs[0])
    for n, job in enumerate(jobs):
        cur, nxt = nxt, (score_fn(jobs[n + 1]) if n + 1 < len(jobs) else None)
        finish_fn(job, cur)


def _lane_half(shape):
    return lax.broadcasted_iota(jnp.int32, shape, 1) // HEAD_DIM


def _half_keep(half):
    return tuple(jnp.where(half == a, 1.0, 0.0).astype(BF) for a in (0, 1))


def _da_attn_kernel(*refs, has_cache, lam_init):
    if has_cache:
        q_ref, k_ref, v_ref, ck_ref, cv_ref, lam_ref, g_ref, o_ref = refs
    else:
        q_ref, k_ref, v_ref, lam_ref, g_ref, o_ref = refs
    lp = lam_ref[...]
    lam = (jnp.exp(jnp.sum(lp[0:1] * lp[1:2], axis=-1, keepdims=True))
           - jnp.exp(jnp.sum(lp[2:3] * lp[3:4], axis=-1, keepdims=True)) + lam_init)
    gain = g_ref[...] * (1.0 - lam_init)
    w = 2 * HEAD_DIM
    keep = _half_keep(_lane_half((TQ, w)))
    segs = []
    for hd in range(k_ref.shape[1] // w):
        cols = slice(hd * w, (hd + 1) * w)
        seg = [(k_ref[:, cols], v_ref[:, cols])]
        if has_cache:
            seg.append((ck_ref[:, cols].astype(BF), cv_ref[:, cols].astype(BF)))
        segs.append(seg)
    jobs = [(hd, t, a) for hd in range(len(segs)) for t in range(q_ref.shape[0] // TQ) for a in (0, 1)]
    first = {}

    def score_fn(job):
        hd, t, a = job
        return _scores(q_ref[t * TQ:(t + 1) * TQ, hd * w:(hd + 1) * w] * keep[a], segs[hd])

    def finish_fn(job, scores):
        hd, t, a = job
        o = _softmax_finish(scores, segs[hd])
        if a == 0:
            first[0] = o
            return
        o = first[0] - lam * o
        ms = jnp.mean(o * o, axis=-1, keepdims=True)
        o_ref[t * TQ:(t + 1) * TQ, hd * w:(hd + 1) * w] = (o * lax.rsqrt(ms + RMS_EPS) * gain).astype(BF)

    _pipelined(jobs, score_fn, finish_fn)


def _da_attention(qb, kb, vb, cache_k, cache_v, lam_p, subln_g, layer_idx):
    lam_init = 0.8 - 0.6 * math.exp(-0.3 * layer_idx)
    w = 2 * HEAD_DIM
    small = [pl.BlockSpec((4, HEAD_DIM), lambda *_: (0, 0)), pl.BlockSpec((1, w), lambda *_: (0, 0))]
    o_ctx = pl.pallas_call(
        functools.partial(_da_attn_kernel, has_cache=False, lam_init=lam_init),
        grid=(BATCH,),
        in_specs=[pl.BlockSpec((SEQ, D), lambda b: (b, 0))] * 3 + small,
        out_specs=pl.BlockSpec((SEQ, D), lambda b: (b, 0)),
        out_shape=jax.ShapeDtypeStruct((T_CTX, D), BF),
        compiler_params=_cparams(1),
        name="da_attn_ctx",
    )(qb, kb, vb, lam_p, subln_g)
    k0 = T_CTX // DEC_SEQ
    tok = pl.BlockSpec((DEC_SEQ, w), lambda b, h: (k0 + b, h))
    c_spec = pl.BlockSpec((None, PAST, w), lambda b, h: (b, 0, h))
    o_lat = pl.pallas_call(
        functools.partial(_da_attn_kernel, has_cache=True, lam_init=lam_init),
        grid=(DEC_BATCH, DA_HEADS),
        in_specs=[tok, tok, tok, c_spec, c_spec] + small,
        out_specs=pl.BlockSpec((DEC_SEQ, w), lambda b, h: (b, h)),
        out_shape=jax.ShapeDtypeStruct((T_LAT, D), BF),
        compiler_params=_cparams(2),
        name="da_attn_lat",
    )(qb, kb, vb, cache_k, cache_v, lam_p, subln_g)
    return o_ctx, o_lat


def _na_ctx_kernel(q_ref, k_ref, v_ref, o_ref):
    half = _lane_half((SEQ, LANES))
    keep = _half_keep(half)
    jobs = [(p, a) for p in range(NA_HEADS // 2) for a in (0, 1)]
    first = {}

    def seg(p):
        return [(k_ref[:, p * LANES:(p + 1) * LANES], v_ref[:, p * LANES:(p + 1) * LANES])]

    def score_fn(job):
        p, a = job
        return _scores(q_ref[:, p * LANES:(p + 1) * LANES] * keep[a], seg(p))

    def finish_fn(job, scores):
        p, a = job
        o = _softmax_finish(scores, seg(p))
        if a == 0:
            first[0] = o
        else:
            o_ref[:, p * LANES:(p + 1) * LANES] = jnp.where(half == 0, first[0], o).astype(BF)

    _pipelined(jobs, score_fn, finish_fn)


def _na_ctx_attention(qb, kb, vb):
    spec = pl.BlockSpec((SEQ, D), lambda b: (b, 0))
    return pl.pallas_call(
        _na_ctx_kernel,
        grid=(BATCH,),
        in_specs=[spec] * 3,
        out_specs=spec,
        out_shape=jax.ShapeDtypeStruct((T_CTX, D), BF),
        compiler_params=_cparams(1),
        name="na_attn_ctx",
    )(qb, kb, vb)


NA_TILES = ((0, (0, 2, 4, 6)), (4, (0, 2, 4, 6, 8, 10)), (8, (4, 6, 8, 10, 12, 14)), (12, (8, 10, 12, 14)))
NA_MAX_CHUNKS = 6
NA_BIAS_BLOCKS = 2 * NA_WIN_ROWS - 2


def _na_lat_kernel(q_ref, k_ref, v_ref, ck_ref, cv_ref, w_ref, m_ref, o_ref):
    ckb, cvb = ck_ref[...].astype(BF), cv_ref[...].astype(BF)
    rows = 4 * GRID_W
    half = _lane_half((rows, LANES))
    keep = _half_keep(half)
    jobs = [(i, a) for i in range(len(NA_TILES)) for a in (0, 1)]
    first = {}

    def key_rows(i):
        chunks = NA_TILES[i][1]
        return slice(chunks[0] * GRID_W, chunks[0] * GRID_W + len(chunks) * LANES)

    def score_fn(job):
        i, a = job
        r0, chunks = NA_TILES[i]
        qm = q_ref[i * rows:(i + 1) * rows, :] * keep[a]
        bias = jnp.concatenate(
            [w_ref[a, (6 - kr + r0) * GRID_W:(6 - kr + r0) * GRID_W + rows, :] for kr in chunks], axis=1)
        s_loc = _dot_nt(qm, k_ref[key_rows(i), :]) + bias + m_ref[i, :, 0:len(chunks) * LANES]
        return [s_loc, _dot_nt(qm, ckb)]

    def finish_fn(job, scores):
        i, a = job
        o = _softmax_finish(scores, [(None, v_ref[key_rows(i), :]), (None, cvb)])
        if a == 0:
            first[0] = o
        else:
            o_ref[i * rows:(i + 1) * rows, :] = jnp.where(half == 0, first[0], o).astype(BF)

    _pipelined(jobs, score_fn, finish_fn)


def _na_lat_attention(qb, kb, vb, cache_k, cache_v, bias_tab, mask_tab):
    k0 = T_CTX // DEC_SEQ
    tok = pl.BlockSpec((DEC_SEQ, LANES), lambda p, b: (k0 + b, p))
    c_spec = pl.BlockSpec((None, PAST, LANES), lambda p, b: (b, 0, p))
    return pl.pallas_call(
        _na_lat_kernel,
        grid=(NA_HEADS // 2, DEC_BATCH),
        in_specs=[tok, tok, tok, c_spec, c_spec,
                  pl.BlockSpec((None, 2, NA_BIAS_BLOCKS * GRID_W, LANES), lambda p, b: (p, 0, 0, 0)),
                  _const_spec(mask_tab.shape)],
        out_specs=pl.BlockSpec((DEC_SEQ, LANES), lambda p, b: (b, p)),
        out_shape=jax.ShapeDtypeStruct((T_LAT, D), BF),
        compiler_params=_cparams(2),
        name="na_attn_lat",
    )(qb, kb, vb, cache_k, cache_v, bias_tab, mask_tab)


def _na_bias_kernel(t_ref, r_ref, n_ref, o_ref):
    t = t_ref[...]
    t1 = t.astype(BF)
    r1 = t - t1.astype(F32)
    t2 = r1.astype(BF)
    t3 = (r1 - t2.astype(F32)).astype(BF)
    r = r_ref[...]
    o_ref[...] = (_dot(t1, r) + _dot(t2, r) + _dot(t3, r) + n_ref[...]) * LOG2E


def _na_bias_table(rel_bias, onehot, neg):
    nrel = 2 * NA_WIN_COLS
    idx = 13 - np.arange(NA_BIAS_BLOCKS)[:, None] + np.arange(2)[None, :]
    t = jnp.pad(rel_bias[:, idx, :], ((0, 0), (0, 0), (0, 0), (0, 1)))
    t = t.reshape(NA_HEADS * NA_BIAS_BLOCKS, 2 * nrel)
    n = GRID_W * LANES
    tn = 2048
    out = pl.pallas_call(
        _na_bias_kernel,
        grid=(n // tn,),
        in_specs=[pl.BlockSpec(t.shape, lambda j: (0, 0)),
                  pl.BlockSpec((2 * nrel, tn), lambda j: (0, j)),
                  pl.BlockSpec((1, tn), lambda j: (0, j))],
        out_specs=pl.BlockSpec((t.shape[0], tn), lambda j: (0, j)),
        out_shape=jax.ShapeDtypeStruct((t.shape[0], n), F32),
        compiler_params=_cparams(1),
        name="na_bias_table",
    )(t, onehot, neg)
    return out.reshape(NA_HEADS // 2, 2, NA_BIAS_BLOCKS * GRID_W, LANES)


def _na_constants():
    nrel = 2 * NA_WIN_COLS
    qc = np.arange(GRID_W)[:, None]
    kc = np.arange(GRID_W)[None, :]
    rel = np.clip(kc - qc, -(NA_WIN_COLS - 1), NA_WIN_COLS - 1) + NA_WIN_COLS - 1
    cs = np.clip(qc - NA_WIN_COLS // 2, 0, GRID_W - NA_WIN_COLS)
    col_in = (kc >= cs) & (kc < cs + NA_WIN_COLS)
    onehot = np.zeros((2, nrel, GRID_W, 2, GRID_W), np.float32)
    for hf in range(2):
        onehot[hf, rel, qc, hf, kc] = 1.0
    neg = np.where(col_in, 0.0, NEG_INF).astype(np.float32)
    neg = np.broadcast_to(neg[:, None, :], (GRID_W, 2, GRID_W)).reshape(1, -1)
    rows = 4 * GRID_W
    mask = np.full((len(NA_TILES), rows, NA_MAX_CHUNKS * LANES), NEG_INF, np.float32)
    kr = min(NA_WIN_ROWS, GRID_ROWS)
    for i, (r0, chunks) in enumerate(NA_TILES):
        qr = r0 + np.arange(rows)[:, None] // GRID_W
        rs = np.clip(qr - kr // 2, 0, GRID_ROWS - kr)
        for c, krow0 in enumerate(chunks):
            krow = krow0 + np.arange(LANES)[None, :] // GRID_W
            mask[i, :, c * LANES:(c + 1) * LANES] = np.where((krow >= rs) & (krow < rs + kr), 0.0, NEG_INF)
    return (jnp.asarray(onehot.reshape(2 * nrel, GRID_W * LANES), BF), jnp.asarray(neg), jnp.asarray(mask))


def _gq_attn_kernel(*refs, has_cache):
    if has_cache:
        q_ref, k_ref, v_ref, ck_ref, cv_ref, o_ref = refs
    else:
        q_ref, k_ref, v_ref, o_ref = refs
    group = GQ_HEADS // GQ_KV_HEADS
    qw = LANES * group
    half = _lane_half((q_ref.shape[0], LANES))
    keep = _half_keep(half)
    for kvp in range(k_ref.shape[1] // LANES):
        kcols = slice(kvp * LANES, (kvp + 1) * LANES)
        segs = [(k_ref[:, kcols], v_ref[:, kcols])]
        if has_cache:
            segs.append((ck_ref[:, kcols].astype(BF), cv_ref[:, kcols].astype(BF)))
        first = {}

        def score_fn(j):
            pair, a = divmod(j, 2)
            qm = q_ref[:, kvp * qw + pair * LANES:kvp * qw + (pair + 1) * LANES] * keep[a]
            if a != (2 * pair) // group:
                qm = pltpu.roll(qm.astype(F32), HEAD_DIM, 1).astype(BF)
            return _scores(qm, segs)

        def finish_fn(j, scores):
            pair, a = divmod(j, 2)
            o = _softmax_finish(scores, segs)
            if a != (2 * pair) // group:
                o = pltpu.roll(o, HEAD_DIM, 1)
            if a == 0:
                first[0] = o
            else:
                cols = slice(kvp * qw + pair * LANES, kvp * qw + (pair + 1) * LANES)
                o_ref[:, cols] = jnp.where(half == 0, first[0], o).astype(BF)

        _pipelined(list(range(2 * group)), score_fn, finish_fn)


def _gq_attention(qb, kb, vb, cache_k, cache_v):
    nk = GQ_KV_HEADS * HEAD_DIM
    qw = LANES * (GQ_HEADS // GQ_KV_HEADS)
    npair = GQ_KV_HEADS // 2
    o_ctx = pl.pallas_call(
        functools.partial(_gq_attn_kernel, has_cache=False),
        grid=(BATCH,),
        in_specs=[pl.BlockSpec((SEQ, D), lambda b: (b, 0))] + [pl.BlockSpec((SEQ, nk), lambda b: (b, 0))] * 2,
        out_specs=pl.BlockSpec((SEQ, D), lambda b: (b, 0)),
        out_shape=jax.ShapeDtypeStruct((T_CTX, D), BF),
        compiler_params=_cparams(1),
        name="gq_attn_ctx",
    )(qb, kb, vb)
    qt = DEC_SEQ // TQ
    q0, k0 = T_CTX // TQ, T_CTX // DEC_SEQ
    kv_spec = pl.BlockSpec((DEC_SEQ, LANES), lambda b, p, t: (k0 + b, p))
    c_spec = pl.BlockSpec((None, PAST, LANES), lambda b, p, t: (b, 0, p))
    o_lat = pl.pallas_call(
        functools.partial(_gq_attn_kernel, has_cache=True),
        grid=(DEC_BATCH, npair, qt),
        in_specs=[pl.BlockSpec((TQ, qw), lambda b, p, t: (q0 + b * qt + t, p)), kv_spec, kv_spec, c_spec, c_spec],
        out_specs=pl.BlockSpec((TQ, qw), lambda b, p, t: (b * qt + t, p)),
        out_shape=jax.ShapeDtypeStruct((T_LAT, D), BF),
        compiler_params=_cparams(3),
        name="gq_attn_lat",
    )(qb, kb, vb, cache_k, cache_v)
    return o_ctx, o_lat


def _hy_filter_kernel(emb_ref, w1_ref, b1_ref, w2_ref, b2_ref, fr_ref, w3f_ref, w3b_ref, ldf_ref, ldb_ref,
                      c_ref, s_ref, hre_ref, him_ref, hny_ref):
    seq = emb_ref.shape[0]
    hp = lax.Precision.HIGHEST
    emb = emb_ref[...]
    fr = fr_ref[...]
    hid = jnp.sin(fr * (jnp.dot(emb, w1_ref[...], precision=hp, preferred_element_type=F32) + b1_ref[...]))
    hid = jnp.sin(fr * (jnp.dot(hid, w2_ref[...], precision=hp, preferred_element_type=F32) + b2_ref[...]))
    t = emb[:, 0:1]
    fwd = jnp.dot(hid, w3f_ref[...], precision=hp, preferred_element_type=F32) * jnp.exp(-jnp.exp(ldf_ref[...]) * t)
    bwd = jnp.dot(hid, w3b_ref[...], precision=hp, preferred_element_type=F32) * jnp.exp(-jnp.exp(ldb_ref[...]) * t)
    row = lax.broadcasted_iota(jnp.int32, fwd.shape, 0)
    bwd = jnp.where(row == 0, 0.0, bwd)
    even = fwd + bwd
    odd = bwd - fwd
    wk = jnp.where(row == 0, 0.5 / seq, 1.0 / seq)
    hre_ref[...] = _dot(c_ref[...].astype(BF), even.astype(BF)) * wk
    him_ref[...] = _dot(s_ref[...].astype(BF), odd.astype(BF)) * wk
    alt = jnp.where((row & 1) == 0, 1.0, -1.0)
    hny_ref[...] = jnp.sum(alt * even, axis=0, keepdims=True) * (0.5 / seq)


def _hy_filter(seq, emb, w1, b1, w2, b2, freq, w3, log_decay, cmat, smat):
    dc = 512
    nj = D // dc
    small = [_const_spec(a.shape) for a in (emb, w1, b1, w2, b2, freq)]
    return pl.pallas_call(
        _hy_filter_kernel,
        grid=(HY_ORDER, nj),
        in_specs=small + [pl.BlockSpec((HY_FFN, dc), lambda o, j: (0, (2 * o) * nj + j)),
                          pl.BlockSpec((HY_FFN, dc), lambda o, j: (0, (2 * o + 1) * nj + j)),
                          pl.BlockSpec((1, dc), lambda o, j: (0, (2 * o) * nj + j)),
                          pl.BlockSpec((1, dc), lambda o, j: (0, (2 * o + 1) * nj + j)),
                          _const_spec((seq, seq)), _const_spec((seq, seq))],
        out_specs=[pl.BlockSpec((None, seq, dc), lambda o, j: (o, 0, j)),
                   pl.BlockSpec((None, seq, dc), lambda o, j: (o, 0, j)),
                   pl.BlockSpec((None, 1, dc), lambda o, j: (o, 0, j))],
        out_shape=[jax.ShapeDtypeStruct((HY_ORDER, seq, D), F32), jax.ShapeDtypeStruct((HY_ORDER, seq, D), F32),
                   jax.ShapeDtypeStruct((HY_ORDER, 1, D), F32)],
        compiler_params=_cparams(2),
        name=f"hy_filter_{seq}",
    )(emb, w1, b1, w2, b2, freq, w3, w3, log_decay, log_decay, cmat, smat)


def _hy_conv_kernel(u0_ref, u1_ref, u2_ref, sw0_ref, sw1_ref, sw2_ref, sb0_ref, sb1_ref, sb2_ref,
                    fb_ref, hre_ref, him_ref, hny_ref, c_ref, s_ref, o_ref, cb_ref, sb_ref):
    seq = u0_ref.shape[0]

    @pl.when((pl.program_id(0) == 0) & (pl.program_id(1) == 0))
    def _():
        cb_ref[...] = c_ref[...].astype(BF)
        sb_ref[...] = s_ref[...].astype(BF)

    row = lax.broadcasted_iota(jnp.int32, u0_ref.shape, 0)
    alt = jnp.where((row & 1) == 0, 1.0, -1.0)

    def short_conv(u_ref, w_ref, b_ref):
        u = u_ref[...]
        prev = jnp.where(row == 0, 0.0, pltpu.roll(u, 1, 0))
        nxt = jnp.where(row == seq - 1, 0.0, pltpu.roll(u, seq - 1, 0))
        return prev * w_ref[0:1, :] + u * w_ref[1:2, :] + nxt * w_ref[2:3, :] + b_ref[...]

    cm, sm = cb_ref[...], sb_ref[...]
    z = short_conv(u0_ref, sw0_ref, sb0_ref)
    gates = (short_conv(u1_ref, sw1_ref, sb1_ref), short_conv(u2_ref, sw2_ref, sb2_ref))
    for o in range(HY_ORDER):
        zb = z.astype(BF)
        zc, zs = _dot(cm, zb), _dot(sm, zb)
        hre, him = hre_ref[o], him_ref[o]
        p_re = zc * hre + zs * him
        p_im = zc * him - zs * hre
        nyq = jnp.sum(alt * z, axis=0, keepdims=True) * hny_ref[o]
        y = _dot(cm, p_re.astype(BF)) - _dot(sm, p_im.astype(BF)) + alt * nyq
        z = gates[o] * (y + z * fb_ref[o:o + 1, :])
    o_ref[...] = z.astype(BF)


def _hy_conv(u, short_w, short_b, filter_bias, hre, him, hny, cmat, smat, seq, nbatch, row0, dc):
    nj = D // dc
    r0 = row0 // seq

    def part(p):
        return pl.BlockSpec((seq, dc), lambda j, b: (r0 + b, p * nj + j))

    def vec(rows, p):
        return pl.BlockSpec((rows, dc), lambda j, b: (0, p * nj + j))

    in_specs = ([part(p) for p in range(3)] + [vec(3, p) for p in range(3)] + [vec(1, p) for p in range(3)]
                + [pl.BlockSpec((HY_ORDER, dc), lambda j, b: (0, j)),
                   pl.BlockSpec((HY_ORDER, seq, dc), lambda j, b: (0, 0, j)),
                   pl.BlockSpec((HY_ORDER, seq, dc), lambda j, b: (0, 0, j)),
                   pl.BlockSpec((HY_ORDER, 1, dc), lambda j, b: (0, 0, j)),
                   _const_spec((seq, seq)), _const_spec((seq, seq))])
    return pl.pallas_call(
        _hy_conv_kernel,
        grid=(nj, nbatch),
        in_specs=in_specs,
        out_specs=pl.BlockSpec((seq, dc), lambda j, b: (b, j)),
        out_shape=jax.ShapeDtypeStruct((nbatch * seq, D), BF),
        scratch_shapes=[pltpu.VMEM((seq, seq), BF), pltpu.VMEM((seq, seq), BF)],
        compiler_params=_cparams(2),
        name=f"hy_conv_{seq}",
    )(u, u, u, short_w, short_w, short_w, short_b, short_b, short_b, filter_bias, hre, him, hny, cmat, smat)


def _dft_tables(seq):
    k = np.arange(seq, dtype=np.int64)
    ang = np.pi * ((k[:, None] * k[None, :]) % (2 * seq)) / seq
    return jnp.asarray(np.cos(ang), F32), jnp.asarray(np.sin(ang), F32)


def _hy_embedding(seq):
    t = np.arange(seq, dtype=np.float32) / np.float32(seq)
    ang = (2.0 * math.pi) * t[:, None] * np.arange(1, HY_BANDS + 1, dtype=np.float32)
    emb = np.concatenate([t[:, None], np.cos(ang), np.sin(ang)], axis=-1).astype(np.float32)
    return jnp.asarray(np.pad(emb, ((0, 0), (0, HY_EMB_PAD - HY_EMB))))


def _post_kernel(*refs, split_x, split_out):
    oc_ref, ol_ref = refs[0:2]
    if split_x:
        x = _pick(refs[2], refs[3])
        refs = refs[4:]
    else:
        x = refs[2][...]
        refs = refs[3:]
    mod_ref, wo_ref, g1_ref, b1_ref, w1_ref, w2_ref, g2_ref, b2_ref = refs[0:8]
    outs = refs[8:]
    a = _dot(_pick(oc_ref, ol_ref), wo_ref[...])
    x = _layer_norm(DN_ALPHA * x + mod_ref[2:3, :] * a, g1_ref[...], b1_ref[...])
    h = _modulate(x, mod_ref, 3, 4)
    acc = None
    fc = 1024
    for c in range(D_FF // fc):
        a = jnp.maximum(_dot(h, w1_ref[:, c * fc:(c + 1) * fc]), 0.0)
        part = _dot((a * a).astype(BF), w2_ref[c * fc:(c + 1) * fc, :])
        acc = part if acc is None else acc + part
    y = _layer_norm(DN_ALPHA * x + mod_ref[5:6, :] * acc, g2_ref[...], b2_ref[...])
    if split_out:
        yc_ref, yl_ref = outs

        @pl.when(jnp.logical_not(_is_lat()))
        def _():
            yc_ref[...] = y

        @pl.when(_is_lat())
        def _():
            yl_ref[...] = y
    else:
        outs[0][...] = y


def _layer_spec(layer, shape):
    nd = len(shape)
    return pl.BlockSpec((None,) + shape, lambda *_: (layer,) + (0,) * nd, pipeline_mode=pl.Buffered(1))


def _post(o_ctx, o_lat, xs, mods, layer, w_o, g1, b1, w1, w2, g2, b2, split_out):
    split_x = len(xs) == 2
    x_specs = [_ctx_spec(D), _lat_spec(D)] if split_x else [_tok_spec(D)]
    vec = _const_spec((1, D))
    if split_out:
        out_specs = [_ctx_spec(D), _lat_spec(D)]
        out_shape = [jax.ShapeDtypeStruct((T_CTX, D), F32), jax.ShapeDtypeStruct((T_LAT, D), F32)]
    else:
        out_specs = _tok_spec(D)
        out_shape = jax.ShapeDtypeStruct((T, D), F32)
    return pl.pallas_call(
        functools.partial(_post_kernel, split_x=split_x, split_out=split_out),
        grid=(N_TILES,),
        in_specs=[_ctx_spec(D), _lat_spec(D)] + x_specs + [
            _mod_spec(layer), _const_spec((D, D)), vec, vec, _layer_spec(layer, (D, D_FF)), _layer_spec(layer, (D_FF, D)),
            vec, vec],
        out_specs=out_specs,
        out_shape=out_shape,
        compiler_params=_cparams(1),
        name=f"post_l{layer}",
    )(o_ctx, o_lat, *xs, mods, w_o, g1, b1, w1, w2, g2, b2)


def _rope_tables():
    n = HEAD_DIM // 4
    pos = np.arange(DEC_SEQ)
    inv = (np.float32(ROPE_BASE) ** (-np.arange(n, dtype=np.float32) / np.float32(n))).astype(np.float32)
    ang_r = ((pos // GRID_W).astype(np.float32)[:, None] * inv).astype(np.float32)
    ang_c = ((pos % GRID_W).astype(np.float32)[:, None] * inv).astype(np.float32)
    cr, sr, cc, sc = np.cos(ang_r), np.sin(ang_r), np.cos(ang_c), np.sin(ang_c)
    a = np.tile(np.concatenate([cr, cr, cc, cc], axis=-1), (1, D // HEAD_DIM))
    b = np.tile(np.concatenate([-sr, sr, -sc, sc], axis=-1), (1, D // HEAD_DIM))
    a = np.concatenate([a, np.ones((TM, D), np.float32)], axis=0)
    b = np.concatenate([b, np.zeros((TM, D), np.float32)], axis=0)
    return jnp.asarray(a, F32), jnp.asarray(b, F32)


def kernel(x_prompt, x_sample, c, cache_da_k, cache_da_v, cache_na_k, cache_na_v, cache_gq_k, cache_gq_v, c_ctx, ada_w, ada_b, ln_g, ln_b, mlp_w1, mlp_w2, da_w_qkv, da_w_o, da_lambda, da_subln_g, na_w_qkv, na_w_o, na_rel_bias, gq_w_qkv, gq_w_o, gq_q_norm, gq_k_norm, hy_w_in, hy_short_w, hy_short_b, hy_ffn_w1, hy_ffn_b1, hy_ffn_w2, hy_ffn_b2, hy_ffn_freq, hy_ffn_w3, hy_log_decay, hy_filter_bias, hy_w_o):
    cvec = jnp.concatenate([c_ctx[None, :], c, jnp.zeros((MOD_ROWS - 1 - DEC_BATCH, D), F32)], axis=0)
    mods = _mods(cvec, ada_w, ada_b)
    rope_a, rope_b = _rope_tables()

    mlp_w1b, mlp_w2b = mlp_w1.astype(BF), mlp_w2.astype(BF)

    def finish(o_ctx, o_lat, xs, layer, w_o, split_out=False):
        return _post(o_ctx, o_lat, xs, mods, layer, w_o.astype(BF), ln_g[layer, 0][None], ln_b[layer, 0][None],
                     mlp_w1b, mlp_w2b, ln_g[layer, 1][None], ln_b[layer, 1][None], split_out)

    xs = (x_prompt.reshape(T_CTX, D), x_sample.reshape(T_LAT, D))
    qb, kb, vb, ks, vs = _da_proj(*xs, mods, 0, da_w_qkv[0].astype(BF), rope_a, rope_b)
    state_da_k = ks.reshape(BATCH, 1, SEQ, DA_HEADS, 2 * HEAD_DIM)
    state_da_v = vs.reshape(BATCH, 1, SEQ, DA_HEADS, 2 * HEAD_DIM)
    o_ctx, o_lat = _da_attention(qb, kb, vb, cache_da_k.reshape(DEC_BATCH, PAST, D),
                                 cache_da_v.reshape(DEC_BATCH, PAST, D), da_lambda[0], da_subln_g[0][None], 0)
    x = finish(o_ctx, o_lat, xs, 0, da_w_o[0])

    qb, kb, vb, ks, vs = _na_proj(x, mods, 1, na_w_qkv[0].astype(BF))
    state_na_k, state_na_v = _untranspose_state(ks, NA_HEADS), _untranspose_state(vs, NA_HEADS)
    onehot, neg, mask = _na_constants()
    bias_tab = _na_bias_table(na_rel_bias[0], onehot, neg)
    o_ctx = _na_ctx_attention(qb, kb, vb)
    o_lat = _na_lat_attention(qb, kb, vb, cache_na_k.reshape(DEC_BATCH, PAST, D),
                              cache_na_v.reshape(DEC_BATCH, PAST, D), bias_tab, mask)
    x = finish(o_ctx, o_lat, (x,), 1, na_w_o[0])

    nk = GQ_KV_HEADS * HEAD_DIM
    g_mat = jnp.asarray(np.kron(np.eye(GN_BLOCK // HEAD_DIM), np.full((HEAD_DIM, HEAD_DIM), 1.0 / HEAD_DIM)), BF)
    qb, kb, vb, ks, vs = _gq_proj(x, mods, 2, gq_w_qkv[0].astype(BF), g_mat,
                                  jnp.tile(gq_q_norm[0], GQ_HEADS)[None], jnp.tile(gq_k_norm[0], GQ_KV_HEADS)[None],
                                  rope_a, rope_b)
    state_gq_k, state_gq_v = _untranspose_state(ks, GQ_KV_HEADS), _untranspose_state(vs, GQ_KV_HEADS)
    o_ctx, o_lat = _gq_attention(qb, kb, vb, cache_gq_k.reshape(DEC_BATCH, PAST, nk),
                                 cache_gq_v.reshape(DEC_BATCH, PAST, nk))
    x = finish(o_ctx, o_lat, (x,), 2, gq_w_o[0])

    u = _hy_proj(x, mods, 3, hy_w_in[0].astype(BF))
    w1 = jnp.pad(hy_ffn_w1[0], ((0, HY_EMB_PAD - HY_EMB), (0, 0)))
    zs = []
    for seq, nbatch, row0, dc in ((SEQ, BATCH, 0, D), (DEC_SEQ, DEC_BATCH, T_CTX, 256)):
        cmat, smat = _dft_tables(seq)
        hre, him, hny = _hy_filter(seq, _hy_embedding(seq), w1, hy_ffn_b1[0][None], hy_ffn_w2[0], hy_ffn_b2[0][None],
                                   hy_ffn_freq[0][None], hy_ffn_w3[0], hy_log_decay[0][None], cmat, smat)
        zs.append(_hy_conv(u, hy_short_w[0], hy_short_b[0][None], hy_filter_bias[0], hre, him, hny, cmat, smat,
                           seq, nbatch, row0, dc))
    y_ctx, y_lat = finish(zs[0], zs[1], (x,), 3, hy_w_o[0], split_out=True)

    return (y_ctx.reshape(BATCH, SEQ, D), y_lat.reshape(DEC_BATCH, DEC_SEQ, D),
            state_da_k, state_da_v, state_na_k, state_na_v, state_gq_k, state_gq_v)
```

```python
import functools
import math

import numpy as np
import jax
import jax.numpy as jnp
from jax import lax
from jax.experimental import pallas as pl
from jax.experimental.pallas import tpu as pltpu

F32 = jnp.float32
BF = jnp.bfloat16

D = 1024
BATCH = 16
SEQ = 256
DEC_BATCH = 8
DEC_SEQ = 1024
PAST = 256
DEPTH = 4
GRID_W = 64
GRID_ROWS = DEC_SEQ // GRID_W
D_FF = 4 * D
T_CTX = BATCH * SEQ
T_LAT = DEC_BATCH * DEC_SEQ
T = T_CTX + T_LAT
HEAD_DIM = 64
ATT_SCALE = HEAD_DIM ** -0.5
LOG2E = math.log2(math.e)
Q_SCALE = ATT_SCALE * LOG2E
DA_HEADS = 8
NA_HEADS = 16
NA_WIN_ROWS = 8
NA_WIN_COLS = 16
GQ_HEADS = 16
GQ_KV_HEADS = 4
HY_ORDER = 2
HY_BANDS = 16
HY_EMB = 1 + 2 * HY_BANDS
HY_EMB_PAD = 40
HY_FFN = 64
ROPE_BASE = 10000.0
LN_EPS = 1e-5
RMS_EPS = 1e-6
DN_ALPHA = (2 * DEPTH) ** 0.25
NEG_INF = -1e30

LANES = 128
TM = 512
TM_POST = 1024
SUB_POST = 256
N_CTX_TILES = T_CTX // TM
N_TILES = T // TM
TQ = 512
MOD_ROWS = 16
VMEM_LIMIT = 56 * 1024 * 1024


def _cparams(n_axes, flags=None):
    return pltpu.CompilerParams(dimension_semantics=("arbitrary",) * n_axes,
                                vmem_limit_bytes=VMEM_LIMIT, flags=flags)


def _dot(a, b):
    return jnp.dot(a, b, preferred_element_type=F32)


def _dot_nt(a, b):
    return lax.dot_general(a, b, (((1,), (1,)), ((), ())), preferred_element_type=F32)


def _const_spec(shape):
    nd = len(shape)
    return pl.BlockSpec(shape, lambda *_: (0,) * nd, pipeline_mode=pl.Buffered(1))


def _mod_spec(layer, tm=TM):
    nctx = T_CTX // tm

    def row(i):
        return jnp.where(i < nctx, 0, 1 + (i - nctx) // (DEC_SEQ // tm))

    return pl.BlockSpec((None, None, 6, D), lambda i: (layer, row(i), 0, 0))


def _tok_spec(width, tm=TM):
    return pl.BlockSpec((tm, width), lambda i: (i, 0))


def _ctx_spec(width, tm=TM):
    return pl.BlockSpec((tm, width), lambda i: (jnp.minimum(i, T_CTX // tm - 1), 0))


def _lat_spec(width, tm=TM):
    return pl.BlockSpec((tm, width), lambda i: (jnp.maximum(i - T_CTX // tm, 0), 0))


def _is_lat(tm=TM):
    return pl.program_id(0) >= T_CTX // tm


def _pick(ctx_ref, lat_ref):
    return jnp.where(_is_lat(), lat_ref[...], ctx_ref[...])


def _layer_norm(r, g, b):
    mu = jnp.mean(r, axis=-1, keepdims=True)
    c = r - mu
    var = jnp.mean(c * c, axis=-1, keepdims=True)
    return c * lax.rsqrt(var + LN_EPS) * g + b


def _mods_kernel(c_ref, w_ref, b_ref, o_ref):
    c = c_ref[...]
    s = (c / (1.0 + jnp.exp(-c))).astype(BF)
    o_ref[...] = _dot(s, w_ref[...].astype(BF)) + b_ref[...]


def _mods(cvec, ada_w, ada_b):
    tn = 1536
    out = pl.pallas_call(
        _mods_kernel,
        grid=(DEPTH, 6 * D // tn),
        in_specs=[pl.BlockSpec((MOD_ROWS, D), lambda l, n: (0, 0)),
                  pl.BlockSpec((None, D, tn), lambda l, n: (l, 0, n)),
                  pl.BlockSpec((None, 1, tn), lambda l, n: (l, 0, n))],
        out_specs=pl.BlockSpec((None, MOD_ROWS, tn), lambda l, n: (l, 0, n)),
        out_shape=jax.ShapeDtypeStruct((DEPTH, MOD_ROWS, 6 * D), F32),
        compiler_params=_cparams(2),
        name="adaln_mods",
    )(cvec, ada_w, ada_b.reshape(DEPTH, 1, 6 * D))
    return out.reshape(DEPTH, MOD_ROWS, 6, D)


def _modulate(x, mod_ref, shift, scale):
    return (x * (1.0 + mod_ref[scale:scale + 1, :]) + mod_ref[shift:shift + 1, :]).astype(BF)


def _rope(x, a, b):
    n = x.shape[1]
    lane = lax.broadcasted_iota(jnp.int32, x.shape, 1)
    partner = jnp.where((lane & 16) == 0, pltpu.roll(x, n - 16, 1), pltpu.roll(x, 16, 1))
    return x * a + partner * b


def _rope_spec(width):
    per = DEC_SEQ // TM
    return pl.BlockSpec((TM, width), lambda i: (jnp.where(i < N_CTX_TILES, per, (i - N_CTX_TILES) % per), 0))


def _store_state(k, v, ks_ref, vs_ref, transposed):
    @pl.when(jnp.logical_not(_is_lat()))
    def _():
        if not transposed:
            ks_ref[...] = k
            vs_ref[...] = v
        else:
            n = k.shape[1]
            for x, ref in ((k, ks_ref), (v, vs_ref)):
                xt = x.T
                for j in range(TM // SEQ):
                    ref[j * n:(j + 1) * n, :] = xt[:, j * SEQ:(j + 1) * SEQ]


def _qkv_out(nq, nk, transposed_state):
    specs = [_tok_spec(nq), _tok_spec(nk), _tok_spec(nk)]
    shapes = [jax.ShapeDtypeStruct((T, nq), BF), jax.ShapeDtypeStruct((T, nk), BF), jax.ShapeDtypeStruct((T, nk), BF)]
    if transposed_state:
        rows = (TM // SEQ) * nk
        specs += [pl.BlockSpec((rows, SEQ), lambda i: (jnp.minimum(i, N_CTX_TILES - 1), 0))] * 2
        shapes += [jax.ShapeDtypeStruct((BATCH * nk, SEQ), F32)] * 2
    else:
        specs += [_ctx_spec(nk)] * 2
        shapes += [jax.ShapeDtypeStruct((T_CTX, nk), F32)] * 2
    return specs, shapes


def _untranspose_state(st, heads):
    return st.reshape(BATCH, heads, HEAD_DIM, SEQ).transpose(0, 3, 1, 2)[:, None]


def _da_proj_kernel(xc_ref, xl_ref, mod_ref, w_ref, ra_ref, rb_ref, qb_ref, kb_ref, vb_ref, ks_ref, vs_ref):
    h = _modulate(_pick(xc_ref, xl_ref), mod_ref, 0, 1)
    a, b = ra_ref[...], rb_ref[...]
    q = _dot(h, w_ref[:, 0:D])
    k = _dot(h, w_ref[:, D:2 * D])
    qb_ref[...] = (_rope(q, a, b) * Q_SCALE).astype(BF)
    v = _dot(h, w_ref[:, 2 * D:3 * D])
    kb_ref[...] = _rope(k, a, b).astype(BF)
    vb_ref[...] = v.astype(BF)
    _store_state(k, v, ks_ref, vs_ref, False)


def _da_proj(x_ctx, x_lat, mods, layer, w, rope_a, rope_b):
    specs, shapes = _qkv_out(D, D, False)
    return pl.pallas_call(
        _da_proj_kernel,
        grid=(N_TILES,),
        in_specs=[_ctx_spec(D), _lat_spec(D), _mod_spec(layer), _const_spec((D, 3 * D)),
                  _rope_spec(D), _rope_spec(D)],
        out_specs=specs, out_shape=shapes,
        compiler_params=_cparams(1),
        name=f"da_proj_l{layer}",
    )(x_ctx, x_lat, mods, w, rope_a, rope_b)


def _na_proj_kernel(x_ref, mod_ref, w_ref, qb_ref, kb_ref, vb_ref, ks_ref, vs_ref):
    h = _modulate(x_ref[...], mod_ref, 0, 1)
    q = _dot(h, w_ref[:, 0:D])
    k = _dot(h, w_ref[:, D:2 * D])
    qb_ref[...] = (q * Q_SCALE).astype(BF)
    v = _dot(h, w_ref[:, 2 * D:3 * D])
    kb_ref[...] = k.astype(BF)
    vb_ref[...] = v.astype(BF)
    _store_state(k, v, ks_ref, vs_ref, True)


def _na_proj(x, mods, layer, w):
    specs, shapes = _qkv_out(D, D, True)
    return pl.pallas_call(
        _na_proj_kernel,
        grid=(N_TILES,),
        in_specs=[_tok_spec(D), _mod_spec(layer), _const_spec((D, 3 * D))],
        out_specs=specs, out_shape=shapes,
        compiler_params=_cparams(1),
        name=f"na_proj_l{layer}",
    )(x, mods, w)


GN_BLOCK = 256


def _head_rms(x, g_ref, gain):
    x2 = x * x
    hi = x2.astype(BF)
    lo = (x2 - hi.astype(F32)).astype(BF)
    g = g_ref[...]
    ms = jnp.concatenate(
        [_dot(hi[:, j:j + GN_BLOCK], g) + _dot(lo[:, j:j + GN_BLOCK], g) for j in range(0, x.shape[1], GN_BLOCK)],
        axis=1)
    return x * lax.rsqrt(ms + RMS_EPS) * gain


def _gq_proj_kernel(x_ref, mod_ref, w_ref, g_ref, qn_ref, kn_ref, ra_ref, rb_ref,
                    qb_ref, kb_ref, vb_ref, ks_ref, vs_ref):
    nq, nk = GQ_HEADS * HEAD_DIM, GQ_KV_HEADS * HEAD_DIM
    h = _modulate(x_ref[...], mod_ref, 0, 1)
    a, b = ra_ref[...], rb_ref[...]
    q = _dot(h, w_ref[:, 0:nq])
    k = _dot(h, w_ref[:, nq:nq + nk])
    v = _dot(h, w_ref[:, nq + nk:nq + 2 * nk])
    k = _head_rms(k, g_ref, kn_ref[...])
    q = _head_rms(q, g_ref, qn_ref[...])
    kb_ref[...] = _rope(k, a[:, 0:nk], b[:, 0:nk]).astype(BF)
    qb_ref[...] = (_rope(q, a, b) * Q_SCALE).astype(BF)
    vb_ref[...] = v.astype(BF)
    _store_state(k, v, ks_ref, vs_ref, True)


def _gq_proj(x, mods, layer, w, g_mat, qn, kn, rope_a, rope_b):
    nq, nk = GQ_HEADS * HEAD_DIM, GQ_KV_HEADS * HEAD_DIM
    specs, shapes = _qkv_out(nq, nk, True)
    return pl.pallas_call(
        _gq_proj_kernel,
        grid=(N_TILES,),
        in_specs=[_tok_spec(D), _mod_spec(layer), _const_spec((D, nq + 2 * nk)),
                  _const_spec((GN_BLOCK, GN_BLOCK)), _const_spec((1, nq)), _const_spec((1, nk)),
                  _rope_spec(D), _rope_spec(D)],
        out_specs=specs, out_shape=shapes,
        compiler_params=_cparams(1),
        name=f"gq_proj_l{layer}",
    )(x, mods, w, g_mat, qn, kn, rope_a, rope_b)


def _hy_proj_kernel(x_ref, mod_ref, w_ref, u_ref):
    h = _modulate(x_ref[...], mod_ref, 0, 1)
    for c in range(HY_ORDER + 1):
        u_ref[:, c * D:(c + 1) * D] = _dot(h, w_ref[:, c * D:(c + 1) * D])


def _hy_proj(x, mods, layer, w):
    n = (HY_ORDER + 1) * D
    return pl.pallas_call(
        _hy_proj_kernel,
        grid=(N_TILES,),
        in_specs=[_tok_spec(D), _mod_spec(layer), _const_spec((D, n))],
        out_specs=_tok_spec(n),
        out_shape=jax.ShapeDtypeStruct((T, n), F32),
        compiler_params=_cparams(1),
        name=f"hy_proj_l{layer}",
    )(x, mods, w)


def _softmax_pv(qm, segs):
    return _softmax_finish(_scores(qm, segs), segs)


def _scores(qm, segs):
    return [_dot_nt(qm, k) for k, _ in segs]


def _softmax_finish(scores, segs):
    m = scores[0].max(axis=-1, keepdims=True)
    for s in scores[1:]:
        m = jnp.maximum(m, s.max(axis=-1, keepdims=True))
    den = None
    out = None
    for s, (_, v) in zip(scores, segs):
        e = jnp.exp2(s - m)
        d = e.sum(axis=-1, keepdims=True)
        o = _dot(e.astype(BF), v)
        den = d if den is None else den + d
        out = o if out is None else out + o
    return out / den


def _pipelined(jobs, score_fn, finish_fn):
    nxt = score_fn(jobs[0])
    for n, job in enumerate(jobs):
        cur, nxt = nxt, (score_fn(jobs[n + 1]) if n + 1 < len(jobs) else None)
        finish_fn(job, cur)


def _lane_half(shape):
    return lax.broadcasted_iota(jnp.int32, shape, 1) // HEAD_DIM


def _half_keep(half):
    return tuple(jnp.where(half == a, 1.0, 0.0).astype(BF) for a in (0, 1))


def _da_attn_kernel(*refs, has_cache, lam_init):
    if has_cache:
        q_ref, k_ref, v_ref, ck_ref, cv_ref, lam_ref, g_ref, o_ref = refs
    else:
        q_ref, k_ref, v_ref, lam_ref, g_ref, o_ref = refs
    lp = lam_ref[...]
    lam = (jnp.exp(jnp.sum(lp[0:1] * lp[1:2], axis=-1, keepdims=True))
           - jnp.exp(jnp.sum(lp[2:3] * lp[3:4], axis=-1, keepdims=True)) + lam_init)
    gain = g_ref[...] * (1.0 - lam_init)
    w = 2 * HEAD_DIM
    tq = min(TQ, q_ref.shape[0])
    keep = _half_keep(_lane_half((tq, w)))
    segs = []
    for hd in range(k_ref.shape[1] // w):
        cols = slice(hd * w, (hd + 1) * w)
        seg = [(k_ref[:, cols], v_ref[:, cols])]
        if has_cache:
            seg.append((ck_ref[:, cols].astype(BF), cv_ref[:, cols].astype(BF)))
        segs.append(seg)
    jobs = [(hd, t, a) for hd in range(len(segs)) for t in range(q_ref.shape[0] // tq) for a in (0, 1)]
    first = {}

    def score_fn(job):
        hd, t, a = job
        return _scores(q_ref[t * tq:(t + 1) * tq, hd * w:(hd + 1) * w] * keep[a], segs[hd])

    def finish_fn(job, scores):
        hd, t, a = job
        o = _softmax_finish(scores, segs[hd])
        if a == 0:
            first[0] = o
            return
        o = first[0] - lam * o
        ms = jnp.mean(o * o, axis=-1, keepdims=True)
        o_ref[t * tq:(t + 1) * tq, hd * w:(hd + 1) * w] = (o * lax.rsqrt(ms + RMS_EPS) * gain).astype(BF)

    _pipelined(jobs, score_fn, finish_fn)


def _da_attention(qb, kb, vb, cache_k, cache_v, lam_p, subln_g, layer_idx):
    lam_init = 0.8 - 0.6 * math.exp(-0.3 * layer_idx)
    w = 2 * HEAD_DIM
    small = [pl.BlockSpec((4, HEAD_DIM), lambda *_: (0, 0)), pl.BlockSpec((1, w), lambda *_: (0, 0))]
    o_ctx = pl.pallas_call(
        functools.partial(_da_attn_kernel, has_cache=False, lam_init=lam_init),
        grid=(BATCH,),
        in_specs=[pl.BlockSpec((SEQ, D), lambda b: (b, 0))] * 3 + small,
        out_specs=pl.BlockSpec((SEQ, D), lambda b: (b, 0)),
        out_shape=jax.ShapeDtypeStruct((T_CTX, D), BF),
        compiler_params=_cparams(1),
        name="da_attn_ctx",
    )(qb, kb, vb, lam_p, subln_g)
    k0 = T_CTX // DEC_SEQ
    tok = pl.BlockSpec((DEC_SEQ, w), lambda b, h: (k0 + b, h))
    c_spec = pl.BlockSpec((None, PAST, w), lambda b, h: (b, 0, h))
    o_lat = pl.pallas_call(
        functools.partial(_da_attn_kernel, has_cache=True, lam_init=lam_init),
        grid=(DEC_BATCH, DA_HEADS),
        in_specs=[tok, tok, tok, c_spec, c_spec] + small,
        out_specs=pl.BlockSpec((DEC_SEQ, w), lambda b, h: (b, h)),
        out_shape=jax.ShapeDtypeStruct((T_LAT, D), BF),
        compiler_params=_cparams(2),
        name="da_attn_lat",
    )(qb, kb, vb, cache_k, cache_v, lam_p, subln_g)
    return o_ctx, o_lat


def _na_ctx_kernel(q_ref, k_ref, v_ref, o_ref):
    half = _lane_half((SEQ, LANES))
    keep = _half_keep(half)
    jobs = [(p, a) for p in range(NA_HEADS // 2) for a in (0, 1)]
    first = {}

    def seg(p):
        return [(k_ref[:, p * LANES:(p + 1) * LANES], v_ref[:, p * LANES:(p + 1) * LANES])]

    def score_fn(job):
        p, a = job
        return _scores(q_ref[:, p * LANES:(p + 1) * LANES] * keep[a], seg(p))

    def finish_fn(job, scores):
        p, a = job
        o = _softmax_finish(scores, seg(p))
        if a == 0:
            first[0] = o
        else:
            o_ref[:, p * LANES:(p + 1) * LANES] = jnp.where(half == 0, first[0], o).astype(BF)

    _pipelined(jobs, score_fn, finish_fn)


def _na_ctx_attention(qb, kb, vb):
    spec = pl.BlockSpec((SEQ, D), lambda b: (b, 0))
    return pl.pallas_call(
        _na_ctx_kernel,
        grid=(BATCH,),
        in_specs=[spec] * 3,
        out_specs=spec,
        out_shape=jax.ShapeDtypeStruct((T_CTX, D), BF),
        compiler_params=_cparams(1),
        name="na_attn_ctx",
    )(qb, kb, vb)


NA_TILES = ((0, (0, 2, 4, 6)), (4, (0, 2, 4, 6, 8, 10)), (8, (4, 6, 8, 10, 12, 14)), (12, (8, 10, 12, 14)))
NA_MAX_CHUNKS = 6
NA_BIAS_BLOCKS = 2 * NA_WIN_ROWS - 2


def _na_lat_kernel(q_ref, k_ref, v_ref, ck_ref, cv_ref, w_ref, m_ref, o_ref):
    ckb, cvb = ck_ref[...].astype(BF), cv_ref[...].astype(BF)
    rows = 4 * GRID_W
    half = _lane_half((rows, LANES))
    keep = _half_keep(half)
    jobs = [(i, a) for i in range(len(NA_TILES)) for a in (0, 1)]
    first = {}

    def key_rows(i):
        chunks = NA_TILES[i][1]
        return slice(chunks[0] * GRID_W, chunks[0] * GRID_W + len(chunks) * LANES)

    def score_fn(job):
        i, a = job
        r0, chunks = NA_TILES[i]
        qm = q_ref[i * rows:(i + 1) * rows, :] * keep[a]
        bias = jnp.concatenate(
            [w_ref[a, (6 - kr + r0) * GRID_W:(6 - kr + r0) * GRID_W + rows, :] for kr in chunks], axis=1)
        s_loc = _dot_nt(qm, k_ref[key_rows(i), :]) + bias + m_ref[i, :, 0:len(chunks) * LANES]
        return [s_loc, _dot_nt(qm, ckb)]

    def finish_fn(job, scores):
        i, a = job
        o = _softmax_finish(scores, [(None, v_ref[key_rows(i), :]), (None, cvb)])
        if a == 0:
            first[0] = o
        else:
            o_ref[i * rows:(i + 1) * rows, :] = jnp.where(half == 0, first[0], o).astype(BF)

    _pipelined(jobs, score_fn, finish_fn)


def _na_lat_attention(qb, kb, vb, cache_k, cache_v, bias_tab, mask_tab):
    k0 = T_CTX // DEC_SEQ
    tok = pl.BlockSpec((DEC_SEQ, LANES), lambda p, b: (k0 + b, p))
    c_spec = pl.BlockSpec((None, PAST, LANES), lambda p, b: (b, 0, p))
    return pl.pallas_call(
        _na_lat_kernel,
        grid=(NA_HEADS // 2, DEC_BATCH),
        in_specs=[tok, tok, tok, c_spec, c_spec,
                  pl.BlockSpec((None, 2, NA_BIAS_BLOCKS * GRID_W, LANES), lambda p, b: (p, 0, 0, 0)),
                  _const_spec(mask_tab.shape)],
        out_specs=pl.BlockSpec((DEC_SEQ, LANES), lambda p, b: (b, p)),
        out_shape=jax.ShapeDtypeStruct((T_LAT, D), BF),
        compiler_params=_cparams(2),
        name="na_attn_lat",
    )(qb, kb, vb, cache_k, cache_v, bias_tab, mask_tab)


def _na_bias_kernel(t_ref, r_ref, n_ref, o_ref):
    t = t_ref[...]
    t1 = t.astype(BF)
    r1 = t - t1.astype(F32)
    t2 = r1.astype(BF)
    t3 = (r1 - t2.astype(F32)).astype(BF)
    r = r_ref[...]
    o_ref[...] = (_dot(t1, r) + _dot(t2, r) + _dot(t3, r) + n_ref[...]) * LOG2E


def _na_bias_table(rel_bias, onehot, neg):
    nrel = 2 * NA_WIN_COLS
    idx = 13 - np.arange(NA_BIAS_BLOCKS)[:, None] + np.arange(2)[None, :]
    t = jnp.pad(rel_bias[:, idx, :], ((0, 0), (0, 0), (0, 0), (0, 1)))
    t = t.reshape(NA_HEADS * NA_BIAS_BLOCKS, 2 * nrel)
    n = GRID_W * LANES
    tn = 2048
    out = pl.pallas_call(
        _na_bias_kernel,
        grid=(n // tn,),
        in_specs=[pl.BlockSpec(t.shape, lambda j: (0, 0)),
                  pl.BlockSpec((2 * nrel, tn), lambda j: (0, j)),
                  pl.BlockSpec((1, tn), lambda j: (0, j))],
        out_specs=pl.BlockSpec((t.shape[0], tn), lambda j: (0, j)),
        out_shape=jax.ShapeDtypeStruct((t.shape[0], n), F32),
        compiler_params=_cparams(1),
        name="na_bias_table",
    )(t, onehot, neg)
    return out.reshape(NA_HEADS // 2, 2, NA_BIAS_BLOCKS * GRID_W, LANES)


def _na_constants():
    nrel = 2 * NA_WIN_COLS
    qc = np.arange(GRID_W)[:, None]
    kc = np.arange(GRID_W)[None, :]
    rel = np.clip(kc - qc, -(NA_WIN_COLS - 1), NA_WIN_COLS - 1) + NA_WIN_COLS - 1
    cs = np.clip(qc - NA_WIN_COLS // 2, 0, GRID_W - NA_WIN_COLS)
    col_in = (kc >= cs) & (kc < cs + NA_WIN_COLS)
    onehot = np.zeros((2, nrel, GRID_W, 2, GRID_W), np.float32)
    for hf in range(2):
        onehot[hf, rel, qc, hf, kc] = 1.0
    neg = np.where(col_in, 0.0, NEG_INF).astype(np.float32)
    neg = np.broadcast_to(neg[:, None, :], (GRID_W, 2, GRID_W)).reshape(1, -1)
    rows = 4 * GRID_W
    mask = np.full((len(NA_TILES), rows, NA_MAX_CHUNKS * LANES), NEG_INF, np.float32)
    kr = min(NA_WIN_ROWS, GRID_ROWS)
    for i, (r0, chunks) in enumerate(NA_TILES):
        qr = r0 + np.arange(rows)[:, None] // GRID_W
        rs = np.clip(qr - kr // 2, 0, GRID_ROWS - kr)
        for c, krow0 in enumerate(chunks):
            krow = krow0 + np.arange(LANES)[None, :] // GRID_W
            mask[i, :, c * LANES:(c + 1) * LANES] = np.where((krow >= rs) & (krow < rs + kr), 0.0, NEG_INF)
    return (jnp.asarray(onehot.reshape(2 * nrel, GRID_W * LANES), BF), jnp.asarray(neg), jnp.asarray(mask))


def _gq_attn_kernel(*refs, has_cache):
    if has_cache:
        q_ref, k_ref, v_ref, ck_ref, cv_ref, o_ref = refs
    else:
        q_ref, k_ref, v_ref, o_ref = refs
    group = GQ_HEADS // GQ_KV_HEADS
    qw = LANES * group
    half = _lane_half((q_ref.shape[0], LANES))
    keep = _half_keep(half)
    for kvp in range(k_ref.shape[1] // LANES):
        kcols = slice(kvp * LANES, (kvp + 1) * LANES)
        segs = [(k_ref[:, kcols], v_ref[:, kcols])]
        if has_cache:
            segs.append((ck_ref[:, kcols].astype(BF), cv_ref[:, kcols].astype(BF)))
        first = {}

        def score_fn(j):
            pair, a = divmod(j, 2)
            qm = q_ref[:, kvp * qw + pair * LANES:kvp * qw + (pair + 1) * LANES] * keep[a]
            if a != (2 * pair) // group:
                qm = pltpu.roll(qm.astype(F32), HEAD_DIM, 1).astype(BF)
            return _scores(qm, segs)

        def finish_fn(j, scores):
            pair, a = divmod(j, 2)
            o = _softmax_finish(scores, segs)
            if a != (2 * pair) // group:
                o = pltpu.roll(o, HEAD_DIM, 1)
            if a == 0:
                first[0] = o
            else:
                cols = slice(kvp * qw + pair * LANES, kvp * qw + (pair + 1) * LANES)
                o_ref[:, cols] = jnp.where(half == 0, first[0], o).astype(BF)

        _pipelined(list(range(2 * group)), score_fn, finish_fn)


def _gq_attention(qb, kb, vb, cache_k, cache_v):
    nk = GQ_KV_HEADS * HEAD_DIM
    qw = LANES * (GQ_HEADS // GQ_KV_HEADS)
    npair = GQ_KV_HEADS // 2
    o_ctx = pl.pallas_call(
        functools.partial(_gq_attn_kernel, has_cache=False),
        grid=(BATCH,),
        in_specs=[pl.BlockSpec((SEQ, D), lambda b: (b, 0))] + [pl.BlockSpec((SEQ, nk), lambda b: (b, 0))] * 2,
        out_specs=pl.BlockSpec((SEQ, D), lambda b: (b, 0)),
        out_shape=jax.ShapeDtypeStruct((T_CTX, D), BF),
        compiler_params=_cparams(1),
        name="gq_attn_ctx",
    )(qb, kb, vb)
    qt = DEC_SEQ // TQ
    q0, k0 = T_CTX // TQ, T_CTX // DEC_SEQ
    kv_spec = pl.BlockSpec((DEC_SEQ, LANES), lambda b, p, t: (k0 + b, p))
    c_spec = pl.BlockSpec((None, PAST, LANES), lambda b, p, t: (b, 0, p))
    o_lat = pl.pallas_call(
        functools.partial(_gq_attn_kernel, has_cache=True),
        grid=(DEC_BATCH, npair, qt),
        in_specs=[pl.BlockSpec((TQ, qw), lambda b, p, t: (q0 + b * qt + t, p)), kv_spec, kv_spec, c_spec, c_spec],
        out_specs=pl.BlockSpec((TQ, qw), lambda b, p, t: (b * qt + t, p)),
        out_shape=jax.ShapeDtypeStruct((T_LAT, D), BF),
        compiler_params=_cparams(3),
        name="gq_attn_lat",
    )(qb, kb, vb, cache_k, cache_v)
    return o_ctx, o_lat


def _hy_filter_kernel(emb_ref, w1_ref, b1_ref, w2_ref, b2_ref, fr_ref, w3f_ref, w3b_ref, ldf_ref, ldb_ref,
                      c_ref, s_ref, hre_ref, him_ref, hny_ref):
    seq = emb_ref.shape[0]
    hp = lax.Precision.HIGHEST
    emb = emb_ref[...]
    fr = fr_ref[...]
    hid = jnp.sin(fr * (jnp.dot(emb, w1_ref[...], precision=hp, preferred_element_type=F32) + b1_ref[...]))
    hid = jnp.sin(fr * (jnp.dot(hid, w2_ref[...], precision=hp, preferred_element_type=F32) + b2_ref[...]))
    t = emb[:, 0:1]
    fwd = jnp.dot(hid, w3f_ref[...], precision=hp, preferred_element_type=F32) * jnp.exp(-jnp.exp(ldf_ref[...]) * t)
    bwd = jnp.dot(hid, w3b_ref[...], precision=hp, preferred_element_type=F32) * jnp.exp(-jnp.exp(ldb_ref[...]) * t)
    row = lax.broadcasted_iota(jnp.int32, fwd.shape, 0)
    bwd = jnp.where(row == 0, 0.0, bwd)
    even = fwd + bwd
    odd = bwd - fwd
    wk = jnp.where(row == 0, 0.5 / seq, 1.0 / seq)
    hre_ref[...] = _dot(c_ref[...].astype(BF), even.astype(BF)) * wk
    him_ref[...] = _dot(s_ref[...].astype(BF), odd.astype(BF)) * wk
    alt = jnp.where((row & 1) == 0, 1.0, -1.0)
    hny_ref[...] = jnp.sum(alt * even, axis=0, keepdims=True) * (0.5 / seq)


def _hy_filter(seq, emb, w1, b1, w2, b2, freq, w3, log_decay, cmat, smat):
    dc = 512
    nj = D // dc
    small = [_const_spec(a.shape) for a in (emb, w1, b1, w2, b2, freq)]
    return pl.pallas_call(
        _hy_filter_kernel,
        grid=(HY_ORDER, nj),
        in_specs=small + [pl.BlockSpec((HY_FFN, dc), lambda o, j: (0, (2 * o) * nj + j)),
                          pl.BlockSpec((HY_FFN, dc), lambda o, j: (0, (2 * o + 1) * nj + j)),
                          pl.BlockSpec((1, dc), lambda o, j: (0, (2 * o) * nj + j)),
                          pl.BlockSpec((1, dc), lambda o, j: (0, (2 * o + 1) * nj + j)),
                          _const_spec((seq, seq)), _const_spec((seq, seq))],
        out_specs=[pl.BlockSpec((None, seq, dc), lambda o, j: (o, 0, j)),
                   pl.BlockSpec((None, seq, dc), lambda o, j: (o, 0, j)),
                   pl.BlockSpec((None, 1, dc), lambda o, j: (o, 0, j))],
        out_shape=[jax.ShapeDtypeStruct((HY_ORDER, seq, D), F32), jax.ShapeDtypeStruct((HY_ORDER, seq, D), F32),
                   jax.ShapeDtypeStruct((HY_ORDER, 1, D), F32)],
        compiler_params=_cparams(2),
        name=f"hy_filter_{seq}",
    )(emb, w1, b1, w2, b2, freq, w3, w3, log_decay, log_decay, cmat, smat)


def _hy_conv_kernel(u0_ref, u1_ref, u2_ref, sw0_ref, sw1_ref, sw2_ref, sb0_ref, sb1_ref, sb2_ref,
                    fb_ref, hre_ref, him_ref, hny_ref, c_ref, s_ref, o_ref, cb_ref, sb_ref):
    seq = u0_ref.shape[0]

    @pl.when((pl.program_id(0) == 0) & (pl.program_id(1) == 0))
    def _():
        cb_ref[...] = c_ref[...].astype(BF)
        sb_ref[...] = s_ref[...].astype(BF)

    row = lax.broadcasted_iota(jnp.int32, u0_ref.shape, 0)
    alt = jnp.where((row & 1) == 0, 1.0, -1.0)

    def short_conv(u_ref, w_ref, b_ref):
        u = u_ref[...]
        prev = jnp.where(row == 0, 0.0, pltpu.roll(u, 1, 0))
        nxt = jnp.where(row == seq - 1, 0.0, pltpu.roll(u, seq - 1, 0))
        return prev * w_ref[0:1, :] + u * w_ref[1:2, :] + nxt * w_ref[2:3, :] + b_ref[...]

    cm, sm = cb_ref[...], sb_ref[...]
    z = short_conv(u0_ref, sw0_ref, sb0_ref)
    gates = (short_conv(u1_ref, sw1_ref, sb1_ref), short_conv(u2_ref, sw2_ref, sb2_ref))
    for o in range(HY_ORDER):
        zb = z.astype(BF)
        zc, zs = _dot(cm, zb), _dot(sm, zb)
        hre, him = hre_ref[o], him_ref[o]
        p_re = zc * hre + zs * him
        p_im = zc * him - zs * hre
        nyq = jnp.sum(alt * z, axis=0, keepdims=True) * hny_ref[o]
        y = _dot(cm, p_re.astype(BF)) - _dot(sm, p_im.astype(BF)) + alt * nyq
        z = gates[o] * (y + z * fb_ref[o:o + 1, :])
    o_ref[...] = z.astype(BF)


def _hy_conv(u, short_w, short_b, filter_bias, hre, him, hny, cmat, smat, seq, nbatch, row0, dc):
    nj = D // dc
    r0 = row0 // seq

    def part(p):
        return pl.BlockSpec((seq, dc), lambda j, b: (r0 + b, p * nj + j))

    def vec(rows, p):
        return pl.BlockSpec((rows, dc), lambda j, b: (0, p * nj + j))

    in_specs = ([part(p) for p in range(3)] + [vec(3, p) for p in range(3)] + [vec(1, p) for p in range(3)]
                + [pl.BlockSpec((HY_ORDER, dc), lambda j, b: (0, j)),
                   pl.BlockSpec((HY_ORDER, seq, dc), lambda j, b: (0, 0, j)),
                   pl.BlockSpec((HY_ORDER, seq, dc), lambda j, b: (0, 0, j)),
                   pl.BlockSpec((HY_ORDER, 1, dc), lambda j, b: (0, 0, j)),
                   _const_spec((seq, seq)), _const_spec((seq, seq))])
    return pl.pallas_call(
        _hy_conv_kernel,
        grid=(nj, nbatch),
        in_specs=in_specs,
        out_specs=pl.BlockSpec((seq, dc), lambda j, b: (b, j)),
        out_shape=jax.ShapeDtypeStruct((nbatch * seq, D), BF),
        scratch_shapes=[pltpu.VMEM((seq, seq), BF), pltpu.VMEM((seq, seq), BF)],
        compiler_params=_cparams(2),
        name=f"hy_conv_{seq}",
    )(u, u, u, short_w, short_w, short_w, short_b, short_b, short_b, filter_bias, hre, him, hny, cmat, smat)


def _dft_tables(seq):
    k = np.arange(seq, dtype=np.int64)
    ang = np.pi * ((k[:, None] * k[None, :]) % (2 * seq)) / seq
    return jnp.asarray(np.cos(ang), F32), jnp.asarray(np.sin(ang), F32)


def _hy_embedding(seq):
    t = np.arange(seq, dtype=np.float32) / np.float32(seq)
    ang = (2.0 * math.pi) * t[:, None] * np.arange(1, HY_BANDS + 1, dtype=np.float32)
    emb = np.concatenate([t[:, None], np.cos(ang), np.sin(ang)], axis=-1).astype(np.float32)
    return jnp.asarray(np.pad(emb, ((0, 0), (0, HY_EMB_PAD - HY_EMB))))


def _post_kernel(*refs, split_x, split_out, tm):
    oc_ref, ol_ref = refs[0:2]
    x_refs, refs = (refs[2:4], refs[4:]) if split_x else (refs[2:3], refs[3:])
    mod_ref, wo_ref, g1_ref, b1_ref, w1_ref, w2_ref, g2_ref, b2_ref = refs[0:8]
    outs = refs[8:]
    is_lat = _is_lat(tm)
    fc = 1024
    nc = D_FF // fc
    nsub = tm // SUB_POST

    def rows(j):
        return slice(j * SUB_POST, (j + 1) * SUB_POST)

    def pick(c_ref, l_ref, j):
        return jnp.where(is_lat, l_ref[rows(j), :], c_ref[rows(j), :])

    def norm1(j):
        a = _dot(pick(oc_ref, ol_ref, j), wo_ref[...])
        x = pick(x_refs[0], x_refs[1], j) if split_x else x_refs[0][rows(j), :]
        x1 = _layer_norm(DN_ALPHA * x + mod_ref[2:3, :] * a, g1_ref[...], b1_ref[...])
        return x1, _modulate(x1, mod_ref, 3, 4)

    def mlp_chunk(h, c):
        a = jnp.maximum(_dot(h, w1_ref[:, c * fc:(c + 1) * fc]), 0.0)
        return _dot((a * a).astype(BF), w2_ref[c * fc:(c + 1) * fc, :])

    ys = []

    def norm2_store(j, x1, acc):
        y = _layer_norm(DN_ALPHA * x1 + mod_ref[5:6, :] * acc, g2_ref[...], b2_ref[...])
        if split_out:
            ys.append(y)
        else:
            outs[0][rows(j), :] = y

    cur = norm1(0)
    prev = None
    for j in range(nsub):
        x1, h = cur
        acc = mlp_chunk(h, 0)
        if j + 1 < nsub:
            cur = norm1(j + 1)
        if prev is not None:
            norm2_store(j - 1, *prev)
        for c in range(1, nc):
            acc = acc + mlp_chunk(h, c)
        prev = (x1, acc)
    norm2_store(nsub - 1, *prev)
    if split_out:
        yc_ref, yl_ref = outs

        @pl.when(jnp.logical_not(is_lat))
        def _():
            for j, y in enumerate(ys):
                yc_ref[rows(j), :] = y

        @pl.when(is_lat)
        def _():
            for j, y in enumerate(ys):
                yl_ref[rows(j), :] = y


def _layer_spec(layer, shape):
    nd = len(shape)
    return pl.BlockSpec((None,) + shape, lambda *_: (layer,) + (0,) * nd, pipeline_mode=pl.Buffered(1))


def _post(o_ctx, o_lat, xs, mods, layer, w_o, g1, b1, w1, w2, g2, b2, split_out):
    split_x = len(xs) == 2
    tm = TM if split_x or split_out else TM_POST
    x_specs = [_ctx_spec(D, tm), _lat_spec(D, tm)] if split_x else [_tok_spec(D, tm)]
    vec = _const_spec((1, D))
    if split_out:
        out_specs = [_ctx_spec(D, tm), _lat_spec(D, tm)]
        out_shape = [jax.ShapeDtypeStruct((T_CTX, D), F32), jax.ShapeDtypeStruct((T_LAT, D), F32)]
    else:
        out_specs = _tok_spec(D, tm)
        out_shape = jax.ShapeDtypeStruct((T, D), F32)
    return pl.pallas_call(
        functools.partial(_post_kernel, split_x=split_x, split_out=split_out, tm=tm),
        grid=(T // tm,),
        in_specs=[_ctx_spec(D, tm), _lat_spec(D, tm)] + x_specs + [
            _mod_spec(layer, tm), _const_spec((D, D)), vec, vec, _layer_spec(layer, (D, D_FF)),
            _layer_spec(layer, (D_FF, D)), vec, vec],
        out_specs=out_specs,
        out_shape=out_shape,
        compiler_params=_cparams(1),
        name=f"post_l{layer}",
    )(o_ctx, o_lat, *xs, mods, w_o, g1, b1, w1, w2, g2, b2)


def _rope_tables():
    n = HEAD_DIM // 4
    pos = np.arange(DEC_SEQ)
    inv = (np.float32(ROPE_BASE) ** (-np.arange(n, dtype=np.float32) / np.float32(n))).astype(np.float32)
    ang_r = ((pos // GRID_W).astype(np.float32)[:, None] * inv).astype(np.float32)
    ang_c = ((pos % GRID_W).astype(np.float32)[:, None] * inv).astype(np.float32)
    cr, sr, cc, sc = np.cos(ang_r), np.sin(ang_r), np.cos(ang_c), np.sin(ang_c)
    a = np.tile(np.concatenate([cr, cr, cc, cc], axis=-1), (1, D // HEAD_DIM))
    b = np.tile(np.concatenate([-sr, sr, -sc, sc], axis=-1), (1, D // HEAD_DIM))
    a = np.concatenate([a, np.ones((TM, D), np.float32)], axis=0)
    b = np.concatenate([b, np.zeros((TM, D), np.float32)], axis=0)
    return jnp.asarray(a, F32), jnp.asarray(b, F32)


def kernel(x_prompt, x_sample, c, cache_da_k, cache_da_v, cache_na_k, cache_na_v, cache_gq_k, cache_gq_v, c_ctx, ada_w, ada_b, ln_g, ln_b, mlp_w1, mlp_w2, da_w_qkv, da_w_o, da_lambda, da_subln_g, na_w_qkv, na_w_o, na_rel_bias, gq_w_qkv, gq_w_o, gq_q_norm, gq_k_norm, hy_w_in, hy_short_w, hy_short_b, hy_ffn_w1, hy_ffn_b1, hy_ffn_w2, hy_ffn_b2, hy_ffn_freq, hy_ffn_w3, hy_log_decay, hy_filter_bias, hy_w_o):
    cvec = jnp.concatenate([c_ctx[None, :], c, jnp.zeros((MOD_ROWS - 1 - DEC_BATCH, D), F32)], axis=0)
    mods = _mods(cvec, ada_w, ada_b)
    rope_a, rope_b = _rope_tables()

    mlp_w1b, mlp_w2b = mlp_w1.astype(BF), mlp_w2.astype(BF)

    def finish(o_ctx, o_lat, xs, layer, w_o, split_out=False):
        return _post(o_ctx, o_lat, xs, mods, layer, w_o.astype(BF), ln_g[layer, 0][None], ln_b[layer, 0][None],
                     mlp_w1b, mlp_w2b, ln_g[layer, 1][None], ln_b[layer, 1][None], split_out)

    xs = (x_prompt.reshape(T_CTX, D), x_sample.reshape(T_LAT, D))
    qb, kb, vb, ks, vs = _da_proj(*xs, mods, 0, da_w_qkv[0].astype(BF), rope_a, rope_b)
    state_da_k = ks.reshape(BATCH, 1, SEQ, DA_HEADS, 2 * HEAD_DIM)
    state_da_v = vs.reshape(BATCH, 1, SEQ, DA_HEADS, 2 * HEAD_DIM)
    o_ctx, o_lat = _da_attention(qb, kb, vb, cache_da_k.reshape(DEC_BATCH, PAST, D),
                                 cache_da_v.reshape(DEC_BATCH, PAST, D), da_lambda[0], da_subln_g[0][None], 0)
    x = finish(o_ctx, o_lat, xs, 0, da_w_o[0])

    qb, kb, vb, ks, vs = _na_proj(x, mods, 1, na_w_qkv[0].astype(BF))
    state_na_k, state_na_v = _untranspose_state(ks, NA_HEADS), _untranspose_state(vs, NA_HEADS)
    onehot, neg, mask = _na_constants()
    bias_tab = _na_bias_table(na_rel_bias[0], onehot, neg)
    o_ctx = _na_ctx_attention(qb, kb, vb)
    o_lat = _na_lat_attention(qb, kb, vb, cache_na_k.reshape(DEC_BATCH, PAST, D),
                              cache_na_v.reshape(DEC_BATCH, PAST, D), bias_tab, mask)
    x = finish(o_ctx, o_lat, (x,), 1, na_w_o[0])

    nk = GQ_KV_HEADS * HEAD_DIM
    g_mat = jnp.asarray(np.kron(np.eye(GN_BLOCK // HEAD_DIM), np.full((HEAD_DIM, HEAD_DIM), 1.0 / HEAD_DIM)), BF)
    qb, kb, vb, ks, vs = _gq_proj(x, mods, 2, gq_w_qkv[0].astype(BF), g_mat,
                                  jnp.tile(gq_q_norm[0], GQ_HEADS)[None], jnp.tile(gq_k_norm[0], GQ_KV_HEADS)[None],
                                  rope_a, rope_b)
    state_gq_k, state_gq_v = _untranspose_state(ks, GQ_KV_HEADS), _untranspose_state(vs, GQ_KV_HEADS)
    o_ctx, o_lat = _gq_attention(qb, kb, vb, cache_gq_k.reshape(DEC_BATCH, PAST, nk),
                                 cache_gq_v.reshape(DEC_BATCH, PAST, nk))
    x = finish(o_ctx, o_lat, (x,), 2, gq_w_o[0])

    u = _hy_proj(x, mods, 3, hy_w_in[0].astype(BF))
    w1 = jnp.pad(hy_ffn_w1[0], ((0, HY_EMB_PAD - HY_EMB), (0, 0)))
    zs = []
    for seq, nbatch, row0, dc in ((SEQ, BATCH, 0, D), (DEC_SEQ, DEC_BATCH, T_CTX, 256)):
        cmat, smat = _dft_tables(seq)
        hre, him, hny = _hy_filter(seq, _hy_embedding(seq), w1, hy_ffn_b1[0][None], hy_ffn_w2[0], hy_ffn_b2[0][None],
                                   hy_ffn_freq[0][None], hy_ffn_w3[0], hy_log_decay[0][None], cmat, smat)
        zs.append(_hy_conv(u, hy_short_w[0], hy_short_b[0][None], hy_filter_bias[0], hre, him, hny, cmat, smat,
                           seq, nbatch, row0, dc))
    y_ctx, y_lat = finish(zs[0], zs[1], (x,), 3, hy_w_o[0], split_out=True)

    return (y_ctx.reshape(BATCH, SEQ, D), y_lat.reshape(DEC_BATCH, DEC_SEQ, D),
            state_da_k, state_da_v, state_na_k, state_na_v, state_gq_k, state_gq_v)
```

```python
import functools
import math

import numpy as np
import jax
import jax.numpy as jnp
from jax import lax
from jax.experimental import pallas as pl
from jax.experimental.pallas import tpu as pltpu

F32 = jnp.float32
BF = jnp.bfloat16

D = 1024
BATCH = 16
SEQ = 256
DEC_BATCH = 8
DEC_SEQ = 1024
PAST = 256
DEPTH = 4
GRID_W = 64
GRID_ROWS = DEC_SEQ // GRID_W
D_FF = 4 * D
T_CTX = BATCH * SEQ
T_LAT = DEC_BATCH * DEC_SEQ
T = T_CTX + T_LAT
HEAD_DIM = 64
ATT_SCALE = HEAD_DIM ** -0.5
LOG2E = math.log2(math.e)
Q_SCALE = ATT_SCALE * LOG2E
DA_HEADS = 8
NA_HEADS = 16
NA_WIN_ROWS = 8
NA_WIN_COLS = 16
GQ_HEADS = 16
GQ_KV_HEADS = 4
HY_ORDER = 2
HY_BANDS = 16
HY_EMB = 1 + 2 * HY_BANDS
HY_EMB_PAD = 40
HY_FFN = 64
ROPE_BASE = 10000.0
LN_EPS = 1e-5
RMS_EPS = 1e-6
DN_ALPHA = (2 * DEPTH) ** 0.25
NEG_INF = -1e30

LANES = 128
TM = 512
TM_POST = 512
FF_CHUNK = 512
MLP_CHUNK = 1024
N_FF_CHUNKS = D_FF // FF_CHUNK
SUB_POST = 256
N_CTX_TILES = T_CTX // TM
N_TILES = T // TM
TQ = 512
MOD_ROWS = 16
VMEM_LIMIT = 56 * 1024 * 1024


def _cparams(n_axes, flags=None):
    return pltpu.CompilerParams(dimension_semantics=("arbitrary",) * n_axes,
                                vmem_limit_bytes=VMEM_LIMIT, flags=flags)


def _dot(a, b):
    return jnp.dot(a, b, preferred_element_type=F32)


def _dot_nt(a, b):
    return lax.dot_general(a, b, (((1,), (1,)), ((), ())), preferred_element_type=F32)


def _const_spec(shape):
    nd = len(shape)
    return pl.BlockSpec(shape, lambda *_: (0,) * nd, pipeline_mode=pl.Buffered(1))


def _mod_spec(layer, tm=TM, off=0):
    nctx = T_CTX // tm

    def row(i):
        t = jnp.maximum(i - off, 0)
        return jnp.where(t < nctx, 0, 1 + (t - nctx) // (DEC_SEQ // tm))

    return pl.BlockSpec((None, None, 6, D), lambda i: (layer, row(i), 0, 0))


def _tok_spec(width, tm=TM, off=0):
    return pl.BlockSpec((tm, width), lambda i: (jnp.maximum(i - off, 0), 0))


def _ctx_spec(width, tm=TM, off=0):
    return pl.BlockSpec((tm, width), lambda i: (jnp.clip(i - off, 0, T_CTX // tm - 1), 0))


def _lat_spec(width, tm=TM, off=0):
    return pl.BlockSpec((tm, width), lambda i: (jnp.maximum(i - off - T_CTX // tm, 0), 0))


def _is_lat(tm=TM, off=0):
    return pl.program_id(0) >= off + T_CTX // tm


def _pick(ctx_ref, lat_ref):
    return jnp.where(_is_lat(), lat_ref[...], ctx_ref[...])


def _layer_norm(r, g, b):
    mu = jnp.mean(r, axis=-1, keepdims=True)
    c = r - mu
    var = jnp.mean(c * c, axis=-1, keepdims=True)
    return c * lax.rsqrt(var + LN_EPS) * g + b


def _mods_kernel(c_ref, w_ref, b_ref, o_ref):
    c = c_ref[...]
    s = (c / (1.0 + jnp.exp(-c))).astype(BF)
    o_ref[...] = _dot(s, w_ref[...].astype(BF)) + b_ref[...]


def _mods(cvec, ada_w, ada_b):
    tn = 1536
    out = pl.pallas_call(
        _mods_kernel,
        grid=(DEPTH, 6 * D // tn),
        in_specs=[pl.BlockSpec((MOD_ROWS, D), lambda l, n: (0, 0)),
                  pl.BlockSpec((None, D, tn), lambda l, n: (l, 0, n)),
                  pl.BlockSpec((None, 1, tn), lambda l, n: (l, 0, n))],
        out_specs=pl.BlockSpec((None, MOD_ROWS, tn), lambda l, n: (l, 0, n)),
        out_shape=jax.ShapeDtypeStruct((DEPTH, MOD_ROWS, 6 * D), F32),
        compiler_params=_cparams(2),
        name="adaln_mods",
    )(cvec, ada_w, ada_b.reshape(DEPTH, 1, 6 * D))
    return out.reshape(DEPTH, MOD_ROWS, 6, D)


def _modulate(x, mod_ref, shift, scale):
    return (x * (1.0 + mod_ref[scale:scale + 1, :]) + mod_ref[shift:shift + 1, :]).astype(BF)


def _rope(x, a, b):
    n = x.shape[1]
    lane = lax.broadcasted_iota(jnp.int32, x.shape, 1)
    partner = jnp.where((lane & 16) == 0, pltpu.roll(x, n - 16, 1), pltpu.roll(x, 16, 1))
    return x * a + partner * b


def _rope_spec(width):
    per = DEC_SEQ // TM
    return pl.BlockSpec((TM, width), lambda i: (jnp.where(i < N_CTX_TILES, per, (i - N_CTX_TILES) % per), 0))


def _store_state(k, v, ks_ref, vs_ref, transposed):
    @pl.when(jnp.logical_not(_is_lat()))
    def _():
        if not transposed:
            ks_ref[...] = k
            vs_ref[...] = v
        else:
            n = k.shape[1]
            for x, ref in ((k, ks_ref), (v, vs_ref)):
                xt = x.T
                for j in range(TM // SEQ):
                    ref[j * n:(j + 1) * n, :] = xt[:, j * SEQ:(j + 1) * SEQ]


def _qkv_out(nq, nk, transposed_state):
    specs = [_tok_spec(nq), _tok_spec(nk), _tok_spec(nk)]
    shapes = [jax.ShapeDtypeStruct((T, nq), BF), jax.ShapeDtypeStruct((T, nk), BF), jax.ShapeDtypeStruct((T, nk), BF)]
    if transposed_state:
        rows = (TM // SEQ) * nk
        specs += [pl.BlockSpec((rows, SEQ), lambda i: (jnp.minimum(i, N_CTX_TILES - 1), 0))] * 2
        shapes += [jax.ShapeDtypeStruct((BATCH * nk, SEQ), F32)] * 2
    else:
        specs += [_ctx_spec(nk)] * 2
        shapes += [jax.ShapeDtypeStruct((T_CTX, nk), F32)] * 2
    return specs, shapes


def _untranspose_state(st, heads):
    return st.reshape(BATCH, heads, HEAD_DIM, SEQ).transpose(0, 3, 1, 2)[:, None]


def _da_proj_kernel(xc_ref, xl_ref, mod_ref, w_ref, ra_ref, rb_ref, qb_ref, kb_ref, vb_ref, ks_ref, vs_ref):
    h = _modulate(_pick(xc_ref, xl_ref), mod_ref, 0, 1)
    a, b = ra_ref[...], rb_ref[...]
    q = _dot(h, w_ref[:, 0:D])
    k = _dot(h, w_ref[:, D:2 * D])
    qb_ref[...] = (_rope(q, a, b) * Q_SCALE).astype(BF)
    v = _dot(h, w_ref[:, 2 * D:3 * D])
    kb_ref[...] = _rope(k, a, b).astype(BF)
    vb_ref[...] = v.astype(BF)
    _store_state(k, v, ks_ref, vs_ref, False)


def _da_proj(x_ctx, x_lat, mods, layer, w, rope_a, rope_b):
    specs, shapes = _qkv_out(D, D, False)
    return pl.pallas_call(
        _da_proj_kernel,
        grid=(N_TILES,),
        in_specs=[_ctx_spec(D), _lat_spec(D), _mod_spec(layer), _const_spec((D, 3 * D)),
                  _rope_spec(D), _rope_spec(D)],
        out_specs=specs, out_shape=shapes,
        compiler_params=_cparams(1),
        name=f"da_proj_l{layer}",
    )(x_ctx, x_lat, mods, w, rope_a, rope_b)


def _na_proj_kernel(x_ref, mod_ref, w_ref, qb_ref, kb_ref, vb_ref, ks_ref, vs_ref):
    h = _modulate(x_ref[...], mod_ref, 0, 1)
    q = _dot(h, w_ref[:, 0:D])
    k = _dot(h, w_ref[:, D:2 * D])
    qb_ref[...] = (q * Q_SCALE).astype(BF)
    v = _dot(h, w_ref[:, 2 * D:3 * D])
    kb_ref[...] = k.astype(BF)
    vb_ref[...] = v.astype(BF)
    _store_state(k, v, ks_ref, vs_ref, True)


def _na_proj(x, mods, layer, w):
    specs, shapes = _qkv_out(D, D, True)
    return pl.pallas_call(
        _na_proj_kernel,
        grid=(N_TILES,),
        in_specs=[_tok_spec(D), _mod_spec(layer), _const_spec((D, 3 * D))],
        out_specs=specs, out_shape=shapes,
        compiler_params=_cparams(1),
        name=f"na_proj_l{layer}",
    )(x, mods, w)


GN_BLOCK = 256


def _head_rms(x, g_ref, gain):
    x2 = x * x
    hi = x2.astype(BF)
    lo = (x2 - hi.astype(F32)).astype(BF)
    g = g_ref[...]
    ms = jnp.concatenate(
        [_dot(hi[:, j:j + GN_BLOCK], g) + _dot(lo[:, j:j + GN_BLOCK], g) for j in range(0, x.shape[1], GN_BLOCK)],
        axis=1)
    return x * lax.rsqrt(ms + RMS_EPS) * gain


def _gq_proj_kernel(x_ref, mod_ref, w_ref, g_ref, qn_ref, kn_ref, ra_ref, rb_ref,
                    qb_ref, kb_ref, vb_ref, ks_ref, vs_ref):
    nq, nk = GQ_HEADS * HEAD_DIM, GQ_KV_HEADS * HEAD_DIM
    h = _modulate(x_ref[...], mod_ref, 0, 1)
    a, b = ra_ref[...], rb_ref[...]
    q = _dot(h, w_ref[:, 0:nq])
    k = _dot(h, w_ref[:, nq:nq + nk])
    v = _dot(h, w_ref[:, nq + nk:nq + 2 * nk])
    k = _head_rms(k, g_ref, kn_ref[...])
    q = _head_rms(q, g_ref, qn_ref[...])
    kb_ref[...] = _rope(k, a[:, 0:nk], b[:, 0:nk]).astype(BF)
    qb_ref[...] = (_rope(q, a, b) * Q_SCALE).astype(BF)
    vb_ref[...] = v.astype(BF)
    _store_state(k, v, ks_ref, vs_ref, True)


def _gq_proj(x, mods, layer, w, g_mat, qn, kn, rope_a, rope_b):
    nq, nk = GQ_HEADS * HEAD_DIM, GQ_KV_HEADS * HEAD_DIM
    specs, shapes = _qkv_out(nq, nk, True)
    return pl.pallas_call(
        _gq_proj_kernel,
        grid=(N_TILES,),
        in_specs=[_tok_spec(D), _mod_spec(layer), _const_spec((D, nq + 2 * nk)),
                  _const_spec((GN_BLOCK, GN_BLOCK)), _const_spec((1, nq)), _const_spec((1, nk)),
                  _rope_spec(D), _rope_spec(D)],
        out_specs=specs, out_shape=shapes,
        compiler_params=_cparams(1),
        name=f"gq_proj_l{layer}",
    )(x, mods, w, g_mat, qn, kn, rope_a, rope_b)


def _hy_proj_kernel(x_ref, mod_ref, w_ref, u_ref):
    h = _modulate(x_ref[...], mod_ref, 0, 1)
    for c in range(HY_ORDER + 1):
        u_ref[:, c * D:(c + 1) * D] = _dot(h, w_ref[:, c * D:(c + 1) * D])


def _hy_proj(x, mods, layer, w):
    n = (HY_ORDER + 1) * D
    return pl.pallas_call(
        _hy_proj_kernel,
        grid=(N_TILES,),
        in_specs=[_tok_spec(D), _mod_spec(layer), _const_spec((D, n))],
        out_specs=_tok_spec(n),
        out_shape=jax.ShapeDtypeStruct((T, n), F32),
        compiler_params=_cparams(1),
        name=f"hy_proj_l{layer}",
    )(x, mods, w)


def _softmax_pv(qm, segs):
    return _softmax_finish(_scores(qm, segs), segs)


def _scores(qm, segs):
    return [_dot_nt(qm, k) for k, _ in segs]


def _softmax_finish(scores, segs):
    m = scores[0].max(axis=-1, keepdims=True)
    for s in scores[1:]:
        m = jnp.maximum(m, s.max(axis=-1, keepdims=True))
    den = None
    out = None
    for s, (_, v) in zip(scores, segs):
        e = jnp.exp2(s - m)
        d = e.sum(axis=-1, keepdims=True)
        o = _dot(e.astype(BF), v)
        den = d if den is None else den + d
        out = o if out is None else out + o
    return out / den


def _pipelined(jobs, score_fn, finish_fn):
    nxt = score_fn(jobs[0])
    for n, job in enumerate(jobs):
        cur, nxt = nxt, (score_fn(jobs[n + 1]) if n + 1 < len(jobs) else None)
        finish_fn(job, cur)


def _lane_half(shape):
    return lax.broadcasted_iota(jnp.int32, shape, 1) // HEAD_DIM


def _half_keep(half):
    return tuple(jnp.where(half == a, 1.0, 0.0).astype(BF) for a in (0, 1))


def _da_attn_kernel(*refs, has_cache, lam_init):
    if has_cache:
        q_ref, k_ref, v_ref, ck_ref, cv_ref, lam_ref, g_ref, o_ref = refs
    else:
        q_ref, k_ref, v_ref, lam_ref, g_ref, o_ref = refs
    lp = lam_ref[...]
    lam = (jnp.exp(jnp.sum(lp[0:1] * lp[1:2], axis=-1, keepdims=True))
           - jnp.exp(jnp.sum(lp[2:3] * lp[3:4], axis=-1, keepdims=True)) + lam_init)
    gain = g_ref[...] * (1.0 - lam_init)
    w = 2 * HEAD_DIM
    tq = min(TQ, q_ref.shape[0])
    keep = _half_keep(_lane_half((tq, w)))
    segs = []
    for hd in range(k_ref.shape[1] // w):
        cols = slice(hd * w, (hd + 1) * w)
        seg = [(k_ref[:, cols], v_ref[:, cols])]
        if has_cache:
            head = pl.program_id(1)
            seg.append((ck_ref[:, head, :].astype(BF), cv_ref[:, head, :].astype(BF)))
        segs.append(seg)
    jobs = [(hd, t, a) for hd in range(len(segs)) for t in range(q_ref.shape[0] // tq) for a in (0, 1)]
    first = {}

    def score_fn(job):
        hd, t, a = job
        return _scores(q_ref[t * tq:(t + 1) * tq, hd * w:(hd + 1) * w] * keep[a], segs[hd])

    def finish_fn(job, scores):
        hd, t, a = job
        o = _softmax_finish(scores, segs[hd])
        if a == 0:
            first[0] = o
            return
        o = first[0] - lam * o
        ms = jnp.mean(o * o, axis=-1, keepdims=True)
        o_ref[t * tq:(t + 1) * tq, hd * w:(hd + 1) * w] = (o * lax.rsqrt(ms + RMS_EPS) * gain).astype(BF)

    _pipelined(jobs, score_fn, finish_fn)


def _da_attention(qb, kb, vb, cache_k, cache_v, lam_p, subln_g, layer_idx):
    lam_init = 0.8 - 0.6 * math.exp(-0.3 * layer_idx)
    w = 2 * HEAD_DIM
    small = [pl.BlockSpec((4, HEAD_DIM), lambda *_: (0, 0)), pl.BlockSpec((1, w), lambda *_: (0, 0))]
    o_ctx = pl.pallas_call(
        functools.partial(_da_attn_kernel, has_cache=False, lam_init=lam_init),
        grid=(BATCH,),
        in_specs=[pl.BlockSpec((SEQ, D), lambda b: (b, 0))] * 3 + small,
        out_specs=pl.BlockSpec((SEQ, D), lambda b: (b, 0)),
        out_shape=jax.ShapeDtypeStruct((T_CTX, D), BF),
        compiler_params=_cparams(1),
        name="da_attn_ctx",
    )(qb, kb, vb, lam_p, subln_g)
    k0 = T_CTX // DEC_SEQ
    tok = pl.BlockSpec((DEC_SEQ, w), lambda b, h: (k0 + b, h))
    c_spec = pl.BlockSpec((None, None, PAST, DA_HEADS, w), lambda b, h: (b, 0, 0, 0, 0))
    o_lat = pl.pallas_call(
        functools.partial(_da_attn_kernel, has_cache=True, lam_init=lam_init),
        grid=(DEC_BATCH, DA_HEADS),
        in_specs=[tok, tok, tok, c_spec, c_spec] + small,
        out_specs=pl.BlockSpec((DEC_SEQ, w), lambda b, h: (b, h)),
        out_shape=jax.ShapeDtypeStruct((T_LAT, D), BF),
        compiler_params=_cparams(2),
        name="da_attn_lat",
    )(qb, kb, vb, cache_k, cache_v, lam_p, subln_g)
    return o_ctx, o_lat


def _na_ctx_kernel(q_ref, k_ref, v_ref, o_ref):
    half = _lane_half((SEQ, LANES))
    keep = _half_keep(half)
    jobs = [(p, a) for p in range(NA_HEADS // 2) for a in (0, 1)]
    first = {}

    def seg(p):
        return [(k_ref[:, p * LANES:(p + 1) * LANES], v_ref[:, p * LANES:(p + 1) * LANES])]

    def score_fn(job):
        p, a = job
        return _scores(q_ref[:, p * LANES:(p + 1) * LANES] * keep[a], seg(p))

    def finish_fn(job, scores):
        p, a = job
        o = _softmax_finish(scores, seg(p))
        if a == 0:
            first[0] = o
        else:
            o_ref[:, p * LANES:(p + 1) * LANES] = jnp.where(half == 0, first[0], o).astype(BF)

    _pipelined(jobs, score_fn, finish_fn)


def _na_ctx_attention(qb, kb, vb):
    spec = pl.BlockSpec((SEQ, D), lambda b: (b, 0))
    return pl.pallas_call(
        _na_ctx_kernel,
        grid=(BATCH,),
        in_specs=[spec] * 3,
        out_specs=spec,
        out_shape=jax.ShapeDtypeStruct((T_CTX, D), BF),
        compiler_params=_cparams(1),
        name="na_attn_ctx",
    )(qb, kb, vb)


NA_TILES = ((0, (0, 2, 4, 6)), (4, (0, 2, 4, 6, 8, 10)), (8, (4, 6, 8, 10, 12, 14)), (12, (8, 10, 12, 14)))
NA_MAX_CHUNKS = 6
NA_BIAS_BLOCKS = 2 * NA_WIN_ROWS - 2


def _na_lat_kernel(q_ref, k_ref, v_ref, ck_ref, cv_ref, w_ref, m_ref, o_ref):
    pair = pl.program_id(1)
    ckb = jnp.concatenate([ck_ref[:, 2 * pair, :], ck_ref[:, 2 * pair + 1, :]], axis=1).astype(BF)
    cvb = jnp.concatenate([cv_ref[:, 2 * pair, :], cv_ref[:, 2 * pair + 1, :]], axis=1).astype(BF)
    rows = 4 * GRID_W
    half = _lane_half((rows, LANES))
    keep = _half_keep(half)
    jobs = [(i, a) for i in range(len(NA_TILES)) for a in (0, 1)]
    first = {}

    def key_rows(i):
        chunks = NA_TILES[i][1]
        return slice(chunks[0] * GRID_W, chunks[0] * GRID_W + len(chunks) * LANES)

    def score_fn(job):
        i, a = job
        r0, chunks = NA_TILES[i]
        qm = q_ref[i * rows:(i + 1) * rows, :] * keep[a]
        bias = jnp.concatenate(
            [w_ref[a, (6 - kr + r0) * GRID_W:(6 - kr + r0) * GRID_W + rows, :] for kr in chunks], axis=1)
        s_loc = _dot_nt(qm, k_ref[key_rows(i), :]) + bias + m_ref[i, :, 0:len(chunks) * LANES]
        return [s_loc, _dot_nt(qm, ckb)]

    def finish_fn(job, scores):
        i, a = job
        o = _softmax_finish(scores, [(None, v_ref[key_rows(i), :]), (None, cvb)])
        if a == 0:
            first[0] = o
        else:
            o_ref[i * rows:(i + 1) * rows, :] = jnp.where(half == 0, first[0], o).astype(BF)

    _pipelined(jobs, score_fn, finish_fn)


def _na_lat_attention(qb, kb, vb, cache_k, cache_v, bias_tab, mask_tab):
    k0 = T_CTX // DEC_SEQ
    tok = pl.BlockSpec((DEC_SEQ, LANES), lambda b, p: (k0 + b, p))
    c_spec = pl.BlockSpec((None, None, PAST, NA_HEADS, HEAD_DIM), lambda b, p: (b, 0, 0, 0, 0))
    return pl.pallas_call(
        _na_lat_kernel,
        grid=(DEC_BATCH, NA_HEADS // 2),
        in_specs=[tok, tok, tok, c_spec, c_spec,
                  pl.BlockSpec((None, 2, NA_BIAS_BLOCKS * GRID_W, LANES), lambda b, p: (p, 0, 0, 0)),
                  _const_spec(mask_tab.shape)],
        out_specs=pl.BlockSpec((DEC_SEQ, LANES), lambda b, p: (b, p)),
        out_shape=jax.ShapeDtypeStruct((T_LAT, D), BF),
        compiler_params=_cparams(2),
        name="na_attn_lat",
    )(qb, kb, vb, cache_k, cache_v, bias_tab, mask_tab)


def _na_bias_kernel(t_ref, r_ref, n_ref, o_ref):
    t = t_ref[...]
    t1 = t.astype(BF)
    r1 = t - t1.astype(F32)
    t2 = r1.astype(BF)
    t3 = (r1 - t2.astype(F32)).astype(BF)
    r = r_ref[...]
    o_ref[...] = (_dot(t1, r) + _dot(t2, r) + _dot(t3, r) + n_ref[...]) * LOG2E


def _na_bias_table(rel_bias, onehot, neg):
    nrel = 2 * NA_WIN_COLS
    idx = 13 - np.arange(NA_BIAS_BLOCKS)[:, None] + np.arange(2)[None, :]
    t = jnp.pad(rel_bias[:, idx, :], ((0, 0), (0, 0), (0, 0), (0, 1)))
    t = t.reshape(NA_HEADS * NA_BIAS_BLOCKS, 2 * nrel)
    n = GRID_W * LANES
    tn = 2048
    out = pl.pallas_call(
        _na_bias_kernel,
        grid=(n // tn,),
        in_specs=[pl.BlockSpec(t.shape, lambda j: (0, 0)),
                  pl.BlockSpec((2 * nrel, tn), lambda j: (0, j)),
                  pl.BlockSpec((1, tn), lambda j: (0, j))],
        out_specs=pl.BlockSpec((t.shape[0], tn), lambda j: (0, j)),
        out_shape=jax.ShapeDtypeStruct((t.shape[0], n), F32),
        compiler_params=_cparams(1),
        name="na_bias_table",
    )(t, onehot, neg)
    return out.reshape(NA_HEADS // 2, 2, NA_BIAS_BLOCKS * GRID_W, LANES)


def _na_constants():
    nrel = 2 * NA_WIN_COLS
    qc = np.arange(GRID_W)[:, None]
    kc = np.arange(GRID_W)[None, :]
    rel = np.clip(kc - qc, -(NA_WIN_COLS - 1), NA_WIN_COLS - 1) + NA_WIN_COLS - 1
    cs = np.clip(qc - NA_WIN_COLS // 2, 0, GRID_W - NA_WIN_COLS)
    col_in = (kc >= cs) & (kc < cs + NA_WIN_COLS)
    onehot = np.zeros((2, nrel, GRID_W, 2, GRID_W), np.float32)
    for hf in range(2):
        onehot[hf, rel, qc, hf, kc] = 1.0
    neg = np.where(col_in, 0.0, NEG_INF).astype(np.float32)
    neg = np.broadcast_to(neg[:, None, :], (GRID_W, 2, GRID_W)).reshape(1, -1)
    rows = 4 * GRID_W
    mask = np.full((len(NA_TILES), rows, NA_MAX_CHUNKS * LANES), NEG_INF, np.float32)
    kr = min(NA_WIN_ROWS, GRID_ROWS)
    for i, (r0, chunks) in enumerate(NA_TILES):
        qr = r0 + np.arange(rows)[:, None] // GRID_W
        rs = np.clip(qr - kr // 2, 0, GRID_ROWS - kr)
        for c, krow0 in enumerate(chunks):
            krow = krow0 + np.arange(LANES)[None, :] // GRID_W
            mask[i, :, c * LANES:(c + 1) * LANES] = np.where((krow >= rs) & (krow < rs + kr), 0.0, NEG_INF)
    return (jnp.asarray(onehot.reshape(2 * nrel, GRID_W * LANES), BF), jnp.asarray(neg), jnp.asarray(mask))


def _gq_attn_kernel(*refs, has_cache):
    if has_cache:
        q_ref, k_ref, v_ref, ck_ref, cv_ref, o_ref = refs
    else:
        q_ref, k_ref, v_ref, o_ref = refs
    group = GQ_HEADS // GQ_KV_HEADS
    qw = LANES * group
    half = _lane_half((q_ref.shape[0], LANES))
    keep = _half_keep(half)
    for kvp in range(k_ref.shape[1] // LANES):
        kcols = slice(kvp * LANES, (kvp + 1) * LANES)
        segs = [(k_ref[:, kcols], v_ref[:, kcols])]
        if has_cache:
            segs.append((ck_ref[:, kcols].astype(BF), cv_ref[:, kcols].astype(BF)))
        first = {}

        def score_fn(j):
            pair, a = divmod(j, 2)
            qm = q_ref[:, kvp * qw + pair * LANES:kvp * qw + (pair + 1) * LANES] * keep[a]
            if a != (2 * pair) // group:
                qm = pltpu.roll(qm.astype(F32), HEAD_DIM, 1).astype(BF)
            return _scores(qm, segs)

        def finish_fn(j, scores):
            pair, a = divmod(j, 2)
            o = _softmax_finish(scores, segs)
            if a != (2 * pair) // group:
                o = pltpu.roll(o, HEAD_DIM, 1)
            if a == 0:
                first[0] = o
            else:
                cols = slice(kvp * qw + pair * LANES, kvp * qw + (pair + 1) * LANES)
                o_ref[:, cols] = jnp.where(half == 0, first[0], o).astype(BF)

        _pipelined(list(range(2 * group)), score_fn, finish_fn)


def _gq_attention(qb, kb, vb, cache_k, cache_v):
    nk = GQ_KV_HEADS * HEAD_DIM
    qw = LANES * (GQ_HEADS // GQ_KV_HEADS)
    npair = GQ_KV_HEADS // 2
    o_ctx = pl.pallas_call(
        functools.partial(_gq_attn_kernel, has_cache=False),
        grid=(BATCH,),
        in_specs=[pl.BlockSpec((SEQ, D), lambda b: (b, 0))] + [pl.BlockSpec((SEQ, nk), lambda b: (b, 0))] * 2,
        out_specs=pl.BlockSpec((SEQ, D), lambda b: (b, 0)),
        out_shape=jax.ShapeDtypeStruct((T_CTX, D), BF),
        compiler_params=_cparams(1),
        name="gq_attn_ctx",
    )(qb, kb, vb)
    qt = DEC_SEQ // TQ
    q0, k0 = T_CTX // TQ, T_CTX // DEC_SEQ
    kv_spec = pl.BlockSpec((DEC_SEQ, LANES), lambda b, p, t: (k0 + b, p))
    c_spec = pl.BlockSpec((None, PAST, LANES), lambda b, p, t: (b, 0, p))
    o_lat = pl.pallas_call(
        functools.partial(_gq_attn_kernel, has_cache=True),
        grid=(DEC_BATCH, npair, qt),
        in_specs=[pl.BlockSpec((TQ, qw), lambda b, p, t: (q0 + b * qt + t, p)), kv_spec, kv_spec, c_spec, c_spec],
        out_specs=pl.BlockSpec((TQ, qw), lambda b, p, t: (b * qt + t, p)),
        out_shape=jax.ShapeDtypeStruct((T_LAT, D), BF),
        compiler_params=_cparams(3),
        name="gq_attn_lat",
    )(qb, kb, vb, cache_k, cache_v)
    return o_ctx, o_lat


def _hy_filter_kernel(emb_ref, w1_ref, b1_ref, w2_ref, b2_ref, fr_ref, w3f_ref, w3b_ref, ldf_ref, ldb_ref,
                      c_ref, s_ref, hre_ref, him_ref, hny_ref):
    seq = emb_ref.shape[0]
    hp = lax.Precision.HIGHEST
    emb = emb_ref[...]
    fr = fr_ref[...]
    hid = jnp.sin(fr * (jnp.dot(emb, w1_ref[...], precision=hp, preferred_element_type=F32) + b1_ref[...]))
    hid = jnp.sin(fr * (jnp.dot(hid, w2_ref[...], precision=hp, preferred_element_type=F32) + b2_ref[...]))
    t = emb[:, 0:1]
    fwd = jnp.dot(hid, w3f_ref[...], precision=hp, preferred_element_type=F32) * jnp.exp(-jnp.exp(ldf_ref[...]) * t)
    bwd = jnp.dot(hid, w3b_ref[...], precision=hp, preferred_element_type=F32) * jnp.exp(-jnp.exp(ldb_ref[...]) * t)
    row = lax.broadcasted_iota(jnp.int32, fwd.shape, 0)
    bwd = jnp.where(row == 0, 0.0, bwd)
    even = fwd + bwd
    odd = bwd - fwd
    wk = jnp.where(row == 0, 0.5 / seq, 1.0 / seq)
    hre_ref[...] = _dot(c_ref[...].astype(BF), even.astype(BF)) * wk
    him_ref[...] = _dot(s_ref[...].astype(BF), odd.astype(BF)) * wk
    alt = jnp.where((row & 1) == 0, 1.0, -1.0)
    hny_ref[...] = jnp.sum(alt * even, axis=0, keepdims=True) * (0.5 / seq)


def _hy_filter(seq, emb, w1, b1, w2, b2, freq, w3, log_decay, cmat, smat):
    dc = 512
    nj = D // dc
    small = [_const_spec(a.shape) for a in (emb, w1, b1, w2, b2, freq)]
    return pl.pallas_call(
        _hy_filter_kernel,
        grid=(HY_ORDER, nj),
        in_specs=small + [pl.BlockSpec((HY_FFN, dc), lambda o, j: (0, (2 * o) * nj + j)),
                          pl.BlockSpec((HY_FFN, dc), lambda o, j: (0, (2 * o + 1) * nj + j)),
                          pl.BlockSpec((1, dc), lambda o, j: (0, (2 * o) * nj + j)),
                          pl.BlockSpec((1, dc), lambda o, j: (0, (2 * o + 1) * nj + j)),
                          _const_spec((seq, seq)), _const_spec((seq, seq))],
        out_specs=[pl.BlockSpec((None, seq, dc), lambda o, j: (o, 0, j)),
                   pl.BlockSpec((None, seq, dc), lambda o, j: (o, 0, j)),
                   pl.BlockSpec((None, 1, dc), lambda o, j: (o, 0, j))],
        out_shape=[jax.ShapeDtypeStruct((HY_ORDER, seq, D), F32), jax.ShapeDtypeStruct((HY_ORDER, seq, D), F32),
                   jax.ShapeDtypeStruct((HY_ORDER, 1, D), F32)],
        compiler_params=_cparams(2),
        name=f"hy_filter_{seq}",
    )(emb, w1, b1, w2, b2, freq, w3, w3, log_decay, log_decay, cmat, smat)


def _hy_conv_kernel(u0_ref, u1_ref, u2_ref, sw0_ref, sw1_ref, sw2_ref, sb0_ref, sb1_ref, sb2_ref,
                    fb_ref, hre_ref, him_ref, hny_ref, c_ref, s_ref, o_ref, cb_ref, sb_ref):
    seq = u0_ref.shape[0]

    @pl.when((pl.program_id(0) == 0) & (pl.program_id(1) == 0))
    def _():
        cb_ref[...] = c_ref[...].astype(BF)
        sb_ref[...] = s_ref[...].astype(BF)

    row = lax.broadcasted_iota(jnp.int32, u0_ref.shape, 0)
    alt = jnp.where((row & 1) == 0, 1.0, -1.0)

    def short_conv(u_ref, w_ref, b_ref):
        u = u_ref[...]
        prev = jnp.where(row == 0, 0.0, pltpu.roll(u, 1, 0))
        nxt = jnp.where(row == seq - 1, 0.0, pltpu.roll(u, seq - 1, 0))
        return prev * w_ref[0:1, :] + u * w_ref[1:2, :] + nxt * w_ref[2:3, :] + b_ref[...]

    cm, sm = cb_ref[...], sb_ref[...]
    z = short_conv(u0_ref, sw0_ref, sb0_ref)
    gates = (short_conv(u1_ref, sw1_ref, sb1_ref), short_conv(u2_ref, sw2_ref, sb2_ref))
    for o in range(HY_ORDER):
        zb = z.astype(BF)
        zc, zs = _dot(cm, zb), _dot(sm, zb)
        hre, him = hre_ref[o], him_ref[o]
        p_re = zc * hre + zs * him
        p_im = zc * him - zs * hre
        nyq = jnp.sum(alt * z, axis=0, keepdims=True) * hny_ref[o]
        y = _dot(cm, p_re.astype(BF)) - _dot(sm, p_im.astype(BF)) + alt * nyq
        z = gates[o] * (y + z * fb_ref[o:o + 1, :])
    o_ref[...] = z.astype(BF)


def _hy_conv(u, short_w, short_b, filter_bias, hre, him, hny, cmat, smat, seq, nbatch, row0, dc):
    nj = D // dc
    r0 = row0 // seq

    def part(p):
        return pl.BlockSpec((seq, dc), lambda j, b: (r0 + b, p * nj + j))

    def vec(rows, p):
        return pl.BlockSpec((rows, dc), lambda j, b: (0, p * nj + j))

    in_specs = ([part(p) for p in range(3)] + [vec(3, p) for p in range(3)] + [vec(1, p) for p in range(3)]
                + [pl.BlockSpec((HY_ORDER, dc), lambda j, b: (0, j)),
                   pl.BlockSpec((HY_ORDER, seq, dc), lambda j, b: (0, 0, j)),
                   pl.BlockSpec((HY_ORDER, seq, dc), lambda j, b: (0, 0, j)),
                   pl.BlockSpec((HY_ORDER, 1, dc), lambda j, b: (0, 0, j)),
                   _const_spec((seq, seq)), _const_spec((seq, seq))])
    return pl.pallas_call(
        _hy_conv_kernel,
        grid=(nj, nbatch),
        in_specs=in_specs,
        out_specs=pl.BlockSpec((seq, dc), lambda j, b: (b, j)),
        out_shape=jax.ShapeDtypeStruct((nbatch * seq, D), BF),
        scratch_shapes=[pltpu.VMEM((seq, seq), BF), pltpu.VMEM((seq, seq), BF)],
        compiler_params=_cparams(2),
        name=f"hy_conv_{seq}",
    )(u, u, u, short_w, short_w, short_w, short_b, short_b, short_b, filter_bias, hre, him, hny, cmat, smat)


def _dft_tables(seq):
    k = np.arange(seq, dtype=np.int64)
    ang = np.pi * ((k[:, None] * k[None, :]) % (2 * seq)) / seq
    return jnp.asarray(np.cos(ang), F32), jnp.asarray(np.sin(ang), F32)


def _hy_embedding(seq):
    t = np.arange(seq, dtype=np.float32) / np.float32(seq)
    ang = (2.0 * math.pi) * t[:, None] * np.arange(1, HY_BANDS + 1, dtype=np.float32)
    emb = np.concatenate([t[:, None], np.cos(ang), np.sin(ang)], axis=-1).astype(np.float32)
    return jnp.asarray(np.pad(emb, ((0, 0), (0, HY_EMB_PAD - HY_EMB))))


def _post_kernel(*refs, split_x, split_out, tm):
    oc_ref, ol_ref = refs[0:2]
    x_refs, refs = (refs[2:4], refs[4:]) if split_x else (refs[2:3], refs[3:])
    mod_ref, wo_ref, g1_ref, b1_ref, w1c_ref, w2c_ref, g2_ref, b2_ref = refs[0:8]
    outs, (w1_ref, w2_ref) = refs[8:-2], refs[-2:]
    step = pl.program_id(0)
    is_lat = _is_lat(tm, N_FF_CHUNKS)
    nsub = tm // SUB_POST

    @pl.when(step < N_FF_CHUNKS)
    def _():
        per = MLP_CHUNK // FF_CHUNK
        w1_ref[step] = w1c_ref[...].astype(BF)
        w2_ref[step // per, pl.ds(pl.multiple_of((step % per) * FF_CHUNK, FF_CHUNK), FF_CHUNK), :] = (
            w2c_ref[...].astype(BF))

    def rows(j):
        return slice(j * SUB_POST, (j + 1) * SUB_POST)

    def pick(c_ref, l_ref, j):
        return jnp.where(is_lat, l_ref[rows(j), :], c_ref[rows(j), :])

    def norm1(j):
        a = _dot(pick(oc_ref, ol_ref, j), wo_ref[...])
        x = pick(x_refs[0], x_refs[1], j) if split_x else x_refs[0][rows(j), :]
        x1 = _layer_norm(DN_ALPHA * x + mod_ref[2:3, :] * a, g1_ref[...], b1_ref[...])
        return x1, _modulate(x1, mod_ref, 3, 4)

    def mlp_chunk(h, c):
        per = MLP_CHUNK // FF_CHUNK
        a = jnp.concatenate([_dot(h, w1_ref[per * c + i]) for i in range(per)], axis=1)
        a = jnp.maximum(a, 0.0)
        return _dot((a * a).astype(BF), w2_ref[c])

    def token_tile():
        ys = []

        def norm2_store(j, x1, acc):
            y = _layer_norm(DN_ALPHA * x1 + mod_ref[5:6, :] * acc, g2_ref[...], b2_ref[...])
            if split_out:
                ys.append(y)
            else:
                outs[0][rows(j), :] = y

        cur = norm1(0)
        prev = None
        for j in range(nsub):
            x1, h = cur
            acc = mlp_chunk(h, 0)
            if j + 1 < nsub:
                cur = norm1(j + 1)
            if prev is not None:
                norm2_store(j - 1, *prev)
            for c in range(1, D_FF // MLP_CHUNK):
                acc = acc + mlp_chunk(h, c)
            prev = (x1, acc)
        norm2_store(nsub - 1, *prev)
        if split_out:
            yc_ref, yl_ref = outs

            @pl.when(jnp.logical_not(is_lat))
            def _():
                for j, y in enumerate(ys):
                    yc_ref[rows(j), :] = y

            @pl.when(is_lat)
            def _():
                for j, y in enumerate(ys):
                    yl_ref[rows(j), :] = y

    pl.when(step >= N_FF_CHUNKS)(token_tile)


def _post(o_ctx, o_lat, xs, mods, layer, w_o, g1, b1, w1, w2, g2, b2, split_out):
    tm, off = TM_POST, N_FF_CHUNKS
    split_x = len(xs) == 2
    x_specs = [_ctx_spec(D, tm, off), _lat_spec(D, tm, off)] if split_x else [_tok_spec(D, tm, off)]
    vec = _const_spec((1, D))
    if split_out:
        out_specs = [_ctx_spec(D, tm, off), _lat_spec(D, tm, off)]
        out_shape = [jax.ShapeDtypeStruct((T_CTX, D), F32), jax.ShapeDtypeStruct((T_LAT, D), F32)]
    else:
        out_specs = _tok_spec(D, tm, off)
        out_shape = jax.ShapeDtypeStruct((T, D), F32)

    def chunk(i):
        return jnp.minimum(i, N_FF_CHUNKS - 1)

    return pl.pallas_call(
        functools.partial(_post_kernel, split_x=split_x, split_out=split_out, tm=tm),
        grid=(off + T // tm,),
        in_specs=[_ctx_spec(D, tm, off), _lat_spec(D, tm, off)] + x_specs + [
            _mod_spec(layer, tm, off), _const_spec((D, D)), vec, vec,
            pl.BlockSpec((None, D, FF_CHUNK), lambda i: (layer, 0, chunk(i))),
            pl.BlockSpec((None, FF_CHUNK, D), lambda i: (layer, chunk(i), 0)), vec, vec],
        out_specs=out_specs,
        out_shape=out_shape,
        scratch_shapes=[pltpu.VMEM((N_FF_CHUNKS, D, FF_CHUNK), BF),
                        pltpu.VMEM((D_FF // MLP_CHUNK, MLP_CHUNK, D), BF)],
        compiler_params=_cparams(1),
        name=f"post_l{layer}",
    )(o_ctx, o_lat, *xs, mods, w_o, g1, b1, w1, w2, g2, b2)


def _rope_tables():
    n = HEAD_DIM // 4
    pos = np.arange(DEC_SEQ)
    inv = (np.float32(ROPE_BASE) ** (-np.arange(n, dtype=np.float32) / np.float32(n))).astype(np.float32)
    ang_r = ((pos // GRID_W).astype(np.float32)[:, None] * inv).astype(np.float32)
    ang_c = ((pos % GRID_W).astype(np.float32)[:, None] * inv).astype(np.float32)
    cr, sr, cc, sc = np.cos(ang_r), np.sin(ang_r), np.cos(ang_c), np.sin(ang_c)
    a = np.tile(np.concatenate([cr, cr, cc, cc], axis=-1), (1, D // HEAD_DIM))
    b = np.tile(np.concatenate([-sr, sr, -sc, sc], axis=-1), (1, D // HEAD_DIM))
    a = np.concatenate([a, np.ones((TM, D), np.float32)], axis=0)
    b = np.concatenate([b, np.zeros((TM, D), np.float32)], axis=0)
    return jnp.asarray(a, F32), jnp.asarray(b, F32)


def kernel(x_prompt, x_sample, c, cache_da_k, cache_da_v, cache_na_k, cache_na_v, cache_gq_k, cache_gq_v, c_ctx, ada_w, ada_b, ln_g, ln_b, mlp_w1, mlp_w2, da_w_qkv, da_w_o, da_lambda, da_subln_g, na_w_qkv, na_w_o, na_rel_bias, gq_w_qkv, gq_w_o, gq_q_norm, gq_k_norm, hy_w_in, hy_short_w, hy_short_b, hy_ffn_w1, hy_ffn_b1, hy_ffn_w2, hy_ffn_b2, hy_ffn_freq, hy_ffn_w3, hy_log_decay, hy_filter_bias, hy_w_o):
    cvec = jnp.concatenate([c_ctx[None, :], c, jnp.zeros((MOD_ROWS - 1 - DEC_BATCH, D), F32)], axis=0)
    mods = _mods(cvec, ada_w, ada_b)
    rope_a, rope_b = _rope_tables()

    def finish(o_ctx, o_lat, xs, layer, w_o, split_out=False):
        return _post(o_ctx, o_lat, xs, mods, layer, w_o.astype(BF), ln_g[layer, 0][None], ln_b[layer, 0][None],
                     mlp_w1, mlp_w2, ln_g[layer, 1][None], ln_b[layer, 1][None], split_out)

    xs = (x_prompt.reshape(T_CTX, D), x_sample.reshape(T_LAT, D))
    qb, kb, vb, ks, vs = _da_proj(*xs, mods, 0, da_w_qkv[0].astype(BF), rope_a, rope_b)
    state_da_k = ks.reshape(BATCH, 1, SEQ, DA_HEADS, 2 * HEAD_DIM)
    state_da_v = vs.reshape(BATCH, 1, SEQ, DA_HEADS, 2 * HEAD_DIM)
    o_ctx, o_lat = _da_attention(qb, kb, vb, cache_da_k, cache_da_v, da_lambda[0], da_subln_g[0][None], 0)
    x = finish(o_ctx, o_lat, xs, 0, da_w_o[0])

    qb, kb, vb, ks, vs = _na_proj(x, mods, 1, na_w_qkv[0].astype(BF))
    state_na_k, state_na_v = _untranspose_state(ks, NA_HEADS), _untranspose_state(vs, NA_HEADS)
    onehot, neg, mask = _na_constants()
    bias_tab = _na_bias_table(na_rel_bias[0], onehot, neg)
    o_ctx = _na_ctx_attention(qb, kb, vb)
    o_lat = _na_lat_attention(qb, kb, vb, cache_na_k, cache_na_v, bias_tab, mask)
    x = finish(o_ctx, o_lat, (x,), 1, na_w_o[0])

    nk = GQ_KV_HEADS * HEAD_DIM
    g_mat = jnp.asarray(np.kron(np.eye(GN_BLOCK // HEAD_DIM), np.full((HEAD_DIM, HEAD_DIM), 1.0 / HEAD_DIM)), BF)
    qb, kb, vb, ks, vs = _gq_proj(x, mods, 2, gq_w_qkv[0].astype(BF), g_mat,
                                  jnp.tile(gq_q_norm[0], GQ_HEADS)[None], jnp.tile(gq_k_norm[0], GQ_KV_HEADS)[None],
                                  rope_a, rope_b)
    state_gq_k, state_gq_v = _untranspose_state(ks, GQ_KV_HEADS), _untranspose_state(vs, GQ_KV_HEADS)
    o_ctx, o_lat = _gq_attention(qb, kb, vb, cache_gq_k.reshape(DEC_BATCH, PAST, nk),
                                 cache_gq_v.reshape(DEC_BATCH, PAST, nk))
    x = finish(o_ctx, o_lat, (x,), 2, gq_w_o[0])

    u = _hy_proj(x, mods, 3, hy_w_in[0].astype(BF))
    w1 = jnp.pad(hy_ffn_w1[0], ((0, HY_EMB_PAD - HY_EMB), (0, 0)))
    zs = []
    for seq, nbatch, row0, dc in ((SEQ, BATCH, 0, D), (DEC_SEQ, DEC_BATCH, T_CTX, 256)):
        cmat, smat = _dft_tables(seq)
        hre, him, hny = _hy_filter(seq, _hy_embedding(seq), w1, hy_ffn_b1[0][None], hy_ffn_w2[0], hy_ffn_b2[0][None],
                                   hy_ffn_freq[0][None], hy_ffn_w3[0], hy_log_decay[0][None], cmat, smat)
        zs.append(_hy_conv(u, hy_short_w[0], hy_short_b[0][None], hy_filter_bias[0], hre, him, hny, cmat, smat,
                           seq, nbatch, row0, dc))
    y_ctx, y_lat = finish(zs[0], zs[1], (x,), 3, hy_w_o[0], split_out=True)

    return (y_ctx.reshape(BATCH, SEQ, D), y_lat.reshape(DEC_BATCH, DEC_SEQ, D),
            state_da_k, state_da_v, state_na_k, state_na_v, state_gq_k, state_gq_v)
```

```python
import functools
import math

import numpy as np
import jax
import jax.numpy as jnp
from jax import lax
from jax.experimental import pallas as pl
from jax.experimental.pallas import tpu as pltpu

F32 = jnp.float32
BF = jnp.bfloat16

D = 1024
BATCH = 16
SEQ = 256
DEC_BATCH = 8
DEC_SEQ = 1024
PAST = 256
DEPTH = 4
GRID_W = 64
GRID_ROWS = DEC_SEQ // GRID_W
D_FF = 4 * D
T_CTX = BATCH * SEQ
T_LAT = DEC_BATCH * DEC_SEQ
T = T_CTX + T_LAT
HEAD_DIM = 64
ATT_SCALE = HEAD_DIM ** -0.5
LOG2E = math.log2(math.e)
Q_SCALE = ATT_SCALE * LOG2E
DA_HEADS = 8
NA_HEADS = 16
NA_WIN_ROWS = 8
NA_WIN_COLS = 16
GQ_HEADS = 16
GQ_KV_HEADS = 4
HY_ORDER = 2
HY_BANDS = 16
HY_EMB = 1 + 2 * HY_BANDS
HY_EMB_PAD = 40
HY_FFN = 64
ROPE_BASE = 10000.0
LN_EPS = 1e-5
RMS_EPS = 1e-6
DN_ALPHA = (2 * DEPTH) ** 0.25
NEG_INF = -1e30

LANES = 128
TM = 512
TM_POST = 512
FF_CHUNK = 512
MLP_CHUNK = 1024
N_FF_CHUNKS = D_FF // FF_CHUNK
SUB_POST = 256
N_CTX_TILES = T_CTX // TM
N_TILES = T // TM
TQ = 512
MOD_ROWS = 16
VMEM_LIMIT = 56 * 1024 * 1024


def _cparams(n_axes, flags=None):
    return pltpu.CompilerParams(dimension_semantics=("arbitrary",) * n_axes,
                                vmem_limit_bytes=VMEM_LIMIT, flags=flags)


def _dot(a, b):
    return jnp.dot(a, b, preferred_element_type=F32)


def _dot_nt(a, b):
    return lax.dot_general(a, b, (((1,), (1,)), ((), ())), preferred_element_type=F32)


def _const_spec(shape):
    nd = len(shape)
    return pl.BlockSpec(shape, lambda *_: (0,) * nd, pipeline_mode=pl.Buffered(1))


def _mod_spec(layer, tm=TM, off=0):
    nctx = T_CTX // tm

    def row(i):
        t = jnp.maximum(i - off, 0)
        return jnp.where(t < nctx, 0, 1 + (t - nctx) // (DEC_SEQ // tm))

    return pl.BlockSpec((None, None, 6, D), lambda i: (layer, row(i), 0, 0))


def _tok_spec(width, tm=TM, off=0):
    return pl.BlockSpec((tm, width), lambda i: (jnp.maximum(i - off, 0), 0))


def _ctx_spec(width, tm=TM, off=0):
    return pl.BlockSpec((tm, width), lambda i: (jnp.clip(i - off, 0, T_CTX // tm - 1), 0))


def _lat_spec(width, tm=TM, off=0):
    return pl.BlockSpec((tm, width), lambda i: (jnp.maximum(i - off - T_CTX // tm, 0), 0))


def _is_lat(tm=TM, off=0):
    return pl.program_id(0) >= off + T_CTX // tm


def _pick(ctx_ref, lat_ref):
    return jnp.where(_is_lat(), lat_ref[...], ctx_ref[...])


def _layer_norm(r, g, b):
    mu = jnp.mean(r, axis=-1, keepdims=True)
    c = r - mu
    var = jnp.mean(c * c, axis=-1, keepdims=True)
    return c * lax.rsqrt(var + LN_EPS) * g + b


def _mods_kernel(c_ref, w_ref, b_ref, o_ref):
    c = c_ref[...]
    s = (c / (1.0 + jnp.exp(-c))).astype(BF)
    o_ref[...] = _dot(s, w_ref[...].astype(BF)) + b_ref[...]


def _mods(cvec, ada_w, ada_b):
    tn = 1536
    out = pl.pallas_call(
        _mods_kernel,
        grid=(DEPTH, 6 * D // tn),
        in_specs=[pl.BlockSpec((MOD_ROWS, D), lambda l, n: (0, 0)),
                  pl.BlockSpec((None, D, tn), lambda l, n: (l, 0, n)),
                  pl.BlockSpec((None, 1, tn), lambda l, n: (l, 0, n))],
        out_specs=pl.BlockSpec((None, MOD_ROWS, tn), lambda l, n: (l, 0, n)),
        out_shape=jax.ShapeDtypeStruct((DEPTH, MOD_ROWS, 6 * D), F32),
        compiler_params=_cparams(2),
        name="adaln_mods",
    )(cvec, ada_w, ada_b.reshape(DEPTH, 1, 6 * D))
    return out.reshape(DEPTH, MOD_ROWS, 6, D)


def _modulate(x, mod_ref, shift, scale):
    return (x * (1.0 + mod_ref[scale:scale + 1, :]) + mod_ref[shift:shift + 1, :]).astype(BF)


def _rope(x, a, b):
    n = x.shape[1]
    lane = lax.broadcasted_iota(jnp.int32, x.shape, 1)
    partner = jnp.where((lane & 16) == 0, pltpu.roll(x, n - 16, 1), pltpu.roll(x, 16, 1))
    return x * a + partner * b


def _rope_spec(width):
    per = DEC_SEQ // TM
    return pl.BlockSpec((TM, width), lambda i: (jnp.where(i < N_CTX_TILES, per, (i - N_CTX_TILES) % per), 0))


def _store_state(k, v, ks_ref, vs_ref, transposed):
    @pl.when(jnp.logical_not(_is_lat()))
    def _():
        if not transposed:
            ks_ref[...] = k
            vs_ref[...] = v
        else:
            n = k.shape[1]
            for x, ref in ((k, ks_ref), (v, vs_ref)):
                xt = x.T
                for j in range(TM // SEQ):
                    ref[j * n:(j + 1) * n, :] = xt[:, j * SEQ:(j + 1) * SEQ]


def _qkv_out(nq, nk, transposed_state):
    specs = [_tok_spec(nq), _tok_spec(nk), _tok_spec(nk)]
    shapes = [jax.ShapeDtypeStruct((T, nq), BF), jax.ShapeDtypeStruct((T, nk), BF), jax.ShapeDtypeStruct((T, nk), BF)]
    if transposed_state:
        rows = (TM // SEQ) * nk
        specs += [pl.BlockSpec((rows, SEQ), lambda i: (jnp.minimum(i, N_CTX_TILES - 1), 0))] * 2
        shapes += [jax.ShapeDtypeStruct((BATCH * nk, SEQ), F32)] * 2
    else:
        specs += [_ctx_spec(nk)] * 2
        shapes += [jax.ShapeDtypeStruct((T_CTX, nk), F32)] * 2
    return specs, shapes


def _features_major(cache):
    b, _, past, heads, dh = cache.shape
    return cache.transpose(0, 1, 3, 4, 2).reshape(b, heads * dh, past)


def _untranspose_state(st, heads):
    return st.reshape(BATCH, heads, HEAD_DIM, SEQ).transpose(0, 3, 1, 2)[:, None]


def _da_proj_kernel(xc_ref, xl_ref, mod_ref, w_ref, ra_ref, rb_ref, qb_ref, kb_ref, vb_ref, ks_ref, vs_ref):
    h = _modulate(_pick(xc_ref, xl_ref), mod_ref, 0, 1)
    a, b = ra_ref[...], rb_ref[...]
    q = _dot(h, w_ref[:, 0:D])
    k = _dot(h, w_ref[:, D:2 * D])
    qb_ref[...] = (_rope(q, a, b) * Q_SCALE).astype(BF)
    v = _dot(h, w_ref[:, 2 * D:3 * D])
    kb_ref[...] = _rope(k, a, b).astype(BF)
    vb_ref[...] = v.astype(BF)
    _store_state(k, v, ks_ref, vs_ref, False)


def _da_proj(x_ctx, x_lat, mods, layer, w, rope_a, rope_b):
    specs, shapes = _qkv_out(D, D, False)
    return pl.pallas_call(
        _da_proj_kernel,
        grid=(N_TILES,),
        in_specs=[_ctx_spec(D), _lat_spec(D), _mod_spec(layer), _const_spec((D, 3 * D)),
                  _rope_spec(D), _rope_spec(D)],
        out_specs=specs, out_shape=shapes,
        compiler_params=_cparams(1),
        name=f"da_proj_l{layer}",
    )(x_ctx, x_lat, mods, w, rope_a, rope_b)


def _na_proj_kernel(x_ref, mod_ref, w_ref, qb_ref, kb_ref, vb_ref, ks_ref, vs_ref):
    h = _modulate(x_ref[...], mod_ref, 0, 1)
    q = _dot(h, w_ref[:, 0:D])
    k = _dot(h, w_ref[:, D:2 * D])
    qb_ref[...] = (q * Q_SCALE).astype(BF)
    v = _dot(h, w_ref[:, 2 * D:3 * D])
    kb_ref[...] = k.astype(BF)
    vb_ref[...] = v.astype(BF)
    _store_state(k, v, ks_ref, vs_ref, True)


def _na_proj(x, mods, layer, w):
    specs, shapes = _qkv_out(D, D, True)
    return pl.pallas_call(
        _na_proj_kernel,
        grid=(N_TILES,),
        in_specs=[_tok_spec(D), _mod_spec(layer), _const_spec((D, 3 * D))],
        out_specs=specs, out_shape=shapes,
        compiler_params=_cparams(1),
        name=f"na_proj_l{layer}",
    )(x, mods, w)


GN_BLOCK = 256


def _head_rms(x, g_ref, gain):
    x2 = x * x
    hi = x2.astype(BF)
    lo = (x2 - hi.astype(F32)).astype(BF)
    g = g_ref[...]
    ms = jnp.concatenate(
        [_dot(hi[:, j:j + GN_BLOCK], g) + _dot(lo[:, j:j + GN_BLOCK], g) for j in range(0, x.shape[1], GN_BLOCK)],
        axis=1)
    return x * lax.rsqrt(ms + RMS_EPS) * gain


def _gq_proj_kernel(x_ref, mod_ref, w_ref, g_ref, qn_ref, kn_ref, ra_ref, rb_ref,
                    qb_ref, kb_ref, vb_ref, ks_ref, vs_ref):
    nq, nk = GQ_HEADS * HEAD_DIM, GQ_KV_HEADS * HEAD_DIM
    h = _modulate(x_ref[...], mod_ref, 0, 1)
    a, b = ra_ref[...], rb_ref[...]
    q = _dot(h, w_ref[:, 0:nq])
    k = _dot(h, w_ref[:, nq:nq + nk])
    v = _dot(h, w_ref[:, nq + nk:nq + 2 * nk])
    k = _head_rms(k, g_ref, kn_ref[...])
    q = _head_rms(q, g_ref, qn_ref[...])
    kb_ref[...] = _rope(k, a[:, 0:nk], b[:, 0:nk]).astype(BF)
    qb_ref[...] = (_rope(q, a, b) * Q_SCALE).astype(BF)
    vb_ref[...] = v.astype(BF)
    _store_state(k, v, ks_ref, vs_ref, True)


def _gq_proj(x, mods, layer, w, g_mat, qn, kn, rope_a, rope_b):
    nq, nk = GQ_HEADS * HEAD_DIM, GQ_KV_HEADS * HEAD_DIM
    specs, shapes = _qkv_out(nq, nk, True)
    return pl.pallas_call(
        _gq_proj_kernel,
        grid=(N_TILES,),
        in_specs=[_tok_spec(D), _mod_spec(layer), _const_spec((D, nq + 2 * nk)),
                  _const_spec((GN_BLOCK, GN_BLOCK)), _const_spec((1, nq)), _const_spec((1, nk)),
                  _rope_spec(D), _rope_spec(D)],
        out_specs=specs, out_shape=shapes,
        compiler_params=_cparams(1),
        name=f"gq_proj_l{layer}",
    )(x, mods, w, g_mat, qn, kn, rope_a, rope_b)


def _hy_proj_kernel(x_ref, mod_ref, w_ref, u_ref):
    h = _modulate(x_ref[...], mod_ref, 0, 1)
    for c in range(HY_ORDER + 1):
        u_ref[:, c * D:(c + 1) * D] = _dot(h, w_ref[:, c * D:(c + 1) * D])


def _hy_proj(x, mods, layer, w):
    n = (HY_ORDER + 1) * D
    return pl.pallas_call(
        _hy_proj_kernel,
        grid=(N_TILES,),
        in_specs=[_tok_spec(D), _mod_spec(layer), _const_spec((D, n))],
        out_specs=_tok_spec(n),
        out_shape=jax.ShapeDtypeStruct((T, n), F32),
        compiler_params=_cparams(1),
        name=f"hy_proj_l{layer}",
    )(x, mods, w)


def _softmax_pv(qm, segs):
    return _softmax_finish(_scores(qm, segs), segs)


def _scores(qm, segs):
    return [_dot(qm, seg[0]) if len(seg) == 3 else _dot_nt(qm, seg[0]) for seg in segs]


def _softmax_finish(scores, segs):
    m = scores[0].max(axis=-1, keepdims=True)
    for s in scores[1:]:
        m = jnp.maximum(m, s.max(axis=-1, keepdims=True))
    den = None
    out = None
    for s, seg in zip(scores, segs):
        e = jnp.exp2(s - m)
        d = e.sum(axis=-1, keepdims=True)
        o = _dot_nt(e.astype(BF), seg[1]) if len(seg) == 3 else _dot(e.astype(BF), seg[1])
        den = d if den is None else den + d
        out = o if out is None else out + o
    return out / den


def _pipelined(jobs, score_fn, finish_fn):
    nxt = score_fn(jobs[0])
    for n, job in enumerate(jobs):
        cur, nxt = nxt, (score_fn(jobs[n + 1]) if n + 1 < len(jobs) else None)
        finish_fn(job, cur)


def _lane_half(shape):
    return lax.broadcasted_iota(jnp.int32, shape, 1) // HEAD_DIM


def _half_keep(half):
    return tuple(jnp.where(half == a, 1.0, 0.0).astype(BF) for a in (0, 1))


def _da_attn_kernel(*refs, has_cache, lam_init):
    if has_cache:
        q_ref, k_ref, v_ref, ck_ref, cv_ref, lam_ref, g_ref, o_ref = refs
    else:
        q_ref, k_ref, v_ref, lam_ref, g_ref, o_ref = refs
    lp = lam_ref[...]
    lam = (jnp.exp(jnp.sum(lp[0:1] * lp[1:2], axis=-1, keepdims=True))
           - jnp.exp(jnp.sum(lp[2:3] * lp[3:4], axis=-1, keepdims=True)) + lam_init)
    gain = g_ref[...] * (1.0 - lam_init)
    w = 2 * HEAD_DIM
    tq = min(TQ, q_ref.shape[0])
    keep = _half_keep(_lane_half((tq, w)))
    segs = []
    for hd in range(k_ref.shape[1] // w):
        cols = slice(hd * w, (hd + 1) * w)
        seg = [(k_ref[:, cols], v_ref[:, cols])]
        if has_cache:
            head = pl.program_id(1)
            seg.append((ck_ref[:, head, :].astype(BF), cv_ref[:, head, :].astype(BF)))
        segs.append(seg)
    jobs = [(hd, t, a) for hd in range(len(segs)) for t in range(q_ref.shape[0] // tq) for a in (0, 1)]
    first = {}

    def score_fn(job):
        hd, t, a = job
        return _scores(q_ref[t * tq:(t + 1) * tq, hd * w:(hd + 1) * w] * keep[a], segs[hd])

    def finish_fn(job, scores):
        hd, t, a = job
        o = _softmax_finish(scores, segs[hd])
        if a == 0:
            first[0] = o
            return
        o = first[0] - lam * o
        ms = jnp.mean(o * o, axis=-1, keepdims=True)
        o_ref[t * tq:(t + 1) * tq, hd * w:(hd + 1) * w] = (o * lax.rsqrt(ms + RMS_EPS) * gain).astype(BF)

    _pipelined(jobs, score_fn, finish_fn)


def _da_attention(qb, kb, vb, cache_k, cache_v, lam_p, subln_g, layer_idx):
    lam_init = 0.8 - 0.6 * math.exp(-0.3 * layer_idx)
    w = 2 * HEAD_DIM
    small = [pl.BlockSpec((4, HEAD_DIM), lambda *_: (0, 0)), pl.BlockSpec((1, w), lambda *_: (0, 0))]
    o_ctx = pl.pallas_call(
        functools.partial(_da_attn_kernel, has_cache=False, lam_init=lam_init),
        grid=(BATCH,),
        in_specs=[pl.BlockSpec((SEQ, D), lambda b: (b, 0))] * 3 + small,
        out_specs=pl.BlockSpec((SEQ, D), lambda b: (b, 0)),
        out_shape=jax.ShapeDtypeStruct((T_CTX, D), BF),
        compiler_params=_cparams(1),
        name="da_attn_ctx",
    )(qb, kb, vb, lam_p, subln_g)
    k0 = T_CTX // DEC_SEQ
    tok = pl.BlockSpec((DEC_SEQ, w), lambda b, h: (k0 + b, h))
    c_spec = pl.BlockSpec((None, None, PAST, DA_HEADS, w), lambda b, h: (b, 0, 0, 0, 0))
    o_lat = pl.pallas_call(
        functools.partial(_da_attn_kernel, has_cache=True, lam_init=lam_init),
        grid=(DEC_BATCH, DA_HEADS),
        in_specs=[tok, tok, tok, c_spec, c_spec] + small,
        out_specs=pl.BlockSpec((DEC_SEQ, w), lambda b, h: (b, h)),
        out_shape=jax.ShapeDtypeStruct((T_LAT, D), BF),
        compiler_params=_cparams(2),
        name="da_attn_lat",
    )(qb, kb, vb, cache_k, cache_v, lam_p, subln_g)
    return o_ctx, o_lat


def _na_ctx_kernel(q_ref, k_ref, v_ref, o_ref):
    half = _lane_half((SEQ, LANES))
    keep = _half_keep(half)
    jobs = [(p, a) for p in range(NA_HEADS // 2) for a in (0, 1)]
    first = {}

    def seg(p):
        return [(k_ref[:, p * LANES:(p + 1) * LANES], v_ref[:, p * LANES:(p + 1) * LANES])]

    def score_fn(job):
        p, a = job
        return _scores(q_ref[:, p * LANES:(p + 1) * LANES] * keep[a], seg(p))

    def finish_fn(job, scores):
        p, a = job
        o = _softmax_finish(scores, seg(p))
        if a == 0:
            first[0] = o
        else:
            o_ref[:, p * LANES:(p + 1) * LANES] = jnp.where(half == 0, first[0], o).astype(BF)

    _pipelined(jobs, score_fn, finish_fn)


def _na_ctx_attention(qb, kb, vb):
    spec = pl.BlockSpec((SEQ, D), lambda b: (b, 0))
    return pl.pallas_call(
        _na_ctx_kernel,
        grid=(BATCH,),
        in_specs=[spec] * 3,
        out_specs=spec,
        out_shape=jax.ShapeDtypeStruct((T_CTX, D), BF),
        compiler_params=_cparams(1),
        name="na_attn_ctx",
    )(qb, kb, vb)


NA_TILES = ((0, (0, 2, 4, 6)), (4, (0, 2, 4, 6, 8, 10)), (8, (4, 6, 8, 10, 12, 14)), (12, (8, 10, 12, 14)))
NA_MAX_CHUNKS = 6
NA_BIAS_BLOCKS = 2 * NA_WIN_ROWS - 2


def _na_lat_kernel(q_ref, k_ref, v_ref, ck_ref, cv_ref, w_ref, m_ref, o_ref):
    ckt, cvt = ck_ref[...].astype(BF), cv_ref[...].astype(BF)
    rows = 4 * GRID_W
    half = _lane_half((rows, LANES))
    keep = _half_keep(half)
    jobs = [(i, a) for i in range(len(NA_TILES)) for a in (0, 1)]
    first = {}

    def key_rows(i):
        chunks = NA_TILES[i][1]
        return slice(chunks[0] * GRID_W, chunks[0] * GRID_W + len(chunks) * LANES)

    def score_fn(job):
        i, a = job
        r0, chunks = NA_TILES[i]
        qm = q_ref[i * rows:(i + 1) * rows, :] * keep[a]
        bias = jnp.concatenate(
            [w_ref[a, (6 - kr + r0) * GRID_W:(6 - kr + r0) * GRID_W + rows, :] for kr in chunks], axis=1)
        s_loc = _dot_nt(qm, k_ref[key_rows(i), :]) + bias + m_ref[i, :, 0:len(chunks) * LANES]
        return [s_loc, _dot(qm, ckt)]

    def finish_fn(job, scores):
        i, a = job
        o = _softmax_finish(scores, [(None, v_ref[key_rows(i), :]), (None, cvt, True)])
        if a == 0:
            first[0] = o
        else:
            o_ref[i * rows:(i + 1) * rows, :] = jnp.where(half == 0, first[0], o).astype(BF)

    _pipelined(jobs, score_fn, finish_fn)


def _na_lat_attention(qb, kb, vb, cache_k, cache_v, bias_tab, mask_tab):
    k0 = T_CTX // DEC_SEQ
    tok = pl.BlockSpec((DEC_SEQ, LANES), lambda b, p: (k0 + b, p))
    c_spec = pl.BlockSpec((None, LANES, PAST), lambda b, p: (b, p, 0))
    return pl.pallas_call(
        _na_lat_kernel,
        grid=(DEC_BATCH, NA_HEADS // 2),
        in_specs=[tok, tok, tok, c_spec, c_spec,
                  pl.BlockSpec((None, 2, NA_BIAS_BLOCKS * GRID_W, LANES), lambda b, p: (p, 0, 0, 0)),
                  _const_spec(mask_tab.shape)],
        out_specs=pl.BlockSpec((DEC_SEQ, LANES), lambda b, p: (b, p)),
        out_shape=jax.ShapeDtypeStruct((T_LAT, D), BF),
        compiler_params=_cparams(2),
        name="na_attn_lat",
    )(qb, kb, vb, cache_k, cache_v, bias_tab, mask_tab)


def _na_bias_kernel(t_ref, r_ref, n_ref, o_ref):
    t = t_ref[...]
    t1 = t.astype(BF)
    r1 = t - t1.astype(F32)
    t2 = r1.astype(BF)
    t3 = (r1 - t2.astype(F32)).astype(BF)
    r = r_ref[...]
    res = (_dot(t1, r) + _dot(t2, r) + _dot(t3, r) + n_ref[...]) * LOG2E
    for qc in range(GRID_W):
        o_ref[pl.ds(qc, t.shape[0], stride=GRID_W), :] = res[:, qc * LANES:(qc + 1) * LANES]


def _na_bias_table(rel_bias, onehot, neg):
    nrel = 2 * NA_WIN_COLS
    idx = 13 - np.arange(NA_BIAS_BLOCKS)[:, None] + np.arange(2)[None, :]
    t = jnp.pad(rel_bias[:, idx, :], ((0, 0), (0, 0), (0, 0), (0, 1)))
    t = t.reshape(NA_HEADS * NA_BIAS_BLOCKS, 2 * nrel)
    n = GRID_W * LANES
    out = pl.pallas_call(
        _na_bias_kernel,
        grid=(1,),
        in_specs=[pl.BlockSpec(t.shape, lambda j: (0, 0)),
                  pl.BlockSpec((2 * nrel, n), lambda j: (0, 0)),
                  pl.BlockSpec((1, n), lambda j: (0, 0))],
        out_specs=pl.BlockSpec((t.shape[0] * GRID_W, LANES), lambda j: (0, 0)),
        out_shape=jax.ShapeDtypeStruct((t.shape[0] * GRID_W, LANES), F32),
        compiler_params=_cparams(1),
        name="na_bias_table",
    )(t, onehot, neg)
    return out.reshape(NA_HEADS // 2, 2, NA_BIAS_BLOCKS * GRID_W, LANES)


def _na_constants():
    nrel = 2 * NA_WIN_COLS
    qc = np.arange(GRID_W)[:, None]
    kc = np.arange(GRID_W)[None, :]
    rel = np.clip(kc - qc, -(NA_WIN_COLS - 1), NA_WIN_COLS - 1) + NA_WIN_COLS - 1
    cs = np.clip(qc - NA_WIN_COLS // 2, 0, GRID_W - NA_WIN_COLS)
    col_in = (kc >= cs) & (kc < cs + NA_WIN_COLS)
    onehot = np.zeros((2, nrel, GRID_W, 2, GRID_W), np.float32)
    for hf in range(2):
        onehot[hf, rel, qc, hf, kc] = 1.0
    neg = np.where(col_in, 0.0, NEG_INF).astype(np.float32)
    neg = np.broadcast_to(neg[:, None, :], (GRID_W, 2, GRID_W)).reshape(1, -1)
    rows = 4 * GRID_W
    mask = np.full((len(NA_TILES), rows, NA_MAX_CHUNKS * LANES), NEG_INF, np.float32)
    kr = min(NA_WIN_ROWS, GRID_ROWS)
    for i, (r0, chunks) in enumerate(NA_TILES):
        qr = r0 + np.arange(rows)[:, None] // GRID_W
        rs = np.clip(qr - kr // 2, 0, GRID_ROWS - kr)
        for c, krow0 in enumerate(chunks):
            krow = krow0 + np.arange(LANES)[None, :] // GRID_W
            mask[i, :, c * LANES:(c + 1) * LANES] = np.where((krow >= rs) & (krow < rs + kr), 0.0, NEG_INF)
    return (jnp.asarray(onehot.reshape(2 * nrel, GRID_W * LANES), BF), jnp.asarray(neg), jnp.asarray(mask))


def _gq_attn_kernel(*refs, has_cache):
    if has_cache:
        q_ref, k_ref, v_ref, ck_ref, cv_ref, o_ref = refs
    else:
        q_ref, k_ref, v_ref, o_ref = refs
    group = GQ_HEADS // GQ_KV_HEADS
    qw = LANES * group
    half = _lane_half((q_ref.shape[0], LANES))
    keep = _half_keep(half)
    for kvp in range(k_ref.shape[1] // LANES):
        kcols = slice(kvp * LANES, (kvp + 1) * LANES)
        segs = [(k_ref[:, kcols], v_ref[:, kcols])]
        if has_cache:
            segs.append((ck_ref[kcols, :].astype(BF), cv_ref[kcols, :].astype(BF), True))
        first = {}

        def score_fn(j):
            pair, a = divmod(j, 2)
            qm = q_ref[:, kvp * qw + pair * LANES:kvp * qw + (pair + 1) * LANES] * keep[a]
            if a != (2 * pair) // group:
                qm = pltpu.roll(qm.astype(F32), HEAD_DIM, 1).astype(BF)
            return _scores(qm, segs)

        def finish_fn(j, scores):
            pair, a = divmod(j, 2)
            o = _softmax_finish(scores, segs)
            if a != (2 * pair) // group:
                o = pltpu.roll(o, HEAD_DIM, 1)
            if a == 0:
                first[0] = o
            else:
                cols = slice(kvp * qw + pair * LANES, kvp * qw + (pair + 1) * LANES)
                o_ref[:, cols] = jnp.where(half == 0, first[0], o).astype(BF)

        _pipelined(list(range(2 * group)), score_fn, finish_fn)


def _gq_attention(qb, kb, vb, cache_k, cache_v):
    nk = GQ_KV_HEADS * HEAD_DIM
    qw = LANES * (GQ_HEADS // GQ_KV_HEADS)
    npair = GQ_KV_HEADS // 2
    o_ctx = pl.pallas_call(
        functools.partial(_gq_attn_kernel, has_cache=False),
        grid=(BATCH,),
        in_specs=[pl.BlockSpec((SEQ, D), lambda b: (b, 0))] + [pl.BlockSpec((SEQ, nk), lambda b: (b, 0))] * 2,
        out_specs=pl.BlockSpec((SEQ, D), lambda b: (b, 0)),
        out_shape=jax.ShapeDtypeStruct((T_CTX, D), BF),
        compiler_params=_cparams(1),
        name="gq_attn_ctx",
    )(qb, kb, vb)
    qt = DEC_SEQ // TQ
    q0, k0 = T_CTX // TQ, T_CTX // DEC_SEQ
    kv_spec = pl.BlockSpec((DEC_SEQ, LANES), lambda b, p, t: (k0 + b, p))
    c_spec = pl.BlockSpec((None, LANES, PAST), lambda b, p, t: (b, p, 0))
    o_lat = pl.pallas_call(
        functools.partial(_gq_attn_kernel, has_cache=True),
        grid=(DEC_BATCH, npair, qt),
        in_specs=[pl.BlockSpec((TQ, qw), lambda b, p, t: (q0 + b * qt + t, p)), kv_spec, kv_spec, c_spec, c_spec],
        out_specs=pl.BlockSpec((TQ, qw), lambda b, p, t: (b * qt + t, p)),
        out_shape=jax.ShapeDtypeStruct((T_LAT, D), BF),
        compiler_params=_cparams(3),
        name="gq_attn_lat",
    )(qb, kb, vb, cache_k, cache_v)
    return o_ctx, o_lat


def _hy_filter_kernel(emb_ref, w1_ref, b1_ref, w2_ref, b2_ref, fr_ref, w3f_ref, w3b_ref, ldf_ref, ldb_ref,
                      c_ref, s_ref, hre_ref, him_ref, hny_ref):
    seq = emb_ref.shape[0]
    hp = lax.Precision.HIGHEST
    emb = emb_ref[...]
    fr = fr_ref[...]
    hid = jnp.sin(fr * (jnp.dot(emb, w1_ref[...], precision=hp, preferred_element_type=F32) + b1_ref[...]))
    hid = jnp.sin(fr * (jnp.dot(hid, w2_ref[...], precision=hp, preferred_element_type=F32) + b2_ref[...]))
    t = emb[:, 0:1]
    fwd = jnp.dot(hid, w3f_ref[...], precision=hp, preferred_element_type=F32) * jnp.exp(-jnp.exp(ldf_ref[...]) * t)
    bwd = jnp.dot(hid, w3b_ref[...], precision=hp, preferred_element_type=F32) * jnp.exp(-jnp.exp(ldb_ref[...]) * t)
    row = lax.broadcasted_iota(jnp.int32, fwd.shape, 0)
    bwd = jnp.where(row == 0, 0.0, bwd)
    even = fwd + bwd
    odd = bwd - fwd
    wk = jnp.where(row == 0, 0.5 / seq, 1.0 / seq)
    hre_ref[...] = _dot(c_ref[...].astype(BF), even.astype(BF)) * wk
    him_ref[...] = _dot(s_ref[...].astype(BF), odd.astype(BF)) * wk
    alt = jnp.where((row & 1) == 0, 1.0, -1.0)
    hny_ref[...] = jnp.sum(alt * even, axis=0, keepdims=True) * (0.5 / seq)


def _hy_filter(seq, emb, w1, b1, w2, b2, freq, w3, log_decay, cmat, smat):
    dc = 512
    nj = D // dc
    small = [_const_spec(a.shape) for a in (emb, w1, b1, w2, b2, freq)]
    return pl.pallas_call(
        _hy_filter_kernel,
        grid=(HY_ORDER, nj),
        in_specs=small + [pl.BlockSpec((HY_FFN, dc), lambda o, j: (0, (2 * o) * nj + j)),
                          pl.BlockSpec((HY_FFN, dc), lambda o, j: (0, (2 * o + 1) * nj + j)),
                          pl.BlockSpec((1, dc), lambda o, j: (0, (2 * o) * nj + j)),
                          pl.BlockSpec((1, dc), lambda o, j: (0, (2 * o + 1) * nj + j)),
                          _const_spec((seq, seq)), _const_spec((seq, seq))],
        out_specs=[pl.BlockSpec((None, seq, dc), lambda o, j: (o, 0, j)),
                   pl.BlockSpec((None, seq, dc), lambda o, j: (o, 0, j)),
                   pl.BlockSpec((None, 1, dc), lambda o, j: (o, 0, j))],
        out_shape=[jax.ShapeDtypeStruct((HY_ORDER, seq, D), F32), jax.ShapeDtypeStruct((HY_ORDER, seq, D), F32),
                   jax.ShapeDtypeStruct((HY_ORDER, 1, D), F32)],
        compiler_params=_cparams(2),
        name=f"hy_filter_{seq}",
    )(emb, w1, b1, w2, b2, freq, w3, w3, log_decay, log_decay, cmat, smat)


def _hy_conv_kernel(u0_ref, u1_ref, u2_ref, sw0_ref, sw1_ref, sw2_ref, sb0_ref, sb1_ref, sb2_ref,
                    fb_ref, hre_ref, him_ref, hny_ref, c_ref, s_ref, o_ref, cb_ref, sb_ref):
    seq = u0_ref.shape[0]

    @pl.when((pl.program_id(0) == 0) & (pl.program_id(1) == 0))
    def _():
        cb_ref[...] = c_ref[...].astype(BF)
        sb_ref[...] = s_ref[...].astype(BF)

    row = lax.broadcasted_iota(jnp.int32, u0_ref.shape, 0)
    alt = jnp.where((row & 1) == 0, 1.0, -1.0)

    def short_conv(u_ref, w_ref, b_ref):
        u = u_ref[...]
        prev = jnp.where(row == 0, 0.0, pltpu.roll(u, 1, 0))
        nxt = jnp.where(row == seq - 1, 0.0, pltpu.roll(u, seq - 1, 0))
        return prev * w_ref[0:1, :] + u * w_ref[1:2, :] + nxt * w_ref[2:3, :] + b_ref[...]

    cm, sm = cb_ref[...], sb_ref[...]
    z = short_conv(u0_ref, sw0_ref, sb0_ref)
    gates = (short_conv(u1_ref, sw1_ref, sb1_ref), short_conv(u2_ref, sw2_ref, sb2_ref))
    for o in range(HY_ORDER):
        zb = z.astype(BF)
        zc, zs = _dot(cm, zb), _dot(sm, zb)
        hre, him = hre_ref[o], him_ref[o]
        p_re = zc * hre + zs * him
        p_im = zc * him - zs * hre
        nyq = jnp.sum(alt * z, axis=0, keepdims=True) * hny_ref[o]
        y = _dot(cm, p_re.astype(BF)) - _dot(sm, p_im.astype(BF)) + alt * nyq
        z = gates[o] * (y + z * fb_ref[o:o + 1, :])
    o_ref[...] = z.astype(BF)


def _hy_conv(u, short_w, short_b, filter_bias, hre, him, hny, cmat, smat, seq, nbatch, row0, dc):
    nj = D // dc
    r0 = row0 // seq

    def part(p):
        return pl.BlockSpec((seq, dc), lambda j, b: (r0 + b, p * nj + j))

    def vec(rows, p):
        return pl.BlockSpec((rows, dc), lambda j, b: (0, p * nj + j))

    in_specs = ([part(p) for p in range(3)] + [vec(3, p) for p in range(3)] + [vec(1, p) for p in range(3)]
                + [pl.BlockSpec((HY_ORDER, dc), lambda j, b: (0, j)),
                   pl.BlockSpec((HY_ORDER, seq, dc), lambda j, b: (0, 0, j)),
                   pl.BlockSpec((HY_ORDER, seq, dc), lambda j, b: (0, 0, j)),
                   pl.BlockSpec((HY_ORDER, 1, dc), lambda j, b: (0, 0, j)),
                   _const_spec((seq, seq)), _const_spec((seq, seq))])
    return pl.pallas_call(
        _hy_conv_kernel,
        grid=(nj, nbatch),
        in_specs=in_specs,
        out_specs=pl.BlockSpec((seq, dc), lambda j, b: (b, j)),
        out_shape=jax.ShapeDtypeStruct((nbatch * seq, D), BF),
        scratch_shapes=[pltpu.VMEM((seq, seq), BF), pltpu.VMEM((seq, seq), BF)],
        compiler_params=_cparams(2),
        name=f"hy_conv_{seq}",
    )(u, u, u, short_w, short_w, short_w, short_b, short_b, short_b, filter_bias, hre, him, hny, cmat, smat)


def _dft_tables(seq):
    k = np.arange(seq, dtype=np.int64)
    ang = np.pi * ((k[:, None] * k[None, :]) % (2 * seq)) / seq
    return jnp.asarray(np.cos(ang), F32), jnp.asarray(np.sin(ang), F32)


def _hy_embedding(seq):
    t = np.arange(seq, dtype=np.float32) / np.float32(seq)
    ang = (2.0 * math.pi) * t[:, None] * np.arange(1, HY_BANDS + 1, dtype=np.float32)
    emb = np.concatenate([t[:, None], np.cos(ang), np.sin(ang)], axis=-1).astype(np.float32)
    return jnp.asarray(np.pad(emb, ((0, 0), (0, HY_EMB_PAD - HY_EMB))))


def _post_kernel(*refs, split_x, split_out, tm):
    oc_ref, ol_ref = refs[0:2]
    x_refs, refs = (refs[2:4], refs[4:]) if split_x else (refs[2:3], refs[3:])
    mod_ref, wo_ref, g1_ref, b1_ref, w1c_ref, w2c_ref, g2_ref, b2_ref = refs[0:8]
    outs, (w1_ref, w2_ref) = refs[8:-2], refs[-2:]
    step = pl.program_id(0)
    is_lat = _is_lat(tm, N_FF_CHUNKS)
    nsub = tm // SUB_POST

    @pl.when(step < N_FF_CHUNKS)
    def _():
        per = MLP_CHUNK // FF_CHUNK
        w1_ref[step] = w1c_ref[...].astype(BF)
        w2_ref[step // per, pl.ds(pl.multiple_of((step % per) * FF_CHUNK, FF_CHUNK), FF_CHUNK), :] = (
            w2c_ref[...].astype(BF))

    def rows(j):
        return slice(j * SUB_POST, (j + 1) * SUB_POST)

    def pick(c_ref, l_ref, j):
        return jnp.where(is_lat, l_ref[rows(j), :], c_ref[rows(j), :])

    def norm1(j):
        a = _dot(pick(oc_ref, ol_ref, j), wo_ref[...])
        x = pick(x_refs[0], x_refs[1], j) if split_x else x_refs[0][rows(j), :]
        x1 = _layer_norm(DN_ALPHA * x + mod_ref[2:3, :] * a, g1_ref[...], b1_ref[...])
        return x1, _modulate(x1, mod_ref, 3, 4)

    def mlp_chunk(h, c):
        per = MLP_CHUNK // FF_CHUNK
        a = jnp.concatenate([_dot(h, w1_ref[per * c + i]) for i in range(per)], axis=1)
        a = jnp.maximum(a, 0.0)
        return _dot((a * a).astype(BF), w2_ref[c])

    def token_tile():
        ys = []

        def norm2_store(j, x1, acc):
            y = _layer_norm(DN_ALPHA * x1 + mod_ref[5:6, :] * acc, g2_ref[...], b2_ref[...])
            if split_out:
                ys.append(y)
            else:
                outs[0][rows(j), :] = y

        cur = norm1(0)
        prev = None
        for j in range(nsub):
            x1, h = cur
            acc = mlp_chunk(h, 0)
            if j + 1 < nsub:
                cur = norm1(j + 1)
            if prev is not None:
                norm2_store(j - 1, *prev)
            for c in range(1, D_FF // MLP_CHUNK):
                acc = acc + mlp_chunk(h, c)
            prev = (x1, acc)
        norm2_store(nsub - 1, *prev)
        if split_out:
            yc_ref, yl_ref = outs

            @pl.when(jnp.logical_not(is_lat))
            def _():
                for j, y in enumerate(ys):
                    yc_ref[rows(j), :] = y

            @pl.when(is_lat)
            def _():
                for j, y in enumerate(ys):
                    yl_ref[rows(j), :] = y

    pl.when(step >= N_FF_CHUNKS)(token_tile)


def _post(o_ctx, o_lat, xs, mods, layer, w_o, g1, b1, w1, w2, g2, b2, split_out):
    tm, off = TM_POST, N_FF_CHUNKS
    split_x = len(xs) == 2
    x_specs = [_ctx_spec(D, tm, off), _lat_spec(D, tm, off)] if split_x else [_tok_spec(D, tm, off)]
    vec = _const_spec((1, D))
    if split_out:
        out_specs = [_ctx_spec(D, tm, off), _lat_spec(D, tm, off)]
        out_shape = [jax.ShapeDtypeStruct((T_CTX, D), F32), jax.ShapeDtypeStruct((T_LAT, D), F32)]
    else:
        out_specs = _tok_spec(D, tm, off)
        out_shape = jax.ShapeDtypeStruct((T, D), F32)

    def chunk(i):
        return jnp.minimum(i, N_FF_CHUNKS - 1)

    return pl.pallas_call(
        functools.partial(_post_kernel, split_x=split_x, split_out=split_out, tm=tm),
        grid=(off + T // tm,),
        in_specs=[_ctx_spec(D, tm, off), _lat_spec(D, tm, off)] + x_specs + [
            _mod_spec(layer, tm, off), _const_spec((D, D)), vec, vec,
            pl.BlockSpec((None, D, FF_CHUNK), lambda i: (layer, 0, chunk(i))),
            pl.BlockSpec((None, FF_CHUNK, D), lambda i: (layer, chunk(i), 0)), vec, vec],
        out_specs=out_specs,
        out_shape=out_shape,
        scratch_shapes=[pltpu.VMEM((N_FF_CHUNKS, D, FF_CHUNK), BF),
                        pltpu.VMEM((D_FF // MLP_CHUNK, MLP_CHUNK, D), BF)],
        compiler_params=_cparams(1),
        name=f"post_l{layer}",
    )(o_ctx, o_lat, *xs, mods, w_o, g1, b1, w1, w2, g2, b2)


def _rope_tables():
    n = HEAD_DIM // 4
    pos = np.arange(DEC_SEQ)
    inv = (np.float32(ROPE_BASE) ** (-np.arange(n, dtype=np.float32) / np.float32(n))).astype(np.float32)
    ang_r = ((pos // GRID_W).astype(np.float32)[:, None] * inv).astype(np.float32)
    ang_c = ((pos % GRID_W).astype(np.float32)[:, None] * inv).astype(np.float32)
    cr, sr, cc, sc = np.cos(ang_r), np.sin(ang_r), np.cos(ang_c), np.sin(ang_c)
    a = np.tile(np.concatenate([cr, cr, cc, cc], axis=-1), (1, D // HEAD_DIM))
    b = np.tile(np.concatenate([-sr, sr, -sc, sc], axis=-1), (1, D // HEAD_DIM))
    a = np.concatenate([a, np.ones((TM, D), np.float32)], axis=0)
    b = np.concatenate([b, np.zeros((TM, D), np.float32)], axis=0)
    return jnp.asarray(a, F32), jnp.asarray(b, F32)


def kernel(x_prompt, x_sample, c, cache_da_k, cache_da_v, cache_na_k, cache_na_v, cache_gq_k, cache_gq_v, c_ctx, ada_w, ada_b, ln_g, ln_b, mlp_w1, mlp_w2, da_w_qkv, da_w_o, da_lambda, da_subln_g, na_w_qkv, na_w_o, na_rel_bias, gq_w_qkv, gq_w_o, gq_q_norm, gq_k_norm, hy_w_in, hy_short_w, hy_short_b, hy_ffn_w1, hy_ffn_b1, hy_ffn_w2, hy_ffn_b2, hy_ffn_freq, hy_ffn_w3, hy_log_decay, hy_filter_bias, hy_w_o):
    cvec = jnp.concatenate([c_ctx[None, :], c, jnp.zeros((MOD_ROWS - 1 - DEC_BATCH, D), F32)], axis=0)
    mods = _mods(cvec, ada_w, ada_b)
    rope_a, rope_b = _rope_tables()

    def finish(o_ctx, o_lat, xs, layer, w_o, split_out=False):
        return _post(o_ctx, o_lat, xs, mods, layer, w_o.astype(BF), ln_g[layer, 0][None], ln_b[layer, 0][None],
                     mlp_w1, mlp_w2, ln_g[layer, 1][None], ln_b[layer, 1][None], split_out)

    xs = (x_prompt.reshape(T_CTX, D), x_sample.reshape(T_LAT, D))
    qb, kb, vb, ks, vs = _da_proj(*xs, mods, 0, da_w_qkv[0].astype(BF), rope_a, rope_b)
    state_da_k = ks.reshape(BATCH, 1, SEQ, DA_HEADS, 2 * HEAD_DIM)
    state_da_v = vs.reshape(BATCH, 1, SEQ, DA_HEADS, 2 * HEAD_DIM)
    o_ctx, o_lat = _da_attention(qb, kb, vb, cache_da_k, cache_da_v, da_lambda[0], da_subln_g[0][None], 0)
    x = finish(o_ctx, o_lat, xs, 0, da_w_o[0])

    qb, kb, vb, ks, vs = _na_proj(x, mods, 1, na_w_qkv[0].astype(BF))
    state_na_k, state_na_v = _untranspose_state(ks, NA_HEADS), _untranspose_state(vs, NA_HEADS)
    onehot, neg, mask = _na_constants()
    bias_tab = _na_bias_table(na_rel_bias[0], onehot, neg)
    o_ctx = _na_ctx_attention(qb, kb, vb)
    o_lat = _na_lat_attention(qb, kb, vb, _features_major(cache_na_k), _features_major(cache_na_v), bias_tab, mask)
    x = finish(o_ctx, o_lat, (x,), 1, na_w_o[0])

    g_mat = jnp.asarray(np.kron(np.eye(GN_BLOCK // HEAD_DIM), np.full((HEAD_DIM, HEAD_DIM), 1.0 / HEAD_DIM)), BF)
    qb, kb, vb, ks, vs = _gq_proj(x, mods, 2, gq_w_qkv[0].astype(BF), g_mat,
                                  jnp.tile(gq_q_norm[0], GQ_HEADS)[None], jnp.tile(gq_k_norm[0], GQ_KV_HEADS)[None],
                                  rope_a, rope_b)
    state_gq_k, state_gq_v = _untranspose_state(ks, GQ_KV_HEADS), _untranspose_state(vs, GQ_KV_HEADS)
    o_ctx, o_lat = _gq_attention(qb, kb, vb, _features_major(cache_gq_k), _features_major(cache_gq_v))
    x = finish(o_ctx, o_lat, (x,), 2, gq_w_o[0])

    u = _hy_proj(x, mods, 3, hy_w_in[0].astype(BF))
    w1 = jnp.pad(hy_ffn_w1[0], ((0, HY_EMB_PAD - HY_EMB), (0, 0)))
    zs = []
    for seq, nbatch, row0, dc in ((SEQ, BATCH, 0, D), (DEC_SEQ, DEC_BATCH, T_CTX, 256)):
        cmat, smat = _dft_tables(seq)
        hre, him, hny = _hy_filter(seq, _hy_embedding(seq), w1, hy_ffn_b1[0][None], hy_ffn_w2[0], hy_ffn_b2[0][None],
                                   hy_ffn_freq[0][None], hy_ffn_w3[0], hy_log_decay[0][None], cmat, smat)
        zs.append(_hy_conv(u, hy_short_w[0], hy_short_b[0][None], hy_filter_bias[0], hre, him, hny, cmat, smat,
                           seq, nbatch, row0, dc))
    y_ctx, y_lat = finish(zs[0], zs[1], (x,), 3, hy_w_o[0], split_out=True)

    return (y_ctx.reshape(BATCH, SEQ, D), y_lat.reshape(DEC_BATCH, DEC_SEQ, D),
            state_da_k, state_da_v, state_na_k, state_na_v, state_gq_k, state_gq_v)
```

```python
import functools
import math

import numpy as np
import jax
import jax.numpy as jnp
from jax import lax
from jax.experimental import pallas as pl
from jax.experimental.pallas import tpu as pltpu

F32 = jnp.float32
BF = jnp.bfloat16

D = 1024
BATCH = 16
SEQ = 256
DEC_BATCH = 8
DEC_SEQ = 1024
PAST = 256
DEPTH = 4
GRID_W = 64
GRID_ROWS = DEC_SEQ // GRID_W
D_FF = 4 * D
T_CTX = BATCH * SEQ
T_LAT = DEC_BATCH * DEC_SEQ
T = T_CTX + T_LAT
HEAD_DIM = 64
ATT_SCALE = HEAD_DIM ** -0.5
LOG2E = math.log2(math.e)
Q_SCALE = ATT_SCALE * LOG2E
DA_HEADS = 8
NA_HEADS = 16
NA_WIN_ROWS = 8
NA_WIN_COLS = 16
GQ_HEADS = 16
GQ_KV_HEADS = 4
HY_ORDER = 2
HY_BANDS = 16
HY_EMB = 1 + 2 * HY_BANDS
HY_EMB_PAD = 40
HY_FFN = 64
ROPE_BASE = 10000.0
LN_EPS = 1e-5
RMS_EPS = 1e-6
DN_ALPHA = (2 * DEPTH) ** 0.25
NEG_INF = -1e30

LANES = 128
TM = 512
TM_POST = 512
FF_CHUNK = 512
MLP_CHUNK = 1024
N_FF_CHUNKS = D_FF // FF_CHUNK
SUB_POST = 256
N_CTX_TILES = T_CTX // TM
N_TILES = T // TM
TQ = 512
MOD_ROWS = 16
VMEM_LIMIT = 56 * 1024 * 1024


def _cparams(n_axes, flags=None):
    return pltpu.CompilerParams(dimension_semantics=("arbitrary",) * n_axes,
                                vmem_limit_bytes=VMEM_LIMIT, flags=flags)


def _dot(a, b):
    return jnp.dot(a, b, preferred_element_type=F32)


def _dot_nt(a, b):
    return lax.dot_general(a, b, (((1,), (1,)), ((), ())), preferred_element_type=F32)


def _const_spec(shape):
    nd = len(shape)
    return pl.BlockSpec(shape, lambda *_: (0,) * nd, pipeline_mode=pl.Buffered(1))


def _mod_spec(layer, tm=TM, off=0):
    nctx = T_CTX // tm

    def row(i):
        t = jnp.maximum(i - off, 0)
        return jnp.where(t < nctx, 0, 1 + (t - nctx) // (DEC_SEQ // tm))

    return pl.BlockSpec((None, None, 6, D), lambda i: (layer, row(i), 0, 0))


def _tok_spec(width, tm=TM, off=0):
    return pl.BlockSpec((tm, width), lambda i: (jnp.maximum(i - off, 0), 0))


def _ctx_spec(width, tm=TM, off=0):
    return pl.BlockSpec((tm, width), lambda i: (jnp.clip(i - off, 0, T_CTX // tm - 1), 0))


def _lat_spec(width, tm=TM, off=0):
    return pl.BlockSpec((tm, width), lambda i: (jnp.maximum(i - off - T_CTX // tm, 0), 0))


def _is_lat(tm=TM, off=0):
    return pl.program_id(0) >= off + T_CTX // tm


def _pick(ctx_ref, lat_ref):
    return jnp.where(_is_lat(), lat_ref[...], ctx_ref[...])


def _layer_norm(r, g, b):
    mu = jnp.mean(r, axis=-1, keepdims=True)
    c = r - mu
    var = jnp.mean(c * c, axis=-1, keepdims=True)
    return c * lax.rsqrt(var + LN_EPS) * g + b


def _mods_kernel(c_ref, w_ref, b_ref, o_ref):
    c = c_ref[...]
    s = (c / (1.0 + jnp.exp(-c))).astype(BF)
    o_ref[...] = _dot(s, w_ref[...].astype(BF)) + b_ref[...]


def _mods(cvec, ada_w, ada_b):
    tn = 1536
    out = pl.pallas_call(
        _mods_kernel,
        grid=(DEPTH, 6 * D // tn),
        in_specs=[pl.BlockSpec((MOD_ROWS, D), lambda l, n: (0, 0)),
                  pl.BlockSpec((None, D, tn), lambda l, n: (l, 0, n)),
                  pl.BlockSpec((None, 1, tn), lambda l, n: (l, 0, n))],
        out_specs=pl.BlockSpec((None, MOD_ROWS, tn), lambda l, n: (l, 0, n)),
        out_shape=jax.ShapeDtypeStruct((DEPTH, MOD_ROWS, 6 * D), F32),
        compiler_params=_cparams(2),
        name="adaln_mods",
    )(cvec, ada_w, ada_b.reshape(DEPTH, 1, 6 * D))
    return out.reshape(DEPTH, MOD_ROWS, 6, D)


def _modulate(x, mod_ref, shift, scale):
    return (x * (1.0 + mod_ref[scale:scale + 1, :]) + mod_ref[shift:shift + 1, :]).astype(BF)


def _rope(x, a, b):
    n = x.shape[1]
    lane = lax.broadcasted_iota(jnp.int32, x.shape, 1)
    partner = jnp.where((lane & 16) == 0, pltpu.roll(x, n - 16, 1), pltpu.roll(x, 16, 1))
    return x * a + partner * b


def _rope_spec(width):
    per = DEC_SEQ // TM
    return pl.BlockSpec((TM, width), lambda i: (jnp.where(i < N_CTX_TILES, per, (i - N_CTX_TILES) % per), 0))


def _store_state(k, v, ks_ref, vs_ref, transposed):
    @pl.when(jnp.logical_not(_is_lat()))
    def _():
        if not transposed:
            ks_ref[...] = k
            vs_ref[...] = v
        else:
            n = k.shape[1]
            for x, ref in ((k, ks_ref), (v, vs_ref)):
                xt = x.T
                for j in range(TM // SEQ):
                    ref[j * n:(j + 1) * n, :] = xt[:, j * SEQ:(j + 1) * SEQ]


def _qkv_out(nq, nk, transposed_state):
    specs = [_tok_spec(nq), _tok_spec(nk), _tok_spec(nk)]
    shapes = [jax.ShapeDtypeStruct((T, nq), BF), jax.ShapeDtypeStruct((T, nk), BF), jax.ShapeDtypeStruct((T, nk), BF)]
    if transposed_state:
        rows = (TM // SEQ) * nk
        specs += [pl.BlockSpec((rows, SEQ), lambda i: (jnp.minimum(i, N_CTX_TILES - 1), 0))] * 2
        shapes += [jax.ShapeDtypeStruct((BATCH * nk, SEQ), F32)] * 2
    else:
        specs += [_ctx_spec(nk)] * 2
        shapes += [jax.ShapeDtypeStruct((T_CTX, nk), F32)] * 2
    return specs, shapes


def _features_major(cache):
    b, _, past, heads, dh = cache.shape
    return cache.transpose(0, 1, 3, 4, 2).reshape(b, heads * dh, past)


def _untranspose_state(st, heads):
    return st.reshape(BATCH, heads, HEAD_DIM, SEQ).transpose(0, 3, 1, 2)[:, None]


def _da_proj_kernel(xc_ref, xl_ref, mod_ref, w_ref, ra_ref, rb_ref, qb_ref, kb_ref, vb_ref, ks_ref, vs_ref):
    h = _modulate(_pick(xc_ref, xl_ref), mod_ref, 0, 1)
    a, b = ra_ref[...], rb_ref[...]
    q = _dot(h, w_ref[:, 0:D])
    k = _dot(h, w_ref[:, D:2 * D])
    qb_ref[...] = (_rope(q, a, b) * Q_SCALE).astype(BF)
    v = _dot(h, w_ref[:, 2 * D:3 * D])
    kb_ref[...] = _rope(k, a, b).astype(BF)
    vb_ref[...] = v.astype(BF)
    _store_state(k, v, ks_ref, vs_ref, False)


def _da_proj(x_ctx, x_lat, mods, layer, w, rope_a, rope_b):
    specs, shapes = _qkv_out(D, D, False)
    return pl.pallas_call(
        _da_proj_kernel,
        grid=(N_TILES,),
        in_specs=[_ctx_spec(D), _lat_spec(D), _mod_spec(layer), _const_spec((D, 3 * D)),
                  _rope_spec(D), _rope_spec(D)],
        out_specs=specs, out_shape=shapes,
        compiler_params=_cparams(1),
        name=f"da_proj_l{layer}",
    )(x_ctx, x_lat, mods, w, rope_a, rope_b)


def _na_proj_kernel(x_ref, mod_ref, w_ref, qb_ref, kb_ref, vb_ref, ks_ref, vs_ref):
    h = _modulate(x_ref[...], mod_ref, 0, 1)
    q = _dot(h, w_ref[:, 0:D])
    k = _dot(h, w_ref[:, D:2 * D])
    qb_ref[...] = (q * Q_SCALE).astype(BF)
    v = _dot(h, w_ref[:, 2 * D:3 * D])
    kb_ref[...] = k.astype(BF)
    vb_ref[...] = v.astype(BF)
    _store_state(k, v, ks_ref, vs_ref, True)


def _na_proj(x, mods, layer, w):
    specs, shapes = _qkv_out(D, D, True)
    return pl.pallas_call(
        _na_proj_kernel,
        grid=(N_TILES,),
        in_specs=[_tok_spec(D), _mod_spec(layer), _const_spec((D, 3 * D))],
        out_specs=specs, out_shape=shapes,
        compiler_params=_cparams(1),
        name=f"na_proj_l{layer}",
    )(x, mods, w)


GN_BLOCK = 256


def _head_rms(x, g_ref, gain):
    x2 = x * x
    hi = x2.astype(BF)
    lo = (x2 - hi.astype(F32)).astype(BF)
    g = g_ref[...]
    ms = jnp.concatenate(
        [_dot(hi[:, j:j + GN_BLOCK], g) + _dot(lo[:, j:j + GN_BLOCK], g) for j in range(0, x.shape[1], GN_BLOCK)],
        axis=1)
    return x * lax.rsqrt(ms + RMS_EPS) * gain


def _gq_proj_kernel(x_ref, mod_ref, w_ref, g_ref, qn_ref, kn_ref, ra_ref, rb_ref,
                    qb_ref, kb_ref, vb_ref, ks_ref, vs_ref):
    nq, nk = GQ_HEADS * HEAD_DIM, GQ_KV_HEADS * HEAD_DIM
    h = _modulate(x_ref[...], mod_ref, 0, 1)
    a, b = ra_ref[...], rb_ref[...]
    q = _dot(h, w_ref[:, 0:nq])
    k = _dot(h, w_ref[:, nq:nq + nk])
    v = _dot(h, w_ref[:, nq + nk:nq + 2 * nk])
    k = _head_rms(k, g_ref, kn_ref[...])
    q = _head_rms(q, g_ref, qn_ref[...])
    kb_ref[...] = _rope(k, a[:, 0:nk], b[:, 0:nk]).astype(BF)
    qb_ref[...] = (_rope(q, a, b) * Q_SCALE).astype(BF)
    vb_ref[...] = v.astype(BF)
    _store_state(k, v, ks_ref, vs_ref, True)


def _gq_proj(x, mods, layer, w, g_mat, qn, kn, rope_a, rope_b):
    nq, nk = GQ_HEADS * HEAD_DIM, GQ_KV_HEADS * HEAD_DIM
    specs, shapes = _qkv_out(nq, nk, True)
    return pl.pallas_call(
        _gq_proj_kernel,
        grid=(N_TILES,),
        in_specs=[_tok_spec(D), _mod_spec(layer), _const_spec((D, nq + 2 * nk)),
                  _const_spec((GN_BLOCK, GN_BLOCK)), _const_spec((1, nq)), _const_spec((1, nk)),
                  _rope_spec(D), _rope_spec(D)],
        out_specs=specs, out_shape=shapes,
        compiler_params=_cparams(1),
        name=f"gq_proj_l{layer}",
    )(x, mods, w, g_mat, qn, kn, rope_a, rope_b)


def _hy_proj_kernel(x_ref, mod_ref, w_ref, u_ref):
    h = _modulate(x_ref[...], mod_ref, 0, 1)
    for c in range(HY_ORDER + 1):
        u_ref[:, c * D:(c + 1) * D] = _dot(h, w_ref[:, c * D:(c + 1) * D])


def _hy_proj(x, mods, layer, w):
    n = (HY_ORDER + 1) * D
    return pl.pallas_call(
        _hy_proj_kernel,
        grid=(N_TILES,),
        in_specs=[_tok_spec(D), _mod_spec(layer), _const_spec((D, n))],
        out_specs=_tok_spec(n),
        out_shape=jax.ShapeDtypeStruct((T, n), F32),
        compiler_params=_cparams(1),
        name=f"hy_proj_l{layer}",
    )(x, mods, w)


def _softmax_pv(qm, segs):
    return _softmax_finish(_scores(qm, segs), segs)


def _scores(qm, segs):
    return [_dot(qm, seg[0]) if len(seg) == 3 else _dot_nt(qm, seg[0]) for seg in segs]


def _softmax_finish(scores, segs):
    m = scores[0].max(axis=-1, keepdims=True)
    for s in scores[1:]:
        m = jnp.maximum(m, s.max(axis=-1, keepdims=True))
    den = None
    out = None
    for s, seg in zip(scores, segs):
        e = jnp.exp2(s - m)
        d = e.sum(axis=-1, keepdims=True)
        o = _dot_nt(e.astype(BF), seg[1]) if len(seg) == 3 else _dot(e.astype(BF), seg[1])
        den = d if den is None else den + d
        out = o if out is None else out + o
    return out / den


def _stack_halves(q, keep):
    return jnp.concatenate([q * keep[0], q * keep[1]], axis=0)


def _pipelined(jobs, score_fn, finish_fn):
    nxt = score_fn(jobs[0])
    for n, job in enumerate(jobs):
        cur, nxt = nxt, (score_fn(jobs[n + 1]) if n + 1 < len(jobs) else None)
        finish_fn(job, cur)


def _lane_half(shape):
    return lax.broadcasted_iota(jnp.int32, shape, 1) // HEAD_DIM


def _half_keep(half):
    return tuple(jnp.where(half == a, 1.0, 0.0).astype(BF) for a in (0, 1))


def _da_attn_kernel(*refs, has_cache, lam_init):
    if has_cache:
        q_ref, k_ref, v_ref, ck_ref, cv_ref, lam_ref, g_ref, o_ref = refs
    else:
        q_ref, k_ref, v_ref, lam_ref, g_ref, o_ref = refs
    lp = lam_ref[...]
    lam = (jnp.exp(jnp.sum(lp[0:1] * lp[1:2], axis=-1, keepdims=True))
           - jnp.exp(jnp.sum(lp[2:3] * lp[3:4], axis=-1, keepdims=True)) + lam_init)
    gain = g_ref[...] * (1.0 - lam_init)
    w = 2 * HEAD_DIM
    tq = min(TQ // 2, q_ref.shape[0])
    keep = _half_keep(_lane_half((tq, w)))
    segs = []
    for hd in range(k_ref.shape[1] // w):
        cols = slice(hd * w, (hd + 1) * w)
        seg = [(k_ref[:, cols], v_ref[:, cols])]
        if has_cache:
            head = pl.program_id(1)
            seg.append((ck_ref[:, head, :].astype(BF), cv_ref[:, head, :].astype(BF)))
        segs.append(seg)
    jobs = [(hd, t) for hd in range(len(segs)) for t in range(q_ref.shape[0] // tq)]

    def score_fn(job):
        hd, t = job
        return _scores(_stack_halves(q_ref[t * tq:(t + 1) * tq, hd * w:(hd + 1) * w], keep), segs[hd])

    def finish_fn(job, scores):
        hd, t = job
        o = _softmax_finish(scores, segs[hd])
        o = o[0:tq] - lam * o[tq:2 * tq]
        ms = jnp.mean(o * o, axis=-1, keepdims=True)
        o_ref[t * tq:(t + 1) * tq, hd * w:(hd + 1) * w] = (o * lax.rsqrt(ms + RMS_EPS) * gain).astype(BF)

    _pipelined(jobs, score_fn, finish_fn)


def _da_attention(qb, kb, vb, cache_k, cache_v, lam_p, subln_g, layer_idx):
    lam_init = 0.8 - 0.6 * math.exp(-0.3 * layer_idx)
    w = 2 * HEAD_DIM
    small = [pl.BlockSpec((4, HEAD_DIM), lambda *_: (0, 0)), pl.BlockSpec((1, w), lambda *_: (0, 0))]
    o_ctx = pl.pallas_call(
        functools.partial(_da_attn_kernel, has_cache=False, lam_init=lam_init),
        grid=(BATCH,),
        in_specs=[pl.BlockSpec((SEQ, D), lambda b: (b, 0))] * 3 + small,
        out_specs=pl.BlockSpec((SEQ, D), lambda b: (b, 0)),
        out_shape=jax.ShapeDtypeStruct((T_CTX, D), BF),
        compiler_params=_cparams(1),
        name="da_attn_ctx",
    )(qb, kb, vb, lam_p, subln_g)
    k0 = T_CTX // DEC_SEQ
    tok = pl.BlockSpec((DEC_SEQ, w), lambda b, h: (k0 + b, h))
    c_spec = pl.BlockSpec((None, None, PAST, DA_HEADS, w), lambda b, h: (b, 0, 0, 0, 0))
    o_lat = pl.pallas_call(
        functools.partial(_da_attn_kernel, has_cache=True, lam_init=lam_init),
        grid=(DEC_BATCH, DA_HEADS),
        in_specs=[tok, tok, tok, c_spec, c_spec] + small,
        out_specs=pl.BlockSpec((DEC_SEQ, w), lambda b, h: (b, h)),
        out_shape=jax.ShapeDtypeStruct((T_LAT, D), BF),
        compiler_params=_cparams(2),
        name="da_attn_lat",
    )(qb, kb, vb, cache_k, cache_v, lam_p, subln_g)
    return o_ctx, o_lat


def _na_ctx_kernel(q_ref, k_ref, v_ref, o_ref):
    half = _lane_half((SEQ, LANES))
    keep = _half_keep(half)

    def seg(p):
        return [(k_ref[:, p * LANES:(p + 1) * LANES], v_ref[:, p * LANES:(p + 1) * LANES])]

    def score_fn(p):
        return _scores(_stack_halves(q_ref[:, p * LANES:(p + 1) * LANES], keep), seg(p))

    def finish_fn(p, scores):
        o = _softmax_finish(scores, seg(p))
        o_ref[:, p * LANES:(p + 1) * LANES] = jnp.where(half == 0, o[0:SEQ], o[SEQ:2 * SEQ]).astype(BF)

    _pipelined(list(range(NA_HEADS // 2)), score_fn, finish_fn)


def _na_ctx_attention(qb, kb, vb):
    spec = pl.BlockSpec((SEQ, D), lambda b: (b, 0))
    return pl.pallas_call(
        _na_ctx_kernel,
        grid=(BATCH,),
        in_specs=[spec] * 3,
        out_specs=spec,
        out_shape=jax.ShapeDtypeStruct((T_CTX, D), BF),
        compiler_params=_cparams(1),
        name="na_attn_ctx",
    )(qb, kb, vb)


NA_TILES = ((0, (0, 2, 4, 6)), (4, (0, 2, 4, 6, 8, 10)), (8, (4, 6, 8, 10, 12, 14)), (12, (8, 10, 12, 14)))
NA_MAX_CHUNKS = 6
NA_BIAS_BLOCKS = 2 * NA_WIN_ROWS - 2


def _na_lat_kernel(q_ref, k_ref, v_ref, ck_ref, cv_ref, w_ref, m_ref, o_ref):
    ckt, cvt = ck_ref[...].astype(BF), cv_ref[...].astype(BF)
    rows = 4 * GRID_W
    half = _lane_half((rows, LANES))
    keep = _half_keep(half)

    def key_rows(i):
        chunks = NA_TILES[i][1]
        return slice(chunks[0] * GRID_W, chunks[0] * GRID_W + len(chunks) * LANES)

    def score_fn(i):
        r0, chunks = NA_TILES[i]
        qm = _stack_halves(q_ref[i * rows:(i + 1) * rows, :], keep)
        mask = m_ref[i, :, 0:len(chunks) * LANES]
        bias = jnp.concatenate(
            [jnp.concatenate([w_ref[a, (6 - kr + r0) * GRID_W:(6 - kr + r0) * GRID_W + rows, :] for kr in chunks],
                             axis=1) + mask for a in (0, 1)], axis=0)
        return [_dot_nt(qm, k_ref[key_rows(i), :]) + bias, _dot(qm, ckt)]

    def finish_fn(i, scores):
        o = _softmax_finish(scores, [(None, v_ref[key_rows(i), :]), (None, cvt, True)])
        o_ref[i * rows:(i + 1) * rows, :] = jnp.where(half == 0, o[0:rows], o[rows:2 * rows]).astype(BF)

    _pipelined(list(range(len(NA_TILES))), score_fn, finish_fn)


def _na_lat_attention(qb, kb, vb, cache_k, cache_v, bias_tab, mask_tab):
    k0 = T_CTX // DEC_SEQ
    tok = pl.BlockSpec((DEC_SEQ, LANES), lambda b, p: (k0 + b, p))
    c_spec = pl.BlockSpec((None, LANES, PAST), lambda b, p: (b, p, 0))
    return pl.pallas_call(
        _na_lat_kernel,
        grid=(DEC_BATCH, NA_HEADS // 2),
        in_specs=[tok, tok, tok, c_spec, c_spec,
                  pl.BlockSpec((None, 2, NA_BIAS_BLOCKS * GRID_W, LANES), lambda b, p: (p, 0, 0, 0)),
                  _const_spec(mask_tab.shape)],
        out_specs=pl.BlockSpec((DEC_SEQ, LANES), lambda b, p: (b, p)),
        out_shape=jax.ShapeDtypeStruct((T_LAT, D), BF),
        compiler_params=_cparams(2),
        name="na_attn_lat",
    )(qb, kb, vb, cache_k, cache_v, bias_tab, mask_tab)


def _na_bias_kernel(t_ref, r_ref, n_ref, o_ref):
    t = t_ref[...]
    t1 = t.astype(BF)
    r1 = t - t1.astype(F32)
    t2 = r1.astype(BF)
    t3 = (r1 - t2.astype(F32)).astype(BF)
    r = r_ref[...]
    res = (_dot(t1, r) + _dot(t2, r) + _dot(t3, r) + n_ref[...]) * LOG2E
    for qc in range(GRID_W):
        o_ref[pl.ds(qc, t.shape[0], stride=GRID_W), :] = res[:, qc * LANES:(qc + 1) * LANES]


def _na_bias_table(rel_bias, onehot, neg):
    nrel = 2 * NA_WIN_COLS
    idx = 13 - np.arange(NA_BIAS_BLOCKS)[:, None] + np.arange(2)[None, :]
    t = jnp.pad(rel_bias[:, idx, :], ((0, 0), (0, 0), (0, 0), (0, 1)))
    t = t.reshape(NA_HEADS * NA_BIAS_BLOCKS, 2 * nrel)
    n = GRID_W * LANES
    out = pl.pallas_call(
        _na_bias_kernel,
        grid=(1,),
        in_specs=[pl.BlockSpec(t.shape, lambda j: (0, 0)),
                  pl.BlockSpec((2 * nrel, n), lambda j: (0, 0)),
                  pl.BlockSpec((1, n), lambda j: (0, 0))],
        out_specs=pl.BlockSpec((t.shape[0] * GRID_W, LANES), lambda j: (0, 0)),
        out_shape=jax.ShapeDtypeStruct((t.shape[0] * GRID_W, LANES), F32),
        compiler_params=_cparams(1),
        name="na_bias_table",
    )(t, onehot, neg)
    return out.reshape(NA_HEADS // 2, 2, NA_BIAS_BLOCKS * GRID_W, LANES)


def _na_constants():
    nrel = 2 * NA_WIN_COLS
    qc = np.arange(GRID_W)[:, None]
    kc = np.arange(GRID_W)[None, :]
    rel = np.clip(kc - qc, -(NA_WIN_COLS - 1), NA_WIN_COLS - 1) + NA_WIN_COLS - 1
    cs = np.clip(qc - NA_WIN_COLS // 2, 0, GRID_W - NA_WIN_COLS)
    col_in = (kc >= cs) & (kc < cs + NA_WIN_COLS)
    onehot = np.zeros((2, nrel, GRID_W, 2, GRID_W), np.float32)
    for hf in range(2):
        onehot[hf, rel, qc, hf, kc] = 1.0
    neg = np.where(col_in, 0.0, NEG_INF).astype(np.float32)
    neg = np.broadcast_to(neg[:, None, :], (GRID_W, 2, GRID_W)).reshape(1, -1)
    rows = 4 * GRID_W
    mask = np.full((len(NA_TILES), rows, NA_MAX_CHUNKS * LANES), NEG_INF, np.float32)
    kr = min(NA_WIN_ROWS, GRID_ROWS)
    for i, (r0, chunks) in enumerate(NA_TILES):
        qr = r0 + np.arange(rows)[:, None] // GRID_W
        rs = np.clip(qr - kr // 2, 0, GRID_ROWS - kr)
        for c, krow0 in enumerate(chunks):
            krow = krow0 + np.arange(LANES)[None, :] // GRID_W
            mask[i, :, c * LANES:(c + 1) * LANES] = np.where((krow >= rs) & (krow < rs + kr), 0.0, NEG_INF)
    return (jnp.asarray(onehot.reshape(2 * nrel, GRID_W * LANES), BF), jnp.asarray(neg), jnp.asarray(mask))


def _gq_attn_kernel(*refs, has_cache):
    if has_cache:
        q_ref, k_ref, v_ref, ck_ref, cv_ref, o_ref = refs
    else:
        q_ref, k_ref, v_ref, o_ref = refs
    group = GQ_HEADS // GQ_KV_HEADS
    qw = LANES * group
    tq = min(TQ // 2, q_ref.shape[0])
    half = _lane_half((tq, LANES))
    keep = _half_keep(half)
    segs = []
    for kvp in range(k_ref.shape[1] // LANES):
        kcols = slice(kvp * LANES, (kvp + 1) * LANES)
        seg = [(k_ref[:, kcols], v_ref[:, kcols])]
        if has_cache:
            seg.append((ck_ref[kcols, :].astype(BF), cv_ref[kcols, :].astype(BF), True))
        segs.append(seg)
    jobs = [(kvp, t, kh) for kvp in range(len(segs)) for t in range(q_ref.shape[0] // tq) for kh in (0, 1)]

    def blocks(job):
        kvp, t, kh = job
        for pair in (2 * kh, 2 * kh + 1):
            yield slice(t * tq, (t + 1) * tq), slice(kvp * qw + pair * LANES, kvp * qw + (pair + 1) * LANES)

    def score_fn(job):
        kh = job[2]
        parts = []
        for rows, cols in blocks(job):
            for a in (0, 1):
                qm = q_ref[rows, cols] * keep[a]
                parts.append(qm if a == kh else pltpu.roll(qm.astype(F32), HEAD_DIM, 1).astype(BF))
        return _scores(jnp.concatenate(parts, axis=0), segs[job[0]])

    def finish_fn(job, scores):
        kh = job[2]
        o = _softmax_finish(scores, segs[job[0]])
        for n, (rows, cols) in enumerate(blocks(job)):
            heads = [o[(2 * n + a) * tq:(2 * n + a + 1) * tq] for a in (0, 1)]
            heads = [h if a == kh else pltpu.roll(h, HEAD_DIM, 1) for a, h in enumerate(heads)]
            o_ref[rows, cols] = jnp.where(half == 0, heads[0], heads[1]).astype(BF)

    _pipelined(jobs, score_fn, finish_fn)


def _gq_attention(qb, kb, vb, cache_k, cache_v):
    nk = GQ_KV_HEADS * HEAD_DIM
    qw = LANES * (GQ_HEADS // GQ_KV_HEADS)
    npair = GQ_KV_HEADS // 2
    o_ctx = pl.pallas_call(
        functools.partial(_gq_attn_kernel, has_cache=False),
        grid=(BATCH,),
        in_specs=[pl.BlockSpec((SEQ, D), lambda b: (b, 0))] + [pl.BlockSpec((SEQ, nk), lambda b: (b, 0))] * 2,
        out_specs=pl.BlockSpec((SEQ, D), lambda b: (b, 0)),
        out_shape=jax.ShapeDtypeStruct((T_CTX, D), BF),
        compiler_params=_cparams(1),
        name="gq_attn_ctx",
    )(qb, kb, vb)
    qt = DEC_SEQ // TQ
    q0, k0 = T_CTX // TQ, T_CTX // DEC_SEQ
    kv_spec = pl.BlockSpec((DEC_SEQ, LANES), lambda b, p, t: (k0 + b, p))
    c_spec = pl.BlockSpec((None, LANES, PAST), lambda b, p, t: (b, p, 0))
    o_lat = pl.pallas_call(
        functools.partial(_gq_attn_kernel, has_cache=True),
        grid=(DEC_BATCH, npair, qt),
        in_specs=[pl.BlockSpec((TQ, qw), lambda b, p, t: (q0 + b * qt + t, p)), kv_spec, kv_spec, c_spec, c_spec],
        out_specs=pl.BlockSpec((TQ, qw), lambda b, p, t: (b * qt + t, p)),
        out_shape=jax.ShapeDtypeStruct((T_LAT, D), BF),
        compiler_params=_cparams(3),
        name="gq_attn_lat",
    )(qb, kb, vb, cache_k, cache_v)
    return o_ctx, o_lat


def _hy_filter_kernel(emb_ref, w1_ref, b1_ref, w2_ref, b2_ref, fr_ref, w3f_ref, w3b_ref, ldf_ref, ldb_ref,
                      c_ref, s_ref, hre_ref, him_ref, hny_ref):
    seq = emb_ref.shape[0]
    hp = lax.Precision.HIGHEST
    emb = emb_ref[...]
    fr = fr_ref[...]
    hid = jnp.sin(fr * (jnp.dot(emb, w1_ref[...], precision=hp, preferred_element_type=F32) + b1_ref[...]))
    hid = jnp.sin(fr * (jnp.dot(hid, w2_ref[...], precision=hp, preferred_element_type=F32) + b2_ref[...]))
    t = emb[:, 0:1]
    fwd = jnp.dot(hid, w3f_ref[...], precision=hp, preferred_element_type=F32) * jnp.exp(-jnp.exp(ldf_ref[...]) * t)
    bwd = jnp.dot(hid, w3b_ref[...], precision=hp, preferred_element_type=F32) * jnp.exp(-jnp.exp(ldb_ref[...]) * t)
    row = lax.broadcasted_iota(jnp.int32, fwd.shape, 0)
    bwd = jnp.where(row == 0, 0.0, bwd)
    even = fwd + bwd
    odd = bwd - fwd
    wk = jnp.where(row == 0, 0.5 / seq, 1.0 / seq)
    hre_ref[...] = _dot(c_ref[...].astype(BF), even.astype(BF)) * wk
    him_ref[...] = _dot(s_ref[...].astype(BF), odd.astype(BF)) * wk
    alt = jnp.where((row & 1) == 0, 1.0, -1.0)
    hny_ref[...] = jnp.sum(alt * even, axis=0, keepdims=True) * (0.5 / seq)


def _hy_filter(seq, emb, w1, b1, w2, b2, freq, w3, log_decay, cmat, smat):
    dc = 512
    nj = D // dc
    small = [_const_spec(a.shape) for a in (emb, w1, b1, w2, b2, freq)]
    return pl.pallas_call(
        _hy_filter_kernel,
        grid=(HY_ORDER, nj),
        in_specs=small + [pl.BlockSpec((HY_FFN, dc), lambda o, j: (0, (2 * o) * nj + j)),
                          pl.BlockSpec((HY_FFN, dc), lambda o, j: (0, (2 * o + 1) * nj + j)),
                          pl.BlockSpec((1, dc), lambda o, j: (0, (2 * o) * nj + j)),
                          pl.BlockSpec((1, dc), lambda o, j: (0, (2 * o + 1) * nj + j)),
                          _const_spec((seq, seq)), _const_spec((seq, seq))],
        out_specs=[pl.BlockSpec((None, seq, dc), lambda o, j: (o, 0, j)),
                   pl.BlockSpec((None, seq, dc), lambda o, j: (o, 0, j)),
                   pl.BlockSpec((None, 1, dc), lambda o, j: (o, 0, j))],
        out_shape=[jax.ShapeDtypeStruct((HY_ORDER, seq, D), F32), jax.ShapeDtypeStruct((HY_ORDER, seq, D), F32),
                   jax.ShapeDtypeStruct((HY_ORDER, 1, D), F32)],
        compiler_params=_cparams(2),
        name=f"hy_filter_{seq}",
    )(emb, w1, b1, w2, b2, freq, w3, w3, log_decay, log_decay, cmat, smat)


HY_SUB = 256


def _hy_conv_kernel(u0_ref, u1_ref, u2_ref, sw0_ref, sw1_ref, sw2_ref, sb0_ref, sb1_ref, sb2_ref,
                    fb_ref, hre_ref, him_ref, hny_ref, c_ref, s_ref, o_ref, cb_ref, sb_ref):
    seq, dc = u0_ref.shape

    @pl.when((pl.program_id(0) == 0) & (pl.program_id(1) == 0))
    def _():
        cb_ref[...] = c_ref[...].astype(BF)
        sb_ref[...] = s_ref[...].astype(BF)

    row = lax.broadcasted_iota(jnp.int32, (seq, HY_SUB), 0)
    alt = jnp.where((row & 1) == 0, 1.0, -1.0)

    def sub_tile(cols):
        def short_conv(u_ref, w_ref, b_ref):
            u = u_ref[:, cols]
            prev = jnp.where(row == 0, 0.0, pltpu.roll(u, 1, 0))
            nxt = jnp.where(row == seq - 1, 0.0, pltpu.roll(u, seq - 1, 0))
            return prev * w_ref[0:1, cols] + u * w_ref[1:2, cols] + nxt * w_ref[2:3, cols] + b_ref[:, cols]

        z = short_conv(u0_ref, sw0_ref, sb0_ref)
        gates = (short_conv(u1_ref, sw1_ref, sb1_ref), short_conv(u2_ref, sw2_ref, sb2_ref))
        yield
        for o in range(HY_ORDER):
            zb = z.astype(BF)
            zc, zs = _dot(cb_ref[...], zb), _dot(sb_ref[...], zb)
            yield
            hre, him = hre_ref[o, :, cols], him_ref[o, :, cols]
            p_re = (zc * hre + zs * him).astype(BF)
            p_im = (zc * him - zs * hre).astype(BF)
            y = _dot(cb_ref[...], p_re) - _dot(sb_ref[...], p_im)
            yield
            nyq = jnp.sum(alt * z, axis=0, keepdims=True) * hny_ref[o, :, cols]
            z = gates[o] * (y + alt * nyq + z * fb_ref[o:o + 1, cols])
        o_ref[:, cols] = z.astype(BF)

    tiles = [sub_tile(slice(j * HY_SUB, (j + 1) * HY_SUB)) for j in range(dc // HY_SUB)]
    while tiles:
        tiles = [t for t in tiles if next(t, True) is None]


def _hy_conv(u, short_w, short_b, filter_bias, hre, him, hny, cmat, smat, seq, nbatch, row0, dc):
    nj = D // dc
    r0 = row0 // seq

    def part(p):
        return pl.BlockSpec((seq, dc), lambda j, b: (r0 + b, p * nj + j))

    def vec(rows, p):
        return pl.BlockSpec((rows, dc), lambda j, b: (0, p * nj + j))

    in_specs = ([part(p) for p in range(3)] + [vec(3, p) for p in range(3)] + [vec(1, p) for p in range(3)]
                + [pl.BlockSpec((HY_ORDER, dc), lambda j, b: (0, j)),
                   pl.BlockSpec((HY_ORDER, seq, dc), lambda j, b: (0, 0, j), pipeline_mode=pl.Buffered(1)),
                   pl.BlockSpec((HY_ORDER, seq, dc), lambda j, b: (0, 0, j), pipeline_mode=pl.Buffered(1)),
                   pl.BlockSpec((HY_ORDER, 1, dc), lambda j, b: (0, 0, j)),
                   _const_spec((seq, seq)), _const_spec((seq, seq))])
    return pl.pallas_call(
        _hy_conv_kernel,
        grid=(nj, nbatch),
        in_specs=in_specs,
        out_specs=pl.BlockSpec((seq, dc), lambda j, b: (b, j)),
        out_shape=jax.ShapeDtypeStruct((nbatch * seq, D), BF),
        scratch_shapes=[pltpu.VMEM((seq, seq), BF), pltpu.VMEM((seq, seq), BF)],
        compiler_params=_cparams(2),
        name=f"hy_conv_{seq}",
    )(u, u, u, short_w, short_w, short_w, short_b, short_b, short_b, filter_bias, hre, him, hny, cmat, smat)


def _dft_tables(seq):
    k = np.arange(seq, dtype=np.int64)
    ang = np.pi * ((k[:, None] * k[None, :]) % (2 * seq)) / seq
    return jnp.asarray(np.cos(ang), F32), jnp.asarray(np.sin(ang), F32)


def _hy_embedding(seq):
    t = np.arange(seq, dtype=np.float32) / np.float32(seq)
    ang = (2.0 * math.pi) * t[:, None] * np.arange(1, HY_BANDS + 1, dtype=np.float32)
    emb = np.concatenate([t[:, None], np.cos(ang), np.sin(ang)], axis=-1).astype(np.float32)
    return jnp.asarray(np.pad(emb, ((0, 0), (0, HY_EMB_PAD - HY_EMB))))


def _post_kernel(*refs, split_x, split_out, tm):
    oc_ref, ol_ref = refs[0:2]
    x_refs, refs = (refs[2:4], refs[4:]) if split_x else (refs[2:3], refs[3:])
    mod_ref, wo_ref, g1_ref, b1_ref, w1c_ref, w2c_ref, g2_ref, b2_ref = refs[0:8]
    outs, (w1_ref, w2_ref) = refs[8:-2], refs[-2:]
    step = pl.program_id(0)
    is_lat = _is_lat(tm, N_FF_CHUNKS)
    nsub = tm // SUB_POST

    @pl.when(step < N_FF_CHUNKS)
    def _():
        per = MLP_CHUNK // FF_CHUNK
        w1_ref[step] = w1c_ref[...].astype(BF)
        w2_ref[step // per, pl.ds(pl.multiple_of((step % per) * FF_CHUNK, FF_CHUNK), FF_CHUNK), :] = (
            w2c_ref[...].astype(BF))

    def rows(j):
        return slice(j * SUB_POST, (j + 1) * SUB_POST)

    def pick(c_ref, l_ref, j):
        return jnp.where(is_lat, l_ref[rows(j), :], c_ref[rows(j), :])

    def norm1(j):
        a = _dot(pick(oc_ref, ol_ref, j), wo_ref[...])
        x = pick(x_refs[0], x_refs[1], j) if split_x else x_refs[0][rows(j), :]
        x1 = _layer_norm(DN_ALPHA * x + mod_ref[2:3, :] * a, g1_ref[...], b1_ref[...])
        return x1, _modulate(x1, mod_ref, 3, 4)

    def mlp_chunk(h, c):
        per = MLP_CHUNK // FF_CHUNK
        a = jnp.concatenate([_dot(h, w1_ref[per * c + i]) for i in range(per)], axis=1)
        a = jnp.maximum(a, 0.0)
        return _dot((a * a).astype(BF), w2_ref[c])

    def token_tile():
        ys = []

        def norm2_store(j, x1, acc):
            y = _layer_norm(DN_ALPHA * x1 + mod_ref[5:6, :] * acc, g2_ref[...], b2_ref[...])
            if split_out:
                ys.append(y)
            else:
                outs[0][rows(j), :] = y

        cur = norm1(0)
        prev = None
        for j in range(nsub):
            x1, h = cur
            acc = mlp_chunk(h, 0)
            if j + 1 < nsub:
                cur = norm1(j + 1)
            if prev is not None:
                norm2_store(j - 1, *prev)
            for c in range(1, D_FF // MLP_CHUNK):
                acc = acc + mlp_chunk(h, c)
            prev = (x1, acc)
        norm2_store(nsub - 1, *prev)
        if split_out:
            yc_ref, yl_ref = outs

            @pl.when(jnp.logical_not(is_lat))
            def _():
                for j, y in enumerate(ys):
                    yc_ref[rows(j), :] = y

            @pl.when(is_lat)
            def _():
                for j, y in enumerate(ys):
                    yl_ref[rows(j), :] = y

    pl.when(step >= N_FF_CHUNKS)(token_tile)


def _post(o_ctx, o_lat, xs, mods, layer, w_o, g1, b1, w1, w2, g2, b2, split_out):
    tm, off = TM_POST, N_FF_CHUNKS
    split_x = len(xs) == 2
    x_specs = [_ctx_spec(D, tm, off), _lat_spec(D, tm, off)] if split_x else [_tok_spec(D, tm, off)]
    vec = _const_spec((1, D))
    if split_out:
        out_specs = [_ctx_spec(D, tm, off), _lat_spec(D, tm, off)]
        out_shape = [jax.ShapeDtypeStruct((T_CTX, D), F32), jax.ShapeDtypeStruct((T_LAT, D), F32)]
    else:
        out_specs = _tok_spec(D, tm, off)
        out_shape = jax.ShapeDtypeStruct((T, D), F32)

    def chunk(i):
        return jnp.minimum(i, N_FF_CHUNKS - 1)

    return pl.pallas_call(
        functools.partial(_post_kernel, split_x=split_x, split_out=split_out, tm=tm),
        grid=(off + T // tm,),
        in_specs=[_ctx_spec(D, tm, off), _lat_spec(D, tm, off)] + x_specs + [
            _mod_spec(layer, tm, off), _const_spec((D, D)), vec, vec,
            pl.BlockSpec((None, D, FF_CHUNK), lambda i: (layer, 0, chunk(i))),
            pl.BlockSpec((None, FF_CHUNK, D), lambda i: (layer, chunk(i), 0)), vec, vec],
        out_specs=out_specs,
        out_shape=out_shape,
        scratch_shapes=[pltpu.VMEM((N_FF_CHUNKS, D, FF_CHUNK), BF),
                        pltpu.VMEM((D_FF // MLP_CHUNK, MLP_CHUNK, D), BF)],
        compiler_params=_cparams(1),
        name=f"post_l{layer}",
    )(o_ctx, o_lat, *xs, mods, w_o, g1, b1, w1, w2, g2, b2)


def _rope_tables():
    n = HEAD_DIM // 4
    pos = np.arange(DEC_SEQ)
    inv = (np.float32(ROPE_BASE) ** (-np.arange(n, dtype=np.float32) / np.float32(n))).astype(np.float32)
    ang_r = ((pos // GRID_W).astype(np.float32)[:, None] * inv).astype(np.float32)
    ang_c = ((pos % GRID_W).astype(np.float32)[:, None] * inv).astype(np.float32)
    cr, sr, cc, sc = np.cos(ang_r), np.sin(ang_r), np.cos(ang_c), np.sin(ang_c)
    a = np.tile(np.concatenate([cr, cr, cc, cc], axis=-1), (1, D // HEAD_DIM))
    b = np.tile(np.concatenate([-sr, sr, -sc, sc], axis=-1), (1, D // HEAD_DIM))
    a = np.concatenate([a, np.ones((TM, D), np.float32)], axis=0)
    b = np.concatenate([b, np.zeros((TM, D), np.float32)], axis=0)
    return jnp.asarray(a, F32), jnp.asarray(b, F32)


def kernel(x_prompt, x_sample, c, cache_da_k, cache_da_v, cache_na_k, cache_na_v, cache_gq_k, cache_gq_v, c_ctx, ada_w, ada_b, ln_g, ln_b, mlp_w1, mlp_w2, da_w_qkv, da_w_o, da_lambda, da_subln_g, na_w_qkv, na_w_o, na_rel_bias, gq_w_qkv, gq_w_o, gq_q_norm, gq_k_norm, hy_w_in, hy_short_w, hy_short_b, hy_ffn_w1, hy_ffn_b1, hy_ffn_w2, hy_ffn_b2, hy_ffn_freq, hy_ffn_w3, hy_log_decay, hy_filter_bias, hy_w_o):
    cvec = jnp.concatenate([c_ctx[None, :], c, jnp.zeros((MOD_ROWS - 1 - DEC_BATCH, D), F32)], axis=0)
    mods = _mods(cvec, ada_w, ada_b)
    rope_a, rope_b = _rope_tables()

    def finish(o_ctx, o_lat, xs, layer, w_o, split_out=False):
        return _post(o_ctx, o_lat, xs, mods, layer, w_o.astype(BF), ln_g[layer, 0][None], ln_b[layer, 0][None],
                     mlp_w1, mlp_w2, ln_g[layer, 1][None], ln_b[layer, 1][None], split_out)

    xs = (x_prompt.reshape(T_CTX, D), x_sample.reshape(T_LAT, D))
    qb, kb, vb, ks, vs = _da_proj(*xs, mods, 0, da_w_qkv[0].astype(BF), rope_a, rope_b)
    state_da_k = ks.reshape(BATCH, 1, SEQ, DA_HEADS, 2 * HEAD_DIM)
    state_da_v = vs.reshape(BATCH, 1, SEQ, DA_HEADS, 2 * HEAD_DIM)
    o_ctx, o_lat = _da_attention(qb, kb, vb, cache_da_k, cache_da_v, da_lambda[0], da_subln_g[0][None], 0)
    x = finish(o_ctx, o_lat, xs, 0, da_w_o[0])

    qb, kb, vb, ks, vs = _na_proj(x, mods, 1, na_w_qkv[0].astype(BF))
    state_na_k, state_na_v = _untranspose_state(ks, NA_HEADS), _untranspose_state(vs, NA_HEADS)
    onehot, neg, mask = _na_constants()
    bias_tab = _na_bias_table(na_rel_bias[0], onehot, neg)
    o_ctx = _na_ctx_attention(qb, kb, vb)
    o_lat = _na_lat_attention(qb, kb, vb, _features_major(cache_na_k), _features_major(cache_na_v), bias_tab, mask)
    x = finish(o_ctx, o_lat, (x,), 1, na_w_o[0])

    g_mat = jnp.asarray(np.kron(np.eye(GN_BLOCK // HEAD_DIM), np.full((HEAD_DIM, HEAD_DIM), 1.0 / HEAD_DIM)), BF)
    qb, kb, vb, ks, vs = _gq_proj(x, mods, 2, gq_w_qkv[0].astype(BF), g_mat,
                                  jnp.tile(gq_q_norm[0], GQ_HEADS)[None], jnp.tile(gq_k_norm[0], GQ_KV_HEADS)[None],
                                  rope_a, rope_b)
    state_gq_k, state_gq_v = _untranspose_state(ks, GQ_KV_HEADS), _untranspose_state(vs, GQ_KV_HEADS)
    o_ctx, o_lat = _gq_attention(qb, kb, vb, _features_major(cache_gq_k), _features_major(cache_gq_v))
    x = finish(o_ctx, o_lat, (x,), 2, gq_w_o[0])

    u = _hy_proj(x, mods, 3, hy_w_in[0].astype(BF))
    w1 = jnp.pad(hy_ffn_w1[0], ((0, HY_EMB_PAD - HY_EMB), (0, 0)))
    zs = []
    for seq, nbatch, row0, dc in ((SEQ, BATCH, 0, D), (DEC_SEQ, DEC_BATCH, T_CTX, 512)):
        cmat, smat = _dft_tables(seq)
        hre, him, hny = _hy_filter(seq, _hy_embedding(seq), w1, hy_ffn_b1[0][None], hy_ffn_w2[0], hy_ffn_b2[0][None],
                                   hy_ffn_freq[0][None], hy_ffn_w3[0], hy_log_decay[0][None], cmat, smat)
        zs.append(_hy_conv(u, hy_short_w[0], hy_short_b[0][None], hy_filter_bias[0], hre, him, hny, cmat, smat,
                           seq, nbatch, row0, dc))
    y_ctx, y_lat = finish(zs[0], zs[1], (x,), 3, hy_w_o[0], split_out=True)

    return (y_ctx.reshape(BATCH, SEQ, D), y_lat.reshape(DEC_BATCH, DEC_SEQ, D),
            state_da_k, state_da_v, state_na_k, state_na_v, state_gq_k, state_gq_v)
```

```python
import functools
import math

import numpy as np
import jax
import jax.numpy as jnp
from jax import lax
from jax.experimental import pallas as pl
from jax.experimental.pallas import tpu as pltpu

F32 = jnp.float32
BF = jnp.bfloat16

D = 1024
BATCH = 16
SEQ = 256
DEC_BATCH = 8
DEC_SEQ = 1024
PAST = 256
DEPTH = 4
GRID_W = 64
GRID_ROWS = DEC_SEQ // GRID_W
D_FF = 4 * D
T_CTX = BATCH * SEQ
T_LAT = DEC_BATCH * DEC_SEQ
T = T_CTX + T_LAT
HEAD_DIM = 64
ATT_SCALE = HEAD_DIM ** -0.5
LOG2E = math.log2(math.e)
Q_SCALE = ATT_SCALE * LOG2E
DA_HEADS = 8
NA_HEADS = 16
NA_WIN_ROWS = 8
NA_WIN_COLS = 16
GQ_HEADS = 16
GQ_KV_HEADS = 4
HY_ORDER = 2
HY_BANDS = 16
HY_EMB = 1 + 2 * HY_BANDS
HY_EMB_PAD = 40
HY_FFN = 64
ROPE_BASE = 10000.0
LN_EPS = 1e-5
RMS_EPS = 1e-6
DN_ALPHA = (2 * DEPTH) ** 0.25
NEG_INF = -1e30

LANES = 128
TM = 512
TM_POST = 512
FF_CHUNK = 512
MLP_CHUNK = 1024
N_FF_CHUNKS = D_FF // FF_CHUNK
SUB_POST = 256
N_CTX_TILES = T_CTX // TM
N_TILES = T // TM
TQ = 512
MOD_ROWS = 16
VMEM_LIMIT = 56 * 1024 * 1024


def _cparams(n_axes, flags=None):
    return pltpu.CompilerParams(dimension_semantics=("arbitrary",) * n_axes,
                                vmem_limit_bytes=VMEM_LIMIT, flags=flags)


def _dot(a, b):
    return jnp.dot(a, b, preferred_element_type=F32)


def _dot_nt(a, b):
    return lax.dot_general(a, b, (((1,), (1,)), ((), ())), preferred_element_type=F32)


def _const_spec(shape):
    nd = len(shape)
    return pl.BlockSpec(shape, lambda *_: (0,) * nd, pipeline_mode=pl.Buffered(1))


def _mod_spec(layer, tm=TM, off=0):
    nctx = T_CTX // tm

    def row(i):
        t = jnp.maximum(i - off, 0)
        return jnp.where(t < nctx, 0, 1 + (t - nctx) // (DEC_SEQ // tm))

    return pl.BlockSpec((None, None, 6, D), lambda i: (layer, row(i), 0, 0))


def _tok_spec(width, tm=TM, off=0):
    return pl.BlockSpec((tm, width), lambda i: (jnp.maximum(i - off, 0), 0))


def _ctx_spec(width, tm=TM, off=0):
    return pl.BlockSpec((tm, width), lambda i: (jnp.clip(i - off, 0, T_CTX // tm - 1), 0))


def _lat_spec(width, tm=TM, off=0):
    return pl.BlockSpec((tm, width), lambda i: (jnp.maximum(i - off - T_CTX // tm, 0), 0))


def _is_lat(tm=TM, off=0):
    return pl.program_id(0) >= off + T_CTX // tm


def _pick(ctx_ref, lat_ref):
    return jnp.where(_is_lat(), lat_ref[...], ctx_ref[...])


def _layer_norm(r, g, b):
    mu = jnp.mean(r, axis=-1, keepdims=True)
    c = r - mu
    var = jnp.mean(c * c, axis=-1, keepdims=True)
    return c * lax.rsqrt(var + LN_EPS) * g + b


def _mods_kernel(c_ref, w_ref, b_ref, o_ref):
    c = c_ref[...]
    s = (c / (1.0 + jnp.exp(-c))).astype(BF)
    o_ref[...] = _dot(s, w_ref[...].astype(BF)) + b_ref[...]


def _mods(cvec, ada_w, ada_b):
    tn = 1536
    out = pl.pallas_call(
        _mods_kernel,
        grid=(DEPTH, 6 * D // tn),
        in_specs=[pl.BlockSpec((MOD_ROWS, D), lambda l, n: (0, 0)),
                  pl.BlockSpec((None, D, tn), lambda l, n: (l, 0, n)),
                  pl.BlockSpec((None, 1, tn), lambda l, n: (l, 0, n))],
        out_specs=pl.BlockSpec((None, MOD_ROWS, tn), lambda l, n: (l, 0, n)),
        out_shape=jax.ShapeDtypeStruct((DEPTH, MOD_ROWS, 6 * D), F32),
        compiler_params=_cparams(2),
        name="adaln_mods",
    )(cvec, ada_w, ada_b.reshape(DEPTH, 1, 6 * D))
    return out.reshape(DEPTH, MOD_ROWS, 6, D)


def _modulate(x, mod_ref, shift, scale):
    return (x * (1.0 + mod_ref[scale:scale + 1, :]) + mod_ref[shift:shift + 1, :]).astype(BF)


def _rope(x, a, b):
    n = x.shape[1]
    lane = lax.broadcasted_iota(jnp.int32, x.shape, 1)
    partner = jnp.where((lane & 16) == 0, pltpu.roll(x, n - 16, 1), pltpu.roll(x, 16, 1))
    return x * a + partner * b


def _rope_spec(width):
    per = DEC_SEQ // TM
    return pl.BlockSpec((TM, width), lambda i: (jnp.where(i < N_CTX_TILES, per, (i - N_CTX_TILES) % per), 0))


def _store_state(k, v, ks_ref, vs_ref, transposed):
    @pl.when(jnp.logical_not(_is_lat()))
    def _():
        if not transposed:
            ks_ref[...] = k
            vs_ref[...] = v
        else:
            n = k.shape[1]
            for x, ref in ((k, ks_ref), (v, vs_ref)):
                xt = x.T
                for j in range(TM // SEQ):
                    ref[j * n:(j + 1) * n, :] = xt[:, j * SEQ:(j + 1) * SEQ]


def _qkv_out(nq, nk, transposed_state):
    specs = [_tok_spec(nq), _tok_spec(nk), _tok_spec(nk)]
    shapes = [jax.ShapeDtypeStruct((T, nq), BF), jax.ShapeDtypeStruct((T, nk), BF), jax.ShapeDtypeStruct((T, nk), BF)]
    if transposed_state:
        rows = (TM // SEQ) * nk
        specs += [pl.BlockSpec((rows, SEQ), lambda i: (jnp.minimum(i, N_CTX_TILES - 1), 0))] * 2
        shapes += [jax.ShapeDtypeStruct((BATCH * nk, SEQ), F32)] * 2
    else:
        specs += [_ctx_spec(nk)] * 2
        shapes += [jax.ShapeDtypeStruct((T_CTX, nk), F32)] * 2
    return specs, shapes


def _features_major(cache):
    b, _, past, heads, dh = cache.shape
    return cache.transpose(0, 1, 3, 4, 2).reshape(b, heads * dh, past)


def _untranspose_state(st, heads):
    return st.reshape(BATCH, heads, HEAD_DIM, SEQ).transpose(0, 3, 1, 2)[:, None]


def _da_proj_kernel(xc_ref, xl_ref, mod_ref, w_ref, ra_ref, rb_ref, qb_ref, kb_ref, vb_ref, ks_ref, vs_ref):
    h = _modulate(_pick(xc_ref, xl_ref), mod_ref, 0, 1)
    a, b = ra_ref[...], rb_ref[...]
    q = _dot(h, w_ref[:, 0:D])
    k = _dot(h, w_ref[:, D:2 * D])
    qb_ref[...] = (_rope(q, a, b) * Q_SCALE).astype(BF)
    v = _dot(h, w_ref[:, 2 * D:3 * D])
    kb_ref[...] = _rope(k, a, b).astype(BF)
    vb_ref[...] = v.astype(BF)
    _store_state(k, v, ks_ref, vs_ref, False)


def _da_proj(x_ctx, x_lat, mods, layer, w, rope_a, rope_b):
    specs, shapes = _qkv_out(D, D, False)
    return pl.pallas_call(
        _da_proj_kernel,
        grid=(N_TILES,),
        in_specs=[_ctx_spec(D), _lat_spec(D), _mod_spec(layer), _const_spec((D, 3 * D)),
                  _rope_spec(D), _rope_spec(D)],
        out_specs=specs, out_shape=shapes,
        compiler_params=_cparams(1),
        name=f"da_proj_l{layer}",
    )(x_ctx, x_lat, mods, w, rope_a, rope_b)


def _na_proj_kernel(x_ref, mod_ref, w_ref, qb_ref, kb_ref, vb_ref, ks_ref, vs_ref):
    h = _modulate(x_ref[...], mod_ref, 0, 1)
    q = _dot(h, w_ref[:, 0:D])
    k = _dot(h, w_ref[:, D:2 * D])
    qb_ref[...] = (q * Q_SCALE).astype(BF)
    v = _dot(h, w_ref[:, 2 * D:3 * D])
    kb_ref[...] = k.astype(BF)
    vb_ref[...] = v.astype(BF)
    _store_state(k, v, ks_ref, vs_ref, True)


def _na_proj(x, mods, layer, w):
    specs, shapes = _qkv_out(D, D, True)
    return pl.pallas_call(
        _na_proj_kernel,
        grid=(N_TILES,),
        in_specs=[_tok_spec(D), _mod_spec(layer), _const_spec((D, 3 * D))],
        out_specs=specs, out_shape=shapes,
        compiler_params=_cparams(1),
        name=f"na_proj_l{layer}",
    )(x, mods, w)


GN_BLOCK = 256


def _head_rms(x, g_ref, gain):
    x2 = x * x
    hi = x2.astype(BF)
    lo = (x2 - hi.astype(F32)).astype(BF)
    g = g_ref[...]
    ms = jnp.concatenate(
        [_dot(hi[:, j:j + GN_BLOCK], g) + _dot(lo[:, j:j + GN_BLOCK], g) for j in range(0, x.shape[1], GN_BLOCK)],
        axis=1)
    return x * lax.rsqrt(ms + RMS_EPS) * gain


def _gq_proj_kernel(x_ref, mod_ref, w_ref, g_ref, qn_ref, kn_ref, ra_ref, rb_ref,
                    qb_ref, kb_ref, vb_ref, ks_ref, vs_ref):
    nq, nk = GQ_HEADS * HEAD_DIM, GQ_KV_HEADS * HEAD_DIM
    h = _modulate(x_ref[...], mod_ref, 0, 1)
    a, b = ra_ref[...], rb_ref[...]
    q = _dot(h, w_ref[:, 0:nq])
    k = _dot(h, w_ref[:, nq:nq + nk])
    v = _dot(h, w_ref[:, nq + nk:nq + 2 * nk])
    k = _head_rms(k, g_ref, kn_ref[...])
    q = _head_rms(q, g_ref, qn_ref[...])
    kb_ref[...] = _rope(k, a[:, 0:nk], b[:, 0:nk]).astype(BF)
    qb_ref[...] = (_rope(q, a, b) * Q_SCALE).astype(BF)
    vb_ref[...] = v.astype(BF)
    _store_state(k, v, ks_ref, vs_ref, True)


def _gq_proj(x, mods, layer, w, g_mat, qn, kn, rope_a, rope_b):
    nq, nk = GQ_HEADS * HEAD_DIM, GQ_KV_HEADS * HEAD_DIM
    specs, shapes = _qkv_out(nq, nk, True)
    return pl.pallas_call(
        _gq_proj_kernel,
        grid=(N_TILES,),
        in_specs=[_tok_spec(D), _mod_spec(layer), _const_spec((D, nq + 2 * nk)),
                  _const_spec((GN_BLOCK, GN_BLOCK)), _const_spec((1, nq)), _const_spec((1, nk)),
                  _rope_spec(D), _rope_spec(D)],
        out_specs=specs, out_shape=shapes,
        compiler_params=_cparams(1),
        name=f"gq_proj_l{layer}",
    )(x, mods, w, g_mat, qn, kn, rope_a, rope_b)


def _hy_proj_kernel(x_ref, mod_ref, w_ref, u_ref):
    h = _modulate(x_ref[...], mod_ref, 0, 1)
    for c in range(HY_ORDER + 1):
        u_ref[:, c * D:(c + 1) * D] = _dot(h, w_ref[:, c * D:(c + 1) * D])


def _hy_proj(x, mods, layer, w):
    n = (HY_ORDER + 1) * D
    return pl.pallas_call(
        _hy_proj_kernel,
        grid=(N_TILES,),
        in_specs=[_tok_spec(D), _mod_spec(layer), _const_spec((D, n))],
        out_specs=_tok_spec(n),
        out_shape=jax.ShapeDtypeStruct((T, n), F32),
        compiler_params=_cparams(1),
        name=f"hy_proj_l{layer}",
    )(x, mods, w)


def _scores(qm, segs):
    return [_dot(qm, seg[0]) if len(seg) == 3 else _dot_nt(qm, seg[0]) for seg in segs]


def _softmax_finish(scores, segs):
    m = scores[0].max(axis=-1, keepdims=True)
    for s in scores[1:]:
        m = jnp.maximum(m, s.max(axis=-1, keepdims=True))
    den = None
    out = None
    for s, seg in zip(scores, segs):
        e = jnp.exp2(s - m)
        d = e.sum(axis=-1, keepdims=True)
        o = _dot_nt(e.astype(BF), seg[1]) if len(seg) == 3 else _dot(e.astype(BF), seg[1])
        den = d if den is None else den + d
        out = o if out is None else out + o
    return out / den


def _stack_halves(q, keep):
    return jnp.concatenate([q * keep[0], q * keep[1]], axis=0)


def _pipelined(jobs, score_fn, finish_fn):
    nxt = score_fn(jobs[0])
    for n, job in enumerate(jobs):
        cur, nxt = nxt, (score_fn(jobs[n + 1]) if n + 1 < len(jobs) else None)
        finish_fn(job, cur)


def _lane_half(shape):
    return lax.broadcasted_iota(jnp.int32, shape, 1) // HEAD_DIM


def _half_keep(half):
    return tuple(jnp.where(half == a, 1.0, 0.0).astype(BF) for a in (0, 1))


def _da_attn_kernel(*refs, has_cache, lam_init):
    if has_cache:
        q_ref, k_ref, v_ref, ck_ref, cv_ref, lam_ref, g_ref, o_ref = refs
    else:
        q_ref, k_ref, v_ref, lam_ref, g_ref, o_ref = refs
    lp = lam_ref[...]
    lam = (jnp.exp(jnp.sum(lp[0:1] * lp[1:2], axis=-1, keepdims=True))
           - jnp.exp(jnp.sum(lp[2:3] * lp[3:4], axis=-1, keepdims=True)) + lam_init)
    gain = g_ref[...] * (1.0 - lam_init)
    w = 2 * HEAD_DIM
    tq = min(TQ, q_ref.shape[0])
    keep = _half_keep(_lane_half((tq, w)))
    segs = []
    for hd in range(k_ref.shape[1] // w):
        cols = slice(hd * w, (hd + 1) * w)
        seg = [(k_ref[:, cols], v_ref[:, cols])]
        if has_cache:
            head = pl.program_id(1)
            seg.append((ck_ref[:, head, :].astype(BF), cv_ref[:, head, :].astype(BF)))
        segs.append(seg)
    jobs = [(hd, t, a) for hd in range(len(segs)) for t in range(q_ref.shape[0] // tq) for a in (0, 1)]
    first = {}

    def score_fn(job):
        hd, t, a = job
        return _scores(q_ref[t * tq:(t + 1) * tq, hd * w:(hd + 1) * w] * keep[a], segs[hd])

    def finish_fn(job, scores):
        hd, t, a = job
        o = _softmax_finish(scores, segs[hd])
        if a == 0:
            first[0] = o
            return
        o = first[0] - lam * o
        ms = jnp.mean(o * o, axis=-1, keepdims=True)
        o_ref[t * tq:(t + 1) * tq, hd * w:(hd + 1) * w] = (o * lax.rsqrt(ms + RMS_EPS) * gain).astype(BF)

    _pipelined(jobs, score_fn, finish_fn)


def _da_attention(qb, kb, vb, cache_k, cache_v, lam_p, subln_g, layer_idx):
    lam_init = 0.8 - 0.6 * math.exp(-0.3 * layer_idx)
    w = 2 * HEAD_DIM
    small = [pl.BlockSpec((4, HEAD_DIM), lambda *_: (0, 0)), pl.BlockSpec((1, w), lambda *_: (0, 0))]
    o_ctx = pl.pallas_call(
        functools.partial(_da_attn_kernel, has_cache=False, lam_init=lam_init),
        grid=(BATCH,),
        in_specs=[pl.BlockSpec((SEQ, D), lambda b: (b, 0))] * 3 + small,
        out_specs=pl.BlockSpec((SEQ, D), lambda b: (b, 0)),
        out_shape=jax.ShapeDtypeStruct((T_CTX, D), BF),
        compiler_params=_cparams(1),
        name="da_attn_ctx",
    )(qb, kb, vb, lam_p, subln_g)
    k0 = T_CTX // DEC_SEQ
    tok = pl.BlockSpec((DEC_SEQ, w), lambda b, h: (k0 + b, h))
    c_spec = pl.BlockSpec((None, None, PAST, DA_HEADS, w), lambda b, h: (b, 0, 0, 0, 0))
    o_lat = pl.pallas_call(
        functools.partial(_da_attn_kernel, has_cache=True, lam_init=lam_init),
        grid=(DEC_BATCH, DA_HEADS),
        in_specs=[tok, tok, tok, c_spec, c_spec] + small,
        out_specs=pl.BlockSpec((DEC_SEQ, w), lambda b, h: (b, h)),
        out_shape=jax.ShapeDtypeStruct((T_LAT, D), BF),
        compiler_params=_cparams(2),
        name="da_attn_lat",
    )(qb, kb, vb, cache_k, cache_v, lam_p, subln_g)
    return o_ctx, o_lat


def _na_ctx_kernel(q_ref, k_ref, v_ref, o_ref):
    half = _lane_half((SEQ, LANES))
    keep = _half_keep(half)

    def seg(p):
        return [(k_ref[:, p * LANES:(p + 1) * LANES], v_ref[:, p * LANES:(p + 1) * LANES])]

    def score_fn(p):
        return _scores(_stack_halves(q_ref[:, p * LANES:(p + 1) * LANES], keep), seg(p))

    def finish_fn(p, scores):
        o = _softmax_finish(scores, seg(p))
        o_ref[:, p * LANES:(p + 1) * LANES] = jnp.where(half == 0, o[0:SEQ], o[SEQ:2 * SEQ]).astype(BF)

    _pipelined(list(range(NA_HEADS // 2)), score_fn, finish_fn)


def _na_ctx_attention(qb, kb, vb):
    spec = pl.BlockSpec((SEQ, D), lambda b: (b, 0))
    return pl.pallas_call(
        _na_ctx_kernel,
        grid=(BATCH,),
        in_specs=[spec] * 3,
        out_specs=spec,
        out_shape=jax.ShapeDtypeStruct((T_CTX, D), BF),
        compiler_params=_cparams(1),
        name="na_attn_ctx",
    )(qb, kb, vb)


NA_TILES = ((0, (0, 2, 4, 6)), (4, (0, 2, 4, 6, 8, 10)), (8, (4, 6, 8, 10, 12, 14)), (12, (8, 10, 12, 14)))
NA_MAX_CHUNKS = 6
NA_BIAS_BLOCKS = 2 * NA_WIN_ROWS - 2


def _na_lat_kernel(q_ref, k_ref, v_ref, ck_ref, cv_ref, w_ref, m_ref, o_ref):
    ckt, cvt = ck_ref[...].astype(BF), cv_ref[...].astype(BF)
    rows = 4 * GRID_W
    half = _lane_half((rows, LANES))
    keep = _half_keep(half)

    def key_rows(i):
        chunks = NA_TILES[i][1]
        return slice(chunks[0] * GRID_W, chunks[0] * GRID_W + len(chunks) * LANES)

    def score_fn(i):
        r0, chunks = NA_TILES[i]
        qm = _stack_halves(q_ref[i * rows:(i + 1) * rows, :], keep)
        mask = m_ref[i, :, 0:len(chunks) * LANES]
        bias = jnp.concatenate(
            [jnp.concatenate([w_ref[a, (6 - kr + r0) * GRID_W:(6 - kr + r0) * GRID_W + rows, :] for kr in chunks],
                             axis=1) + mask for a in (0, 1)], axis=0)
        return [_dot_nt(qm, k_ref[key_rows(i), :]) + bias, _dot(qm, ckt)]

    def finish_fn(i, scores):
        o = _softmax_finish(scores, [(None, v_ref[key_rows(i), :]), (None, cvt, True)])
        o_ref[i * rows:(i + 1) * rows, :] = jnp.where(half == 0, o[0:rows], o[rows:2 * rows]).astype(BF)

    _pipelined(list(range(len(NA_TILES))), score_fn, finish_fn)


def _na_lat_attention(qb, kb, vb, cache_k, cache_v, bias_tab, mask_tab):
    k0 = T_CTX // DEC_SEQ
    tok = pl.BlockSpec((DEC_SEQ, LANES), lambda b, p: (k0 + b, p))
    c_spec = pl.BlockSpec((None, LANES, PAST), lambda b, p: (b, p, 0))
    return pl.pallas_call(
        _na_lat_kernel,
        grid=(DEC_BATCH, NA_HEADS // 2),
        in_specs=[tok, tok, tok, c_spec, c_spec,
                  pl.BlockSpec((None, 2, NA_BIAS_BLOCKS * GRID_W, LANES), lambda b, p: (p, 0, 0, 0)),
                  _const_spec(mask_tab.shape)],
        out_specs=pl.BlockSpec((DEC_SEQ, LANES), lambda b, p: (b, p)),
        out_shape=jax.ShapeDtypeStruct((T_LAT, D), BF),
        compiler_params=_cparams(2),
        name="na_attn_lat",
    )(qb, kb, vb, cache_k, cache_v, bias_tab, mask_tab)


def _na_bias_kernel(t_ref, r_ref, n_ref, o_ref):
    t = t_ref[...]
    t1 = t.astype(BF)
    r1 = t - t1.astype(F32)
    t2 = r1.astype(BF)
    t3 = (r1 - t2.astype(F32)).astype(BF)
    r = r_ref[...]
    res = (_dot(t1, r) + _dot(t2, r) + _dot(t3, r) + n_ref[...]) * LOG2E
    for qc in range(GRID_W):
        o_ref[pl.ds(qc, t.shape[0], stride=GRID_W), :] = res[:, qc * LANES:(qc + 1) * LANES]


def _na_bias_table(rel_bias, onehot, neg):
    nrel = 2 * NA_WIN_COLS
    idx = 13 - np.arange(NA_BIAS_BLOCKS)[:, None] + np.arange(2)[None, :]
    t = jnp.pad(rel_bias[:, idx, :], ((0, 0), (0, 0), (0, 0), (0, 1)))
    t = t.reshape(NA_HEADS * NA_BIAS_BLOCKS, 2 * nrel)
    n = GRID_W * LANES
    out = pl.pallas_call(
        _na_bias_kernel,
        grid=(1,),
        in_specs=[pl.BlockSpec(t.shape, lambda j: (0, 0)),
                  pl.BlockSpec((2 * nrel, n), lambda j: (0, 0)),
                  pl.BlockSpec((1, n), lambda j: (0, 0))],
        out_specs=pl.BlockSpec((t.shape[0] * GRID_W, LANES), lambda j: (0, 0)),
        out_shape=jax.ShapeDtypeStruct((t.shape[0] * GRID_W, LANES), F32),
        compiler_params=_cparams(1),
        name="na_bias_table",
    )(t, onehot, neg)
    return out.reshape(NA_HEADS // 2, 2, NA_BIAS_BLOCKS * GRID_W, LANES)


def _na_constants():
    nrel = 2 * NA_WIN_COLS
    qc = np.arange(GRID_W)[:, None]
    kc = np.arange(GRID_W)[None, :]
    rel = np.clip(kc - qc, -(NA_WIN_COLS - 1), NA_WIN_COLS - 1) + NA_WIN_COLS - 1
    cs = np.clip(qc - NA_WIN_COLS // 2, 0, GRID_W - NA_WIN_COLS)
    col_in = (kc >= cs) & (kc < cs + NA_WIN_COLS)
    onehot = np.zeros((2, nrel, GRID_W, 2, GRID_W), np.float32)
    for hf in range(2):
        onehot[hf, rel, qc, hf, kc] = 1.0
    neg = np.where(col_in, 0.0, NEG_INF).astype(np.float32)
    neg = np.broadcast_to(neg[:, None, :], (GRID_W, 2, GRID_W)).reshape(1, -1)
    rows = 4 * GRID_W
    mask = np.full((len(NA_TILES), rows, NA_MAX_CHUNKS * LANES), NEG_INF, np.float32)
    kr = min(NA_WIN_ROWS, GRID_ROWS)
    for i, (r0, chunks) in enumerate(NA_TILES):
        qr = r0 + np.arange(rows)[:, None] // GRID_W
        rs = np.clip(qr - kr // 2, 0, GRID_ROWS - kr)
        for c, krow0 in enumerate(chunks):
            krow = krow0 + np.arange(LANES)[None, :] // GRID_W
            mask[i, :, c * LANES:(c + 1) * LANES] = np.where((krow >= rs) & (krow < rs + kr), 0.0, NEG_INF)
    return (jnp.asarray(onehot.reshape(2 * nrel, GRID_W * LANES), BF), jnp.asarray(neg), jnp.asarray(mask))


def _gq_attn_kernel(*refs, has_cache):
    if has_cache:
        q_ref, k_ref, v_ref, ck_ref, cv_ref, o_ref = refs
    else:
        q_ref, k_ref, v_ref, o_ref = refs
    group = GQ_HEADS // GQ_KV_HEADS
    qw = LANES * group
    tq = min(TQ // 2, q_ref.shape[0])
    half = _lane_half((tq, LANES))
    keep = _half_keep(half)
    segs = []
    for kvp in range(k_ref.shape[1] // LANES):
        kcols = slice(kvp * LANES, (kvp + 1) * LANES)
        seg = [(k_ref[:, kcols], v_ref[:, kcols])]
        if has_cache:
            seg.append((ck_ref[kcols, :].astype(BF), cv_ref[kcols, :].astype(BF), True))
        segs.append(seg)
    jobs = [(kvp, t, kh) for kvp in range(len(segs)) for t in range(q_ref.shape[0] // tq) for kh in (0, 1)]

    def blocks(job):
        kvp, t, kh = job
        for pair in (2 * kh, 2 * kh + 1):
            yield slice(t * tq, (t + 1) * tq), slice(kvp * qw + pair * LANES, kvp * qw + (pair + 1) * LANES)

    def score_fn(job):
        kh = job[2]
        parts = []
        for rows, cols in blocks(job):
            for a in (0, 1):
                qm = q_ref[rows, cols] * keep[a]
                parts.append(qm if a == kh else pltpu.roll(qm.astype(F32), HEAD_DIM, 1).astype(BF))
        return _scores(jnp.concatenate(parts, axis=0), segs[job[0]])

    def finish_fn(job, scores):
        kh = job[2]
        o = _softmax_finish(scores, segs[job[0]])
        for n, (rows, cols) in enumerate(blocks(job)):
            heads = [o[(2 * n + a) * tq:(2 * n + a + 1) * tq] for a in (0, 1)]
            heads = [h if a == kh else pltpu.roll(h, HEAD_DIM, 1) for a, h in enumerate(heads)]
            o_ref[rows, cols] = jnp.where(half == 0, heads[0], heads[1]).astype(BF)

    _pipelined(jobs, score_fn, finish_fn)


def _gq_attention(qb, kb, vb, cache_k, cache_v):
    nk = GQ_KV_HEADS * HEAD_DIM
    qw = LANES * (GQ_HEADS // GQ_KV_HEADS)
    npair = GQ_KV_HEADS // 2
    o_ctx = pl.pallas_call(
        functools.partial(_gq_attn_kernel, has_cache=False),
        grid=(BATCH,),
        in_specs=[pl.BlockSpec((SEQ, D), lambda b: (b, 0))] + [pl.BlockSpec((SEQ, nk), lambda b: (b, 0))] * 2,
        out_specs=pl.BlockSpec((SEQ, D), lambda b: (b, 0)),
        out_shape=jax.ShapeDtypeStruct((T_CTX, D), BF),
        compiler_params=_cparams(1),
        name="gq_attn_ctx",
    )(qb, kb, vb)
    qt = DEC_SEQ // TQ
    q0, k0 = T_CTX // TQ, T_CTX // DEC_SEQ
    kv_spec = pl.BlockSpec((DEC_SEQ, LANES), lambda b, p, t: (k0 + b, p))
    c_spec = pl.BlockSpec((None, LANES, PAST), lambda b, p, t: (b, p, 0))
    o_lat = pl.pallas_call(
        functools.partial(_gq_attn_kernel, has_cache=True),
        grid=(DEC_BATCH, npair, qt),
        in_specs=[pl.BlockSpec((TQ, qw), lambda b, p, t: (q0 + b * qt + t, p)), kv_spec, kv_spec, c_spec, c_spec],
        out_specs=pl.BlockSpec((TQ, qw), lambda b, p, t: (b * qt + t, p)),
        out_shape=jax.ShapeDtypeStruct((T_LAT, D), BF),
        compiler_params=_cparams(3),
        name="gq_attn_lat",
    )(qb, kb, vb, cache_k, cache_v)
    return o_ctx, o_lat


def _dot_3pass(a, b):
    ah, bh = a.astype(BF), b.astype(BF)
    al, bl = (a - ah.astype(F32)).astype(BF), (b - bh.astype(F32)).astype(BF)
    return _dot(ah, bh) + _dot(ah, bl) + _dot(al, bh)


def _hy_filter_kernel(emb_ref, w1_ref, b1_ref, w2_ref, b2_ref, fr_ref, w3f_ref, w3b_ref, ldf_ref, ldb_ref,
                      c_ref, s_ref, hre_ref, him_ref, hny_ref, hid_ref, cb_ref, sb_ref):
    seq = emb_ref.shape[0]

    @pl.when((pl.program_id(0) == 0) & (pl.program_id(1) == 0))
    def _():
        hp = lax.Precision.HIGHEST
        fr = fr_ref[...]
        hid = jnp.sin(fr * (jnp.dot(emb_ref[...], w1_ref[...], precision=hp, preferred_element_type=F32)
                            + b1_ref[...]))
        hid_ref[...] = jnp.sin(fr * (jnp.dot(hid, w2_ref[...], precision=hp, preferred_element_type=F32)
                                     + b2_ref[...]))
        cb_ref[...] = c_ref[...].astype(BF)
        sb_ref[...] = s_ref[...].astype(BF)

    hid = hid_ref[...]
    t = emb_ref[:, 0:1]
    fwd = _dot_3pass(hid, w3f_ref[...]) * jnp.exp(-jnp.exp(ldf_ref[...]) * t)
    bwd = _dot_3pass(hid, w3b_ref[...]) * jnp.exp(-jnp.exp(ldb_ref[...]) * t)
    row = lax.broadcasted_iota(jnp.int32, fwd.shape, 0)
    bwd = jnp.where(row == 0, 0.0, bwd)
    even = fwd + bwd
    odd = bwd - fwd
    wk = jnp.where(row == 0, 0.5 / seq, 1.0 / seq)
    hre_ref[...] = _dot(cb_ref[...], even.astype(BF)) * wk
    him_ref[...] = _dot(sb_ref[...], odd.astype(BF)) * wk
    alt = jnp.where((row & 1) == 0, 1.0, -1.0)
    hny_ref[...] = jnp.sum(alt * even, axis=0, keepdims=True) * (0.5 / seq)


def _hy_filter(seq, emb, w1, b1, w2, b2, freq, w3, log_decay, cmat, smat):
    dc = 512
    nj = D // dc
    small = [_const_spec(a.shape) for a in (emb, w1, b1, w2, b2, freq)]
    return pl.pallas_call(
        _hy_filter_kernel,
        grid=(HY_ORDER, nj),
        in_specs=small + [pl.BlockSpec((HY_FFN, dc), lambda o, j: (0, (2 * o) * nj + j)),
                          pl.BlockSpec((HY_FFN, dc), lambda o, j: (0, (2 * o + 1) * nj + j)),
                          pl.BlockSpec((1, dc), lambda o, j: (0, (2 * o) * nj + j)),
                          pl.BlockSpec((1, dc), lambda o, j: (0, (2 * o + 1) * nj + j)),
                          _const_spec((seq, seq)), _const_spec((seq, seq))],
        out_specs=[pl.BlockSpec((None, seq, dc), lambda o, j: (o, 0, j)),
                   pl.BlockSpec((None, seq, dc), lambda o, j: (o, 0, j)),
                   pl.BlockSpec((None, 1, dc), lambda o, j: (o, 0, j))],
        out_shape=[jax.ShapeDtypeStruct((HY_ORDER, seq, D), F32), jax.ShapeDtypeStruct((HY_ORDER, seq, D), F32),
                   jax.ShapeDtypeStruct((HY_ORDER, 1, D), F32)],
        scratch_shapes=[pltpu.VMEM((seq, HY_FFN), F32), pltpu.VMEM((seq, seq), BF), pltpu.VMEM((seq, seq), BF)],
        compiler_params=_cparams(2),
        name=f"hy_filter_{seq}",
    )(emb, w1, b1, w2, b2, freq, w3, w3, log_decay, log_decay, cmat, smat)


HY_SUB = 256


def _hy_conv_kernel(u0_ref, u1_ref, u2_ref, sw0_ref, sw1_ref, sw2_ref, sb0_ref, sb1_ref, sb2_ref,
                    fb_ref, hre_ref, him_ref, hny_ref, c_ref, s_ref, o_ref, cb_ref, sb_ref):
    seq, dc = u0_ref.shape

    @pl.when((pl.program_id(0) == 0) & (pl.program_id(1) == 0))
    def _():
        cb_ref[...] = c_ref[...].astype(BF)
        sb_ref[...] = s_ref[...].astype(BF)

    row = lax.broadcasted_iota(jnp.int32, (seq, HY_SUB), 0)
    alt = jnp.where((row & 1) == 0, 1.0, -1.0)

    def sub_tile(cols):
        def short_conv(u_ref, w_ref, b_ref):
            u = u_ref[:, cols]
            prev = jnp.where(row == 0, 0.0, pltpu.roll(u, 1, 0))
            nxt = jnp.where(row == seq - 1, 0.0, pltpu.roll(u, seq - 1, 0))
            return prev * w_ref[0:1, cols] + u * w_ref[1:2, cols] + nxt * w_ref[2:3, cols] + b_ref[:, cols]

        z = short_conv(u0_ref, sw0_ref, sb0_ref)
        gates = (short_conv(u1_ref, sw1_ref, sb1_ref), short_conv(u2_ref, sw2_ref, sb2_ref))
        yield
        for o in range(HY_ORDER):
            zb = z.astype(BF)
            zc, zs = _dot(cb_ref[...], zb), _dot(sb_ref[...], zb)
            yield
            hre, him = hre_ref[o, :, cols], him_ref[o, :, cols]
            p_re = (zc * hre + zs * him).astype(BF)
            p_im = (zc * him - zs * hre).astype(BF)
            y = _dot(cb_ref[...], p_re) - _dot(sb_ref[...], p_im)
            yield
            nyq = jnp.sum(alt * z, axis=0, keepdims=True) * hny_ref[o, :, cols]
            z = gates[o] * (y + alt * nyq + z * fb_ref[o:o + 1, cols])
        o_ref[:, cols] = z.astype(BF)

    tiles = [sub_tile(slice(j * HY_SUB, (j + 1) * HY_SUB)) for j in range(dc // HY_SUB)]
    while tiles:
        tiles = [t for t in tiles if next(t, True) is None]


def _hy_conv(u, short_w, short_b, filter_bias, hre, him, hny, cmat, smat, seq, nbatch, row0, dc):
    nj = D // dc
    r0 = row0 // seq

    def part(p):
        return pl.BlockSpec((seq, dc), lambda j, b: (r0 + b, p * nj + j))

    def vec(rows, p):
        return pl.BlockSpec((rows, dc), lambda j, b: (0, p * nj + j))

    in_specs = ([part(p) for p in range(3)] + [vec(3, p) for p in range(3)] + [vec(1, p) for p in range(3)]
                + [pl.BlockSpec((HY_ORDER, dc), lambda j, b: (0, j)),
                   pl.BlockSpec((HY_ORDER, seq, dc), lambda j, b: (0, 0, j), pipeline_mode=pl.Buffered(1)),
                   pl.BlockSpec((HY_ORDER, seq, dc), lambda j, b: (0, 0, j), pipeline_mode=pl.Buffered(1)),
                   pl.BlockSpec((HY_ORDER, 1, dc), lambda j, b: (0, 0, j)),
                   _const_spec((seq, seq)), _const_spec((seq, seq))])
    return pl.pallas_call(
        _hy_conv_kernel,
        grid=(nj, nbatch),
        in_specs=in_specs,
        out_specs=pl.BlockSpec((seq, dc), lambda j, b: (b, j)),
        out_shape=jax.ShapeDtypeStruct((nbatch * seq, D), BF),
        scratch_shapes=[pltpu.VMEM((seq, seq), BF), pltpu.VMEM((seq, seq), BF)],
        compiler_params=_cparams(2),
        name=f"hy_conv_{seq}",
    )(u, u, u, short_w, short_w, short_w, short_b, short_b, short_b, filter_bias, hre, him, hny, cmat, smat)


def _dft_tables(seq):
    k = np.arange(seq, dtype=np.int64)
    ang = np.pi * ((k[:, None] * k[None, :]) % (2 * seq)) / seq
    return jnp.asarray(np.cos(ang), F32), jnp.asarray(np.sin(ang), F32)


def _hy_embedding(seq):
    t = np.arange(seq, dtype=np.float32) / np.float32(seq)
    ang = (2.0 * math.pi) * t[:, None] * np.arange(1, HY_BANDS + 1, dtype=np.float32)
    emb = np.concatenate([t[:, None], np.cos(ang), np.sin(ang)], axis=-1).astype(np.float32)
    return jnp.asarray(np.pad(emb, ((0, 0), (0, HY_EMB_PAD - HY_EMB))))


def _post_kernel(*refs, split_x, split_out, tm):
    oc_ref, ol_ref = refs[0:2]
    x_refs, refs = (refs[2:4], refs[4:]) if split_x else (refs[2:3], refs[3:])
    mod_ref, wo_ref, g1_ref, b1_ref, w1c_ref, w2c_ref, g2_ref, b2_ref = refs[0:8]
    outs, (w1_ref, w2_ref) = refs[8:-2], refs[-2:]
    step = pl.program_id(0)
    is_lat = _is_lat(tm, N_FF_CHUNKS)
    nsub = tm // SUB_POST

    @pl.when(step < N_FF_CHUNKS)
    def _():
        per = MLP_CHUNK // FF_CHUNK
        w1_ref[step] = w1c_ref[...].astype(BF)
        w2_ref[step // per, pl.ds(pl.multiple_of((step % per) * FF_CHUNK, FF_CHUNK), FF_CHUNK), :] = (
            w2c_ref[...].astype(BF))

    def rows(j):
        return slice(j * SUB_POST, (j + 1) * SUB_POST)

    def pick(c_ref, l_ref, j):
        return jnp.where(is_lat, l_ref[rows(j), :], c_ref[rows(j), :])

    def norm1(j):
        a = _dot(pick(oc_ref, ol_ref, j), wo_ref[...])
        x = pick(x_refs[0], x_refs[1], j) if split_x else x_refs[0][rows(j), :]
        x1 = _layer_norm(DN_ALPHA * x + mod_ref[2:3, :] * a, g1_ref[...], b1_ref[...])
        return x1, _modulate(x1, mod_ref, 3, 4)

    def mlp_chunk(h, c):
        per = MLP_CHUNK // FF_CHUNK
        a = jnp.concatenate([_dot(h, w1_ref[per * c + i]) for i in range(per)], axis=1)
        a = jnp.maximum(a, 0.0)
        return _dot((a * a).astype(BF), w2_ref[c])

    def token_tile():
        ys = []

        def norm2_store(j, x1, acc):
            y = _layer_norm(DN_ALPHA * x1 + mod_ref[5:6, :] * acc, g2_ref[...], b2_ref[...])
            if split_out:
                ys.append(y)
            else:
                outs[0][rows(j), :] = y

        cur = norm1(0)
        prev = None
        for j in range(nsub):
            x1, h = cur
            acc = mlp_chunk(h, 0)
            if j + 1 < nsub:
                cur = norm1(j + 1)
            if prev is not None:
                norm2_store(j - 1, *prev)
            for c in range(1, D_FF // MLP_CHUNK):
                acc = acc + mlp_chunk(h, c)
            prev = (x1, acc)
        norm2_store(nsub - 1, *prev)
        if split_out:
            yc_ref, yl_ref = outs

            @pl.when(jnp.logical_not(is_lat))
            def _():
                for j, y in enumerate(ys):
                    yc_ref[rows(j), :] = y

            @pl.when(is_lat)
            def _():
                for j, y in enumerate(ys):
                    yl_ref[rows(j), :] = y

    pl.when(step >= N_FF_CHUNKS)(token_tile)


def _post(o_ctx, o_lat, xs, mods, layer, w_o, g1, b1, w1, w2, g2, b2, split_out):
    tm, off = TM_POST, N_FF_CHUNKS
    split_x = len(xs) == 2
    x_specs = [_ctx_spec(D, tm, off), _lat_spec(D, tm, off)] if split_x else [_tok_spec(D, tm, off)]
    vec = _const_spec((1, D))
    if split_out:
        out_specs = [_ctx_spec(D, tm, off), _lat_spec(D, tm, off)]
        out_shape = [jax.ShapeDtypeStruct((T_CTX, D), F32), jax.ShapeDtypeStruct((T_LAT, D), F32)]
    else:
        out_specs = _tok_spec(D, tm, off)
        out_shape = jax.ShapeDtypeStruct((T, D), F32)

    def chunk(i):
        return jnp.minimum(i, N_FF_CHUNKS - 1)

    return pl.pallas_call(
        functools.partial(_post_kernel, split_x=split_x, split_out=split_out, tm=tm),
        grid=(off + T // tm,),
        in_specs=[_ctx_spec(D, tm, off), _lat_spec(D, tm, off)] + x_specs + [
            _mod_spec(layer, tm, off), _const_spec((D, D)), vec, vec,
            pl.BlockSpec((None, D, FF_CHUNK), lambda i: (layer, 0, chunk(i))),
            pl.BlockSpec((None, FF_CHUNK, D), lambda i: (layer, chunk(i), 0)), vec, vec],
        out_specs=out_specs,
        out_shape=out_shape,
        scratch_shapes=[pltpu.VMEM((N_FF_CHUNKS, D, FF_CHUNK), BF),
                        pltpu.VMEM((D_FF // MLP_CHUNK, MLP_CHUNK, D), BF)],
        compiler_params=_cparams(1),
        name=f"post_l{layer}",
    )(o_ctx, o_lat, *xs, mods, w_o, g1, b1, w1, w2, g2, b2)


def _rope_tables():
    n = HEAD_DIM // 4
    pos = np.arange(DEC_SEQ)
    inv = (np.float32(ROPE_BASE) ** (-np.arange(n, dtype=np.float32) / np.float32(n))).astype(np.float32)
    ang_r = ((pos // GRID_W).astype(np.float32)[:, None] * inv).astype(np.float32)
    ang_c = ((pos % GRID_W).astype(np.float32)[:, None] * inv).astype(np.float32)
    cr, sr, cc, sc = np.cos(ang_r), np.sin(ang_r), np.cos(ang_c), np.sin(ang_c)
    a = np.tile(np.concatenate([cr, cr, cc, cc], axis=-1), (1, D // HEAD_DIM))
    b = np.tile(np.concatenate([-sr, sr, -sc, sc], axis=-1), (1, D // HEAD_DIM))
    a = np.concatenate([a, np.ones((TM, D), np.float32)], axis=0)
    b = np.concatenate([b, np.zeros((TM, D), np.float32)], axis=0)
    return jnp.asarray(a, F32), jnp.asarray(b, F32)


def kernel(x_prompt, x_sample, c, cache_da_k, cache_da_v, cache_na_k, cache_na_v, cache_gq_k, cache_gq_v, c_ctx, ada_w, ada_b, ln_g, ln_b, mlp_w1, mlp_w2, da_w_qkv, da_w_o, da_lambda, da_subln_g, na_w_qkv, na_w_o, na_rel_bias, gq_w_qkv, gq_w_o, gq_q_norm, gq_k_norm, hy_w_in, hy_short_w, hy_short_b, hy_ffn_w1, hy_ffn_b1, hy_ffn_w2, hy_ffn_b2, hy_ffn_freq, hy_ffn_w3, hy_log_decay, hy_filter_bias, hy_w_o):
    cvec = jnp.concatenate([c_ctx[None, :], c, jnp.zeros((MOD_ROWS - 1 - DEC_BATCH, D), F32)], axis=0)
    mods = _mods(cvec, ada_w, ada_b)
    rope_a, rope_b = _rope_tables()

    def finish(o_ctx, o_lat, xs, layer, w_o, split_out=False):
        return _post(o_ctx, o_lat, xs, mods, layer, w_o.astype(BF), ln_g[layer, 0][None], ln_b[layer, 0][None],
                     mlp_w1, mlp_w2, ln_g[layer, 1][None], ln_b[layer, 1][None], split_out)

    xs = (x_prompt.reshape(T_CTX, D), x_sample.reshape(T_LAT, D))
    qb, kb, vb, ks, vs = _da_proj(*xs, mods, 0, da_w_qkv[0].astype(BF), rope_a, rope_b)
    state_da_k = ks.reshape(BATCH, 1, SEQ, DA_HEADS, 2 * HEAD_DIM)
    state_da_v = vs.reshape(BATCH, 1, SEQ, DA_HEADS, 2 * HEAD_DIM)
    o_ctx, o_lat = _da_attention(qb, kb, vb, cache_da_k, cache_da_v, da_lambda[0], da_subln_g[0][None], 0)
    x = finish(o_ctx, o_lat, xs, 0, da_w_o[0])

    qb, kb, vb, ks, vs = _na_proj(x, mods, 1, na_w_qkv[0].astype(BF))
    state_na_k, state_na_v = _untranspose_state(ks, NA_HEADS), _untranspose_state(vs, NA_HEADS)
    onehot, neg, mask = _na_constants()
    bias_tab = _na_bias_table(na_rel_bias[0], onehot, neg)
    o_ctx = _na_ctx_attention(qb, kb, vb)
    o_lat = _na_lat_attention(qb, kb, vb, _features_major(cache_na_k), _features_major(cache_na_v), bias_tab, mask)
    x = finish(o_ctx, o_lat, (x,), 1, na_w_o[0])

    g_mat = jnp.asarray(np.kron(np.eye(GN_BLOCK // HEAD_DIM), np.full((HEAD_DIM, HEAD_DIM), 1.0 / HEAD_DIM)), BF)
    qb, kb, vb, ks, vs = _gq_proj(x, mods, 2, gq_w_qkv[0].astype(BF), g_mat,
                                  jnp.tile(gq_q_norm[0], GQ_HEADS)[None], jnp.tile(gq_k_norm[0], GQ_KV_HEADS)[None],
                                  rope_a, rope_b)
    state_gq_k, state_gq_v = _untranspose_state(ks, GQ_KV_HEADS), _untranspose_state(vs, GQ_KV_HEADS)
    o_ctx, o_lat = _gq_attention(qb, kb, vb, _features_major(cache_gq_k), _features_major(cache_gq_v))
    x = finish(o_ctx, o_lat, (x,), 2, gq_w_o[0])

    u = _hy_proj(x, mods, 3, hy_w_in[0].astype(BF))
    w1 = jnp.pad(hy_ffn_w1[0], ((0, HY_EMB_PAD - HY_EMB), (0, 0)))
    zs = []
    for seq, nbatch, row0, dc in ((SEQ, BATCH, 0, D), (DEC_SEQ, DEC_BATCH, T_CTX, 512)):
        cmat, smat = _dft_tables(seq)
        hre, him, hny = _hy_filter(seq, _hy_embedding(seq), w1, hy_ffn_b1[0][None], hy_ffn_w2[0], hy_ffn_b2[0][None],
                                   hy_ffn_freq[0][None], hy_ffn_w3[0], hy_log_decay[0][None], cmat, smat)
        zs.append(_hy_conv(u, hy_short_w[0], hy_short_b[0][None], hy_filter_bias[0], hre, him, hny, cmat, smat,
                           seq, nbatch, row0, dc))
    y_ctx, y_lat = finish(zs[0], zs[1], (x,), 3, hy_w_o[0], split_out=True)

    return (y_ctx.reshape(BATCH, SEQ, D), y_lat.reshape(DEC_BATCH, DEC_SEQ, D),
            state_da_k, state_da_v, state_na_k, state_na_v, state_gq_k, state_gq_v)
```

```python
import functools
import math

import numpy as np
import jax
import jax.numpy as jnp
from jax import lax
from jax.experimental import pallas as pl
from jax.experimental.pallas import tpu as pltpu

F32 = jnp.float32
BF = jnp.bfloat16

D = 1024
BATCH = 16
SEQ = 256
DEC_BATCH = 8
DEC_SEQ = 1024
PAST = 256
DEPTH = 4
GRID_W = 64
GRID_ROWS = DEC_SEQ // GRID_W
D_FF = 4 * D
T_CTX = BATCH * SEQ
T_LAT = DEC_BATCH * DEC_SEQ
T = T_CTX + T_LAT
HEAD_DIM = 64
ATT_SCALE = HEAD_DIM ** -0.5
LOG2E = math.log2(math.e)
Q_SCALE = ATT_SCALE * LOG2E
DA_HEADS = 8
NA_HEADS = 16
NA_WIN_ROWS = 8
NA_WIN_COLS = 16
GQ_HEADS = 16
GQ_KV_HEADS = 4
HY_ORDER = 2
HY_BANDS = 16
HY_EMB = 1 + 2 * HY_BANDS
HY_EMB_PAD = 40
HY_FFN = 64
ROPE_BASE = 10000.0
LN_EPS = 1e-5
RMS_EPS = 1e-6
DN_ALPHA = (2 * DEPTH) ** 0.25
NEG_INF = -1e30

LANES = 128
TM = 512
TM_POST = 512
FF_CHUNK = 512
MLP_CHUNK = 1024
N_FF_CHUNKS = D_FF // FF_CHUNK
SUB_POST = 256
N_CTX_TILES = T_CTX // TM
N_TILES = T // TM
TQ = 512
MOD_ROWS = 16
VMEM_LIMIT = 56 * 1024 * 1024


def _cparams(n_axes, flags=None):
    return pltpu.CompilerParams(dimension_semantics=("arbitrary",) * n_axes,
                                vmem_limit_bytes=VMEM_LIMIT, flags=flags)


def _dot(a, b):
    return jnp.dot(a, b, preferred_element_type=F32)


def _dot_nt(a, b):
    return lax.dot_general(a, b, (((1,), (1,)), ((), ())), preferred_element_type=F32)


def _const_spec(shape):
    nd = len(shape)
    return pl.BlockSpec(shape, lambda *_: (0,) * nd, pipeline_mode=pl.Buffered(1))


def _mod_spec(layer, tm=TM, off=0):
    nctx = T_CTX // tm

    def row(i):
        t = jnp.maximum(i - off, 0)
        return jnp.where(t < nctx, 0, 1 + (t - nctx) // (DEC_SEQ // tm))

    return pl.BlockSpec((None, None, 6, D), lambda i: (layer, row(i), 0, 0))


def _tok_spec(width, tm=TM, off=0):
    return pl.BlockSpec((tm, width), lambda i: (jnp.maximum(i - off, 0), 0))


def _ctx_spec(width, tm=TM, off=0):
    return pl.BlockSpec((tm, width), lambda i: (jnp.clip(i - off, 0, T_CTX // tm - 1), 0))


def _lat_spec(width, tm=TM, off=0):
    return pl.BlockSpec((tm, width), lambda i: (jnp.maximum(i - off - T_CTX // tm, 0), 0))


def _is_lat(tm=TM, off=0):
    return pl.program_id(0) >= off + T_CTX // tm


def _pick(ctx_ref, lat_ref):
    return jnp.where(_is_lat(), lat_ref[...], ctx_ref[...])


def _layer_norm(r, g, b):
    mu = jnp.mean(r, axis=-1, keepdims=True)
    c = r - mu
    var = jnp.mean(c * c, axis=-1, keepdims=True)
    return c * lax.rsqrt(var + LN_EPS) * g + b


def _mods_kernel(c_ref, w_ref, b_ref, o_ref):
    c = c_ref[...]
    s = (c / (1.0 + jnp.exp(-c))).astype(BF)
    o_ref[...] = _dot(s, w_ref[...].astype(BF)) + b_ref[...]


def _mods(cvec, ada_w, ada_b):
    tn = 1536
    out = pl.pallas_call(
        _mods_kernel,
        grid=(DEPTH, 6 * D // tn),
        in_specs=[pl.BlockSpec((MOD_ROWS, D), lambda l, n: (0, 0)),
                  pl.BlockSpec((None, D, tn), lambda l, n: (l, 0, n)),
                  pl.BlockSpec((None, 1, tn), lambda l, n: (l, 0, n))],
        out_specs=pl.BlockSpec((None, MOD_ROWS, tn), lambda l, n: (l, 0, n)),
        out_shape=jax.ShapeDtypeStruct((DEPTH, MOD_ROWS, 6 * D), F32),
        compiler_params=_cparams(2),
        name="adaln_mods",
    )(cvec, ada_w, ada_b.reshape(DEPTH, 1, 6 * D))
    return out.reshape(DEPTH, MOD_ROWS, 6, D)


def _modulate(x, mod_ref, shift, scale):
    return (x * (1.0 + mod_ref[scale:scale + 1, :]) + mod_ref[shift:shift + 1, :]).astype(BF)


def _rope(x, a, b):
    n = x.shape[1]
    lane = lax.broadcasted_iota(jnp.int32, x.shape, 1)
    partner = jnp.where((lane & 16) == 0, pltpu.roll(x, n - 16, 1), pltpu.roll(x, 16, 1))
    return x * a + partner * b


def _rope_spec(width):
    per = DEC_SEQ // TM
    return pl.BlockSpec((TM, width), lambda i: (jnp.where(i < N_CTX_TILES, per, (i - N_CTX_TILES) % per), 0))


def _lockstep(gens):
    gens = list(gens)
    while gens:
        gens = [g for g in gens if next(g, True) is None]


def _store_state(k, v, ks_ref, vs_ref, transposed):
    @pl.when(jnp.logical_not(_is_lat()))
    def _():
        if not transposed:
            ks_ref[...] = k
            vs_ref[...] = v
        else:
            n = k.shape[1]
            for x, ref in ((k, ks_ref), (v, vs_ref)):
                xt = x.T
                for j in range(TM // SEQ):
                    ref[j * n:(j + 1) * n, :] = xt[:, j * SEQ:(j + 1) * SEQ]


def _qkv_out(nq, nk, transposed_state):
    specs = [_tok_spec(nq), _tok_spec(nk), _tok_spec(nk)]
    shapes = [jax.ShapeDtypeStruct((T, nq), BF), jax.ShapeDtypeStruct((T, nk), BF), jax.ShapeDtypeStruct((T, nk), BF)]
    if transposed_state:
        rows = (TM // SEQ) * nk
        specs += [pl.BlockSpec((rows, SEQ), lambda i: (jnp.minimum(i, N_CTX_TILES - 1), 0))] * 2
        shapes += [jax.ShapeDtypeStruct((BATCH * nk, SEQ), F32)] * 2
    else:
        specs += [_ctx_spec(nk)] * 2
        shapes += [jax.ShapeDtypeStruct((T_CTX, nk), F32)] * 2
    return specs, shapes


def _features_major(cache):
    b, _, past, heads, dh = cache.shape
    return cache.transpose(0, 1, 3, 4, 2).reshape(b, heads * dh, past)


def _untranspose_state(st, heads):
    return st.reshape(BATCH, heads, HEAD_DIM, SEQ).transpose(0, 3, 1, 2)[:, None]


def _da_proj_kernel(xc_ref, xl_ref, mod_ref, w_ref, ra_ref, rb_ref, qb_ref, kb_ref, vb_ref, ks_ref, vs_ref):
    h = _modulate(_pick(xc_ref, xl_ref), mod_ref, 0, 1)
    a, b = ra_ref[...], rb_ref[...]
    q = _dot(h, w_ref[:, 0:D])
    k = _dot(h, w_ref[:, D:2 * D])
    qb_ref[...] = (_rope(q, a, b) * Q_SCALE).astype(BF)
    v = _dot(h, w_ref[:, 2 * D:3 * D])
    kb_ref[...] = _rope(k, a, b).astype(BF)
    vb_ref[...] = v.astype(BF)
    _store_state(k, v, ks_ref, vs_ref, False)


def _da_proj(x_ctx, x_lat, mods, layer, w, rope_a, rope_b):
    specs, shapes = _qkv_out(D, D, False)
    return pl.pallas_call(
        _da_proj_kernel,
        grid=(N_TILES,),
        in_specs=[_ctx_spec(D), _lat_spec(D), _mod_spec(layer), _const_spec((D, 3 * D)),
                  _rope_spec(D), _rope_spec(D)],
        out_specs=specs, out_shape=shapes,
        compiler_params=_cparams(1),
        name=f"da_proj_l{layer}",
    )(x_ctx, x_lat, mods, w, rope_a, rope_b)


def _na_proj_kernel(x_ref, mod_ref, w_ref, qb_ref, kb_ref, vb_ref, ks_ref, vs_ref):
    h = _modulate(x_ref[...], mod_ref, 0, 1)
    q = _dot(h, w_ref[:, 0:D])
    k = _dot(h, w_ref[:, D:2 * D])
    qb_ref[...] = (q * Q_SCALE).astype(BF)
    v = _dot(h, w_ref[:, 2 * D:3 * D])
    kb_ref[...] = k.astype(BF)
    vb_ref[...] = v.astype(BF)
    _store_state(k, v, ks_ref, vs_ref, True)


def _na_proj(x, mods, layer, w):
    specs, shapes = _qkv_out(D, D, True)
    return pl.pallas_call(
        _na_proj_kernel,
        grid=(N_TILES,),
        in_specs=[_tok_spec(D), _mod_spec(layer), _const_spec((D, 3 * D))],
        out_specs=specs, out_shape=shapes,
        compiler_params=_cparams(1),
        name=f"na_proj_l{layer}",
    )(x, mods, w)


GN_BLOCK = 256


def _head_rms(x, g_ref, gain):
    x2 = x * x
    hi = x2.astype(BF)
    lo = (x2 - hi.astype(F32)).astype(BF)
    g = g_ref[...]
    ms = jnp.concatenate(
        [_dot(hi[:, j:j + GN_BLOCK], g) + _dot(lo[:, j:j + GN_BLOCK], g) for j in range(0, x.shape[1], GN_BLOCK)],
        axis=1)
    return x * lax.rsqrt(ms + RMS_EPS) * gain


def _gq_proj_kernel(x_ref, mod_ref, w_ref, g_ref, qn_ref, kn_ref, ra_ref, rb_ref,
                    qb_ref, kb_ref, vb_ref, ks_ref, vs_ref):
    nq, nk = GQ_HEADS * HEAD_DIM, GQ_KV_HEADS * HEAD_DIM
    h = _modulate(x_ref[...], mod_ref, 0, 1)
    a, b = ra_ref[...], rb_ref[...]
    q = _dot(h, w_ref[:, 0:nq])
    k = _dot(h, w_ref[:, nq:nq + nk])
    v = _dot(h, w_ref[:, nq + nk:nq + 2 * nk])
    k = _head_rms(k, g_ref, kn_ref[...])
    q = _head_rms(q, g_ref, qn_ref[...])
    kb_ref[...] = _rope(k, a[:, 0:nk], b[:, 0:nk]).astype(BF)
    qb_ref[...] = (_rope(q, a, b) * Q_SCALE).astype(BF)
    vb_ref[...] = v.astype(BF)
    _store_state(k, v, ks_ref, vs_ref, True)


def _gq_proj(x, mods, layer, w, g_mat, qn, kn, rope_a, rope_b):
    nq, nk = GQ_HEADS * HEAD_DIM, GQ_KV_HEADS * HEAD_DIM
    specs, shapes = _qkv_out(nq, nk, True)
    return pl.pallas_call(
        _gq_proj_kernel,
        grid=(N_TILES,),
        in_specs=[_tok_spec(D), _mod_spec(layer), _const_spec((D, nq + 2 * nk)),
                  _const_spec((GN_BLOCK, GN_BLOCK)), _const_spec((1, nq)), _const_spec((1, nk)),
                  _rope_spec(D), _rope_spec(D)],
        out_specs=specs, out_shape=shapes,
        compiler_params=_cparams(1),
        name=f"gq_proj_l{layer}",
    )(x, mods, w, g_mat, qn, kn, rope_a, rope_b)


def _hy_proj_kernel(x_ref, mod_ref, w_ref, u_ref):
    h = _modulate(x_ref[...], mod_ref, 0, 1)
    for c in range(HY_ORDER + 1):
        u_ref[:, c * D:(c + 1) * D] = _dot(h, w_ref[:, c * D:(c + 1) * D])


def _hy_proj(x, mods, layer, w):
    n = (HY_ORDER + 1) * D
    return pl.pallas_call(
        _hy_proj_kernel,
        grid=(N_TILES,),
        in_specs=[_tok_spec(D), _mod_spec(layer), _const_spec((D, n))],
        out_specs=_tok_spec(n),
        out_shape=jax.ShapeDtypeStruct((T, n), F32),
        compiler_params=_cparams(1),
        name=f"hy_proj_l{layer}",
    )(x, mods, w)


def _scores(qm, segs):
    return [_dot(qm, seg[0]) if len(seg) == 3 else _dot_nt(qm, seg[0]) for seg in segs]


def _softmax_finish(scores, segs):
    m = scores[0].max(axis=-1, keepdims=True)
    for s in scores[1:]:
        m = jnp.maximum(m, s.max(axis=-1, keepdims=True))
    den = None
    out = None
    for s, seg in zip(scores, segs):
        e = jnp.exp2(s - m)
        d = e.sum(axis=-1, keepdims=True)
        o = _dot_nt(e.astype(BF), seg[1]) if len(seg) == 3 else _dot(e.astype(BF), seg[1])
        den = d if den is None else den + d
        out = o if out is None else out + o
    return out / den


def _stack_halves(q, keep):
    return jnp.concatenate([q * keep[0], q * keep[1]], axis=0)


def _pipelined(jobs, score_fn, finish_fn):
    nxt = score_fn(jobs[0])
    for n, job in enumerate(jobs):
        cur, nxt = nxt, (score_fn(jobs[n + 1]) if n + 1 < len(jobs) else None)
        finish_fn(job, cur)


def _lane_half(shape):
    return lax.broadcasted_iota(jnp.int32, shape, 1) // HEAD_DIM


def _half_keep(half):
    return tuple(jnp.where(half == a, 1.0, 0.0).astype(BF) for a in (0, 1))


def _da_attn_kernel(*refs, has_cache, lam_init):
    if has_cache:
        q_ref, k_ref, v_ref, ck_ref, cv_ref, lam_ref, g_ref, o_ref = refs
    else:
        q_ref, k_ref, v_ref, lam_ref, g_ref, o_ref = refs
    lp = lam_ref[...]
    lam = (jnp.exp(jnp.sum(lp[0:1] * lp[1:2], axis=-1, keepdims=True))
           - jnp.exp(jnp.sum(lp[2:3] * lp[3:4], axis=-1, keepdims=True)) + lam_init)
    gain = g_ref[...] * (1.0 - lam_init)
    w = 2 * HEAD_DIM
    tq = min(TQ, q_ref.shape[0])
    keep = _half_keep(_lane_half((tq, w)))
    segs = []
    for hd in range(k_ref.shape[1] // w):
        cols = slice(hd * w, (hd + 1) * w)
        seg = [(k_ref[:, cols], v_ref[:, cols])]
        if has_cache:
            head = pl.program_id(1) * (k_ref.shape[1] // w) + hd
            seg.append((ck_ref[:, head, :].astype(BF), cv_ref[:, head, :].astype(BF)))
        segs.append(seg)
    jobs = [(hd, t, a) for hd in range(len(segs)) for t in range(q_ref.shape[0] // tq) for a in (0, 1)]
    first = {}

    def score_fn(job):
        hd, t, a = job
        return _scores(q_ref[t * tq:(t + 1) * tq, hd * w:(hd + 1) * w] * keep[a], segs[hd])

    def finish_fn(job, scores):
        hd, t, a = job
        o = _softmax_finish(scores, segs[hd])
        if a == 0:
            first[0] = o
            return
        o = first[0] - lam * o
        ms = jnp.mean(o * o, axis=-1, keepdims=True)
        o_ref[t * tq:(t + 1) * tq, hd * w:(hd + 1) * w] = (o * lax.rsqrt(ms + RMS_EPS) * gain).astype(BF)

    _pipelined(jobs, score_fn, finish_fn)


DA_LAT_HEADS_PER_STEP = 2


def _da_attention(qb, kb, vb, cache_k, cache_v, lam_p, subln_g, layer_idx):
    lam_init = 0.8 - 0.6 * math.exp(-0.3 * layer_idx)
    w = 2 * HEAD_DIM
    small = [pl.BlockSpec((4, HEAD_DIM), lambda *_: (0, 0)), pl.BlockSpec((1, w), lambda *_: (0, 0))]
    o_ctx = pl.pallas_call(
        functools.partial(_da_attn_kernel, has_cache=False, lam_init=lam_init),
        grid=(BATCH,),
        in_specs=[pl.BlockSpec((SEQ, D), lambda b: (b, 0))] * 3 + small,
        out_specs=pl.BlockSpec((SEQ, D), lambda b: (b, 0)),
        out_shape=jax.ShapeDtypeStruct((T_CTX, D), BF),
        compiler_params=_cparams(1),
        name="da_attn_ctx",
    )(qb, kb, vb, lam_p, subln_g)
    k0 = T_CTX // DEC_SEQ
    hw = DA_LAT_HEADS_PER_STEP * w
    tok = pl.BlockSpec((DEC_SEQ, hw), lambda b, h: (k0 + b, h))
    c_spec = pl.BlockSpec((None, None, PAST, DA_HEADS, w), lambda b, h: (b, 0, 0, 0, 0))
    o_lat = pl.pallas_call(
        functools.partial(_da_attn_kernel, has_cache=True, lam_init=lam_init),
        grid=(DEC_BATCH, DA_HEADS // DA_LAT_HEADS_PER_STEP),
        in_specs=[tok, tok, tok, c_spec, c_spec] + small,
        out_specs=pl.BlockSpec((DEC_SEQ, hw), lambda b, h: (b, h)),
        out_shape=jax.ShapeDtypeStruct((T_LAT, D), BF),
        compiler_params=_cparams(2),
        name="da_attn_lat",
    )(qb, kb, vb, cache_k, cache_v, lam_p, subln_g)
    return o_ctx, o_lat


def _na_ctx_kernel(q_ref, k_ref, v_ref, o_ref):
    half = _lane_half((SEQ, LANES))
    keep = _half_keep(half)

    def seg(p):
        return [(k_ref[:, p * LANES:(p + 1) * LANES], v_ref[:, p * LANES:(p + 1) * LANES])]

    def score_fn(p):
        return _scores(_stack_halves(q_ref[:, p * LANES:(p + 1) * LANES], keep), seg(p))

    def finish_fn(p, scores):
        o = _softmax_finish(scores, seg(p))
        o_ref[:, p * LANES:(p + 1) * LANES] = jnp.where(half == 0, o[0:SEQ], o[SEQ:2 * SEQ]).astype(BF)

    _pipelined(list(range(NA_HEADS // 2)), score_fn, finish_fn)


def _na_ctx_attention(qb, kb, vb):
    spec = pl.BlockSpec((SEQ, D), lambda b: (b, 0))
    return pl.pallas_call(
        _na_ctx_kernel,
        grid=(BATCH,),
        in_specs=[spec] * 3,
        out_specs=spec,
        out_shape=jax.ShapeDtypeStruct((T_CTX, D), BF),
        compiler_params=_cparams(1),
        name="na_attn_ctx",
    )(qb, kb, vb)


NA_TILES = ((0, (0, 2, 4, 6)), (4, (0, 2, 4, 6, 8, 10)), (8, (4, 6, 8, 10, 12, 14)), (12, (8, 10, 12, 14)))
NA_MAX_CHUNKS = 6
NA_BIAS_BLOCKS = 2 * NA_WIN_ROWS - 2


NA_LAT_PAIRS_PER_STEP = 2


def _na_lat_kernel(q_ref, k_ref, v_ref, ck_ref, cv_ref, w_ref, m_ref, o_ref):
    rows = 4 * GRID_W
    half = _lane_half((rows, LANES))
    keep = _half_keep(half)
    caches = [(ck_ref[p * LANES:(p + 1) * LANES, :].astype(BF), cv_ref[p * LANES:(p + 1) * LANES, :].astype(BF))
              for p in range(NA_LAT_PAIRS_PER_STEP)]
    jobs = [(p, i) for p in range(NA_LAT_PAIRS_PER_STEP) for i in range(len(NA_TILES))]

    def key_rows(i):
        chunks = NA_TILES[i][1]
        return slice(chunks[0] * GRID_W, chunks[0] * GRID_W + len(chunks) * LANES)

    def score_fn(job):
        p, i = job
        cols = slice(p * LANES, (p + 1) * LANES)
        r0, chunks = NA_TILES[i]
        qm = _stack_halves(q_ref[i * rows:(i + 1) * rows, cols], keep)
        mask = m_ref[i, :, 0:len(chunks) * LANES]
        bias = jnp.concatenate(
            [jnp.concatenate([w_ref[p, a, (6 - kr + r0) * GRID_W:(6 - kr + r0) * GRID_W + rows, :] for kr in chunks],
                             axis=1) + mask for a in (0, 1)], axis=0)
        return [_dot_nt(qm, k_ref[key_rows(i), cols]) + bias, _dot(qm, caches[p][0])]

    def finish_fn(job, scores):
        p, i = job
        cols = slice(p * LANES, (p + 1) * LANES)
        o = _softmax_finish(scores, [(None, v_ref[key_rows(i), cols]), (None, caches[p][1], True)])
        o_ref[i * rows:(i + 1) * rows, cols] = jnp.where(half == 0, o[0:rows], o[rows:2 * rows]).astype(BF)

    _pipelined(jobs, score_fn, finish_fn)


def _na_lat_attention(qb, kb, vb, cache_k, cache_v, bias_tab, mask_tab):
    k0 = T_CTX // DEC_SEQ
    npair = NA_LAT_PAIRS_PER_STEP
    tok = pl.BlockSpec((DEC_SEQ, npair * LANES), lambda b, p: (k0 + b, p))
    c_spec = pl.BlockSpec((None, npair * LANES, PAST), lambda b, p: (b, p, 0))
    return pl.pallas_call(
        _na_lat_kernel,
        grid=(DEC_BATCH, NA_HEADS // 2 // npair),
        in_specs=[tok, tok, tok, c_spec, c_spec,
                  pl.BlockSpec((npair, 2, NA_BIAS_BLOCKS * GRID_W, LANES), lambda b, p: (p, 0, 0, 0)),
                  _const_spec(mask_tab.shape)],
        out_specs=pl.BlockSpec((DEC_SEQ, npair * LANES), lambda b, p: (b, p)),
        out_shape=jax.ShapeDtypeStruct((T_LAT, D), BF),
        compiler_params=_cparams(2),
        name="na_attn_lat",
    )(qb, kb, vb, cache_k, cache_v, bias_tab, mask_tab)


def _na_bias_kernel(t_ref, r_ref, n_ref, o_ref):
    t = t_ref[...]
    t1 = t.astype(BF)
    r1 = t - t1.astype(F32)
    t2 = r1.astype(BF)
    t3 = (r1 - t2.astype(F32)).astype(BF)
    r = r_ref[...]
    res = (_dot(t1, r) + _dot(t2, r) + _dot(t3, r) + n_ref[...]) * LOG2E
    for qc in range(GRID_W):
        o_ref[pl.ds(qc, t.shape[0], stride=GRID_W), :] = res[:, qc * LANES:(qc + 1) * LANES]


def _na_bias_table(rel_bias, onehot, neg):
    nrel = 2 * NA_WIN_COLS
    idx = 13 - np.arange(NA_BIAS_BLOCKS)[:, None] + np.arange(2)[None, :]
    t = jnp.pad(rel_bias[:, idx, :], ((0, 0), (0, 0), (0, 0), (0, 1)))
    t = t.reshape(NA_HEADS * NA_BIAS_BLOCKS, 2 * nrel)
    n = GRID_W * LANES
    out = pl.pallas_call(
        _na_bias_kernel,
        grid=(1,),
        in_specs=[pl.BlockSpec(t.shape, lambda j: (0, 0)),
                  pl.BlockSpec((2 * nrel, n), lambda j: (0, 0)),
                  pl.BlockSpec((1, n), lambda j: (0, 0))],
        out_specs=pl.BlockSpec((t.shape[0] * GRID_W, LANES), lambda j: (0, 0)),
        out_shape=jax.ShapeDtypeStruct((t.shape[0] * GRID_W, LANES), F32),
        compiler_params=_cparams(1),
        name="na_bias_table",
    )(t, onehot, neg)
    return out.reshape(NA_HEADS // 2, 2, NA_BIAS_BLOCKS * GRID_W, LANES)


def _na_constants():
    nrel = 2 * NA_WIN_COLS
    qc = np.arange(GRID_W)[:, None]
    kc = np.arange(GRID_W)[None, :]
    rel = np.clip(kc - qc, -(NA_WIN_COLS - 1), NA_WIN_COLS - 1) + NA_WIN_COLS - 1
    cs = np.clip(qc - NA_WIN_COLS // 2, 0, GRID_W - NA_WIN_COLS)
    col_in = (kc >= cs) & (kc < cs + NA_WIN_COLS)
    onehot = np.zeros((2, nrel, GRID_W, 2, GRID_W), np.float32)
    for hf in range(2):
        onehot[hf, rel, qc, hf, kc] = 1.0
    neg = np.where(col_in, 0.0, NEG_INF).astype(np.float32)
    neg = np.broadcast_to(neg[:, None, :], (GRID_W, 2, GRID_W)).reshape(1, -1)
    rows = 4 * GRID_W
    mask = np.full((len(NA_TILES), rows, NA_MAX_CHUNKS * LANES), NEG_INF, np.float32)
    kr = min(NA_WIN_ROWS, GRID_ROWS)
    for i, (r0, chunks) in enumerate(NA_TILES):
        qr = r0 + np.arange(rows)[:, None] // GRID_W
        rs = np.clip(qr - kr // 2, 0, GRID_ROWS - kr)
        for c, krow0 in enumerate(chunks):
            krow = krow0 + np.arange(LANES)[None, :] // GRID_W
            mask[i, :, c * LANES:(c + 1) * LANES] = np.where((krow >= rs) & (krow < rs + kr), 0.0, NEG_INF)
    return (jnp.asarray(onehot.reshape(2 * nrel, GRID_W * LANES), BF), jnp.asarray(neg), jnp.asarray(mask))


def _gq_attn_kernel(*refs, has_cache):
    if has_cache:
        q_ref, k_ref, v_ref, ck_ref, cv_ref, o_ref = refs
    else:
        q_ref, k_ref, v_ref, o_ref = refs
    group = GQ_HEADS // GQ_KV_HEADS
    qw = LANES * group
    tq = min(TQ // 2, q_ref.shape[0])
    half = _lane_half((tq, LANES))
    keep = _half_keep(half)
    segs = []
    for kvp in range(k_ref.shape[1] // LANES):
        kcols = slice(kvp * LANES, (kvp + 1) * LANES)
        seg = [(k_ref[:, kcols], v_ref[:, kcols])]
        if has_cache:
            seg.append((ck_ref[kcols, :].astype(BF), cv_ref[kcols, :].astype(BF), True))
        segs.append(seg)
    jobs = [(kvp, t, kh) for kvp in range(len(segs)) for t in range(q_ref.shape[0] // tq) for kh in (0, 1)]

    def blocks(job):
        kvp, t, kh = job
        for pair in (2 * kh, 2 * kh + 1):
            yield slice(t * tq, (t + 1) * tq), slice(kvp * qw + pair * LANES, kvp * qw + (pair + 1) * LANES)

    def score_fn(job):
        kh = job[2]
        parts = []
        for rows, cols in blocks(job):
            for a in (0, 1):
                qm = q_ref[rows, cols] * keep[a]
                parts.append(qm if a == kh else pltpu.roll(qm.astype(F32), HEAD_DIM, 1).astype(BF))
        return _scores(jnp.concatenate(parts, axis=0), segs[job[0]])

    def finish_fn(job, scores):
        kh = job[2]
        o = _softmax_finish(scores, segs[job[0]])
        for n, (rows, cols) in enumerate(blocks(job)):
            heads = [o[(2 * n + a) * tq:(2 * n + a + 1) * tq] for a in (0, 1)]
            heads = [h if a == kh else pltpu.roll(h, HEAD_DIM, 1) for a, h in enumerate(heads)]
            o_ref[rows, cols] = jnp.where(half == 0, heads[0], heads[1]).astype(BF)

    _pipelined(jobs, score_fn, finish_fn)


def _gq_attention(qb, kb, vb, cache_k, cache_v):
    nk = GQ_KV_HEADS * HEAD_DIM
    qw = LANES * (GQ_HEADS // GQ_KV_HEADS)
    npair = GQ_KV_HEADS // 2
    o_ctx = pl.pallas_call(
        functools.partial(_gq_attn_kernel, has_cache=False),
        grid=(BATCH,),
        in_specs=[pl.BlockSpec((SEQ, D), lambda b: (b, 0))] + [pl.BlockSpec((SEQ, nk), lambda b: (b, 0))] * 2,
        out_specs=pl.BlockSpec((SEQ, D), lambda b: (b, 0)),
        out_shape=jax.ShapeDtypeStruct((T_CTX, D), BF),
        compiler_params=_cparams(1),
        name="gq_attn_ctx",
    )(qb, kb, vb)
    qt = DEC_SEQ // TQ
    q0, k0 = T_CTX // TQ, T_CTX // DEC_SEQ
    kv_spec = pl.BlockSpec((DEC_SEQ, LANES), lambda b, p, t: (k0 + b, p))
    c_spec = pl.BlockSpec((None, LANES, PAST), lambda b, p, t: (b, p, 0))
    o_lat = pl.pallas_call(
        functools.partial(_gq_attn_kernel, has_cache=True),
        grid=(DEC_BATCH, npair, qt),
        in_specs=[pl.BlockSpec((TQ, qw), lambda b, p, t: (q0 + b * qt + t, p)), kv_spec, kv_spec, c_spec, c_spec],
        out_specs=pl.BlockSpec((TQ, qw), lambda b, p, t: (b * qt + t, p)),
        out_shape=jax.ShapeDtypeStruct((T_LAT, D), BF),
        compiler_params=_cparams(3),
        name="gq_attn_lat",
    )(qb, kb, vb, cache_k, cache_v)
    return o_ctx, o_lat


def _dot_3pass(a, b):
    ah, bh = a.astype(BF), b.astype(BF)
    al, bl = (a - ah.astype(F32)).astype(BF), (b - bh.astype(F32)).astype(BF)
    return _dot(ah, bh) + _dot(ah, bl) + _dot(al, bh)


def _hy_filter_kernel(emb_ref, w1_ref, b1_ref, w2_ref, b2_ref, fr_ref, w3f_ref, w3b_ref, ldf_ref, ldb_ref,
                      c_ref, s_ref, hre_ref, him_ref, hny_ref, hid_ref, cb_ref, sb_ref):
    seq = emb_ref.shape[0]

    @pl.when((pl.program_id(0) == 0) & (pl.program_id(1) == 0))
    def _():
        hp = lax.Precision.HIGHEST
        fr = fr_ref[...]
        hid = jnp.sin(fr * (jnp.dot(emb_ref[...], w1_ref[...], precision=hp, preferred_element_type=F32)
                            + b1_ref[...]))
        hid_ref[...] = jnp.sin(fr * (jnp.dot(hid, w2_ref[...], precision=hp, preferred_element_type=F32)
                                     + b2_ref[...]))
        cb_ref[...] = c_ref[...].astype(BF)
        sb_ref[...] = s_ref[...].astype(BF)

    hid = hid_ref[...]
    t = emb_ref[:, 0:1]
    fwd = _dot_3pass(hid, w3f_ref[...]) * jnp.exp(-jnp.exp(ldf_ref[...]) * t)
    bwd = _dot_3pass(hid, w3b_ref[...]) * jnp.exp(-jnp.exp(ldb_ref[...]) * t)
    row = lax.broadcasted_iota(jnp.int32, fwd.shape, 0)
    bwd = jnp.where(row == 0, 0.0, bwd)
    even = fwd + bwd
    odd = bwd - fwd
    wk = jnp.where(row == 0, 0.5 / seq, 1.0 / seq)
    hre_ref[...] = _dot(cb_ref[...], even.astype(BF)) * wk
    him_ref[...] = _dot(sb_ref[...], odd.astype(BF)) * wk
    alt = jnp.where((row & 1) == 0, 1.0, -1.0)
    hny_ref[...] = jnp.sum(alt * even, axis=0, keepdims=True) * (0.5 / seq)


def _hy_filter(seq, emb, w1, b1, w2, b2, freq, w3, log_decay, cmat, smat):
    dc = 512
    nj = D // dc
    small = [_const_spec(a.shape) for a in (emb, w1, b1, w2, b2, freq)]
    return pl.pallas_call(
        _hy_filter_kernel,
        grid=(HY_ORDER, nj),
        in_specs=small + [pl.BlockSpec((HY_FFN, dc), lambda o, j: (0, (2 * o) * nj + j)),
                          pl.BlockSpec((HY_FFN, dc), lambda o, j: (0, (2 * o + 1) * nj + j)),
                          pl.BlockSpec((1, dc), lambda o, j: (0, (2 * o) * nj + j)),
                          pl.BlockSpec((1, dc), lambda o, j: (0, (2 * o + 1) * nj + j)),
                          _const_spec((seq, seq)), _const_spec((seq, seq))],
        out_specs=[pl.BlockSpec((None, seq, dc), lambda o, j: (o, 0, j)),
                   pl.BlockSpec((None, seq, dc), lambda o, j: (o, 0, j)),
                   pl.BlockSpec((None, 1, dc), lambda o, j: (o, 0, j))],
        out_shape=[jax.ShapeDtypeStruct((HY_ORDER, seq, D), F32), jax.ShapeDtypeStruct((HY_ORDER, seq, D), F32),
                   jax.ShapeDtypeStruct((HY_ORDER, 1, D), F32)],
        scratch_shapes=[pltpu.VMEM((seq, HY_FFN), F32), pltpu.VMEM((seq, seq), BF), pltpu.VMEM((seq, seq), BF)],
        compiler_params=_cparams(2),
        name=f"hy_filter_{seq}",
    )(emb, w1, b1, w2, b2, freq, w3, w3, log_decay, log_decay, cmat, smat)


HY_SUB = 256


def _hy_conv_kernel(u0_ref, u1_ref, u2_ref, sw0_ref, sw1_ref, sw2_ref, sb0_ref, sb1_ref, sb2_ref,
                    fb_ref, hre_ref, him_ref, hny_ref, c_ref, s_ref, o_ref, cb_ref, sb_ref):
    seq, dc = u0_ref.shape

    @pl.when((pl.program_id(0) == 0) & (pl.program_id(1) == 0))
    def _():
        cb_ref[...] = c_ref[...].astype(BF)
        sb_ref[...] = s_ref[...].astype(BF)

    row = lax.broadcasted_iota(jnp.int32, (seq, HY_SUB), 0)
    alt = jnp.where((row & 1) == 0, 1.0, -1.0)

    def sub_tile(cols):
        def short_conv(u_ref, w_ref, b_ref):
            u = u_ref[:, cols]
            prev = jnp.where(row == 0, 0.0, pltpu.roll(u, 1, 0))
            nxt = jnp.where(row == seq - 1, 0.0, pltpu.roll(u, seq - 1, 0))
            return prev * w_ref[0:1, cols] + u * w_ref[1:2, cols] + nxt * w_ref[2:3, cols] + b_ref[:, cols]

        z = short_conv(u0_ref, sw0_ref, sb0_ref)
        gates = (short_conv(u1_ref, sw1_ref, sb1_ref), short_conv(u2_ref, sw2_ref, sb2_ref))
        yield
        for o in range(HY_ORDER):
            zb = z.astype(BF)
            zc, zs = _dot(cb_ref[...], zb), _dot(sb_ref[...], zb)
            yield
            hre, him = hre_ref[o, :, cols], him_ref[o, :, cols]
            p_re = (zc * hre + zs * him).astype(BF)
            p_im = (zc * him - zs * hre).astype(BF)
            y = _dot(cb_ref[...], p_re) - _dot(sb_ref[...], p_im)
            yield
            nyq = jnp.sum(alt * z, axis=0, keepdims=True) * hny_ref[o, :, cols]
            z = gates[o] * (y + alt * nyq + z * fb_ref[o:o + 1, cols])
        o_ref[:, cols] = z.astype(BF)

    _lockstep(sub_tile(slice(j * HY_SUB, (j + 1) * HY_SUB)) for j in range(dc // HY_SUB))


def _hy_conv(u, short_w, short_b, filter_bias, hre, him, hny, cmat, smat, seq, nbatch, row0, dc):
    nj = D // dc
    r0 = row0 // seq

    def part(p):
        return pl.BlockSpec((seq, dc), lambda j, b: (r0 + b, p * nj + j))

    def vec(rows, p):
        return pl.BlockSpec((rows, dc), lambda j, b: (0, p * nj + j))

    in_specs = ([part(p) for p in range(3)] + [vec(3, p) for p in range(3)] + [vec(1, p) for p in range(3)]
                + [pl.BlockSpec((HY_ORDER, dc), lambda j, b: (0, j)),
                   pl.BlockSpec((HY_ORDER, seq, dc), lambda j, b: (0, 0, j), pipeline_mode=pl.Buffered(1)),
                   pl.BlockSpec((HY_ORDER, seq, dc), lambda j, b: (0, 0, j), pipeline_mode=pl.Buffered(1)),
                   pl.BlockSpec((HY_ORDER, 1, dc), lambda j, b: (0, 0, j)),
                   _const_spec((seq, seq)), _const_spec((seq, seq))])
    return pl.pallas_call(
        _hy_conv_kernel,
        grid=(nj, nbatch),
        in_specs=in_specs,
        out_specs=pl.BlockSpec((seq, dc), lambda j, b: (b, j)),
        out_shape=jax.ShapeDtypeStruct((nbatch * seq, D), BF),
        scratch_shapes=[pltpu.VMEM((seq, seq), BF), pltpu.VMEM((seq, seq), BF)],
        compiler_params=_cparams(2),
        name=f"hy_conv_{seq}",
    )(u, u, u, short_w, short_w, short_w, short_b, short_b, short_b, filter_bias, hre, him, hny, cmat, smat)


def _dft_tables(seq):
    k = np.arange(seq, dtype=np.int64)
    ang = np.pi * ((k[:, None] * k[None, :]) % (2 * seq)) / seq
    return jnp.asarray(np.cos(ang), F32), jnp.asarray(np.sin(ang), F32)


def _hy_embedding(seq):
    t = np.arange(seq, dtype=np.float32) / np.float32(seq)
    ang = (2.0 * math.pi) * t[:, None] * np.arange(1, HY_BANDS + 1, dtype=np.float32)
    emb = np.concatenate([t[:, None], np.cos(ang), np.sin(ang)], axis=-1).astype(np.float32)
    return jnp.asarray(np.pad(emb, ((0, 0), (0, HY_EMB_PAD - HY_EMB))))


def _post_kernel(*refs, split_x, split_out, tm):
    oc_ref, ol_ref = refs[0:2]
    x_refs, refs = (refs[2:4], refs[4:]) if split_x else (refs[2:3], refs[3:])
    mod_ref, wo_ref, g1_ref, b1_ref, w1c_ref, w2c_ref, g2_ref, b2_ref = refs[0:8]
    outs, (w1_ref, w2_ref) = refs[8:-2], refs[-2:]
    step = pl.program_id(0)
    is_lat = _is_lat(tm, N_FF_CHUNKS)
    nsub = tm // SUB_POST

    @pl.when(step < N_FF_CHUNKS)
    def _():
        per = MLP_CHUNK // FF_CHUNK
        w1_ref[step] = w1c_ref[...].astype(BF)
        w2_ref[step // per, pl.ds(pl.multiple_of((step % per) * FF_CHUNK, FF_CHUNK), FF_CHUNK), :] = (
            w2c_ref[...].astype(BF))

    def rows(j):
        return slice(j * SUB_POST, (j + 1) * SUB_POST)

    def pick(c_ref, l_ref, j):
        return jnp.where(is_lat, l_ref[rows(j), :], c_ref[rows(j), :])

    def norm1(j):
        a = _dot(pick(oc_ref, ol_ref, j), wo_ref[...])
        x = pick(x_refs[0], x_refs[1], j) if split_x else x_refs[0][rows(j), :]
        x1 = _layer_norm(DN_ALPHA * x + mod_ref[2:3, :] * a, g1_ref[...], b1_ref[...])
        return x1, _modulate(x1, mod_ref, 3, 4)

    def mlp_chunk(h, c):
        per = MLP_CHUNK // FF_CHUNK
        a = jnp.concatenate([_dot(h, w1_ref[per * c + i]) for i in range(per)], axis=1)
        a = jnp.maximum(a, 0.0)
        return _dot((a * a).astype(BF), w2_ref[c])

    def token_tile():
        ys = []

        def norm2_store(j, x1, acc):
            y = _layer_norm(DN_ALPHA * x1 + mod_ref[5:6, :] * acc, g2_ref[...], b2_ref[...])
            if split_out:
                ys.append(y)
            else:
                outs[0][rows(j), :] = y

        cur = norm1(0)
        prev = None
        for j in range(nsub):
            x1, h = cur
            acc = mlp_chunk(h, 0)
            if j + 1 < nsub:
                cur = norm1(j + 1)
            if prev is not None:
                norm2_store(j - 1, *prev)
            for c in range(1, D_FF // MLP_CHUNK):
                acc = acc + mlp_chunk(h, c)
            prev = (x1, acc)
        norm2_store(nsub - 1, *prev)
        if split_out:
            yc_ref, yl_ref = outs

            @pl.when(jnp.logical_not(is_lat))
            def _():
                for j, y in enumerate(ys):
                    yc_ref[rows(j), :] = y

            @pl.when(is_lat)
            def _():
                for j, y in enumerate(ys):
                    yl_ref[rows(j), :] = y

    pl.when(step >= N_FF_CHUNKS)(token_tile)


def _post(o_ctx, o_lat, xs, mods, layer, w_o, g1, b1, w1, w2, g2, b2, split_out):
    tm, off = TM_POST, N_FF_CHUNKS
    split_x = len(xs) == 2
    x_specs = [_ctx_spec(D, tm, off), _lat_spec(D, tm, off)] if split_x else [_tok_spec(D, tm, off)]
    vec = _const_spec((1, D))
    if split_out:
        out_specs = [_ctx_spec(D, tm, off), _lat_spec(D, tm, off)]
        out_shape = [jax.ShapeDtypeStruct((T_CTX, D), F32), jax.ShapeDtypeStruct((T_LAT, D), F32)]
    else:
        out_specs = _tok_spec(D, tm, off)
        out_shape = jax.ShapeDtypeStruct((T, D), F32)

    def chunk(i):
        return jnp.minimum(i, N_FF_CHUNKS - 1)

    return pl.pallas_call(
        functools.partial(_post_kernel, split_x=split_x, split_out=split_out, tm=tm),
        grid=(off + T // tm,),
        in_specs=[_ctx_spec(D, tm, off), _lat_spec(D, tm, off)] + x_specs + [
            _mod_spec(layer, tm, off), _const_spec((D, D)), vec, vec,
            pl.BlockSpec((None, D, FF_CHUNK), lambda i: (layer, 0, chunk(i))),
            pl.BlockSpec((None, FF_CHUNK, D), lambda i: (layer, chunk(i), 0)), vec, vec],
        out_specs=out_specs,
        out_shape=out_shape,
        scratch_shapes=[pltpu.VMEM((N_FF_CHUNKS, D, FF_CHUNK), BF),
                        pltpu.VMEM((D_FF // MLP_CHUNK, MLP_CHUNK, D), BF)],
        compiler_params=_cparams(1),
        name=f"post_l{layer}",
    )(o_ctx, o_lat, *xs, mods, w_o, g1, b1, w1, w2, g2, b2)


def _rope_tables():
    n = HEAD_DIM // 4
    pos = np.arange(DEC_SEQ)
    inv = (np.float32(ROPE_BASE) ** (-np.arange(n, dtype=np.float32) / np.float32(n))).astype(np.float32)
    ang_r = ((pos // GRID_W).astype(np.float32)[:, None] * inv).astype(np.float32)
    ang_c = ((pos % GRID_W).astype(np.float32)[:, None] * inv).astype(np.float32)
    cr, sr, cc, sc = np.cos(ang_r), np.sin(ang_r), np.cos(ang_c), np.sin(ang_c)
    a = np.tile(np.concatenate([cr, cr, cc, cc], axis=-1), (1, D // HEAD_DIM))
    b = np.tile(np.concatenate([-sr, sr, -sc, sc], axis=-1), (1, D // HEAD_DIM))
    a = np.concatenate([a, np.ones((TM, D), np.float32)], axis=0)
    b = np.concatenate([b, np.zeros((TM, D), np.float32)], axis=0)
    return jnp.asarray(a, F32), jnp.asarray(b, F32)


def kernel(x_prompt, x_sample, c, cache_da_k, cache_da_v, cache_na_k, cache_na_v, cache_gq_k, cache_gq_v, c_ctx, ada_w, ada_b, ln_g, ln_b, mlp_w1, mlp_w2, da_w_qkv, da_w_o, da_lambda, da_subln_g, na_w_qkv, na_w_o, na_rel_bias, gq_w_qkv, gq_w_o, gq_q_norm, gq_k_norm, hy_w_in, hy_short_w, hy_short_b, hy_ffn_w1, hy_ffn_b1, hy_ffn_w2, hy_ffn_b2, hy_ffn_freq, hy_ffn_w3, hy_log_decay, hy_filter_bias, hy_w_o):
    cvec = jnp.concatenate([c_ctx[None, :], c, jnp.zeros((MOD_ROWS - 1 - DEC_BATCH, D), F32)], axis=0)
    mods = _mods(cvec, ada_w, ada_b)
    rope_a, rope_b = _rope_tables()

    def finish(o_ctx, o_lat, xs, layer, w_o, split_out=False):
        return _post(o_ctx, o_lat, xs, mods, layer, w_o.astype(BF), ln_g[layer, 0][None], ln_b[layer, 0][None],
                     mlp_w1, mlp_w2, ln_g[layer, 1][None], ln_b[layer, 1][None], split_out)

    xs = (x_prompt.reshape(T_CTX, D), x_sample.reshape(T_LAT, D))
    qb, kb, vb, ks, vs = _da_proj(*xs, mods, 0, da_w_qkv[0].astype(BF), rope_a, rope_b)
    state_da_k = ks.reshape(BATCH, 1, SEQ, DA_HEADS, 2 * HEAD_DIM)
    state_da_v = vs.reshape(BATCH, 1, SEQ, DA_HEADS, 2 * HEAD_DIM)
    o_ctx, o_lat = _da_attention(qb, kb, vb, cache_da_k, cache_da_v, da_lambda[0], da_subln_g[0][None], 0)
    x = finish(o_ctx, o_lat, xs, 0, da_w_o[0])

    qb, kb, vb, ks, vs = _na_proj(x, mods, 1, na_w_qkv[0].astype(BF))
    state_na_k, state_na_v = _untranspose_state(ks, NA_HEADS), _untranspose_state(vs, NA_HEADS)
    onehot, neg, mask = _na_constants()
    bias_tab = _na_bias_table(na_rel_bias[0], onehot, neg)
    o_ctx = _na_ctx_attention(qb, kb, vb)
    o_lat = _na_lat_attention(qb, kb, vb, _features_major(cache_na_k), _features_major(cache_na_v), bias_tab, mask)
    x = finish(o_ctx, o_lat, (x,), 1, na_w_o[0])

    g_mat = jnp.asarray(np.kron(np.eye(GN_BLOCK // HEAD_DIM), np.full((HEAD_DIM, HEAD_DIM), 1.0 / HEAD_DIM)), BF)
    qb, kb, vb, ks, vs = _gq_proj(x, mods, 2, gq_w_qkv[0].astype(BF), g_mat,
                                  jnp.tile(gq_q_norm[0], GQ_HEADS)[None], jnp.tile(gq_k_norm[0], GQ_KV_HEADS)[None],
                                  rope_a, rope_b)
    state_gq_k, state_gq_v = _untranspose_state(ks, GQ_KV_HEADS), _untranspose_state(vs, GQ_KV_HEADS)
    o_ctx, o_lat = _gq_attention(qb, kb, vb, _features_major(cache_gq_k), _features_major(cache_gq_v))
    x = finish(o_ctx, o_lat, (x,), 2, gq_w_o[0])

    u = _hy_proj(x, mods, 3, hy_w_in[0].astype(BF))
    w1 = jnp.pad(hy_ffn_w1[0], ((0, HY_EMB_PAD - HY_EMB), (0, 0)))
    zs = []
    for seq, nbatch, row0, dc in ((SEQ, BATCH, 0, D), (DEC_SEQ, DEC_BATCH, T_CTX, 512)):
        cmat, smat = _dft_tables(seq)
        hre, him, hny = _hy_filter(seq, _hy_embedding(seq), w1, hy_ffn_b1[0][None], hy_ffn_w2[0], hy_ffn_b2[0][None],
                                   hy_ffn_freq[0][None], hy_ffn_w3[0], hy_log_decay[0][None], cmat, smat)
        zs.append(_hy_conv(u, hy_short_w[0], hy_short_b[0][None], hy_filter_bias[0], hre, him, hny, cmat, smat,
                           seq, nbatch, row0, dc))
    y_ctx, y_lat = finish(zs[0], zs[1], (x,), 3, hy_w_o[0], split_out=True)

    return (y_ctx.reshape(BATCH, SEQ, D), y_lat.reshape(DEC_BATCH, DEC_SEQ, D),
            state_da_k, state_da_v, state_na_k, state_na_v, state_gq_k, state_gq_v)
```

```python
import functools
import math

import numpy as np
import jax
import jax.numpy as jnp
from jax import lax
from jax.experimental import pallas as pl
from jax.experimental.pallas import tpu as pltpu

F32 = jnp.float32
BF = jnp.bfloat16

D = 1024
BATCH = 16
SEQ = 256
DEC_BATCH = 8
DEC_SEQ = 1024
PAST = 256
DEPTH = 4
GRID_W = 64
GRID_ROWS = DEC_SEQ // GRID_W
D_FF = 4 * D
T_CTX = BATCH * SEQ
T_LAT = DEC_BATCH * DEC_SEQ
T = T_CTX + T_LAT
HEAD_DIM = 64
ATT_SCALE = HEAD_DIM ** -0.5
LOG2E = math.log2(math.e)
Q_SCALE = ATT_SCALE * LOG2E
DA_HEADS = 8
NA_HEADS = 16
NA_WIN_ROWS = 8
NA_WIN_COLS = 16
GQ_HEADS = 16
GQ_KV_HEADS = 4
HY_ORDER = 2
HY_BANDS = 16
HY_EMB = 1 + 2 * HY_BANDS
HY_EMB_PAD = 40
HY_FFN = 64
ROPE_BASE = 10000.0
LN_EPS = 1e-5
RMS_EPS = 1e-6
DN_ALPHA = (2 * DEPTH) ** 0.25
NEG_INF = -1e30

LANES = 128
TM = 512
TM_POST = 512
FF_CHUNK = 512
MLP_CHUNK = 1024
N_FF_CHUNKS = D_FF // FF_CHUNK
SUB_POST = 256
N_CTX_TILES = T_CTX // TM
N_TILES = T // TM
TQ = 512
MOD_ROWS = 16
VMEM_LIMIT = 56 * 1024 * 1024
POST_VMEM_LIMIT = 58 * 1024 * 1024


def _cparams(n_axes, vmem_limit=VMEM_LIMIT):
    return pltpu.CompilerParams(dimension_semantics=("arbitrary",) * n_axes,
                                vmem_limit_bytes=vmem_limit)


def _dot(a, b):
    return jnp.dot(a, b, preferred_element_type=F32)


def _dot_nt(a, b):
    return lax.dot_general(a, b, (((1,), (1,)), ((), ())), preferred_element_type=F32)


def _const_spec(shape):
    nd = len(shape)
    return pl.BlockSpec(shape, lambda *_: (0,) * nd, pipeline_mode=pl.Buffered(1))


def _mod_spec(layer, tm=TM, off=0):
    nctx = T_CTX // tm

    def row(i):
        t = jnp.maximum(i - off, 0)
        return jnp.where(t < nctx, 0, 1 + (t - nctx) // (DEC_SEQ // tm))

    return pl.BlockSpec((None, None, 6, D), lambda i: (layer, row(i), 0, 0))


def _tok_spec(width, tm=TM, off=0):
    return pl.BlockSpec((tm, width), lambda i: (jnp.maximum(i - off, 0), 0))


def _ctx_spec(width, tm=TM, off=0):
    return pl.BlockSpec((tm, width), lambda i: (jnp.clip(i - off, 0, T_CTX // tm - 1), 0))


def _lat_spec(width, tm=TM, off=0):
    return pl.BlockSpec((tm, width), lambda i: (jnp.maximum(i - off - T_CTX // tm, 0), 0))


def _is_lat(tm=TM, off=0):
    return pl.program_id(0) >= off + T_CTX // tm


def _pick(ctx_ref, lat_ref):
    return jnp.where(_is_lat(), lat_ref[...], ctx_ref[...])


def _layer_norm(r, g, b):
    mu = jnp.mean(r, axis=-1, keepdims=True)
    c = r - mu
    var = jnp.mean(c * c, axis=-1, keepdims=True)
    return c * lax.rsqrt(var + LN_EPS) * g + b


def _mods_kernel(c_ref, w_ref, b_ref, o_ref):
    c = c_ref[...]
    s = (c / (1.0 + jnp.exp(-c))).astype(BF)
    o_ref[...] = _dot(s, w_ref[...].astype(BF)) + b_ref[...]


def _mods(cvec, ada_w, ada_b):
    tn = 1536
    out = pl.pallas_call(
        _mods_kernel,
        grid=(DEPTH, 6 * D // tn),
        in_specs=[pl.BlockSpec((MOD_ROWS, D), lambda l, n: (0, 0)),
                  pl.BlockSpec((None, D, tn), lambda l, n: (l, 0, n)),
                  pl.BlockSpec((None, 1, tn), lambda l, n: (l, 0, n))],
        out_specs=pl.BlockSpec((None, MOD_ROWS, tn), lambda l, n: (l, 0, n)),
        out_shape=jax.ShapeDtypeStruct((DEPTH, MOD_ROWS, 6 * D), F32),
        compiler_params=_cparams(2),
        name="adaln_mods",
    )(cvec, ada_w, ada_b.reshape(DEPTH, 1, 6 * D))
    return out.reshape(DEPTH, MOD_ROWS, 6, D)


def _modulate(x, mod_ref, shift, scale):
    return (x * (1.0 + mod_ref[scale:scale + 1, :]) + mod_ref[shift:shift + 1, :]).astype(BF)


def _rope(x, a, b):
    n = x.shape[1]
    lane = lax.broadcasted_iota(jnp.int32, x.shape, 1)
    partner = jnp.where((lane & 16) == 0, pltpu.roll(x, n - 16, 1), pltpu.roll(x, 16, 1))
    return x * a + partner * b


def _rope_spec(width):
    per = DEC_SEQ // TM
    return pl.BlockSpec((TM, width), lambda i: (jnp.where(i < N_CTX_TILES, per, (i - N_CTX_TILES) % per), 0))


def _lockstep(gens):
    gens = list(gens)
    while gens:
        gens = [g for g in gens if next(g, True) is None]


def _store_state(k, v, ks_ref, vs_ref, transposed):
    @pl.when(jnp.logical_not(_is_lat()))
    def _():
        if not transposed:
            ks_ref[...] = k
            vs_ref[...] = v
        else:
            n = k.shape[1]
            for x, ref in ((k, ks_ref), (v, vs_ref)):
                xt = x.T
                for j in range(TM // SEQ):
                    ref[j * n:(j + 1) * n, :] = xt[:, j * SEQ:(j + 1) * SEQ]


def _qkv_out(nq, nk, transposed_state):
    specs = [_tok_spec(nq), _tok_spec(nk), _tok_spec(nk)]
    shapes = [jax.ShapeDtypeStruct((T, nq), BF), jax.ShapeDtypeStruct((T, nk), BF), jax.ShapeDtypeStruct((T, nk), BF)]
    if transposed_state:
        rows = (TM // SEQ) * nk
        specs += [pl.BlockSpec((rows, SEQ), lambda i: (jnp.minimum(i, N_CTX_TILES - 1), 0))] * 2
        shapes += [jax.ShapeDtypeStruct((BATCH * nk, SEQ), F32)] * 2
    else:
        specs += [_ctx_spec(nk)] * 2
        shapes += [jax.ShapeDtypeStruct((T_CTX, nk), F32)] * 2
    return specs, shapes


def _features_major(cache):
    b, _, past, heads, dh = cache.shape
    return cache.transpose(0, 1, 3, 4, 2).reshape(b, heads * dh, past)


def _untranspose_state(st, heads):
    return st.reshape(BATCH, heads, HEAD_DIM, SEQ).transpose(0, 3, 1, 2)[:, None]


def _da_proj_kernel(xc_ref, xl_ref, mod_ref, w_ref, ra_ref, rb_ref, qb_ref, kb_ref, vb_ref, ks_ref, vs_ref):
    h = _modulate(_pick(xc_ref, xl_ref), mod_ref, 0, 1)
    a, b = ra_ref[...], rb_ref[...]
    q = _dot(h, w_ref[:, 0:D])
    k = _dot(h, w_ref[:, D:2 * D])
    qb_ref[...] = (_rope(q, a, b) * Q_SCALE).astype(BF)
    v = _dot(h, w_ref[:, 2 * D:3 * D])
    kb_ref[...] = _rope(k, a, b).astype(BF)
    vb_ref[...] = v.astype(BF)
    _store_state(k, v, ks_ref, vs_ref, False)


def _da_proj(x_ctx, x_lat, mods, layer, w, rope_a, rope_b):
    specs, shapes = _qkv_out(D, D, False)
    return pl.pallas_call(
        _da_proj_kernel,
        grid=(N_TILES,),
        in_specs=[_ctx_spec(D), _lat_spec(D), _mod_spec(layer), _const_spec((D, 3 * D)),
                  _rope_spec(D), _rope_spec(D)],
        out_specs=specs, out_shape=shapes,
        compiler_params=_cparams(1),
        name=f"da_proj_l{layer}",
    )(x_ctx, x_lat, mods, w, rope_a, rope_b)


def _na_proj_kernel(x_ref, mod_ref, w_ref, qb_ref, kb_ref, vb_ref, ks_ref, vs_ref):
    h = _modulate(x_ref[...], mod_ref, 0, 1)
    q = _dot(h, w_ref[:, 0:D])
    k = _dot(h, w_ref[:, D:2 * D])
    qb_ref[...] = (q * Q_SCALE).astype(BF)
    v = _dot(h, w_ref[:, 2 * D:3 * D])
    kb_ref[...] = k.astype(BF)
    vb_ref[...] = v.astype(BF)
    _store_state(k, v, ks_ref, vs_ref, True)


def _na_proj(x, mods, layer, w):
    specs, shapes = _qkv_out(D, D, True)
    return pl.pallas_call(
        _na_proj_kernel,
        grid=(N_TILES,),
        in_specs=[_tok_spec(D), _mod_spec(layer), _const_spec((D, 3 * D))],
        out_specs=specs, out_shape=shapes,
        compiler_params=_cparams(1),
        name=f"na_proj_l{layer}",
    )(x, mods, w)


GN_BLOCK = 256


def _head_rms(x, g_ref, gain):
    x2 = x * x
    hi = x2.astype(BF)
    lo = (x2 - hi.astype(F32)).astype(BF)
    g = g_ref[...]
    ms = jnp.concatenate(
        [_dot(hi[:, j:j + GN_BLOCK], g) + _dot(lo[:, j:j + GN_BLOCK], g) for j in range(0, x.shape[1], GN_BLOCK)],
        axis=1)
    return x * lax.rsqrt(ms + RMS_EPS) * gain


def _gq_proj_kernel(x_ref, mod_ref, w_ref, g_ref, qn_ref, kn_ref, ra_ref, rb_ref,
                    qb_ref, kb_ref, vb_ref, ks_ref, vs_ref):
    nq, nk = GQ_HEADS * HEAD_DIM, GQ_KV_HEADS * HEAD_DIM
    h = _modulate(x_ref[...], mod_ref, 0, 1)
    a, b = ra_ref[...], rb_ref[...]
    q = _dot(h, w_ref[:, 0:nq])
    k = _dot(h, w_ref[:, nq:nq + nk])
    v = _dot(h, w_ref[:, nq + nk:nq + 2 * nk])
    k = _head_rms(k, g_ref, kn_ref[...])
    q = _head_rms(q, g_ref, qn_ref[...])
    kb_ref[...] = _rope(k, a[:, 0:nk], b[:, 0:nk]).astype(BF)
    qb_ref[...] = (_rope(q, a, b) * Q_SCALE).astype(BF)
    vb_ref[...] = v.astype(BF)
    _store_state(k, v, ks_ref, vs_ref, True)


def _gq_proj(x, mods, layer, w, g_mat, qn, kn, rope_a, rope_b):
    nq, nk = GQ_HEADS * HEAD_DIM, GQ_KV_HEADS * HEAD_DIM
    specs, shapes = _qkv_out(nq, nk, True)
    return pl.pallas_call(
        _gq_proj_kernel,
        grid=(N_TILES,),
        in_specs=[_tok_spec(D), _mod_spec(layer), _const_spec((D, nq + 2 * nk)),
                  _const_spec((GN_BLOCK, GN_BLOCK)), _const_spec((1, nq)), _const_spec((1, nk)),
                  _rope_spec(D), _rope_spec(D)],
        out_specs=specs, out_shape=shapes,
        compiler_params=_cparams(1),
        name=f"gq_proj_l{layer}",
    )(x, mods, w, g_mat, qn, kn, rope_a, rope_b)


def _hy_proj_kernel(x_ref, mod_ref, w_ref, u_ref):
    h = _modulate(x_ref[...], mod_ref, 0, 1)
    for c in range(HY_ORDER + 1):
        u_ref[:, c * D:(c + 1) * D] = _dot(h, w_ref[:, c * D:(c + 1) * D])


def _hy_proj(x, mods, layer, w):
    n = (HY_ORDER + 1) * D
    return pl.pallas_call(
        _hy_proj_kernel,
        grid=(N_TILES,),
        in_specs=[_tok_spec(D), _mod_spec(layer), _const_spec((D, n))],
        out_specs=_tok_spec(n),
        out_shape=jax.ShapeDtypeStruct((T, n), F32),
        compiler_params=_cparams(1),
        name=f"hy_proj_l{layer}",
    )(x, mods, w)


def _scores(qm, segs):
    return [_dot(qm, seg[0]) if len(seg) == 3 else _dot_nt(qm, seg[0]) for seg in segs]


def _softmax_finish(scores, segs):
    m = scores[0].max(axis=-1, keepdims=True)
    for s in scores[1:]:
        m = jnp.maximum(m, s.max(axis=-1, keepdims=True))
    den = None
    out = None
    for s, seg in zip(scores, segs):
        e = jnp.exp2(s - m)
        d = e.sum(axis=-1, keepdims=True)
        o = _dot_nt(e.astype(BF), seg[1]) if len(seg) == 3 else _dot(e.astype(BF), seg[1])
        den = d if den is None else den + d
        out = o if out is None else out + o
    return out / den


def _stack_halves(q, keep):
    return jnp.concatenate([q * keep[0], q * keep[1]], axis=0)


def _pipelined(jobs, score_fn, finish_fn):
    nxt = score_fn(jobs[0])
    for n, job in enumerate(jobs):
        cur, nxt = nxt, (score_fn(jobs[n + 1]) if n + 1 < len(jobs) else None)
        finish_fn(job, cur)


def _lane_half(shape):
    return lax.broadcasted_iota(jnp.int32, shape, 1) // HEAD_DIM


def _half_keep(half):
    return tuple(jnp.where(half == a, 1.0, 0.0).astype(BF) for a in (0, 1))


def _da_attn_kernel(*refs, has_cache, lam_init):
    if has_cache:
        q_ref, k_ref, v_ref, ck_ref, cv_ref, lam_ref, g_ref, o_ref = refs
    else:
        q_ref, k_ref, v_ref, lam_ref, g_ref, o_ref = refs
    lp = lam_ref[...]
    lam = (jnp.exp(jnp.sum(lp[0:1] * lp[1:2], axis=-1, keepdims=True))
           - jnp.exp(jnp.sum(lp[2:3] * lp[3:4], axis=-1, keepdims=True)) + lam_init)
    gain = g_ref[...] * (1.0 - lam_init)
    w = 2 * HEAD_DIM
    tq = min(TQ, q_ref.shape[0])
    keep = _half_keep(_lane_half((tq, w)))
    segs = []
    for hd in range(k_ref.shape[1] // w):
        cols = slice(hd * w, (hd + 1) * w)
        seg = [(k_ref[:, cols], v_ref[:, cols])]
        if has_cache:
            head = pl.program_id(1) * (k_ref.shape[1] // w) + hd
            seg.append((ck_ref[:, head, :].astype(BF), cv_ref[:, head, :].astype(BF)))
        segs.append(seg)
    jobs = [(hd, t, a) for hd in range(len(segs)) for t in range(q_ref.shape[0] // tq) for a in (0, 1)]
    first = {}

    def score_fn(job):
        hd, t, a = job
        return _scores(q_ref[t * tq:(t + 1) * tq, hd * w:(hd + 1) * w] * keep[a], segs[hd])

    def finish_fn(job, scores):
        hd, t, a = job
        o = _softmax_finish(scores, segs[hd])
        if a == 0:
            first[0] = o
            return
        o = first[0] - lam * o
        ms = jnp.mean(o * o, axis=-1, keepdims=True)
        o_ref[t * tq:(t + 1) * tq, hd * w:(hd + 1) * w] = (o * lax.rsqrt(ms + RMS_EPS) * gain).astype(BF)

    _pipelined(jobs, score_fn, finish_fn)


DA_LAT_HEADS_PER_STEP = 2


def _da_attention(qb, kb, vb, cache_k, cache_v, lam_p, subln_g, layer_idx):
    lam_init = 0.8 - 0.6 * math.exp(-0.3 * layer_idx)
    w = 2 * HEAD_DIM
    small = [pl.BlockSpec((4, HEAD_DIM), lambda *_: (0, 0)), pl.BlockSpec((1, w), lambda *_: (0, 0))]
    o_ctx = pl.pallas_call(
        functools.partial(_da_attn_kernel, has_cache=False, lam_init=lam_init),
        grid=(BATCH,),
        in_specs=[pl.BlockSpec((SEQ, D), lambda b: (b, 0))] * 3 + small,
        out_specs=pl.BlockSpec((SEQ, D), lambda b: (b, 0)),
        out_shape=jax.ShapeDtypeStruct((T_CTX, D), BF),
        compiler_params=_cparams(1),
        name="da_attn_ctx",
    )(qb, kb, vb, lam_p, subln_g)
    k0 = T_CTX // DEC_SEQ
    hw = DA_LAT_HEADS_PER_STEP * w
    tok = pl.BlockSpec((DEC_SEQ, hw), lambda b, h: (k0 + b, h))
    c_spec = pl.BlockSpec((None, None, PAST, DA_HEADS, w), lambda b, h: (b, 0, 0, 0, 0))
    o_lat = pl.pallas_call(
        functools.partial(_da_attn_kernel, has_cache=True, lam_init=lam_init),
        grid=(DEC_BATCH, DA_HEADS // DA_LAT_HEADS_PER_STEP),
        in_specs=[tok, tok, tok, c_spec, c_spec] + small,
        out_specs=pl.BlockSpec((DEC_SEQ, hw), lambda b, h: (b, h)),
        out_shape=jax.ShapeDtypeStruct((T_LAT, D), BF),
        compiler_params=_cparams(2),
        name="da_attn_lat",
    )(qb, kb, vb, cache_k, cache_v, lam_p, subln_g)
    return o_ctx, o_lat


def _na_ctx_kernel(q_ref, k_ref, v_ref, o_ref):
    half = _lane_half((SEQ, LANES))
    keep = _half_keep(half)

    def seg(p):
        return [(k_ref[:, p * LANES:(p + 1) * LANES], v_ref[:, p * LANES:(p + 1) * LANES])]

    def score_fn(p):
        return _scores(_stack_halves(q_ref[:, p * LANES:(p + 1) * LANES], keep), seg(p))

    def finish_fn(p, scores):
        o = _softmax_finish(scores, seg(p))
        o_ref[:, p * LANES:(p + 1) * LANES] = jnp.where(half == 0, o[0:SEQ], o[SEQ:2 * SEQ]).astype(BF)

    _pipelined(list(range(NA_HEADS // 2)), score_fn, finish_fn)


def _na_ctx_attention(qb, kb, vb):
    spec = pl.BlockSpec((SEQ, D), lambda b: (b, 0))
    return pl.pallas_call(
        _na_ctx_kernel,
        grid=(BATCH,),
        in_specs=[spec] * 3,
        out_specs=spec,
        out_shape=jax.ShapeDtypeStruct((T_CTX, D), BF),
        compiler_params=_cparams(1),
        name="na_attn_ctx",
    )(qb, kb, vb)


NA_TILES = ((0, (0, 2, 4, 6)), (4, (0, 2, 4, 6, 8, 10)), (8, (4, 6, 8, 10, 12, 14)), (12, (8, 10, 12, 14)))
NA_MAX_CHUNKS = 6
NA_BIAS_BLOCKS = 2 * NA_WIN_ROWS - 2


NA_LAT_PAIRS_PER_STEP = 2


def _na_lat_kernel(q_ref, k_ref, v_ref, ck_ref, cv_ref, w_ref, m_ref, o_ref):
    rows = 4 * GRID_W
    half = _lane_half((rows, LANES))
    keep = _half_keep(half)
    caches = [(ck_ref[p * LANES:(p + 1) * LANES, :].astype(BF), cv_ref[p * LANES:(p + 1) * LANES, :].astype(BF))
              for p in range(NA_LAT_PAIRS_PER_STEP)]
    jobs = [(p, i) for p in range(NA_LAT_PAIRS_PER_STEP) for i in range(len(NA_TILES))]

    def key_rows(i):
        chunks = NA_TILES[i][1]
        return slice(chunks[0] * GRID_W, chunks[0] * GRID_W + len(chunks) * LANES)

    def score_fn(job):
        p, i = job
        cols = slice(p * LANES, (p + 1) * LANES)
        r0, chunks = NA_TILES[i]
        qm = _stack_halves(q_ref[i * rows:(i + 1) * rows, cols], keep)
        mask = m_ref[i, :, 0:len(chunks) * LANES]
        bias = jnp.concatenate(
            [jnp.concatenate([w_ref[p, a, (6 - kr + r0) * GRID_W:(6 - kr + r0) * GRID_W + rows, :] for kr in chunks],
                             axis=1) + mask for a in (0, 1)], axis=0)
        return [_dot_nt(qm, k_ref[key_rows(i), cols]) + bias, _dot(qm, caches[p][0])]

    def finish_fn(job, scores):
        p, i = job
        cols = slice(p * LANES, (p + 1) * LANES)
        o = _softmax_finish(scores, [(None, v_ref[key_rows(i), cols]), (None, caches[p][1], True)])
        o_ref[i * rows:(i + 1) * rows, cols] = jnp.where(half == 0, o[0:rows], o[rows:2 * rows]).astype(BF)

    _pipelined(jobs, score_fn, finish_fn)


def _na_lat_attention(qb, kb, vb, cache_k, cache_v, bias_tab, mask_tab):
    k0 = T_CTX // DEC_SEQ
    npair = NA_LAT_PAIRS_PER_STEP
    tok = pl.BlockSpec((DEC_SEQ, npair * LANES), lambda b, p: (k0 + b, p))
    c_spec = pl.BlockSpec((None, npair * LANES, PAST), lambda b, p: (b, p, 0))
    return pl.pallas_call(
        _na_lat_kernel,
        grid=(DEC_BATCH, NA_HEADS // 2 // npair),
        in_specs=[tok, tok, tok, c_spec, c_spec,
                  pl.BlockSpec((npair, 2, NA_BIAS_BLOCKS * GRID_W, LANES), lambda b, p: (p, 0, 0, 0)),
                  _const_spec(mask_tab.shape)],
        out_specs=pl.BlockSpec((DEC_SEQ, npair * LANES), lambda b, p: (b, p)),
        out_shape=jax.ShapeDtypeStruct((T_LAT, D), BF),
        compiler_params=_cparams(2),
        name="na_attn_lat",
    )(qb, kb, vb, cache_k, cache_v, bias_tab, mask_tab)


def _na_bias_kernel(t_ref, r_ref, n_ref, o_ref):
    t = t_ref[...]
    t1 = t.astype(BF)
    r1 = t - t1.astype(F32)
    t2 = r1.astype(BF)
    t3 = (r1 - t2.astype(F32)).astype(BF)
    r = r_ref[...]
    res = (_dot(t1, r) + _dot(t2, r) + _dot(t3, r) + n_ref[...]) * LOG2E
    for qc in range(GRID_W):
        o_ref[pl.ds(qc, t.shape[0], stride=GRID_W), :] = res[:, qc * LANES:(qc + 1) * LANES]


def _na_bias_table(rel_bias, onehot, neg):
    nrel = 2 * NA_WIN_COLS
    idx = 13 - np.arange(NA_BIAS_BLOCKS)[:, None] + np.arange(2)[None, :]
    t = jnp.pad(rel_bias[:, idx, :], ((0, 0), (0, 0), (0, 0), (0, 1)))
    t = t.reshape(NA_HEADS * NA_BIAS_BLOCKS, 2 * nrel)
    n = GRID_W * LANES
    out = pl.pallas_call(
        _na_bias_kernel,
        grid=(1,),
        in_specs=[pl.BlockSpec(t.shape, lambda j: (0, 0)),
                  pl.BlockSpec((2 * nrel, n), lambda j: (0, 0)),
                  pl.BlockSpec((1, n), lambda j: (0, 0))],
        out_specs=pl.BlockSpec((t.shape[0] * GRID_W, LANES), lambda j: (0, 0)),
        out_shape=jax.ShapeDtypeStruct((t.shape[0] * GRID_W, LANES), F32),
        compiler_params=_cparams(1),
        name="na_bias_table",
    )(t, onehot, neg)
    return out.reshape(NA_HEADS // 2, 2, NA_BIAS_BLOCKS * GRID_W, LANES)


def _na_constants():
    nrel = 2 * NA_WIN_COLS
    qc = np.arange(GRID_W)[:, None]
    kc = np.arange(GRID_W)[None, :]
    rel = np.clip(kc - qc, -(NA_WIN_COLS - 1), NA_WIN_COLS - 1) + NA_WIN_COLS - 1
    cs = np.clip(qc - NA_WIN_COLS // 2, 0, GRID_W - NA_WIN_COLS)
    col_in = (kc >= cs) & (kc < cs + NA_WIN_COLS)
    onehot = np.zeros((2, nrel, GRID_W, 2, GRID_W), np.float32)
    for hf in range(2):
        onehot[hf, rel, qc, hf, kc] = 1.0
    neg = np.where(col_in, 0.0, NEG_INF).astype(np.float32)
    neg = np.broadcast_to(neg[:, None, :], (GRID_W, 2, GRID_W)).reshape(1, -1)
    rows = 4 * GRID_W
    mask = np.full((len(NA_TILES), rows, NA_MAX_CHUNKS * LANES), NEG_INF, np.float32)
    kr = min(NA_WIN_ROWS, GRID_ROWS)
    for i, (r0, chunks) in enumerate(NA_TILES):
        qr = r0 + np.arange(rows)[:, None] // GRID_W
        rs = np.clip(qr - kr // 2, 0, GRID_ROWS - kr)
        for c, krow0 in enumerate(chunks):
            krow = krow0 + np.arange(LANES)[None, :] // GRID_W
            mask[i, :, c * LANES:(c + 1) * LANES] = np.where((krow >= rs) & (krow < rs + kr), 0.0, NEG_INF)
    return (jnp.asarray(onehot.reshape(2 * nrel, GRID_W * LANES), BF), jnp.asarray(neg), jnp.asarray(mask))


def _gq_attn_kernel(*refs, has_cache):
    if has_cache:
        q_ref, k_ref, v_ref, ck_ref, cv_ref, o_ref = refs
    else:
        q_ref, k_ref, v_ref, o_ref = refs
    group = GQ_HEADS // GQ_KV_HEADS
    qw = LANES * group
    tq = min(TQ // 2, q_ref.shape[0])
    half = _lane_half((tq, LANES))
    keep = _half_keep(half)
    segs = []
    for kvp in range(k_ref.shape[1] // LANES):
        kcols = slice(kvp * LANES, (kvp + 1) * LANES)
        seg = [(k_ref[:, kcols], v_ref[:, kcols])]
        if has_cache:
            seg.append((ck_ref[kcols, :].astype(BF), cv_ref[kcols, :].astype(BF), True))
        segs.append(seg)
    jobs = [(kvp, t, kh) for kvp in range(len(segs)) for t in range(q_ref.shape[0] // tq) for kh in (0, 1)]

    def blocks(job):
        kvp, t, kh = job
        for pair in (2 * kh, 2 * kh + 1):
            yield slice(t * tq, (t + 1) * tq), slice(kvp * qw + pair * LANES, kvp * qw + (pair + 1) * LANES)

    def score_fn(job):
        kh = job[2]
        parts = []
        for rows, cols in blocks(job):
            for a in (0, 1):
                qm = q_ref[rows, cols] * keep[a]
                parts.append(qm if a == kh else pltpu.roll(qm.astype(F32), HEAD_DIM, 1).astype(BF))
        return _scores(jnp.concatenate(parts, axis=0), segs[job[0]])

    def finish_fn(job, scores):
        kh = job[2]
        o = _softmax_finish(scores, segs[job[0]])
        for n, (rows, cols) in enumerate(blocks(job)):
            heads = [o[(2 * n + a) * tq:(2 * n + a + 1) * tq] for a in (0, 1)]
            heads = [h if a == kh else pltpu.roll(h, HEAD_DIM, 1) for a, h in enumerate(heads)]
            o_ref[rows, cols] = jnp.where(half == 0, heads[0], heads[1]).astype(BF)

    _pipelined(jobs, score_fn, finish_fn)


def _gq_attention(qb, kb, vb, cache_k, cache_v):
    nk = GQ_KV_HEADS * HEAD_DIM
    qw = LANES * (GQ_HEADS // GQ_KV_HEADS)
    npair = GQ_KV_HEADS // 2
    o_ctx = pl.pallas_call(
        functools.partial(_gq_attn_kernel, has_cache=False),
        grid=(BATCH,),
        in_specs=[pl.BlockSpec((SEQ, D), lambda b: (b, 0))] + [pl.BlockSpec((SEQ, nk), lambda b: (b, 0))] * 2,
        out_specs=pl.BlockSpec((SEQ, D), lambda b: (b, 0)),
        out_shape=jax.ShapeDtypeStruct((T_CTX, D), BF),
        compiler_params=_cparams(1),
        name="gq_attn_ctx",
    )(qb, kb, vb)
    qt = DEC_SEQ // TQ
    q0, k0 = T_CTX // TQ, T_CTX // DEC_SEQ
    kv_spec = pl.BlockSpec((DEC_SEQ, LANES), lambda b, p, t: (k0 + b, p))
    c_spec = pl.BlockSpec((None, LANES, PAST), lambda b, p, t: (b, p, 0))
    o_lat = pl.pallas_call(
        functools.partial(_gq_attn_kernel, has_cache=True),
        grid=(DEC_BATCH, npair, qt),
        in_specs=[pl.BlockSpec((TQ, qw), lambda b, p, t: (q0 + b * qt + t, p)), kv_spec, kv_spec, c_spec, c_spec],
        out_specs=pl.BlockSpec((TQ, qw), lambda b, p, t: (b * qt + t, p)),
        out_shape=jax.ShapeDtypeStruct((T_LAT, D), BF),
        compiler_params=_cparams(3),
        name="gq_attn_lat",
    )(qb, kb, vb, cache_k, cache_v)
    return o_ctx, o_lat


def _dot_3pass(a, b):
    ah, bh = a.astype(BF), b.astype(BF)
    al, bl = (a - ah.astype(F32)).astype(BF), (b - bh.astype(F32)).astype(BF)
    return _dot(ah, bh) + _dot(ah, bl) + _dot(al, bh)


def _hy_filter_kernel(emb_ref, w1_ref, b1_ref, w2_ref, b2_ref, fr_ref, w3f_ref, w3b_ref, ldf_ref, ldb_ref,
                      c_ref, s_ref, hre_ref, him_ref, hny_ref, hid_ref, cb_ref, sb_ref):
    seq = emb_ref.shape[0]

    @pl.when((pl.program_id(0) == 0) & (pl.program_id(1) == 0))
    def _():
        hp = lax.Precision.HIGHEST
        fr = fr_ref[...]
        hid = jnp.sin(fr * (jnp.dot(emb_ref[...], w1_ref[...], precision=hp, preferred_element_type=F32)
                            + b1_ref[...]))
        hid_ref[...] = jnp.sin(fr * (jnp.dot(hid, w2_ref[...], precision=hp, preferred_element_type=F32)
                                     + b2_ref[...]))
        cb_ref[...] = c_ref[...].astype(BF)
        sb_ref[...] = s_ref[...].astype(BF)

    hid = hid_ref[...]
    t = emb_ref[:, 0:1]
    fwd = _dot_3pass(hid, w3f_ref[...]) * jnp.exp(-jnp.exp(ldf_ref[...]) * t)
    bwd = _dot_3pass(hid, w3b_ref[...]) * jnp.exp(-jnp.exp(ldb_ref[...]) * t)
    row = lax.broadcasted_iota(jnp.int32, fwd.shape, 0)
    bwd = jnp.where(row == 0, 0.0, bwd)
    even = fwd + bwd
    odd = bwd - fwd
    wk = jnp.where(row == 0, 0.5 / seq, 1.0 / seq)
    hre_ref[...] = _dot(cb_ref[...], even.astype(BF)) * wk
    him_ref[...] = _dot(sb_ref[...], odd.astype(BF)) * wk
    alt = jnp.where((row & 1) == 0, 1.0, -1.0)
    hny_ref[...] = jnp.sum(alt * even, axis=0, keepdims=True) * (0.5 / seq)


def _hy_filter(seq, emb, w1, b1, w2, b2, freq, w3, log_decay, cmat, smat):
    dc = 512
    nj = D // dc
    small = [_const_spec(a.shape) for a in (emb, w1, b1, w2, b2, freq)]
    return pl.pallas_call(
        _hy_filter_kernel,
        grid=(HY_ORDER, nj),
        in_specs=small + [pl.BlockSpec((HY_FFN, dc), lambda o, j: (0, (2 * o) * nj + j)),
                          pl.BlockSpec((HY_FFN, dc), lambda o, j: (0, (2 * o + 1) * nj + j)),
                          pl.BlockSpec((1, dc), lambda o, j: (0, (2 * o) * nj + j)),
                          pl.BlockSpec((1, dc), lambda o, j: (0, (2 * o + 1) * nj + j)),
                          _const_spec((seq, seq)), _const_spec((seq, seq))],
        out_specs=[pl.BlockSpec((None, seq, dc), lambda o, j: (o, 0, j)),
                   pl.BlockSpec((None, seq, dc), lambda o, j: (o, 0, j)),
                   pl.BlockSpec((None, 1, dc), lambda o, j: (o, 0, j))],
        out_shape=[jax.ShapeDtypeStruct((HY_ORDER, seq, D), F32), jax.ShapeDtypeStruct((HY_ORDER, seq, D), F32),
                   jax.ShapeDtypeStruct((HY_ORDER, 1, D), F32)],
        scratch_shapes=[pltpu.VMEM((seq, HY_FFN), F32), pltpu.VMEM((seq, seq), BF), pltpu.VMEM((seq, seq), BF)],
        compiler_params=_cparams(2),
        name=f"hy_filter_{seq}",
    )(emb, w1, b1, w2, b2, freq, w3, w3, log_decay, log_decay, cmat, smat)


HY_SUB = 256


def _hy_conv_kernel(u0_ref, u1_ref, u2_ref, sw0_ref, sw1_ref, sw2_ref, sb0_ref, sb1_ref, sb2_ref,
                    fb_ref, hre_ref, him_ref, hny_ref, c_ref, s_ref, o_ref, cb_ref, sb_ref):
    seq, dc = u0_ref.shape

    @pl.when((pl.program_id(0) == 0) & (pl.program_id(1) == 0))
    def _():
        cb_ref[...] = c_ref[...].astype(BF)
        sb_ref[...] = s_ref[...].astype(BF)

    row = lax.broadcasted_iota(jnp.int32, (seq, HY_SUB), 0)
    alt = jnp.where((row & 1) == 0, 1.0, -1.0)

    def sub_tile(cols):
        def short_conv(u_ref, w_ref, b_ref):
            u = u_ref[:, cols]
            prev = jnp.where(row == 0, 0.0, pltpu.roll(u, 1, 0))
            nxt = jnp.where(row == seq - 1, 0.0, pltpu.roll(u, seq - 1, 0))
            return prev * w_ref[0:1, cols] + u * w_ref[1:2, cols] + nxt * w_ref[2:3, cols] + b_ref[:, cols]

        z = short_conv(u0_ref, sw0_ref, sb0_ref)
        gates = (short_conv(u1_ref, sw1_ref, sb1_ref), short_conv(u2_ref, sw2_ref, sb2_ref))
        yield
        for o in range(HY_ORDER):
            zb = z.astype(BF)
            zc, zs = _dot(cb_ref[...], zb), _dot(sb_ref[...], zb)
            yield
            hre, him = hre_ref[o, :, cols], him_ref[o, :, cols]
            p_re = (zc * hre + zs * him).astype(BF)
            p_im = (zc * him - zs * hre).astype(BF)
            y = _dot(cb_ref[...], p_re) - _dot(sb_ref[...], p_im)
            yield
            nyq = jnp.sum(alt * z, axis=0, keepdims=True) * hny_ref[o, :, cols]
            z = gates[o] * (y + alt * nyq + z * fb_ref[o:o + 1, cols])
        o_ref[:, cols] = z.astype(BF)

    _lockstep(sub_tile(slice(j * HY_SUB, (j + 1) * HY_SUB)) for j in range(dc // HY_SUB))


def _hy_conv(u, short_w, short_b, filter_bias, hre, him, hny, cmat, smat, seq, nbatch, row0, dc):
    nj = D // dc
    r0 = row0 // seq

    def part(p):
        return pl.BlockSpec((seq, dc), lambda j, b: (r0 + b, p * nj + j))

    def vec(rows, p):
        return pl.BlockSpec((rows, dc), lambda j, b: (0, p * nj + j))

    in_specs = ([part(p) for p in range(3)] + [vec(3, p) for p in range(3)] + [vec(1, p) for p in range(3)]
                + [pl.BlockSpec((HY_ORDER, dc), lambda j, b: (0, j)),
                   pl.BlockSpec((HY_ORDER, seq, dc), lambda j, b: (0, 0, j), pipeline_mode=pl.Buffered(1)),
                   pl.BlockSpec((HY_ORDER, seq, dc), lambda j, b: (0, 0, j), pipeline_mode=pl.Buffered(1)),
                   pl.BlockSpec((HY_ORDER, 1, dc), lambda j, b: (0, 0, j)),
                   _const_spec((seq, seq)), _const_spec((seq, seq))])
    return pl.pallas_call(
        _hy_conv_kernel,
        grid=(nj, nbatch),
        in_specs=in_specs,
        out_specs=pl.BlockSpec((seq, dc), lambda j, b: (b, j)),
        out_shape=jax.ShapeDtypeStruct((nbatch * seq, D), BF),
        scratch_shapes=[pltpu.VMEM((seq, seq), BF), pltpu.VMEM((seq, seq), BF)],
        compiler_params=_cparams(2),
        name=f"hy_conv_{seq}",
    )(u, u, u, short_w, short_w, short_w, short_b, short_b, short_b, filter_bias, hre, him, hny, cmat, smat)


def _dft_tables(seq):
    k = np.arange(seq, dtype=np.int64)
    ang = np.pi * ((k[:, None] * k[None, :]) % (2 * seq)) / seq
    return jnp.asarray(np.cos(ang), F32), jnp.asarray(np.sin(ang), F32)


def _hy_embedding(seq):
    t = np.arange(seq, dtype=np.float32) / np.float32(seq)
    ang = (2.0 * math.pi) * t[:, None] * np.arange(1, HY_BANDS + 1, dtype=np.float32)
    emb = np.concatenate([t[:, None], np.cos(ang), np.sin(ang)], axis=-1).astype(np.float32)
    return jnp.asarray(np.pad(emb, ((0, 0), (0, HY_EMB_PAD - HY_EMB))))


def _post_kernel(*refs, split_x, split_out, tm):
    oc_ref, ol_ref = refs[0:2]
    x_refs, refs = (refs[2:4], refs[4:]) if split_x else (refs[2:3], refs[3:])
    mod_ref, wo_ref, g1_ref, b1_ref, w1c_ref, w2c_ref, g2_ref, b2_ref = refs[0:8]
    outs, (w1_ref, w2_ref, h_ref, acc_ref) = refs[8:-4], refs[-4:]
    x1_ref = outs[0]
    step = pl.program_id(0)
    is_lat = _is_lat(tm, N_FF_CHUNKS - 1)
    nsub = tm // SUB_POST
    per = MLP_CHUNK // FF_CHUNK

    def rows(j):
        return slice(j * SUB_POST, (j + 1) * SUB_POST)

    def pick(c_ref, l_ref, j):
        return jnp.where(is_lat, l_ref[rows(j), :], c_ref[rows(j), :])

    def norm1(j):
        a = _dot(pick(oc_ref, ol_ref, j), wo_ref[...])
        x = pick(x_refs[0], x_refs[1], j) if split_x else x_refs[0][rows(j), :]
        x1 = _layer_norm(DN_ALPHA * x + mod_ref[2:3, :] * a, g1_ref[...], b1_ref[...])
        return x1, _modulate(x1, mod_ref, 3, 4)

    def norm2(x1, acc):
        return _layer_norm(DN_ALPHA * x1 + mod_ref[5:6, :] * acc, g2_ref[...], b2_ref[...])

    def mlp_chunk(h, c):
        a = jnp.concatenate([_dot(h, w1_ref[per * c + i]) for i in range(per)], axis=1)
        a = jnp.maximum(a, 0.0)
        return _dot((a * a).astype(BF), w2_ref[c])

    def write_branched(ys):
        yc_ref, yl_ref = outs

        @pl.when(jnp.logical_not(is_lat))
        def _():
            for j, y in enumerate(ys):
                yc_ref[rows(j), :] = y

        @pl.when(is_lat)
        def _():
            for j, y in enumerate(ys):
                yl_ref[rows(j), :] = y

    @pl.when(step < N_FF_CHUNKS)
    def _():
        w2_rows = pl.ds(pl.multiple_of((step % per) * FF_CHUNK, FF_CHUNK), FF_CHUNK)
        w1_ref[step] = w1c_ref[...].astype(BF)
        w2_ref[step // per, w2_rows, :] = w2c_ref[...].astype(BF)

        @pl.when(step == 0)
        def _():
            for j in range(nsub):
                x1_ref[rows(j), :], h_ref[rows(j), :] = norm1(j)
            acc_ref[...] = jnp.zeros_like(acc_ref)

        a = jnp.maximum(_dot(h_ref[...], w1_ref[step]), 0.0)
        acc_ref[...] += _dot((a * a).astype(BF), w2_ref[step // per, w2_rows, :])

        @pl.when(step == N_FF_CHUNKS - 1)
        def _():
            ys = [norm2(x1_ref[rows(j), :], acc_ref[rows(j), :]) for j in range(nsub)]
            if split_out:
                write_branched(ys)
            else:
                for j, y in enumerate(ys):
                    outs[0][rows(j), :] = y

    def token_tile():
        ys = []
        cur = norm1(0)
        prev = None
        for j in range(nsub):
            x1, h = cur
            acc = mlp_chunk(h, 0)
            if j + 1 < nsub:
                cur = norm1(j + 1)
            if prev is not None:
                ys.append(norm2(*prev))
                if not split_out:
                    outs[0][rows(j - 1), :] = ys[-1]
            for c in range(1, D_FF // MLP_CHUNK):
                acc = acc + mlp_chunk(h, c)
            prev = (x1, acc)
        ys.append(norm2(*prev))
        if split_out:
            write_branched(ys)
        else:
            outs[0][rows(nsub - 1), :] = ys[-1]

    pl.when(step >= N_FF_CHUNKS)(token_tile)


def _post(o_ctx, o_lat, xs, mods, layer, w_o, g1, b1, w1, w2, g2, b2, split_out):
    tm, off = TM_POST, N_FF_CHUNKS - 1
    split_x = len(xs) == 2
    x_specs = [_ctx_spec(D, tm, off), _lat_spec(D, tm, off)] if split_x else [_tok_spec(D, tm, off)]
    vec = _const_spec((1, D))
    if split_out:
        out_specs = [_ctx_spec(D, tm, off), _lat_spec(D, tm, off)]
        out_shape = [jax.ShapeDtypeStruct((T_CTX, D), F32), jax.ShapeDtypeStruct((T_LAT, D), F32)]
    else:
        out_specs = _tok_spec(D, tm, off)
        out_shape = jax.ShapeDtypeStruct((T, D), F32)

    def chunk(i):
        return jnp.minimum(i, N_FF_CHUNKS - 1)

    return pl.pallas_call(
        functools.partial(_post_kernel, split_x=split_x, split_out=split_out, tm=tm),
        grid=(off + T // tm,),
        in_specs=[_ctx_spec(D, tm, off), _lat_spec(D, tm, off)] + x_specs + [
            _mod_spec(layer, tm, off), _const_spec((D, D)), vec, vec,
            pl.BlockSpec((None, D, FF_CHUNK), lambda i: (layer, 0, chunk(i))),
            pl.BlockSpec((None, FF_CHUNK, D), lambda i: (layer, chunk(i), 0)), vec, vec],
        out_specs=out_specs,
        out_shape=out_shape,
        scratch_shapes=[pltpu.VMEM((N_FF_CHUNKS, D, FF_CHUNK), BF), pltpu.VMEM((D_FF // MLP_CHUNK, MLP_CHUNK, D), BF),
                        pltpu.VMEM((tm, D), BF), pltpu.VMEM((tm, D), F32)],
        compiler_params=_cparams(1, POST_VMEM_LIMIT),
        name=f"post_l{layer}",
    )(o_ctx, o_lat, *xs, mods, w_o, g1, b1, w1, w2, g2, b2)


def _rope_tables():
    n = HEAD_DIM // 4
    pos = np.arange(DEC_SEQ)
    inv = (np.float32(ROPE_BASE) ** (-np.arange(n, dtype=np.float32) / np.float32(n))).astype(np.float32)
    ang_r = ((pos // GRID_W).astype(np.float32)[:, None] * inv).astype(np.float32)
    ang_c = ((pos % GRID_W).astype(np.float32)[:, None] * inv).astype(np.float32)
    cr, sr, cc, sc = np.cos(ang_r), np.sin(ang_r), np.cos(ang_c), np.sin(ang_c)
    a = np.tile(np.concatenate([cr, cr, cc, cc], axis=-1), (1, D // HEAD_DIM))
    b = np.tile(np.concatenate([-sr, sr, -sc, sc], axis=-1), (1, D // HEAD_DIM))
    a = np.concatenate([a, np.ones((TM, D), np.float32)], axis=0)
    b = np.concatenate([b, np.zeros((TM, D), np.float32)], axis=0)
    return jnp.asarray(a, F32), jnp.asarray(b, F32)


def kernel(x_prompt, x_sample, c, cache_da_k, cache_da_v, cache_na_k, cache_na_v, cache_gq_k, cache_gq_v, c_ctx, ada_w, ada_b, ln_g, ln_b, mlp_w1, mlp_w2, da_w_qkv, da_w_o, da_lambda, da_subln_g, na_w_qkv, na_w_o, na_rel_bias, gq_w_qkv, gq_w_o, gq_q_norm, gq_k_norm, hy_w_in, hy_short_w, hy_short_b, hy_ffn_w1, hy_ffn_b1, hy_ffn_w2, hy_ffn_b2, hy_ffn_freq, hy_ffn_w3, hy_log_decay, hy_filter_bias, hy_w_o):
    cvec = jnp.concatenate([c_ctx[None, :], c, jnp.zeros((MOD_ROWS - 1 - DEC_BATCH, D), F32)], axis=0)
    mods = _mods(cvec, ada_w, ada_b)
    rope_a, rope_b = _rope_tables()

    def finish(o_ctx, o_lat, xs, layer, w_o, split_out=False):
        return _post(o_ctx, o_lat, xs, mods, layer, w_o.astype(BF), ln_g[layer, 0][None], ln_b[layer, 0][None],
                     mlp_w1, mlp_w2, ln_g[layer, 1][None], ln_b[layer, 1][None], split_out)

    xs = (x_prompt.reshape(T_CTX, D), x_sample.reshape(T_LAT, D))
    qb, kb, vb, ks, vs = _da_proj(*xs, mods, 0, da_w_qkv[0].astype(BF), rope_a, rope_b)
    state_da_k = ks.reshape(BATCH, 1, SEQ, DA_HEADS, 2 * HEAD_DIM)
    state_da_v = vs.reshape(BATCH, 1, SEQ, DA_HEADS, 2 * HEAD_DIM)
    o_ctx, o_lat = _da_attention(qb, kb, vb, cache_da_k, cache_da_v, da_lambda[0], da_subln_g[0][None], 0)
    x = finish(o_ctx, o_lat, xs, 0, da_w_o[0])

    qb, kb, vb, ks, vs = _na_proj(x, mods, 1, na_w_qkv[0].astype(BF))
    state_na_k, state_na_v = _untranspose_state(ks, NA_HEADS), _untranspose_state(vs, NA_HEADS)
    onehot, neg, mask = _na_constants()
    bias_tab = _na_bias_table(na_rel_bias[0], onehot, neg)
    o_ctx = _na_ctx_attention(qb, kb, vb)
    o_lat = _na_lat_attention(qb, kb, vb, _features_major(cache_na_k), _features_major(cache_na_v), bias_tab, mask)
    x = finish(o_ctx, o_lat, (x,), 1, na_w_o[0])

    g_mat = jnp.asarray(np.kron(np.eye(GN_BLOCK // HEAD_DIM), np.full((HEAD_DIM, HEAD_DIM), 1.0 / HEAD_DIM)), BF)
    qb, kb, vb, ks, vs = _gq_proj(x, mods, 2, gq_w_qkv[0].astype(BF), g_mat,
                                  jnp.tile(gq_q_norm[0], GQ_HEADS)[None], jnp.tile(gq_k_norm[0], GQ_KV_HEADS)[None],
                                  rope_a, rope_b)
    state_gq_k, state_gq_v = _untranspose_state(ks, GQ_KV_HEADS), _untranspose_state(vs, GQ_KV_HEADS)
    o_ctx, o_lat = _gq_attention(qb, kb, vb, _features_major(cache_gq_k), _features_major(cache_gq_v))
    x = finish(o_ctx, o_lat, (x,), 2, gq_w_o[0])

    u = _hy_proj(x, mods, 3, hy_w_in[0].astype(BF))
    w1 = jnp.pad(hy_ffn_w1[0], ((0, HY_EMB_PAD - HY_EMB), (0, 0)))
    zs = []
    for seq, nbatch, row0, dc in ((SEQ, BATCH, 0, D), (DEC_SEQ, DEC_BATCH, T_CTX, 512)):
        cmat, smat = _dft_tables(seq)
        hre, him, hny = _hy_filter(seq, _hy_embedding(seq), w1, hy_ffn_b1[0][None], hy_ffn_w2[0], hy_ffn_b2[0][None],
                                   hy_ffn_freq[0][None], hy_ffn_w3[0], hy_log_decay[0][None], cmat, smat)
        zs.append(_hy_conv(u, hy_short_w[0], hy_short_b[0][None], hy_filter_bias[0], hre, him, hny, cmat, smat,
                           seq, nbatch, row0, dc))
    y_ctx, y_lat = finish(zs[0], zs[1], (x,), 3, hy_w_o[0], split_out=True)

    return (y_ctx.reshape(BATCH, SEQ, D), y_lat.reshape(DEC_BATCH, DEC_SEQ, D),
            state_da_k, state_da_v, state_na_k, state_na_v, state_gq_k, state_gq_v)
```

```python
import functools
import math

import numpy as np
import jax
import jax.numpy as jnp
from jax import lax
from jax.experimental import pallas as pl
from jax.experimental.pallas import tpu as pltpu

F32 = jnp.float32
BF = jnp.bfloat16

D = 1024
BATCH = 16
SEQ = 256
DEC_BATCH = 8
DEC_SEQ = 1024
PAST = 256
DEPTH = 4
GRID_W = 64
GRID_ROWS = DEC_SEQ // GRID_W
D_FF = 4 * D
T_CTX = BATCH * SEQ
T_LAT = DEC_BATCH * DEC_SEQ
T = T_CTX + T_LAT
HEAD_DIM = 64
ATT_SCALE = HEAD_DIM ** -0.5
LOG2E = math.log2(math.e)
Q_SCALE = ATT_SCALE * LOG2E
DA_HEADS = 8
NA_HEADS = 16
NA_WIN_ROWS = 8
NA_WIN_COLS = 16
GQ_HEADS = 16
GQ_KV_HEADS = 4
HY_ORDER = 2
HY_BANDS = 16
HY_EMB = 1 + 2 * HY_BANDS
HY_EMB_PAD = 40
HY_FFN = 64
ROPE_BASE = 10000.0
LN_EPS = 1e-5
RMS_EPS = 1e-6
DN_ALPHA = (2 * DEPTH) ** 0.25
NEG_INF = -1e30

LANES = 128
TM = 512
TM_POST = 512
FF_CHUNK = 512
MLP_CHUNK = 1024
N_FF_CHUNKS = D_FF // FF_CHUNK
SUB_POST = 256
N_CTX_TILES = T_CTX // TM
N_TILES = T // TM
TQ = 512
MOD_ROWS = 16
VMEM_LIMIT = 56 * 1024 * 1024
POST_VMEM_LIMIT = 58 * 1024 * 1024


def _cparams(n_axes, vmem_limit=VMEM_LIMIT):
    return pltpu.CompilerParams(dimension_semantics=("arbitrary",) * n_axes,
                                vmem_limit_bytes=vmem_limit)


def _dot(a, b):
    return jnp.dot(a, b, preferred_element_type=F32)


def _dot_nt(a, b):
    return lax.dot_general(a, b, (((1,), (1,)), ((), ())), preferred_element_type=F32)


def _const_spec(shape):
    nd = len(shape)
    return pl.BlockSpec(shape, lambda *_: (0,) * nd, pipeline_mode=pl.Buffered(1))


def _mod_spec(layer, tm=TM, off=0):
    nctx = T_CTX // tm

    def row(i):
        t = jnp.maximum(i - off, 0)
        return jnp.where(t < nctx, 0, 1 + (t - nctx) // (DEC_SEQ // tm))

    return pl.BlockSpec((None, None, 6, D), lambda i: (layer, row(i), 0, 0))


def _tok_spec(width, tm=TM, off=0):
    return pl.BlockSpec((tm, width), lambda i: (jnp.maximum(i - off, 0), 0))


def _ctx_spec(width, tm=TM, off=0):
    return pl.BlockSpec((tm, width), lambda i: (jnp.clip(i - off, 0, T_CTX // tm - 1), 0))


def _lat_spec(width, tm=TM, off=0):
    return pl.BlockSpec((tm, width), lambda i: (jnp.maximum(i - off - T_CTX // tm, 0), 0))


def _is_lat(tm=TM, off=0):
    return pl.program_id(0) >= off + T_CTX // tm


def _pick(ctx_ref, lat_ref):
    return jnp.where(_is_lat(), lat_ref[...], ctx_ref[...])


def _layer_norm(r, g, b):
    mu = jnp.mean(r, axis=-1, keepdims=True)
    c = r - mu
    var = jnp.mean(c * c, axis=-1, keepdims=True)
    return c * lax.rsqrt(var + LN_EPS) * g + b


def _mods_kernel(c_ref, w_ref, b_ref, o_ref):
    c = c_ref[...]
    s = (c / (1.0 + jnp.exp(-c))).astype(BF)
    o_ref[...] = _dot(s, w_ref[...].astype(BF)) + b_ref[...]


def _mods(cvec, ada_w, ada_b):
    tn = 1536
    out = pl.pallas_call(
        _mods_kernel,
        grid=(DEPTH, 6 * D // tn),
        in_specs=[pl.BlockSpec((MOD_ROWS, D), lambda l, n: (0, 0)),
                  pl.BlockSpec((None, D, tn), lambda l, n: (l, 0, n)),
                  pl.BlockSpec((None, 1, tn), lambda l, n: (l, 0, n))],
        out_specs=pl.BlockSpec((None, MOD_ROWS, tn), lambda l, n: (l, 0, n)),
        out_shape=jax.ShapeDtypeStruct((DEPTH, MOD_ROWS, 6 * D), F32),
        compiler_params=_cparams(2),
        name="adaln_mods",
    )(cvec, ada_w, ada_b.reshape(DEPTH, 1, 6 * D))
    return out.reshape(DEPTH, MOD_ROWS, 6, D)


def _modulate(x, mod_ref, shift, scale):
    return (x * (1.0 + mod_ref[scale:scale + 1, :]) + mod_ref[shift:shift + 1, :]).astype(BF)


def _rope(x, a, b):
    n = x.shape[1]
    lane = lax.broadcasted_iota(jnp.int32, x.shape, 1)
    partner = jnp.where((lane & 16) == 0, pltpu.roll(x, n - 16, 1), pltpu.roll(x, 16, 1))
    return x * a + partner * b


def _rope_spec(width):
    per = DEC_SEQ // TM
    return pl.BlockSpec((TM, width), lambda i: (jnp.where(i < N_CTX_TILES, per, (i - N_CTX_TILES) % per), 0))


def _lockstep(gens):
    gens = list(gens)
    while gens:
        gens = [g for g in gens if next(g, True) is None]


def _store_state(k, v, ks_ref, vs_ref, transposed):
    @pl.when(jnp.logical_not(_is_lat()))
    def _():
        if not transposed:
            ks_ref[...] = k
            vs_ref[...] = v
        else:
            n = k.shape[1]
            for x, ref in ((k, ks_ref), (v, vs_ref)):
                xt = x.T
                for j in range(TM // SEQ):
                    ref[j * n:(j + 1) * n, :] = xt[:, j * SEQ:(j + 1) * SEQ]


def _qkv_out(nq, nk, transposed_state):
    specs = [_tok_spec(nq), _tok_spec(nk), _tok_spec(nk)]
    shapes = [jax.ShapeDtypeStruct((T, nq), BF), jax.ShapeDtypeStruct((T, nk), BF), jax.ShapeDtypeStruct((T, nk), BF)]
    if transposed_state:
        rows = (TM // SEQ) * nk
        specs += [pl.BlockSpec((rows, SEQ), lambda i: (jnp.minimum(i, N_CTX_TILES - 1), 0))] * 2
        shapes += [jax.ShapeDtypeStruct((BATCH * nk, SEQ), F32)] * 2
    else:
        specs += [_ctx_spec(nk)] * 2
        shapes += [jax.ShapeDtypeStruct((T_CTX, nk), F32)] * 2
    return specs, shapes


def _features_major(cache):
    b, _, past, heads, dh = cache.shape
    return cache.transpose(0, 1, 3, 4, 2).reshape(b, heads * dh, past)


def _untranspose_state(st, heads):
    return st.reshape(BATCH, heads, HEAD_DIM, SEQ).transpose(0, 3, 1, 2)[:, None]


def _da_proj_kernel(xc_ref, xl_ref, mod_ref, w_ref, ra_ref, rb_ref, qb_ref, kb_ref, vb_ref, ks_ref, vs_ref):
    h = _modulate(_pick(xc_ref, xl_ref), mod_ref, 0, 1)
    a, b = ra_ref[...], rb_ref[...]
    q = _dot(h, w_ref[:, 0:D])
    k = _dot(h, w_ref[:, D:2 * D])
    qb_ref[...] = (_rope(q, a, b) * Q_SCALE).astype(BF)
    v = _dot(h, w_ref[:, 2 * D:3 * D])
    kb_ref[...] = _rope(k, a, b).astype(BF)
    vb_ref[...] = v.astype(BF)
    _store_state(k, v, ks_ref, vs_ref, False)


def _da_proj(x_ctx, x_lat, mods, layer, w, rope_a, rope_b):
    specs, shapes = _qkv_out(D, D, False)
    return pl.pallas_call(
        _da_proj_kernel,
        grid=(N_TILES,),
        in_specs=[_ctx_spec(D), _lat_spec(D), _mod_spec(layer), _const_spec((D, 3 * D)),
                  _rope_spec(D), _rope_spec(D)],
        out_specs=specs, out_shape=shapes,
        compiler_params=_cparams(1),
        name=f"da_proj_l{layer}",
    )(x_ctx, x_lat, mods, w, rope_a, rope_b)


def _na_proj_kernel(x_ref, mod_ref, w_ref, qb_ref, kb_ref, vb_ref, ks_ref, vs_ref):
    h = _modulate(x_ref[...], mod_ref, 0, 1)
    q = _dot(h, w_ref[:, 0:D])
    k = _dot(h, w_ref[:, D:2 * D])
    qb_ref[...] = (q * Q_SCALE).astype(BF)
    v = _dot(h, w_ref[:, 2 * D:3 * D])
    kb_ref[...] = k.astype(BF)
    vb_ref[...] = v.astype(BF)
    _store_state(k, v, ks_ref, vs_ref, True)


def _na_proj(x, mods, layer, w):
    specs, shapes = _qkv_out(D, D, True)
    return pl.pallas_call(
        _na_proj_kernel,
        grid=(N_TILES,),
        in_specs=[_tok_spec(D), _mod_spec(layer), _const_spec((D, 3 * D))],
        out_specs=specs, out_shape=shapes,
        compiler_params=_cparams(1),
        name=f"na_proj_l{layer}",
    )(x, mods, w)


GN_BLOCK = 256


def _head_rms(x, g_ref, gain):
    x2 = x * x
    hi = x2.astype(BF)
    lo = (x2 - hi.astype(F32)).astype(BF)
    g = g_ref[...]
    ms = jnp.concatenate(
        [_dot(hi[:, j:j + GN_BLOCK], g) + _dot(lo[:, j:j + GN_BLOCK], g) for j in range(0, x.shape[1], GN_BLOCK)],
        axis=1)
    return x * lax.rsqrt(ms + RMS_EPS) * gain


def _gq_proj_kernel(x_ref, mod_ref, w_ref, g_ref, qn_ref, kn_ref, ra_ref, rb_ref,
                    qb_ref, kb_ref, vb_ref, ks_ref, vs_ref):
    nq, nk = GQ_HEADS * HEAD_DIM, GQ_KV_HEADS * HEAD_DIM
    h = _modulate(x_ref[...], mod_ref, 0, 1)
    a, b = ra_ref[...], rb_ref[...]
    q = _dot(h, w_ref[:, 0:nq])
    k = _dot(h, w_ref[:, nq:nq + nk])
    v = _dot(h, w_ref[:, nq + nk:nq + 2 * nk])
    k = _head_rms(k, g_ref, kn_ref[...])
    q = _head_rms(q, g_ref, qn_ref[...])
    kb_ref[...] = _rope(k, a[:, 0:nk], b[:, 0:nk]).astype(BF)
    qb_ref[...] = (_rope(q, a, b) * Q_SCALE).astype(BF)
    vb_ref[...] = v.astype(BF)
    _store_state(k, v, ks_ref, vs_ref, True)


def _gq_proj(x, mods, layer, w, g_mat, qn, kn, rope_a, rope_b):
    nq, nk = GQ_HEADS * HEAD_DIM, GQ_KV_HEADS * HEAD_DIM
    specs, shapes = _qkv_out(nq, nk, True)
    return pl.pallas_call(
        _gq_proj_kernel,
        grid=(N_TILES,),
        in_specs=[_tok_spec(D), _mod_spec(layer), _const_spec((D, nq + 2 * nk)),
                  _const_spec((GN_BLOCK, GN_BLOCK)), _const_spec((1, nq)), _const_spec((1, nk)),
                  _rope_spec(D), _rope_spec(D)],
        out_specs=specs, out_shape=shapes,
        compiler_params=_cparams(1),
        name=f"gq_proj_l{layer}",
    )(x, mods, w, g_mat, qn, kn, rope_a, rope_b)


def _hy_proj_kernel(x_ref, mod_ref, w_ref, u_ref):
    h = _modulate(x_ref[...], mod_ref, 0, 1)
    for c in range(HY_ORDER + 1):
        u_ref[:, c * D:(c + 1) * D] = _dot(h, w_ref[:, c * D:(c + 1) * D])


def _hy_proj(x, mods, layer, w):
    n = (HY_ORDER + 1) * D
    return pl.pallas_call(
        _hy_proj_kernel,
        grid=(N_TILES,),
        in_specs=[_tok_spec(D), _mod_spec(layer), _const_spec((D, n))],
        out_specs=_tok_spec(n),
        out_shape=jax.ShapeDtypeStruct((T, n), F32),
        compiler_params=_cparams(1),
        name=f"hy_proj_l{layer}",
    )(x, mods, w)


def _scores(qm, segs):
    return [_dot(qm, seg[0]) if len(seg) == 3 else _dot_nt(qm, seg[0]) for seg in segs]


def _softmax_finish(scores, segs):
    m = scores[0].max(axis=-1, keepdims=True)
    for s in scores[1:]:
        m = jnp.maximum(m, s.max(axis=-1, keepdims=True))
    den = None
    out = None
    for s, seg in zip(scores, segs):
        e = jnp.exp2(s - m)
        d = e.sum(axis=-1, keepdims=True)
        o = _dot_nt(e.astype(BF), seg[1]) if len(seg) == 3 else _dot(e.astype(BF), seg[1])
        den = d if den is None else den + d
        out = o if out is None else out + o
    return out / den


def _stack_halves(q, keep):
    return jnp.concatenate([q * keep[0], q * keep[1]], axis=0)


def _pipelined(jobs, score_fn, finish_fn):
    nxt = score_fn(jobs[0])
    for n, job in enumerate(jobs):
        cur, nxt = nxt, (score_fn(jobs[n + 1]) if n + 1 < len(jobs) else None)
        finish_fn(job, cur)


def _lane_half(shape):
    return lax.broadcasted_iota(jnp.int32, shape, 1) // HEAD_DIM


def _half_keep(half):
    return tuple(jnp.where(half == a, 1.0, 0.0).astype(BF) for a in (0, 1))


def _da_attn_kernel(*refs, has_cache, lam_init):
    if has_cache:
        q_ref, k_ref, v_ref, ck_ref, cv_ref, lam_ref, g_ref, o_ref = refs
    else:
        q_ref, k_ref, v_ref, lam_ref, g_ref, o_ref = refs
    lp = lam_ref[...]
    lam = (jnp.exp(jnp.sum(lp[0:1] * lp[1:2], axis=-1, keepdims=True))
           - jnp.exp(jnp.sum(lp[2:3] * lp[3:4], axis=-1, keepdims=True)) + lam_init)
    gain = g_ref[...] * (1.0 - lam_init)
    w = 2 * HEAD_DIM
    tq = min(TQ, q_ref.shape[0])
    keep = _half_keep(_lane_half((tq, w)))
    segs = []
    for hd in range(k_ref.shape[1] // w):
        cols = slice(hd * w, (hd + 1) * w)
        seg = [(k_ref[:, cols], v_ref[:, cols])]
        if has_cache:
            head = pl.program_id(1) * (k_ref.shape[1] // w) + hd
            seg.append((ck_ref[:, head, :].astype(BF), cv_ref[:, head, :].astype(BF)))
        segs.append(seg)
    jobs = [(hd, t, a) for hd in range(len(segs)) for t in range(q_ref.shape[0] // tq) for a in (0, 1)]
    first = {}

    def score_fn(job):
        hd, t, a = job
        return _scores(q_ref[t * tq:(t + 1) * tq, hd * w:(hd + 1) * w] * keep[a], segs[hd])

    def finish_fn(job, scores):
        hd, t, a = job
        o = _softmax_finish(scores, segs[hd])
        if a == 0:
            first[0] = o
            return
        o = first[0] - lam * o
        ms = jnp.mean(o * o, axis=-1, keepdims=True)
        o_ref[t * tq:(t + 1) * tq, hd * w:(hd + 1) * w] = (o * lax.rsqrt(ms + RMS_EPS) * gain).astype(BF)

    _pipelined(jobs, score_fn, finish_fn)


DA_LAT_HEADS_PER_STEP = 4


def _da_attention(qb, kb, vb, cache_k, cache_v, lam_p, subln_g, layer_idx):
    lam_init = 0.8 - 0.6 * math.exp(-0.3 * layer_idx)
    w = 2 * HEAD_DIM
    small = [pl.BlockSpec((4, HEAD_DIM), lambda *_: (0, 0)), pl.BlockSpec((1, w), lambda *_: (0, 0))]
    o_ctx = pl.pallas_call(
        functools.partial(_da_attn_kernel, has_cache=False, lam_init=lam_init),
        grid=(BATCH,),
        in_specs=[pl.BlockSpec((SEQ, D), lambda b: (b, 0))] * 3 + small,
        out_specs=pl.BlockSpec((SEQ, D), lambda b: (b, 0)),
        out_shape=jax.ShapeDtypeStruct((T_CTX, D), BF),
        compiler_params=_cparams(1),
        name="da_attn_ctx",
    )(qb, kb, vb, lam_p, subln_g)
    k0 = T_CTX // DEC_SEQ
    hw = DA_LAT_HEADS_PER_STEP * w
    tok = pl.BlockSpec((DEC_SEQ, hw), lambda b, h: (k0 + b, h))
    c_spec = pl.BlockSpec((None, None, PAST, DA_HEADS, w), lambda b, h: (b, 0, 0, 0, 0))
    o_lat = pl.pallas_call(
        functools.partial(_da_attn_kernel, has_cache=True, lam_init=lam_init),
        grid=(DEC_BATCH, DA_HEADS // DA_LAT_HEADS_PER_STEP),
        in_specs=[tok, tok, tok, c_spec, c_spec] + small,
        out_specs=pl.BlockSpec((DEC_SEQ, hw), lambda b, h: (b, h)),
        out_shape=jax.ShapeDtypeStruct((T_LAT, D), BF),
        compiler_params=_cparams(2),
        name="da_attn_lat",
    )(qb, kb, vb, cache_k, cache_v, lam_p, subln_g)
    return o_ctx, o_lat


def _na_ctx_kernel(q_ref, k_ref, v_ref, o_ref):
    half = _lane_half((SEQ, LANES))
    keep = _half_keep(half)

    def seg(p):
        return [(k_ref[:, p * LANES:(p + 1) * LANES], v_ref[:, p * LANES:(p + 1) * LANES])]

    def score_fn(p):
        return _scores(_stack_halves(q_ref[:, p * LANES:(p + 1) * LANES], keep), seg(p))

    def finish_fn(p, scores):
        o = _softmax_finish(scores, seg(p))
        o_ref[:, p * LANES:(p + 1) * LANES] = jnp.where(half == 0, o[0:SEQ], o[SEQ:2 * SEQ]).astype(BF)

    _pipelined(list(range(NA_HEADS // 2)), score_fn, finish_fn)


def _na_ctx_attention(qb, kb, vb):
    spec = pl.BlockSpec((SEQ, D), lambda b: (b, 0))
    return pl.pallas_call(
        _na_ctx_kernel,
        grid=(BATCH,),
        in_specs=[spec] * 3,
        out_specs=spec,
        out_shape=jax.ShapeDtypeStruct((T_CTX, D), BF),
        compiler_params=_cparams(1),
        name="na_attn_ctx",
    )(qb, kb, vb)


NA_TILES = ((0, (0, 2, 4, 6)), (4, (0, 2, 4, 6, 8, 10)), (8, (4, 6, 8, 10, 12, 14)), (12, (8, 10, 12, 14)))
NA_MAX_CHUNKS = 6
NA_BIAS_BLOCKS = 2 * NA_WIN_ROWS - 2


NA_LAT_PAIRS_PER_STEP = 4


def _na_lat_kernel(q_ref, k_ref, v_ref, ck_ref, cv_ref, w_ref, m_ref, o_ref):
    rows = 4 * GRID_W
    half = _lane_half((rows, LANES))
    keep = _half_keep(half)
    caches = [(ck_ref[p * LANES:(p + 1) * LANES, :].astype(BF), cv_ref[p * LANES:(p + 1) * LANES, :].astype(BF))
              for p in range(NA_LAT_PAIRS_PER_STEP)]
    jobs = [(p, i) for p in range(NA_LAT_PAIRS_PER_STEP) for i in range(len(NA_TILES))]

    def key_rows(i):
        chunks = NA_TILES[i][1]
        return slice(chunks[0] * GRID_W, chunks[0] * GRID_W + len(chunks) * LANES)

    def score_fn(job):
        p, i = job
        cols = slice(p * LANES, (p + 1) * LANES)
        r0, chunks = NA_TILES[i]
        qm = _stack_halves(q_ref[i * rows:(i + 1) * rows, cols], keep)
        mask = m_ref[i, :, 0:len(chunks) * LANES]
        bias = jnp.concatenate(
            [jnp.concatenate([w_ref[p, a, (6 - kr + r0) * GRID_W:(6 - kr + r0) * GRID_W + rows, :] for kr in chunks],
                             axis=1) + mask for a in (0, 1)], axis=0)
        return [_dot_nt(qm, k_ref[key_rows(i), cols]) + bias, _dot(qm, caches[p][0])]

    def finish_fn(job, scores):
        p, i = job
        cols = slice(p * LANES, (p + 1) * LANES)
        o = _softmax_finish(scores, [(None, v_ref[key_rows(i), cols]), (None, caches[p][1], True)])
        o_ref[i * rows:(i + 1) * rows, cols] = jnp.where(half == 0, o[0:rows], o[rows:2 * rows]).astype(BF)

    _pipelined(jobs, score_fn, finish_fn)


def _na_lat_attention(qb, kb, vb, cache_k, cache_v, bias_tab, mask_tab):
    k0 = T_CTX // DEC_SEQ
    npair = NA_LAT_PAIRS_PER_STEP
    tok = pl.BlockSpec((DEC_SEQ, npair * LANES), lambda b, p: (k0 + b, p))
    c_spec = pl.BlockSpec((None, npair * LANES, PAST), lambda b, p: (b, p, 0))
    return pl.pallas_call(
        _na_lat_kernel,
        grid=(DEC_BATCH, NA_HEADS // 2 // npair),
        in_specs=[tok, tok, tok, c_spec, c_spec,
                  pl.BlockSpec((npair, 2, NA_BIAS_BLOCKS * GRID_W, LANES), lambda b, p: (p, 0, 0, 0)),
                  _const_spec(mask_tab.shape)],
        out_specs=pl.BlockSpec((DEC_SEQ, npair * LANES), lambda b, p: (b, p)),
        out_shape=jax.ShapeDtypeStruct((T_LAT, D), BF),
        compiler_params=_cparams(2),
        name="na_attn_lat",
    )(qb, kb, vb, cache_k, cache_v, bias_tab, mask_tab)


def _na_bias_kernel(t_ref, r_ref, n_ref, o_ref):
    t = t_ref[...]
    t1 = t.astype(BF)
    r1 = t - t1.astype(F32)
    t2 = r1.astype(BF)
    t3 = (r1 - t2.astype(F32)).astype(BF)
    r = r_ref[...]
    res = (_dot(t1, r) + _dot(t2, r) + _dot(t3, r) + n_ref[...]) * LOG2E
    for qc in range(GRID_W):
        o_ref[pl.ds(qc, t.shape[0], stride=GRID_W), :] = res[:, qc * LANES:(qc + 1) * LANES]


def _na_bias_table(rel_bias, onehot, neg):
    nrel = 2 * NA_WIN_COLS
    idx = 13 - np.arange(NA_BIAS_BLOCKS)[:, None] + np.arange(2)[None, :]
    t = jnp.pad(rel_bias[:, idx, :], ((0, 0), (0, 0), (0, 0), (0, 1)))
    t = t.reshape(NA_HEADS * NA_BIAS_BLOCKS, 2 * nrel)
    n = GRID_W * LANES
    out = pl.pallas_call(
        _na_bias_kernel,
        grid=(1,),
        in_specs=[pl.BlockSpec(t.shape, lambda j: (0, 0)),
                  pl.BlockSpec((2 * nrel, n), lambda j: (0, 0)),
                  pl.BlockSpec((1, n), lambda j: (0, 0))],
        out_specs=pl.BlockSpec((t.shape[0] * GRID_W, LANES), lambda j: (0, 0)),
        out_shape=jax.ShapeDtypeStruct((t.shape[0] * GRID_W, LANES), F32),
        compiler_params=_cparams(1),
        name="na_bias_table",
    )(t, onehot, neg)
    return out.reshape(NA_HEADS // 2, 2, NA_BIAS_BLOCKS * GRID_W, LANES)


def _na_constants():
    nrel = 2 * NA_WIN_COLS
    qc = np.arange(GRID_W)[:, None]
    kc = np.arange(GRID_W)[None, :]
    rel = np.clip(kc - qc, -(NA_WIN_COLS - 1), NA_WIN_COLS - 1) + NA_WIN_COLS - 1
    cs = np.clip(qc - NA_WIN_COLS // 2, 0, GRID_W - NA_WIN_COLS)
    col_in = (kc >= cs) & (kc < cs + NA_WIN_COLS)
    onehot = np.zeros((2, nrel, GRID_W, 2, GRID_W), np.float32)
    for hf in range(2):
        onehot[hf, rel, qc, hf, kc] = 1.0
    neg = np.where(col_in, 0.0, NEG_INF).astype(np.float32)
    neg = np.broadcast_to(neg[:, None, :], (GRID_W, 2, GRID_W)).reshape(1, -1)
    rows = 4 * GRID_W
    mask = np.full((len(NA_TILES), rows, NA_MAX_CHUNKS * LANES), NEG_INF, np.float32)
    kr = min(NA_WIN_ROWS, GRID_ROWS)
    for i, (r0, chunks) in enumerate(NA_TILES):
        qr = r0 + np.arange(rows)[:, None] // GRID_W
        rs = np.clip(qr - kr // 2, 0, GRID_ROWS - kr)
        for c, krow0 in enumerate(chunks):
            krow = krow0 + np.arange(LANES)[None, :] // GRID_W
            mask[i, :, c * LANES:(c + 1) * LANES] = np.where((krow >= rs) & (krow < rs + kr), 0.0, NEG_INF)
    return (jnp.asarray(onehot.reshape(2 * nrel, GRID_W * LANES), BF), jnp.asarray(neg), jnp.asarray(mask))


def _gq_attn_kernel(*refs, has_cache):
    if has_cache:
        q_ref, k_ref, v_ref, ck_ref, cv_ref, o_ref = refs
    else:
        q_ref, k_ref, v_ref, o_ref = refs
    group = GQ_HEADS // GQ_KV_HEADS
    qw = LANES * group
    tq = min(TQ // 2, q_ref.shape[0])
    half = _lane_half((tq, LANES))
    keep = _half_keep(half)
    segs = []
    for kvp in range(k_ref.shape[1] // LANES):
        kcols = slice(kvp * LANES, (kvp + 1) * LANES)
        seg = [(k_ref[:, kcols], v_ref[:, kcols])]
        if has_cache:
            seg.append((ck_ref[kcols, :].astype(BF), cv_ref[kcols, :].astype(BF), True))
        segs.append(seg)
    jobs = [(kvp, t, kh) for kvp in range(len(segs)) for t in range(q_ref.shape[0] // tq) for kh in (0, 1)]

    def blocks(job):
        kvp, t, kh = job
        for pair in (2 * kh, 2 * kh + 1):
            yield slice(t * tq, (t + 1) * tq), slice(kvp * qw + pair * LANES, kvp * qw + (pair + 1) * LANES)

    def score_fn(job):
        kh = job[2]
        parts = []
        for rows, cols in blocks(job):
            for a in (0, 1):
                qm = q_ref[rows, cols] * keep[a]
                parts.append(qm if a == kh else pltpu.roll(qm.astype(F32), HEAD_DIM, 1).astype(BF))
        return _scores(jnp.concatenate(parts, axis=0), segs[job[0]])

    def finish_fn(job, scores):
        kh = job[2]
        o = _softmax_finish(scores, segs[job[0]])
        for n, (rows, cols) in enumerate(blocks(job)):
            heads = [o[(2 * n + a) * tq:(2 * n + a + 1) * tq] for a in (0, 1)]
            heads = [h if a == kh else pltpu.roll(h, HEAD_DIM, 1) for a, h in enumerate(heads)]
            o_ref[rows, cols] = jnp.where(half == 0, heads[0], heads[1]).astype(BF)

    _pipelined(jobs, score_fn, finish_fn)


GQ_LAT_ROWS = 1024


def _gq_attention(qb, kb, vb, cache_k, cache_v):
    nk = GQ_KV_HEADS * HEAD_DIM
    qw = LANES * (GQ_HEADS // GQ_KV_HEADS)
    npair = GQ_KV_HEADS // 2
    o_ctx = pl.pallas_call(
        functools.partial(_gq_attn_kernel, has_cache=False),
        grid=(BATCH,),
        in_specs=[pl.BlockSpec((SEQ, D), lambda b: (b, 0))] + [pl.BlockSpec((SEQ, nk), lambda b: (b, 0))] * 2,
        out_specs=pl.BlockSpec((SEQ, D), lambda b: (b, 0)),
        out_shape=jax.ShapeDtypeStruct((T_CTX, D), BF),
        compiler_params=_cparams(1),
        name="gq_attn_ctx",
    )(qb, kb, vb)
    qt = DEC_SEQ // GQ_LAT_ROWS
    q0, k0 = T_CTX // GQ_LAT_ROWS, T_CTX // DEC_SEQ
    kv_spec = pl.BlockSpec((DEC_SEQ, LANES), lambda b, p, t: (k0 + b, p))
    c_spec = pl.BlockSpec((None, LANES, PAST), lambda b, p, t: (b, p, 0))
    o_lat = pl.pallas_call(
        functools.partial(_gq_attn_kernel, has_cache=True),
        grid=(DEC_BATCH, npair, qt),
        in_specs=[pl.BlockSpec((GQ_LAT_ROWS, qw), lambda b, p, t: (q0 + b * qt + t, p)), kv_spec, kv_spec, c_spec,
                  c_spec],
        out_specs=pl.BlockSpec((GQ_LAT_ROWS, qw), lambda b, p, t: (b * qt + t, p)),
        out_shape=jax.ShapeDtypeStruct((T_LAT, D), BF),
        compiler_params=_cparams(3),
        name="gq_attn_lat",
    )(qb, kb, vb, cache_k, cache_v)
    return o_ctx, o_lat


def _dot_3pass(a, b):
    ah, bh = a.astype(BF), b.astype(BF)
    al, bl = (a - ah.astype(F32)).astype(BF), (b - bh.astype(F32)).astype(BF)
    return _dot(ah, bh) + _dot(ah, bl) + _dot(al, bh)


def _hy_filter_kernel(emb_ref, w1_ref, b1_ref, w2_ref, b2_ref, fr_ref, w3f_ref, w3b_ref, ldf_ref, ldb_ref,
                      c_ref, s_ref, hre_ref, him_ref, hny_ref, hid_ref, cb_ref, sb_ref):
    seq = emb_ref.shape[0]

    @pl.when((pl.program_id(0) == 0) & (pl.program_id(1) == 0))
    def _():
        hp = lax.Precision.HIGHEST
        fr = fr_ref[...]
        hid = jnp.sin(fr * (jnp.dot(emb_ref[...], w1_ref[...], precision=hp, preferred_element_type=F32)
                            + b1_ref[...]))
        hid_ref[...] = jnp.sin(fr * (jnp.dot(hid, w2_ref[...], precision=hp, preferred_element_type=F32)
                                     + b2_ref[...]))
        cb_ref[...] = c_ref[...].astype(BF)
        sb_ref[...] = s_ref[...].astype(BF)

    hid = hid_ref[...]
    t = emb_ref[:, 0:1]
    fwd = _dot_3pass(hid, w3f_ref[...]) * jnp.exp(-jnp.exp(ldf_ref[...]) * t)
    bwd = _dot_3pass(hid, w3b_ref[...]) * jnp.exp(-jnp.exp(ldb_ref[...]) * t)
    row = lax.broadcasted_iota(jnp.int32, fwd.shape, 0)
    bwd = jnp.where(row == 0, 0.0, bwd)
    even = fwd + bwd
    odd = bwd - fwd
    wk = jnp.where(row == 0, 0.5 / seq, 1.0 / seq)
    hre_ref[...] = _dot(cb_ref[...], even.astype(BF)) * wk
    him_ref[...] = _dot(sb_ref[...], odd.astype(BF)) * wk
    alt = jnp.where((row & 1) == 0, 1.0, -1.0)
    hny_ref[...] = jnp.sum(alt * even, axis=0, keepdims=True) * (0.5 / seq)


def _hy_filter(seq, emb, w1, b1, w2, b2, freq, w3, log_decay, cmat, smat):
    dc = 512
    nj = D // dc
    small = [_const_spec(a.shape) for a in (emb, w1, b1, w2, b2, freq)]
    return pl.pallas_call(
        _hy_filter_kernel,
        grid=(HY_ORDER, nj),
        in_specs=small + [pl.BlockSpec((HY_FFN, dc), lambda o, j: (0, (2 * o) * nj + j)),
                          pl.BlockSpec((HY_FFN, dc), lambda o, j: (0, (2 * o + 1) * nj + j)),
                          pl.BlockSpec((1, dc), lambda o, j: (0, (2 * o) * nj + j)),
                          pl.BlockSpec((1, dc), lambda o, j: (0, (2 * o + 1) * nj + j)),
                          _const_spec((seq, seq)), _const_spec((seq, seq))],
        out_specs=[pl.BlockSpec((None, seq, dc), lambda o, j: (o, 0, j)),
                   pl.BlockSpec((None, seq, dc), lambda o, j: (o, 0, j)),
                   pl.BlockSpec((None, 1, dc), lambda o, j: (o, 0, j))],
        out_shape=[jax.ShapeDtypeStruct((HY_ORDER, seq, D), F32), jax.ShapeDtypeStruct((HY_ORDER, seq, D), F32),
                   jax.ShapeDtypeStruct((HY_ORDER, 1, D), F32)],
        scratch_shapes=[pltpu.VMEM((seq, HY_FFN), F32), pltpu.VMEM((seq, seq), BF), pltpu.VMEM((seq, seq), BF)],
        compiler_params=_cparams(2),
        name=f"hy_filter_{seq}",
    )(emb, w1, b1, w2, b2, freq, w3, w3, log_decay, log_decay, cmat, smat)


HY_SUB = 256


def _hy_conv_kernel(u0_ref, u1_ref, u2_ref, sw0_ref, sw1_ref, sw2_ref, sb0_ref, sb1_ref, sb2_ref,
                    fb_ref, hre_ref, him_ref, hny_ref, c_ref, s_ref, o_ref, cb_ref, sb_ref):
    seq, dc = u0_ref.shape

    @pl.when((pl.program_id(0) == 0) & (pl.program_id(1) == 0))
    def _():
        cb_ref[...] = c_ref[...].astype(BF)
        sb_ref[...] = s_ref[...].astype(BF)

    row = lax.broadcasted_iota(jnp.int32, (seq, HY_SUB), 0)
    alt = jnp.where((row & 1) == 0, 1.0, -1.0)

    def sub_tile(cols):
        def short_conv(u_ref, w_ref, b_ref):
            u = u_ref[:, cols]
            prev = jnp.where(row == 0, 0.0, pltpu.roll(u, 1, 0))
            nxt = jnp.where(row == seq - 1, 0.0, pltpu.roll(u, seq - 1, 0))
            return prev * w_ref[0:1, cols] + u * w_ref[1:2, cols] + nxt * w_ref[2:3, cols] + b_ref[:, cols]

        z = short_conv(u0_ref, sw0_ref, sb0_ref)
        gates = (short_conv(u1_ref, sw1_ref, sb1_ref), short_conv(u2_ref, sw2_ref, sb2_ref))
        yield
        for o in range(HY_ORDER):
            zb = z.astype(BF)
            zc, zs = _dot(cb_ref[...], zb), _dot(sb_ref[...], zb)
            yield
            hre, him = hre_ref[o, :, cols], him_ref[o, :, cols]
            p_re = (zc * hre + zs * him).astype(BF)
            p_im = (zc * him - zs * hre).astype(BF)
            y = _dot(cb_ref[...], p_re) - _dot(sb_ref[...], p_im)
            yield
            nyq = jnp.sum(alt * z, axis=0, keepdims=True) * hny_ref[o, :, cols]
            z = gates[o] * (y + alt * nyq + z * fb_ref[o:o + 1, cols])
        o_ref[:, cols] = z.astype(BF)

    _lockstep(sub_tile(slice(j * HY_SUB, (j + 1) * HY_SUB)) for j in range(dc // HY_SUB))


def _hy_conv(u, short_w, short_b, filter_bias, hre, him, hny, cmat, smat, seq, nbatch, row0, dc):
    nj = D // dc
    r0 = row0 // seq

    def part(p):
        return pl.BlockSpec((seq, dc), lambda j, b: (r0 + b, p * nj + j))

    def vec(rows, p):
        return pl.BlockSpec((rows, dc), lambda j, b: (0, p * nj + j))

    in_specs = ([part(p) for p in range(3)] + [vec(3, p) for p in range(3)] + [vec(1, p) for p in range(3)]
                + [pl.BlockSpec((HY_ORDER, dc), lambda j, b: (0, j)),
                   pl.BlockSpec((HY_ORDER, seq, dc), lambda j, b: (0, 0, j), pipeline_mode=pl.Buffered(1)),
                   pl.BlockSpec((HY_ORDER, seq, dc), lambda j, b: (0, 0, j), pipeline_mode=pl.Buffered(1)),
                   pl.BlockSpec((HY_ORDER, 1, dc), lambda j, b: (0, 0, j)),
                   _const_spec((seq, seq)), _const_spec((seq, seq))])
    return pl.pallas_call(
        _hy_conv_kernel,
        grid=(nj, nbatch),
        in_specs=in_specs,
        out_specs=pl.BlockSpec((seq, dc), lambda j, b: (b, j)),
        out_shape=jax.ShapeDtypeStruct((nbatch * seq, D), BF),
        scratch_shapes=[pltpu.VMEM((seq, seq), BF), pltpu.VMEM((seq, seq), BF)],
        compiler_params=_cparams(2),
        name=f"hy_conv_{seq}",
    )(u, u, u, short_w, short_w, short_w, short_b, short_b, short_b, filter_bias, hre, him, hny, cmat, smat)


def _dft_tables(seq):
    k = np.arange(seq, dtype=np.int64)
    ang = np.pi * ((k[:, None] * k[None, :]) % (2 * seq)) / seq
    return jnp.asarray(np.cos(ang), F32), jnp.asarray(np.sin(ang), F32)


def _hy_embedding(seq):
    t = np.arange(seq, dtype=np.float32) / np.float32(seq)
    ang = (2.0 * math.pi) * t[:, None] * np.arange(1, HY_BANDS + 1, dtype=np.float32)
    emb = np.concatenate([t[:, None], np.cos(ang), np.sin(ang)], axis=-1).astype(np.float32)
    return jnp.asarray(np.pad(emb, ((0, 0), (0, HY_EMB_PAD - HY_EMB))))


def _post_kernel(*refs, split_x, split_out, tm):
    oc_ref, ol_ref = refs[0:2]
    x_refs, refs = (refs[2:4], refs[4:]) if split_x else (refs[2:3], refs[3:])
    mod_ref, wo_ref, g1_ref, b1_ref, w1c_ref, w2c_ref, g2_ref, b2_ref = refs[0:8]
    outs, (w1_ref, w2_ref, h_ref, acc_ref) = refs[8:-4], refs[-4:]
    x1_ref = outs[0]
    step = pl.program_id(0)
    is_lat = _is_lat(tm, N_FF_CHUNKS - 1)
    nsub = tm // SUB_POST
    per = MLP_CHUNK // FF_CHUNK

    def rows(j):
        return slice(j * SUB_POST, (j + 1) * SUB_POST)

    def pick(c_ref, l_ref, j):
        return jnp.where(is_lat, l_ref[rows(j), :], c_ref[rows(j), :])

    def norm1(j):
        a = _dot(pick(oc_ref, ol_ref, j), wo_ref[...])
        x = pick(x_refs[0], x_refs[1], j) if split_x else x_refs[0][rows(j), :]
        x1 = _layer_norm(DN_ALPHA * x + mod_ref[2:3, :] * a, g1_ref[...], b1_ref[...])
        return x1, _modulate(x1, mod_ref, 3, 4)

    def norm2(x1, acc):
        return _layer_norm(DN_ALPHA * x1 + mod_ref[5:6, :] * acc, g2_ref[...], b2_ref[...])

    def mlp_chunk(h, c):
        a = jnp.concatenate([_dot(h, w1_ref[per * c + i]) for i in range(per)], axis=1)
        a = jnp.maximum(a, 0.0)
        return _dot((a * a).astype(BF), w2_ref[c])

    def write_branched(ys):
        yc_ref, yl_ref = outs

        @pl.when(jnp.logical_not(is_lat))
        def _():
            for j, y in enumerate(ys):
                yc_ref[rows(j), :] = y

        @pl.when(is_lat)
        def _():
            for j, y in enumerate(ys):
                yl_ref[rows(j), :] = y

    @pl.when(step < N_FF_CHUNKS)
    def _():
        w2_rows = pl.ds(pl.multiple_of((step % per) * FF_CHUNK, FF_CHUNK), FF_CHUNK)
        w1_ref[step] = w1c_ref[...].astype(BF)
        w2_ref[step // per, w2_rows, :] = w2c_ref[...].astype(BF)

        @pl.when(step == 0)
        def _():
            for j in range(nsub):
                x1_ref[rows(j), :], h_ref[rows(j), :] = norm1(j)
            acc_ref[...] = jnp.zeros_like(acc_ref)

        a = jnp.maximum(_dot(h_ref[...], w1_ref[step]), 0.0)
        acc_ref[...] += _dot((a * a).astype(BF), w2_ref[step // per, w2_rows, :])

        @pl.when(step == N_FF_CHUNKS - 1)
        def _():
            ys = [norm2(x1_ref[rows(j), :], acc_ref[rows(j), :]) for j in range(nsub)]
            if split_out:
                write_branched(ys)
            else:
                for j, y in enumerate(ys):
                    outs[0][rows(j), :] = y

    def token_tile():
        ys = []
        cur = norm1(0)
        prev = None
        for j in range(nsub):
            x1, h = cur
            acc = mlp_chunk(h, 0)
            if j + 1 < nsub:
                cur = norm1(j + 1)
            if prev is not None:
                ys.append(norm2(*prev))
                if not split_out:
                    outs[0][rows(j - 1), :] = ys[-1]
            for c in range(1, D_FF // MLP_CHUNK):
                acc = acc + mlp_chunk(h, c)
            prev = (x1, acc)
        ys.append(norm2(*prev))
        if split_out:
            write_branched(ys)
        else:
            outs[0][rows(nsub - 1), :] = ys[-1]

    pl.when(step >= N_FF_CHUNKS)(token_tile)


def _post(o_ctx, o_lat, xs, mods, layer, w_o, g1, b1, w1, w2, g2, b2, split_out):
    tm, off = TM_POST, N_FF_CHUNKS - 1
    split_x = len(xs) == 2
    x_specs = [_ctx_spec(D, tm, off), _lat_spec(D, tm, off)] if split_x else [_tok_spec(D, tm, off)]
    vec = _const_spec((1, D))
    if split_out:
        out_specs = [_ctx_spec(D, tm, off), _lat_spec(D, tm, off)]
        out_shape = [jax.ShapeDtypeStruct((T_CTX, D), F32), jax.ShapeDtypeStruct((T_LAT, D), F32)]
    else:
        out_specs = _tok_spec(D, tm, off)
        out_shape = jax.ShapeDtypeStruct((T, D), F32)

    def chunk(i):
        return jnp.minimum(i, N_FF_CHUNKS - 1)

    return pl.pallas_call(
        functools.partial(_post_kernel, split_x=split_x, split_out=split_out, tm=tm),
        grid=(off + T // tm,),
        in_specs=[_ctx_spec(D, tm, off), _lat_spec(D, tm, off)] + x_specs + [
            _mod_spec(layer, tm, off), _const_spec((D, D)), vec, vec,
            pl.BlockSpec((None, D, FF_CHUNK), lambda i: (layer, 0, chunk(i))),
            pl.BlockSpec((None, FF_CHUNK, D), lambda i: (layer, chunk(i), 0)), vec, vec],
        out_specs=out_specs,
        out_shape=out_shape,
        scratch_shapes=[pltpu.VMEM((N_FF_CHUNKS, D, FF_CHUNK), BF), pltpu.VMEM((D_FF // MLP_CHUNK, MLP_CHUNK, D), BF),
                        pltpu.VMEM((tm, D), BF), pltpu.VMEM((tm, D), F32)],
        compiler_params=_cparams(1, POST_VMEM_LIMIT),
        name=f"post_l{layer}",
    )(o_ctx, o_lat, *xs, mods, w_o, g1, b1, w1, w2, g2, b2)


def _rope_tables():
    n = HEAD_DIM // 4
    pos = np.arange(DEC_SEQ)
    inv = (np.float32(ROPE_BASE) ** (-np.arange(n, dtype=np.float32) / np.float32(n))).astype(np.float32)
    ang_r = ((pos // GRID_W).astype(np.float32)[:, None] * inv).astype(np.float32)
    ang_c = ((pos % GRID_W).astype(np.float32)[:, None] * inv).astype(np.float32)
    cr, sr, cc, sc = np.cos(ang_r), np.sin(ang_r), np.cos(ang_c), np.sin(ang_c)
    a = np.tile(np.concatenate([cr, cr, cc, cc], axis=-1), (1, D // HEAD_DIM))
    b = np.tile(np.concatenate([-sr, sr, -sc, sc], axis=-1), (1, D // HEAD_DIM))
    a = np.concatenate([a, np.ones((TM, D), np.float32)], axis=0)
    b = np.concatenate([b, np.zeros((TM, D), np.float32)], axis=0)
    return jnp.asarray(a, F32), jnp.asarray(b, F32)


def kernel(x_prompt, x_sample, c, cache_da_k, cache_da_v, cache_na_k, cache_na_v, cache_gq_k, cache_gq_v, c_ctx, ada_w, ada_b, ln_g, ln_b, mlp_w1, mlp_w2, da_w_qkv, da_w_o, da_lambda, da_subln_g, na_w_qkv, na_w_o, na_rel_bias, gq_w_qkv, gq_w_o, gq_q_norm, gq_k_norm, hy_w_in, hy_short_w, hy_short_b, hy_ffn_w1, hy_ffn_b1, hy_ffn_w2, hy_ffn_b2, hy_ffn_freq, hy_ffn_w3, hy_log_decay, hy_filter_bias, hy_w_o):
    cvec = jnp.concatenate([c_ctx[None, :], c, jnp.zeros((MOD_ROWS - 1 - DEC_BATCH, D), F32)], axis=0)
    mods = _mods(cvec, ada_w, ada_b)
    rope_a, rope_b = _rope_tables()

    def finish(o_ctx, o_lat, xs, layer, w_o, split_out=False):
        return _post(o_ctx, o_lat, xs, mods, layer, w_o.astype(BF), ln_g[layer, 0][None], ln_b[layer, 0][None],
                     mlp_w1, mlp_w2, ln_g[layer, 1][None], ln_b[layer, 1][None], split_out)

    xs = (x_prompt.reshape(T_CTX, D), x_sample.reshape(T_LAT, D))
    qb, kb, vb, ks, vs = _da_proj(*xs, mods, 0, da_w_qkv[0].astype(BF), rope_a, rope_b)
    state_da_k = ks.reshape(BATCH, 1, SEQ, DA_HEADS, 2 * HEAD_DIM)
    state_da_v = vs.reshape(BATCH, 1, SEQ, DA_HEADS, 2 * HEAD_DIM)
    o_ctx, o_lat = _da_attention(qb, kb, vb, cache_da_k, cache_da_v, da_lambda[0], da_subln_g[0][None], 0)
    x = finish(o_ctx, o_lat, xs, 0, da_w_o[0])

    qb, kb, vb, ks, vs = _na_proj(x, mods, 1, na_w_qkv[0].astype(BF))
    state_na_k, state_na_v = _untranspose_state(ks, NA_HEADS), _untranspose_state(vs, NA_HEADS)
    onehot, neg, mask = _na_constants()
    bias_tab = _na_bias_table(na_rel_bias[0], onehot, neg)
    o_ctx = _na_ctx_attention(qb, kb, vb)
    o_lat = _na_lat_attention(qb, kb, vb, _features_major(cache_na_k), _features_major(cache_na_v), bias_tab, mask)
    x = finish(o_ctx, o_lat, (x,), 1, na_w_o[0])

    g_mat = jnp.asarray(np.kron(np.eye(GN_BLOCK // HEAD_DIM), np.full((HEAD_DIM, HEAD_DIM), 1.0 / HEAD_DIM)), BF)
    qb, kb, vb, ks, vs = _gq_proj(x, mods, 2, gq_w_qkv[0].astype(BF), g_mat,
                                  jnp.tile(gq_q_norm[0], GQ_HEADS)[None], jnp.tile(gq_k_norm[0], GQ_KV_HEADS)[None],
                                  rope_a, rope_b)
    state_gq_k, state_gq_v = _untranspose_state(ks, GQ_KV_HEADS), _untranspose_state(vs, GQ_KV_HEADS)
    o_ctx, o_lat = _gq_attention(qb, kb, vb, _features_major(cache_gq_k), _features_major(cache_gq_v))
    x = finish(o_ctx, o_lat, (x,), 2, gq_w_o[0])

    u = _hy_proj(x, mods, 3, hy_w_in[0].astype(BF))
    w1 = jnp.pad(hy_ffn_w1[0], ((0, HY_EMB_PAD - HY_EMB), (0, 0)))
    zs = []
    for seq, nbatch, row0, dc in ((SEQ, BATCH, 0, D), (DEC_SEQ, DEC_BATCH, T_CTX, 512)):
        cmat, smat = _dft_tables(seq)
        hre, him, hny = _hy_filter(seq, _hy_embedding(seq), w1, hy_ffn_b1[0][None], hy_ffn_w2[0], hy_ffn_b2[0][None],
                                   hy_ffn_freq[0][None], hy_ffn_w3[0], hy_log_decay[0][None], cmat, smat)
        zs.append(_hy_conv(u, hy_short_w[0], hy_short_b[0][None], hy_filter_bias[0], hre, him, hny, cmat, smat,
                           seq, nbatch, row0, dc))
    y_ctx, y_lat = finish(zs[0], zs[1], (x,), 3, hy_w_o[0], split_out=True)

    return (y_ctx.reshape(BATCH, SEQ, D), y_lat.reshape(DEC_BATCH, DEC_SEQ, D),
            state_da_k, state_da_v, state_na_k, state_na_v, state_gq_k, state_gq_v)
```

```python
import functools
import math

import numpy as np
import jax
import jax.numpy as jnp
from jax import lax
from jax.experimental import pallas as pl
from jax.experimental.pallas import tpu as pltpu

F32 = jnp.float32
BF = jnp.bfloat16

D = 1024
BATCH = 16
SEQ = 256
DEC_BATCH = 8
DEC_SEQ = 1024
PAST = 256
DEPTH = 4
GRID_W = 64
GRID_ROWS = DEC_SEQ // GRID_W
D_FF = 4 * D
T_CTX = BATCH * SEQ
T_LAT = DEC_BATCH * DEC_SEQ
T = T_CTX + T_LAT
HEAD_DIM = 64
ATT_SCALE = HEAD_DIM ** -0.5
LOG2E = math.log2(math.e)
Q_SCALE = ATT_SCALE * LOG2E
DA_HEADS = 8
NA_HEADS = 16
NA_WIN_ROWS = 8
NA_WIN_COLS = 16
GQ_HEADS = 16
GQ_KV_HEADS = 4
HY_ORDER = 2
HY_BANDS = 16
HY_EMB = 1 + 2 * HY_BANDS
HY_EMB_PAD = 40
HY_FFN = 64
ROPE_BASE = 10000.0
LN_EPS = 1e-5
RMS_EPS = 1e-6
DN_ALPHA = (2 * DEPTH) ** 0.25
NEG_INF = -1e30

LANES = 128
TM = 512
TM_POST = 512
FF_CHUNK = 512
MLP_CHUNK = 1024
N_FF_CHUNKS = D_FF // FF_CHUNK
SUB_POST = 256
N_CTX_TILES = T_CTX // TM
N_TILES = T // TM
TQ = 512
CTX_BATCHES_PER_STEP = 4
CTX_ROWS = CTX_BATCHES_PER_STEP * SEQ
MOD_ROWS = 16
VMEM_LIMIT = 56 * 1024 * 1024
POST_VMEM_LIMIT = 58 * 1024 * 1024


def _cparams(n_axes, vmem_limit=VMEM_LIMIT):
    return pltpu.CompilerParams(dimension_semantics=("arbitrary",) * n_axes,
                                vmem_limit_bytes=vmem_limit)


def _dot(a, b):
    return jnp.dot(a, b, preferred_element_type=F32)


def _dot_nt(a, b):
    return lax.dot_general(a, b, (((1,), (1,)), ((), ())), preferred_element_type=F32)


def _const_spec(shape):
    nd = len(shape)
    return pl.BlockSpec(shape, lambda *_: (0,) * nd, pipeline_mode=pl.Buffered(1))


def _mod_spec(layer, tm=TM, off=0):
    nctx = T_CTX // tm

    def row(i):
        t = jnp.maximum(i - off, 0)
        return jnp.where(t < nctx, 0, 1 + (t - nctx) // (DEC_SEQ // tm))

    return pl.BlockSpec((None, None, 6, D), lambda i: (layer, row(i), 0, 0))


def _tok_spec(width, tm=TM, off=0):
    return pl.BlockSpec((tm, width), lambda i: (jnp.maximum(i - off, 0), 0))


def _ctx_spec(width, tm=TM, off=0):
    return pl.BlockSpec((tm, width), lambda i: (jnp.clip(i - off, 0, T_CTX // tm - 1), 0))


def _lat_spec(width, tm=TM, off=0):
    return pl.BlockSpec((tm, width), lambda i: (jnp.maximum(i - off - T_CTX // tm, 0), 0))


def _is_lat(tm=TM, off=0):
    return pl.program_id(0) >= off + T_CTX // tm


def _pick(ctx_ref, lat_ref):
    return jnp.where(_is_lat(), lat_ref[...], ctx_ref[...])


def _layer_norm(r, g, b):
    mu = jnp.mean(r, axis=-1, keepdims=True)
    c = r - mu
    var = jnp.mean(c * c, axis=-1, keepdims=True)
    return c * lax.rsqrt(var + LN_EPS) * g + b


def _mods_kernel(c_ref, w_ref, b_ref, o_ref):
    c = c_ref[...]
    s = (c / (1.0 + jnp.exp(-c))).astype(BF)
    o_ref[...] = _dot(s, w_ref[...].astype(BF)) + b_ref[...]


def _mods(cvec, ada_w, ada_b):
    tn = 1536
    out = pl.pallas_call(
        _mods_kernel,
        grid=(DEPTH, 6 * D // tn),
        in_specs=[pl.BlockSpec((MOD_ROWS, D), lambda l, n: (0, 0)),
                  pl.BlockSpec((None, D, tn), lambda l, n: (l, 0, n)),
                  pl.BlockSpec((None, 1, tn), lambda l, n: (l, 0, n))],
        out_specs=pl.BlockSpec((None, MOD_ROWS, tn), lambda l, n: (l, 0, n)),
        out_shape=jax.ShapeDtypeStruct((DEPTH, MOD_ROWS, 6 * D), F32),
        compiler_params=_cparams(2),
        name="adaln_mods",
    )(cvec, ada_w, ada_b.reshape(DEPTH, 1, 6 * D))
    return out.reshape(DEPTH, MOD_ROWS, 6, D)


def _modulate(x, mod_ref, shift, scale):
    return (x * (1.0 + mod_ref[scale:scale + 1, :]) + mod_ref[shift:shift + 1, :]).astype(BF)


def _rope(x, a, b):
    n = x.shape[1]
    lane = lax.broadcasted_iota(jnp.int32, x.shape, 1)
    partner = jnp.where((lane & 16) == 0, pltpu.roll(x, n - 16, 1), pltpu.roll(x, 16, 1))
    return x * a + partner * b


def _rope_spec(width):
    per = DEC_SEQ // TM
    return pl.BlockSpec((TM, width), lambda i: (jnp.where(i < N_CTX_TILES, per, (i - N_CTX_TILES) % per), 0))


def _lockstep(gens):
    gens = list(gens)
    while gens:
        gens = [g for g in gens if next(g, True) is None]


def _store_state(k, v, ks_ref, vs_ref, transposed):
    @pl.when(jnp.logical_not(_is_lat()))
    def _():
        if not transposed:
            ks_ref[...] = k
            vs_ref[...] = v
        else:
            n = k.shape[1]
            for x, ref in ((k, ks_ref), (v, vs_ref)):
                xt = x.T
                for j in range(TM // SEQ):
                    ref[j * n:(j + 1) * n, :] = xt[:, j * SEQ:(j + 1) * SEQ]


def _qkv_out(nq, nk, transposed_state):
    specs = [_tok_spec(nq), _tok_spec(nk), _tok_spec(nk)]
    shapes = [jax.ShapeDtypeStruct((T, nq), BF), jax.ShapeDtypeStruct((T, nk), BF), jax.ShapeDtypeStruct((T, nk), BF)]
    if transposed_state:
        rows = (TM // SEQ) * nk
        specs += [pl.BlockSpec((rows, SEQ), lambda i: (jnp.minimum(i, N_CTX_TILES - 1), 0))] * 2
        shapes += [jax.ShapeDtypeStruct((BATCH * nk, SEQ), F32)] * 2
    else:
        specs += [_ctx_spec(nk)] * 2
        shapes += [jax.ShapeDtypeStruct((T_CTX, nk), F32)] * 2
    return specs, shapes


def _features_major(cache):
    b, _, past, heads, dh = cache.shape
    return cache.transpose(0, 1, 3, 4, 2).reshape(b, heads * dh, past)


def _untranspose_state(st, heads):
    return st.reshape(BATCH, heads, HEAD_DIM, SEQ).transpose(0, 3, 1, 2)[:, None]


def _da_proj_kernel(xc_ref, xl_ref, mod_ref, w_ref, ra_ref, rb_ref, qb_ref, kb_ref, vb_ref, ks_ref, vs_ref):
    h = _modulate(_pick(xc_ref, xl_ref), mod_ref, 0, 1)
    a, b = ra_ref[...], rb_ref[...]
    q = _dot(h, w_ref[:, 0:D])
    k = _dot(h, w_ref[:, D:2 * D])
    qb_ref[...] = (_rope(q, a, b) * Q_SCALE).astype(BF)
    v = _dot(h, w_ref[:, 2 * D:3 * D])
    kb_ref[...] = _rope(k, a, b).astype(BF)
    vb_ref[...] = v.astype(BF)
    _store_state(k, v, ks_ref, vs_ref, False)


def _da_proj(x_ctx, x_lat, mods, layer, w, rope_a, rope_b):
    specs, shapes = _qkv_out(D, D, False)
    return pl.pallas_call(
        _da_proj_kernel,
        grid=(N_TILES,),
        in_specs=[_ctx_spec(D), _lat_spec(D), _mod_spec(layer), _const_spec((D, 3 * D)),
                  _rope_spec(D), _rope_spec(D)],
        out_specs=specs, out_shape=shapes,
        compiler_params=_cparams(1),
        name=f"da_proj_l{layer}",
    )(x_ctx, x_lat, mods, w, rope_a, rope_b)


def _na_proj_kernel(x_ref, mod_ref, w_ref, qb_ref, kb_ref, vb_ref, ks_ref, vs_ref):
    h = _modulate(x_ref[...], mod_ref, 0, 1)
    q = _dot(h, w_ref[:, 0:D])
    k = _dot(h, w_ref[:, D:2 * D])
    qb_ref[...] = (q * Q_SCALE).astype(BF)
    v = _dot(h, w_ref[:, 2 * D:3 * D])
    kb_ref[...] = k.astype(BF)
    vb_ref[...] = v.astype(BF)
    _store_state(k, v, ks_ref, vs_ref, True)


def _na_proj(x, mods, layer, w):
    specs, shapes = _qkv_out(D, D, True)
    return pl.pallas_call(
        _na_proj_kernel,
        grid=(N_TILES,),
        in_specs=[_tok_spec(D), _mod_spec(layer), _const_spec((D, 3 * D))],
        out_specs=specs, out_shape=shapes,
        compiler_params=_cparams(1),
        name=f"na_proj_l{layer}",
    )(x, mods, w)


GN_BLOCK = 256


def _head_rms(x, g_ref, gain):
    x2 = x * x
    hi = x2.astype(BF)
    lo = (x2 - hi.astype(F32)).astype(BF)
    g = g_ref[...]
    ms = jnp.concatenate(
        [_dot(hi[:, j:j + GN_BLOCK], g) + _dot(lo[:, j:j + GN_BLOCK], g) for j in range(0, x.shape[1], GN_BLOCK)],
        axis=1)
    return x * lax.rsqrt(ms + RMS_EPS) * gain


def _gq_proj_kernel(x_ref, mod_ref, w_ref, g_ref, qn_ref, kn_ref, ra_ref, rb_ref,
                    qb_ref, kb_ref, vb_ref, ks_ref, vs_ref):
    nq, nk = GQ_HEADS * HEAD_DIM, GQ_KV_HEADS * HEAD_DIM
    h = _modulate(x_ref[...], mod_ref, 0, 1)
    a, b = ra_ref[...], rb_ref[...]
    q = _dot(h, w_ref[:, 0:nq])
    k = _dot(h, w_ref[:, nq:nq + nk])
    v = _dot(h, w_ref[:, nq + nk:nq + 2 * nk])
    k = _head_rms(k, g_ref, kn_ref[...])
    q = _head_rms(q, g_ref, qn_ref[...])
    kb_ref[...] = _rope(k, a[:, 0:nk], b[:, 0:nk]).astype(BF)
    qb_ref[...] = (_rope(q, a, b) * Q_SCALE).astype(BF)
    vb_ref[...] = v.astype(BF)
    _store_state(k, v, ks_ref, vs_ref, True)


def _gq_proj(x, mods, layer, w, g_mat, qn, kn, rope_a, rope_b):
    nq, nk = GQ_HEADS * HEAD_DIM, GQ_KV_HEADS * HEAD_DIM
    specs, shapes = _qkv_out(nq, nk, True)
    return pl.pallas_call(
        _gq_proj_kernel,
        grid=(N_TILES,),
        in_specs=[_tok_spec(D), _mod_spec(layer), _const_spec((D, nq + 2 * nk)),
                  _const_spec((GN_BLOCK, GN_BLOCK)), _const_spec((1, nq)), _const_spec((1, nk)),
                  _rope_spec(D), _rope_spec(D)],
        out_specs=specs, out_shape=shapes,
        compiler_params=_cparams(1),
        name=f"gq_proj_l{layer}",
    )(x, mods, w, g_mat, qn, kn, rope_a, rope_b)


def _hy_proj_kernel(x_ref, mod_ref, w_ref, u_ref):
    h = _modulate(x_ref[...], mod_ref, 0, 1)
    for c in range(HY_ORDER + 1):
        u_ref[:, c * D:(c + 1) * D] = _dot(h, w_ref[:, c * D:(c + 1) * D])


def _hy_proj(x, mods, layer, w):
    n = (HY_ORDER + 1) * D
    return pl.pallas_call(
        _hy_proj_kernel,
        grid=(N_TILES,),
        in_specs=[_tok_spec(D), _mod_spec(layer), _const_spec((D, n))],
        out_specs=_tok_spec(n),
        out_shape=jax.ShapeDtypeStruct((T, n), F32),
        compiler_params=_cparams(1),
        name=f"hy_proj_l{layer}",
    )(x, mods, w)


def _scores(qm, segs):
    return [_dot(qm, seg[0]) if len(seg) == 3 else _dot_nt(qm, seg[0]) for seg in segs]


def _softmax_finish(scores, segs):
    m = scores[0].max(axis=-1, keepdims=True)
    for s in scores[1:]:
        m = jnp.maximum(m, s.max(axis=-1, keepdims=True))
    den = None
    out = None
    for s, seg in zip(scores, segs):
        e = jnp.exp2(s - m)
        d = e.sum(axis=-1, keepdims=True)
        o = _dot_nt(e.astype(BF), seg[1]) if len(seg) == 3 else _dot(e.astype(BF), seg[1])
        den = d if den is None else den + d
        out = o if out is None else out + o
    return out / den


def _stack_halves(q, keep):
    return jnp.concatenate([q * keep[0], q * keep[1]], axis=0)


def _pipelined(jobs, score_fn, finish_fn):
    nxt = score_fn(jobs[0])
    for n, job in enumerate(jobs):
        cur, nxt = nxt, (score_fn(jobs[n + 1]) if n + 1 < len(jobs) else None)
        finish_fn(job, cur)


def _lane_half(shape):
    return lax.broadcasted_iota(jnp.int32, shape, 1) // HEAD_DIM


def _half_keep(half):
    return tuple(jnp.where(half == a, 1.0, 0.0).astype(BF) for a in (0, 1))


def _da_attn_kernel(*refs, has_cache, lam_init):
    if has_cache:
        q_ref, k_ref, v_ref, ck_ref, cv_ref, lam_ref, g_ref, o_ref = refs
    else:
        q_ref, k_ref, v_ref, lam_ref, g_ref, o_ref = refs
    lp = lam_ref[...]
    lam = (jnp.exp(jnp.sum(lp[0:1] * lp[1:2], axis=-1, keepdims=True))
           - jnp.exp(jnp.sum(lp[2:3] * lp[3:4], axis=-1, keepdims=True)) + lam_init)
    gain = g_ref[...] * (1.0 - lam_init)
    w = 2 * HEAD_DIM
    nheads = k_ref.shape[1] // w
    tq = min(TQ, q_ref.shape[0]) if has_cache else SEQ
    keep = _half_keep(_lane_half((tq, w)))
    caches = []
    if has_cache:
        for hd in range(nheads):
            head = pl.program_id(1) * nheads + hd
            caches.append((ck_ref[:, head, :].astype(BF), cv_ref[:, head, :].astype(BF)))

    def seg(hd, t):
        cols = slice(hd * w, (hd + 1) * w)
        if has_cache:
            return [(k_ref[:, cols], v_ref[:, cols]), caches[hd]]
        return [(k_ref[t * tq:(t + 1) * tq, cols], v_ref[t * tq:(t + 1) * tq, cols])]

    jobs = [(hd, t, a) for t in range(q_ref.shape[0] // tq) for hd in range(nheads) for a in (0, 1)]
    first = {}

    def score_fn(job):
        hd, t, a = job
        return _scores(q_ref[t * tq:(t + 1) * tq, hd * w:(hd + 1) * w] * keep[a], seg(hd, t))

    def finish_fn(job, scores):
        hd, t, a = job
        o = _softmax_finish(scores, seg(hd, t))
        if a == 0:
            first[0] = o
            return
        o = first[0] - lam * o
        ms = jnp.mean(o * o, axis=-1, keepdims=True)
        o_ref[t * tq:(t + 1) * tq, hd * w:(hd + 1) * w] = (o * lax.rsqrt(ms + RMS_EPS) * gain).astype(BF)

    _pipelined(jobs, score_fn, finish_fn)


DA_LAT_HEADS_PER_STEP = 2


def _da_attention(qb, kb, vb, cache_k, cache_v, lam_p, subln_g, layer_idx):
    lam_init = 0.8 - 0.6 * math.exp(-0.3 * layer_idx)
    w = 2 * HEAD_DIM
    small = [pl.BlockSpec((4, HEAD_DIM), lambda *_: (0, 0)), pl.BlockSpec((1, w), lambda *_: (0, 0))]
    o_ctx = pl.pallas_call(
        functools.partial(_da_attn_kernel, has_cache=False, lam_init=lam_init),
        grid=(BATCH,),
        in_specs=[pl.BlockSpec((SEQ, D), lambda b: (b, 0))] * 3 + small,
        out_specs=pl.BlockSpec((SEQ, D), lambda b: (b, 0)),
        out_shape=jax.ShapeDtypeStruct((T_CTX, D), BF),
        compiler_params=_cparams(1),
        name="da_attn_ctx",
    )(qb, kb, vb, lam_p, subln_g)
    k0 = T_CTX // DEC_SEQ
    hw = DA_LAT_HEADS_PER_STEP * w
    tok = pl.BlockSpec((DEC_SEQ, hw), lambda b, h: (k0 + b, h))
    c_spec = pl.BlockSpec((None, None, PAST, DA_HEADS, w), lambda b, h: (b, 0, 0, 0, 0))
    o_lat = pl.pallas_call(
        functools.partial(_da_attn_kernel, has_cache=True, lam_init=lam_init),
        grid=(DEC_BATCH, DA_HEADS // DA_LAT_HEADS_PER_STEP),
        in_specs=[tok, tok, tok, c_spec, c_spec] + small,
        out_specs=pl.BlockSpec((DEC_SEQ, hw), lambda b, h: (b, h)),
        out_shape=jax.ShapeDtypeStruct((T_LAT, D), BF),
        compiler_params=_cparams(2),
        name="da_attn_lat",
    )(qb, kb, vb, cache_k, cache_v, lam_p, subln_g)
    return o_ctx, o_lat


def _na_ctx_kernel(q_ref, k_ref, v_ref, o_ref):
    half = _lane_half((SEQ, LANES))
    keep = _half_keep(half)

    def block(job):
        b, p = job
        return slice(b * SEQ, (b + 1) * SEQ), slice(p * LANES, (p + 1) * LANES)

    def seg(job):
        return [(k_ref[block(job)], v_ref[block(job)])]

    def score_fn(job):
        return _scores(_stack_halves(q_ref[block(job)], keep), seg(job))

    def finish_fn(job, scores):
        o = _softmax_finish(scores, seg(job))
        o_ref[block(job)] = jnp.where(half == 0, o[0:SEQ], o[SEQ:2 * SEQ]).astype(BF)

    jobs = [(b, p) for b in range(q_ref.shape[0] // SEQ) for p in range(NA_HEADS // 2)]
    _pipelined(jobs, score_fn, finish_fn)


def _na_ctx_attention(qb, kb, vb):
    spec = pl.BlockSpec((CTX_ROWS, D), lambda b: (b, 0))
    return pl.pallas_call(
        _na_ctx_kernel,
        grid=(BATCH // CTX_BATCHES_PER_STEP,),
        in_specs=[spec] * 3,
        out_specs=spec,
        out_shape=jax.ShapeDtypeStruct((T_CTX, D), BF),
        compiler_params=_cparams(1),
        name="na_attn_ctx",
    )(qb, kb, vb)


NA_TILES = ((0, (0, 2, 4, 6)), (4, (0, 2, 4, 6, 8, 10)), (8, (4, 6, 8, 10, 12, 14)), (12, (8, 10, 12, 14)))
NA_MAX_CHUNKS = 6
NA_BIAS_BLOCKS = 2 * NA_WIN_ROWS - 2


NA_LAT_PAIRS_PER_STEP = 4


def _na_lat_kernel(q_ref, k_ref, v_ref, ck_ref, cv_ref, w_ref, m_ref, o_ref):
    rows = 4 * GRID_W
    half = _lane_half((rows, LANES))
    keep = _half_keep(half)
    caches = [(ck_ref[p * LANES:(p + 1) * LANES, :].astype(BF), cv_ref[p * LANES:(p + 1) * LANES, :].astype(BF))
              for p in range(NA_LAT_PAIRS_PER_STEP)]
    jobs = [(p, i) for p in range(NA_LAT_PAIRS_PER_STEP) for i in range(len(NA_TILES))]

    def key_rows(i):
        chunks = NA_TILES[i][1]
        return slice(chunks[0] * GRID_W, chunks[0] * GRID_W + len(chunks) * LANES)

    def score_fn(job):
        p, i = job
        cols = slice(p * LANES, (p + 1) * LANES)
        r0, chunks = NA_TILES[i]
        qm = _stack_halves(q_ref[i * rows:(i + 1) * rows, cols], keep)
        mask = m_ref[i, :, 0:len(chunks) * LANES]
        bias = jnp.concatenate(
            [jnp.concatenate([w_ref[p, a, (6 - kr + r0) * GRID_W:(6 - kr + r0) * GRID_W + rows, :] for kr in chunks],
                             axis=1) + mask for a in (0, 1)], axis=0)
        return [_dot_nt(qm, k_ref[key_rows(i), cols]) + bias, _dot(qm, caches[p][0])]

    def finish_fn(job, scores):
        p, i = job
        cols = slice(p * LANES, (p + 1) * LANES)
        o = _softmax_finish(scores, [(None, v_ref[key_rows(i), cols]), (None, caches[p][1], True)])
        o_ref[i * rows:(i + 1) * rows, cols] = jnp.where(half == 0, o[0:rows], o[rows:2 * rows]).astype(BF)

    _pipelined(jobs, score_fn, finish_fn)


def _na_lat_attention(qb, kb, vb, cache_k, cache_v, bias_tab, mask_tab):
    k0 = T_CTX // DEC_SEQ
    npair = NA_LAT_PAIRS_PER_STEP
    tok = pl.BlockSpec((DEC_SEQ, npair * LANES), lambda b, p: (k0 + b, p))
    c_spec = pl.BlockSpec((None, npair * LANES, PAST), lambda b, p: (b, p, 0))
    return pl.pallas_call(
        _na_lat_kernel,
        grid=(DEC_BATCH, NA_HEADS // 2 // npair),
        in_specs=[tok, tok, tok, c_spec, c_spec,
                  pl.BlockSpec((npair, 2, NA_BIAS_BLOCKS * GRID_W, LANES), lambda b, p: (p, 0, 0, 0)),
                  _const_spec(mask_tab.shape)],
        out_specs=pl.BlockSpec((DEC_SEQ, npair * LANES), lambda b, p: (b, p)),
        out_shape=jax.ShapeDtypeStruct((T_LAT, D), BF),
        compiler_params=_cparams(2),
        name="na_attn_lat",
    )(qb, kb, vb, cache_k, cache_v, bias_tab, mask_tab)


def _na_bias_kernel(t_ref, r_ref, n_ref, o_ref):
    t = t_ref[...]
    t1 = t.astype(BF)
    r1 = t - t1.astype(F32)
    t2 = r1.astype(BF)
    t3 = (r1 - t2.astype(F32)).astype(BF)
    r = r_ref[...]
    res = (_dot(t1, r) + _dot(t2, r) + _dot(t3, r) + n_ref[...]) * LOG2E
    for qc in range(GRID_W):
        o_ref[pl.ds(qc, t.shape[0], stride=GRID_W), :] = res[:, qc * LANES:(qc + 1) * LANES]


def _na_bias_table(rel_bias, onehot, neg):
    nrel = 2 * NA_WIN_COLS
    idx = 13 - np.arange(NA_BIAS_BLOCKS)[:, None] + np.arange(2)[None, :]
    t = jnp.pad(rel_bias[:, idx, :], ((0, 0), (0, 0), (0, 0), (0, 1)))
    t = t.reshape(NA_HEADS * NA_BIAS_BLOCKS, 2 * nrel)
    n = GRID_W * LANES
    out = pl.pallas_call(
        _na_bias_kernel,
        grid=(1,),
        in_specs=[pl.BlockSpec(t.shape, lambda j: (0, 0)),
                  pl.BlockSpec((2 * nrel, n), lambda j: (0, 0)),
                  pl.BlockSpec((1, n), lambda j: (0, 0))],
        out_specs=pl.BlockSpec((t.shape[0] * GRID_W, LANES), lambda j: (0, 0)),
        out_shape=jax.ShapeDtypeStruct((t.shape[0] * GRID_W, LANES), F32),
        compiler_params=_cparams(1),
        name="na_bias_table",
    )(t, onehot, neg)
    return out.reshape(NA_HEADS // 2, 2, NA_BIAS_BLOCKS * GRID_W, LANES)


def _na_constants():
    nrel = 2 * NA_WIN_COLS
    qc = np.arange(GRID_W)[:, None]
    kc = np.arange(GRID_W)[None, :]
    rel = np.clip(kc - qc, -(NA_WIN_COLS - 1), NA_WIN_COLS - 1) + NA_WIN_COLS - 1
    cs = np.clip(qc - NA_WIN_COLS // 2, 0, GRID_W - NA_WIN_COLS)
    col_in = (kc >= cs) & (kc < cs + NA_WIN_COLS)
    onehot = np.zeros((2, nrel, GRID_W, 2, GRID_W), np.float32)
    for hf in range(2):
        onehot[hf, rel, qc, hf, kc] = 1.0
    neg = np.where(col_in, 0.0, NEG_INF).astype(np.float32)
    neg = np.broadcast_to(neg[:, None, :], (GRID_W, 2, GRID_W)).reshape(1, -1)
    rows = 4 * GRID_W
    mask = np.full((len(NA_TILES), rows, NA_MAX_CHUNKS * LANES), NEG_INF, np.float32)
    kr = min(NA_WIN_ROWS, GRID_ROWS)
    for i, (r0, chunks) in enumerate(NA_TILES):
        qr = r0 + np.arange(rows)[:, None] // GRID_W
        rs = np.clip(qr - kr // 2, 0, GRID_ROWS - kr)
        for c, krow0 in enumerate(chunks):
            krow = krow0 + np.arange(LANES)[None, :] // GRID_W
            mask[i, :, c * LANES:(c + 1) * LANES] = np.where((krow >= rs) & (krow < rs + kr), 0.0, NEG_INF)
    return (jnp.asarray(onehot.reshape(2 * nrel, GRID_W * LANES), BF), jnp.asarray(neg), jnp.asarray(mask))


def _gq_attn_kernel(*refs, has_cache):
    if has_cache:
        q_ref, k_ref, v_ref, ck_ref, cv_ref, o_ref = refs
    else:
        q_ref, k_ref, v_ref, o_ref = refs
    group = GQ_HEADS // GQ_KV_HEADS
    qw = LANES * group
    tq = min(TQ // 2, q_ref.shape[0])
    half = _lane_half((tq, LANES))
    keep = _half_keep(half)
    nkvp = k_ref.shape[1] // LANES
    caches = []
    if has_cache:
        for kvp in range(nkvp):
            kcols = slice(kvp * LANES, (kvp + 1) * LANES)
            caches.append((ck_ref[kcols, :].astype(BF), cv_ref[kcols, :].astype(BF), True))

    def seg(job):
        kvp, t, _ = job
        kcols = slice(kvp * LANES, (kvp + 1) * LANES)
        if has_cache:
            return [(k_ref[:, kcols], v_ref[:, kcols]), caches[kvp]]
        return [(k_ref[t * tq:(t + 1) * tq, kcols], v_ref[t * tq:(t + 1) * tq, kcols])]

    jobs = [(kvp, t, kh) for t in range(q_ref.shape[0] // tq) for kvp in range(nkvp) for kh in (0, 1)]

    def blocks(job):
        kvp, t, kh = job
        for pair in (2 * kh, 2 * kh + 1):
            yield slice(t * tq, (t + 1) * tq), slice(kvp * qw + pair * LANES, kvp * qw + (pair + 1) * LANES)

    def score_fn(job):
        kh = job[2]
        parts = []
        for rows, cols in blocks(job):
            for a in (0, 1):
                qm = q_ref[rows, cols] * keep[a]
                parts.append(qm if a == kh else pltpu.roll(qm.astype(F32), HEAD_DIM, 1).astype(BF))
        return _scores(jnp.concatenate(parts, axis=0), seg(job))

    def finish_fn(job, scores):
        kh = job[2]
        o = _softmax_finish(scores, seg(job))
        for n, (rows, cols) in enumerate(blocks(job)):
            heads = [o[(2 * n + a) * tq:(2 * n + a + 1) * tq] for a in (0, 1)]
            heads = [h if a == kh else pltpu.roll(h, HEAD_DIM, 1) for a, h in enumerate(heads)]
            o_ref[rows, cols] = jnp.where(half == 0, heads[0], heads[1]).astype(BF)

    _pipelined(jobs, score_fn, finish_fn)


GQ_LAT_ROWS = 1024


def _gq_attention(qb, kb, vb, cache_k, cache_v):
    nk = GQ_KV_HEADS * HEAD_DIM
    qw = LANES * (GQ_HEADS // GQ_KV_HEADS)
    npair = GQ_KV_HEADS // 2
    o_ctx = pl.pallas_call(
        functools.partial(_gq_attn_kernel, has_cache=False),
        grid=(BATCH // CTX_BATCHES_PER_STEP,),
        in_specs=[pl.BlockSpec((CTX_ROWS, D), lambda b: (b, 0))] + [pl.BlockSpec((CTX_ROWS, nk), lambda b: (b, 0))] * 2,
        out_specs=pl.BlockSpec((CTX_ROWS, D), lambda b: (b, 0)),
        out_shape=jax.ShapeDtypeStruct((T_CTX, D), BF),
        compiler_params=_cparams(1),
        name="gq_attn_ctx",
    )(qb, kb, vb)
    qt = DEC_SEQ // GQ_LAT_ROWS
    q0, k0 = T_CTX // GQ_LAT_ROWS, T_CTX // DEC_SEQ
    kv_spec = pl.BlockSpec((DEC_SEQ, LANES), lambda b, p, t: (k0 + b, p))
    c_spec = pl.BlockSpec((None, LANES, PAST), lambda b, p, t: (b, p, 0))
    o_lat = pl.pallas_call(
        functools.partial(_gq_attn_kernel, has_cache=True),
        grid=(DEC_BATCH, npair, qt),
        in_specs=[pl.BlockSpec((GQ_LAT_ROWS, qw), lambda b, p, t: (q0 + b * qt + t, p)), kv_spec, kv_spec, c_spec,
                  c_spec],
        out_specs=pl.BlockSpec((GQ_LAT_ROWS, qw), lambda b, p, t: (b * qt + t, p)),
        out_shape=jax.ShapeDtypeStruct((T_LAT, D), BF),
        compiler_params=_cparams(3),
        name="gq_attn_lat",
    )(qb, kb, vb, cache_k, cache_v)
    return o_ctx, o_lat


def _dot_3pass(a, b):
    ah, bh = a.astype(BF), b.astype(BF)
    al, bl = (a - ah.astype(F32)).astype(BF), (b - bh.astype(F32)).astype(BF)
    return _dot(ah, bh) + _dot(ah, bl) + _dot(al, bh)


def _hy_filter_kernel(emb_ref, w1_ref, b1_ref, w2_ref, b2_ref, fr_ref, w3f_ref, w3b_ref, ldf_ref, ldb_ref,
                      c_ref, s_ref, hre_ref, him_ref, hny_ref, hid_ref, cb_ref, sb_ref):
    seq = emb_ref.shape[0]

    @pl.when((pl.program_id(0) == 0) & (pl.program_id(1) == 0))
    def _():
        hp = lax.Precision.HIGHEST
        fr = fr_ref[...]
        hid = jnp.sin(fr * (jnp.dot(emb_ref[...], w1_ref[...], precision=hp, preferred_element_type=F32)
                            + b1_ref[...]))
        hid_ref[...] = jnp.sin(fr * (jnp.dot(hid, w2_ref[...], precision=hp, preferred_element_type=F32)
                                     + b2_ref[...]))
        cb_ref[...] = c_ref[...].astype(BF)
        sb_ref[...] = s_ref[...].astype(BF)

    hid = hid_ref[...]
    t = emb_ref[:, 0:1]
    fwd = _dot_3pass(hid, w3f_ref[...]) * jnp.exp(-jnp.exp(ldf_ref[...]) * t)
    bwd = _dot_3pass(hid, w3b_ref[...]) * jnp.exp(-jnp.exp(ldb_ref[...]) * t)
    row = lax.broadcasted_iota(jnp.int32, fwd.shape, 0)
    bwd = jnp.where(row == 0, 0.0, bwd)
    even = fwd + bwd
    odd = bwd - fwd
    wk = jnp.where(row == 0, 0.5 / seq, 1.0 / seq)
    hre_ref[...] = _dot(cb_ref[...], even.astype(BF)) * wk
    him_ref[...] = _dot(sb_ref[...], odd.astype(BF)) * wk
    alt = jnp.where((row & 1) == 0, 1.0, -1.0)
    hny_ref[...] = jnp.sum(alt * even, axis=0, keepdims=True) * (0.5 / seq)


def _hy_filter(seq, emb, w1, b1, w2, b2, freq, w3, log_decay, cmat, smat):
    dc = 512
    nj = D // dc
    small = [_const_spec(a.shape) for a in (emb, w1, b1, w2, b2, freq)]
    return pl.pallas_call(
        _hy_filter_kernel,
        grid=(HY_ORDER, nj),
        in_specs=small + [pl.BlockSpec((HY_FFN, dc), lambda o, j: (0, (2 * o) * nj + j)),
                          pl.BlockSpec((HY_FFN, dc), lambda o, j: (0, (2 * o + 1) * nj + j)),
                          pl.BlockSpec((1, dc), lambda o, j: (0, (2 * o) * nj + j)),
                          pl.BlockSpec((1, dc), lambda o, j: (0, (2 * o + 1) * nj + j)),
                          _const_spec((seq, seq)), _const_spec((seq, seq))],
        out_specs=[pl.BlockSpec((None, seq, dc), lambda o, j: (o, 0, j)),
                   pl.BlockSpec((None, seq, dc), lambda o, j: (o, 0, j)),
                   pl.BlockSpec((None, 1, dc), lambda o, j: (o, 0, j))],
        out_shape=[jax.ShapeDtypeStruct((HY_ORDER, seq, D), F32), jax.ShapeDtypeStruct((HY_ORDER, seq, D), F32),
                   jax.ShapeDtypeStruct((HY_ORDER, 1, D), F32)],
        scratch_shapes=[pltpu.VMEM((seq, HY_FFN), F32), pltpu.VMEM((seq, seq), BF), pltpu.VMEM((seq, seq), BF)],
        compiler_params=_cparams(2),
        name=f"hy_filter_{seq}",
    )(emb, w1, b1, w2, b2, freq, w3, w3, log_decay, log_decay, cmat, smat)


HY_SUB = 256


def _hy_conv_kernel(u0_ref, u1_ref, u2_ref, sw0_ref, sw1_ref, sw2_ref, sb0_ref, sb1_ref, sb2_ref,
                    fb_ref, hre_ref, him_ref, hny_ref, c_ref, s_ref, o_ref, cb_ref, sb_ref):
    seq, dc = u0_ref.shape

    @pl.when((pl.program_id(0) == 0) & (pl.program_id(1) == 0))
    def _():
        cb_ref[...] = c_ref[...].astype(BF)
        sb_ref[...] = s_ref[...].astype(BF)

    row = lax.broadcasted_iota(jnp.int32, (seq, HY_SUB), 0)
    alt = jnp.where((row & 1) == 0, 1.0, -1.0)

    def sub_tile(cols):
        def short_conv(u_ref, w_ref, b_ref):
            u = u_ref[:, cols]
            prev = jnp.where(row == 0, 0.0, pltpu.roll(u, 1, 0))
            nxt = jnp.where(row == seq - 1, 0.0, pltpu.roll(u, seq - 1, 0))
            return prev * w_ref[0:1, cols] + u * w_ref[1:2, cols] + nxt * w_ref[2:3, cols] + b_ref[:, cols]

        z = short_conv(u0_ref, sw0_ref, sb0_ref)
        gates = (short_conv(u1_ref, sw1_ref, sb1_ref), short_conv(u2_ref, sw2_ref, sb2_ref))
        yield
        for o in range(HY_ORDER):
            zb = z.astype(BF)
            zc, zs = _dot(cb_ref[...], zb), _dot(sb_ref[...], zb)
            yield
            hre, him = hre_ref[o, :, cols], him_ref[o, :, cols]
            p_re = (zc * hre + zs * him).astype(BF)
            p_im = (zc * him - zs * hre).astype(BF)
            y = _dot(cb_ref[...], p_re) - _dot(sb_ref[...], p_im)
            yield
            nyq = jnp.sum(alt * z, axis=0, keepdims=True) * hny_ref[o, :, cols]
            z = gates[o] * (y + alt * nyq + z * fb_ref[o:o + 1, cols])
        o_ref[:, cols] = z.astype(BF)

    _lockstep(sub_tile(slice(j * HY_SUB, (j + 1) * HY_SUB)) for j in range(dc // HY_SUB))


def _hy_conv(u, short_w, short_b, filter_bias, hre, him, hny, cmat, smat, seq, nbatch, row0, dc):
    nj = D // dc
    r0 = row0 // seq

    def part(p):
        return pl.BlockSpec((seq, dc), lambda j, b: (r0 + b, p * nj + j))

    def vec(rows, p):
        return pl.BlockSpec((rows, dc), lambda j, b: (0, p * nj + j))

    in_specs = ([part(p) for p in range(3)] + [vec(3, p) for p in range(3)] + [vec(1, p) for p in range(3)]
                + [pl.BlockSpec((HY_ORDER, dc), lambda j, b: (0, j)),
                   pl.BlockSpec((HY_ORDER, seq, dc), lambda j, b: (0, 0, j), pipeline_mode=pl.Buffered(1)),
                   pl.BlockSpec((HY_ORDER, seq, dc), lambda j, b: (0, 0, j), pipeline_mode=pl.Buffered(1)),
                   pl.BlockSpec((HY_ORDER, 1, dc), lambda j, b: (0, 0, j)),
                   _const_spec((seq, seq)), _const_spec((seq, seq))])
    return pl.pallas_call(
        _hy_conv_kernel,
        grid=(nj, nbatch),
        in_specs=in_specs,
        out_specs=pl.BlockSpec((seq, dc), lambda j, b: (b, j)),
        out_shape=jax.ShapeDtypeStruct((nbatch * seq, D), BF),
        scratch_shapes=[pltpu.VMEM((seq, seq), BF), pltpu.VMEM((seq, seq), BF)],
        compiler_params=_cparams(2),
        name=f"hy_conv_{seq}",
    )(u, u, u, short_w, short_w, short_w, short_b, short_b, short_b, filter_bias, hre, him, hny, cmat, smat)


def _dft_tables(seq):
    k = np.arange(seq, dtype=np.int64)
    ang = np.pi * ((k[:, None] * k[None, :]) % (2 * seq)) / seq
    return jnp.asarray(np.cos(ang), F32), jnp.asarray(np.sin(ang), F32)


def _hy_embedding(seq):
    t = np.arange(seq, dtype=np.float32) / np.float32(seq)
    ang = (2.0 * math.pi) * t[:, None] * np.arange(1, HY_BANDS + 1, dtype=np.float32)
    emb = np.concatenate([t[:, None], np.cos(ang), np.sin(ang)], axis=-1).astype(np.float32)
    return jnp.asarray(np.pad(emb, ((0, 0), (0, HY_EMB_PAD - HY_EMB))))


def _post_kernel(*refs, split_x, split_out, tm):
    oc_ref, ol_ref = refs[0:2]
    x_refs, refs = (refs[2:4], refs[4:]) if split_x else (refs[2:3], refs[3:])
    mod_ref, wo_ref, g1_ref, b1_ref, w1c_ref, w2c_ref, g2_ref, b2_ref = refs[0:8]
    outs, (w1_ref, w2_ref, h_ref, acc_ref) = refs[8:-4], refs[-4:]
    x1_ref = outs[0]
    step = pl.program_id(0)
    is_lat = _is_lat(tm, N_FF_CHUNKS - 1)
    nsub = tm // SUB_POST
    per = MLP_CHUNK // FF_CHUNK

    def rows(j):
        return slice(j * SUB_POST, (j + 1) * SUB_POST)

    def pick(c_ref, l_ref, j):
        return jnp.where(is_lat, l_ref[rows(j), :], c_ref[rows(j), :])

    def norm1(j):
        a = _dot(pick(oc_ref, ol_ref, j), wo_ref[...])
        x = pick(x_refs[0], x_refs[1], j) if split_x else x_refs[0][rows(j), :]
        x1 = _layer_norm(DN_ALPHA * x + mod_ref[2:3, :] * a, g1_ref[...], b1_ref[...])
        return x1, _modulate(x1, mod_ref, 3, 4)

    def norm2(x1, acc):
        return _layer_norm(DN_ALPHA * x1 + mod_ref[5:6, :] * acc, g2_ref[...], b2_ref[...])

    def mlp_chunk(h, c):
        a = jnp.concatenate([_dot(h, w1_ref[per * c + i]) for i in range(per)], axis=1)
        a = jnp.maximum(a, 0.0)
        return _dot((a * a).astype(BF), w2_ref[c])

    def write_branched(ys):
        yc_ref, yl_ref = outs

        @pl.when(jnp.logical_not(is_lat))
        def _():
            for j, y in enumerate(ys):
                yc_ref[rows(j), :] = y

        @pl.when(is_lat)
        def _():
            for j, y in enumerate(ys):
                yl_ref[rows(j), :] = y

    @pl.when(step < N_FF_CHUNKS)
    def _():
        w2_rows = pl.ds(pl.multiple_of((step % per) * FF_CHUNK, FF_CHUNK), FF_CHUNK)
        w1_ref[step] = w1c_ref[...].astype(BF)
        w2_ref[step // per, w2_rows, :] = w2c_ref[...].astype(BF)

        @pl.when(step == 0)
        def _():
            for j in range(nsub):
                x1_ref[rows(j), :], h_ref[rows(j), :] = norm1(j)
            acc_ref[...] = jnp.zeros_like(acc_ref)

        a = jnp.maximum(_dot(h_ref[...], w1_ref[step]), 0.0)
        acc_ref[...] += _dot((a * a).astype(BF), w2_ref[step // per, w2_rows, :])

        @pl.when(step == N_FF_CHUNKS - 1)
        def _():
            ys = [norm2(x1_ref[rows(j), :], acc_ref[rows(j), :]) for j in range(nsub)]
            if split_out:
                write_branched(ys)
            else:
                for j, y in enumerate(ys):
                    outs[0][rows(j), :] = y

    def token_tile():
        ys = []
        cur = norm1(0)
        prev = None
        for j in range(nsub):
            x1, h = cur
            acc = mlp_chunk(h, 0)
            if j + 1 < nsub:
                cur = norm1(j + 1)
            if prev is not None:
                ys.append(norm2(*prev))
                if not split_out:
                    outs[0][rows(j - 1), :] = ys[-1]
            for c in range(1, D_FF // MLP_CHUNK):
                acc = acc + mlp_chunk(h, c)
            prev = (x1, acc)
        ys.append(norm2(*prev))
        if split_out:
            write_branched(ys)
        else:
            outs[0][rows(nsub - 1), :] = ys[-1]

    pl.when(step >= N_FF_CHUNKS)(token_tile)


def _post(o_ctx, o_lat, xs, mods, layer, w_o, g1, b1, w1, w2, g2, b2, split_out):
    tm, off = TM_POST, N_FF_CHUNKS - 1
    split_x = len(xs) == 2
    x_specs = [_ctx_spec(D, tm, off), _lat_spec(D, tm, off)] if split_x else [_tok_spec(D, tm, off)]
    vec = _const_spec((1, D))
    if split_out:
        out_specs = [_ctx_spec(D, tm, off), _lat_spec(D, tm, off)]
        out_shape = [jax.ShapeDtypeStruct((T_CTX, D), F32), jax.ShapeDtypeStruct((T_LAT, D), F32)]
    else:
        out_specs = _tok_spec(D, tm, off)
        out_shape = jax.ShapeDtypeStruct((T, D), F32)

    def chunk(i):
        return jnp.minimum(i, N_FF_CHUNKS - 1)

    return pl.pallas_call(
        functools.partial(_post_kernel, split_x=split_x, split_out=split_out, tm=tm),
        grid=(off + T // tm,),
        in_specs=[_ctx_spec(D, tm, off), _lat_spec(D, tm, off)] + x_specs + [
            _mod_spec(layer, tm, off), _const_spec((D, D)), vec, vec,
            pl.BlockSpec((None, D, FF_CHUNK), lambda i: (layer, 0, chunk(i))),
            pl.BlockSpec((None, FF_CHUNK, D), lambda i: (layer, chunk(i), 0)), vec, vec],
        out_specs=out_specs,
        out_shape=out_shape,
        scratch_shapes=[pltpu.VMEM((N_FF_CHUNKS, D, FF_CHUNK), BF), pltpu.VMEM((D_FF // MLP_CHUNK, MLP_CHUNK, D), BF),
                        pltpu.VMEM((tm, D), BF), pltpu.VMEM((tm, D), F32)],
        compiler_params=_cparams(1, POST_VMEM_LIMIT),
        name=f"post_l{layer}",
    )(o_ctx, o_lat, *xs, mods, w_o, g1, b1, w1, w2, g2, b2)


def _rope_tables():
    n = HEAD_DIM // 4
    pos = np.arange(DEC_SEQ)
    inv = (np.float32(ROPE_BASE) ** (-np.arange(n, dtype=np.float32) / np.float32(n))).astype(np.float32)
    ang_r = ((pos // GRID_W).astype(np.float32)[:, None] * inv).astype(np.float32)
    ang_c = ((pos % GRID_W).astype(np.float32)[:, None] * inv).astype(np.float32)
    cr, sr, cc, sc = np.cos(ang_r), np.sin(ang_r), np.cos(ang_c), np.sin(ang_c)
    a = np.tile(np.concatenate([cr, cr, cc, cc], axis=-1), (1, D // HEAD_DIM))
    b = np.tile(np.concatenate([-sr, sr, -sc, sc], axis=-1), (1, D // HEAD_DIM))
    a = np.concatenate([a, np.ones((TM, D), np.float32)], axis=0)
    b = np.concatenate([b, np.zeros((TM, D), np.float32)], axis=0)
    return jnp.asarray(a, F32), jnp.asarray(b, F32)


def kernel(x_prompt, x_sample, c, cache_da_k, cache_da_v, cache_na_k, cache_na_v, cache_gq_k, cache_gq_v, c_ctx, ada_w, ada_b, ln_g, ln_b, mlp_w1, mlp_w2, da_w_qkv, da_w_o, da_lambda, da_subln_g, na_w_qkv, na_w_o, na_rel_bias, gq_w_qkv, gq_w_o, gq_q_norm, gq_k_norm, hy_w_in, hy_short_w, hy_short_b, hy_ffn_w1, hy_ffn_b1, hy_ffn_w2, hy_ffn_b2, hy_ffn_freq, hy_ffn_w3, hy_log_decay, hy_filter_bias, hy_w_o):
    cvec = jnp.concatenate([c_ctx[None, :], c, jnp.zeros((MOD_ROWS - 1 - DEC_BATCH, D), F32)], axis=0)
    mods = _mods(cvec, ada_w, ada_b)
    rope_a, rope_b = _rope_tables()

    def finish(o_ctx, o_lat, xs, layer, w_o, split_out=False):
        return _post(o_ctx, o_lat, xs, mods, layer, w_o.astype(BF), ln_g[layer, 0][None], ln_b[layer, 0][None],
                     mlp_w1, mlp_w2, ln_g[layer, 1][None], ln_b[layer, 1][None], split_out)

    xs = (x_prompt.reshape(T_CTX, D), x_sample.reshape(T_LAT, D))
    qb, kb, vb, ks, vs = _da_proj(*xs, mods, 0, da_w_qkv[0].astype(BF), rope_a, rope_b)
    state_da_k = ks.reshape(BATCH, 1, SEQ, DA_HEADS, 2 * HEAD_DIM)
    state_da_v = vs.reshape(BATCH, 1, SEQ, DA_HEADS, 2 * HEAD_DIM)
    o_ctx, o_lat = _da_attention(qb, kb, vb, cache_da_k, cache_da_v, da_lambda[0], da_subln_g[0][None], 0)
    x = finish(o_ctx, o_lat, xs, 0, da_w_o[0])

    qb, kb, vb, ks, vs = _na_proj(x, mods, 1, na_w_qkv[0].astype(BF))
    state_na_k, state_na_v = _untranspose_state(ks, NA_HEADS), _untranspose_state(vs, NA_HEADS)
    onehot, neg, mask = _na_constants()
    bias_tab = _na_bias_table(na_rel_bias[0], onehot, neg)
    o_ctx = _na_ctx_attention(qb, kb, vb)
    o_lat = _na_lat_attention(qb, kb, vb, _features_major(cache_na_k), _features_major(cache_na_v), bias_tab, mask)
    x = finish(o_ctx, o_lat, (x,), 1, na_w_o[0])

    g_mat = jnp.asarray(np.kron(np.eye(GN_BLOCK // HEAD_DIM), np.full((HEAD_DIM, HEAD_DIM), 1.0 / HEAD_DIM)), BF)
    qb, kb, vb, ks, vs = _gq_proj(x, mods, 2, gq_w_qkv[0].astype(BF), g_mat,
                                  jnp.tile(gq_q_norm[0], GQ_HEADS)[None], jnp.tile(gq_k_norm[0], GQ_KV_HEADS)[None],
                                  rope_a, rope_b)
    state_gq_k, state_gq_v = _untranspose_state(ks, GQ_KV_HEADS), _untranspose_state(vs, GQ_KV_HEADS)
    o_ctx, o_lat = _gq_attention(qb, kb, vb, _features_major(cache_gq_k), _features_major(cache_gq_v))
    x = finish(o_ctx, o_lat, (x,), 2, gq_w_o[0])

    u = _hy_proj(x, mods, 3, hy_w_in[0].astype(BF))
    w1 = jnp.pad(hy_ffn_w1[0], ((0, HY_EMB_PAD - HY_EMB), (0, 0)))
    zs = []
    for seq, nbatch, row0, dc in ((SEQ, BATCH, 0, D), (DEC_SEQ, DEC_BATCH, T_CTX, 512)):
        cmat, smat = _dft_tables(seq)
        hre, him, hny = _hy_filter(seq, _hy_embedding(seq), w1, hy_ffn_b1[0][None], hy_ffn_w2[0], hy_ffn_b2[0][None],
                                   hy_ffn_freq[0][None], hy_ffn_w3[0], hy_log_decay[0][None], cmat, smat)
        zs.append(_hy_conv(u, hy_short_w[0], hy_short_b[0][None], hy_filter_bias[0], hre, him, hny, cmat, smat,
                           seq, nbatch, row0, dc))
    y_ctx, y_lat = finish(zs[0], zs[1], (x,), 3, hy_w_o[0], split_out=True)

    return (y_ctx.reshape(BATCH, SEQ, D), y_lat.reshape(DEC_BATCH, DEC_SEQ, D),
            state_da_k, state_da_v, state_na_k, state_na_v, state_gq_k, state_gq_v)
```

```python
import functools
import math

import numpy as np
import jax
import jax.numpy as jnp
from jax import lax
from jax.experimental import pallas as pl
from jax.experimental.pallas import tpu as pltpu

F32 = jnp.float32
BF = jnp.bfloat16

D = 1024
BATCH = 16
SEQ = 256
DEC_BATCH = 8
DEC_SEQ = 1024
PAST = 256
DEPTH = 4
GRID_W = 64
GRID_ROWS = DEC_SEQ // GRID_W
D_FF = 4 * D
T_CTX = BATCH * SEQ
T_LAT = DEC_BATCH * DEC_SEQ
T = T_CTX + T_LAT
HEAD_DIM = 64
ATT_SCALE = HEAD_DIM ** -0.5
LOG2E = math.log2(math.e)
Q_SCALE = ATT_SCALE * LOG2E
DA_HEADS = 8
NA_HEADS = 16
NA_WIN_ROWS = 8
NA_WIN_COLS = 16
GQ_HEADS = 16
GQ_KV_HEADS = 4
HY_ORDER = 2
HY_BANDS = 16
HY_EMB = 1 + 2 * HY_BANDS
HY_EMB_PAD = 40
HY_FFN = 64
ROPE_BASE = 10000.0
LN_EPS = 1e-5
RMS_EPS = 1e-6
DN_ALPHA = (2 * DEPTH) ** 0.25
NEG_INF = -1e30

LANES = 128
TM = 512
TM_POST = 512
FF_CHUNK = 512
MLP_CHUNK = 1024
N_FF_CHUNKS = D_FF // FF_CHUNK
SUB_POST = 256
N_CTX_TILES = T_CTX // TM
N_TILES = T // TM
TQ = 512
CTX_BATCHES_PER_STEP = 4
CTX_ROWS = CTX_BATCHES_PER_STEP * SEQ
MOD_ROWS = 16
VMEM_LIMIT = 56 * 1024 * 1024
POST_VMEM_LIMIT = 58 * 1024 * 1024


def _cparams(n_axes, vmem_limit=VMEM_LIMIT):
    return pltpu.CompilerParams(dimension_semantics=("arbitrary",) * n_axes,
                                vmem_limit_bytes=vmem_limit)


def _dot(a, b):
    return jnp.dot(a, b, preferred_element_type=F32)


def _dot_nt(a, b):
    return lax.dot_general(a, b, (((1,), (1,)), ((), ())), preferred_element_type=F32)


def _const_spec(shape):
    nd = len(shape)
    return pl.BlockSpec(shape, lambda *_: (0,) * nd, pipeline_mode=pl.Buffered(1))


def _mod_spec(layer, tm=TM, off=0):
    nctx = T_CTX // tm

    def row(i):
        t = jnp.maximum(i - off, 0)
        return jnp.where(t < nctx, 0, 1 + (t - nctx) // (DEC_SEQ // tm))

    return pl.BlockSpec((None, None, 6, D), lambda i: (layer, row(i), 0, 0))


def _tok_spec(width, tm=TM, off=0):
    return pl.BlockSpec((tm, width), lambda i: (jnp.maximum(i - off, 0), 0))


def _ctx_spec(width, tm=TM, off=0):
    return pl.BlockSpec((tm, width), lambda i: (jnp.clip(i - off, 0, T_CTX // tm - 1), 0))


def _lat_spec(width, tm=TM, off=0):
    return pl.BlockSpec((tm, width), lambda i: (jnp.maximum(i - off - T_CTX // tm, 0), 0))


def _is_lat(tm=TM, off=0):
    return pl.program_id(0) >= off + T_CTX // tm


def _pick(ctx_ref, lat_ref):
    return jnp.where(_is_lat(), lat_ref[...], ctx_ref[...])


def _layer_norm(r, g, b):
    mu = jnp.mean(r, axis=-1, keepdims=True)
    c = r - mu
    var = jnp.mean(c * c, axis=-1, keepdims=True)
    return c * lax.rsqrt(var + LN_EPS) * g + b


def _mods_kernel(c_ref, w_ref, b_ref, o_ref):
    c = c_ref[...]
    s = (c / (1.0 + jnp.exp(-c))).astype(BF)
    o_ref[...] = _dot(s, w_ref[...].astype(BF)) + b_ref[...]


def _mods(cvec, ada_w, ada_b):
    tn = 1536
    out = pl.pallas_call(
        _mods_kernel,
        grid=(DEPTH, 6 * D // tn),
        in_specs=[pl.BlockSpec((MOD_ROWS, D), lambda l, n: (0, 0)),
                  pl.BlockSpec((None, D, tn), lambda l, n: (l, 0, n)),
                  pl.BlockSpec((None, 1, tn), lambda l, n: (l, 0, n))],
        out_specs=pl.BlockSpec((None, MOD_ROWS, tn), lambda l, n: (l, 0, n)),
        out_shape=jax.ShapeDtypeStruct((DEPTH, MOD_ROWS, 6 * D), F32),
        compiler_params=_cparams(2),
        name="adaln_mods",
    )(cvec, ada_w, ada_b.reshape(DEPTH, 1, 6 * D))
    return out.reshape(DEPTH, MOD_ROWS, 6, D)


def _modulate(x, mod_ref, shift, scale):
    return (x * (1.0 + mod_ref[scale:scale + 1, :]) + mod_ref[shift:shift + 1, :]).astype(BF)


def _rope(x, a, b):
    n = x.shape[1]
    lane = lax.broadcasted_iota(jnp.int32, x.shape, 1)
    partner = jnp.where((lane & 16) == 0, pltpu.roll(x, n - 16, 1), pltpu.roll(x, 16, 1))
    return x * a + partner * b


def _rope_spec(width):
    per = DEC_SEQ // TM
    return pl.BlockSpec((TM, width), lambda i: (jnp.where(i < N_CTX_TILES, per, (i - N_CTX_TILES) % per), 0))


def _lockstep(gens):
    gens = list(gens)
    while gens:
        gens = [g for g in gens if next(g, True) is None]


def _store_state(k, v, ks_ref, vs_ref, transposed):
    @pl.when(jnp.logical_not(_is_lat()))
    def _():
        if not transposed:
            ks_ref[...] = k
            vs_ref[...] = v
        else:
            n = k.shape[0]
            for xt, ref in ((k, ks_ref), (v, vs_ref)):
                for j in range(TM // SEQ):
                    ref[j * n:(j + 1) * n, :] = xt[:, j * SEQ:(j + 1) * SEQ]


def _qkv_out(nq, nk, transposed_state):
    specs = [_tok_spec(nq), _tok_spec(nk), _tok_spec(nk)]
    shapes = [jax.ShapeDtypeStruct((T, nq), BF), jax.ShapeDtypeStruct((T, nk), BF), jax.ShapeDtypeStruct((T, nk), BF)]
    if transposed_state:
        rows = (TM // SEQ) * nk
        specs += [pl.BlockSpec((rows, SEQ), lambda i: (jnp.minimum(i, N_CTX_TILES - 1), 0))] * 2
        shapes += [jax.ShapeDtypeStruct((BATCH * nk, SEQ), F32)] * 2
    else:
        specs += [_ctx_spec(nk)] * 2
        shapes += [jax.ShapeDtypeStruct((T_CTX, nk), F32)] * 2
    return specs, shapes


def _features_major(cache):
    b, _, past, heads, dh = cache.shape
    return cache.transpose(0, 1, 3, 4, 2).reshape(b, heads * dh, past)


def _untranspose_state(st, heads):
    return st.reshape(BATCH, heads, HEAD_DIM, SEQ).transpose(0, 3, 1, 2)[:, None]


def _da_proj_kernel(xc_ref, xl_ref, mod_ref, w_ref, ra_ref, rb_ref, qb_ref, kb_ref, vb_ref, ks_ref, vs_ref):
    h = _modulate(_pick(xc_ref, xl_ref), mod_ref, 0, 1)
    a, b = ra_ref[...], rb_ref[...]
    q = _dot(h, w_ref[:, 0:D])
    k = _dot(h, w_ref[:, D:2 * D])
    qb_ref[...] = (_rope(q, a, b) * Q_SCALE).astype(BF)
    v = _dot(h, w_ref[:, 2 * D:3 * D])
    kb_ref[...] = _rope(k, a, b).astype(BF)
    vb_ref[...] = v.astype(BF)
    _store_state(k, v, ks_ref, vs_ref, False)


def _da_proj(x_ctx, x_lat, mods, layer, w, rope_a, rope_b):
    specs, shapes = _qkv_out(D, D, False)
    return pl.pallas_call(
        _da_proj_kernel,
        grid=(N_TILES,),
        in_specs=[_ctx_spec(D), _lat_spec(D), _mod_spec(layer), _const_spec((D, 3 * D)),
                  _rope_spec(D), _rope_spec(D)],
        out_specs=specs, out_shape=shapes,
        compiler_params=_cparams(1),
        name=f"da_proj_l{layer}",
    )(x_ctx, x_lat, mods, w, rope_a, rope_b)


def _na_proj_kernel(x_ref, mod_ref, w_ref, qb_ref, kb_ref, vb_ref, ks_ref, vs_ref):
    h = _modulate(x_ref[...], mod_ref, 0, 1)
    v = _dot(h, w_ref[:, 2 * D:3 * D])
    k = _dot(h, w_ref[:, D:2 * D])
    vb_ref[...] = v.astype(BF)
    vt = v.T
    q = _dot(h, w_ref[:, 0:D])
    kb_ref[...] = k.astype(BF)
    kt = k.T
    qb_ref[...] = (q * Q_SCALE).astype(BF)
    _store_state(kt, vt, ks_ref, vs_ref, True)


def _na_proj(x, mods, layer, w):
    specs, shapes = _qkv_out(D, D, True)
    return pl.pallas_call(
        _na_proj_kernel,
        grid=(N_TILES,),
        in_specs=[_tok_spec(D), _mod_spec(layer), _const_spec((D, 3 * D))],
        out_specs=specs, out_shape=shapes,
        compiler_params=_cparams(1),
        name=f"na_proj_l{layer}",
    )(x, mods, w)


GN_BLOCK = 256


def _head_rms(x, g_ref, gain):
    x2 = x * x
    hi = x2.astype(BF)
    lo = (x2 - hi.astype(F32)).astype(BF)
    g = g_ref[...]
    ms = jnp.concatenate(
        [_dot(hi[:, j:j + GN_BLOCK], g) + _dot(lo[:, j:j + GN_BLOCK], g) for j in range(0, x.shape[1], GN_BLOCK)],
        axis=1)
    return x * lax.rsqrt(ms + RMS_EPS) * gain


def _gq_proj_kernel(x_ref, mod_ref, w_ref, g_ref, qn_ref, kn_ref, ra_ref, rb_ref,
                    qb_ref, kb_ref, vb_ref, ks_ref, vs_ref):
    nq, nk = GQ_HEADS * HEAD_DIM, GQ_KV_HEADS * HEAD_DIM
    h = _modulate(x_ref[...], mod_ref, 0, 1)

    def finish_q(c, qc):
        cols = slice(c * GN_BLOCK, (c + 1) * GN_BLOCK)
        qc = _head_rms(qc, g_ref, qn_ref[:, cols])
        qb_ref[:, cols] = (_rope(qc, ra_ref[:, cols], rb_ref[:, cols]) * Q_SCALE).astype(BF)

    k = _dot(h, w_ref[:, nq:nq + nk])
    v = _dot(h, w_ref[:, nq + nk:nq + 2 * nk])
    prev = None
    for c in range(nq // GN_BLOCK):
        qc = _dot(h, w_ref[:, c * GN_BLOCK:(c + 1) * GN_BLOCK])
        if c == 0:
            k = _head_rms(k, g_ref, kn_ref[...])
            kb_ref[...] = _rope(k, ra_ref[:, 0:nk], rb_ref[:, 0:nk]).astype(BF)
            vb_ref[...] = v.astype(BF)
            kt, vt = k.T, v.T
        else:
            finish_q(*prev)
        prev = (c, qc)
    finish_q(*prev)
    _store_state(kt, vt, ks_ref, vs_ref, True)


def _gq_proj(x, mods, layer, w, g_mat, qn, kn, rope_a, rope_b):
    nq, nk = GQ_HEADS * HEAD_DIM, GQ_KV_HEADS * HEAD_DIM
    specs, shapes = _qkv_out(nq, nk, True)
    return pl.pallas_call(
        _gq_proj_kernel,
        grid=(N_TILES,),
        in_specs=[_tok_spec(D), _mod_spec(layer), _const_spec((D, nq + 2 * nk)),
                  _const_spec((GN_BLOCK, GN_BLOCK)), _const_spec((1, nq)), _const_spec((1, nk)),
                  _rope_spec(D), _rope_spec(D)],
        out_specs=specs, out_shape=shapes,
        compiler_params=_cparams(1),
        name=f"gq_proj_l{layer}",
    )(x, mods, w, g_mat, qn, kn, rope_a, rope_b)


def _hy_proj_kernel(x_ref, mod_ref, w_ref, u_ref):
    h = _modulate(x_ref[...], mod_ref, 0, 1)
    for c in range(HY_ORDER + 1):
        u_ref[:, c * D:(c + 1) * D] = _dot(h, w_ref[:, c * D:(c + 1) * D])


def _hy_proj(x, mods, layer, w):
    n = (HY_ORDER + 1) * D
    return pl.pallas_call(
        _hy_proj_kernel,
        grid=(N_TILES,),
        in_specs=[_tok_spec(D), _mod_spec(layer), _const_spec((D, n))],
        out_specs=_tok_spec(n),
        out_shape=jax.ShapeDtypeStruct((T, n), F32),
        compiler_params=_cparams(1),
        name=f"hy_proj_l{layer}",
    )(x, mods, w)


def _scores(qm, segs):
    return [_dot(qm, seg[0]) if len(seg) == 3 else _dot_nt(qm, seg[0]) for seg in segs]


def _softmax_finish(scores, segs):
    m = scores[0].max(axis=-1, keepdims=True)
    for s in scores[1:]:
        m = jnp.maximum(m, s.max(axis=-1, keepdims=True))
    den = None
    out = None
    for s, seg in zip(scores, segs):
        e = jnp.exp2(s - m)
        d = e.sum(axis=-1, keepdims=True)
        o = _dot_nt(e.astype(BF), seg[1]) if len(seg) == 3 else _dot(e.astype(BF), seg[1])
        den = d if den is None else den + d
        out = o if out is None else out + o
    return out / den


def _stack_halves(q, keep):
    return jnp.concatenate([q * keep[0], q * keep[1]], axis=0)


def _pipelined(jobs, score_fn, finish_fn):
    nxt = score_fn(jobs[0])
    for n, job in enumerate(jobs):
        cur, nxt = nxt, (score_fn(jobs[n + 1]) if n + 1 < len(jobs) else None)
        finish_fn(job, cur)


def _lane_half(shape):
    return lax.broadcasted_iota(jnp.int32, shape, 1) // HEAD_DIM


def _half_keep(half):
    return tuple(jnp.where(half == a, 1.0, 0.0).astype(BF) for a in (0, 1))


def _da_attn_kernel(*refs, has_cache, lam_init):
    if has_cache:
        q_ref, k_ref, v_ref, ck_ref, cv_ref, lam_ref, g_ref, o_ref = refs
    else:
        q_ref, k_ref, v_ref, lam_ref, g_ref, o_ref = refs
    lp = lam_ref[...]
    lam = (jnp.exp(jnp.sum(lp[0:1] * lp[1:2], axis=-1, keepdims=True))
           - jnp.exp(jnp.sum(lp[2:3] * lp[3:4], axis=-1, keepdims=True)) + lam_init)
    gain = g_ref[...] * (1.0 - lam_init)
    w = 2 * HEAD_DIM
    nheads = k_ref.shape[1] // w
    tq = min(TQ, q_ref.shape[0]) if has_cache else SEQ
    keep = _half_keep(_lane_half((tq, w)))
    caches = []
    if has_cache:
        for hd in range(nheads):
            head = pl.program_id(1) * nheads + hd
            caches.append((ck_ref[:, head, :].astype(BF), cv_ref[:, head, :].astype(BF)))

    def seg(hd, t):
        cols = slice(hd * w, (hd + 1) * w)
        if has_cache:
            return [(k_ref[:, cols], v_ref[:, cols]), caches[hd]]
        return [(k_ref[t * tq:(t + 1) * tq, cols], v_ref[t * tq:(t + 1) * tq, cols])]

    jobs = [(hd, t, a) for t in range(q_ref.shape[0] // tq) for hd in range(nheads) for a in (0, 1)]
    first = {}

    def score_fn(job):
        hd, t, a = job
        return _scores(q_ref[t * tq:(t + 1) * tq, hd * w:(hd + 1) * w] * keep[a], seg(hd, t))

    def finish_fn(job, scores):
        hd, t, a = job
        o = _softmax_finish(scores, seg(hd, t))
        if a == 0:
            first[0] = o
            return
        o = first[0] - lam * o
        ms = jnp.mean(o * o, axis=-1, keepdims=True)
        o_ref[t * tq:(t + 1) * tq, hd * w:(hd + 1) * w] = (o * lax.rsqrt(ms + RMS_EPS) * gain).astype(BF)

    _pipelined(jobs, score_fn, finish_fn)


DA_LAT_HEADS_PER_STEP = 2


def _da_attention(qb, kb, vb, cache_k, cache_v, lam_p, subln_g, layer_idx):
    lam_init = 0.8 - 0.6 * math.exp(-0.3 * layer_idx)
    w = 2 * HEAD_DIM
    small = [pl.BlockSpec((4, HEAD_DIM), lambda *_: (0, 0)), pl.BlockSpec((1, w), lambda *_: (0, 0))]
    o_ctx = pl.pallas_call(
        functools.partial(_da_attn_kernel, has_cache=False, lam_init=lam_init),
        grid=(BATCH,),
        in_specs=[pl.BlockSpec((SEQ, D), lambda b: (b, 0))] * 3 + small,
        out_specs=pl.BlockSpec((SEQ, D), lambda b: (b, 0)),
        out_shape=jax.ShapeDtypeStruct((T_CTX, D), BF),
        compiler_params=_cparams(1),
        name="da_attn_ctx",
    )(qb, kb, vb, lam_p, subln_g)
    k0 = T_CTX // DEC_SEQ
    hw = DA_LAT_HEADS_PER_STEP * w
    tok = pl.BlockSpec((DEC_SEQ, hw), lambda b, h: (k0 + b, h))
    c_spec = pl.BlockSpec((None, None, PAST, DA_HEADS, w), lambda b, h: (b, 0, 0, 0, 0))
    o_lat = pl.pallas_call(
        functools.partial(_da_attn_kernel, has_cache=True, lam_init=lam_init),
        grid=(DEC_BATCH, DA_HEADS // DA_LAT_HEADS_PER_STEP),
        in_specs=[tok, tok, tok, c_spec, c_spec] + small,
        out_specs=pl.BlockSpec((DEC_SEQ, hw), lambda b, h: (b, h)),
        out_shape=jax.ShapeDtypeStruct((T_LAT, D), BF),
        compiler_params=_cparams(2),
        name="da_attn_lat",
    )(qb, kb, vb, cache_k, cache_v, lam_p, subln_g)
    return o_ctx, o_lat


def _na_ctx_kernel(q_ref, k_ref, v_ref, o_ref):
    half = _lane_half((SEQ, LANES))
    keep = _half_keep(half)

    def block(job):
        b, p = job
        return slice(b * SEQ, (b + 1) * SEQ), slice(p * LANES, (p + 1) * LANES)

    def seg(job):
        return [(k_ref[block(job)], v_ref[block(job)])]

    def score_fn(job):
        return _scores(_stack_halves(q_ref[block(job)], keep), seg(job))

    def finish_fn(job, scores):
        o = _softmax_finish(scores, seg(job))
        o_ref[block(job)] = jnp.where(half == 0, o[0:SEQ], o[SEQ:2 * SEQ]).astype(BF)

    jobs = [(b, p) for b in range(q_ref.shape[0] // SEQ) for p in range(NA_HEADS // 2)]
    _pipelined(jobs, score_fn, finish_fn)


def _na_ctx_attention(qb, kb, vb):
    spec = pl.BlockSpec((CTX_ROWS, D), lambda b: (b, 0))
    return pl.pallas_call(
        _na_ctx_kernel,
        grid=(BATCH // CTX_BATCHES_PER_STEP,),
        in_specs=[spec] * 3,
        out_specs=spec,
        out_shape=jax.ShapeDtypeStruct((T_CTX, D), BF),
        compiler_params=_cparams(1),
        name="na_attn_ctx",
    )(qb, kb, vb)


NA_TILES = ((0, (0, 2, 4, 6)), (4, (0, 2, 4, 6, 8, 10)), (8, (4, 6, 8, 10, 12, 14)), (12, (8, 10, 12, 14)))
NA_MAX_CHUNKS = 6
NA_BIAS_BLOCKS = 2 * NA_WIN_ROWS - 2


NA_LAT_PAIRS_PER_STEP = 4


def _na_lat_kernel(q_ref, k_ref, v_ref, ck_ref, cv_ref, w_ref, m_ref, o_ref):
    rows = 4 * GRID_W
    half = _lane_half((rows, LANES))
    keep = _half_keep(half)
    caches = [(ck_ref[p * LANES:(p + 1) * LANES, :].astype(BF), cv_ref[p * LANES:(p + 1) * LANES, :].astype(BF))
              for p in range(NA_LAT_PAIRS_PER_STEP)]
    jobs = [(p, i) for p in range(NA_LAT_PAIRS_PER_STEP) for i in range(len(NA_TILES))]

    def key_rows(i):
        chunks = NA_TILES[i][1]
        return slice(chunks[0] * GRID_W, chunks[0] * GRID_W + len(chunks) * LANES)

    def score_fn(job):
        p, i = job
        cols = slice(p * LANES, (p + 1) * LANES)
        r0, chunks = NA_TILES[i]
        qm = _stack_halves(q_ref[i * rows:(i + 1) * rows, cols], keep)
        mask = m_ref[i, :, 0:len(chunks) * LANES]
        bias = jnp.concatenate(
            [jnp.concatenate([w_ref[p, a, (6 - kr + r0) * GRID_W:(6 - kr + r0) * GRID_W + rows, :] for kr in chunks],
                             axis=1) + mask for a in (0, 1)], axis=0)
        return [_dot_nt(qm, k_ref[key_rows(i), cols]) + bias, _dot(qm, caches[p][0])]

    def finish_fn(job, scores):
        p, i = job
        cols = slice(p * LANES, (p + 1) * LANES)
        o = _softmax_finish(scores, [(None, v_ref[key_rows(i), cols]), (None, caches[p][1], True)])
        o_ref[i * rows:(i + 1) * rows, cols] = jnp.where(half == 0, o[0:rows], o[rows:2 * rows]).astype(BF)

    _pipelined(jobs, score_fn, finish_fn)


def _na_lat_attention(qb, kb, vb, cache_k, cache_v, bias_tab, mask_tab):
    k0 = T_CTX // DEC_SEQ
    npair = NA_LAT_PAIRS_PER_STEP
    tok = pl.BlockSpec((DEC_SEQ, npair * LANES), lambda b, p: (k0 + b, p))
    c_spec = pl.BlockSpec((None, npair * LANES, PAST), lambda b, p: (b, p, 0))
    return pl.pallas_call(
        _na_lat_kernel,
        grid=(DEC_BATCH, NA_HEADS // 2 // npair),
        in_specs=[tok, tok, tok, c_spec, c_spec,
                  pl.BlockSpec((npair, 2, NA_BIAS_BLOCKS * GRID_W, LANES), lambda b, p: (p, 0, 0, 0)),
                  _const_spec(mask_tab.shape)],
        out_specs=pl.BlockSpec((DEC_SEQ, npair * LANES), lambda b, p: (b, p)),
        out_shape=jax.ShapeDtypeStruct((T_LAT, D), BF),
        compiler_params=_cparams(2),
        name="na_attn_lat",
    )(qb, kb, vb, cache_k, cache_v, bias_tab, mask_tab)


def _na_bias_kernel(t_ref, r_ref, n_ref, o_ref):
    t = t_ref[...]
    t1 = t.astype(BF)
    r1 = t - t1.astype(F32)
    t2 = r1.astype(BF)
    t3 = (r1 - t2.astype(F32)).astype(BF)
    r = r_ref[...]
    res = (_dot(t1, r) + _dot(t2, r) + _dot(t3, r) + n_ref[...]) * LOG2E
    for qc in range(GRID_W):
        o_ref[pl.ds(qc, t.shape[0], stride=GRID_W), :] = res[:, qc * LANES:(qc + 1) * LANES]


def _na_bias_table(rel_bias, onehot, neg):
    nrel = 2 * NA_WIN_COLS
    idx = 13 - np.arange(NA_BIAS_BLOCKS)[:, None] + np.arange(2)[None, :]
    t = jnp.pad(rel_bias[:, idx, :], ((0, 0), (0, 0), (0, 0), (0, 1)))
    t = t.reshape(NA_HEADS * NA_BIAS_BLOCKS, 2 * nrel)
    n = GRID_W * LANES
    out = pl.pallas_call(
        _na_bias_kernel,
        grid=(1,),
        in_specs=[pl.BlockSpec(t.shape, lambda j: (0, 0)),
                  pl.BlockSpec((2 * nrel, n), lambda j: (0, 0)),
                  pl.BlockSpec((1, n), lambda j: (0, 0))],
        out_specs=pl.BlockSpec((t.shape[0] * GRID_W, LANES), lambda j: (0, 0)),
        out_shape=jax.ShapeDtypeStruct((t.shape[0] * GRID_W, LANES), F32),
        compiler_params=_cparams(1),
        name="na_bias_table",
    )(t, onehot, neg)
    return out.reshape(NA_HEADS // 2, 2, NA_BIAS_BLOCKS * GRID_W, LANES)


def _na_constants():
    nrel = 2 * NA_WIN_COLS
    qc = np.arange(GRID_W)[:, None]
    kc = np.arange(GRID_W)[None, :]
    rel = np.clip(kc - qc, -(NA_WIN_COLS - 1), NA_WIN_COLS - 1) + NA_WIN_COLS - 1
    cs = np.clip(qc - NA_WIN_COLS // 2, 0, GRID_W - NA_WIN_COLS)
    col_in = (kc >= cs) & (kc < cs + NA_WIN_COLS)
    onehot = np.zeros((2, nrel, GRID_W, 2, GRID_W), np.float32)
    for hf in range(2):
        onehot[hf, rel, qc, hf, kc] = 1.0
    neg = np.where(col_in, 0.0, NEG_INF).astype(np.float32)
    neg = np.broadcast_to(neg[:, None, :], (GRID_W, 2, GRID_W)).reshape(1, -1)
    rows = 4 * GRID_W
    mask = np.full((len(NA_TILES), rows, NA_MAX_CHUNKS * LANES), NEG_INF, np.float32)
    kr = min(NA_WIN_ROWS, GRID_ROWS)
    for i, (r0, chunks) in enumerate(NA_TILES):
        qr = r0 + np.arange(rows)[:, None] // GRID_W
        rs = np.clip(qr - kr // 2, 0, GRID_ROWS - kr)
        for c, krow0 in enumerate(chunks):
            krow = krow0 + np.arange(LANES)[None, :] // GRID_W
            mask[i, :, c * LANES:(c + 1) * LANES] = np.where((krow >= rs) & (krow < rs + kr), 0.0, NEG_INF)
    return (jnp.asarray(onehot.reshape(2 * nrel, GRID_W * LANES), BF), jnp.asarray(neg), jnp.asarray(mask))


def _gq_attn_kernel(*refs, has_cache):
    if has_cache:
        q_ref, k_ref, v_ref, ck_ref, cv_ref, o_ref = refs
    else:
        q_ref, k_ref, v_ref, o_ref = refs
    group = GQ_HEADS // GQ_KV_HEADS
    qw = LANES * group
    tq = min(TQ // 2, q_ref.shape[0])
    half = _lane_half((tq, LANES))
    keep = _half_keep(half)
    nkvp = k_ref.shape[1] // LANES
    caches = []
    if has_cache:
        for kvp in range(nkvp):
            kcols = slice(kvp * LANES, (kvp + 1) * LANES)
            caches.append((ck_ref[kcols, :].astype(BF), cv_ref[kcols, :].astype(BF), True))

    def seg(job):
        kvp, t, _ = job
        kcols = slice(kvp * LANES, (kvp + 1) * LANES)
        if has_cache:
            return [(k_ref[:, kcols], v_ref[:, kcols]), caches[kvp]]
        return [(k_ref[t * tq:(t + 1) * tq, kcols], v_ref[t * tq:(t + 1) * tq, kcols])]

    jobs = [(kvp, t, kh) for t in range(q_ref.shape[0] // tq) for kvp in range(nkvp) for kh in (0, 1)]

    def blocks(job):
        kvp, t, kh = job
        for pair in (2 * kh, 2 * kh + 1):
            yield slice(t * tq, (t + 1) * tq), slice(kvp * qw + pair * LANES, kvp * qw + (pair + 1) * LANES)

    def score_fn(job):
        kh = job[2]
        parts = []
        for rows, cols in blocks(job):
            for a in (0, 1):
                qm = q_ref[rows, cols] * keep[a]
                parts.append(qm if a == kh else pltpu.roll(qm.astype(F32), HEAD_DIM, 1).astype(BF))
        return _scores(jnp.concatenate(parts, axis=0), seg(job))

    def finish_fn(job, scores):
        kh = job[2]
        o = _softmax_finish(scores, seg(job))
        for n, (rows, cols) in enumerate(blocks(job)):
            heads = [o[(2 * n + a) * tq:(2 * n + a + 1) * tq] for a in (0, 1)]
            heads = [h if a == kh else pltpu.roll(h, HEAD_DIM, 1) for a, h in enumerate(heads)]
            o_ref[rows, cols] = jnp.where(half == 0, heads[0], heads[1]).astype(BF)

    _pipelined(jobs, score_fn, finish_fn)


GQ_LAT_ROWS = 1024


def _gq_attention(qb, kb, vb, cache_k, cache_v):
    nk = GQ_KV_HEADS * HEAD_DIM
    qw = LANES * (GQ_HEADS // GQ_KV_HEADS)
    npair = GQ_KV_HEADS // 2
    o_ctx = pl.pallas_call(
        functools.partial(_gq_attn_kernel, has_cache=False),
        grid=(BATCH // CTX_BATCHES_PER_STEP,),
        in_specs=[pl.BlockSpec((CTX_ROWS, D), lambda b: (b, 0))] + [pl.BlockSpec((CTX_ROWS, nk), lambda b: (b, 0))] * 2,
        out_specs=pl.BlockSpec((CTX_ROWS, D), lambda b: (b, 0)),
        out_shape=jax.ShapeDtypeStruct((T_CTX, D), BF),
        compiler_params=_cparams(1),
        name="gq_attn_ctx",
    )(qb, kb, vb)
    qt = DEC_SEQ // GQ_LAT_ROWS
    q0, k0 = T_CTX // GQ_LAT_ROWS, T_CTX // DEC_SEQ
    kv_spec = pl.BlockSpec((DEC_SEQ, LANES), lambda b, p, t: (k0 + b, p))
    c_spec = pl.BlockSpec((None, LANES, PAST), lambda b, p, t: (b, p, 0))
    o_lat = pl.pallas_call(
        functools.partial(_gq_attn_kernel, has_cache=True),
        grid=(DEC_BATCH, npair, qt),
        in_specs=[pl.BlockSpec((GQ_LAT_ROWS, qw), lambda b, p, t: (q0 + b * qt + t, p)), kv_spec, kv_spec, c_spec,
                  c_spec],
        out_specs=pl.BlockSpec((GQ_LAT_ROWS, qw), lambda b, p, t: (b * qt + t, p)),
        out_shape=jax.ShapeDtypeStruct((T_LAT, D), BF),
        compiler_params=_cparams(3),
        name="gq_attn_lat",
    )(qb, kb, vb, cache_k, cache_v)
    return o_ctx, o_lat


def _dot_3pass(a, b):
    ah, bh = a.astype(BF), b.astype(BF)
    al, bl = (a - ah.astype(F32)).astype(BF), (b - bh.astype(F32)).astype(BF)
    return _dot(ah, bh) + _dot(ah, bl) + _dot(al, bh)


def _hy_filter_kernel(emb_ref, w1_ref, b1_ref, w2_ref, b2_ref, fr_ref, w3f_ref, w3b_ref, ldf_ref, ldb_ref,
                      c_ref, s_ref, hre_ref, him_ref, hny_ref, hid_ref, cb_ref, sb_ref):
    seq = emb_ref.shape[0]

    @pl.when((pl.program_id(0) == 0) & (pl.program_id(1) == 0))
    def _():
        hp = lax.Precision.HIGHEST
        fr = fr_ref[...]
        hid = jnp.sin(fr * (jnp.dot(emb_ref[...], w1_ref[...], precision=hp, preferred_element_type=F32)
                            + b1_ref[...]))
        hid_ref[...] = jnp.sin(fr * (jnp.dot(hid, w2_ref[...], precision=hp, preferred_element_type=F32)
                                     + b2_ref[...]))
        cb_ref[...] = c_ref[...].astype(BF)
        sb_ref[...] = s_ref[...].astype(BF)

    hid = hid_ref[...]
    t = emb_ref[:, 0:1]
    fwd = _dot_3pass(hid, w3f_ref[...]) * jnp.exp(-jnp.exp(ldf_ref[...]) * t)
    bwd = _dot_3pass(hid, w3b_ref[...]) * jnp.exp(-jnp.exp(ldb_ref[...]) * t)
    row = lax.broadcasted_iota(jnp.int32, fwd.shape, 0)
    bwd = jnp.where(row == 0, 0.0, bwd)
    even = fwd + bwd
    odd = bwd - fwd
    wk = jnp.where(row == 0, 0.5 / seq, 1.0 / seq)
    hre_ref[...] = _dot(cb_ref[...], even.astype(BF)) * wk
    him_ref[...] = _dot(sb_ref[...], odd.astype(BF)) * wk
    alt = jnp.where((row & 1) == 0, 1.0, -1.0)
    hny_ref[...] = jnp.sum(alt * even, axis=0, keepdims=True) * (0.5 / seq)


def _hy_filter(seq, emb, w1, b1, w2, b2, freq, w3, log_decay, cmat, smat):
    dc = 512
    nj = D // dc
    small = [_const_spec(a.shape) for a in (emb, w1, b1, w2, b2, freq)]
    return pl.pallas_call(
        _hy_filter_kernel,
        grid=(HY_ORDER, nj),
        in_specs=small + [pl.BlockSpec((HY_FFN, dc), lambda o, j: (0, (2 * o) * nj + j)),
                          pl.BlockSpec((HY_FFN, dc), lambda o, j: (0, (2 * o + 1) * nj + j)),
                          pl.BlockSpec((1, dc), lambda o, j: (0, (2 * o) * nj + j)),
                          pl.BlockSpec((1, dc), lambda o, j: (0, (2 * o + 1) * nj + j)),
                          _const_spec((seq, seq)), _const_spec((seq, seq))],
        out_specs=[pl.BlockSpec((None, seq, dc), lambda o, j: (o, 0, j)),
                   pl.BlockSpec((None, seq, dc), lambda o, j: (o, 0, j)),
                   pl.BlockSpec((None, 1, dc), lambda o, j: (o, 0, j))],
        out_shape=[jax.ShapeDtypeStruct((HY_ORDER, seq, D), F32), jax.ShapeDtypeStruct((HY_ORDER, seq, D), F32),
                   jax.ShapeDtypeStruct((HY_ORDER, 1, D), F32)],
        scratch_shapes=[pltpu.VMEM((seq, HY_FFN), F32), pltpu.VMEM((seq, seq), BF), pltpu.VMEM((seq, seq), BF)],
        compiler_params=_cparams(2),
        name=f"hy_filter_{seq}",
    )(emb, w1, b1, w2, b2, freq, w3, w3, log_decay, log_decay, cmat, smat)


HY_SUB = 256


def _hy_conv_kernel(u0_ref, u1_ref, u2_ref, sw0_ref, sw1_ref, sw2_ref, sb0_ref, sb1_ref, sb2_ref,
                    fb_ref, hre_ref, him_ref, hny_ref, c_ref, s_ref, o_ref, cb_ref, sb_ref):
    seq, dc = u0_ref.shape

    @pl.when((pl.program_id(0) == 0) & (pl.program_id(1) == 0))
    def _():
        cb_ref[...] = c_ref[...].astype(BF)
        sb_ref[...] = s_ref[...].astype(BF)

    row = lax.broadcasted_iota(jnp.int32, (seq, HY_SUB), 0)
    alt = jnp.where((row & 1) == 0, 1.0, -1.0)

    def sub_tile(cols):
        def short_conv(u_ref, w_ref, b_ref):
            u = u_ref[:, cols]
            prev = jnp.where(row == 0, 0.0, pltpu.roll(u, 1, 0))
            nxt = jnp.where(row == seq - 1, 0.0, pltpu.roll(u, seq - 1, 0))
            return prev * w_ref[0:1, cols] + u * w_ref[1:2, cols] + nxt * w_ref[2:3, cols] + b_ref[:, cols]

        z = short_conv(u0_ref, sw0_ref, sb0_ref)
        gates = (short_conv(u1_ref, sw1_ref, sb1_ref), short_conv(u2_ref, sw2_ref, sb2_ref))
        yield
        for o in range(HY_ORDER):
            zb = z.astype(BF)
            zc, zs = _dot(cb_ref[...], zb), _dot(sb_ref[...], zb)
            yield
            hre, him = hre_ref[o, :, cols], him_ref[o, :, cols]
            p_re = (zc * hre + zs * him).astype(BF)
            p_im = (zc * him - zs * hre).astype(BF)
            y = _dot(cb_ref[...], p_re) - _dot(sb_ref[...], p_im)
            yield
            nyq = jnp.sum(alt * z, axis=0, keepdims=True) * hny_ref[o, :, cols]
            z = gates[o] * (y + alt * nyq + z * fb_ref[o:o + 1, cols])
        o_ref[:, cols] = z.astype(BF)

    _lockstep(sub_tile(slice(j * HY_SUB, (j + 1) * HY_SUB)) for j in range(dc // HY_SUB))


def _hy_conv(u, short_w, short_b, filter_bias, hre, him, hny, cmat, smat, seq, nbatch, row0, dc):
    nj = D // dc
    r0 = row0 // seq

    def part(p):
        return pl.BlockSpec((seq, dc), lambda j, b: (r0 + b, p * nj + j))

    def vec(rows, p):
        return pl.BlockSpec((rows, dc), lambda j, b: (0, p * nj + j))

    in_specs = ([part(p) for p in range(3)] + [vec(3, p) for p in range(3)] + [vec(1, p) for p in range(3)]
                + [pl.BlockSpec((HY_ORDER, dc), lambda j, b: (0, j)),
                   pl.BlockSpec((HY_ORDER, seq, dc), lambda j, b: (0, 0, j), pipeline_mode=pl.Buffered(1)),
                   pl.BlockSpec((HY_ORDER, seq, dc), lambda j, b: (0, 0, j), pipeline_mode=pl.Buffered(1)),
                   pl.BlockSpec((HY_ORDER, 1, dc), lambda j, b: (0, 0, j)),
                   _const_spec((seq, seq)), _const_spec((seq, seq))])
    return pl.pallas_call(
        _hy_conv_kernel,
        grid=(nj, nbatch),
        in_specs=in_specs,
        out_specs=pl.BlockSpec((seq, dc), lambda j, b: (b, j)),
        out_shape=jax.ShapeDtypeStruct((nbatch * seq, D), BF),
        scratch_shapes=[pltpu.VMEM((seq, seq), BF), pltpu.VMEM((seq, seq), BF)],
        compiler_params=_cparams(2),
        name=f"hy_conv_{seq}",
    )(u, u, u, short_w, short_w, short_w, short_b, short_b, short_b, filter_bias, hre, him, hny, cmat, smat)


def _dft_tables(seq):
    k = np.arange(seq, dtype=np.int64)
    ang = np.pi * ((k[:, None] * k[None, :]) % (2 * seq)) / seq
    return jnp.asarray(np.cos(ang), F32), jnp.asarray(np.sin(ang), F32)


def _hy_embedding(seq):
    t = np.arange(seq, dtype=np.float32) / np.float32(seq)
    ang = (2.0 * math.pi) * t[:, None] * np.arange(1, HY_BANDS + 1, dtype=np.float32)
    emb = np.concatenate([t[:, None], np.cos(ang), np.sin(ang)], axis=-1).astype(np.float32)
    return jnp.asarray(np.pad(emb, ((0, 0), (0, HY_EMB_PAD - HY_EMB))))


def _post_kernel(*refs, split_x, split_out, tm):
    oc_ref, ol_ref = refs[0:2]
    x_refs, refs = (refs[2:4], refs[4:]) if split_x else (refs[2:3], refs[3:])
    mod_ref, wo_ref, g1_ref, b1_ref, w1c_ref, w2c_ref, g2_ref, b2_ref = refs[0:8]
    outs, (w1_ref, w2_ref, h_ref, acc_ref) = refs[8:-4], refs[-4:]
    x1_ref = outs[0]
    step = pl.program_id(0)
    is_lat = _is_lat(tm, N_FF_CHUNKS - 1)
    nsub = tm // SUB_POST
    per = MLP_CHUNK // FF_CHUNK

    def rows(j):
        return slice(j * SUB_POST, (j + 1) * SUB_POST)

    def pick(c_ref, l_ref, j):
        return jnp.where(is_lat, l_ref[rows(j), :], c_ref[rows(j), :])

    def norm1(j):
        a = _dot(pick(oc_ref, ol_ref, j), wo_ref[...])
        x = pick(x_refs[0], x_refs[1], j) if split_x else x_refs[0][rows(j), :]
        x1 = _layer_norm(DN_ALPHA * x + mod_ref[2:3, :] * a, g1_ref[...], b1_ref[...])
        return x1, _modulate(x1, mod_ref, 3, 4)

    def norm2(x1, acc):
        return _layer_norm(DN_ALPHA * x1 + mod_ref[5:6, :] * acc, g2_ref[...], b2_ref[...])

    def mlp_chunk(h, c):
        a = jnp.concatenate([_dot(h, w1_ref[per * c + i]) for i in range(per)], axis=1)
        a = jnp.maximum(a, 0.0)
        return _dot((a * a).astype(BF), w2_ref[c])

    def write_branched(ys):
        yc_ref, yl_ref = outs

        @pl.when(jnp.logical_not(is_lat))
        def _():
            for j, y in enumerate(ys):
                yc_ref[rows(j), :] = y

        @pl.when(is_lat)
        def _():
            for j, y in enumerate(ys):
                yl_ref[rows(j), :] = y

    @pl.when(step < N_FF_CHUNKS)
    def _():
        w2_rows = pl.ds(pl.multiple_of((step % per) * FF_CHUNK, FF_CHUNK), FF_CHUNK)
        w1_ref[step] = w1c_ref[...].astype(BF)
        w2_ref[step // per, w2_rows, :] = w2c_ref[...].astype(BF)

        @pl.when(step == 0)
        def _():
            for j in range(nsub):
                x1_ref[rows(j), :], h_ref[rows(j), :] = norm1(j)
            acc_ref[...] = jnp.zeros_like(acc_ref)

        a = jnp.maximum(_dot(h_ref[...], w1_ref[step]), 0.0)
        acc_ref[...] += _dot((a * a).astype(BF), w2_ref[step // per, w2_rows, :])

        @pl.when(step == N_FF_CHUNKS - 1)
        def _():
            ys = [norm2(x1_ref[rows(j), :], acc_ref[rows(j), :]) for j in range(nsub)]
            if split_out:
                write_branched(ys)
            else:
                for j, y in enumerate(ys):
                    outs[0][rows(j), :] = y

    def token_tile():
        ys = []
        cur = norm1(0)
        prev = None
        for j in range(nsub):
            x1, h = cur
            acc = mlp_chunk(h, 0)
            if j + 1 < nsub:
                cur = norm1(j + 1)
            if prev is not None:
                ys.append(norm2(*prev))
                if not split_out:
                    outs[0][rows(j - 1), :] = ys[-1]
            for c in range(1, D_FF // MLP_CHUNK):
                acc = acc + mlp_chunk(h, c)
            prev = (x1, acc)
        ys.append(norm2(*prev))
        if split_out:
            write_branched(ys)
        else:
            outs[0][rows(nsub - 1), :] = ys[-1]

    pl.when(step >= N_FF_CHUNKS)(token_tile)


def _post(o_ctx, o_lat, xs, mods, layer, w_o, g1, b1, w1, w2, g2, b2, split_out):
    tm, off = TM_POST, N_FF_CHUNKS - 1
    split_x = len(xs) == 2
    x_specs = [_ctx_spec(D, tm, off), _lat_spec(D, tm, off)] if split_x else [_tok_spec(D, tm, off)]
    vec = _const_spec((1, D))
    if split_out:
        out_specs = [_ctx_spec(D, tm, off), _lat_spec(D, tm, off)]
        out_shape = [jax.ShapeDtypeStruct((T_CTX, D), F32), jax.ShapeDtypeStruct((T_LAT, D), F32)]
    else:
        out_specs = _tok_spec(D, tm, off)
        out_shape = jax.ShapeDtypeStruct((T, D), F32)

    def chunk(i):
        return jnp.minimum(i, N_FF_CHUNKS - 1)

    return pl.pallas_call(
        functools.partial(_post_kernel, split_x=split_x, split_out=split_out, tm=tm),
        grid=(off + T // tm,),
        in_specs=[_ctx_spec(D, tm, off), _lat_spec(D, tm, off)] + x_specs + [
            _mod_spec(layer, tm, off), _const_spec((D, D)), vec, vec,
            pl.BlockSpec((None, D, FF_CHUNK), lambda i: (layer, 0, chunk(i))),
            pl.BlockSpec((None, FF_CHUNK, D), lambda i: (layer, chunk(i), 0)), vec, vec],
        out_specs=out_specs,
        out_shape=out_shape,
        scratch_shapes=[pltpu.VMEM((N_FF_CHUNKS, D, FF_CHUNK), BF), pltpu.VMEM((D_FF // MLP_CHUNK, MLP_CHUNK, D), BF),
                        pltpu.VMEM((tm, D), BF), pltpu.VMEM((tm, D), F32)],
        compiler_params=_cparams(1, POST_VMEM_LIMIT),
        name=f"post_l{layer}",
    )(o_ctx, o_lat, *xs, mods, w_o, g1, b1, w1, w2, g2, b2)


def _rope_tables():
    n = HEAD_DIM // 4
    pos = np.arange(DEC_SEQ)
    inv = (np.float32(ROPE_BASE) ** (-np.arange(n, dtype=np.float32) / np.float32(n))).astype(np.float32)
    ang_r = ((pos // GRID_W).astype(np.float32)[:, None] * inv).astype(np.float32)
    ang_c = ((pos % GRID_W).astype(np.float32)[:, None] * inv).astype(np.float32)
    cr, sr, cc, sc = np.cos(ang_r), np.sin(ang_r), np.cos(ang_c), np.sin(ang_c)
    a = np.tile(np.concatenate([cr, cr, cc, cc], axis=-1), (1, D // HEAD_DIM))
    b = np.tile(np.concatenate([-sr, sr, -sc, sc], axis=-1), (1, D // HEAD_DIM))
    a = np.concatenate([a, np.ones((TM, D), np.float32)], axis=0)
    b = np.concatenate([b, np.zeros((TM, D), np.float32)], axis=0)
    return jnp.asarray(a, F32), jnp.asarray(b, F32)


def kernel(x_prompt, x_sample, c, cache_da_k, cache_da_v, cache_na_k, cache_na_v, cache_gq_k, cache_gq_v, c_ctx, ada_w, ada_b, ln_g, ln_b, mlp_w1, mlp_w2, da_w_qkv, da_w_o, da_lambda, da_subln_g, na_w_qkv, na_w_o, na_rel_bias, gq_w_qkv, gq_w_o, gq_q_norm, gq_k_norm, hy_w_in, hy_short_w, hy_short_b, hy_ffn_w1, hy_ffn_b1, hy_ffn_w2, hy_ffn_b2, hy_ffn_freq, hy_ffn_w3, hy_log_decay, hy_filter_bias, hy_w_o):
    cvec = jnp.concatenate([c_ctx[None, :], c, jnp.zeros((MOD_ROWS - 1 - DEC_BATCH, D), F32)], axis=0)
    mods = _mods(cvec, ada_w, ada_b)
    rope_a, rope_b = _rope_tables()

    def finish(o_ctx, o_lat, xs, layer, w_o, split_out=False):
        return _post(o_ctx, o_lat, xs, mods, layer, w_o.astype(BF), ln_g[layer, 0][None], ln_b[layer, 0][None],
                     mlp_w1, mlp_w2, ln_g[layer, 1][None], ln_b[layer, 1][None], split_out)

    xs = (x_prompt.reshape(T_CTX, D), x_sample.reshape(T_LAT, D))
    qb, kb, vb, ks, vs = _da_proj(*xs, mods, 0, da_w_qkv[0].astype(BF), rope_a, rope_b)
    state_da_k = ks.reshape(BATCH, 1, SEQ, DA_HEADS, 2 * HEAD_DIM)
    state_da_v = vs.reshape(BATCH, 1, SEQ, DA_HEADS, 2 * HEAD_DIM)
    o_ctx, o_lat = _da_attention(qb, kb, vb, cache_da_k, cache_da_v, da_lambda[0], da_subln_g[0][None], 0)
    x = finish(o_ctx, o_lat, xs, 0, da_w_o[0])

    qb, kb, vb, ks, vs = _na_proj(x, mods, 1, na_w_qkv[0].astype(BF))
    state_na_k, state_na_v = _untranspose_state(ks, NA_HEADS), _untranspose_state(vs, NA_HEADS)
    onehot, neg, mask = _na_constants()
    bias_tab = _na_bias_table(na_rel_bias[0], onehot, neg)
    o_ctx = _na_ctx_attention(qb, kb, vb)
    o_lat = _na_lat_attention(qb, kb, vb, _features_major(cache_na_k), _features_major(cache_na_v), bias_tab, mask)
    x = finish(o_ctx, o_lat, (x,), 1, na_w_o[0])

    g_mat = jnp.asarray(np.kron(np.eye(GN_BLOCK // HEAD_DIM), np.full((HEAD_DIM, HEAD_DIM), 1.0 / HEAD_DIM)), BF)
    qb, kb, vb, ks, vs = _gq_proj(x, mods, 2, gq_w_qkv[0].astype(BF), g_mat,
                                  jnp.tile(gq_q_norm[0], GQ_HEADS)[None], jnp.tile(gq_k_norm[0], GQ_KV_HEADS)[None],
                                  rope_a, rope_b)
    state_gq_k, state_gq_v = _untranspose_state(ks, GQ_KV_HEADS), _untranspose_state(vs, GQ_KV_HEADS)
    o_ctx, o_lat = _gq_attention(qb, kb, vb, _features_major(cache_gq_k), _features_major(cache_gq_v))
    x = finish(o_ctx, o_lat, (x,), 2, gq_w_o[0])

    u = _hy_proj(x, mods, 3, hy_w_in[0].astype(BF))
    w1 = jnp.pad(hy_ffn_w1[0], ((0, HY_EMB_PAD - HY_EMB), (0, 0)))
    zs = []
    for seq, nbatch, row0, dc in ((SEQ, BATCH, 0, D), (DEC_SEQ, DEC_BATCH, T_CTX, 512)):
        cmat, smat = _dft_tables(seq)
        hre, him, hny = _hy_filter(seq, _hy_embedding(seq), w1, hy_ffn_b1[0][None], hy_ffn_w2[0], hy_ffn_b2[0][None],
                                   hy_ffn_freq[0][None], hy_ffn_w3[0], hy_log_decay[0][None], cmat, smat)
        zs.append(_hy_conv(u, hy_short_w[0], hy_short_b[0][None], hy_filter_bias[0], hre, him, hny, cmat, smat,
                           seq, nbatch, row0, dc))
    y_ctx, y_lat = finish(zs[0], zs[1], (x,), 3, hy_w_o[0], split_out=True)

    return (y_ctx.reshape(BATCH, SEQ, D), y_lat.reshape(DEC_BATCH, DEC_SEQ, D),
            state_da_k, state_da_v, state_na_k, state_na_v, state_gq_k, state_gq_v)
```

```python
import functools
import math

import numpy as np
import jax
import jax.numpy as jnp
from jax import lax
from jax.experimental import pallas as pl
from jax.experimental.pallas import tpu as pltpu

F32 = jnp.float32
BF = jnp.bfloat16

D = 1024
BATCH = 16
SEQ = 256
DEC_BATCH = 8
DEC_SEQ = 1024
PAST = 256
DEPTH = 4
GRID_W = 64
GRID_ROWS = DEC_SEQ // GRID_W
D_FF = 4 * D
T_CTX = BATCH * SEQ
T_LAT = DEC_BATCH * DEC_SEQ
T = T_CTX + T_LAT
HEAD_DIM = 64
ATT_SCALE = HEAD_DIM ** -0.5
LOG2E = math.log2(math.e)
Q_SCALE = ATT_SCALE * LOG2E
DA_HEADS = 8
NA_HEADS = 16
NA_WIN_ROWS = 8
NA_WIN_COLS = 16
GQ_HEADS = 16
GQ_KV_HEADS = 4
HY_ORDER = 2
HY_BANDS = 16
HY_EMB = 1 + 2 * HY_BANDS
HY_EMB_PAD = 40
HY_FFN = 64
ROPE_BASE = 10000.0
LN_EPS = 1e-5
RMS_EPS = 1e-6
DN_ALPHA = (2 * DEPTH) ** 0.25
NEG_INF = -1e30

LANES = 128
TM = 512
TM_POST = 512
FF_CHUNK = 512
MLP_CHUNK = 1024
N_FF_CHUNKS = D_FF // FF_CHUNK
SUB_POST = 256
N_CTX_TILES = T_CTX // TM
N_TILES = T // TM
TQ = 512
CTX_BATCHES_PER_STEP = 4
CTX_ROWS = CTX_BATCHES_PER_STEP * SEQ
MOD_ROWS = 16
VMEM_LIMIT = 56 * 1024 * 1024
POST_VMEM_LIMIT = 58 * 1024 * 1024


def _cparams(n_axes, vmem_limit=VMEM_LIMIT):
    return pltpu.CompilerParams(dimension_semantics=("arbitrary",) * n_axes,
                                vmem_limit_bytes=vmem_limit)


def _dot(a, b):
    return jnp.dot(a, b, preferred_element_type=F32)


def _dot_nt(a, b):
    return lax.dot_general(a, b, (((1,), (1,)), ((), ())), preferred_element_type=F32)


def _const_spec(shape):
    nd = len(shape)
    return pl.BlockSpec(shape, lambda *_: (0,) * nd, pipeline_mode=pl.Buffered(1))


def _mod_spec(layer, tm=TM, off=0):
    nctx = T_CTX // tm

    def row(i):
        t = jnp.maximum(i - off, 0)
        return jnp.where(t < nctx, 0, 1 + (t - nctx) // (DEC_SEQ // tm))

    return pl.BlockSpec((None, None, 6, D), lambda i: (layer, row(i), 0, 0))


def _tok_spec(width, tm=TM, off=0):
    return pl.BlockSpec((tm, width), lambda i: (jnp.maximum(i - off, 0), 0))


def _ctx_spec(width, tm=TM, off=0):
    return pl.BlockSpec((tm, width), lambda i: (jnp.clip(i - off, 0, T_CTX // tm - 1), 0))


def _lat_spec(width, tm=TM, off=0):
    return pl.BlockSpec((tm, width), lambda i: (jnp.maximum(i - off - T_CTX // tm, 0), 0))


def _is_lat(tm=TM, off=0):
    return pl.program_id(0) >= off + T_CTX // tm


def _pick(ctx_ref, lat_ref):
    return jnp.where(_is_lat(), lat_ref[...], ctx_ref[...])


def _layer_norm(r, g, b):
    mu = jnp.mean(r, axis=-1, keepdims=True)
    c = r - mu
    var = jnp.mean(c * c, axis=-1, keepdims=True)
    return c * lax.rsqrt(var + LN_EPS) * g + b


def _mods_kernel(c_ref, w_ref, b_ref, o_ref):
    c = c_ref[...]
    s = (c / (1.0 + jnp.exp(-c))).astype(BF)
    o_ref[...] = _dot(s, w_ref[...].astype(BF)) + b_ref[...]


def _mods(cvec, ada_w, ada_b):
    tn = 3072
    out = pl.pallas_call(
        _mods_kernel,
        grid=(DEPTH, 6 * D // tn),
        in_specs=[pl.BlockSpec((MOD_ROWS, D), lambda l, n: (0, 0)),
                  pl.BlockSpec((None, D, tn), lambda l, n: (l, 0, n)),
                  pl.BlockSpec((None, 1, tn), lambda l, n: (l, 0, n))],
        out_specs=pl.BlockSpec((None, MOD_ROWS, tn), lambda l, n: (l, 0, n)),
        out_shape=jax.ShapeDtypeStruct((DEPTH, MOD_ROWS, 6 * D), F32),
        compiler_params=_cparams(2),
        name="adaln_mods",
    )(cvec, ada_w, ada_b.reshape(DEPTH, 1, 6 * D))
    return out.reshape(DEPTH, MOD_ROWS, 6, D)


def _modulate(x, mod_ref, shift, scale):
    return (x * (1.0 + mod_ref[scale:scale + 1, :]) + mod_ref[shift:shift + 1, :]).astype(BF)


def _rope(x, a, b):
    n = x.shape[1]
    lane = lax.broadcasted_iota(jnp.int32, x.shape, 1)
    partner = jnp.where((lane & 16) == 0, pltpu.roll(x, n - 16, 1), pltpu.roll(x, 16, 1))
    return x * a + partner * b


def _rope_spec(width):
    per = DEC_SEQ // TM
    return pl.BlockSpec((TM, width), lambda i: (jnp.where(i < N_CTX_TILES, per, (i - N_CTX_TILES) % per), 0))


def _lockstep(gens):
    waiting, active = list(gens), []
    while waiting or active:
        if waiting:
            active.append(waiting.pop(0))
        active = [g for g in active if next(g, True) is None]


def _store_state(k, v, ks_ref, vs_ref, transposed):
    @pl.when(jnp.logical_not(_is_lat()))
    def _():
        if not transposed:
            ks_ref[...] = k
            vs_ref[...] = v
        else:
            n = k.shape[0]
            for xt, ref in ((k, ks_ref), (v, vs_ref)):
                for j in range(TM // SEQ):
                    ref[j * n:(j + 1) * n, :] = xt[:, j * SEQ:(j + 1) * SEQ]


def _qkv_out(nq, nk, transposed_state):
    specs = [_tok_spec(nq), _tok_spec(nk), _tok_spec(nk)]
    shapes = [jax.ShapeDtypeStruct((T, nq), BF), jax.ShapeDtypeStruct((T, nk), BF), jax.ShapeDtypeStruct((T, nk), BF)]
    if transposed_state:
        rows = (TM // SEQ) * nk
        specs += [pl.BlockSpec((rows, SEQ), lambda i: (jnp.minimum(i, N_CTX_TILES - 1), 0))] * 2
        shapes += [jax.ShapeDtypeStruct((BATCH * nk, SEQ), F32)] * 2
    else:
        specs += [_ctx_spec(nk)] * 2
        shapes += [jax.ShapeDtypeStruct((T_CTX, nk), F32)] * 2
    return specs, shapes


def _features_major(cache):
    b, _, past, heads, dh = cache.shape
    return cache.transpose(0, 1, 3, 4, 2).reshape(b, heads * dh, past)


def _untranspose_state(st, heads):
    return st.reshape(BATCH, heads, HEAD_DIM, SEQ).transpose(0, 3, 1, 2)[:, None]


def _da_proj_kernel(xc_ref, xl_ref, mod_ref, w_ref, ra_ref, rb_ref, qb_ref, kb_ref, vb_ref, ks_ref, vs_ref):
    h = _modulate(_pick(xc_ref, xl_ref), mod_ref, 0, 1)
    a, b = ra_ref[...], rb_ref[...]
    q = _dot(h, w_ref[:, 0:D])
    k = _dot(h, w_ref[:, D:2 * D])
    qb_ref[...] = (_rope(q, a, b) * Q_SCALE).astype(BF)
    v = _dot(h, w_ref[:, 2 * D:3 * D])
    kb_ref[...] = _rope(k, a, b).astype(BF)
    vb_ref[...] = v.astype(BF)
    _store_state(k, v, ks_ref, vs_ref, False)


def _da_proj(x_ctx, x_lat, mods, layer, w, rope_a, rope_b):
    specs, shapes = _qkv_out(D, D, False)
    return pl.pallas_call(
        _da_proj_kernel,
        grid=(N_TILES,),
        in_specs=[_ctx_spec(D), _lat_spec(D), _mod_spec(layer), _const_spec((D, 3 * D)),
                  _rope_spec(D), _rope_spec(D)],
        out_specs=specs, out_shape=shapes,
        compiler_params=_cparams(1),
        name=f"da_proj_l{layer}",
    )(x_ctx, x_lat, mods, w, rope_a, rope_b)


def _na_proj_kernel(x_ref, mod_ref, w_ref, qb_ref, kb_ref, vb_ref, ks_ref, vs_ref):
    h = _modulate(x_ref[...], mod_ref, 0, 1)
    v = _dot(h, w_ref[:, 2 * D:3 * D])
    k = _dot(h, w_ref[:, D:2 * D])
    vb_ref[...] = v.astype(BF)
    vt = v.T
    q = _dot(h, w_ref[:, 0:D])
    kb_ref[...] = k.astype(BF)
    kt = k.T
    qb_ref[...] = (q * Q_SCALE).astype(BF)
    _store_state(kt, vt, ks_ref, vs_ref, True)


def _na_proj(x, mods, layer, w):
    specs, shapes = _qkv_out(D, D, True)
    return pl.pallas_call(
        _na_proj_kernel,
        grid=(N_TILES,),
        in_specs=[_tok_spec(D), _mod_spec(layer), _const_spec((D, 3 * D))],
        out_specs=specs, out_shape=shapes,
        compiler_params=_cparams(1),
        name=f"na_proj_l{layer}",
    )(x, mods, w)


GN_BLOCK = 256


def _head_rms(x, g_ref, gain):
    x2 = x * x
    hi = x2.astype(BF)
    lo = (x2 - hi.astype(F32)).astype(BF)
    g = g_ref[...]
    ms = jnp.concatenate(
        [_dot(hi[:, j:j + GN_BLOCK], g) + _dot(lo[:, j:j + GN_BLOCK], g) for j in range(0, x.shape[1], GN_BLOCK)],
        axis=1)
    return x * lax.rsqrt(ms + RMS_EPS) * gain


def _gq_proj_kernel(x_ref, mod_ref, w_ref, g_ref, qn_ref, kn_ref, ra_ref, rb_ref,
                    qb_ref, kb_ref, vb_ref, ks_ref, vs_ref):
    nq, nk = GQ_HEADS * HEAD_DIM, GQ_KV_HEADS * HEAD_DIM
    h = _modulate(x_ref[...], mod_ref, 0, 1)

    def finish_q(c, qc):
        cols = slice(c * GN_BLOCK, (c + 1) * GN_BLOCK)
        qc = _head_rms(qc, g_ref, qn_ref[:, cols])
        qb_ref[:, cols] = (_rope(qc, ra_ref[:, cols], rb_ref[:, cols]) * Q_SCALE).astype(BF)

    k = _dot(h, w_ref[:, nq:nq + nk])
    v = _dot(h, w_ref[:, nq + nk:nq + 2 * nk])
    prev = None
    for c in range(nq // GN_BLOCK):
        qc = _dot(h, w_ref[:, c * GN_BLOCK:(c + 1) * GN_BLOCK])
        if c == 0:
            k = _head_rms(k, g_ref, kn_ref[...])
            kb_ref[...] = _rope(k, ra_ref[:, 0:nk], rb_ref[:, 0:nk]).astype(BF)
            vb_ref[...] = v.astype(BF)
            kt, vt = k.T, v.T
        else:
            finish_q(*prev)
        prev = (c, qc)
    finish_q(*prev)
    _store_state(kt, vt, ks_ref, vs_ref, True)


def _gq_proj(x, mods, layer, w, g_mat, qn, kn, rope_a, rope_b):
    nq, nk = GQ_HEADS * HEAD_DIM, GQ_KV_HEADS * HEAD_DIM
    specs, shapes = _qkv_out(nq, nk, True)
    return pl.pallas_call(
        _gq_proj_kernel,
        grid=(N_TILES,),
        in_specs=[_tok_spec(D), _mod_spec(layer), _const_spec((D, nq + 2 * nk)),
                  _const_spec((GN_BLOCK, GN_BLOCK)), _const_spec((1, nq)), _const_spec((1, nk)),
                  _rope_spec(D), _rope_spec(D)],
        out_specs=specs, out_shape=shapes,
        compiler_params=_cparams(1),
        name=f"gq_proj_l{layer}",
    )(x, mods, w, g_mat, qn, kn, rope_a, rope_b)


def _hy_proj_kernel(x_ref, mod_ref, w_ref, u_ref):
    h = _modulate(x_ref[...], mod_ref, 0, 1)
    for c in range(HY_ORDER + 1):
        u_ref[:, c * D:(c + 1) * D] = _dot(h, w_ref[:, c * D:(c + 1) * D])


def _hy_proj(x, mods, layer, w):
    n = (HY_ORDER + 1) * D
    return pl.pallas_call(
        _hy_proj_kernel,
        grid=(N_TILES,),
        in_specs=[_tok_spec(D), _mod_spec(layer), _const_spec((D, n))],
        out_specs=_tok_spec(n),
        out_shape=jax.ShapeDtypeStruct((T, n), F32),
        compiler_params=_cparams(1),
        name=f"hy_proj_l{layer}",
    )(x, mods, w)


def _scores(qm, segs):
    return [_dot(qm, seg[0]) if len(seg) == 3 else _dot_nt(qm, seg[0]) for seg in segs]


def _softmax_finish(scores, segs):
    m = scores[0].max(axis=-1, keepdims=True)
    for s in scores[1:]:
        m = jnp.maximum(m, s.max(axis=-1, keepdims=True))
    den = None
    out = None
    for s, seg in zip(scores, segs):
        e = jnp.exp2(s - m)
        d = e.sum(axis=-1, keepdims=True)
        o = _dot_nt(e.astype(BF), seg[1]) if len(seg) == 3 else _dot(e.astype(BF), seg[1])
        den = d if den is None else den + d
        out = o if out is None else out + o
    return out / den


def _stack_halves(q, keep):
    return jnp.concatenate([q * keep[0], q * keep[1]], axis=0)


def _pipelined(jobs, score_fn, finish_fn):
    nxt = score_fn(jobs[0])
    for n, job in enumerate(jobs):
        cur, nxt = nxt, (score_fn(jobs[n + 1]) if n + 1 < len(jobs) else None)
        finish_fn(job, cur)


def _lane_half(shape):
    return lax.broadcasted_iota(jnp.int32, shape, 1) // HEAD_DIM


def _half_keep(half):
    return tuple(jnp.where(half == a, 1.0, 0.0).astype(BF) for a in (0, 1))


def _da_attn_kernel(*refs, has_cache, lam_init):
    if has_cache:
        q_ref, k_ref, v_ref, ck_ref, cv_ref, lam_ref, g_ref, o_ref = refs
    else:
        q_ref, k_ref, v_ref, lam_ref, g_ref, o_ref = refs
    lp = lam_ref[...]
    lam = (jnp.exp(jnp.sum(lp[0:1] * lp[1:2], axis=-1, keepdims=True))
           - jnp.exp(jnp.sum(lp[2:3] * lp[3:4], axis=-1, keepdims=True)) + lam_init)
    gain = g_ref[...] * (1.0 - lam_init)
    w = 2 * HEAD_DIM
    nheads = k_ref.shape[1] // w
    tq = min(TQ, q_ref.shape[0]) if has_cache else SEQ
    keep = _half_keep(_lane_half((tq, w)))
    caches = []
    if has_cache:
        for hd in range(nheads):
            head = pl.program_id(1) * nheads + hd
            caches.append((ck_ref[:, head, :].astype(BF), cv_ref[:, head, :].astype(BF)))

    def seg(hd, t):
        cols = slice(hd * w, (hd + 1) * w)
        if has_cache:
            return [(k_ref[:, cols], v_ref[:, cols]), caches[hd]]
        return [(k_ref[t * tq:(t + 1) * tq, cols], v_ref[t * tq:(t + 1) * tq, cols])]

    jobs = [(hd, t, a) for t in range(q_ref.shape[0] // tq) for hd in range(nheads) for a in (0, 1)]
    first = {}

    def score_fn(job):
        hd, t, a = job
        return _scores(q_ref[t * tq:(t + 1) * tq, hd * w:(hd + 1) * w] * keep[a], seg(hd, t))

    def finish_fn(job, scores):
        hd, t, a = job
        o = _softmax_finish(scores, seg(hd, t))
        if a == 0:
            first[0] = o
            return
        o = first[0] - lam * o
        ms = jnp.mean(o * o, axis=-1, keepdims=True)
        o_ref[t * tq:(t + 1) * tq, hd * w:(hd + 1) * w] = (o * lax.rsqrt(ms + RMS_EPS) * gain).astype(BF)

    _pipelined(jobs, score_fn, finish_fn)


DA_LAT_HEADS_PER_STEP = 2


def _da_attention(qb, kb, vb, cache_k, cache_v, lam_p, subln_g, layer_idx):
    lam_init = 0.8 - 0.6 * math.exp(-0.3 * layer_idx)
    w = 2 * HEAD_DIM
    small = [pl.BlockSpec((4, HEAD_DIM), lambda *_: (0, 0)), pl.BlockSpec((1, w), lambda *_: (0, 0))]
    o_ctx = pl.pallas_call(
        functools.partial(_da_attn_kernel, has_cache=False, lam_init=lam_init),
        grid=(BATCH,),
        in_specs=[pl.BlockSpec((SEQ, D), lambda b: (b, 0))] * 3 + small,
        out_specs=pl.BlockSpec((SEQ, D), lambda b: (b, 0)),
        out_shape=jax.ShapeDtypeStruct((T_CTX, D), BF),
        compiler_params=_cparams(1),
        name="da_attn_ctx",
    )(qb, kb, vb, lam_p, subln_g)
    k0 = T_CTX // DEC_SEQ
    hw = DA_LAT_HEADS_PER_STEP * w
    tok = pl.BlockSpec((DEC_SEQ, hw), lambda b, h: (k0 + b, h))
    c_spec = pl.BlockSpec((None, None, PAST, DA_HEADS, w), lambda b, h: (b, 0, 0, 0, 0))
    o_lat = pl.pallas_call(
        functools.partial(_da_attn_kernel, has_cache=True, lam_init=lam_init),
        grid=(DEC_BATCH, DA_HEADS // DA_LAT_HEADS_PER_STEP),
        in_specs=[tok, tok, tok, c_spec, c_spec] + small,
        out_specs=pl.BlockSpec((DEC_SEQ, hw), lambda b, h: (b, h)),
        out_shape=jax.ShapeDtypeStruct((T_LAT, D), BF),
        compiler_params=_cparams(2),
        name="da_attn_lat",
    )(qb, kb, vb, cache_k, cache_v, lam_p, subln_g)
    return o_ctx, o_lat


def _na_ctx_kernel(q_ref, k_ref, v_ref, o_ref):
    half = _lane_half((SEQ, LANES))
    keep = _half_keep(half)

    def block(job):
        b, p = job
        return slice(b * SEQ, (b + 1) * SEQ), slice(p * LANES, (p + 1) * LANES)

    def seg(job):
        return [(k_ref[block(job)], v_ref[block(job)])]

    def score_fn(job):
        return _scores(_stack_halves(q_ref[block(job)], keep), seg(job))

    def finish_fn(job, scores):
        o = _softmax_finish(scores, seg(job))
        o_ref[block(job)] = jnp.where(half == 0, o[0:SEQ], o[SEQ:2 * SEQ]).astype(BF)

    jobs = [(b, p) for b in range(q_ref.shape[0] // SEQ) for p in range(NA_HEADS // 2)]
    _pipelined(jobs, score_fn, finish_fn)


def _na_ctx_attention(qb, kb, vb):
    spec = pl.BlockSpec((CTX_ROWS, D), lambda b: (b, 0))
    return pl.pallas_call(
        _na_ctx_kernel,
        grid=(BATCH // CTX_BATCHES_PER_STEP,),
        in_specs=[spec] * 3,
        out_specs=spec,
        out_shape=jax.ShapeDtypeStruct((T_CTX, D), BF),
        compiler_params=_cparams(1),
        name="na_attn_ctx",
    )(qb, kb, vb)


NA_TILES = ((0, (0, 2, 4, 6)), (4, (0, 2, 4, 6, 8, 10)), (8, (4, 6, 8, 10, 12, 14)), (12, (8, 10, 12, 14)))
NA_MAX_CHUNKS = 6
NA_BIAS_BLOCKS = 2 * NA_WIN_ROWS - 2


NA_LAT_PAIRS_PER_STEP = 4


def _na_row_window(qr):
    kr = min(NA_WIN_ROWS, GRID_ROWS)
    return min(max(qr - kr // 2, 0), GRID_ROWS - kr), kr


def _na_rows_all_valid(i):
    r0, chunks = NA_TILES[i]
    key_rows = range(chunks[0], chunks[-1] + 2)
    return all(_na_row_window(qr)[0] <= kr < sum(_na_row_window(qr)) for qr in range(r0, r0 + 4) for kr in key_rows)


def _na_lat_kernel(q_ref, k_ref, v_ref, ck_ref, cv_ref, w_ref, m_ref, o_ref):
    rows = 4 * GRID_W
    half = _lane_half((rows, LANES))
    keep = _half_keep(half)
    caches = [(ck_ref[p * LANES:(p + 1) * LANES, :].astype(BF), cv_ref[p * LANES:(p + 1) * LANES, :].astype(BF))
              for p in range(NA_LAT_PAIRS_PER_STEP)]
    jobs = [(p, i) for p in range(NA_LAT_PAIRS_PER_STEP) for i in range(len(NA_TILES))]

    def key_rows(i):
        chunks = NA_TILES[i][1]
        return slice(chunks[0] * GRID_W, chunks[0] * GRID_W + len(chunks) * LANES)

    def score_fn(job):
        p, i = job
        cols = slice(p * LANES, (p + 1) * LANES)
        r0, chunks = NA_TILES[i]
        qm = _stack_halves(q_ref[i * rows:(i + 1) * rows, cols], keep)
        heads = [jnp.concatenate([w_ref[p, a, (6 - kr + r0) * GRID_W:(6 - kr + r0) * GRID_W + rows, :] for kr in chunks],
                                 axis=1) for a in (0, 1)]
        if not _na_rows_all_valid(i):
            mask = m_ref[i, :, 0:len(chunks) * LANES]
            heads = [hb + mask for hb in heads]
        bias = jnp.concatenate(heads, axis=0)
        return [_dot_nt(qm, k_ref[key_rows(i), cols]) + bias, _dot(qm, caches[p][0])]

    def finish_fn(job, scores):
        p, i = job
        cols = slice(p * LANES, (p + 1) * LANES)
        o = _softmax_finish(scores, [(None, v_ref[key_rows(i), cols]), (None, caches[p][1], True)])
        o_ref[i * rows:(i + 1) * rows, cols] = jnp.where(half == 0, o[0:rows], o[rows:2 * rows]).astype(BF)

    _pipelined(jobs, score_fn, finish_fn)


def _na_lat_attention(qb, kb, vb, cache_k, cache_v, bias_tab, mask_tab):
    k0 = T_CTX // DEC_SEQ
    npair = NA_LAT_PAIRS_PER_STEP
    tok = pl.BlockSpec((DEC_SEQ, npair * LANES), lambda b, p: (k0 + b, p))
    c_spec = pl.BlockSpec((None, npair * LANES, PAST), lambda b, p: (b, p, 0))
    return pl.pallas_call(
        _na_lat_kernel,
        grid=(DEC_BATCH, NA_HEADS // 2 // npair),
        in_specs=[tok, tok, tok, c_spec, c_spec,
                  pl.BlockSpec((npair, 2, NA_BIAS_BLOCKS * GRID_W, LANES), lambda b, p: (p, 0, 0, 0)),
                  _const_spec(mask_tab.shape)],
        out_specs=pl.BlockSpec((DEC_SEQ, npair * LANES), lambda b, p: (b, p)),
        out_shape=jax.ShapeDtypeStruct((T_LAT, D), BF),
        compiler_params=_cparams(2),
        name="na_attn_lat",
    )(qb, kb, vb, cache_k, cache_v, bias_tab, mask_tab)


def _na_bias_kernel(t_ref, r_ref, n_ref, o_ref):
    t = t_ref[...]
    t1 = t.astype(BF)
    r1 = t - t1.astype(F32)
    t2 = r1.astype(BF)
    t3 = (r1 - t2.astype(F32)).astype(BF)
    r = r_ref[...]
    res = (_dot(t1, r) + _dot(t2, r) + _dot(t3, r) + n_ref[...]) * LOG2E
    for qc in range(GRID_W):
        o_ref[pl.ds(qc, t.shape[0], stride=GRID_W), :] = res[:, qc * LANES:(qc + 1) * LANES]


def _na_bias_table(rel_bias, onehot, neg):
    nrel = 2 * NA_WIN_COLS
    idx = 13 - np.arange(NA_BIAS_BLOCKS)[:, None] + np.arange(2)[None, :]
    t = jnp.pad(rel_bias[:, idx, :], ((0, 0), (0, 0), (0, 0), (0, 1)))
    t = t.reshape(NA_HEADS * NA_BIAS_BLOCKS, 2 * nrel)
    n = GRID_W * LANES
    out = pl.pallas_call(
        _na_bias_kernel,
        grid=(1,),
        in_specs=[pl.BlockSpec(t.shape, lambda j: (0, 0)),
                  pl.BlockSpec((2 * nrel, n), lambda j: (0, 0)),
                  pl.BlockSpec((1, n), lambda j: (0, 0))],
        out_specs=pl.BlockSpec((t.shape[0] * GRID_W, LANES), lambda j: (0, 0)),
        out_shape=jax.ShapeDtypeStruct((t.shape[0] * GRID_W, LANES), F32),
        compiler_params=_cparams(1),
        name="na_bias_table",
    )(t, onehot, neg)
    return out.reshape(NA_HEADS // 2, 2, NA_BIAS_BLOCKS * GRID_W, LANES)


def _na_constants():
    nrel = 2 * NA_WIN_COLS
    qc = np.arange(GRID_W)[:, None]
    kc = np.arange(GRID_W)[None, :]
    rel = np.clip(kc - qc, -(NA_WIN_COLS - 1), NA_WIN_COLS - 1) + NA_WIN_COLS - 1
    cs = np.clip(qc - NA_WIN_COLS // 2, 0, GRID_W - NA_WIN_COLS)
    col_in = (kc >= cs) & (kc < cs + NA_WIN_COLS)
    onehot = np.zeros((2, nrel, GRID_W, 2, GRID_W), np.float32)
    for hf in range(2):
        onehot[hf, rel, qc, hf, kc] = 1.0
    neg = np.where(col_in, 0.0, NEG_INF).astype(np.float32)
    neg = np.broadcast_to(neg[:, None, :], (GRID_W, 2, GRID_W)).reshape(1, -1)
    rows = 4 * GRID_W
    mask = np.full((len(NA_TILES), rows, NA_MAX_CHUNKS * LANES), NEG_INF, np.float32)
    kr = min(NA_WIN_ROWS, GRID_ROWS)
    for i, (r0, chunks) in enumerate(NA_TILES):
        qr = r0 + np.arange(rows)[:, None] // GRID_W
        rs = np.clip(qr - kr // 2, 0, GRID_ROWS - kr)
        for c, krow0 in enumerate(chunks):
            krow = krow0 + np.arange(LANES)[None, :] // GRID_W
            mask[i, :, c * LANES:(c + 1) * LANES] = np.where((krow >= rs) & (krow < rs + kr), 0.0, NEG_INF)
    return (jnp.asarray(onehot.reshape(2 * nrel, GRID_W * LANES), BF), jnp.asarray(neg), jnp.asarray(mask))


def _gq_attn_kernel(*refs, has_cache):
    if has_cache:
        q_ref, k_ref, v_ref, ck_ref, cv_ref, o_ref = refs
    else:
        q_ref, k_ref, v_ref, o_ref = refs
    group = GQ_HEADS // GQ_KV_HEADS
    qw = LANES * group
    tq = min(TQ // 2, q_ref.shape[0])
    half = _lane_half((tq, LANES))
    keep = _half_keep(half)
    nkvp = k_ref.shape[1] // LANES
    caches = []
    if has_cache:
        for kvp in range(nkvp):
            kcols = slice(kvp * LANES, (kvp + 1) * LANES)
            caches.append((ck_ref[kcols, :].astype(BF), cv_ref[kcols, :].astype(BF), True))

    def seg(job):
        kvp, t, _ = job
        kcols = slice(kvp * LANES, (kvp + 1) * LANES)
        if has_cache:
            return [(k_ref[:, kcols], v_ref[:, kcols]), caches[kvp]]
        return [(k_ref[t * tq:(t + 1) * tq, kcols], v_ref[t * tq:(t + 1) * tq, kcols])]

    jobs = [(kvp, t, kh) for t in range(q_ref.shape[0] // tq) for kvp in range(nkvp) for kh in (0, 1)]

    def blocks(job):
        kvp, t, kh = job
        for pair in (2 * kh, 2 * kh + 1):
            yield slice(t * tq, (t + 1) * tq), slice(kvp * qw + pair * LANES, kvp * qw + (pair + 1) * LANES)

    def score_fn(job):
        kh = job[2]
        parts = []
        for rows, cols in blocks(job):
            for a in (0, 1):
                qm = q_ref[rows, cols] * keep[a]
                parts.append(qm if a == kh else pltpu.roll(qm.astype(F32), HEAD_DIM, 1).astype(BF))
        return _scores(jnp.concatenate(parts, axis=0), seg(job))

    def finish_fn(job, scores):
        kh = job[2]
        o = _softmax_finish(scores, seg(job))
        for n, (rows, cols) in enumerate(blocks(job)):
            heads = [o[(2 * n + a) * tq:(2 * n + a + 1) * tq] for a in (0, 1)]
            heads = [h if a == kh else pltpu.roll(h, HEAD_DIM, 1) for a, h in enumerate(heads)]
            o_ref[rows, cols] = jnp.where(half == 0, heads[0], heads[1]).astype(BF)

    _pipelined(jobs, score_fn, finish_fn)


GQ_LAT_ROWS = 1024


def _gq_attention(qb, kb, vb, cache_k, cache_v):
    nk = GQ_KV_HEADS * HEAD_DIM
    qw = LANES * (GQ_HEADS // GQ_KV_HEADS)
    npair = GQ_KV_HEADS // 2
    o_ctx = pl.pallas_call(
        functools.partial(_gq_attn_kernel, has_cache=False),
        grid=(BATCH // CTX_BATCHES_PER_STEP,),
        in_specs=[pl.BlockSpec((CTX_ROWS, D), lambda b: (b, 0))] + [pl.BlockSpec((CTX_ROWS, nk), lambda b: (b, 0))] * 2,
        out_specs=pl.BlockSpec((CTX_ROWS, D), lambda b: (b, 0)),
        out_shape=jax.ShapeDtypeStruct((T_CTX, D), BF),
        compiler_params=_cparams(1),
        name="gq_attn_ctx",
    )(qb, kb, vb)
    qt = DEC_SEQ // GQ_LAT_ROWS
    q0, k0 = T_CTX // GQ_LAT_ROWS, T_CTX // DEC_SEQ
    kv_spec = pl.BlockSpec((DEC_SEQ, LANES), lambda b, p, t: (k0 + b, p))
    c_spec = pl.BlockSpec((None, LANES, PAST), lambda b, p, t: (b, p, 0))
    o_lat = pl.pallas_call(
        functools.partial(_gq_attn_kernel, has_cache=True),
        grid=(DEC_BATCH, npair, qt),
        in_specs=[pl.BlockSpec((GQ_LAT_ROWS, qw), lambda b, p, t: (q0 + b * qt + t, p)), kv_spec, kv_spec, c_spec,
                  c_spec],
        out_specs=pl.BlockSpec((GQ_LAT_ROWS, qw), lambda b, p, t: (b * qt + t, p)),
        out_shape=jax.ShapeDtypeStruct((T_LAT, D), BF),
        compiler_params=_cparams(3),
        name="gq_attn_lat",
    )(qb, kb, vb, cache_k, cache_v)
    return o_ctx, o_lat


def _dot_3pass(a, b):
    ah, bh = a.astype(BF), b.astype(BF)
    al, bl = (a - ah.astype(F32)).astype(BF), (b - bh.astype(F32)).astype(BF)
    return _dot(ah, bh) + _dot(ah, bl) + _dot(al, bh)


def _hy_filter_kernel(emb_ref, w1_ref, b1_ref, w2_ref, b2_ref, fr_ref, w3f_ref, w3b_ref, ldf_ref, ldb_ref,
                      c_ref, s_ref, hre_ref, him_ref, hny_ref, hid_ref, cb_ref, sb_ref):
    seq = emb_ref.shape[0]

    @pl.when((pl.program_id(0) == 0) & (pl.program_id(1) == 0))
    def _():
        hp = lax.Precision.HIGHEST
        fr = fr_ref[...]
        hid = jnp.sin(fr * (jnp.dot(emb_ref[...], w1_ref[...], precision=hp, preferred_element_type=F32)
                            + b1_ref[...]))
        hid_ref[...] = jnp.sin(fr * (jnp.dot(hid, w2_ref[...], precision=hp, preferred_element_type=F32)
                                     + b2_ref[...]))
        cb_ref[...] = c_ref[...].astype(BF)
        sb_ref[...] = s_ref[...].astype(BF)

    hid = hid_ref[...]
    t = emb_ref[:, 0:1]
    fwd = _dot_3pass(hid, w3f_ref[...]) * jnp.exp(-jnp.exp(ldf_ref[...]) * t)
    bwd = _dot_3pass(hid, w3b_ref[...]) * jnp.exp(-jnp.exp(ldb_ref[...]) * t)
    row = lax.broadcasted_iota(jnp.int32, fwd.shape, 0)
    bwd = jnp.where(row == 0, 0.0, bwd)
    even = fwd + bwd
    odd = bwd - fwd
    wk = jnp.where(row == 0, 0.5 / seq, 1.0 / seq)
    hre_ref[...] = _dot(cb_ref[...], even.astype(BF)) * wk
    him_ref[...] = _dot(sb_ref[...], odd.astype(BF)) * wk
    alt = jnp.where((row & 1) == 0, 1.0, -1.0)
    hny_ref[...] = jnp.sum(alt * even, axis=0, keepdims=True) * (0.5 / seq)


def _hy_filter(seq, emb, w1, b1, w2, b2, freq, w3, log_decay, cmat, smat):
    dc = 512
    nj = D // dc
    small = [_const_spec(a.shape) for a in (emb, w1, b1, w2, b2, freq)]
    return pl.pallas_call(
        _hy_filter_kernel,
        grid=(HY_ORDER, nj),
        in_specs=small + [pl.BlockSpec((HY_FFN, dc), lambda o, j: (0, (2 * o) * nj + j)),
                          pl.BlockSpec((HY_FFN, dc), lambda o, j: (0, (2 * o + 1) * nj + j)),
                          pl.BlockSpec((1, dc), lambda o, j: (0, (2 * o) * nj + j)),
                          pl.BlockSpec((1, dc), lambda o, j: (0, (2 * o + 1) * nj + j)),
                          _const_spec((seq, seq)), _const_spec((seq, seq))],
        out_specs=[pl.BlockSpec((None, seq, dc), lambda o, j: (o, 0, j)),
                   pl.BlockSpec((None, seq, dc), lambda o, j: (o, 0, j)),
                   pl.BlockSpec((None, 1, dc), lambda o, j: (o, 0, j))],
        out_shape=[jax.ShapeDtypeStruct((HY_ORDER, seq, D), F32), jax.ShapeDtypeStruct((HY_ORDER, seq, D), F32),
                   jax.ShapeDtypeStruct((HY_ORDER, 1, D), F32)],
        scratch_shapes=[pltpu.VMEM((seq, HY_FFN), F32), pltpu.VMEM((seq, seq), BF), pltpu.VMEM((seq, seq), BF)],
        compiler_params=_cparams(2),
        name=f"hy_filter_{seq}",
    )(emb, w1, b1, w2, b2, freq, w3, w3, log_decay, log_decay, cmat, smat)


HY_SUB = 256


def _hy_conv_kernel(u0_ref, u1_ref, u2_ref, sw0_ref, sw1_ref, sw2_ref, sb0_ref, sb1_ref, sb2_ref,
                    fb_ref, hre_ref, him_ref, hny_ref, c_ref, s_ref, o_ref, cb_ref, sb_ref):
    seq, dc = u0_ref.shape

    @pl.when((pl.program_id(0) == 0) & (pl.program_id(1) == 0))
    def _():
        cb_ref[...] = c_ref[...].astype(BF)
        sb_ref[...] = s_ref[...].astype(BF)

    row = lax.broadcasted_iota(jnp.int32, (seq, HY_SUB), 0)
    alt = jnp.where((row & 1) == 0, 1.0, -1.0)

    def sub_tile(cols):
        def short_conv(u_ref, w_ref, b_ref):
            u = u_ref[:, cols]
            prev = jnp.where(row == 0, 0.0, pltpu.roll(u, 1, 0))
            nxt = jnp.where(row == seq - 1, 0.0, pltpu.roll(u, seq - 1, 0))
            return prev * w_ref[0:1, cols] + u * w_ref[1:2, cols] + nxt * w_ref[2:3, cols] + b_ref[:, cols]

        z = short_conv(u0_ref, sw0_ref, sb0_ref)
        gates = (short_conv(u1_ref, sw1_ref, sb1_ref), short_conv(u2_ref, sw2_ref, sb2_ref))
        yield
        for o in range(HY_ORDER):
            zb = z.astype(BF)
            zc, zs = _dot(cb_ref[...], zb), _dot(sb_ref[...], zb)
            yield
            hre, him = hre_ref[o, :, cols], him_ref[o, :, cols]
            p_re = (zc * hre + zs * him).astype(BF)
            p_im = (zc * him - zs * hre).astype(BF)
            y = _dot(cb_ref[...], p_re) - _dot(sb_ref[...], p_im)
            yield
            nyq = jnp.sum(alt * z, axis=0, keepdims=True) * hny_ref[o, :, cols]
            z = gates[o] * (y + alt * nyq + z * fb_ref[o:o + 1, cols])
        o_ref[:, cols] = z.astype(BF)

    _lockstep(sub_tile(slice(j * HY_SUB, (j + 1) * HY_SUB)) for j in range(dc // HY_SUB))


def _hy_conv(u, short_w, short_b, filter_bias, hre, him, hny, cmat, smat, seq, nbatch, row0, dc):
    nj = D // dc
    r0 = row0 // seq

    def part(p):
        return pl.BlockSpec((seq, dc), lambda j, b: (r0 + b, p * nj + j))

    def vec(rows, p):
        return pl.BlockSpec((rows, dc), lambda j, b: (0, p * nj + j))

    in_specs = ([part(p) for p in range(3)] + [vec(3, p) for p in range(3)] + [vec(1, p) for p in range(3)]
                + [pl.BlockSpec((HY_ORDER, dc), lambda j, b: (0, j)),
                   pl.BlockSpec((HY_ORDER, seq, dc), lambda j, b: (0, 0, j), pipeline_mode=pl.Buffered(1)),
                   pl.BlockSpec((HY_ORDER, seq, dc), lambda j, b: (0, 0, j), pipeline_mode=pl.Buffered(1)),
                   pl.BlockSpec((HY_ORDER, 1, dc), lambda j, b: (0, 0, j)),
                   _const_spec((seq, seq)), _const_spec((seq, seq))])
    return pl.pallas_call(
        _hy_conv_kernel,
        grid=(nj, nbatch),
        in_specs=in_specs,
        out_specs=pl.BlockSpec((seq, dc), lambda j, b: (b, j)),
        out_shape=jax.ShapeDtypeStruct((nbatch * seq, D), BF),
        scratch_shapes=[pltpu.VMEM((seq, seq), BF), pltpu.VMEM((seq, seq), BF)],
        compiler_params=_cparams(2),
        name=f"hy_conv_{seq}",
    )(u, u, u, short_w, short_w, short_w, short_b, short_b, short_b, filter_bias, hre, him, hny, cmat, smat)


def _dft_tables(seq):
    k = np.arange(seq, dtype=np.int64)
    ang = np.pi * ((k[:, None] * k[None, :]) % (2 * seq)) / seq
    return jnp.asarray(np.cos(ang), F32), jnp.asarray(np.sin(ang), F32)


def _hy_embedding(seq):
    t = np.arange(seq, dtype=np.float32) / np.float32(seq)
    ang = (2.0 * math.pi) * t[:, None] * np.arange(1, HY_BANDS + 1, dtype=np.float32)
    emb = np.concatenate([t[:, None], np.cos(ang), np.sin(ang)], axis=-1).astype(np.float32)
    return jnp.asarray(np.pad(emb, ((0, 0), (0, HY_EMB_PAD - HY_EMB))))


def _post_kernel(*refs, split_x, split_out, tm):
    oc_ref, ol_ref = refs[0:2]
    x_refs, refs = (refs[2:4], refs[4:]) if split_x else (refs[2:3], refs[3:])
    mod_ref, wo_ref, g1_ref, b1_ref, w1c_ref, w2c_ref, g2_ref, b2_ref = refs[0:8]
    outs, (w1_ref, w2_ref, h_ref, acc_ref) = refs[8:-4], refs[-4:]
    x1_ref = outs[0]
    step = pl.program_id(0)
    is_lat = _is_lat(tm, N_FF_CHUNKS - 1)
    nsub = tm // SUB_POST
    per = MLP_CHUNK // FF_CHUNK

    def rows(j):
        return slice(j * SUB_POST, (j + 1) * SUB_POST)

    def pick(c_ref, l_ref, j):
        return jnp.where(is_lat, l_ref[rows(j), :], c_ref[rows(j), :])

    def norm1(j):
        a = _dot(pick(oc_ref, ol_ref, j), wo_ref[...])
        x = pick(x_refs[0], x_refs[1], j) if split_x else x_refs[0][rows(j), :]
        x1 = _layer_norm(DN_ALPHA * x + mod_ref[2:3, :] * a, g1_ref[...], b1_ref[...])
        return x1, _modulate(x1, mod_ref, 3, 4)

    def norm2(x1, acc):
        return _layer_norm(DN_ALPHA * x1 + mod_ref[5:6, :] * acc, g2_ref[...], b2_ref[...])

    def mlp_chunk(h, c):
        a = jnp.concatenate([_dot(h, w1_ref[per * c + i]) for i in range(per)], axis=1)
        a = jnp.maximum(a, 0.0)
        return _dot((a * a).astype(BF), w2_ref[c])

    def write_branched(ys):
        yc_ref, yl_ref = outs

        @pl.when(jnp.logical_not(is_lat))
        def _():
            for j, y in enumerate(ys):
                yc_ref[rows(j), :] = y

        @pl.when(is_lat)
        def _():
            for j, y in enumerate(ys):
                yl_ref[rows(j), :] = y

    @pl.when(step < N_FF_CHUNKS)
    def _():
        w2_rows = pl.ds(pl.multiple_of((step % per) * FF_CHUNK, FF_CHUNK), FF_CHUNK)
        w1_ref[step] = w1c_ref[...].astype(BF)
        w2_ref[step // per, w2_rows, :] = w2c_ref[...].astype(BF)

        @pl.when(step == 0)
        def _():
            for j in range(nsub):
                x1_ref[rows(j), :], h_ref[rows(j), :] = norm1(j)
            acc_ref[...] = jnp.zeros_like(acc_ref)

        a = jnp.maximum(_dot(h_ref[...], w1_ref[step]), 0.0)
        acc_ref[...] += _dot((a * a).astype(BF), w2_ref[step // per, w2_rows, :])

        @pl.when(step == N_FF_CHUNKS - 1)
        def _():
            ys = [norm2(x1_ref[rows(j), :], acc_ref[rows(j), :]) for j in range(nsub)]
            if split_out:
                write_branched(ys)
            else:
                for j, y in enumerate(ys):
                    outs[0][rows(j), :] = y

    def token_tile():
        ys = []
        cur = norm1(0)
        prev = None
        for j in range(nsub):
            x1, h = cur
            acc = mlp_chunk(h, 0)
            if j + 1 < nsub:
                cur = norm1(j + 1)
            if prev is not None:
                ys.append(norm2(*prev))
                if not split_out:
                    outs[0][rows(j - 1), :] = ys[-1]
            for c in range(1, D_FF // MLP_CHUNK):
                acc = acc + mlp_chunk(h, c)
            prev = (x1, acc)
        ys.append(norm2(*prev))
        if split_out:
            write_branched(ys)
        else:
            outs[0][rows(nsub - 1), :] = ys[-1]

    pl.when(step >= N_FF_CHUNKS)(token_tile)


def _post(o_ctx, o_lat, xs, mods, layer, w_o, g1, b1, w1, w2, g2, b2, split_out):
    tm, off = TM_POST, N_FF_CHUNKS - 1
    split_x = len(xs) == 2
    x_specs = [_ctx_spec(D, tm, off), _lat_spec(D, tm, off)] if split_x else [_tok_spec(D, tm, off)]
    vec = _const_spec((1, D))
    if split_out:
        out_specs = [_ctx_spec(D, tm, off), _lat_spec(D, tm, off)]
        out_shape = [jax.ShapeDtypeStruct((T_CTX, D), F32), jax.ShapeDtypeStruct((T_LAT, D), F32)]
    else:
        out_specs = _tok_spec(D, tm, off)
        out_shape = jax.ShapeDtypeStruct((T, D), F32)

    def chunk(i):
        return jnp.minimum(i, N_FF_CHUNKS - 1)

    return pl.pallas_call(
        functools.partial(_post_kernel, split_x=split_x, split_out=split_out, tm=tm),
        grid=(off + T // tm,),
        in_specs=[_ctx_spec(D, tm, off), _lat_spec(D, tm, off)] + x_specs + [
            _mod_spec(layer, tm, off), _const_spec((D, D)), vec, vec,
            pl.BlockSpec((None, D, FF_CHUNK), lambda i: (layer, 0, chunk(i))),
            pl.BlockSpec((None, FF_CHUNK, D), lambda i: (layer, chunk(i), 0)), vec, vec],
        out_specs=out_specs,
        out_shape=out_shape,
        scratch_shapes=[pltpu.VMEM((N_FF_CHUNKS, D, FF_CHUNK), BF), pltpu.VMEM((D_FF // MLP_CHUNK, MLP_CHUNK, D), BF),
                        pltpu.VMEM((tm, D), BF), pltpu.VMEM((tm, D), F32)],
        compiler_params=_cparams(1, POST_VMEM_LIMIT),
        name=f"post_l{layer}",
    )(o_ctx, o_lat, *xs, mods, w_o, g1, b1, w1, w2, g2, b2)


def _rope_tables():
    n = HEAD_DIM // 4
    pos = np.arange(DEC_SEQ)
    inv = (np.float32(ROPE_BASE) ** (-np.arange(n, dtype=np.float32) / np.float32(n))).astype(np.float32)
    ang_r = ((pos // GRID_W).astype(np.float32)[:, None] * inv).astype(np.float32)
    ang_c = ((pos % GRID_W).astype(np.float32)[:, None] * inv).astype(np.float32)
    cr, sr, cc, sc = np.cos(ang_r), np.sin(ang_r), np.cos(ang_c), np.sin(ang_c)
    a = np.tile(np.concatenate([cr, cr, cc, cc], axis=-1), (1, D // HEAD_DIM))
    b = np.tile(np.concatenate([-sr, sr, -sc, sc], axis=-1), (1, D // HEAD_DIM))
    a = np.concatenate([a, np.ones((TM, D), np.float32)], axis=0)
    b = np.concatenate([b, np.zeros((TM, D), np.float32)], axis=0)
    return jnp.asarray(a, F32), jnp.asarray(b, F32)


def kernel(x_prompt, x_sample, c, cache_da_k, cache_da_v, cache_na_k, cache_na_v, cache_gq_k, cache_gq_v, c_ctx, ada_w, ada_b, ln_g, ln_b, mlp_w1, mlp_w2, da_w_qkv, da_w_o, da_lambda, da_subln_g, na_w_qkv, na_w_o, na_rel_bias, gq_w_qkv, gq_w_o, gq_q_norm, gq_k_norm, hy_w_in, hy_short_w, hy_short_b, hy_ffn_w1, hy_ffn_b1, hy_ffn_w2, hy_ffn_b2, hy_ffn_freq, hy_ffn_w3, hy_log_decay, hy_filter_bias, hy_w_o):
    cvec = jnp.concatenate([c_ctx[None, :], c, jnp.zeros((MOD_ROWS - 1 - DEC_BATCH, D), F32)], axis=0)
    mods = _mods(cvec, ada_w, ada_b)
    rope_a, rope_b = _rope_tables()

    def finish(o_ctx, o_lat, xs, layer, w_o, split_out=False):
        return _post(o_ctx, o_lat, xs, mods, layer, w_o.astype(BF), ln_g[layer, 0][None], ln_b[layer, 0][None],
                     mlp_w1, mlp_w2, ln_g[layer, 1][None], ln_b[layer, 1][None], split_out)

    xs = (x_prompt.reshape(T_CTX, D), x_sample.reshape(T_LAT, D))
    qb, kb, vb, ks, vs = _da_proj(*xs, mods, 0, da_w_qkv[0].astype(BF), rope_a, rope_b)
    state_da_k = ks.reshape(BATCH, 1, SEQ, DA_HEADS, 2 * HEAD_DIM)
    state_da_v = vs.reshape(BATCH, 1, SEQ, DA_HEADS, 2 * HEAD_DIM)
    o_ctx, o_lat = _da_attention(qb, kb, vb, cache_da_k, cache_da_v, da_lambda[0], da_subln_g[0][None], 0)
    x = finish(o_ctx, o_lat, xs, 0, da_w_o[0])

    qb, kb, vb, ks, vs = _na_proj(x, mods, 1, na_w_qkv[0].astype(BF))
    state_na_k, state_na_v = _untranspose_state(ks, NA_HEADS), _untranspose_state(vs, NA_HEADS)
    onehot, neg, mask = _na_constants()
    bias_tab = _na_bias_table(na_rel_bias[0], onehot, neg)
    o_ctx = _na_ctx_attention(qb, kb, vb)
    o_lat = _na_lat_attention(qb, kb, vb, _features_major(cache_na_k), _features_major(cache_na_v), bias_tab, mask)
    x = finish(o_ctx, o_lat, (x,), 1, na_w_o[0])

    g_mat = jnp.asarray(np.kron(np.eye(GN_BLOCK // HEAD_DIM), np.full((HEAD_DIM, HEAD_DIM), 1.0 / HEAD_DIM)), BF)
    qb, kb, vb, ks, vs = _gq_proj(x, mods, 2, gq_w_qkv[0].astype(BF), g_mat,
                                  jnp.tile(gq_q_norm[0], GQ_HEADS)[None], jnp.tile(gq_k_norm[0], GQ_KV_HEADS)[None],
                                  rope_a, rope_b)
    state_gq_k, state_gq_v = _untranspose_state(ks, GQ_KV_HEADS), _untranspose_state(vs, GQ_KV_HEADS)
    o_ctx, o_lat = _gq_attention(qb, kb, vb, _features_major(cache_gq_k), _features_major(cache_gq_v))
    x = finish(o_ctx, o_lat, (x,), 2, gq_w_o[0])

    u = _hy_proj(x, mods, 3, hy_w_in[0].astype(BF))
    w1 = jnp.pad(hy_ffn_w1[0], ((0, HY_EMB_PAD - HY_EMB), (0, 0)))
    zs = []
    for seq, nbatch, row0, dc in ((SEQ, BATCH, 0, D), (DEC_SEQ, DEC_BATCH, T_CTX, 512)):
        cmat, smat = _dft_tables(seq)
        hre, him, hny = _hy_filter(seq, _hy_embedding(seq), w1, hy_ffn_b1[0][None], hy_ffn_w2[0], hy_ffn_b2[0][None],
                                   hy_ffn_freq[0][None], hy_ffn_w3[0], hy_log_decay[0][None], cmat, smat)
        zs.append(_hy_conv(u, hy_short_w[0], hy_short_b[0][None], hy_filter_bias[0], hre, him, hny, cmat, smat,
                           seq, nbatch, row0, dc))
    y_ctx, y_lat = finish(zs[0], zs[1], (x,), 3, hy_w_o[0], split_out=True)

    return (y_ctx.reshape(BATCH, SEQ, D), y_lat.reshape(DEC_BATCH, DEC_SEQ, D),
            state_da_k, state_da_v, state_na_k, state_na_v, state_gq_k, state_gq_v)
```

```python
import functools
import math

import numpy as np
import jax
import jax.numpy as jnp
from jax import lax
from jax.experimental import pallas as pl
from jax.experimental.pallas import tpu as pltpu

F32 = jnp.float32
BF = jnp.bfloat16

D = 1024
BATCH = 16
SEQ = 256
DEC_BATCH = 8
DEC_SEQ = 1024
PAST = 256
DEPTH = 4
GRID_W = 64
GRID_ROWS = DEC_SEQ // GRID_W
D_FF = 4 * D
T_CTX = BATCH * SEQ
T_LAT = DEC_BATCH * DEC_SEQ
T = T_CTX + T_LAT
HEAD_DIM = 64
ATT_SCALE = HEAD_DIM ** -0.5
LOG2E = math.log2(math.e)
Q_SCALE = ATT_SCALE * LOG2E
DA_HEADS = 8
NA_HEADS = 16
NA_WIN_ROWS = 8
NA_WIN_COLS = 16
GQ_HEADS = 16
GQ_KV_HEADS = 4
HY_ORDER = 2
HY_BANDS = 16
HY_EMB = 1 + 2 * HY_BANDS
HY_EMB_PAD = 40
HY_FFN = 64
ROPE_BASE = 10000.0
LN_EPS = 1e-5
RMS_EPS = 1e-6
DN_ALPHA = (2 * DEPTH) ** 0.25
NEG_INF = -1e30

LANES = 128
TM = 512
TM_POST = 512
FF_CHUNK = 512
MLP_CHUNK = 1024
N_FF_CHUNKS = D_FF // FF_CHUNK
SUB_POST = 256
N_CTX_TILES = T_CTX // TM
N_TILES = T // TM
TQ = 512
CTX_BATCHES_PER_STEP = 4
CTX_ROWS = CTX_BATCHES_PER_STEP * SEQ
MOD_ROWS = 16
VMEM_LIMIT = 56 * 1024 * 1024
POST_VMEM_LIMIT = 58 * 1024 * 1024


def _cparams(n_axes, vmem_limit=VMEM_LIMIT):
    return pltpu.CompilerParams(dimension_semantics=("arbitrary",) * n_axes,
                                vmem_limit_bytes=vmem_limit)


def _dot(a, b):
    return jnp.dot(a, b, preferred_element_type=F32)


def _dot_nt(a, b):
    return lax.dot_general(a, b, (((1,), (1,)), ((), ())), preferred_element_type=F32)


def _const_spec(shape):
    nd = len(shape)
    return pl.BlockSpec(shape, lambda *_: (0,) * nd, pipeline_mode=pl.Buffered(1))


def _mod_spec(layer, tm=TM, off=0):
    nctx = T_CTX // tm

    def row(i):
        t = jnp.maximum(i - off, 0)
        return jnp.where(t < nctx, 0, 1 + (t - nctx) // (DEC_SEQ // tm))

    return pl.BlockSpec((None, None, 6, D), lambda i: (layer, row(i), 0, 0))


def _tok_spec(width, tm=TM, off=0):
    return pl.BlockSpec((tm, width), lambda i: (jnp.maximum(i - off, 0), 0))


def _ctx_spec(width, tm=TM, off=0):
    return pl.BlockSpec((tm, width), lambda i: (jnp.clip(i - off, 0, T_CTX // tm - 1), 0))


def _lat_spec(width, tm=TM, off=0):
    return pl.BlockSpec((tm, width), lambda i: (jnp.maximum(i - off - T_CTX // tm, 0), 0))


def _is_lat(tm=TM, off=0):
    return pl.program_id(0) >= off + T_CTX // tm


def _pick(ctx_ref, lat_ref):
    return jnp.where(_is_lat(), lat_ref[...], ctx_ref[...])


def _layer_norm(r, g, b):
    mu = jnp.mean(r, axis=-1, keepdims=True)
    c = r - mu
    var = jnp.mean(c * c, axis=-1, keepdims=True)
    return c * lax.rsqrt(var + LN_EPS) * g + b


def _mods_kernel(c_ref, w_ref, b_ref, o_ref):
    c = c_ref[...]
    s = (c / (1.0 + jnp.exp(-c))).astype(BF)
    o_ref[...] = _dot(s, w_ref[...].astype(BF)) + b_ref[...]


def _mods(cvec, ada_w, ada_b):
    tn = 1536
    out = pl.pallas_call(
        _mods_kernel,
        grid=(DEPTH, 6 * D // tn),
        in_specs=[pl.BlockSpec((MOD_ROWS, D), lambda l, n: (0, 0)),
                  pl.BlockSpec((None, D, tn), lambda l, n: (l, 0, n)),
                  pl.BlockSpec((None, 1, tn), lambda l, n: (l, 0, n))],
        out_specs=pl.BlockSpec((None, MOD_ROWS, tn), lambda l, n: (l, 0, n)),
        out_shape=jax.ShapeDtypeStruct((DEPTH, MOD_ROWS, 6 * D), F32),
        compiler_params=_cparams(2),
        name="adaln_mods",
    )(cvec, ada_w, ada_b.reshape(DEPTH, 1, 6 * D))
    return out.reshape(DEPTH, MOD_ROWS, 6, D)


def _modulate(x, mod_ref, shift, scale):
    return (x * (1.0 + mod_ref[scale:scale + 1, :]) + mod_ref[shift:shift + 1, :]).astype(BF)


def _rope(x, a, b):
    n = x.shape[1]
    lane = lax.broadcasted_iota(jnp.int32, x.shape, 1)
    partner = jnp.where((lane & 16) == 0, pltpu.roll(x, n - 16, 1), pltpu.roll(x, 16, 1))
    return x * a + partner * b


def _rope_spec(width):
    per = DEC_SEQ // TM
    return pl.BlockSpec((TM, width), lambda i: (jnp.where(i < N_CTX_TILES, per, (i - N_CTX_TILES) % per), 0))


def _lockstep(gens):
    waiting, active = list(gens), []
    while waiting or active:
        if waiting:
            active.append(waiting.pop(0))
        active = [g for g in active if next(g, True) is None]


def _store_state(k, v, ks_ref, vs_ref, transposed):
    @pl.when(jnp.logical_not(_is_lat()))
    def _():
        if not transposed:
            ks_ref[...] = k
            vs_ref[...] = v
        else:
            n = k.shape[0]
            for xt, ref in ((k, ks_ref), (v, vs_ref)):
                for j in range(TM // SEQ):
                    ref[j * n:(j + 1) * n, :] = xt[:, j * SEQ:(j + 1) * SEQ]


def _qkv_out(nq, nk, transposed_state):
    specs = [_tok_spec(nq), _tok_spec(nk), _tok_spec(nk)]
    shapes = [jax.ShapeDtypeStruct((T, nq), BF), jax.ShapeDtypeStruct((T, nk), BF), jax.ShapeDtypeStruct((T, nk), BF)]
    if transposed_state:
        rows = (TM // SEQ) * nk
        specs += [pl.BlockSpec((rows, SEQ), lambda i: (jnp.minimum(i, N_CTX_TILES - 1), 0))] * 2
        shapes += [jax.ShapeDtypeStruct((BATCH * nk, SEQ), F32)] * 2
    else:
        specs += [_ctx_spec(nk)] * 2
        shapes += [jax.ShapeDtypeStruct((T_CTX, nk), F32)] * 2
    return specs, shapes


def _features_major(cache):
    b, _, past, heads, dh = cache.shape
    return cache.transpose(0, 1, 3, 4, 2).reshape(b, heads * dh, past)


def _untranspose_state(st, heads):
    return st.reshape(BATCH, heads, HEAD_DIM, SEQ).transpose(0, 3, 1, 2)[:, None]


def _da_proj_kernel(xc_ref, xl_ref, mod_ref, w_ref, ra_ref, rb_ref, qb_ref, kb_ref, vb_ref, ks_ref, vs_ref):
    h = _modulate(_pick(xc_ref, xl_ref), mod_ref, 0, 1)
    a, b = ra_ref[...], rb_ref[...]
    q = _dot(h, w_ref[:, 0:D])
    k = _dot(h, w_ref[:, D:2 * D])
    qb_ref[...] = (_rope(q, a, b) * Q_SCALE).astype(BF)
    v = _dot(h, w_ref[:, 2 * D:3 * D])
    kb_ref[...] = _rope(k, a, b).astype(BF)
    vb_ref[...] = v.astype(BF)
    _store_state(k, v, ks_ref, vs_ref, False)


def _da_proj(x_ctx, x_lat, mods, layer, w, rope_a, rope_b):
    specs, shapes = _qkv_out(D, D, False)
    return pl.pallas_call(
        _da_proj_kernel,
        grid=(N_TILES,),
        in_specs=[_ctx_spec(D), _lat_spec(D), _mod_spec(layer), _const_spec((D, 3 * D)),
                  _rope_spec(D), _rope_spec(D)],
        out_specs=specs, out_shape=shapes,
        compiler_params=_cparams(1),
        name=f"da_proj_l{layer}",
    )(x_ctx, x_lat, mods, w, rope_a, rope_b)


def _na_proj_kernel(x_ref, mod_ref, w_ref, qb_ref, kb_ref, vb_ref, ks_ref, vs_ref):
    h = _modulate(x_ref[...], mod_ref, 0, 1)
    v = _dot(h, w_ref[:, 2 * D:3 * D])
    k = _dot(h, w_ref[:, D:2 * D])
    vb_ref[...] = v.astype(BF)
    vt = v.T
    q = _dot(h, w_ref[:, 0:D])
    kb_ref[...] = k.astype(BF)
    kt = k.T
    qb_ref[...] = (q * Q_SCALE).astype(BF)
    _store_state(kt, vt, ks_ref, vs_ref, True)


def _na_proj(x, mods, layer, w):
    specs, shapes = _qkv_out(D, D, True)
    return pl.pallas_call(
        _na_proj_kernel,
        grid=(N_TILES,),
        in_specs=[_tok_spec(D), _mod_spec(layer), _const_spec((D, 3 * D))],
        out_specs=specs, out_shape=shapes,
        compiler_params=_cparams(1),
        name=f"na_proj_l{layer}",
    )(x, mods, w)


GN_BLOCK = 256


def _head_rms(x, g_ref, gain):
    x2 = x * x
    hi = x2.astype(BF)
    lo = (x2 - hi.astype(F32)).astype(BF)
    g = g_ref[...]
    ms = jnp.concatenate(
        [_dot(hi[:, j:j + GN_BLOCK], g) + _dot(lo[:, j:j + GN_BLOCK], g) for j in range(0, x.shape[1], GN_BLOCK)],
        axis=1)
    return x * lax.rsqrt(ms + RMS_EPS) * gain


def _gq_proj_kernel(x_ref, mod_ref, w_ref, g_ref, qn_ref, kn_ref, ra_ref, rb_ref,
                    qb_ref, kb_ref, vb_ref, ks_ref, vs_ref):
    nq, nk = GQ_HEADS * HEAD_DIM, GQ_KV_HEADS * HEAD_DIM
    h = _modulate(x_ref[...], mod_ref, 0, 1)

    def finish_q(c, qc):
        cols = slice(c * GN_BLOCK, (c + 1) * GN_BLOCK)
        qc = _head_rms(qc, g_ref, qn_ref[:, cols])
        qb_ref[:, cols] = (_rope(qc, ra_ref[:, cols], rb_ref[:, cols]) * Q_SCALE).astype(BF)

    k = _dot(h, w_ref[:, nq:nq + nk])
    v = _dot(h, w_ref[:, nq + nk:nq + 2 * nk])
    prev = None
    for c in range(nq // GN_BLOCK):
        qc = _dot(h, w_ref[:, c * GN_BLOCK:(c + 1) * GN_BLOCK])
        if c == 0:
            k = _head_rms(k, g_ref, kn_ref[...])
            kb_ref[...] = _rope(k, ra_ref[:, 0:nk], rb_ref[:, 0:nk]).astype(BF)
            vb_ref[...] = v.astype(BF)
            kt, vt = k.T, v.T
        else:
            finish_q(*prev)
        prev = (c, qc)
    finish_q(*prev)
    _store_state(kt, vt, ks_ref, vs_ref, True)


def _gq_proj(x, mods, layer, w, g_mat, qn, kn, rope_a, rope_b):
    nq, nk = GQ_HEADS * HEAD_DIM, GQ_KV_HEADS * HEAD_DIM
    specs, shapes = _qkv_out(nq, nk, True)
    return pl.pallas_call(
        _gq_proj_kernel,
        grid=(N_TILES,),
        in_specs=[_tok_spec(D), _mod_spec(layer), _const_spec((D, nq + 2 * nk)),
                  _const_spec((GN_BLOCK, GN_BLOCK)), _const_spec((1, nq)), _const_spec((1, nk)),
                  _rope_spec(D), _rope_spec(D)],
        out_specs=specs, out_shape=shapes,
        compiler_params=_cparams(1),
        name=f"gq_proj_l{layer}",
    )(x, mods, w, g_mat, qn, kn, rope_a, rope_b)


def _hy_proj_kernel(x_ref, mod_ref, w_ref, u_ref):
    h = _modulate(x_ref[...], mod_ref, 0, 1)
    for c in range(HY_ORDER + 1):
        u_ref[:, c * D:(c + 1) * D] = _dot(h, w_ref[:, c * D:(c + 1) * D])


def _hy_proj(x, mods, layer, w):
    n = (HY_ORDER + 1) * D
    return pl.pallas_call(
        _hy_proj_kernel,
        grid=(N_TILES,),
        in_specs=[_tok_spec(D), _mod_spec(layer), _const_spec((D, n))],
        out_specs=_tok_spec(n),
        out_shape=jax.ShapeDtypeStruct((T, n), F32),
        compiler_params=_cparams(1),
        name=f"hy_proj_l{layer}",
    )(x, mods, w)


def _scores(qm, segs):
    return [_dot(qm, seg[0]) if len(seg) == 3 else _dot_nt(qm, seg[0]) for seg in segs]


def _softmax_finish(scores, segs):
    m = scores[0].max(axis=-1, keepdims=True)
    for s in scores[1:]:
        m = jnp.maximum(m, s.max(axis=-1, keepdims=True))
    den = None
    out = None
    for s, seg in zip(scores, segs):
        e = jnp.exp2(s - m)
        d = e.sum(axis=-1, keepdims=True)
        o = _dot_nt(e.astype(BF), seg[1]) if len(seg) == 3 else _dot(e.astype(BF), seg[1])
        den = d if den is None else den + d
        out = o if out is None else out + o
    return out / den


def _stack_halves(q, keep):
    return jnp.concatenate([q * keep[0], q * keep[1]], axis=0)


def _pipelined(jobs, score_fn, finish_fn):
    nxt = score_fn(jobs[0])
    for n, job in enumerate(jobs):
        cur, nxt = nxt, (score_fn(jobs[n + 1]) if n + 1 < len(jobs) else None)
        finish_fn(job, cur)


def _lane_half(shape):
    return lax.broadcasted_iota(jnp.int32, shape, 1) // HEAD_DIM


def _half_keep(half):
    return tuple(jnp.where(half == a, 1.0, 0.0).astype(BF) for a in (0, 1))


def _da_attn_kernel(*refs, has_cache, lam_init):
    if has_cache:
        q_ref, k_ref, v_ref, ck_ref, cv_ref, lam_ref, g_ref, o_ref = refs
    else:
        q_ref, k_ref, v_ref, lam_ref, g_ref, o_ref = refs
    lp = lam_ref[...]
    lam = (jnp.exp(jnp.sum(lp[0:1] * lp[1:2], axis=-1, keepdims=True))
           - jnp.exp(jnp.sum(lp[2:3] * lp[3:4], axis=-1, keepdims=True)) + lam_init)
    gain = g_ref[...] * (1.0 - lam_init)
    w = 2 * HEAD_DIM
    nheads = k_ref.shape[1] // w
    tq = min(TQ, q_ref.shape[0]) if has_cache else SEQ
    keep = _half_keep(_lane_half((tq, w)))
    caches = []
    if has_cache:
        for hd in range(nheads):
            head = pl.program_id(1) * nheads + hd
            caches.append((ck_ref[:, head, :].astype(BF), cv_ref[:, head, :].astype(BF)))

    def seg(hd, t):
        cols = slice(hd * w, (hd + 1) * w)
        if has_cache:
            return [(k_ref[:, cols], v_ref[:, cols]), caches[hd]]
        return [(k_ref[t * tq:(t + 1) * tq, cols], v_ref[t * tq:(t + 1) * tq, cols])]

    jobs = [(hd, t, a) for t in range(q_ref.shape[0] // tq) for hd in range(nheads) for a in (0, 1)]
    first = {}

    def score_fn(job):
        hd, t, a = job
        return _scores(q_ref[t * tq:(t + 1) * tq, hd * w:(hd + 1) * w] * keep[a], seg(hd, t))

    def finish_fn(job, scores):
        hd, t, a = job
        o = _softmax_finish(scores, seg(hd, t))
        if a == 0:
            first[0] = o
            return
        o = first[0] - lam * o
        ms = jnp.mean(o * o, axis=-1, keepdims=True)
        o_ref[t * tq:(t + 1) * tq, hd * w:(hd + 1) * w] = (o * lax.rsqrt(ms + RMS_EPS) * gain).astype(BF)

    _pipelined(jobs, score_fn, finish_fn)


DA_LAT_HEADS_PER_STEP = 2


def _da_attention(qb, kb, vb, cache_k, cache_v, lam_p, subln_g, layer_idx):
    lam_init = 0.8 - 0.6 * math.exp(-0.3 * layer_idx)
    w = 2 * HEAD_DIM
    small = [pl.BlockSpec((4, HEAD_DIM), lambda *_: (0, 0)), pl.BlockSpec((1, w), lambda *_: (0, 0))]
    o_ctx = pl.pallas_call(
        functools.partial(_da_attn_kernel, has_cache=False, lam_init=lam_init),
        grid=(BATCH,),
        in_specs=[pl.BlockSpec((SEQ, D), lambda b: (b, 0))] * 3 + small,
        out_specs=pl.BlockSpec((SEQ, D), lambda b: (b, 0)),
        out_shape=jax.ShapeDtypeStruct((T_CTX, D), BF),
        compiler_params=_cparams(1),
        name="da_attn_ctx",
    )(qb, kb, vb, lam_p, subln_g)
    k0 = T_CTX // DEC_SEQ
    hw = DA_LAT_HEADS_PER_STEP * w
    tok = pl.BlockSpec((DEC_SEQ, hw), lambda b, h: (k0 + b, h))
    c_spec = pl.BlockSpec((None, None, PAST, DA_HEADS, w), lambda b, h: (b, 0, 0, 0, 0))
    o_lat = pl.pallas_call(
        functools.partial(_da_attn_kernel, has_cache=True, lam_init=lam_init),
        grid=(DEC_BATCH, DA_HEADS // DA_LAT_HEADS_PER_STEP),
        in_specs=[tok, tok, tok, c_spec, c_spec] + small,
        out_specs=pl.BlockSpec((DEC_SEQ, hw), lambda b, h: (b, h)),
        out_shape=jax.ShapeDtypeStruct((T_LAT, D), BF),
        compiler_params=_cparams(2),
        name="da_attn_lat",
    )(qb, kb, vb, cache_k, cache_v, lam_p, subln_g)
    return o_ctx, o_lat


def _na_ctx_kernel(q_ref, k_ref, v_ref, o_ref):
    half = _lane_half((SEQ, LANES))
    keep = _half_keep(half)

    def block(job):
        b, p = job
        return slice(b * SEQ, (b + 1) * SEQ), slice(p * LANES, (p + 1) * LANES)

    def seg(job):
        return [(k_ref[block(job)], v_ref[block(job)])]

    def score_fn(job):
        return _scores(_stack_halves(q_ref[block(job)], keep), seg(job))

    def finish_fn(job, scores):
        o = _softmax_finish(scores, seg(job))
        o_ref[block(job)] = jnp.where(half == 0, o[0:SEQ], o[SEQ:2 * SEQ]).astype(BF)

    jobs = [(b, p) for b in range(q_ref.shape[0] // SEQ) for p in range(NA_HEADS // 2)]
    _pipelined(jobs, score_fn, finish_fn)


def _na_ctx_attention(qb, kb, vb):
    spec = pl.BlockSpec((CTX_ROWS, D), lambda b: (b, 0))
    return pl.pallas_call(
        _na_ctx_kernel,
        grid=(BATCH // CTX_BATCHES_PER_STEP,),
        in_specs=[spec] * 3,
        out_specs=spec,
        out_shape=jax.ShapeDtypeStruct((T_CTX, D), BF),
        compiler_params=_cparams(1),
        name="na_attn_ctx",
    )(qb, kb, vb)


NA_TILES = ((0, (0, 2, 4, 6)), (4, (0, 2, 4, 6, 8, 10)), (8, (4, 6, 8, 10, 12, 14)), (12, (8, 10, 12, 14)))
NA_MAX_CHUNKS = 6
NA_BIAS_BLOCKS = 2 * NA_WIN_ROWS - 2


NA_LAT_PAIRS_PER_STEP = 4


def _na_row_window(qr):
    kr = min(NA_WIN_ROWS, GRID_ROWS)
    return min(max(qr - kr // 2, 0), GRID_ROWS - kr), kr


def _na_chunk_all_valid(r0, kr0):
    return all(_na_row_window(qr)[0] <= kr < sum(_na_row_window(qr)) for qr in range(r0, r0 + 4) for kr in (kr0, kr0 + 1))


def _na_lat_kernel(q_ref, k_ref, v_ref, ck_ref, cv_ref, w_ref, m_ref, o_ref):
    rows = 4 * GRID_W
    half = _lane_half((rows, LANES))
    keep = _half_keep(half)
    caches = [(ck_ref[p * LANES:(p + 1) * LANES, :].astype(BF), cv_ref[p * LANES:(p + 1) * LANES, :].astype(BF))
              for p in range(NA_LAT_PAIRS_PER_STEP)]
    jobs = [(p, i) for p in range(NA_LAT_PAIRS_PER_STEP) for i in range(len(NA_TILES))]

    def key_rows(i):
        chunks = NA_TILES[i][1]
        return slice(chunks[0] * GRID_W, chunks[0] * GRID_W + len(chunks) * LANES)

    def score_fn(job):
        p, i = job
        cols = slice(p * LANES, (p + 1) * LANES)
        r0, chunks = NA_TILES[i]
        qm = _stack_halves(q_ref[i * rows:(i + 1) * rows, cols], keep)
        def bias_chunk(a, c, kr):
            blk = w_ref[p, a, (6 - kr + r0) * GRID_W:(6 - kr + r0) * GRID_W + rows, :]
            if _na_chunk_all_valid(r0, kr):
                return blk
            return blk + m_ref[i, :, c * LANES:(c + 1) * LANES]

        bias = jnp.concatenate(
            [jnp.concatenate([bias_chunk(a, c, kr) for c, kr in enumerate(chunks)], axis=1) for a in (0, 1)], axis=0)
        return [_dot_nt(qm, k_ref[key_rows(i), cols]) + bias, _dot(qm, caches[p][0])]

    def finish_fn(job, scores):
        p, i = job
        cols = slice(p * LANES, (p + 1) * LANES)
        o = _softmax_finish(scores, [(None, v_ref[key_rows(i), cols]), (None, caches[p][1], True)])
        o_ref[i * rows:(i + 1) * rows, cols] = jnp.where(half == 0, o[0:rows], o[rows:2 * rows]).astype(BF)

    _pipelined(jobs, score_fn, finish_fn)


def _na_lat_attention(qb, kb, vb, cache_k, cache_v, bias_tab, mask_tab):
    k0 = T_CTX // DEC_SEQ
    npair = NA_LAT_PAIRS_PER_STEP
    tok = pl.BlockSpec((DEC_SEQ, npair * LANES), lambda b, p: (k0 + b, p))
    c_spec = pl.BlockSpec((None, npair * LANES, PAST), lambda b, p: (b, p, 0))
    return pl.pallas_call(
        _na_lat_kernel,
        grid=(DEC_BATCH, NA_HEADS // 2 // npair),
        in_specs=[tok, tok, tok, c_spec, c_spec,
                  pl.BlockSpec((npair, 2, NA_BIAS_BLOCKS * GRID_W, LANES), lambda b, p: (p, 0, 0, 0)),
                  _const_spec(mask_tab.shape)],
        out_specs=pl.BlockSpec((DEC_SEQ, npair * LANES), lambda b, p: (b, p)),
        out_shape=jax.ShapeDtypeStruct((T_LAT, D), BF),
        compiler_params=_cparams(2),
        name="na_attn_lat",
    )(qb, kb, vb, cache_k, cache_v, bias_tab, mask_tab)


def _na_bias_kernel(t_ref, r_ref, n_ref, o_ref):
    t = t_ref[...]
    t1 = t.astype(BF)
    r1 = t - t1.astype(F32)
    t2 = r1.astype(BF)
    t3 = (r1 - t2.astype(F32)).astype(BF)
    r = r_ref[...]
    res = (_dot(t1, r) + _dot(t2, r) + _dot(t3, r) + n_ref[...]) * LOG2E
    for qc in range(GRID_W):
        o_ref[pl.ds(qc, t.shape[0], stride=GRID_W), :] = res[:, qc * LANES:(qc + 1) * LANES]


def _na_bias_table(rel_bias, onehot, neg):
    nrel = 2 * NA_WIN_COLS
    idx = 13 - np.arange(NA_BIAS_BLOCKS)[:, None] + np.arange(2)[None, :]
    t = jnp.pad(rel_bias[:, idx, :], ((0, 0), (0, 0), (0, 0), (0, 1)))
    t = t.reshape(NA_HEADS * NA_BIAS_BLOCKS, 2 * nrel)
    n = GRID_W * LANES
    out = pl.pallas_call(
        _na_bias_kernel,
        grid=(1,),
        in_specs=[pl.BlockSpec(t.shape, lambda j: (0, 0)),
                  pl.BlockSpec((2 * nrel, n), lambda j: (0, 0)),
                  pl.BlockSpec((1, n), lambda j: (0, 0))],
        out_specs=pl.BlockSpec((t.shape[0] * GRID_W, LANES), lambda j: (0, 0)),
        out_shape=jax.ShapeDtypeStruct((t.shape[0] * GRID_W, LANES), F32),
        compiler_params=_cparams(1),
        name="na_bias_table",
    )(t, onehot, neg)
    return out.reshape(NA_HEADS // 2, 2, NA_BIAS_BLOCKS * GRID_W, LANES)


def _na_constants():
    nrel = 2 * NA_WIN_COLS
    qc = np.arange(GRID_W)[:, None]
    kc = np.arange(GRID_W)[None, :]
    rel = np.clip(kc - qc, -(NA_WIN_COLS - 1), NA_WIN_COLS - 1) + NA_WIN_COLS - 1
    cs = np.clip(qc - NA_WIN_COLS // 2, 0, GRID_W - NA_WIN_COLS)
    col_in = (kc >= cs) & (kc < cs + NA_WIN_COLS)
    onehot = np.zeros((2, nrel, GRID_W, 2, GRID_W), np.float32)
    for hf in range(2):
        onehot[hf, rel, qc, hf, kc] = 1.0
    neg = np.where(col_in, 0.0, NEG_INF).astype(np.float32)
    neg = np.broadcast_to(neg[:, None, :], (GRID_W, 2, GRID_W)).reshape(1, -1)
    rows = 4 * GRID_W
    mask = np.full((len(NA_TILES), rows, NA_MAX_CHUNKS * LANES), NEG_INF, np.float32)
    kr = min(NA_WIN_ROWS, GRID_ROWS)
    for i, (r0, chunks) in enumerate(NA_TILES):
        qr = r0 + np.arange(rows)[:, None] // GRID_W
        rs = np.clip(qr - kr // 2, 0, GRID_ROWS - kr)
        for c, krow0 in enumerate(chunks):
            krow = krow0 + np.arange(LANES)[None, :] // GRID_W
            mask[i, :, c * LANES:(c + 1) * LANES] = np.where((krow >= rs) & (krow < rs + kr), 0.0, NEG_INF)
    return (jnp.asarray(onehot.reshape(2 * nrel, GRID_W * LANES), BF), jnp.asarray(neg), jnp.asarray(mask))


def _gq_attn_kernel(*refs, has_cache):
    if has_cache:
        q_ref, k_ref, v_ref, ck_ref, cv_ref, o_ref = refs
    else:
        q_ref, k_ref, v_ref, o_ref = refs
    group = GQ_HEADS // GQ_KV_HEADS
    qw = LANES * group
    tq = min(TQ // 2, q_ref.shape[0])
    half = _lane_half((tq, LANES))
    keep = _half_keep(half)
    nkvp = k_ref.shape[1] // LANES
    caches = []
    if has_cache:
        for kvp in range(nkvp):
            kcols = slice(kvp * LANES, (kvp + 1) * LANES)
            caches.append((ck_ref[kcols, :].astype(BF), cv_ref[kcols, :].astype(BF), True))

    def seg(job):
        kvp, t, _ = job
        kcols = slice(kvp * LANES, (kvp + 1) * LANES)
        if has_cache:
            return [(k_ref[:, kcols], v_ref[:, kcols]), caches[kvp]]
        return [(k_ref[t * tq:(t + 1) * tq, kcols], v_ref[t * tq:(t + 1) * tq, kcols])]

    jobs = [(kvp, t, kh) for t in range(q_ref.shape[0] // tq) for kvp in range(nkvp) for kh in (0, 1)]

    def blocks(job):
        kvp, t, kh = job
        for pair in (2 * kh, 2 * kh + 1):
            yield slice(t * tq, (t + 1) * tq), slice(kvp * qw + pair * LANES, kvp * qw + (pair + 1) * LANES)

    def score_fn(job):
        kh = job[2]
        parts = []
        for rows, cols in blocks(job):
            for a in (0, 1):
                qm = q_ref[rows, cols] * keep[a]
                parts.append(qm if a == kh else pltpu.roll(qm.astype(F32), HEAD_DIM, 1).astype(BF))
        return _scores(jnp.concatenate(parts, axis=0), seg(job))

    def finish_fn(job, scores):
        kh = job[2]
        o = _softmax_finish(scores, seg(job))
        for n, (rows, cols) in enumerate(blocks(job)):
            heads = [o[(2 * n + a) * tq:(2 * n + a + 1) * tq] for a in (0, 1)]
            heads = [h if a == kh else pltpu.roll(h, HEAD_DIM, 1) for a, h in enumerate(heads)]
            o_ref[rows, cols] = jnp.where(half == 0, heads[0], heads[1]).astype(BF)

    _pipelined(jobs, score_fn, finish_fn)


GQ_LAT_ROWS = 1024


def _gq_attention(qb, kb, vb, cache_k, cache_v):
    nk = GQ_KV_HEADS * HEAD_DIM
    qw = LANES * (GQ_HEADS // GQ_KV_HEADS)
    npair = GQ_KV_HEADS // 2
    o_ctx = pl.pallas_call(
        functools.partial(_gq_attn_kernel, has_cache=False),
        grid=(BATCH // CTX_BATCHES_PER_STEP,),
        in_specs=[pl.BlockSpec((CTX_ROWS, D), lambda b: (b, 0))] + [pl.BlockSpec((CTX_ROWS, nk), lambda b: (b, 0))] * 2,
        out_specs=pl.BlockSpec((CTX_ROWS, D), lambda b: (b, 0)),
        out_shape=jax.ShapeDtypeStruct((T_CTX, D), BF),
        compiler_params=_cparams(1),
        name="gq_attn_ctx",
    )(qb, kb, vb)
    qt = DEC_SEQ // GQ_LAT_ROWS
    q0, k0 = T_CTX // GQ_LAT_ROWS, T_CTX // DEC_SEQ
    kv_spec = pl.BlockSpec((DEC_SEQ, LANES), lambda b, p, t: (k0 + b, p))
    c_spec = pl.BlockSpec((None, LANES, PAST), lambda b, p, t: (b, p, 0))
    o_lat = pl.pallas_call(
        functools.partial(_gq_attn_kernel, has_cache=True),
        grid=(DEC_BATCH, npair, qt),
        in_specs=[pl.BlockSpec((GQ_LAT_ROWS, qw), lambda b, p, t: (q0 + b * qt + t, p)), kv_spec, kv_spec, c_spec,
                  c_spec],
        out_specs=pl.BlockSpec((GQ_LAT_ROWS, qw), lambda b, p, t: (b * qt + t, p)),
        out_shape=jax.ShapeDtypeStruct((T_LAT, D), BF),
        compiler_params=_cparams(3),
        name="gq_attn_lat",
    )(qb, kb, vb, cache_k, cache_v)
    return o_ctx, o_lat


def _dot_3pass(a, b):
    ah, bh = a.astype(BF), b.astype(BF)
    al, bl = (a - ah.astype(F32)).astype(BF), (b - bh.astype(F32)).astype(BF)
    return _dot(ah, bh) + _dot(ah, bl) + _dot(al, bh)


def _hy_filter_kernel(emb_ref, w1_ref, b1_ref, w2_ref, b2_ref, fr_ref, w3f_ref, w3b_ref, ldf_ref, ldb_ref,
                      c_ref, s_ref, hre_ref, him_ref, hny_ref, hid_ref, cb_ref, sb_ref):
    seq = emb_ref.shape[0]

    @pl.when((pl.program_id(0) == 0) & (pl.program_id(1) == 0))
    def _():
        hp = lax.Precision.HIGHEST
        fr = fr_ref[...]
        hid = jnp.sin(fr * (jnp.dot(emb_ref[...], w1_ref[...], precision=hp, preferred_element_type=F32)
                            + b1_ref[...]))
        hid_ref[...] = jnp.sin(fr * (jnp.dot(hid, w2_ref[...], precision=hp, preferred_element_type=F32)
                                     + b2_ref[...]))
        cb_ref[...] = c_ref[...].astype(BF)
        sb_ref[...] = s_ref[...].astype(BF)

    hid = hid_ref[...]
    t = emb_ref[:, 0:1]
    fwd = _dot_3pass(hid, w3f_ref[...]) * jnp.exp(-jnp.exp(ldf_ref[...]) * t)
    bwd = _dot_3pass(hid, w3b_ref[...]) * jnp.exp(-jnp.exp(ldb_ref[...]) * t)
    row = lax.broadcasted_iota(jnp.int32, fwd.shape, 0)
    bwd = jnp.where(row == 0, 0.0, bwd)
    even = fwd + bwd
    odd = bwd - fwd
    wk = jnp.where(row == 0, 0.5 / seq, 1.0 / seq)
    hre_ref[...] = _dot(cb_ref[...], even.astype(BF)) * wk
    him_ref[...] = _dot(sb_ref[...], odd.astype(BF)) * wk
    alt = jnp.where((row & 1) == 0, 1.0, -1.0)
    hny_ref[...] = jnp.sum(alt * even, axis=0, keepdims=True) * (0.5 / seq)


def _hy_filter(seq, emb, w1, b1, w2, b2, freq, w3, log_decay, cmat, smat):
    dc = 512
    nj = D // dc
    small = [_const_spec(a.shape) for a in (emb, w1, b1, w2, b2, freq)]
    return pl.pallas_call(
        _hy_filter_kernel,
        grid=(HY_ORDER, nj),
        in_specs=small + [pl.BlockSpec((HY_FFN, dc), lambda o, j: (0, (2 * o) * nj + j)),
                          pl.BlockSpec((HY_FFN, dc), lambda o, j: (0, (2 * o + 1) * nj + j)),
                          pl.BlockSpec((1, dc), lambda o, j: (0, (2 * o) * nj + j)),
                          pl.BlockSpec((1, dc), lambda o, j: (0, (2 * o + 1) * nj + j)),
                          _const_spec((seq, seq)), _const_spec((seq, seq))],
        out_specs=[pl.BlockSpec((None, seq, dc), lambda o, j: (o, 0, j)),
                   pl.BlockSpec((None, seq, dc), lambda o, j: (o, 0, j)),
                   pl.BlockSpec((None, 1, dc), lambda o, j: (o, 0, j))],
        out_shape=[jax.ShapeDtypeStruct((HY_ORDER, seq, D), F32), jax.ShapeDtypeStruct((HY_ORDER, seq, D), F32),
                   jax.ShapeDtypeStruct((HY_ORDER, 1, D), F32)],
        scratch_shapes=[pltpu.VMEM((seq, HY_FFN), F32), pltpu.VMEM((seq, seq), BF), pltpu.VMEM((seq, seq), BF)],
        compiler_params=_cparams(2),
        name=f"hy_filter_{seq}",
    )(emb, w1, b1, w2, b2, freq, w3, w3, log_decay, log_decay, cmat, smat)


HY_SUB = 256


def _hy_conv_kernel(u0_ref, u1_ref, u2_ref, sw0_ref, sw1_ref, sw2_ref, sb0_ref, sb1_ref, sb2_ref,
                    fb_ref, hre_ref, him_ref, hny_ref, c_ref, s_ref, o_ref, cb_ref, sb_ref):
    seq, dc = u0_ref.shape

    @pl.when((pl.program_id(0) == 0) & (pl.program_id(1) == 0))
    def _():
        cb_ref[...] = c_ref[...].astype(BF)
        sb_ref[...] = s_ref[...].astype(BF)

    row = lax.broadcasted_iota(jnp.int32, (seq, HY_SUB), 0)
    alt = jnp.where((row & 1) == 0, 1.0, -1.0)

    def sub_tile(cols):
        def short_conv(u_ref, w_ref, b_ref):
            u = u_ref[:, cols]
            prev = jnp.where(row == 0, 0.0, pltpu.roll(u, 1, 0))
            nxt = jnp.where(row == seq - 1, 0.0, pltpu.roll(u, seq - 1, 0))
            return prev * w_ref[0:1, cols] + u * w_ref[1:2, cols] + nxt * w_ref[2:3, cols] + b_ref[:, cols]

        z = short_conv(u0_ref, sw0_ref, sb0_ref)
        gates = (short_conv(u1_ref, sw1_ref, sb1_ref), short_conv(u2_ref, sw2_ref, sb2_ref))
        yield
        for o in range(HY_ORDER):
            zb = z.astype(BF)
            zc, zs = _dot(cb_ref[...], zb), _dot(sb_ref[...], zb)
            yield
            hre, him = hre_ref[o, :, cols], him_ref[o, :, cols]
            p_re = (zc * hre + zs * him).astype(BF)
            p_im = (zc * him - zs * hre).astype(BF)
            y = _dot(cb_ref[...], p_re) - _dot(sb_ref[...], p_im)
            yield
            nyq = jnp.sum(alt * z, axis=0, keepdims=True) * hny_ref[o, :, cols]
            z = gates[o] * (y + alt * nyq + z * fb_ref[o:o + 1, cols])
        o_ref[:, cols] = z.astype(BF)

    _lockstep(sub_tile(slice(j * HY_SUB, (j + 1) * HY_SUB)) for j in range(dc // HY_SUB))


def _hy_conv(u, short_w, short_b, filter_bias, hre, him, hny, cmat, smat, seq, nbatch, row0, dc):
    nj = D // dc
    r0 = row0 // seq

    def part(p):
        return pl.BlockSpec((seq, dc), lambda j, b: (r0 + b, p * nj + j))

    def vec(rows, p):
        return pl.BlockSpec((rows, dc), lambda j, b: (0, p * nj + j))

    in_specs = ([part(p) for p in range(3)] + [vec(3, p) for p in range(3)] + [vec(1, p) for p in range(3)]
                + [pl.BlockSpec((HY_ORDER, dc), lambda j, b: (0, j)),
                   pl.BlockSpec((HY_ORDER, seq, dc), lambda j, b: (0, 0, j), pipeline_mode=pl.Buffered(1)),
                   pl.BlockSpec((HY_ORDER, seq, dc), lambda j, b: (0, 0, j), pipeline_mode=pl.Buffered(1)),
                   pl.BlockSpec((HY_ORDER, 1, dc), lambda j, b: (0, 0, j)),
                   _const_spec((seq, seq)), _const_spec((seq, seq))])
    return pl.pallas_call(
        _hy_conv_kernel,
        grid=(nj, nbatch),
        in_specs=in_specs,
        out_specs=pl.BlockSpec((seq, dc), lambda j, b: (b, j)),
        out_shape=jax.ShapeDtypeStruct((nbatch * seq, D), BF),
        scratch_shapes=[pltpu.VMEM((seq, seq), BF), pltpu.VMEM((seq, seq), BF)],
        compiler_params=_cparams(2),
        name=f"hy_conv_{seq}",
    )(u, u, u, short_w, short_w, short_w, short_b, short_b, short_b, filter_bias, hre, him, hny, cmat, smat)


def _dft_tables(seq):
    k = np.arange(seq, dtype=np.int64)
    ang = np.pi * ((k[:, None] * k[None, :]) % (2 * seq)) / seq
    return jnp.asarray(np.cos(ang), F32), jnp.asarray(np.sin(ang), F32)


def _hy_embedding(seq):
    t = np.arange(seq, dtype=np.float32) / np.float32(seq)
    ang = (2.0 * math.pi) * t[:, None] * np.arange(1, HY_BANDS + 1, dtype=np.float32)
    emb = np.concatenate([t[:, None], np.cos(ang), np.sin(ang)], axis=-1).astype(np.float32)
    return jnp.asarray(np.pad(emb, ((0, 0), (0, HY_EMB_PAD - HY_EMB))))


def _post_kernel(*refs, split_x, split_out, tm):
    oc_ref, ol_ref = refs[0:2]
    x_refs, refs = (refs[2:4], refs[4:]) if split_x else (refs[2:3], refs[3:])
    mod_ref, wo_ref, g1_ref, b1_ref, w1c_ref, w2c_ref, g2_ref, b2_ref = refs[0:8]
    outs, (w1_ref, w2_ref, h_ref, acc_ref) = refs[8:-4], refs[-4:]
    x1_ref = outs[0]
    step = pl.program_id(0)
    is_lat = _is_lat(tm, N_FF_CHUNKS - 1)
    nsub = tm // SUB_POST
    per = MLP_CHUNK // FF_CHUNK

    def rows(j):
        return slice(j * SUB_POST, (j + 1) * SUB_POST)

    def pick(c_ref, l_ref, j):
        return jnp.where(is_lat, l_ref[rows(j), :], c_ref[rows(j), :])

    def norm1(j):
        a = _dot(pick(oc_ref, ol_ref, j), wo_ref[...])
        x = pick(x_refs[0], x_refs[1], j) if split_x else x_refs[0][rows(j), :]
        x1 = _layer_norm(DN_ALPHA * x + mod_ref[2:3, :] * a, g1_ref[...], b1_ref[...])
        return x1, _modulate(x1, mod_ref, 3, 4)

    def norm2(x1, acc):
        return _layer_norm(DN_ALPHA * x1 + mod_ref[5:6, :] * acc, g2_ref[...], b2_ref[...])

    def mlp_chunk(h, c):
        a = jnp.concatenate([_dot(h, w1_ref[per * c + i]) for i in range(per)], axis=1)
        a = jnp.maximum(a, 0.0)
        return _dot((a * a).astype(BF), w2_ref[c])

    def write_branched(ys):
        yc_ref, yl_ref = outs

        @pl.when(jnp.logical_not(is_lat))
        def _():
            for j, y in enumerate(ys):
                yc_ref[rows(j), :] = y

        @pl.when(is_lat)
        def _():
            for j, y in enumerate(ys):
                yl_ref[rows(j), :] = y

    @pl.when(step < N_FF_CHUNKS)
    def _():
        w2_rows = pl.ds(pl.multiple_of((step % per) * FF_CHUNK, FF_CHUNK), FF_CHUNK)
        w1_ref[step] = w1c_ref[...].astype(BF)
        w2_ref[step // per, w2_rows, :] = w2c_ref[...].astype(BF)

        @pl.when(step == 0)
        def _():
            for j in range(nsub):
                x1_ref[rows(j), :], h_ref[rows(j), :] = norm1(j)
            acc_ref[...] = jnp.zeros_like(acc_ref)

        a = jnp.maximum(_dot(h_ref[...], w1_ref[step]), 0.0)
        acc_ref[...] += _dot((a * a).astype(BF), w2_ref[step // per, w2_rows, :])

        @pl.when(step == N_FF_CHUNKS - 1)
        def _():
            ys = [norm2(x1_ref[rows(j), :], acc_ref[rows(j), :]) for j in range(nsub)]
            if split_out:
                write_branched(ys)
            else:
                for j, y in enumerate(ys):
                    outs[0][rows(j), :] = y

    def token_tile():
        ys = []
        cur = norm1(0)
        prev = None
        for j in range(nsub):
            x1, h = cur
            acc = mlp_chunk(h, 0)
            if j + 1 < nsub:
                cur = norm1(j + 1)
            if prev is not None:
                ys.append(norm2(*prev))
                if not split_out:
                    outs[0][rows(j - 1), :] = ys[-1]
            for c in range(1, D_FF // MLP_CHUNK):
                acc = acc + mlp_chunk(h, c)
            prev = (x1, acc)
        ys.append(norm2(*prev))
        if split_out:
            write_branched(ys)
        else:
            outs[0][rows(nsub - 1), :] = ys[-1]

    pl.when(step >= N_FF_CHUNKS)(token_tile)


def _post(o_ctx, o_lat, xs, mods, layer, w_o, g1, b1, w1, w2, g2, b2, split_out):
    tm, off = TM_POST, N_FF_CHUNKS - 1
    split_x = len(xs) == 2
    x_specs = [_ctx_spec(D, tm, off), _lat_spec(D, tm, off)] if split_x else [_tok_spec(D, tm, off)]
    vec = _const_spec((1, D))
    if split_out:
        out_specs = [_ctx_spec(D, tm, off), _lat_spec(D, tm, off)]
        out_shape = [jax.ShapeDtypeStruct((T_CTX, D), F32), jax.ShapeDtypeStruct((T_LAT, D), F32)]
    else:
        out_specs = _tok_spec(D, tm, off)
        out_shape = jax.ShapeDtypeStruct((T, D), F32)

    def chunk(i):
        return jnp.minimum(i, N_FF_CHUNKS - 1)

    return pl.pallas_call(
        functools.partial(_post_kernel, split_x=split_x, split_out=split_out, tm=tm),
        grid=(off + T // tm,),
        in_specs=[_ctx_spec(D, tm, off), _lat_spec(D, tm, off)] + x_specs + [
            _mod_spec(layer, tm, off), _const_spec((D, D)), vec, vec,
            pl.BlockSpec((None, D, FF_CHUNK), lambda i: (layer, 0, chunk(i))),
            pl.BlockSpec((None, FF_CHUNK, D), lambda i: (layer, chunk(i), 0)), vec, vec],
        out_specs=out_specs,
        out_shape=out_shape,
        scratch_shapes=[pltpu.VMEM((N_FF_CHUNKS, D, FF_CHUNK), BF), pltpu.VMEM((D_FF // MLP_CHUNK, MLP_CHUNK, D), BF),
                        pltpu.VMEM((tm, D), BF), pltpu.VMEM((tm, D), F32)],
        compiler_params=_cparams(1, POST_VMEM_LIMIT),
        name=f"post_l{layer}",
    )(o_ctx, o_lat, *xs, mods, w_o, g1, b1, w1, w2, g2, b2)


def _rope_tables():
    n = HEAD_DIM // 4
    pos = np.arange(DEC_SEQ)
    inv = (np.float32(ROPE_BASE) ** (-np.arange(n, dtype=np.float32) / np.float32(n))).astype(np.float32)
    ang_r = ((pos // GRID_W).astype(np.float32)[:, None] * inv).astype(np.float32)
    ang_c = ((pos % GRID_W).astype(np.float32)[:, None] * inv).astype(np.float32)
    cr, sr, cc, sc = np.cos(ang_r), np.sin(ang_r), np.cos(ang_c), np.sin(ang_c)
    a = np.tile(np.concatenate([cr, cr, cc, cc], axis=-1), (1, D // HEAD_DIM))
    b = np.tile(np.concatenate([-sr, sr, -sc, sc], axis=-1), (1, D // HEAD_DIM))
    a = np.concatenate([a, np.ones((TM, D), np.float32)], axis=0)
    b = np.concatenate([b, np.zeros((TM, D), np.float32)], axis=0)
    return jnp.asarray(a, F32), jnp.asarray(b, F32)


def kernel(x_prompt, x_sample, c, cache_da_k, cache_da_v, cache_na_k, cache_na_v, cache_gq_k, cache_gq_v, c_ctx, ada_w, ada_b, ln_g, ln_b, mlp_w1, mlp_w2, da_w_qkv, da_w_o, da_lambda, da_subln_g, na_w_qkv, na_w_o, na_rel_bias, gq_w_qkv, gq_w_o, gq_q_norm, gq_k_norm, hy_w_in, hy_short_w, hy_short_b, hy_ffn_w1, hy_ffn_b1, hy_ffn_w2, hy_ffn_b2, hy_ffn_freq, hy_ffn_w3, hy_log_decay, hy_filter_bias, hy_w_o):
    cvec = jnp.concatenate([c_ctx[None, :], c, jnp.zeros((MOD_ROWS - 1 - DEC_BATCH, D), F32)], axis=0)
    mods = _mods(cvec, ada_w, ada_b)
    rope_a, rope_b = _rope_tables()

    def finish(o_ctx, o_lat, xs, layer, w_o, split_out=False):
        return _post(o_ctx, o_lat, xs, mods, layer, w_o.astype(BF), ln_g[layer, 0][None], ln_b[layer, 0][None],
                     mlp_w1, mlp_w2, ln_g[layer, 1][None], ln_b[layer, 1][None], split_out)

    xs = (x_prompt.reshape(T_CTX, D), x_sample.reshape(T_LAT, D))
    qb, kb, vb, ks, vs = _da_proj(*xs, mods, 0, da_w_qkv[0].astype(BF), rope_a, rope_b)
    state_da_k = ks.reshape(BATCH, 1, SEQ, DA_HEADS, 2 * HEAD_DIM)
    state_da_v = vs.reshape(BATCH, 1, SEQ, DA_HEADS, 2 * HEAD_DIM)
    o_ctx, o_lat = _da_attention(qb, kb, vb, cache_da_k, cache_da_v, da_lambda[0], da_subln_g[0][None], 0)
    x = finish(o_ctx, o_lat, xs, 0, da_w_o[0])

    qb, kb, vb, ks, vs = _na_proj(x, mods, 1, na_w_qkv[0].astype(BF))
    state_na_k, state_na_v = _untranspose_state(ks, NA_HEADS), _untranspose_state(vs, NA_HEADS)
    onehot, neg, mask = _na_constants()
    bias_tab = _na_bias_table(na_rel_bias[0], onehot, neg)
    o_ctx = _na_ctx_attention(qb, kb, vb)
    o_lat = _na_lat_attention(qb, kb, vb, _features_major(cache_na_k), _features_major(cache_na_v), bias_tab, mask)
    x = finish(o_ctx, o_lat, (x,), 1, na_w_o[0])

    g_mat = jnp.asarray(np.kron(np.eye(GN_BLOCK // HEAD_DIM), np.full((HEAD_DIM, HEAD_DIM), 1.0 / HEAD_DIM)), BF)
    qb, kb, vb, ks, vs = _gq_proj(x, mods, 2, gq_w_qkv[0].astype(BF), g_mat,
                                  jnp.tile(gq_q_norm[0], GQ_HEADS)[None], jnp.tile(gq_k_norm[0], GQ_KV_HEADS)[None],
                                  rope_a, rope_b)
    state_gq_k, state_gq_v = _untranspose_state(ks, GQ_KV_HEADS), _untranspose_state(vs, GQ_KV_HEADS)
    o_ctx, o_lat = _gq_attention(qb, kb, vb, _features_major(cache_gq_k), _features_major(cache_gq_v))
    x = finish(o_ctx, o_lat, (x,), 2, gq_w_o[0])

    u = _hy_proj(x, mods, 3, hy_w_in[0].astype(BF))
    w1 = jnp.pad(hy_ffn_w1[0], ((0, HY_EMB_PAD - HY_EMB), (0, 0)))
    zs = []
    for seq, nbatch, row0, dc in ((SEQ, BATCH, 0, D), (DEC_SEQ, DEC_BATCH, T_CTX, 512)):
        cmat, smat = _dft_tables(seq)
        hre, him, hny = _hy_filter(seq, _hy_embedding(seq), w1, hy_ffn_b1[0][None], hy_ffn_w2[0], hy_ffn_b2[0][None],
                                   hy_ffn_freq[0][None], hy_ffn_w3[0], hy_log_decay[0][None], cmat, smat)
        zs.append(_hy_conv(u, hy_short_w[0], hy_short_b[0][None], hy_filter_bias[0], hre, him, hny, cmat, smat,
                           seq, nbatch, row0, dc))
    y_ctx, y_lat = finish(zs[0], zs[1], (x,), 3, hy_w_o[0], split_out=True)

    return (y_ctx.reshape(BATCH, SEQ, D), y_lat.reshape(DEC_BATCH, DEC_SEQ, D),
            state_da_k, state_da_v, state_na_k, state_na_v, state_gq_k, state_gq_v)
```

```python
import functools
import math

import numpy as np
import jax
import jax.numpy as jnp
from jax import lax
from jax.experimental import pallas as pl
from jax.experimental.pallas import tpu as pltpu

F32 = jnp.float32
BF = jnp.bfloat16

D = 1024
BATCH = 16
SEQ = 256
DEC_BATCH = 8
DEC_SEQ = 1024
PAST = 256
DEPTH = 4
GRID_W = 64
GRID_ROWS = DEC_SEQ // GRID_W
D_FF = 4 * D
T_CTX = BATCH * SEQ
T_LAT = DEC_BATCH * DEC_SEQ
T = T_CTX + T_LAT
HEAD_DIM = 64
ATT_SCALE = HEAD_DIM ** -0.5
LOG2E = math.log2(math.e)
Q_SCALE = ATT_SCALE * LOG2E
DA_HEADS = 8
NA_HEADS = 16
NA_WIN_ROWS = 8
NA_WIN_COLS = 16
GQ_HEADS = 16
GQ_KV_HEADS = 4
HY_ORDER = 2
HY_BANDS = 16
HY_EMB = 1 + 2 * HY_BANDS
HY_EMB_PAD = 40
HY_FFN = 64
ROPE_BASE = 10000.0
LN_EPS = 1e-5
RMS_EPS = 1e-6
DN_ALPHA = (2 * DEPTH) ** 0.25
NEG_INF = -1e30

LANES = 128
TM = 512
TM_POST = 512
FF_CHUNK = 512
MLP_CHUNK = 1024
N_FF_CHUNKS = D_FF // FF_CHUNK
SUB_POST = 256
N_CTX_TILES = T_CTX // TM
N_TILES = T // TM
TQ = 512
CTX_BATCHES_PER_STEP = 4
CTX_ROWS = CTX_BATCHES_PER_STEP * SEQ
MOD_ROWS = 16
VMEM_LIMIT = 56 * 1024 * 1024
POST_VMEM_LIMIT = 58 * 1024 * 1024


def _cparams(n_axes, vmem_limit=VMEM_LIMIT):
    return pltpu.CompilerParams(dimension_semantics=("arbitrary",) * n_axes,
                                vmem_limit_bytes=vmem_limit)


def _dot(a, b):
    return jnp.dot(a, b, preferred_element_type=F32)


def _dot_nt(a, b):
    return lax.dot_general(a, b, (((1,), (1,)), ((), ())), preferred_element_type=F32)


def _const_spec(shape):
    nd = len(shape)
    return pl.BlockSpec(shape, lambda *_: (0,) * nd, pipeline_mode=pl.Buffered(1))


def _mod_spec(layer, tm=TM, off=0):
    nctx = T_CTX // tm

    def row(i):
        t = jnp.maximum(i - off, 0)
        return jnp.where(t < nctx, 0, 1 + (t - nctx) // (DEC_SEQ // tm))

    return pl.BlockSpec((None, None, 6, D), lambda i: (layer, row(i), 0, 0))


def _tok_spec(width, tm=TM, off=0):
    return pl.BlockSpec((tm, width), lambda i: (jnp.maximum(i - off, 0), 0))


def _ctx_spec(width, tm=TM, off=0):
    return pl.BlockSpec((tm, width), lambda i: (jnp.clip(i - off, 0, T_CTX // tm - 1), 0))


def _lat_spec(width, tm=TM, off=0):
    return pl.BlockSpec((tm, width), lambda i: (jnp.maximum(i - off - T_CTX // tm, 0), 0))


def _is_lat(tm=TM, off=0):
    return pl.program_id(0) >= off + T_CTX // tm


def _pick(ctx_ref, lat_ref):
    return jnp.where(_is_lat(), lat_ref[...], ctx_ref[...])


def _layer_norm(r, g, b):
    mu = jnp.mean(r, axis=-1, keepdims=True)
    c = r - mu
    var = jnp.mean(c * c, axis=-1, keepdims=True)
    return c * lax.rsqrt(var + LN_EPS) * g + b


def _mods_kernel(c_ref, w_ref, b_ref, o_ref):
    c = c_ref[...]
    s = (c / (1.0 + jnp.exp(-c))).astype(BF)
    o_ref[...] = _dot(s, w_ref[...].astype(BF)) + b_ref[...]


def _mods(cvec, ada_w, ada_b):
    tn = 1536
    out = pl.pallas_call(
        _mods_kernel,
        grid=(DEPTH, 6 * D // tn),
        in_specs=[pl.BlockSpec((MOD_ROWS, D), lambda l, n: (0, 0)),
                  pl.BlockSpec((None, D, tn), lambda l, n: (l, 0, n)),
                  pl.BlockSpec((None, 1, tn), lambda l, n: (l, 0, n))],
        out_specs=pl.BlockSpec((None, MOD_ROWS, tn), lambda l, n: (l, 0, n)),
        out_shape=jax.ShapeDtypeStruct((DEPTH, MOD_ROWS, 6 * D), F32),
        compiler_params=_cparams(2),
        name="adaln_mods",
    )(cvec, ada_w, ada_b.reshape(DEPTH, 1, 6 * D))
    return out.reshape(DEPTH, MOD_ROWS, 6, D)


def _modulate(x, mod_ref, shift, scale):
    return (x * (1.0 + mod_ref[scale:scale + 1, :]) + mod_ref[shift:shift + 1, :]).astype(BF)


def _rope(x, a, b):
    n = x.shape[1]
    lane = lax.broadcasted_iota(jnp.int32, x.shape, 1)
    partner = jnp.where((lane & 16) == 0, pltpu.roll(x, n - 16, 1), pltpu.roll(x, 16, 1))
    return x * a + partner * b


def _rope_spec(width):
    per = DEC_SEQ // TM
    return pl.BlockSpec((TM, width), lambda i: (jnp.where(i < N_CTX_TILES, per, (i - N_CTX_TILES) % per), 0))


def _lockstep(gens):
    waiting, active = list(gens), []
    while waiting or active:
        if waiting:
            active.append(waiting.pop(0))
        active = [g for g in active if next(g, True) is None]


def _store_state(k, v, ks_ref, vs_ref, transposed):
    @pl.when(jnp.logical_not(_is_lat()))
    def _():
        if not transposed:
            ks_ref[...] = k
            vs_ref[...] = v
        else:
            n = k.shape[0]
            for xt, ref in ((k, ks_ref), (v, vs_ref)):
                for j in range(TM // SEQ):
                    ref[j * n:(j + 1) * n, :] = xt[:, j * SEQ:(j + 1) * SEQ]


def _qkv_out(nq, nk, transposed_state):
    specs = [_tok_spec(nq), _tok_spec(nk), _tok_spec(nk)]
    shapes = [jax.ShapeDtypeStruct((T, nq), BF), jax.ShapeDtypeStruct((T, nk), BF), jax.ShapeDtypeStruct((T, nk), BF)]
    if transposed_state:
        rows = (TM // SEQ) * nk
        specs += [pl.BlockSpec((rows, SEQ), lambda i: (jnp.minimum(i, N_CTX_TILES - 1), 0))] * 2
        shapes += [jax.ShapeDtypeStruct((BATCH * nk, SEQ), F32)] * 2
    else:
        specs += [_ctx_spec(nk)] * 2
        shapes += [jax.ShapeDtypeStruct((T_CTX, nk), F32)] * 2
    return specs, shapes


def _features_major(cache):
    b, _, past, heads, dh = cache.shape
    return cache.transpose(0, 1, 3, 4, 2).reshape(b, heads * dh, past)


def _untranspose_state(st, heads):
    return st.reshape(BATCH, heads, HEAD_DIM, SEQ).transpose(0, 3, 1, 2)[:, None]


def _da_proj_kernel(xc_ref, xl_ref, mod_ref, w_ref, ra_ref, rb_ref, qb_ref, kb_ref, vb_ref, ks_ref, vs_ref):
    h = _modulate(_pick(xc_ref, xl_ref), mod_ref, 0, 1)
    a, b = ra_ref[...], rb_ref[...]
    q = _dot(h, w_ref[:, 0:D])
    k = _dot(h, w_ref[:, D:2 * D])
    qb_ref[...] = (_rope(q, a, b) * Q_SCALE).astype(BF)
    v = _dot(h, w_ref[:, 2 * D:3 * D])
    kb_ref[...] = _rope(k, a, b).astype(BF)
    vb_ref[...] = v.astype(BF)
    _store_state(k, v, ks_ref, vs_ref, False)


def _da_proj(x_ctx, x_lat, mods, layer, w, rope_a, rope_b):
    specs, shapes = _qkv_out(D, D, False)
    return pl.pallas_call(
        _da_proj_kernel,
        grid=(N_TILES,),
        in_specs=[_ctx_spec(D), _lat_spec(D), _mod_spec(layer), _const_spec((D, 3 * D)),
                  _rope_spec(D), _rope_spec(D)],
        out_specs=specs, out_shape=shapes,
        compiler_params=_cparams(1),
        name=f"da_proj_l{layer}",
    )(x_ctx, x_lat, mods, w, rope_a, rope_b)


def _na_proj_kernel(x_ref, mod_ref, w_ref, qb_ref, kb_ref, vb_ref, ks_ref, vs_ref):
    h = _modulate(x_ref[...], mod_ref, 0, 1)
    v = _dot(h, w_ref[:, 2 * D:3 * D])
    k = _dot(h, w_ref[:, D:2 * D])
    vb_ref[...] = v.astype(BF)
    vt = v.T
    q = _dot(h, w_ref[:, 0:D])
    kb_ref[...] = k.astype(BF)
    kt = k.T
    qb_ref[...] = (q * Q_SCALE).astype(BF)
    _store_state(kt, vt, ks_ref, vs_ref, True)


def _na_proj(x, mods, layer, w):
    specs, shapes = _qkv_out(D, D, True)
    return pl.pallas_call(
        _na_proj_kernel,
        grid=(N_TILES,),
        in_specs=[_tok_spec(D), _mod_spec(layer), _const_spec((D, 3 * D))],
        out_specs=specs, out_shape=shapes,
        compiler_params=_cparams(1),
        name=f"na_proj_l{layer}",
    )(x, mods, w)


GN_BLOCK = 256


def _head_rms(x, g_ref, gain):
    x2 = x * x
    hi = x2.astype(BF)
    lo = (x2 - hi.astype(F32)).astype(BF)
    g = g_ref[...]
    ms = jnp.concatenate(
        [_dot(hi[:, j:j + GN_BLOCK], g) + _dot(lo[:, j:j + GN_BLOCK], g) for j in range(0, x.shape[1], GN_BLOCK)],
        axis=1)
    return x * lax.rsqrt(ms + RMS_EPS) * gain


def _gq_proj_kernel(x_ref, mod_ref, w_ref, g_ref, qn_ref, kn_ref, ra_ref, rb_ref,
                    qb_ref, kb_ref, vb_ref, ks_ref, vs_ref):
    nq, nk = GQ_HEADS * HEAD_DIM, GQ_KV_HEADS * HEAD_DIM
    h = _modulate(x_ref[...], mod_ref, 0, 1)

    def finish_q(c, qc):
        cols = slice(c * GN_BLOCK, (c + 1) * GN_BLOCK)
        qc = _head_rms(qc, g_ref, qn_ref[:, cols])
        qb_ref[:, cols] = (_rope(qc, ra_ref[:, cols], rb_ref[:, cols]) * Q_SCALE).astype(BF)

    k = _dot(h, w_ref[:, nq:nq + nk])
    v = _dot(h, w_ref[:, nq + nk:nq + 2 * nk])
    prev = None
    for c in range(nq // GN_BLOCK):
        qc = _dot(h, w_ref[:, c * GN_BLOCK:(c + 1) * GN_BLOCK])
        if c == 0:
            k = _head_rms(k, g_ref, kn_ref[...])
            kb_ref[...] = _rope(k, ra_ref[:, 0:nk], rb_ref[:, 0:nk]).astype(BF)
            vb_ref[...] = v.astype(BF)
            kt, vt = k.T, v.T
        else:
            finish_q(*prev)
        prev = (c, qc)
    finish_q(*prev)
    _store_state(kt, vt, ks_ref, vs_ref, True)


def _gq_proj(x, mods, layer, w, g_mat, qn, kn, rope_a, rope_b):
    nq, nk = GQ_HEADS * HEAD_DIM, GQ_KV_HEADS * HEAD_DIM
    specs, shapes = _qkv_out(nq, nk, True)
    return pl.pallas_call(
        _gq_proj_kernel,
        grid=(N_TILES,),
        in_specs=[_tok_spec(D), _mod_spec(layer), _const_spec((D, nq + 2 * nk)),
                  _const_spec((GN_BLOCK, GN_BLOCK)), _const_spec((1, nq)), _const_spec((1, nk)),
                  _rope_spec(D), _rope_spec(D)],
        out_specs=specs, out_shape=shapes,
        compiler_params=_cparams(1),
        name=f"gq_proj_l{layer}",
    )(x, mods, w, g_mat, qn, kn, rope_a, rope_b)


def _hy_proj_kernel(x_ref, mod_ref, w_ref, u_ref):
    h = _modulate(x_ref[...], mod_ref, 0, 1)
    for c in range(HY_ORDER + 1):
        u_ref[:, c * D:(c + 1) * D] = _dot(h, w_ref[:, c * D:(c + 1) * D])


TM_HY_PROJ = 1024


def _hy_proj(x, mods, layer, w):
    n = (HY_ORDER + 1) * D
    return pl.pallas_call(
        _hy_proj_kernel,
        grid=(T // TM_HY_PROJ,),
        in_specs=[_tok_spec(D, TM_HY_PROJ), _mod_spec(layer, TM_HY_PROJ), _const_spec((D, n))],
        out_specs=_tok_spec(n, TM_HY_PROJ),
        out_shape=jax.ShapeDtypeStruct((T, n), F32),
        compiler_params=_cparams(1),
        name=f"hy_proj_l{layer}",
    )(x, mods, w)


def _scores(qm, segs):
    return [_dot(qm, seg[0]) if len(seg) == 3 else _dot_nt(qm, seg[0]) for seg in segs]


def _softmax_finish(scores, segs):
    m = scores[0].max(axis=-1, keepdims=True)
    for s in scores[1:]:
        m = jnp.maximum(m, s.max(axis=-1, keepdims=True))
    den = None
    out = None
    for s, seg in zip(scores, segs):
        e = jnp.exp2(s - m)
        d = e.sum(axis=-1, keepdims=True)
        o = _dot_nt(e.astype(BF), seg[1]) if len(seg) == 3 else _dot(e.astype(BF), seg[1])
        den = d if den is None else den + d
        out = o if out is None else out + o
    return out / den


def _stack_halves(q, keep):
    return jnp.concatenate([q * keep[0], q * keep[1]], axis=0)


def _pipelined(jobs, score_fn, finish_fn):
    nxt = score_fn(jobs[0])
    for n, job in enumerate(jobs):
        cur, nxt = nxt, (score_fn(jobs[n + 1]) if n + 1 < len(jobs) else None)
        finish_fn(job, cur)


def _lane_half(shape):
    return lax.broadcasted_iota(jnp.int32, shape, 1) // HEAD_DIM


def _half_keep(half):
    return tuple(jnp.where(half == a, 1.0, 0.0).astype(BF) for a in (0, 1))


def _da_attn_kernel(*refs, has_cache, lam_init):
    if has_cache:
        q_ref, k_ref, v_ref, ck_ref, cv_ref, lam_ref, g_ref, o_ref = refs
    else:
        q_ref, k_ref, v_ref, lam_ref, g_ref, o_ref = refs
    lp = lam_ref[...]
    lam = (jnp.exp(jnp.sum(lp[0:1] * lp[1:2], axis=-1, keepdims=True))
           - jnp.exp(jnp.sum(lp[2:3] * lp[3:4], axis=-1, keepdims=True)) + lam_init)
    gain = g_ref[...] * (1.0 - lam_init)
    w = 2 * HEAD_DIM
    nheads = k_ref.shape[1] // w
    tq = min(TQ, q_ref.shape[0]) if has_cache else SEQ
    keep = _half_keep(_lane_half((tq, w)))
    caches = []
    if has_cache:
        for hd in range(nheads):
            head = pl.program_id(1) * nheads + hd
            caches.append((ck_ref[:, head, :].astype(BF), cv_ref[:, head, :].astype(BF)))

    def seg(hd, t):
        cols = slice(hd * w, (hd + 1) * w)
        if has_cache:
            return [(k_ref[:, cols], v_ref[:, cols]), caches[hd]]
        return [(k_ref[t * tq:(t + 1) * tq, cols], v_ref[t * tq:(t + 1) * tq, cols])]

    jobs = [(hd, t, a) for t in range(q_ref.shape[0] // tq) for hd in range(nheads) for a in (0, 1)]
    first = {}

    def score_fn(job):
        hd, t, a = job
        return _scores(q_ref[t * tq:(t + 1) * tq, hd * w:(hd + 1) * w] * keep[a], seg(hd, t))

    def finish_fn(job, scores):
        hd, t, a = job
        o = _softmax_finish(scores, seg(hd, t))
        if a == 0:
            first[0] = o
            return
        o = first[0] - lam * o
        ms = jnp.mean(o * o, axis=-1, keepdims=True)
        o_ref[t * tq:(t + 1) * tq, hd * w:(hd + 1) * w] = (o * lax.rsqrt(ms + RMS_EPS) * gain).astype(BF)

    _pipelined(jobs, score_fn, finish_fn)


DA_LAT_HEADS_PER_STEP = 2


def _da_attention(qb, kb, vb, cache_k, cache_v, lam_p, subln_g, layer_idx):
    lam_init = 0.8 - 0.6 * math.exp(-0.3 * layer_idx)
    w = 2 * HEAD_DIM
    small = [pl.BlockSpec((4, HEAD_DIM), lambda *_: (0, 0)), pl.BlockSpec((1, w), lambda *_: (0, 0))]
    o_ctx = pl.pallas_call(
        functools.partial(_da_attn_kernel, has_cache=False, lam_init=lam_init),
        grid=(BATCH,),
        in_specs=[pl.BlockSpec((SEQ, D), lambda b: (b, 0))] * 3 + small,
        out_specs=pl.BlockSpec((SEQ, D), lambda b: (b, 0)),
        out_shape=jax.ShapeDtypeStruct((T_CTX, D), BF),
        compiler_params=_cparams(1),
        name="da_attn_ctx",
    )(qb, kb, vb, lam_p, subln_g)
    k0 = T_CTX // DEC_SEQ
    hw = DA_LAT_HEADS_PER_STEP * w
    tok = pl.BlockSpec((DEC_SEQ, hw), lambda b, h: (k0 + b, h))
    c_spec = pl.BlockSpec((None, None, PAST, DA_HEADS, w), lambda b, h: (b, 0, 0, 0, 0))
    o_lat = pl.pallas_call(
        functools.partial(_da_attn_kernel, has_cache=True, lam_init=lam_init),
        grid=(DEC_BATCH, DA_HEADS // DA_LAT_HEADS_PER_STEP),
        in_specs=[tok, tok, tok, c_spec, c_spec] + small,
        out_specs=pl.BlockSpec((DEC_SEQ, hw), lambda b, h: (b, h)),
        out_shape=jax.ShapeDtypeStruct((T_LAT, D), BF),
        compiler_params=_cparams(2),
        name="da_attn_lat",
    )(qb, kb, vb, cache_k, cache_v, lam_p, subln_g)
    return o_ctx, o_lat


def _na_ctx_kernel(q_ref, k_ref, v_ref, o_ref):
    half = _lane_half((SEQ, LANES))
    keep = _half_keep(half)

    def block(job):
        b, p = job
        return slice(b * SEQ, (b + 1) * SEQ), slice(p * LANES, (p + 1) * LANES)

    def seg(job):
        return [(k_ref[block(job)], v_ref[block(job)])]

    def score_fn(job):
        return _scores(_stack_halves(q_ref[block(job)], keep), seg(job))

    def finish_fn(job, scores):
        o = _softmax_finish(scores, seg(job))
        o_ref[block(job)] = jnp.where(half == 0, o[0:SEQ], o[SEQ:2 * SEQ]).astype(BF)

    jobs = [(b, p) for b in range(q_ref.shape[0] // SEQ) for p in range(NA_HEADS // 2)]
    _pipelined(jobs, score_fn, finish_fn)


def _na_ctx_attention(qb, kb, vb):
    spec = pl.BlockSpec((CTX_ROWS, D), lambda b: (b, 0))
    return pl.pallas_call(
        _na_ctx_kernel,
        grid=(BATCH // CTX_BATCHES_PER_STEP,),
        in_specs=[spec] * 3,
        out_specs=spec,
        out_shape=jax.ShapeDtypeStruct((T_CTX, D), BF),
        compiler_params=_cparams(1),
        name="na_attn_ctx",
    )(qb, kb, vb)


NA_TILES = ((0, (0, 2, 4, 6)), (4, (0, 2, 4, 6, 8, 10)), (8, (4, 6, 8, 10, 12, 14)), (12, (8, 10, 12, 14)))
NA_MAX_CHUNKS = 6
NA_BIAS_BLOCKS = 2 * NA_WIN_ROWS - 2


NA_LAT_PAIRS_PER_STEP = 4


def _na_row_window(qr):
    kr = min(NA_WIN_ROWS, GRID_ROWS)
    return min(max(qr - kr // 2, 0), GRID_ROWS - kr), kr


def _na_chunk_all_valid(r0, kr0):
    return all(_na_row_window(qr)[0] <= kr < sum(_na_row_window(qr)) for qr in range(r0, r0 + 4) for kr in (kr0, kr0 + 1))


def _na_lat_kernel(q_ref, k_ref, v_ref, ck_ref, cv_ref, w_ref, m_ref, o_ref):
    rows = 4 * GRID_W
    half = _lane_half((rows, LANES))
    keep = _half_keep(half)
    caches = [(ck_ref[p * LANES:(p + 1) * LANES, :].astype(BF), cv_ref[p * LANES:(p + 1) * LANES, :].astype(BF))
              for p in range(NA_LAT_PAIRS_PER_STEP)]
    jobs = [(p, i) for p in range(NA_LAT_PAIRS_PER_STEP) for i in range(len(NA_TILES))]

    def key_rows(i):
        chunks = NA_TILES[i][1]
        return slice(chunks[0] * GRID_W, chunks[0] * GRID_W + len(chunks) * LANES)

    def score_fn(job):
        p, i = job
        cols = slice(p * LANES, (p + 1) * LANES)
        r0, chunks = NA_TILES[i]
        qm = _stack_halves(q_ref[i * rows:(i + 1) * rows, cols], keep)
        def bias_chunk(a, c, kr):
            blk = w_ref[p, a, (6 - kr + r0) * GRID_W:(6 - kr + r0) * GRID_W + rows, :]
            if _na_chunk_all_valid(r0, kr):
                return blk
            return blk + m_ref[i, :, c * LANES:(c + 1) * LANES]

        bias = jnp.concatenate(
            [jnp.concatenate([bias_chunk(a, c, kr) for c, kr in enumerate(chunks)], axis=1) for a in (0, 1)], axis=0)
        return [_dot_nt(qm, k_ref[key_rows(i), cols]) + bias, _dot(qm, caches[p][0])]

    def finish_fn(job, scores):
        p, i = job
        cols = slice(p * LANES, (p + 1) * LANES)
        o = _softmax_finish(scores, [(None, v_ref[key_rows(i), cols]), (None, caches[p][1], True)])
        o_ref[i * rows:(i + 1) * rows, cols] = jnp.where(half == 0, o[0:rows], o[rows:2 * rows]).astype(BF)

    _pipelined(jobs, score_fn, finish_fn)


def _na_lat_attention(qb, kb, vb, cache_k, cache_v, bias_tab, mask_tab):
    k0 = T_CTX // DEC_SEQ
    npair = NA_LAT_PAIRS_PER_STEP
    tok = pl.BlockSpec((DEC_SEQ, npair * LANES), lambda b, p: (k0 + b, p))
    c_spec = pl.BlockSpec((None, npair * LANES, PAST), lambda b, p: (b, p, 0))
    return pl.pallas_call(
        _na_lat_kernel,
        grid=(DEC_BATCH, NA_HEADS // 2 // npair),
        in_specs=[tok, tok, tok, c_spec, c_spec,
                  pl.BlockSpec((npair, 2, NA_BIAS_BLOCKS * GRID_W, LANES), lambda b, p: (p, 0, 0, 0)),
                  _const_spec(mask_tab.shape)],
        out_specs=pl.BlockSpec((DEC_SEQ, npair * LANES), lambda b, p: (b, p)),
        out_shape=jax.ShapeDtypeStruct((T_LAT, D), BF),
        compiler_params=_cparams(2),
        name="na_attn_lat",
    )(qb, kb, vb, cache_k, cache_v, bias_tab, mask_tab)


def _na_bias_kernel(t_ref, r_ref, n_ref, o_ref):
    t = t_ref[...]
    t1 = t.astype(BF)
    r1 = t - t1.astype(F32)
    t2 = r1.astype(BF)
    t3 = (r1 - t2.astype(F32)).astype(BF)
    r = r_ref[...]
    res = (_dot(t1, r) + _dot(t2, r) + _dot(t3, r) + n_ref[...]) * LOG2E
    for qc in range(GRID_W):
        o_ref[pl.ds(qc, t.shape[0], stride=GRID_W), :] = res[:, qc * LANES:(qc + 1) * LANES]


def _na_bias_table(rel_bias, onehot, neg):
    nrel = 2 * NA_WIN_COLS
    idx = 13 - np.arange(NA_BIAS_BLOCKS)[:, None] + np.arange(2)[None, :]
    t = jnp.pad(rel_bias[:, idx, :], ((0, 0), (0, 0), (0, 0), (0, 1)))
    t = t.reshape(NA_HEADS * NA_BIAS_BLOCKS, 2 * nrel)
    n = GRID_W * LANES
    out = pl.pallas_call(
        _na_bias_kernel,
        grid=(1,),
        in_specs=[pl.BlockSpec(t.shape, lambda j: (0, 0)),
                  pl.BlockSpec((2 * nrel, n), lambda j: (0, 0)),
                  pl.BlockSpec((1, n), lambda j: (0, 0))],
        out_specs=pl.BlockSpec((t.shape[0] * GRID_W, LANES), lambda j: (0, 0)),
        out_shape=jax.ShapeDtypeStruct((t.shape[0] * GRID_W, LANES), F32),
        compiler_params=_cparams(1),
        name="na_bias_table",
    )(t, onehot, neg)
    return out.reshape(NA_HEADS // 2, 2, NA_BIAS_BLOCKS * GRID_W, LANES)


def _na_constants():
    nrel = 2 * NA_WIN_COLS
    qc = np.arange(GRID_W)[:, None]
    kc = np.arange(GRID_W)[None, :]
    rel = np.clip(kc - qc, -(NA_WIN_COLS - 1), NA_WIN_COLS - 1) + NA_WIN_COLS - 1
    cs = np.clip(qc - NA_WIN_COLS // 2, 0, GRID_W - NA_WIN_COLS)
    col_in = (kc >= cs) & (kc < cs + NA_WIN_COLS)
    onehot = np.zeros((2, nrel, GRID_W, 2, GRID_W), np.float32)
    for hf in range(2):
        onehot[hf, rel, qc, hf, kc] = 1.0
    neg = np.where(col_in, 0.0, NEG_INF).astype(np.float32)
    neg = np.broadcast_to(neg[:, None, :], (GRID_W, 2, GRID_W)).reshape(1, -1)
    rows = 4 * GRID_W
    mask = np.full((len(NA_TILES), rows, NA_MAX_CHUNKS * LANES), NEG_INF, np.float32)
    kr = min(NA_WIN_ROWS, GRID_ROWS)
    for i, (r0, chunks) in enumerate(NA_TILES):
        qr = r0 + np.arange(rows)[:, None] // GRID_W
        rs = np.clip(qr - kr // 2, 0, GRID_ROWS - kr)
        for c, krow0 in enumerate(chunks):
            krow = krow0 + np.arange(LANES)[None, :] // GRID_W
            mask[i, :, c * LANES:(c + 1) * LANES] = np.where((krow >= rs) & (krow < rs + kr), 0.0, NEG_INF)
    return (jnp.asarray(onehot.reshape(2 * nrel, GRID_W * LANES), BF), jnp.asarray(neg), jnp.asarray(mask))


def _gq_attn_kernel(*refs, has_cache):
    if has_cache:
        q_ref, k_ref, v_ref, ck_ref, cv_ref, o_ref = refs
    else:
        q_ref, k_ref, v_ref, o_ref = refs
    group = GQ_HEADS // GQ_KV_HEADS
    qw = LANES * group
    tq = min(TQ // 2, q_ref.shape[0])
    half = _lane_half((tq, LANES))
    keep = _half_keep(half)
    nkvp = k_ref.shape[1] // LANES
    caches = []
    if has_cache:
        for kvp in range(nkvp):
            kcols = slice(kvp * LANES, (kvp + 1) * LANES)
            caches.append((ck_ref[kcols, :].astype(BF), cv_ref[kcols, :].astype(BF), True))

    def seg(job):
        kvp, t, _ = job
        kcols = slice(kvp * LANES, (kvp + 1) * LANES)
        if has_cache:
            return [(k_ref[:, kcols], v_ref[:, kcols]), caches[kvp]]
        return [(k_ref[t * tq:(t + 1) * tq, kcols], v_ref[t * tq:(t + 1) * tq, kcols])]

    jobs = [(kvp, t, kh) for t in range(q_ref.shape[0] // tq) for kvp in range(nkvp) for kh in (0, 1)]

    def blocks(job):
        kvp, t, kh = job
        for pair in (2 * kh, 2 * kh + 1):
            yield slice(t * tq, (t + 1) * tq), slice(kvp * qw + pair * LANES, kvp * qw + (pair + 1) * LANES)

    def score_fn(job):
        kh = job[2]
        parts = []
        for rows, cols in blocks(job):
            for a in (0, 1):
                qm = q_ref[rows, cols] * keep[a]
                parts.append(qm if a == kh else pltpu.roll(qm.astype(F32), HEAD_DIM, 1).astype(BF))
        return _scores(jnp.concatenate(parts, axis=0), seg(job))

    def finish_fn(job, scores):
        kh = job[2]
        o = _softmax_finish(scores, seg(job))
        for n, (rows, cols) in enumerate(blocks(job)):
            heads = [o[(2 * n + a) * tq:(2 * n + a + 1) * tq] for a in (0, 1)]
            heads = [h if a == kh else pltpu.roll(h, HEAD_DIM, 1) for a, h in enumerate(heads)]
            o_ref[rows, cols] = jnp.where(half == 0, heads[0], heads[1]).astype(BF)

    _pipelined(jobs, score_fn, finish_fn)


GQ_LAT_ROWS = 1024


def _gq_attention(qb, kb, vb, cache_k, cache_v):
    nk = GQ_KV_HEADS * HEAD_DIM
    qw = LANES * (GQ_HEADS // GQ_KV_HEADS)
    npair = GQ_KV_HEADS // 2
    o_ctx = pl.pallas_call(
        functools.partial(_gq_attn_kernel, has_cache=False),
        grid=(BATCH // CTX_BATCHES_PER_STEP,),
        in_specs=[pl.BlockSpec((CTX_ROWS, D), lambda b: (b, 0))] + [pl.BlockSpec((CTX_ROWS, nk), lambda b: (b, 0))] * 2,
        out_specs=pl.BlockSpec((CTX_ROWS, D), lambda b: (b, 0)),
        out_shape=jax.ShapeDtypeStruct((T_CTX, D), BF),
        compiler_params=_cparams(1),
        name="gq_attn_ctx",
    )(qb, kb, vb)
    qt = DEC_SEQ // GQ_LAT_ROWS
    q0, k0 = T_CTX // GQ_LAT_ROWS, T_CTX // DEC_SEQ
    kv_spec = pl.BlockSpec((DEC_SEQ, LANES), lambda b, p, t: (k0 + b, p))
    c_spec = pl.BlockSpec((None, LANES, PAST), lambda b, p, t: (b, p, 0))
    o_lat = pl.pallas_call(
        functools.partial(_gq_attn_kernel, has_cache=True),
        grid=(DEC_BATCH, npair, qt),
        in_specs=[pl.BlockSpec((GQ_LAT_ROWS, qw), lambda b, p, t: (q0 + b * qt + t, p)), kv_spec, kv_spec, c_spec,
                  c_spec],
        out_specs=pl.BlockSpec((GQ_LAT_ROWS, qw), lambda b, p, t: (b * qt + t, p)),
        out_shape=jax.ShapeDtypeStruct((T_LAT, D), BF),
        compiler_params=_cparams(3),
        name="gq_attn_lat",
    )(qb, kb, vb, cache_k, cache_v)
    return o_ctx, o_lat


def _dot_3pass(a, b):
    ah, bh = a.astype(BF), b.astype(BF)
    al, bl = (a - ah.astype(F32)).astype(BF), (b - bh.astype(F32)).astype(BF)
    return _dot(ah, bh) + _dot(ah, bl) + _dot(al, bh)


def _hy_filter_kernel(emb_ref, w1_ref, b1_ref, w2_ref, b2_ref, fr_ref, w3f_ref, w3b_ref, ldf_ref, ldb_ref,
                      c_ref, s_ref, hre_ref, him_ref, hny_ref, hid_ref, cb_ref, sb_ref):
    seq = emb_ref.shape[0]

    @pl.when((pl.program_id(0) == 0) & (pl.program_id(1) == 0))
    def _():
        hp = lax.Precision.HIGHEST
        fr = fr_ref[...]
        hid = jnp.sin(fr * (jnp.dot(emb_ref[...], w1_ref[...], precision=hp, preferred_element_type=F32)
                            + b1_ref[...]))
        hid_ref[...] = jnp.sin(fr * (jnp.dot(hid, w2_ref[...], precision=hp, preferred_element_type=F32)
                                     + b2_ref[...]))
        cb_ref[...] = c_ref[...].astype(BF)
        sb_ref[...] = s_ref[...].astype(BF)

    hid = hid_ref[...]
    t = emb_ref[:, 0:1]
    fwd = _dot_3pass(hid, w3f_ref[...]) * jnp.exp(-jnp.exp(ldf_ref[...]) * t)
    bwd = _dot_3pass(hid, w3b_ref[...]) * jnp.exp(-jnp.exp(ldb_ref[...]) * t)
    row = lax.broadcasted_iota(jnp.int32, fwd.shape, 0)
    bwd = jnp.where(row == 0, 0.0, bwd)
    even = fwd + bwd
    odd = bwd - fwd
    wk = jnp.where(row == 0, 0.5 / seq, 1.0 / seq)
    hre_ref[...] = _dot(cb_ref[...], even.astype(BF)) * wk
    him_ref[...] = _dot(sb_ref[...], odd.astype(BF)) * wk
    alt = jnp.where((row & 1) == 0, 1.0, -1.0)
    hny_ref[...] = jnp.sum(alt * even, axis=0, keepdims=True) * (0.5 / seq)


def _hy_filter(seq, emb, w1, b1, w2, b2, freq, w3, log_decay, cmat, smat):
    dc = 512
    nj = D // dc
    small = [_const_spec(a.shape) for a in (emb, w1, b1, w2, b2, freq)]
    return pl.pallas_call(
        _hy_filter_kernel,
        grid=(HY_ORDER, nj),
        in_specs=small + [pl.BlockSpec((HY_FFN, dc), lambda o, j: (0, (2 * o) * nj + j)),
                          pl.BlockSpec((HY_FFN, dc), lambda o, j: (0, (2 * o + 1) * nj + j)),
                          pl.BlockSpec((1, dc), lambda o, j: (0, (2 * o) * nj + j)),
                          pl.BlockSpec((1, dc), lambda o, j: (0, (2 * o + 1) * nj + j)),
                          _const_spec((seq, seq)), _const_spec((seq, seq))],
        out_specs=[pl.BlockSpec((None, seq, dc), lambda o, j: (o, 0, j)),
                   pl.BlockSpec((None, seq, dc), lambda o, j: (o, 0, j)),
                   pl.BlockSpec((None, 1, dc), lambda o, j: (o, 0, j))],
        out_shape=[jax.ShapeDtypeStruct((HY_ORDER, seq, D), F32), jax.ShapeDtypeStruct((HY_ORDER, seq, D), F32),
                   jax.ShapeDtypeStruct((HY_ORDER, 1, D), F32)],
        scratch_shapes=[pltpu.VMEM((seq, HY_FFN), F32), pltpu.VMEM((seq, seq), BF), pltpu.VMEM((seq, seq), BF)],
        compiler_params=_cparams(2),
        name=f"hy_filter_{seq}",
    )(emb, w1, b1, w2, b2, freq, w3, w3, log_decay, log_decay, cmat, smat)


HY_SUB = 256


def _hy_conv_kernel(u0_ref, u1_ref, u2_ref, sw0_ref, sw1_ref, sw2_ref, sb0_ref, sb1_ref, sb2_ref,
                    fb_ref, hre_ref, him_ref, hny_ref, c_ref, s_ref, o_ref, cb_ref, sb_ref):
    seq, dc = u0_ref.shape

    @pl.when((pl.program_id(0) == 0) & (pl.program_id(1) == 0))
    def _():
        cb_ref[...] = c_ref[...].astype(BF)
        sb_ref[...] = s_ref[...].astype(BF)

    row = lax.broadcasted_iota(jnp.int32, (seq, HY_SUB), 0)
    alt = jnp.where((row & 1) == 0, 1.0, -1.0)

    def sub_tile(cols):
        def short_conv(u_ref, w_ref, b_ref):
            u = u_ref[:, cols]
            prev = jnp.where(row == 0, 0.0, pltpu.roll(u, 1, 0))
            nxt = jnp.where(row == seq - 1, 0.0, pltpu.roll(u, seq - 1, 0))
            return prev * w_ref[0:1, cols] + u * w_ref[1:2, cols] + nxt * w_ref[2:3, cols] + b_ref[:, cols]

        z = short_conv(u0_ref, sw0_ref, sb0_ref)
        gates = (short_conv(u1_ref, sw1_ref, sb1_ref), short_conv(u2_ref, sw2_ref, sb2_ref))
        yield
        for o in range(HY_ORDER):
            zb = z.astype(BF)
            zc, zs = _dot(cb_ref[...], zb), _dot(sb_ref[...], zb)
            yield
            hre, him = hre_ref[o, :, cols], him_ref[o, :, cols]
            p_re = (zc * hre + zs * him).astype(BF)
            p_im = (zc * him - zs * hre).astype(BF)
            y = _dot(cb_ref[...], p_re) - _dot(sb_ref[...], p_im)
            yield
            nyq = jnp.sum(alt * z, axis=0, keepdims=True) * hny_ref[o, :, cols]
            z = gates[o] * (y + alt * nyq + z * fb_ref[o:o + 1, cols])
        o_ref[:, cols] = z.astype(BF)

    _lockstep(sub_tile(slice(j * HY_SUB, (j + 1) * HY_SUB)) for j in range(dc // HY_SUB))


def _hy_conv(u, short_w, short_b, filter_bias, hre, him, hny, cmat, smat, seq, nbatch, row0, dc):
    nj = D // dc
    r0 = row0 // seq

    def part(p):
        return pl.BlockSpec((seq, dc), lambda j, b: (r0 + b, p * nj + j))

    def vec(rows, p):
        return pl.BlockSpec((rows, dc), lambda j, b: (0, p * nj + j))

    in_specs = ([part(p) for p in range(3)] + [vec(3, p) for p in range(3)] + [vec(1, p) for p in range(3)]
                + [pl.BlockSpec((HY_ORDER, dc), lambda j, b: (0, j)),
                   pl.BlockSpec((HY_ORDER, seq, dc), lambda j, b: (0, 0, j), pipeline_mode=pl.Buffered(1)),
                   pl.BlockSpec((HY_ORDER, seq, dc), lambda j, b: (0, 0, j), pipeline_mode=pl.Buffered(1)),
                   pl.BlockSpec((HY_ORDER, 1, dc), lambda j, b: (0, 0, j)),
                   _const_spec((seq, seq)), _const_spec((seq, seq))])
    return pl.pallas_call(
        _hy_conv_kernel,
        grid=(nj, nbatch),
        in_specs=in_specs,
        out_specs=pl.BlockSpec((seq, dc), lambda j, b: (b, j)),
        out_shape=jax.ShapeDtypeStruct((nbatch * seq, D), BF),
        scratch_shapes=[pltpu.VMEM((seq, seq), BF), pltpu.VMEM((seq, seq), BF)],
        compiler_params=_cparams(2),
        name=f"hy_conv_{seq}",
    )(u, u, u, short_w, short_w, short_w, short_b, short_b, short_b, filter_bias, hre, him, hny, cmat, smat)


def _dft_tables(seq):
    k = np.arange(seq, dtype=np.int64)
    ang = np.pi * ((k[:, None] * k[None, :]) % (2 * seq)) / seq
    return jnp.asarray(np.cos(ang), F32), jnp.asarray(np.sin(ang), F32)


def _hy_embedding(seq):
    t = np.arange(seq, dtype=np.float32) / np.float32(seq)
    ang = (2.0 * math.pi) * t[:, None] * np.arange(1, HY_BANDS + 1, dtype=np.float32)
    emb = np.concatenate([t[:, None], np.cos(ang), np.sin(ang)], axis=-1).astype(np.float32)
    return jnp.asarray(np.pad(emb, ((0, 0), (0, HY_EMB_PAD - HY_EMB))))


def _post_kernel(*refs, split_x, split_out, tm):
    oc_ref, ol_ref = refs[0:2]
    x_refs, refs = (refs[2:4], refs[4:]) if split_x else (refs[2:3], refs[3:])
    mod_ref, wo_ref, g1_ref, b1_ref, w1c_ref, w2c_ref, g2_ref, b2_ref = refs[0:8]
    outs, (w1_ref, w2_ref, h_ref, acc_ref) = refs[8:-4], refs[-4:]
    x1_ref = outs[0]
    step = pl.program_id(0)
    is_lat = _is_lat(tm, N_FF_CHUNKS - 1)
    nsub = tm // SUB_POST
    per = MLP_CHUNK // FF_CHUNK

    def rows(j):
        return slice(j * SUB_POST, (j + 1) * SUB_POST)

    def pick(c_ref, l_ref, j):
        return jnp.where(is_lat, l_ref[rows(j), :], c_ref[rows(j), :])

    def norm1(j):
        a = _dot(pick(oc_ref, ol_ref, j), wo_ref[...])
        x = pick(x_refs[0], x_refs[1], j) if split_x else x_refs[0][rows(j), :]
        x1 = _layer_norm(DN_ALPHA * x + mod_ref[2:3, :] * a, g1_ref[...], b1_ref[...])
        return x1, _modulate(x1, mod_ref, 3, 4)

    def norm2(x1, acc):
        return _layer_norm(DN_ALPHA * x1 + mod_ref[5:6, :] * acc, g2_ref[...], b2_ref[...])

    def mlp_chunk(h, c):
        a = jnp.concatenate([_dot(h, w1_ref[per * c + i]) for i in range(per)], axis=1)
        a = jnp.maximum(a, 0.0)
        return _dot((a * a).astype(BF), w2_ref[c])

    def write_branched(ys):
        yc_ref, yl_ref = outs

        @pl.when(jnp.logical_not(is_lat))
        def _():
            for j, y in enumerate(ys):
                yc_ref[rows(j), :] = y

        @pl.when(is_lat)
        def _():
            for j, y in enumerate(ys):
                yl_ref[rows(j), :] = y

    @pl.when(step < N_FF_CHUNKS)
    def _():
        w2_rows = pl.ds(pl.multiple_of((step % per) * FF_CHUNK, FF_CHUNK), FF_CHUNK)
        w1_ref[step] = w1c_ref[...].astype(BF)
        w2_ref[step // per, w2_rows, :] = w2c_ref[...].astype(BF)

        @pl.when(step == 0)
        def _():
            for j in range(nsub):
                x1_ref[rows(j), :], h_ref[rows(j), :] = norm1(j)
            acc_ref[...] = jnp.zeros_like(acc_ref)

        a = jnp.maximum(_dot(h_ref[...], w1_ref[step]), 0.0)
        acc_ref[...] += _dot((a * a).astype(BF), w2_ref[step // per, w2_rows, :])

        @pl.when(step == N_FF_CHUNKS - 1)
        def _():
            ys = [norm2(x1_ref[rows(j), :], acc_ref[rows(j), :]) for j in range(nsub)]
            if split_out:
                write_branched(ys)
            else:
                for j, y in enumerate(ys):
                    outs[0][rows(j), :] = y

    def token_tile():
        ys = []
        cur = norm1(0)
        prev = None
        for j in range(nsub):
            x1, h = cur
            acc = mlp_chunk(h, 0)
            if j + 1 < nsub:
                cur = norm1(j + 1)
            if prev is not None:
                ys.append(norm2(*prev))
                if not split_out:
                    outs[0][rows(j - 1), :] = ys[-1]
            for c in range(1, D_FF // MLP_CHUNK):
                acc = acc + mlp_chunk(h, c)
            prev = (x1, acc)
        ys.append(norm2(*prev))
        if split_out:
            write_branched(ys)
        else:
            outs[0][rows(nsub - 1), :] = ys[-1]

    pl.when(step >= N_FF_CHUNKS)(token_tile)


def _post(o_ctx, o_lat, xs, mods, layer, w_o, g1, b1, w1, w2, g2, b2, split_out):
    tm, off = TM_POST, N_FF_CHUNKS - 1
    split_x = len(xs) == 2
    x_specs = [_ctx_spec(D, tm, off), _lat_spec(D, tm, off)] if split_x else [_tok_spec(D, tm, off)]
    vec = _const_spec((1, D))
    if split_out:
        out_specs = [_ctx_spec(D, tm, off), _lat_spec(D, tm, off)]
        out_shape = [jax.ShapeDtypeStruct((T_CTX, D), F32), jax.ShapeDtypeStruct((T_LAT, D), F32)]
    else:
        out_specs = _tok_spec(D, tm, off)
        out_shape = jax.ShapeDtypeStruct((T, D), F32)

    def chunk(i):
        return jnp.minimum(i, N_FF_CHUNKS - 1)

    return pl.pallas_call(
        functools.partial(_post_kernel, split_x=split_x, split_out=split_out, tm=tm),
        grid=(off + T // tm,),
        in_specs=[_ctx_spec(D, tm, off), _lat_spec(D, tm, off)] + x_specs + [
            _mod_spec(layer, tm, off), _const_spec((D, D)), vec, vec,
            pl.BlockSpec((None, D, FF_CHUNK), lambda i: (layer, 0, chunk(i))),
            pl.BlockSpec((None, FF_CHUNK, D), lambda i: (layer, chunk(i), 0)), vec, vec],
        out_specs=out_specs,
        out_shape=out_shape,
        scratch_shapes=[pltpu.VMEM((N_FF_CHUNKS, D, FF_CHUNK), BF), pltpu.VMEM((D_FF // MLP_CHUNK, MLP_CHUNK, D), BF),
                        pltpu.VMEM((tm, D), BF), pltpu.VMEM((tm, D), F32)],
        compiler_params=_cparams(1, POST_VMEM_LIMIT),
        name=f"post_l{layer}",
    )(o_ctx, o_lat, *xs, mods, w_o, g1, b1, w1, w2, g2, b2)


def _rope_tables():
    n = HEAD_DIM // 4
    pos = np.arange(DEC_SEQ)
    inv = (np.float32(ROPE_BASE) ** (-np.arange(n, dtype=np.float32) / np.float32(n))).astype(np.float32)
    ang_r = ((pos // GRID_W).astype(np.float32)[:, None] * inv).astype(np.float32)
    ang_c = ((pos % GRID_W).astype(np.float32)[:, None] * inv).astype(np.float32)
    cr, sr, cc, sc = np.cos(ang_r), np.sin(ang_r), np.cos(ang_c), np.sin(ang_c)
    a = np.tile(np.concatenate([cr, cr, cc, cc], axis=-1), (1, D // HEAD_DIM))
    b = np.tile(np.concatenate([-sr, sr, -sc, sc], axis=-1), (1, D // HEAD_DIM))
    a = np.concatenate([a, np.ones((TM, D), np.float32)], axis=0)
    b = np.concatenate([b, np.zeros((TM, D), np.float32)], axis=0)
    return jnp.asarray(a, F32), jnp.asarray(b, F32)


def kernel(x_prompt, x_sample, c, cache_da_k, cache_da_v, cache_na_k, cache_na_v, cache_gq_k, cache_gq_v, c_ctx, ada_w, ada_b, ln_g, ln_b, mlp_w1, mlp_w2, da_w_qkv, da_w_o, da_lambda, da_subln_g, na_w_qkv, na_w_o, na_rel_bias, gq_w_qkv, gq_w_o, gq_q_norm, gq_k_norm, hy_w_in, hy_short_w, hy_short_b, hy_ffn_w1, hy_ffn_b1, hy_ffn_w2, hy_ffn_b2, hy_ffn_freq, hy_ffn_w3, hy_log_decay, hy_filter_bias, hy_w_o):
    cvec = jnp.concatenate([c_ctx[None, :], c, jnp.zeros((MOD_ROWS - 1 - DEC_BATCH, D), F32)], axis=0)
    mods = _mods(cvec, ada_w, ada_b)
    rope_a, rope_b = _rope_tables()

    def finish(o_ctx, o_lat, xs, layer, w_o, split_out=False):
        return _post(o_ctx, o_lat, xs, mods, layer, w_o.astype(BF), ln_g[layer, 0][None], ln_b[layer, 0][None],
                     mlp_w1, mlp_w2, ln_g[layer, 1][None], ln_b[layer, 1][None], split_out)

    xs = (x_prompt.reshape(T_CTX, D), x_sample.reshape(T_LAT, D))
    qb, kb, vb, ks, vs = _da_proj(*xs, mods, 0, da_w_qkv[0].astype(BF), rope_a, rope_b)
    state_da_k = ks.reshape(BATCH, 1, SEQ, DA_HEADS, 2 * HEAD_DIM)
    state_da_v = vs.reshape(BATCH, 1, SEQ, DA_HEADS, 2 * HEAD_DIM)
    o_ctx, o_lat = _da_attention(qb, kb, vb, cache_da_k, cache_da_v, da_lambda[0], da_subln_g[0][None], 0)
    x = finish(o_ctx, o_lat, xs, 0, da_w_o[0])

    qb, kb, vb, ks, vs = _na_proj(x, mods, 1, na_w_qkv[0].astype(BF))
    state_na_k, state_na_v = _untranspose_state(ks, NA_HEADS), _untranspose_state(vs, NA_HEADS)
    onehot, neg, mask = _na_constants()
    bias_tab = _na_bias_table(na_rel_bias[0], onehot, neg)
    o_ctx = _na_ctx_attention(qb, kb, vb)
    o_lat = _na_lat_attention(qb, kb, vb, _features_major(cache_na_k), _features_major(cache_na_v), bias_tab, mask)
    x = finish(o_ctx, o_lat, (x,), 1, na_w_o[0])

    g_mat = jnp.asarray(np.kron(np.eye(GN_BLOCK // HEAD_DIM), np.full((HEAD_DIM, HEAD_DIM), 1.0 / HEAD_DIM)), BF)
    qb, kb, vb, ks, vs = _gq_proj(x, mods, 2, gq_w_qkv[0].astype(BF), g_mat,
                                  jnp.tile(gq_q_norm[0], GQ_HEADS)[None], jnp.tile(gq_k_norm[0], GQ_KV_HEADS)[None],
                                  rope_a, rope_b)
    state_gq_k, state_gq_v = _untranspose_state(ks, GQ_KV_HEADS), _untranspose_state(vs, GQ_KV_HEADS)
    o_ctx, o_lat = _gq_attention(qb, kb, vb, _features_major(cache_gq_k), _features_major(cache_gq_v))
    x = finish(o_ctx, o_lat, (x,), 2, gq_w_o[0])

    u = _hy_proj(x, mods, 3, hy_w_in[0].astype(BF))
    w1 = jnp.pad(hy_ffn_w1[0], ((0, HY_EMB_PAD - HY_EMB), (0, 0)))
    zs = []
    for seq, nbatch, row0, dc in ((SEQ, BATCH, 0, D), (DEC_SEQ, DEC_BATCH, T_CTX, 512)):
        cmat, smat = _dft_tables(seq)
        hre, him, hny = _hy_filter(seq, _hy_embedding(seq), w1, hy_ffn_b1[0][None], hy_ffn_w2[0], hy_ffn_b2[0][None],
                                   hy_ffn_freq[0][None], hy_ffn_w3[0], hy_log_decay[0][None], cmat, smat)
        zs.append(_hy_conv(u, hy_short_w[0], hy_short_b[0][None], hy_filter_bias[0], hre, him, hny, cmat, smat,
                           seq, nbatch, row0, dc))
    y_ctx, y_lat = finish(zs[0], zs[1], (x,), 3, hy_w_o[0], split_out=True)

    return (y_ctx.reshape(BATCH, SEQ, D), y_lat.reshape(DEC_BATCH, DEC_SEQ, D),
            state_da_k, state_da_v, state_na_k, state_na_v, state_gq_k, state_gq_v)
```

```python
import functools
import math

import numpy as np
import jax
import jax.numpy as jnp
from jax import lax
from jax.experimental import pallas as pl
from jax.experimental.pallas import tpu as pltpu

F32 = jnp.float32
BF = jnp.bfloat16

D = 1024
BATCH = 16
SEQ = 256
DEC_BATCH = 8
DEC_SEQ = 1024
PAST = 256
DEPTH = 4
GRID_W = 64
GRID_ROWS = DEC_SEQ // GRID_W
D_FF = 4 * D
T_CTX = BATCH * SEQ
T_LAT = DEC_BATCH * DEC_SEQ
T = T_CTX + T_LAT
HEAD_DIM = 64
ATT_SCALE = HEAD_DIM ** -0.5
LOG2E = math.log2(math.e)
Q_SCALE = ATT_SCALE * LOG2E
DA_HEADS = 8
NA_HEADS = 16
NA_WIN_ROWS = 8
NA_WIN_COLS = 16
GQ_HEADS = 16
GQ_KV_HEADS = 4
HY_ORDER = 2
HY_BANDS = 16
HY_EMB = 1 + 2 * HY_BANDS
HY_EMB_PAD = 40
HY_FFN = 64
ROPE_BASE = 10000.0
LN_EPS = 1e-5
RMS_EPS = 1e-6
DN_ALPHA = (2 * DEPTH) ** 0.25
NEG_INF = -1e30

LANES = 128
TM = 512
TM_POST = 512
FF_CHUNK = 512
MLP_CHUNK = 1024
N_FF_CHUNKS = D_FF // FF_CHUNK
SUB_POST = 256
N_CTX_TILES = T_CTX // TM
N_TILES = T // TM
TQ = 512
CTX_BATCHES_PER_STEP = 4
CTX_ROWS = CTX_BATCHES_PER_STEP * SEQ
MOD_ROWS = 16
VMEM_LIMIT = 56 * 1024 * 1024
POST_VMEM_LIMIT = 58 * 1024 * 1024


def _cparams(n_axes, vmem_limit=VMEM_LIMIT):
    return pltpu.CompilerParams(dimension_semantics=("arbitrary",) * n_axes,
                                vmem_limit_bytes=vmem_limit)


def _dot(a, b):
    return jnp.dot(a, b, preferred_element_type=F32)


def _dot_nt(a, b):
    return lax.dot_general(a, b, (((1,), (1,)), ((), ())), preferred_element_type=F32)


def _const_spec(shape):
    nd = len(shape)
    return pl.BlockSpec(shape, lambda *_: (0,) * nd, pipeline_mode=pl.Buffered(1))


def _mod_spec(layer, tm=TM, off=0):
    nctx = T_CTX // tm

    def row(i):
        t = jnp.maximum(i - off, 0)
        return jnp.where(t < nctx, 0, 1 + (t - nctx) // (DEC_SEQ // tm))

    return pl.BlockSpec((None, None, 6, D), lambda i: (layer, row(i), 0, 0))


def _tok_spec(width, tm=TM, off=0):
    return pl.BlockSpec((tm, width), lambda i: (jnp.maximum(i - off, 0), 0))


def _ctx_spec(width, tm=TM, off=0):
    return pl.BlockSpec((tm, width), lambda i: (jnp.clip(i - off, 0, T_CTX // tm - 1), 0))


def _lat_spec(width, tm=TM, off=0):
    return pl.BlockSpec((tm, width), lambda i: (jnp.maximum(i - off - T_CTX // tm, 0), 0))


def _is_lat(tm=TM, off=0):
    return pl.program_id(0) >= off + T_CTX // tm


def _pick(ctx_ref, lat_ref):
    return jnp.where(_is_lat(), lat_ref[...], ctx_ref[...])


def _layer_norm(r, g, b):
    mu = jnp.mean(r, axis=-1, keepdims=True)
    c = r - mu
    var = jnp.mean(c * c, axis=-1, keepdims=True)
    return c * lax.rsqrt(var + LN_EPS) * g + b


def _mods_kernel(c_ref, w_ref, b_ref, o_ref):
    c = c_ref[...]
    s = (c / (1.0 + jnp.exp(-c))).astype(BF)
    o_ref[...] = _dot(s, w_ref[...].astype(BF)) + b_ref[...]


def _mods(cvec, ada_w, ada_b):
    tn = 1536
    out = pl.pallas_call(
        _mods_kernel,
        grid=(DEPTH, 6 * D // tn),
        in_specs=[pl.BlockSpec((MOD_ROWS, D), lambda l, n: (0, 0)),
                  pl.BlockSpec((None, D, tn), lambda l, n: (l, 0, n)),
                  pl.BlockSpec((None, 1, tn), lambda l, n: (l, 0, n))],
        out_specs=pl.BlockSpec((None, MOD_ROWS, tn), lambda l, n: (l, 0, n)),
        out_shape=jax.ShapeDtypeStruct((DEPTH, MOD_ROWS, 6 * D), F32),
        compiler_params=_cparams(2),
        name="adaln_mods",
    )(cvec, ada_w, ada_b.reshape(DEPTH, 1, 6 * D))
    return out.reshape(DEPTH, MOD_ROWS, 6, D)


def _modulate(x, mod_ref, shift, scale):
    return (x * (1.0 + mod_ref[scale:scale + 1, :]) + mod_ref[shift:shift + 1, :]).astype(BF)


def _rope(x, a, b):
    n = x.shape[1]
    lane = lax.broadcasted_iota(jnp.int32, x.shape, 1)
    partner = jnp.where((lane & 16) == 0, pltpu.roll(x, n - 16, 1), pltpu.roll(x, 16, 1))
    return x * a + partner * b


def _rope_spec(width):
    per = DEC_SEQ // TM
    return pl.BlockSpec((TM, width), lambda i: (jnp.where(i < N_CTX_TILES, per, (i - N_CTX_TILES) % per), 0))


def _lockstep(gens):
    waiting, active = list(gens), []
    while waiting or active:
        if waiting:
            active.append(waiting.pop(0))
        active = [g for g in active if next(g, True) is None]


def _store_state(k, v, ks_ref, vs_ref, transposed):
    @pl.when(jnp.logical_not(_is_lat()))
    def _():
        if not transposed:
            ks_ref[...] = k
            vs_ref[...] = v
        else:
            n = k.shape[0]
            for xt, ref in ((k, ks_ref), (v, vs_ref)):
                for j in range(TM // SEQ):
                    ref[j * n:(j + 1) * n, :] = xt[:, j * SEQ:(j + 1) * SEQ]


def _qkv_out(nq, nk, transposed_state):
    specs = [_tok_spec(nq), _tok_spec(nk), _tok_spec(nk)]
    shapes = [jax.ShapeDtypeStruct((T, nq), BF), jax.ShapeDtypeStruct((T, nk), BF), jax.ShapeDtypeStruct((T, nk), BF)]
    if transposed_state:
        rows = (TM // SEQ) * nk
        specs += [pl.BlockSpec((rows, SEQ), lambda i: (jnp.minimum(i, N_CTX_TILES - 1), 0))] * 2
        shapes += [jax.ShapeDtypeStruct((BATCH * nk, SEQ), F32)] * 2
    else:
        specs += [_ctx_spec(nk)] * 2
        shapes += [jax.ShapeDtypeStruct((T_CTX, nk), F32)] * 2
    return specs, shapes


def _features_major(cache):
    b, _, past, heads, dh = cache.shape
    return cache.transpose(0, 1, 3, 4, 2).reshape(b, heads * dh, past)


def _untranspose_state(st, heads):
    return st.reshape(BATCH, heads, HEAD_DIM, SEQ).transpose(0, 3, 1, 2)[:, None]


def _da_proj_kernel(xc_ref, xl_ref, mod_ref, w_ref, ra_ref, rb_ref, qb_ref, kb_ref, vb_ref, ks_ref, vs_ref):
    h = _modulate(_pick(xc_ref, xl_ref), mod_ref, 0, 1)
    a, b = ra_ref[...], rb_ref[...]
    q = _dot(h, w_ref[:, 0:D])
    k = _dot(h, w_ref[:, D:2 * D])
    qb_ref[...] = (_rope(q, a, b) * Q_SCALE).astype(BF)
    v = _dot(h, w_ref[:, 2 * D:3 * D])
    kb_ref[...] = _rope(k, a, b).astype(BF)
    vb_ref[...] = v.astype(BF)
    _store_state(k, v, ks_ref, vs_ref, False)


def _da_proj(x_ctx, x_lat, mods, layer, w, rope_a, rope_b):
    specs, shapes = _qkv_out(D, D, False)
    return pl.pallas_call(
        _da_proj_kernel,
        grid=(N_TILES,),
        in_specs=[_ctx_spec(D), _lat_spec(D), _mod_spec(layer), _const_spec((D, 3 * D)),
                  _rope_spec(D), _rope_spec(D)],
        out_specs=specs, out_shape=shapes,
        compiler_params=_cparams(1),
        name=f"da_proj_l{layer}",
    )(x_ctx, x_lat, mods, w, rope_a, rope_b)


def _na_proj_kernel(x_ref, mod_ref, w_ref, qb_ref, kb_ref, vb_ref, ks_ref, vs_ref):
    h = _modulate(x_ref[...], mod_ref, 0, 1)
    v = _dot(h, w_ref[:, 2 * D:3 * D])
    k = _dot(h, w_ref[:, D:2 * D])
    vb_ref[...] = v.astype(BF)
    vt = v.T
    q = _dot(h, w_ref[:, 0:D])
    kb_ref[...] = k.astype(BF)
    kt = k.T
    qb_ref[...] = (q * Q_SCALE).astype(BF)
    _store_state(kt, vt, ks_ref, vs_ref, True)


def _na_proj(x, mods, layer, w):
    specs, shapes = _qkv_out(D, D, True)
    return pl.pallas_call(
        _na_proj_kernel,
        grid=(N_TILES,),
        in_specs=[_tok_spec(D), _mod_spec(layer), _const_spec((D, 3 * D))],
        out_specs=specs, out_shape=shapes,
        compiler_params=_cparams(1),
        name=f"na_proj_l{layer}",
    )(x, mods, w)


GN_BLOCK = 256


def _head_rms(x, g_ref, gain):
    x2 = x * x
    hi = x2.astype(BF)
    lo = (x2 - hi.astype(F32)).astype(BF)
    g = g_ref[...]
    ms = jnp.concatenate(
        [_dot(hi[:, j:j + GN_BLOCK], g) + _dot(lo[:, j:j + GN_BLOCK], g) for j in range(0, x.shape[1], GN_BLOCK)],
        axis=1)
    return x * lax.rsqrt(ms + RMS_EPS) * gain


def _gq_proj_kernel(x_ref, mod_ref, w_ref, g_ref, qn_ref, kn_ref, ra_ref, rb_ref,
                    qb_ref, kb_ref, vb_ref, ks_ref, vs_ref):
    nq, nk = GQ_HEADS * HEAD_DIM, GQ_KV_HEADS * HEAD_DIM
    h = _modulate(x_ref[...], mod_ref, 0, 1)

    def finish_q(c, qc):
        cols = slice(c * GN_BLOCK, (c + 1) * GN_BLOCK)
        qc = _head_rms(qc, g_ref, qn_ref[:, cols])
        qb_ref[:, cols] = (_rope(qc, ra_ref[:, cols], rb_ref[:, cols]) * Q_SCALE).astype(BF)

    k = _dot(h, w_ref[:, nq:nq + nk])
    v = _dot(h, w_ref[:, nq + nk:nq + 2 * nk])
    prev = None
    for c in range(nq // GN_BLOCK):
        qc = _dot(h, w_ref[:, c * GN_BLOCK:(c + 1) * GN_BLOCK])
        if c == 0:
            k = _head_rms(k, g_ref, kn_ref[...])
            kb_ref[...] = _rope(k, ra_ref[:, 0:nk], rb_ref[:, 0:nk]).astype(BF)
            vb_ref[...] = v.astype(BF)
            kt, vt = k.T, v.T
        else:
            finish_q(*prev)
        prev = (c, qc)
    finish_q(*prev)
    _store_state(kt, vt, ks_ref, vs_ref, True)


def _gq_proj(x, mods, layer, w, g_mat, qn, kn, rope_a, rope_b):
    nq, nk = GQ_HEADS * HEAD_DIM, GQ_KV_HEADS * HEAD_DIM
    specs, shapes = _qkv_out(nq, nk, True)
    return pl.pallas_call(
        _gq_proj_kernel,
        grid=(N_TILES,),
        in_specs=[_tok_spec(D), _mod_spec(layer), _const_spec((D, nq + 2 * nk)),
                  _const_spec((GN_BLOCK, GN_BLOCK)), _const_spec((1, nq)), _const_spec((1, nk)),
                  _rope_spec(D), _rope_spec(D)],
        out_specs=specs, out_shape=shapes,
        compiler_params=_cparams(1),
        name=f"gq_proj_l{layer}",
    )(x, mods, w, g_mat, qn, kn, rope_a, rope_b)


def _hy_proj_kernel(x_ref, mod_ref, w_ref, u_ref):
    h = _modulate(x_ref[...], mod_ref, 0, 1)
    for c in range(HY_ORDER + 1):
        u_ref[:, c * D:(c + 1) * D] = _dot(h, w_ref[:, c * D:(c + 1) * D])


TM_HY_PROJ = 1024


def _hy_proj(x, mods, layer, w):
    n = (HY_ORDER + 1) * D
    return pl.pallas_call(
        _hy_proj_kernel,
        grid=(T // TM_HY_PROJ,),
        in_specs=[_tok_spec(D, TM_HY_PROJ), _mod_spec(layer, TM_HY_PROJ), _const_spec((D, n))],
        out_specs=_tok_spec(n, TM_HY_PROJ),
        out_shape=jax.ShapeDtypeStruct((T, n), F32),
        compiler_params=_cparams(1),
        name=f"hy_proj_l{layer}",
    )(x, mods, w)


def _scores(qm, segs):
    return [_dot(qm, seg[0]) if len(seg) == 3 else _dot_nt(qm, seg[0]) for seg in segs]


def _softmax_finish(scores, segs):
    m = scores[0].max(axis=-1, keepdims=True)
    for s in scores[1:]:
        m = jnp.maximum(m, s.max(axis=-1, keepdims=True))
    den = None
    out = None
    for s, seg in zip(scores, segs):
        e = jnp.exp2(s - m)
        d = e.sum(axis=-1, keepdims=True)
        o = _dot_nt(e.astype(BF), seg[1]) if len(seg) == 3 else _dot(e.astype(BF), seg[1])
        den = d if den is None else den + d
        out = o if out is None else out + o
    return out / den


def _stack_halves(q, keep):
    return jnp.concatenate([q * keep[0], q * keep[1]], axis=0)


def _pipelined(jobs, score_fn, finish_fn):
    nxt = score_fn(jobs[0])
    for n, job in enumerate(jobs):
        cur, nxt = nxt, (score_fn(jobs[n + 1]) if n + 1 < len(jobs) else None)
        finish_fn(job, cur)


def _lane_half(shape):
    return lax.broadcasted_iota(jnp.int32, shape, 1) // HEAD_DIM


def _half_keep(half):
    return tuple(jnp.where(half == a, 1.0, 0.0).astype(BF) for a in (0, 1))


def _da_attn_kernel(*refs, has_cache, lam_init):
    if has_cache:
        q_ref, k_ref, v_ref, ck_ref, cv_ref, lam_ref, g_ref, o_ref = refs
    else:
        q_ref, k_ref, v_ref, lam_ref, g_ref, o_ref = refs
    lp = lam_ref[...]
    lam = (jnp.exp(jnp.sum(lp[0:1] * lp[1:2], axis=-1, keepdims=True))
           - jnp.exp(jnp.sum(lp[2:3] * lp[3:4], axis=-1, keepdims=True)) + lam_init)
    gain = g_ref[...] * (1.0 - lam_init)
    w = 2 * HEAD_DIM
    nheads = k_ref.shape[1] // w
    tq = min(TQ // 2, q_ref.shape[0]) if has_cache else SEQ
    keep = _half_keep(_lane_half((tq, w)))
    caches = []
    if has_cache:
        for hd in range(nheads):
            head = pl.program_id(1) * nheads + hd
            caches.append((ck_ref[:, head, :].astype(BF), cv_ref[:, head, :].astype(BF)))

    def seg(hd, t):
        cols = slice(hd * w, (hd + 1) * w)
        if has_cache:
            return [(k_ref[:, cols], v_ref[:, cols]), caches[hd]]
        return [(k_ref[t * tq:(t + 1) * tq, cols], v_ref[t * tq:(t + 1) * tq, cols])]

    jobs = [(hd, t) for t in range(q_ref.shape[0] // tq) for hd in range(nheads)]

    def score_fn(job):
        hd, t = job
        return _scores(_stack_halves(q_ref[t * tq:(t + 1) * tq, hd * w:(hd + 1) * w], keep), seg(hd, t))

    def finish_fn(job, scores):
        hd, t = job
        if not has_cache:
            o = _softmax_finish(scores, seg(hd, t))
            o = o[0:tq] - lam * o[tq:2 * tq]
            ms = jnp.mean(o * o, axis=-1, keepdims=True)
            o_ref[t * tq:(t + 1) * tq, hd * w:(hd + 1) * w] = (o * lax.rsqrt(ms + RMS_EPS) * gain).astype(BF)
            return
        m = scores[0].max(axis=-1, keepdims=True)
        for s in scores[1:]:
            m = jnp.maximum(m, s.max(axis=-1, keepdims=True))
        es = [jnp.exp2(s - m) for s in scores]
        den = es[0].sum(axis=-1, keepdims=True)
        for e in es[1:]:
            den = den + e.sum(axis=-1, keepdims=True)
        inv = 1.0 / den
        scale1, scale2 = inv[0:tq], lam * inv[tq:2 * tq]
        o = None
        for e, (_, v) in zip(es, seg(hd, t)):
            part = _dot((e[0:tq] * scale1 - e[tq:2 * tq] * scale2).astype(BF), v)
            o = part if o is None else o + part
        ms = jnp.mean(o * o, axis=-1, keepdims=True)
        o_ref[t * tq:(t + 1) * tq, hd * w:(hd + 1) * w] = (o * lax.rsqrt(ms + RMS_EPS) * gain).astype(BF)

    _pipelined(jobs, score_fn, finish_fn)


DA_LAT_HEADS_PER_STEP = 2


def _da_attention(qb, kb, vb, cache_k, cache_v, lam_p, subln_g, layer_idx):
    lam_init = 0.8 - 0.6 * math.exp(-0.3 * layer_idx)
    w = 2 * HEAD_DIM
    small = [pl.BlockSpec((4, HEAD_DIM), lambda *_: (0, 0)), pl.BlockSpec((1, w), lambda *_: (0, 0))]
    o_ctx = pl.pallas_call(
        functools.partial(_da_attn_kernel, has_cache=False, lam_init=lam_init),
        grid=(BATCH,),
        in_specs=[pl.BlockSpec((SEQ, D), lambda b: (b, 0))] * 3 + small,
        out_specs=pl.BlockSpec((SEQ, D), lambda b: (b, 0)),
        out_shape=jax.ShapeDtypeStruct((T_CTX, D), BF),
        compiler_params=_cparams(1),
        name="da_attn_ctx",
    )(qb, kb, vb, lam_p, subln_g)
    k0 = T_CTX // DEC_SEQ
    hw = DA_LAT_HEADS_PER_STEP * w
    tok = pl.BlockSpec((DEC_SEQ, hw), lambda b, h: (k0 + b, h))
    c_spec = pl.BlockSpec((None, None, PAST, DA_HEADS, w), lambda b, h: (b, 0, 0, 0, 0))
    o_lat = pl.pallas_call(
        functools.partial(_da_attn_kernel, has_cache=True, lam_init=lam_init),
        grid=(DEC_BATCH, DA_HEADS // DA_LAT_HEADS_PER_STEP),
        in_specs=[tok, tok, tok, c_spec, c_spec] + small,
        out_specs=pl.BlockSpec((DEC_SEQ, hw), lambda b, h: (b, h)),
        out_shape=jax.ShapeDtypeStruct((T_LAT, D), BF),
        compiler_params=_cparams(2),
        name="da_attn_lat",
    )(qb, kb, vb, cache_k, cache_v, lam_p, subln_g)
    return o_ctx, o_lat


def _na_ctx_kernel(q_ref, k_ref, v_ref, o_ref):
    half = _lane_half((SEQ, LANES))
    keep = _half_keep(half)

    def block(job):
        b, p = job
        return slice(b * SEQ, (b + 1) * SEQ), slice(p * LANES, (p + 1) * LANES)

    def seg(job):
        return [(k_ref[block(job)], v_ref[block(job)])]

    def score_fn(job):
        return _scores(_stack_halves(q_ref[block(job)], keep), seg(job))

    def finish_fn(job, scores):
        o = _softmax_finish(scores, seg(job))
        o_ref[block(job)] = jnp.where(half == 0, o[0:SEQ], o[SEQ:2 * SEQ]).astype(BF)

    jobs = [(b, p) for b in range(q_ref.shape[0] // SEQ) for p in range(NA_HEADS // 2)]
    _pipelined(jobs, score_fn, finish_fn)


def _na_ctx_attention(qb, kb, vb):
    spec = pl.BlockSpec((CTX_ROWS, D), lambda b: (b, 0))
    return pl.pallas_call(
        _na_ctx_kernel,
        grid=(BATCH // CTX_BATCHES_PER_STEP,),
        in_specs=[spec] * 3,
        out_specs=spec,
        out_shape=jax.ShapeDtypeStruct((T_CTX, D), BF),
        compiler_params=_cparams(1),
        name="na_attn_ctx",
    )(qb, kb, vb)


NA_TILES = ((0, (0, 2, 4, 6)), (4, (0, 2, 4, 6, 8, 10)), (8, (4, 6, 8, 10, 12, 14)), (12, (8, 10, 12, 14)))
NA_MAX_CHUNKS = 6
NA_BIAS_BLOCKS = 2 * NA_WIN_ROWS - 2


NA_LAT_PAIRS_PER_STEP = 4


def _na_row_window(qr):
    kr = min(NA_WIN_ROWS, GRID_ROWS)
    return min(max(qr - kr // 2, 0), GRID_ROWS - kr), kr


def _na_chunk_all_valid(r0, kr0):
    return all(_na_row_window(qr)[0] <= kr < sum(_na_row_window(qr)) for qr in range(r0, r0 + 4) for kr in (kr0, kr0 + 1))


def _na_lat_kernel(q_ref, k_ref, v_ref, ck_ref, cv_ref, w_ref, m_ref, o_ref):
    rows = 4 * GRID_W
    half = _lane_half((rows, LANES))
    keep = _half_keep(half)
    caches = [(ck_ref[p * LANES:(p + 1) * LANES, :].astype(BF), cv_ref[p * LANES:(p + 1) * LANES, :].astype(BF))
              for p in range(NA_LAT_PAIRS_PER_STEP)]
    jobs = [(p, i) for p in range(NA_LAT_PAIRS_PER_STEP) for i in range(len(NA_TILES))]

    def key_rows(i):
        chunks = NA_TILES[i][1]
        return slice(chunks[0] * GRID_W, chunks[0] * GRID_W + len(chunks) * LANES)

    def score_fn(job):
        p, i = job
        cols = slice(p * LANES, (p + 1) * LANES)
        r0, chunks = NA_TILES[i]
        qm = _stack_halves(q_ref[i * rows:(i + 1) * rows, cols], keep)
        def bias_chunk(a, c, kr):
            blk = w_ref[p, a, (6 - kr + r0) * GRID_W:(6 - kr + r0) * GRID_W + rows, :]
            if _na_chunk_all_valid(r0, kr):
                return blk
            return blk + m_ref[i, :, c * LANES:(c + 1) * LANES]

        bias = jnp.concatenate(
            [jnp.concatenate([bias_chunk(a, c, kr) for c, kr in enumerate(chunks)], axis=1) for a in (0, 1)], axis=0)
        return [_dot_nt(qm, k_ref[key_rows(i), cols]) + bias, _dot(qm, caches[p][0])]

    def finish_fn(job, scores):
        p, i = job
        cols = slice(p * LANES, (p + 1) * LANES)
        o = _softmax_finish(scores, [(None, v_ref[key_rows(i), cols]), (None, caches[p][1], True)])
        o_ref[i * rows:(i + 1) * rows, cols] = jnp.where(half == 0, o[0:rows], o[rows:2 * rows]).astype(BF)

    _pipelined(jobs, score_fn, finish_fn)


def _na_lat_attention(qb, kb, vb, cache_k, cache_v, bias_tab, mask_tab):
    k0 = T_CTX // DEC_SEQ
    npair = NA_LAT_PAIRS_PER_STEP
    tok = pl.BlockSpec((DEC_SEQ, npair * LANES), lambda b, p: (k0 + b, p))
    c_spec = pl.BlockSpec((None, npair * LANES, PAST), lambda b, p: (b, p, 0))
    return pl.pallas_call(
        _na_lat_kernel,
        grid=(DEC_BATCH, NA_HEADS // 2 // npair),
        in_specs=[tok, tok, tok, c_spec, c_spec,
                  pl.BlockSpec((npair, 2, NA_BIAS_BLOCKS * GRID_W, LANES), lambda b, p: (p, 0, 0, 0)),
                  _const_spec(mask_tab.shape)],
        out_specs=pl.BlockSpec((DEC_SEQ, npair * LANES), lambda b, p: (b, p)),
        out_shape=jax.ShapeDtypeStruct((T_LAT, D), BF),
        compiler_params=_cparams(2),
        name="na_attn_lat",
    )(qb, kb, vb, cache_k, cache_v, bias_tab, mask_tab)


def _na_bias_kernel(t_ref, r_ref, n_ref, o_ref):
    t = t_ref[...]
    t1 = t.astype(BF)
    r1 = t - t1.astype(F32)
    t2 = r1.astype(BF)
    t3 = (r1 - t2.astype(F32)).astype(BF)
    r = r_ref[...]
    res = (_dot(t1, r) + _dot(t2, r) + _dot(t3, r) + n_ref[...]) * LOG2E
    for qc in range(GRID_W):
        o_ref[pl.ds(qc, t.shape[0], stride=GRID_W), :] = res[:, qc * LANES:(qc + 1) * LANES]


def _na_bias_table(rel_bias, onehot, neg):
    nrel = 2 * NA_WIN_COLS
    idx = 13 - np.arange(NA_BIAS_BLOCKS)[:, None] + np.arange(2)[None, :]
    t = jnp.pad(rel_bias[:, idx, :], ((0, 0), (0, 0), (0, 0), (0, 1)))
    t = t.reshape(NA_HEADS * NA_BIAS_BLOCKS, 2 * nrel)
    n = GRID_W * LANES
    out = pl.pallas_call(
        _na_bias_kernel,
        grid=(1,),
        in_specs=[pl.BlockSpec(t.shape, lambda j: (0, 0)),
                  pl.BlockSpec((2 * nrel, n), lambda j: (0, 0)),
                  pl.BlockSpec((1, n), lambda j: (0, 0))],
        out_specs=pl.BlockSpec((t.shape[0] * GRID_W, LANES), lambda j: (0, 0)),
        out_shape=jax.ShapeDtypeStruct((t.shape[0] * GRID_W, LANES), F32),
        compiler_params=_cparams(1),
        name="na_bias_table",
    )(t, onehot, neg)
    return out.reshape(NA_HEADS // 2, 2, NA_BIAS_BLOCKS * GRID_W, LANES)


def _na_constants():
    nrel = 2 * NA_WIN_COLS
    qc = np.arange(GRID_W)[:, None]
    kc = np.arange(GRID_W)[None, :]
    rel = np.clip(kc - qc, -(NA_WIN_COLS - 1), NA_WIN_COLS - 1) + NA_WIN_COLS - 1
    cs = np.clip(qc - NA_WIN_COLS // 2, 0, GRID_W - NA_WIN_COLS)
    col_in = (kc >= cs) & (kc < cs + NA_WIN_COLS)
    onehot = np.zeros((2, nrel, GRID_W, 2, GRID_W), np.float32)
    for hf in range(2):
        onehot[hf, rel, qc, hf, kc] = 1.0
    neg = np.where(col_in, 0.0, NEG_INF).astype(np.float32)
    neg = np.broadcast_to(neg[:, None, :], (GRID_W, 2, GRID_W)).reshape(1, -1)
    rows = 4 * GRID_W
    mask = np.full((len(NA_TILES), rows, NA_MAX_CHUNKS * LANES), NEG_INF, np.float32)
    kr = min(NA_WIN_ROWS, GRID_ROWS)
    for i, (r0, chunks) in enumerate(NA_TILES):
        qr = r0 + np.arange(rows)[:, None] // GRID_W
        rs = np.clip(qr - kr // 2, 0, GRID_ROWS - kr)
        for c, krow0 in enumerate(chunks):
            krow = krow0 + np.arange(LANES)[None, :] // GRID_W
            mask[i, :, c * LANES:(c + 1) * LANES] = np.where((krow >= rs) & (krow < rs + kr), 0.0, NEG_INF)
    return (jnp.asarray(onehot.reshape(2 * nrel, GRID_W * LANES), BF), jnp.asarray(neg), jnp.asarray(mask))


def _gq_attn_kernel(*refs, has_cache):
    if has_cache:
        q_ref, k_ref, v_ref, ck_ref, cv_ref, o_ref = refs
    else:
        q_ref, k_ref, v_ref, o_ref = refs
    group = GQ_HEADS // GQ_KV_HEADS
    qw = LANES * group
    tq = min(TQ // 2, q_ref.shape[0])
    half = _lane_half((tq, LANES))
    keep = _half_keep(half)
    nkvp = k_ref.shape[1] // LANES
    caches = []
    if has_cache:
        for kvp in range(nkvp):
            kcols = slice(kvp * LANES, (kvp + 1) * LANES)
            caches.append((ck_ref[kcols, :].astype(BF), cv_ref[kcols, :].astype(BF), True))

    def seg(job):
        kvp, t, _ = job
        kcols = slice(kvp * LANES, (kvp + 1) * LANES)
        if has_cache:
            return [(k_ref[:, kcols], v_ref[:, kcols]), caches[kvp]]
        return [(k_ref[t * tq:(t + 1) * tq, kcols], v_ref[t * tq:(t + 1) * tq, kcols])]

    jobs = [(kvp, t, kh) for t in range(q_ref.shape[0] // tq) for kvp in range(nkvp) for kh in (0, 1)]

    def blocks(job):
        kvp, t, kh = job
        for pair in (2 * kh, 2 * kh + 1):
            yield slice(t * tq, (t + 1) * tq), slice(kvp * qw + pair * LANES, kvp * qw + (pair + 1) * LANES)

    def score_fn(job):
        kh = job[2]
        parts = []
        for rows, cols in blocks(job):
            for a in (0, 1):
                qm = q_ref[rows, cols] * keep[a]
                parts.append(qm if a == kh else pltpu.roll(qm.astype(F32), HEAD_DIM, 1).astype(BF))
        return _scores(jnp.concatenate(parts, axis=0), seg(job))

    def finish_fn(job, scores):
        kh = job[2]
        o = _softmax_finish(scores, seg(job))
        for n, (rows, cols) in enumerate(blocks(job)):
            heads = [o[(2 * n + a) * tq:(2 * n + a + 1) * tq] for a in (0, 1)]
            heads = [h if a == kh else pltpu.roll(h, HEAD_DIM, 1) for a, h in enumerate(heads)]
            o_ref[rows, cols] = jnp.where(half == 0, heads[0], heads[1]).astype(BF)

    _pipelined(jobs, score_fn, finish_fn)


GQ_LAT_ROWS = 1024


def _gq_attention(qb, kb, vb, cache_k, cache_v):
    nk = GQ_KV_HEADS * HEAD_DIM
    qw = LANES * (GQ_HEADS // GQ_KV_HEADS)
    npair = GQ_KV_HEADS // 2
    o_ctx = pl.pallas_call(
        functools.partial(_gq_attn_kernel, has_cache=False),
        grid=(BATCH // CTX_BATCHES_PER_STEP,),
        in_specs=[pl.BlockSpec((CTX_ROWS, D), lambda b: (b, 0))] + [pl.BlockSpec((CTX_ROWS, nk), lambda b: (b, 0))] * 2,
        out_specs=pl.BlockSpec((CTX_ROWS, D), lambda b: (b, 0)),
        out_shape=jax.ShapeDtypeStruct((T_CTX, D), BF),
        compiler_params=_cparams(1),
        name="gq_attn_ctx",
    )(qb, kb, vb)
    qt = DEC_SEQ // GQ_LAT_ROWS
    q0, k0 = T_CTX // GQ_LAT_ROWS, T_CTX // DEC_SEQ
    kv_spec = pl.BlockSpec((DEC_SEQ, LANES), lambda b, p, t: (k0 + b, p))
    c_spec = pl.BlockSpec((None, LANES, PAST), lambda b, p, t: (b, p, 0))
    o_lat = pl.pallas_call(
        functools.partial(_gq_attn_kernel, has_cache=True),
        grid=(DEC_BATCH, npair, qt),
        in_specs=[pl.BlockSpec((GQ_LAT_ROWS, qw), lambda b, p, t: (q0 + b * qt + t, p)), kv_spec, kv_spec, c_spec,
                  c_spec],
        out_specs=pl.BlockSpec((GQ_LAT_ROWS, qw), lambda b, p, t: (b * qt + t, p)),
        out_shape=jax.ShapeDtypeStruct((T_LAT, D), BF),
        compiler_params=_cparams(3),
        name="gq_attn_lat",
    )(qb, kb, vb, cache_k, cache_v)
    return o_ctx, o_lat


def _dot_3pass(a, b):
    ah, bh = a.astype(BF), b.astype(BF)
    al, bl = (a - ah.astype(F32)).astype(BF), (b - bh.astype(F32)).astype(BF)
    return _dot(ah, bh) + _dot(ah, bl) + _dot(al, bh)


def _hy_filter_kernel(emb_ref, w1_ref, b1_ref, w2_ref, b2_ref, fr_ref, w3f_ref, w3b_ref, ldf_ref, ldb_ref,
                      c_ref, s_ref, hre_ref, him_ref, hny_ref, hid_ref, cb_ref, sb_ref):
    seq = emb_ref.shape[0]

    @pl.when((pl.program_id(0) == 0) & (pl.program_id(1) == 0))
    def _():
        hp = lax.Precision.HIGHEST
        fr = fr_ref[...]
        hid = jnp.sin(fr * (jnp.dot(emb_ref[...], w1_ref[...], precision=hp, preferred_element_type=F32)
                            + b1_ref[...]))
        hid_ref[...] = jnp.sin(fr * (jnp.dot(hid, w2_ref[...], precision=hp, preferred_element_type=F32)
                                     + b2_ref[...]))
        cb_ref[...] = c_ref[...].astype(BF)
        sb_ref[...] = s_ref[...].astype(BF)

    hid = hid_ref[...]
    t = emb_ref[:, 0:1]
    fwd = _dot_3pass(hid, w3f_ref[...]) * jnp.exp(-jnp.exp(ldf_ref[...]) * t)
    bwd = _dot_3pass(hid, w3b_ref[...]) * jnp.exp(-jnp.exp(ldb_ref[...]) * t)
    row = lax.broadcasted_iota(jnp.int32, fwd.shape, 0)
    bwd = jnp.where(row == 0, 0.0, bwd)
    even = fwd + bwd
    odd = bwd - fwd
    wk = jnp.where(row == 0, 0.5 / seq, 1.0 / seq)
    hre_ref[...] = _dot(cb_ref[...], even.astype(BF)) * wk
    him_ref[...] = _dot(sb_ref[...], odd.astype(BF)) * wk
    alt = jnp.where((row & 1) == 0, 1.0, -1.0)
    hny_ref[...] = jnp.sum(alt * even, axis=0, keepdims=True) * (0.5 / seq)


def _hy_filter(seq, emb, w1, b1, w2, b2, freq, w3, log_decay, cmat, smat):
    dc = 512
    nj = D // dc
    small = [_const_spec(a.shape) for a in (emb, w1, b1, w2, b2, freq)]
    return pl.pallas_call(
        _hy_filter_kernel,
        grid=(HY_ORDER, nj),
        in_specs=small + [pl.BlockSpec((HY_FFN, dc), lambda o, j: (0, (2 * o) * nj + j)),
                          pl.BlockSpec((HY_FFN, dc), lambda o, j: (0, (2 * o + 1) * nj + j)),
                          pl.BlockSpec((1, dc), lambda o, j: (0, (2 * o) * nj + j)),
                          pl.BlockSpec((1, dc), lambda o, j: (0, (2 * o + 1) * nj + j)),
                          _const_spec((seq, seq)), _const_spec((seq, seq))],
        out_specs=[pl.BlockSpec((None, seq, dc), lambda o, j: (o, 0, j)),
                   pl.BlockSpec((None, seq, dc), lambda o, j: (o, 0, j)),
                   pl.BlockSpec((None, 1, dc), lambda o, j: (o, 0, j))],
        out_shape=[jax.ShapeDtypeStruct((HY_ORDER, seq, D), F32), jax.ShapeDtypeStruct((HY_ORDER, seq, D), F32),
                   jax.ShapeDtypeStruct((HY_ORDER, 1, D), F32)],
        scratch_shapes=[pltpu.VMEM((seq, HY_FFN), F32), pltpu.VMEM((seq, seq), BF), pltpu.VMEM((seq, seq), BF)],
        compiler_params=_cparams(2),
        name=f"hy_filter_{seq}",
    )(emb, w1, b1, w2, b2, freq, w3, w3, log_decay, log_decay, cmat, smat)


HY_SUB = 256


def _hy_conv_kernel(u0_ref, u1_ref, u2_ref, sw0_ref, sw1_ref, sw2_ref, sb0_ref, sb1_ref, sb2_ref,
                    fb_ref, hre_ref, him_ref, hny_ref, c_ref, s_ref, o_ref, cb_ref, sb_ref):
    seq, dc = u0_ref.shape

    @pl.when((pl.program_id(0) == 0) & (pl.program_id(1) == 0))
    def _():
        cb_ref[...] = c_ref[...].astype(BF)
        sb_ref[...] = s_ref[...].astype(BF)

    row = lax.broadcasted_iota(jnp.int32, (seq, HY_SUB), 0)
    alt = jnp.where((row & 1) == 0, 1.0, -1.0)

    def sub_tile(cols):
        def short_conv(u_ref, w_ref, b_ref):
            u = u_ref[:, cols]
            prev = jnp.where(row == 0, 0.0, pltpu.roll(u, 1, 0))
            nxt = jnp.where(row == seq - 1, 0.0, pltpu.roll(u, seq - 1, 0))
            return prev * w_ref[0:1, cols] + u * w_ref[1:2, cols] + nxt * w_ref[2:3, cols] + b_ref[:, cols]

        z = short_conv(u0_ref, sw0_ref, sb0_ref)
        gates = (short_conv(u1_ref, sw1_ref, sb1_ref), short_conv(u2_ref, sw2_ref, sb2_ref))
        yield
        for o in range(HY_ORDER):
            zb = z.astype(BF)
            zc, zs = _dot(cb_ref[...], zb), _dot(sb_ref[...], zb)
            yield
            hre, him = hre_ref[o, :, cols], him_ref[o, :, cols]
            p_re = (zc * hre + zs * him).astype(BF)
            p_im = (zc * him - zs * hre).astype(BF)
            y = _dot(cb_ref[...], p_re) - _dot(sb_ref[...], p_im)
            yield
            nyq = jnp.sum(alt * z, axis=0, keepdims=True) * hny_ref[o, :, cols]
            z = gates[o] * (y + alt * nyq + z * fb_ref[o:o + 1, cols])
        o_ref[:, cols] = z.astype(BF)

    _lockstep(sub_tile(slice(j * HY_SUB, (j + 1) * HY_SUB)) for j in range(dc // HY_SUB))


def _hy_conv(u, short_w, short_b, filter_bias, hre, him, hny, cmat, smat, seq, nbatch, row0, dc):
    nj = D // dc
    r0 = row0 // seq

    def part(p):
        return pl.BlockSpec((seq, dc), lambda j, b: (r0 + b, p * nj + j))

    def vec(rows, p):
        return pl.BlockSpec((rows, dc), lambda j, b: (0, p * nj + j))

    in_specs = ([part(p) for p in range(3)] + [vec(3, p) for p in range(3)] + [vec(1, p) for p in range(3)]
                + [pl.BlockSpec((HY_ORDER, dc), lambda j, b: (0, j)),
                   pl.BlockSpec((HY_ORDER, seq, dc), lambda j, b: (0, 0, j), pipeline_mode=pl.Buffered(1)),
                   pl.BlockSpec((HY_ORDER, seq, dc), lambda j, b: (0, 0, j), pipeline_mode=pl.Buffered(1)),
                   pl.BlockSpec((HY_ORDER, 1, dc), lambda j, b: (0, 0, j)),
                   _const_spec((seq, seq)), _const_spec((seq, seq))])
    return pl.pallas_call(
        _hy_conv_kernel,
        grid=(nj, nbatch),
        in_specs=in_specs,
        out_specs=pl.BlockSpec((seq, dc), lambda j, b: (b, j)),
        out_shape=jax.ShapeDtypeStruct((nbatch * seq, D), BF),
        scratch_shapes=[pltpu.VMEM((seq, seq), BF), pltpu.VMEM((seq, seq), BF)],
        compiler_params=_cparams(2),
        name=f"hy_conv_{seq}",
    )(u, u, u, short_w, short_w, short_w, short_b, short_b, short_b, filter_bias, hre, him, hny, cmat, smat)


def _dft_tables(seq):
    k = np.arange(seq, dtype=np.int64)
    ang = np.pi * ((k[:, None] * k[None, :]) % (2 * seq)) / seq
    return jnp.asarray(np.cos(ang), F32), jnp.asarray(np.sin(ang), F32)


def _hy_embedding(seq):
    t = np.arange(seq, dtype=np.float32) / np.float32(seq)
    ang = (2.0 * math.pi) * t[:, None] * np.arange(1, HY_BANDS + 1, dtype=np.float32)
    emb = np.concatenate([t[:, None], np.cos(ang), np.sin(ang)], axis=-1).astype(np.float32)
    return jnp.asarray(np.pad(emb, ((0, 0), (0, HY_EMB_PAD - HY_EMB))))


def _post_kernel(*refs, split_x, split_out, tm):
    oc_ref, ol_ref = refs[0:2]
    x_refs, refs = (refs[2:4], refs[4:]) if split_x else (refs[2:3], refs[3:])
    mod_ref, wo_ref, g1_ref, b1_ref, w1c_ref, w2c_ref, g2_ref, b2_ref = refs[0:8]
    outs, (w1_ref, w2_ref, h_ref, acc_ref) = refs[8:-4], refs[-4:]
    x1_ref = outs[0]
    step = pl.program_id(0)
    is_lat = _is_lat(tm, N_FF_CHUNKS - 1)
    nsub = tm // SUB_POST
    per = MLP_CHUNK // FF_CHUNK

    def rows(j):
        return slice(j * SUB_POST, (j + 1) * SUB_POST)

    def pick(c_ref, l_ref, j):
        return jnp.where(is_lat, l_ref[rows(j), :], c_ref[rows(j), :])

    def norm1(j):
        a = _dot(pick(oc_ref, ol_ref, j), wo_ref[...])
        x = pick(x_refs[0], x_refs[1], j) if split_x else x_refs[0][rows(j), :]
        x1 = _layer_norm(DN_ALPHA * x + mod_ref[2:3, :] * a, g1_ref[...], b1_ref[...])
        return x1, _modulate(x1, mod_ref, 3, 4)

    def norm2(x1, acc):
        return _layer_norm(DN_ALPHA * x1 + mod_ref[5:6, :] * acc, g2_ref[...], b2_ref[...])

    def mlp_chunk(h, c):
        a = jnp.concatenate([_dot(h, w1_ref[per * c + i]) for i in range(per)], axis=1)
        a = jnp.maximum(a, 0.0)
        return _dot((a * a).astype(BF), w2_ref[c])

    def write_branched(ys):
        yc_ref, yl_ref = outs

        @pl.when(jnp.logical_not(is_lat))
        def _():
            for j, y in enumerate(ys):
                yc_ref[rows(j), :] = y

        @pl.when(is_lat)
        def _():
            for j, y in enumerate(ys):
                yl_ref[rows(j), :] = y

    @pl.when(step < N_FF_CHUNKS)
    def _():
        w2_rows = pl.ds(pl.multiple_of((step % per) * FF_CHUNK, FF_CHUNK), FF_CHUNK)
        w1_ref[step] = w1c_ref[...].astype(BF)
        w2_ref[step // per, w2_rows, :] = w2c_ref[...].astype(BF)

        @pl.when(step == 0)
        def _():
            for j in range(nsub):
                x1_ref[rows(j), :], h_ref[rows(j), :] = norm1(j)
            acc_ref[...] = jnp.zeros_like(acc_ref)

        a = jnp.maximum(_dot(h_ref[...], w1_ref[step]), 0.0)
        acc_ref[...] += _dot((a * a).astype(BF), w2_ref[step // per, w2_rows, :])

        @pl.when(step == N_FF_CHUNKS - 1)
        def _():
            ys = [norm2(x1_ref[rows(j), :], acc_ref[rows(j), :]) for j in range(nsub)]
            if split_out:
                write_branched(ys)
            else:
                for j, y in enumerate(ys):
                    outs[0][rows(j), :] = y

    def token_tile():
        ys = []
        cur = norm1(0)
        prev = None
        for j in range(nsub):
            x1, h = cur
            acc = mlp_chunk(h, 0)
            if j + 1 < nsub:
                cur = norm1(j + 1)
            if prev is not None:
                ys.append(norm2(*prev))
                if not split_out:
                    outs[0][rows(j - 1), :] = ys[-1]
            for c in range(1, D_FF // MLP_CHUNK):
                acc = acc + mlp_chunk(h, c)
            prev = (x1, acc)
        ys.append(norm2(*prev))
        if split_out:
            write_branched(ys)
        else:
            outs[0][rows(nsub - 1), :] = ys[-1]

    pl.when(step >= N_FF_CHUNKS)(token_tile)


def _post(o_ctx, o_lat, xs, mods, layer, w_o, g1, b1, w1, w2, g2, b2, split_out):
    tm, off = TM_POST, N_FF_CHUNKS - 1
    split_x = len(xs) == 2
    x_specs = [_ctx_spec(D, tm, off), _lat_spec(D, tm, off)] if split_x else [_tok_spec(D, tm, off)]
    vec = _const_spec((1, D))
    if split_out:
        out_specs = [_ctx_spec(D, tm, off), _lat_spec(D, tm, off)]
        out_shape = [jax.ShapeDtypeStruct((T_CTX, D), F32), jax.ShapeDtypeStruct((T_LAT, D), F32)]
    else:
        out_specs = _tok_spec(D, tm, off)
        out_shape = jax.ShapeDtypeStruct((T, D), F32)

    def chunk(i):
        return jnp.minimum(i, N_FF_CHUNKS - 1)

    return pl.pallas_call(
        functools.partial(_post_kernel, split_x=split_x, split_out=split_out, tm=tm),
        grid=(off + T // tm,),
        in_specs=[_ctx_spec(D, tm, off), _lat_spec(D, tm, off)] + x_specs + [
            _mod_spec(layer, tm, off), _const_spec((D, D)), vec, vec,
            pl.BlockSpec((None, D, FF_CHUNK), lambda i: (layer, 0, chunk(i))),
            pl.BlockSpec((None, FF_CHUNK, D), lambda i: (layer, chunk(i), 0)), vec, vec],
        out_specs=out_specs,
        out_shape=out_shape,
        scratch_shapes=[pltpu.VMEM((N_FF_CHUNKS, D, FF_CHUNK), BF), pltpu.VMEM((D_FF // MLP_CHUNK, MLP_CHUNK, D), BF),
                        pltpu.VMEM((tm, D), BF), pltpu.VMEM((tm, D), F32)],
        compiler_params=_cparams(1, POST_VMEM_LIMIT),
        name=f"post_l{layer}",
    )(o_ctx, o_lat, *xs, mods, w_o, g1, b1, w1, w2, g2, b2)


def _rope_tables():
    n = HEAD_DIM // 4
    pos = np.arange(DEC_SEQ)
    inv = (np.float32(ROPE_BASE) ** (-np.arange(n, dtype=np.float32) / np.float32(n))).astype(np.float32)
    ang_r = ((pos // GRID_W).astype(np.float32)[:, None] * inv).astype(np.float32)
    ang_c = ((pos % GRID_W).astype(np.float32)[:, None] * inv).astype(np.float32)
    cr, sr, cc, sc = np.cos(ang_r), np.sin(ang_r), np.cos(ang_c), np.sin(ang_c)
    a = np.tile(np.concatenate([cr, cr, cc, cc], axis=-1), (1, D // HEAD_DIM))
    b = np.tile(np.concatenate([-sr, sr, -sc, sc], axis=-1), (1, D // HEAD_DIM))
    a = np.concatenate([a, np.ones((TM, D), np.float32)], axis=0)
    b = np.concatenate([b, np.zeros((TM, D), np.float32)], axis=0)
    return jnp.asarray(a, F32), jnp.asarray(b, F32)


def kernel(x_prompt, x_sample, c, cache_da_k, cache_da_v, cache_na_k, cache_na_v, cache_gq_k, cache_gq_v, c_ctx, ada_w, ada_b, ln_g, ln_b, mlp_w1, mlp_w2, da_w_qkv, da_w_o, da_lambda, da_subln_g, na_w_qkv, na_w_o, na_rel_bias, gq_w_qkv, gq_w_o, gq_q_norm, gq_k_norm, hy_w_in, hy_short_w, hy_short_b, hy_ffn_w1, hy_ffn_b1, hy_ffn_w2, hy_ffn_b2, hy_ffn_freq, hy_ffn_w3, hy_log_decay, hy_filter_bias, hy_w_o):
    cvec = jnp.concatenate([c_ctx[None, :], c, jnp.zeros((MOD_ROWS - 1 - DEC_BATCH, D), F32)], axis=0)
    mods = _mods(cvec, ada_w, ada_b)
    rope_a, rope_b = _rope_tables()

    def finish(o_ctx, o_lat, xs, layer, w_o, split_out=False):
        return _post(o_ctx, o_lat, xs, mods, layer, w_o.astype(BF), ln_g[layer, 0][None], ln_b[layer, 0][None],
                     mlp_w1, mlp_w2, ln_g[layer, 1][None], ln_b[layer, 1][None], split_out)

    xs = (x_prompt.reshape(T_CTX, D), x_sample.reshape(T_LAT, D))
    qb, kb, vb, ks, vs = _da_proj(*xs, mods, 0, da_w_qkv[0].astype(BF), rope_a, rope_b)
    state_da_k = ks.reshape(BATCH, 1, SEQ, DA_HEADS, 2 * HEAD_DIM)
    state_da_v = vs.reshape(BATCH, 1, SEQ, DA_HEADS, 2 * HEAD_DIM)
    o_ctx, o_lat = _da_attention(qb, kb, vb, cache_da_k, cache_da_v, da_lambda[0], da_subln_g[0][None], 0)
    x = finish(o_ctx, o_lat, xs, 0, da_w_o[0])

    qb, kb, vb, ks, vs = _na_proj(x, mods, 1, na_w_qkv[0].astype(BF))
    state_na_k, state_na_v = _untranspose_state(ks, NA_HEADS), _untranspose_state(vs, NA_HEADS)
    onehot, neg, mask = _na_constants()
    bias_tab = _na_bias_table(na_rel_bias[0], onehot, neg)
    o_ctx = _na_ctx_attention(qb, kb, vb)
    o_lat = _na_lat_attention(qb, kb, vb, _features_major(cache_na_k), _features_major(cache_na_v), bias_tab, mask)
    x = finish(o_ctx, o_lat, (x,), 1, na_w_o[0])

    g_mat = jnp.asarray(np.kron(np.eye(GN_BLOCK // HEAD_DIM), np.full((HEAD_DIM, HEAD_DIM), 1.0 / HEAD_DIM)), BF)
    qb, kb, vb, ks, vs = _gq_proj(x, mods, 2, gq_w_qkv[0].astype(BF), g_mat,
                                  jnp.tile(gq_q_norm[0], GQ_HEADS)[None], jnp.tile(gq_k_norm[0], GQ_KV_HEADS)[None],
                                  rope_a, rope_b)
    state_gq_k, state_gq_v = _untranspose_state(ks, GQ_KV_HEADS), _untranspose_state(vs, GQ_KV_HEADS)
    o_ctx, o_lat = _gq_attention(qb, kb, vb, _features_major(cache_gq_k), _features_major(cache_gq_v))
    x = finish(o_ctx, o_lat, (x,), 2, gq_w_o[0])

    u = _hy_proj(x, mods, 3, hy_w_in[0].astype(BF))
    w1 = jnp.pad(hy_ffn_w1[0], ((0, HY_EMB_PAD - HY_EMB), (0, 0)))
    zs = []
    for seq, nbatch, row0, dc in ((SEQ, BATCH, 0, D), (DEC_SEQ, DEC_BATCH, T_CTX, 512)):
        cmat, smat = _dft_tables(seq)
        hre, him, hny = _hy_filter(seq, _hy_embedding(seq), w1, hy_ffn_b1[0][None], hy_ffn_w2[0], hy_ffn_b2[0][None],
                                   hy_ffn_freq[0][None], hy_ffn_w3[0], hy_log_decay[0][None], cmat, smat)
        zs.append(_hy_conv(u, hy_short_w[0], hy_short_b[0][None], hy_filter_bias[0], hre, him, hny, cmat, smat,
                           seq, nbatch, row0, dc))
    y_ctx, y_lat = finish(zs[0], zs[1], (x,), 3, hy_w_o[0], split_out=True)

    return (y_ctx.reshape(BATCH, SEQ, D), y_lat.reshape(DEC_BATCH, DEC_SEQ, D),
            state_da_k, state_da_v, state_na_k, state_na_v, state_gq_k, state_gq_v)
```
